```python
import jax, jax.numpy as jnp
from jax import lax
import numpy as np

D_MODEL = 1024
BATCH = 8
SEQ = 8192
DEPTH = 2

N_A = DEPTH // 2
N_B = DEPTH - N_A
D_FF = 2816
CONV_WIDTH = 31
N_HEADS = 16
Q_LORA = 512
KV_LORA = 256
QK_NOPE = 64
QK_ROPE = 32
V_HEAD = 64
ROPE_THETA = 10000.0
Q_BLOCK = 128
EPS = 1e-6
N_MOD = 9

kernel_name = "yoco_conformer_mla_macaron_adaln"


def rmsnorm(x, g):
    xf = x.astype(jnp.float32)
    y = xf * lax.rsqrt(jnp.mean(xf * xf, axis=-1, keepdims=True) + EPS)
    return (y * g.astype(jnp.float32)).astype(x.dtype)


def modulate(xn, shift, scale):
    return xn * (1 + scale[:, None, :]) + shift[:, None, :]


def swiglu(x, w13, w2):
    a, b = jnp.split(x @ w13, 2, axis=-1)
    return (jax.nn.silu(a) * b) @ w2


def conv_module(u, w_pw1, b_pw1, w_dw, b_dw, ln_g, ln_b, w_pw2, b_pw2):
    a, g = jnp.split(u @ w_pw1 + b_pw1, 2, axis=-1)
    u = a * jax.nn.sigmoid(g)
    u = lax.conv_general_dilated(
        u, w_dw[:, None, :], window_strides=(1,),
        padding=[(CONV_WIDTH - 1, 0)],
        dimension_numbers=("NWC", "WIO", "NWC"),
        feature_group_count=D_MODEL) + b_dw
    uf = u.astype(jnp.float32)
    mu = jnp.mean(uf, axis=-1, keepdims=True)
    var = jnp.mean(jnp.square(uf - mu), axis=-1, keepdims=True)
    un = ((uf - mu) * lax.rsqrt(var + EPS) * ln_g.astype(jnp.float32)
          + ln_b.astype(jnp.float32)).astype(u.dtype)
    return jax.nn.silu(un) @ w_pw2 + b_pw2


def rope(x, cos, sin):
    x1, x2 = jnp.split(x, 2, axis=-1)
    out = jnp.concatenate([x1 * cos - x2 * sin, x2 * cos + x1 * sin], axis=-1)
    return out.astype(x.dtype)


def mla_shared_kv(h, shift, scale, kv_norm_g, w_kv_a, kv_a_norm_g, w_kv_b, cos, sin):
    b, s, _ = h.shape
    hn = modulate(rmsnorm(h, kv_norm_g), shift, scale)
    c_kv, k_pe = jnp.split(hn @ w_kv_a, [KV_LORA], axis=-1)
    c_kv = rmsnorm(c_kv, kv_a_norm_g)
    kvb = (c_kv @ w_kv_b).reshape(b, s, N_HEADS, QK_NOPE + V_HEAD)
    k_nope, v = jnp.split(kvb, [QK_NOPE], axis=-1)
    k_pe = rope(k_pe, cos[:, :, 0, :], sin[:, :, 0, :])
    return k_nope, k_pe, v


def mla_attention(hn, w_q_a, q_a_norm_g, w_q_b, w_o, k_nope, k_pe, v, cos, sin):
    b, s, _ = hn.shape
    q = (rmsnorm(hn @ w_q_a, q_a_norm_g) @ w_q_b).reshape(b, s, N_HEADS, QK_NOPE + QK_ROPE)
    q_nope, q_pe = jnp.split(q, [QK_NOPE], axis=-1)
    q_pe = rope(q_pe, cos, sin)
    n_blk = s // Q_BLOCK
    qn = q_nope.reshape(b, n_blk, Q_BLOCK, N_HEADS, QK_NOPE).transpose(1, 0, 2, 3, 4)
    qp = q_pe.reshape(b, n_blk, Q_BLOCK, N_HEADS, QK_ROPE).transpose(1, 0, 2, 3, 4)
    k_idx = jnp.arange(s)
    sm_scale = (QK_NOPE + QK_ROPE) ** -0.5
    neg = jnp.finfo(jnp.float32).min

    def block(args):
        qn_b, qp_b, i = args
        sc = (jnp.einsum("bqhd,bkhd->bhqk", qn_b, k_nope)
              + jnp.einsum("bqhr,bkr->bhqk", qp_b, k_pe)).astype(jnp.float32) * sm_scale
        q_idx = i * Q_BLOCK + jnp.arange(Q_BLOCK)
        sc = jnp.where(k_idx[None, :] <= q_idx[:, None], sc, neg)
        p = jax.nn.softmax(sc, axis=-1).astype(v.dtype)
        return jnp.einsum("bhqk,bkhd->bqhd", p, v)

    o = lax.map(block, (qn, qp, jnp.arange(n_blk, dtype=jnp.int32)))
    o = o.transpose(1, 0, 2, 3, 4).reshape(b, s, N_HEADS * V_HEAD)
    return o @ w_o


def _fwd_setup_inputs(seed: int = 0) -> dict:
    key = jax.random.key(seed)
    ks = jax.random.split(key, 32)
    f32 = jnp.float32

    def nrm(k, shape, fan_in, mult=1.0):
        return jax.random.normal(k, shape, f32) * (mult * fan_in ** -0.5)

    def gain(k, shape):
        return 1.0 + 0.02 * jax.random.normal(k, shape, f32)

    def bias(k, shape):
        return 0.01 * jax.random.normal(k, shape, f32)

    D = D_MODEL
    return {
        "x": jax.random.normal(ks[0], (BATCH, SEQ, D), f32),
        "c": jax.random.normal(ks[1], (BATCH, D), f32),
        "positions": (jnp.arange(SEQ, dtype=jnp.int32)[None, :]
                      + jax.random.randint(ks[2], (BATCH, 1), 0, 1024, dtype=jnp.int32)),
        "ada_w": nrm(ks[3], (DEPTH, D, N_MOD * D), D, 0.5),
        "ada_b": bias(ks[4], (DEPTH, N_MOD * D)),
        "norm_g": gain(ks[5], (DEPTH, 3, D)),
        "ffn_w13": nrm(ks[6], (DEPTH, 2, D, 2 * D_FF), D),
        "ffn_w2": nrm(ks[7], (DEPTH, 2, D_FF, D), D_FF),
        "conv_w_pw1": nrm(ks[8], (N_A, D, 2 * D), D),
        "conv_b_pw1": bias(ks[9], (N_A, 2 * D)),
        "conv_w_dw": nrm(ks[10], (N_A, CONV_WIDTH, D), CONV_WIDTH),
        "conv_b_dw": bias(ks[11], (N_A, D)),
        "conv_ln_g": gain(ks[12], (N_A, D)),
        "conv_ln_b": bias(ks[13], (N_A, D)),
        "conv_w_pw2": nrm(ks[14], (N_A, D, D), D),
        "conv_b_pw2": bias(ks[15], (N_A, D)),
        "kv_ada_w": nrm(ks[16], (D, 2 * D), D, 0.5),
        "kv_ada_b": bias(ks[17], (2 * D,)),
        "kv_norm_g": gain(ks[18], (D,)),
        "w_kv_a": nrm(ks[19], (D, KV_LORA + QK_ROPE), D),
        "kv_a_norm_g": gain(ks[20], (KV_LORA,)),
        "w_kv_b": nrm(ks[21], (KV_LORA, N_HEADS * (QK_NOPE + V_HEAD)), KV_LORA),
        "w_q_a": nrm(ks[22], (N_B, D, Q_LORA), D),
        "q_a_norm_g": gain(ks[23], (N_B, Q_LORA)),
        "w_q_b": nrm(ks[24], (N_B, Q_LORA, N_HEADS * (QK_NOPE + QK_ROPE)), Q_LORA),
        "w_o": nrm(ks[25], (N_B, N_HEADS * V_HEAD, D), N_HEADS * V_HEAD),
        "final_norm_g": gain(ks[26], (D,)),
    }


def _fwd_reference(x, c, positions, ada_w, ada_b, norm_g, ffn_w13, ffn_w2,
              conv_w_pw1, conv_b_pw1, conv_w_dw, conv_b_dw, conv_ln_g, conv_ln_b,
              conv_w_pw2, conv_b_pw2, kv_ada_w, kv_ada_b, kv_norm_g, w_kv_a,
              kv_a_norm_g, w_kv_b, w_q_a, q_a_norm_g, w_q_b, w_o, final_norm_g):
    silu_c = jax.nn.silu(c)
    inv_freq = ROPE_THETA ** (-jnp.arange(0, QK_ROPE, 2, dtype=jnp.float32) / QK_ROPE)
    ang = positions.astype(jnp.float32)[..., None] * inv_freq
    cos = jnp.cos(ang)[:, :, None, :]
    sin = jnp.sin(ang)[:, :, None, :]

    h = x
    k_nope = k_pe = v = None
    for l in range(DEPTH):
        if l == N_A:
            kv_shift, kv_scale = jnp.split(silu_c @ kv_ada_w + kv_ada_b, 2, axis=-1)
            k_nope, k_pe, v = mla_shared_kv(h, kv_shift, kv_scale, kv_norm_g, w_kv_a,
                                            kv_a_norm_g, w_kv_b, cos, sin)
        (sh1, sc1, g1, shm, scm, gm, sh2, sc2, g2) = jnp.split(
            silu_c @ ada_w[l] + ada_b[l], N_MOD, axis=-1)
        hn = modulate(rmsnorm(h, norm_g[l, 0]), sh1, sc1)
        h = h + 0.5 * g1[:, None, :] * swiglu(hn, ffn_w13[l, 0], ffn_w2[l, 0])
        hn = modulate(rmsnorm(h, norm_g[l, 1]), shm, scm)
        if l < N_A:
            y = conv_module(hn, conv_w_pw1[l], conv_b_pw1[l], conv_w_dw[l], conv_b_dw[l],
                            conv_ln_g[l], conv_ln_b[l], conv_w_pw2[l], conv_b_pw2[l])
        else:
            j = l - N_A
            y = mla_attention(hn, w_q_a[j], q_a_norm_g[j], w_q_b[j], w_o[j],
                              k_nope, k_pe, v, cos, sin)
        h = h + gm[:, None, :] * y
        hn = modulate(rmsnorm(h, norm_g[l, 2]), sh2, sc2)
        h = h + 0.5 * g2[:, None, :] * swiglu(hn, ffn_w13[l, 1], ffn_w2[l, 1])
    return rmsnorm(h, final_norm_g)


import jax as _jax
import jax.numpy as _jnp

TWIN_FORMAT = 'train_step'
FWD_PARAMS = ['x', 'c', 'positions', 'ada_w', 'ada_b', 'norm_g', 'ffn_w13', 'ffn_w2', 'conv_w_pw1', 'conv_b_pw1', 'conv_w_dw', 'conv_b_dw', 'conv_ln_g', 'conv_ln_b', 'conv_w_pw2', 'conv_b_pw2', 'kv_ada_w', 'kv_ada_b', 'kv_norm_g', 'w_kv_a', 'kv_a_norm_g', 'w_kv_b', 'w_q_a', 'q_a_norm_g', 'w_q_b', 'w_o', 'final_norm_g']
TWIN_WEIGHTS = ['ada_w', 'ada_b', 'norm_g', 'ffn_w13', 'ffn_w2', 'conv_w_pw1', 'conv_b_pw1', 'conv_w_dw', 'conv_b_dw', 'conv_ln_g', 'conv_ln_b', 'conv_w_pw2', 'conv_b_pw2', 'kv_ada_w', 'kv_ada_b', 'kv_norm_g', 'w_kv_a', 'kv_a_norm_g', 'w_kv_b', 'w_q_a', 'q_a_norm_g', 'w_q_b', 'w_o', 'final_norm_g']
TWIN_DIFF_INPUT = 'x'
TWIN_INPUTS = ['x', 'c', 'positions', 'ada_w', 'ada_b', 'norm_g', 'ffn_w13', 'ffn_w2', 'conv_w_pw1', 'conv_b_pw1', 'conv_w_dw', 'conv_b_dw', 'conv_ln_g', 'conv_ln_b', 'conv_w_pw2', 'conv_b_pw2', 'kv_ada_w', 'kv_ada_b', 'kv_norm_g', 'w_kv_a', 'kv_a_norm_g', 'w_kv_b', 'w_q_a', 'q_a_norm_g', 'w_q_b', 'w_o', 'final_norm_g', 'loss_target', 'm_ada_w', 'm_ada_b', 'm_norm_g', 'm_ffn_w13', 'm_ffn_w2', 'm_conv_w_pw1', 'm_conv_b_pw1', 'm_conv_w_dw', 'm_conv_b_dw', 'm_conv_ln_g', 'm_conv_ln_b', 'm_conv_w_pw2', 'm_conv_b_pw2', 'm_kv_ada_w', 'm_kv_ada_b', 'm_kv_norm_g', 'm_w_kv_a', 'm_kv_a_norm_g', 'm_w_kv_b', 'm_w_q_a', 'm_q_a_norm_g', 'm_w_q_b', 'm_w_o', 'm_final_norm_g', 'v_ada_w', 'v_ada_b', 'v_norm_g', 'v_ffn_w13', 'v_ffn_w2', 'v_conv_w_pw1', 'v_conv_b_pw1', 'v_conv_w_dw', 'v_conv_b_dw', 'v_conv_ln_g', 'v_conv_ln_b', 'v_conv_w_pw2', 'v_conv_b_pw2', 'v_kv_ada_w', 'v_kv_ada_b', 'v_kv_norm_g', 'v_w_kv_a', 'v_kv_a_norm_g', 'v_w_kv_b', 'v_w_q_a', 'v_q_a_norm_g', 'v_w_q_b', 'v_w_o', 'v_final_norm_g']
TWIN_OUTPUTS = ['loss', 'grad_x', 'grad_ada_w', 'grad_ada_b', 'grad_norm_g', 'grad_ffn_w13', 'grad_ffn_w2', 'grad_conv_w_pw1', 'grad_conv_b_pw1', 'grad_conv_w_dw', 'grad_conv_b_dw', 'grad_conv_ln_g', 'grad_conv_ln_b', 'grad_conv_w_pw2', 'grad_conv_b_pw2', 'grad_kv_ada_w', 'grad_kv_ada_b', 'grad_kv_norm_g', 'grad_w_kv_a', 'grad_kv_a_norm_g', 'grad_w_kv_b', 'grad_w_q_a', 'grad_q_a_norm_g', 'grad_w_q_b', 'grad_w_o', 'grad_final_norm_g', 'delta_ada_w', 'delta_ada_b', 'delta_norm_g', 'delta_ffn_w13', 'delta_ffn_w2', 'delta_conv_w_pw1', 'delta_conv_b_pw1', 'delta_conv_w_dw', 'delta_conv_b_dw', 'delta_conv_ln_g', 'delta_conv_ln_b', 'delta_conv_w_pw2', 'delta_conv_b_pw2', 'delta_kv_ada_w', 'delta_kv_ada_b', 'delta_kv_norm_g', 'delta_w_kv_a', 'delta_kv_a_norm_g', 'delta_w_kv_b', 'delta_w_q_a', 'delta_q_a_norm_g', 'delta_w_q_b', 'delta_w_o', 'delta_final_norm_g', 'new_m_ada_w', 'new_m_ada_b', 'new_m_norm_g', 'new_m_ffn_w13', 'new_m_ffn_w2', 'new_m_conv_w_pw1', 'new_m_conv_b_pw1', 'new_m_conv_w_dw', 'new_m_conv_b_dw', 'new_m_conv_ln_g', 'new_m_conv_ln_b', 'new_m_conv_w_pw2', 'new_m_conv_b_pw2', 'new_m_kv_ada_w', 'new_m_kv_ada_b', 'new_m_kv_norm_g', 'new_m_w_kv_a', 'new_m_kv_a_norm_g', 'new_m_w_kv_b', 'new_m_w_q_a', 'new_m_q_a_norm_g', 'new_m_w_q_b', 'new_m_w_o', 'new_m_final_norm_g', 'new_v_ada_w', 'new_v_ada_b', 'new_v_norm_g', 'new_v_ffn_w13', 'new_v_ffn_w2', 'new_v_conv_w_pw1', 'new_v_conv_b_pw1', 'new_v_conv_w_dw', 'new_v_conv_b_dw', 'new_v_conv_ln_g', 'new_v_conv_ln_b', 'new_v_conv_w_pw2', 'new_v_conv_b_pw2', 'new_v_kv_ada_w', 'new_v_kv_ada_b', 'new_v_kv_norm_g', 'new_v_w_kv_a', 'new_v_kv_a_norm_g', 'new_v_w_kv_b', 'new_v_w_q_a', 'new_v_q_a_norm_g', 'new_v_w_q_b', 'new_v_w_o', 'new_v_final_norm_g']
TWIN_LEAF_KINDS = {'loss': 'loss', 'grad_x': 'grad_x', 'grad_ada_w': 'grad_w', 'grad_ada_b': 'grad_w', 'grad_norm_g': 'grad_w', 'grad_ffn_w13': 'grad_w', 'grad_ffn_w2': 'grad_w', 'grad_conv_w_pw1': 'grad_w', 'grad_conv_b_pw1': 'grad_w', 'grad_conv_w_dw': 'grad_w', 'grad_conv_b_dw': 'grad_w', 'grad_conv_ln_g': 'grad_w', 'grad_conv_ln_b': 'grad_w', 'grad_conv_w_pw2': 'grad_w', 'grad_conv_b_pw2': 'grad_w', 'grad_kv_ada_w': 'grad_w', 'grad_kv_ada_b': 'grad_w', 'grad_kv_norm_g': 'grad_w', 'grad_w_kv_a': 'grad_w', 'grad_kv_a_norm_g': 'grad_w', 'grad_w_kv_b': 'grad_w', 'grad_w_q_a': 'grad_w', 'grad_q_a_norm_g': 'grad_w', 'grad_w_q_b': 'grad_w', 'grad_w_o': 'grad_w', 'grad_final_norm_g': 'grad_w', 'delta_ada_w': 'delta_w', 'delta_ada_b': 'delta_w', 'delta_norm_g': 'delta_w', 'delta_ffn_w13': 'delta_w', 'delta_ffn_w2': 'delta_w', 'delta_conv_w_pw1': 'delta_w', 'delta_conv_b_pw1': 'delta_w', 'delta_conv_w_dw': 'delta_w', 'delta_conv_b_dw': 'delta_w', 'delta_conv_ln_g': 'delta_w', 'delta_conv_ln_b': 'delta_w', 'delta_conv_w_pw2': 'delta_w', 'delta_conv_b_pw2': 'delta_w', 'delta_kv_ada_w': 'delta_w', 'delta_kv_ada_b': 'delta_w', 'delta_kv_norm_g': 'delta_w', 'delta_w_kv_a': 'delta_w', 'delta_kv_a_norm_g': 'delta_w', 'delta_w_kv_b': 'delta_w', 'delta_w_q_a': 'delta_w', 'delta_q_a_norm_g': 'delta_w', 'delta_w_q_b': 'delta_w', 'delta_w_o': 'delta_w', 'delta_final_norm_g': 'delta_w', 'new_m_ada_w': 'new_m', 'new_m_ada_b': 'new_m', 'new_m_norm_g': 'new_m', 'new_m_ffn_w13': 'new_m', 'new_m_ffn_w2': 'new_m', 'new_m_conv_w_pw1': 'new_m', 'new_m_conv_b_pw1': 'new_m', 'new_m_conv_w_dw': 'new_m', 'new_m_conv_b_dw': 'new_m', 'new_m_conv_ln_g': 'new_m', 'new_m_conv_ln_b': 'new_m', 'new_m_conv_w_pw2': 'new_m', 'new_m_conv_b_pw2': 'new_m', 'new_m_kv_ada_w': 'new_m', 'new_m_kv_ada_b': 'new_m', 'new_m_kv_norm_g': 'new_m', 'new_m_w_kv_a': 'new_m', 'new_m_kv_a_norm_g': 'new_m', 'new_m_w_kv_b': 'new_m', 'new_m_w_q_a': 'new_m', 'new_m_q_a_norm_g': 'new_m', 'new_m_w_q_b': 'new_m', 'new_m_w_o': 'new_m', 'new_m_final_norm_g': 'new_m', 'new_v_ada_w': 'new_v', 'new_v_ada_b': 'new_v', 'new_v_norm_g': 'new_v', 'new_v_ffn_w13': 'new_v', 'new_v_ffn_w2': 'new_v', 'new_v_conv_w_pw1': 'new_v', 'new_v_conv_b_pw1': 'new_v', 'new_v_conv_w_dw': 'new_v', 'new_v_conv_b_dw': 'new_v', 'new_v_conv_ln_g': 'new_v', 'new_v_conv_ln_b': 'new_v', 'new_v_conv_w_pw2': 'new_v', 'new_v_conv_b_pw2': 'new_v', 'new_v_kv_ada_w': 'new_v', 'new_v_kv_ada_b': 'new_v', 'new_v_kv_norm_g': 'new_v', 'new_v_w_kv_a': 'new_v', 'new_v_kv_a_norm_g': 'new_v', 'new_v_w_kv_b': 'new_v', 'new_v_w_q_a': 'new_v', 'new_v_q_a_norm_g': 'new_v', 'new_v_w_q_b': 'new_v', 'new_v_w_o': 'new_v', 'new_v_final_norm_g': 'new_v'}


def _forward(args):
    return _fwd_reference(*[args[k] for k in FWD_PARAMS])


def _output_shape():
    def fwd():
        inp = _fwd_setup_inputs(0)
        return _fwd_reference(*[inp[k] for k in FWD_PARAMS])
    out = _jax.eval_shape(fwd)
    return out.shape, out.dtype

N_MICROBATCH = 1
ADAM_LR = 0.001
ADAM_B1 = 0.9
ADAM_B2 = 0.999
ADAM_EPS = 1e-08
ADAM_WD = 0.01
ADAM_STEP = 10
PER_EXAMPLE_BATCH_AXIS = {'x': 0, 'c': 0, 'positions': 0, 'loss_target': 0}
SHARED_INPUTS = []
_WEIGHT_DTYPES = {'ada_w': _jnp.float32, 'ada_b': _jnp.float32, 'norm_g': _jnp.float32, 'ffn_w13': _jnp.float32, 'ffn_w2': _jnp.float32, 'conv_w_pw1': _jnp.float32, 'conv_b_pw1': _jnp.float32, 'conv_w_dw': _jnp.float32, 'conv_b_dw': _jnp.float32, 'conv_ln_g': _jnp.float32, 'conv_ln_b': _jnp.float32, 'conv_w_pw2': _jnp.float32, 'conv_b_pw2': _jnp.float32, 'kv_ada_w': _jnp.float32, 'kv_ada_b': _jnp.float32, 'kv_norm_g': _jnp.float32, 'w_kv_a': _jnp.float32, 'kv_a_norm_g': _jnp.float32, 'w_kv_b': _jnp.float32, 'w_q_a': _jnp.float32, 'q_a_norm_g': _jnp.float32, 'w_q_b': _jnp.float32, 'w_o': _jnp.float32, 'final_norm_g': _jnp.float32}
MOMENT_SCALE = {'ada_w': 3.923966e-02, 'ada_b': 6.584723e-02, 'norm_g': 3.630068e-02, 'ffn_w13': 1.608545e-02, 'ffn_w2': 2.619522e-02, 'conv_w_pw1': 3.589347e-02, 'conv_b_pw1': 4.068364e-02, 'conv_w_dw': 4.706816e-02, 'conv_b_dw': 9.108474e-02, 'conv_ln_g': 5.755820e-02, 'conv_ln_b': 5.273910e-02, 'conv_w_pw2': 4.582465e-02, 'conv_b_pw2': 8.671941e-02, 'kv_ada_w': 3.109222e-02, 'kv_ada_b': 5.512784e-02, 'kv_norm_g': 2.044779e-02, 'w_kv_a': 5.564096e-02, 'kv_a_norm_g': 5.622708e-02, 'w_kv_b': 2.006038e-02, 'w_q_a': 1.804380e-02, 'q_a_norm_g': 1.751314e-02, 'w_q_b': 1.041786e-02, 'w_o': 2.606820e-02, 'final_norm_g': 6.398530e+01}


def _to_microbatches(a, axis):
    t = _jnp.moveaxis(a, axis, 0)
    t = t.reshape((N_MICROBATCH, t.shape[0] // N_MICROBATCH) + t.shape[1:])
    return _jnp.moveaxis(t, 1, axis + 1)


def setup_inputs(seed: int = 0) -> dict:
    inp = _fwd_setup_inputs(seed)
    key = _jax.random.fold_in(_jax.random.key(seed), 7919)
    shape, _ = _output_shape()
    out = dict(inp)
    out["loss_target"] = _jax.random.normal(_jax.random.fold_in(key, 0), shape, _jnp.float32)
    for i, name in enumerate(TWIN_WEIGHTS):
        w = inp[name].astype(_jnp.float32)
        if MOMENT_SCALE is None:
            s = _jnp.sqrt(_jnp.mean(_jnp.square(w)) + 1e-30)
        else:
            s = MOMENT_SCALE[name]
        km, kv = _jax.random.split(_jax.random.fold_in(key, i + 1))
        out[name] = w
        out["m_" + name] = s * _jax.random.normal(km, w.shape, _jnp.float32)
        out["v_" + name] = (s * s) * _jax.random.uniform(kv, w.shape, _jnp.float32, 0.5, 1.5)
    if N_MICROBATCH > 1:
        for name, axis in PER_EXAMPLE_BATCH_AXIS.items():
            out[name] = _to_microbatches(out[name], axis)
    return {'x': out['x'], 'c': out['c'], 'positions': out['positions'], 'ada_w': out['ada_w'], 'ada_b': out['ada_b'], 'norm_g': out['norm_g'], 'ffn_w13': out['ffn_w13'], 'ffn_w2': out['ffn_w2'], 'conv_w_pw1': out['conv_w_pw1'], 'conv_b_pw1': out['conv_b_pw1'], 'conv_w_dw': out['conv_w_dw'], 'conv_b_dw': out['conv_b_dw'], 'conv_ln_g': out['conv_ln_g'], 'conv_ln_b': out['conv_ln_b'], 'conv_w_pw2': out['conv_w_pw2'], 'conv_b_pw2': out['conv_b_pw2'], 'kv_ada_w': out['kv_ada_w'], 'kv_ada_b': out['kv_ada_b'], 'kv_norm_g': out['kv_norm_g'], 'w_kv_a': out['w_kv_a'], 'kv_a_norm_g': out['kv_a_norm_g'], 'w_kv_b': out['w_kv_b'], 'w_q_a': out['w_q_a'], 'q_a_norm_g': out['q_a_norm_g'], 'w_q_b': out['w_q_b'], 'w_o': out['w_o'], 'final_norm_g': out['final_norm_g'], 'loss_target': out['loss_target'], 'm_ada_w': out['m_ada_w'], 'm_ada_b': out['m_ada_b'], 'm_norm_g': out['m_norm_g'], 'm_ffn_w13': out['m_ffn_w13'], 'm_ffn_w2': out['m_ffn_w2'], 'm_conv_w_pw1': out['m_conv_w_pw1'], 'm_conv_b_pw1': out['m_conv_b_pw1'], 'm_conv_w_dw': out['m_conv_w_dw'], 'm_conv_b_dw': out['m_conv_b_dw'], 'm_conv_ln_g': out['m_conv_ln_g'], 'm_conv_ln_b': out['m_conv_ln_b'], 'm_conv_w_pw2': out['m_conv_w_pw2'], 'm_conv_b_pw2': out['m_conv_b_pw2'], 'm_kv_ada_w': out['m_kv_ada_w'], 'm_kv_ada_b': out['m_kv_ada_b'], 'm_kv_norm_g': out['m_kv_norm_g'], 'm_w_kv_a': out['m_w_kv_a'], 'm_kv_a_norm_g': out['m_kv_a_norm_g'], 'm_w_kv_b': out['m_w_kv_b'], 'm_w_q_a': out['m_w_q_a'], 'm_q_a_norm_g': out['m_q_a_norm_g'], 'm_w_q_b': out['m_w_q_b'], 'm_w_o': out['m_w_o'], 'm_final_norm_g': out['m_final_norm_g'], 'v_ada_w': out['v_ada_w'], 'v_ada_b': out['v_ada_b'], 'v_norm_g': out['v_norm_g'], 'v_ffn_w13': out['v_ffn_w13'], 'v_ffn_w2': out['v_ffn_w2'], 'v_conv_w_pw1': out['v_conv_w_pw1'], 'v_conv_b_pw1': out['v_conv_b_pw1'], 'v_conv_w_dw': out['v_conv_w_dw'], 'v_conv_b_dw': out['v_conv_b_dw'], 'v_conv_ln_g': out['v_conv_ln_g'], 'v_conv_ln_b': out['v_conv_ln_b'], 'v_conv_w_pw2': out['v_conv_w_pw2'], 'v_conv_b_pw2': out['v_conv_b_pw2'], 'v_kv_ada_w': out['v_kv_ada_w'], 'v_kv_ada_b': out['v_kv_ada_b'], 'v_kv_norm_g': out['v_kv_norm_g'], 'v_w_kv_a': out['v_w_kv_a'], 'v_kv_a_norm_g': out['v_kv_a_norm_g'], 'v_w_kv_b': out['v_w_kv_b'], 'v_w_q_a': out['v_w_q_a'], 'v_q_a_norm_g': out['v_q_a_norm_g'], 'v_w_q_b': out['v_w_q_b'], 'v_w_o': out['v_w_o'], 'v_final_norm_g': out['v_final_norm_g']}


def _loss(weights, diff, rest, loss_target):
    with _jax.named_scope("forward"):
        args = {**rest, TWIN_DIFF_INPUT: diff, **{k: w.astype(_WEIGHT_DTYPES[k]) for k, w in weights.items()}}
        y = _forward(args)
    with _jax.named_scope("loss_head"):
        err = _jnp.square(y.astype(_jnp.float32) - loss_target)
        return 0.5 * _jnp.sum(_jnp.mean(err, axis=-1)) if err.ndim else 0.5 * err


def _adamw(w, g, m, v):
    m = ADAM_B1 * m + (1.0 - ADAM_B1) * g
    v = ADAM_B2 * v + (1.0 - ADAM_B2) * _jnp.square(g)
    m_hat = m / (1.0 - ADAM_B1 ** ADAM_STEP)
    v_hat = v / (1.0 - ADAM_B2 ** ADAM_STEP)
    delta = -ADAM_LR * (m_hat / (_jnp.sqrt(v_hat) + ADAM_EPS) + ADAM_WD * w)
    return delta, m, v


def reference(x, c, positions, ada_w, ada_b, norm_g, ffn_w13, ffn_w2, conv_w_pw1, conv_b_pw1, conv_w_dw, conv_b_dw, conv_ln_g, conv_ln_b, conv_w_pw2, conv_b_pw2, kv_ada_w, kv_ada_b, kv_norm_g, w_kv_a, kv_a_norm_g, w_kv_b, w_q_a, q_a_norm_g, w_q_b, w_o, final_norm_g, loss_target, m_ada_w, m_ada_b, m_norm_g, m_ffn_w13, m_ffn_w2, m_conv_w_pw1, m_conv_b_pw1, m_conv_w_dw, m_conv_b_dw, m_conv_ln_g, m_conv_ln_b, m_conv_w_pw2, m_conv_b_pw2, m_kv_ada_w, m_kv_ada_b, m_kv_norm_g, m_w_kv_a, m_kv_a_norm_g, m_w_kv_b, m_w_q_a, m_q_a_norm_g, m_w_q_b, m_w_o, m_final_norm_g, v_ada_w, v_ada_b, v_norm_g, v_ffn_w13, v_ffn_w2, v_conv_w_pw1, v_conv_b_pw1, v_conv_w_dw, v_conv_b_dw, v_conv_ln_g, v_conv_ln_b, v_conv_w_pw2, v_conv_b_pw2, v_kv_ada_w, v_kv_ada_b, v_kv_norm_g, v_w_kv_a, v_kv_a_norm_g, v_w_kv_b, v_w_q_a, v_q_a_norm_g, v_w_q_b, v_w_o, v_final_norm_g):
    given = dict(x=x, c=c, positions=positions, ada_w=ada_w, ada_b=ada_b, norm_g=norm_g, ffn_w13=ffn_w13, ffn_w2=ffn_w2, conv_w_pw1=conv_w_pw1, conv_b_pw1=conv_b_pw1, conv_w_dw=conv_w_dw, conv_b_dw=conv_b_dw, conv_ln_g=conv_ln_g, conv_ln_b=conv_ln_b, conv_w_pw2=conv_w_pw2, conv_b_pw2=conv_b_pw2, kv_ada_w=kv_ada_w, kv_ada_b=kv_ada_b, kv_norm_g=kv_norm_g, w_kv_a=w_kv_a, kv_a_norm_g=kv_a_norm_g, w_kv_b=w_kv_b, w_q_a=w_q_a, q_a_norm_g=q_a_norm_g, w_q_b=w_q_b, w_o=w_o, final_norm_g=final_norm_g, loss_target=loss_target, m_ada_w=m_ada_w, m_ada_b=m_ada_b, m_norm_g=m_norm_g, m_ffn_w13=m_ffn_w13, m_ffn_w2=m_ffn_w2, m_conv_w_pw1=m_conv_w_pw1, m_conv_b_pw1=m_conv_b_pw1, m_conv_w_dw=m_conv_w_dw, m_conv_b_dw=m_conv_b_dw, m_conv_ln_g=m_conv_ln_g, m_conv_ln_b=m_conv_ln_b, m_conv_w_pw2=m_conv_w_pw2, m_conv_b_pw2=m_conv_b_pw2, m_kv_ada_w=m_kv_ada_w, m_kv_ada_b=m_kv_ada_b, m_kv_norm_g=m_kv_norm_g, m_w_kv_a=m_w_kv_a, m_kv_a_norm_g=m_kv_a_norm_g, m_w_kv_b=m_w_kv_b, m_w_q_a=m_w_q_a, m_q_a_norm_g=m_q_a_norm_g, m_w_q_b=m_w_q_b, m_w_o=m_w_o, m_final_norm_g=m_final_norm_g, v_ada_w=v_ada_w, v_ada_b=v_ada_b, v_norm_g=v_norm_g, v_ffn_w13=v_ffn_w13, v_ffn_w2=v_ffn_w2, v_conv_w_pw1=v_conv_w_pw1, v_conv_b_pw1=v_conv_b_pw1, v_conv_w_dw=v_conv_w_dw, v_conv_b_dw=v_conv_b_dw, v_conv_ln_g=v_conv_ln_g, v_conv_ln_b=v_conv_ln_b, v_conv_w_pw2=v_conv_w_pw2, v_conv_b_pw2=v_conv_b_pw2, v_kv_ada_w=v_kv_ada_w, v_kv_ada_b=v_kv_ada_b, v_kv_norm_g=v_kv_norm_g, v_w_kv_a=v_w_kv_a, v_kv_a_norm_g=v_kv_a_norm_g, v_w_kv_b=v_w_kv_b, v_w_q_a=v_w_q_a, v_q_a_norm_g=v_q_a_norm_g, v_w_q_b=v_w_q_b, v_w_o=v_w_o, v_final_norm_g=v_final_norm_g)
    weights = {n: given[n] for n in TWIN_WEIGHTS}
    shared = {n: given[n] for n in SHARED_INPUTS}
    per_example = {n: given[n] for n in ['x', 'c', 'positions']}
    grad_fn = _jax.value_and_grad(_loss, argnums=(0, 1))

    def one_microbatch(ex, loss_target):
        ex = dict(ex)
        diff = ex.pop(TWIN_DIFF_INPUT)
        return grad_fn(weights, diff, {**shared, **ex}, loss_target)

    if N_MICROBATCH == 1:
        loss, (grad_w, grad_x) = one_microbatch(per_example, given["loss_target"])
    else:
        def body(carry, xs):
            loss_sum, grad_sum = carry
            l_k, (gw_k, gx_k) = one_microbatch(xs[0], xs[1])
            with _jax.named_scope("update"):
                return (loss_sum + l_k, _jax.tree.map(_jnp.add, grad_sum, gw_k)), gx_k

        init = (_jnp.zeros((), _jnp.float32), _jax.tree.map(_jnp.zeros_like, weights))
        (loss, grad_w), grad_x = _jax.lax.scan(body, init, (per_example, given["loss_target"]))
    with _jax.named_scope("update"):
        delta_w, new_m, new_v = {}, {}, {}
        for n in TWIN_WEIGHTS:
            delta_w[n], new_m[n], new_v[n] = _adamw(weights[n], grad_w[n], given["m_" + n], given["v_" + n])
    return (loss, grad_x, *[grad_w[n] for n in TWIN_WEIGHTS], *[delta_w[n] for n in TWIN_WEIGHTS],
            *[new_m[n] for n in TWIN_WEIGHTS], *[new_v[n] for n in TWIN_WEIGHTS])
```

```python
import jax
import jax.numpy as jnp
from jax import lax
from jax.experimental import pallas as pl
from jax.experimental.pallas import tpu as pltpu

F32 = jnp.float32
BF16 = jnp.bfloat16
MESH = pl.DeviceIdType.MESH

N_HEADS = 16
QK_NOPE = 64
QK_ROPE = 32
V_HEAD = 64
KV_LORA = 256
CONV_WIDTH = 31
ROPE_THETA = 10000.0
EPS = 1e-6
N_MOD = 9
HEAD_PAD = 128
CONV_HALO = 32

ADAM_LR = 0.001
ADAM_B1 = 0.9
ADAM_B2 = 0.999
ADAM_EPS = 1e-08
ADAM_WD = 0.01
ADAM_STEP = 10

VMEM_LIMIT_BYTES = 56 * 2 ** 20
ROW_TILE_BUDGET = 10 * 2 ** 20
NEG = float(jnp.finfo(jnp.float32).min)


def _tile(n, prefs):
    for t in prefs:
        if n % t == 0:
            return t
    return n


def _params(sem):
    return pltpu.CompilerParams(dimension_semantics=sem, vmem_limit_bytes=VMEM_LIMIT_BYTES)


def mm(a, b, mode, name, out_dtype=F32, bias=None):
    if mode == "nn":
        (M, K), (K2, N) = a.shape, b.shape
        dims = (((1,), (0,)), ((), ()))
    elif mode == "nt":
        (M, K), (N, K2) = a.shape, b.shape
        dims = (((1,), (1,)), ((), ()))
    else:
        (K, M), (K2, N) = a.shape, b.shape
        dims = (((0,), (0,)), ((), ()))
    assert K == K2, (a.shape, b.shape, mode)
    tn = _tile(N, (512, 256, 128))
    if mode == "tn":
        tm = _tile(M, (1024, 512, 256, 128))
        tk = _tile(K, (1024, 512, 256, 128))
    else:
        tm = _tile(M, (1024, 512, 256, 128) if K <= 2816 else (512, 256, 128))
        tk = K
    nk = K // tk
    if mode == "tn":
        a_spec = pl.BlockSpec((tk, tm), lambda i, j, k: (k, i))
        b_spec = pl.BlockSpec((tk, tn), lambda i, j, k: (k, j))
    elif mode == "nn":
        a_spec = pl.BlockSpec((tm, tk), lambda i, j, k: (i, k))
        b_spec = pl.BlockSpec((tk, tn), lambda i, j, k: (k, j))
    else:
        a_spec = pl.BlockSpec((tm, tk), lambda i, j, k: (i, k))
        b_spec = pl.BlockSpec((tn, tk), lambda i, j, k: (j, k))
    in_specs = [a_spec, b_spec]
    operands = [a, b]
    if bias is not None:
        in_specs.append(pl.BlockSpec((1, tn), lambda i, j, k: (0, j)))
        operands.append(bias)
    has_bias = bias is not None

    def body(*refs):
        a_ref, b_ref = refs[0], refs[1]
        bias_ref = refs[2] if has_bias else None
        o_ref = refs[3] if has_bias else refs[2]
        prod = lax.dot_general(a_ref[...].astype(BF16), b_ref[...].astype(BF16), dims,
                               preferred_element_type=F32)
        if nk == 1:
            if has_bias:
                prod = prod + bias_ref[...]
            o_ref[...] = prod.astype(o_ref.dtype)
        else:
            acc_ref = refs[-1]
            k = pl.program_id(2)

            @pl.when(k == 0)
            def _():
                acc_ref[...] = jnp.zeros_like(acc_ref)

            acc_ref[...] += prod

            @pl.when(k == nk - 1)
            def _():
                out = acc_ref[...]
                if has_bias:
                    out = out + bias_ref[...]
                o_ref[...] = out.astype(o_ref.dtype)

    return pl.pallas_call(
        body, name=name,
        grid=(M // tm, N // tn, nk),
        in_specs=in_specs,
        out_specs=pl.BlockSpec((tm, tn), lambda i, j, k: (i, j)),
        out_shape=jax.ShapeDtypeStruct((M, N), out_dtype),
        scratch_shapes=[pltpu.VMEM((tm, tn), F32)] if nk > 1 else [],
        compiler_params=_params(("parallel", "parallel", "arbitrary")),
    )(*operands)


def rowwise(fn, rows, vecs, outs, sums, name, tm=None):
    norm = [(r, r.shape[1], 0) if not isinstance(r, tuple) else r for r in rows]
    S = norm[0][0].shape[0]
    if tm is None:
        per_row = sum(w * r.dtype.itemsize for r, w, _ in norm) + sum(n * jnp.dtype(dt).itemsize for n, dt in outs)
        tm = S
        for t in (512, 256, 128, 64, 32, 16, 8):
            if S % t == 0:
                tm = t
                if t * per_row <= ROW_TILE_BUDGET:
                    break
    n_rows, n_vecs, n_outs, n_sums = len(norm), len(vecs), len(outs), len(sums)
    in_specs = [pl.BlockSpec((tm, w), lambda i, cb=cb: (i, cb)) for _, w, cb in norm]
    in_specs += [pl.BlockSpec(v.shape, lambda i: (0, 0)) for v in vecs]
    out_specs = [pl.BlockSpec((tm, n), lambda i: (i, 0)) for n, _ in outs]
    out_specs += [pl.BlockSpec((1, n), lambda i: (0, 0)) for n in sums]
    out_shape = [jax.ShapeDtypeStruct((S, n), dt) for n, dt in outs]
    out_shape += [jax.ShapeDtypeStruct((1, n), F32) for n in sums]

    def body(*refs):
        ins = [r[...] for r in refs[:n_rows + n_vecs]]
        res = fn(*ins)
        if not isinstance(res, (tuple, list)):
            res = (res,)
        out_refs = refs[n_rows + n_vecs:]
        for o_ref, val in zip(out_refs[:n_outs], res[:n_outs]):
            o_ref[...] = val.astype(o_ref.dtype)
        if n_sums:
            i = pl.program_id(0)
            for s_ref, val in zip(out_refs[n_outs:], res[n_outs:]):
                part = jnp.sum(val.astype(F32), axis=0, keepdims=True)

                @pl.when(i == 0)
                def _(s_ref=s_ref, part=part):
                    s_ref[...] = part

                @pl.when(i != 0)
                def _(s_ref=s_ref, part=part):
                    s_ref[...] += part

    res = pl.pallas_call(
        body, name=name,
        grid=(S // tm,),
        in_specs=in_specs, out_specs=out_specs, out_shape=out_shape,
        compiler_params=_params(("arbitrary",) if n_sums else ("parallel",)),
    )(*[r for r, _, _ in norm], *vecs)
    return res


def _sigmoid(x):
    return jax.nn.sigmoid(x)


def _rms(x):
    r = lax.rsqrt(jnp.mean(x * x, axis=-1, keepdims=True) + EPS)
    return x * r, r


def _rms_bwd(xhat, r, dxhat):
    return r * (dxhat - xhat * jnp.mean(dxhat * xhat, axis=-1, keepdims=True))


def norm_mod(h, g, sh, sc, name):
    def f(h, g, sh, sc):
        xhat, _ = _rms(h)
        return ((xhat * g) * (1 + sc) + sh).astype(BF16)
    return rowwise(f, [h], [g, sh, sc], [(h.shape[1], BF16)], [], name)[0]


def norm_mod_bwd(h, dhn, dh_out, g, sc, name):
    D = h.shape[1]
    with_res = dh_out is not None

    def f(*a):
        if with_res:
            h, dhn, dres, g, sc = a
        else:
            h, dhn, g, sc = a
        xhat, r = _rms(h)
        xn = xhat * g
        dxn = dhn * (1 + sc)
        dh = _rms_bwd(xhat, r, dxn * g)
        if with_res:
            dh = dh + dres
        return dh, dhn, dhn * xn, dxn * xhat

    rows = [h, dhn] + ([dh_out] if with_res else [])
    return rowwise(f, rows, [g, sc], [(D, F32)], [D, D, D], name)


def residual(h, y, gate, coef, name, bias=None):
    D = h.shape[1]
    if bias is None:
        def f(h, y, gate):
            return h + (coef * gate) * y
        return rowwise(f, [h, y], [gate], [(D, F32)], [], name)[0], y

    def fb(h, y, gate, bias):
        yb = y + bias
        return h + (coef * gate) * yb, yb
    return rowwise(fb, [h, y], [gate, bias], [(D, F32), (D, F32)], [], name)


def residual_bwd(dh_out, y, gate, coef, name, with_bias_sum=False):
    D = y.shape[1]

    def f(dh, y, gate):
        dy = (coef * gate) * dh
        res = (dy.astype(BF16), coef * dh * y)
        return res + ((dy,) if with_bias_sum else ())
    return rowwise(f, [dh_out, y], [gate], [(D, BF16)], [D, D] if with_bias_sum else [D], name)


def ffn_fwd(h, g, sh, sc, gate, w13, w2):
    F = w2.shape[0]
    hn = norm_mod(h, g, sh, sc, "ffn_norm_mod")
    ab = mm(hn, w13, "nn", "ffn_w13")

    def act(a, b):
        return ((a * _sigmoid(a)) * b).astype(BF16)
    t = rowwise(act, [(ab, F, 0), (ab, F, 1)], [], [(F, BF16)], [], "ffn_act")[0]
    y = mm(t, w2, "nn", "ffn_w2")
    h_out, _ = residual(h, y, gate, 0.5, "ffn_residual")
    return h_out, (h, hn, ab, y)


def ffn_bwd(dh_out, saved, g, sc, gate, w13, w2):
    h, hn, ab, y = saved
    F = w2.shape[0]
    dy, d_gate = residual_bwd(dh_out, y, gate, 0.5, "ffn_residual_bwd")
    dt = mm(dy, w2, "nt", "ffn_w2_dx")

    def act_bwd(a, b, dt):
        sig = _sigmoid(a)
        sa = a * sig
        da = dt * b * (sig * (1 + a * (1 - sig)))
        db = dt * sa
        return (sa * b).astype(BF16), jnp.concatenate([da, db], axis=1).astype(BF16)
    t, dab = rowwise(act_bwd, [(ab, F, 0), (ab, F, 1), dt], [], [(F, BF16), (2 * F, BF16)], [], "ffn_act_bwd")
    dw2 = mm(t, dy, "tn", "ffn_w2_dw")
    dw13 = mm(hn, dab, "tn", "ffn_w13_dw")
    dhn = mm(dab, w13, "nt", "ffn_w13_dx")
    dh_in, d_sh, d_sc, d_g = norm_mod_bwd(h, dhn, dh_out, g, sc, "norm_mod_bwd")
    return dh_in, (d_sh, d_sc, d_gate, d_g), dw13, dw2


def _shifted(xbuf, n):
    return [xbuf] + [pltpu.roll(xbuf, n - b, 0) for b in range(1, 8)]


def conv_fwd(u, w_dw, b_dw, ln_g, ln_b):
    S, D = u.shape
    tm = _tile(S, (256, 128))
    rc = 32
    first_tap = CONV_HALO - (CONV_WIDTH - 1)
    w = jnp.concatenate([w_dw, jnp.zeros((CONV_HALO - CONV_WIDTH, D), F32)], axis=0)

    def body(cur_ref, prev_ref, w_ref, b_ref, g_ref, beta_ref, z_ref, s_ref):
        i = pl.program_id(0)
        prev = jnp.where(i == 0, jnp.zeros((CONV_HALO, D), F32), prev_ref[...])
        xs = _shifted(jnp.concatenate([prev, cur_ref[...]], axis=0), tm + CONV_HALO)
        for c0 in range(0, tm, rc):
            acc = jnp.zeros((rc, D), F32)
            for k in range(CONV_WIDTH):
                off = first_tap + k
                a8, b = off // 8 * 8, off % 8
                acc = acc + w_ref[k:k + 1, :] * xs[b][c0 + a8:c0 + a8 + rc, :]
            z_ref[c0:c0 + rc, :] = acc + b_ref[...]
        z = z_ref[...]
        mu = jnp.mean(z, axis=-1, keepdims=True)
        zc = z - mu
        r = lax.rsqrt(jnp.mean(zc * zc, axis=-1, keepdims=True) + EPS)
        un = zc * r * g_ref[...] + beta_ref[...]
        s_ref[...] = (un * _sigmoid(un)).astype(BF16)

    nb = tm // CONV_HALO
    vec = pl.BlockSpec((1, D), lambda i: (0, 0))
    return pl.pallas_call(
        body, name="conv_fwd",
        grid=(S // tm,),
        in_specs=[pl.BlockSpec((tm, D), lambda i: (i, 0)),
                  pl.BlockSpec((CONV_HALO, D), lambda i: (jnp.maximum(i * nb - 1, 0), 0)),
                  pl.BlockSpec((CONV_HALO, D), lambda i: (0, 0)), vec, vec, vec],
        out_specs=[pl.BlockSpec((tm, D), lambda i: (i, 0)), pl.BlockSpec((tm, D), lambda i: (i, 0))],
        out_shape=[jax.ShapeDtypeStruct((S, D), F32), jax.ShapeDtypeStruct((S, D), BF16)],
        compiler_params=_params(("parallel",)),
    )(u, u, w, b_dw, ln_g, ln_b)


def conv_bwd(dz, u, w_dw):
    S, D = u.shape
    tm = _tile(S, (256, 128))
    rc = 32
    first_tap = CONV_HALO - (CONV_WIDTH - 1)
    w = jnp.concatenate([w_dw, jnp.zeros((CONV_HALO - CONV_WIDTH, D), F32)], axis=0)
    n_tiles = S // tm
    nb = tm // CONV_HALO

    def body(dz_ref, dzn_ref, u_ref, up_ref, w_ref, du_ref, dw_ref):
        i = pl.program_id(0)
        nxt = jnp.where(i == n_tiles - 1, jnp.zeros((CONV_HALO, D), F32), dzn_ref[...])
        dzs = _shifted(jnp.concatenate([dz_ref[...], nxt], axis=0), tm + CONV_HALO)
        for c0 in range(0, tm, rc):
            acc = jnp.zeros((rc, D), F32)
            for m in range(CONV_WIDTH):
                a8, b = m // 8 * 8, m % 8
                acc = acc + w_ref[CONV_WIDTH - 1 - m:CONV_WIDTH - m, :] * dzs[b][c0 + a8:c0 + a8 + rc, :]
            du_ref[c0:c0 + rc, :] = acc
        prev = jnp.where(i == 0, jnp.zeros((CONV_HALO, D), F32), up_ref[...])
        us = _shifted(jnp.concatenate([prev, u_ref[...]], axis=0), tm + CONV_HALO)
        dz = dz_ref[...]

        @pl.when(i == 0)
        def _():
            dw_ref[...] = jnp.zeros_like(dw_ref)

        for k in range(CONV_WIDTH):
            off = first_tap + k
            a8, b = off // 8 * 8, off % 8
            dw_ref[k:k + 1, :] += jnp.sum(dz * us[b][a8:a8 + tm, :], axis=0, keepdims=True)

    last_blk = S // CONV_HALO - 1
    du, dw = pl.pallas_call(
        body, name="conv_bwd",
        grid=(n_tiles,),
        in_specs=[pl.BlockSpec((tm, D), lambda i: (i, 0)),
                  pl.BlockSpec((CONV_HALO, D), lambda i: (jnp.minimum((i + 1) * nb, last_blk), 0)),
                  pl.BlockSpec((tm, D), lambda i: (i, 0)),
                  pl.BlockSpec((CONV_HALO, D), lambda i: (jnp.maximum(i * nb - 1, 0), 0)),
                  pl.BlockSpec((CONV_HALO, D), lambda i: (0, 0))],
        out_specs=[pl.BlockSpec((tm, D), lambda i: (i, 0)), pl.BlockSpec((CONV_HALO, D), lambda i: (0, 0))],
        out_shape=[jax.ShapeDtypeStruct((S, D), F32), jax.ShapeDtypeStruct((CONV_HALO, D), F32)],
        compiler_params=_params(("arbitrary",)),
    )(dz, dz, u, u, w)
    return du, dw[:CONV_WIDTH]


def conv_module_fwd(h, g, sh, sc, gate, p):
    D = h.shape[1]
    hn = norm_mod(h, g, sh, sc, "conv_norm_mod")
    pre = mm(hn, p["w_pw1"], "nn", "conv_pw1")
    ba, bg = p["b_pw1"][:, :D], p["b_pw1"][:, D:]

    def glu(a, gt, ba, bg):
        return (a + ba) * _sigmoid(gt + bg)
    u = rowwise(glu, [(pre, D, 0), (pre, D, 1)], [ba, bg], [(D, F32)], [], "conv_glu")[0]
    z, s = conv_fwd(u, p["w_dw"], p["b_dw"], p["ln_g"], p["ln_b"])
    yraw = mm(s, p["w_pw2"], "nn", "conv_pw2")
    h_out, y = residual(h, yraw, gate, 1.0, "conv_residual", bias=p["b_pw2"])
    return h_out, (h, hn, pre, u, z, s, y)


def conv_module_bwd(dh_out, saved, g, sc, gate, p):
    h, hn, pre, u, z, s, y = saved
    D = h.shape[1]
    dy, d_gate, d_b_pw2 = residual_bwd(dh_out, y, gate, 1.0, "conv_residual_bwd", with_bias_sum=True)
    d_w_pw2 = mm(s, dy, "tn", "conv_pw2_dw")
    ds = mm(dy, p["w_pw2"], "nt", "conv_pw2_dx")

    def ln_bwd(z, ds, g, beta):
        mu = jnp.mean(z, axis=-1, keepdims=True)
        zc = z - mu
        r = lax.rsqrt(jnp.mean(zc * zc, axis=-1, keepdims=True) + EPS)
        xhat = zc * r
        un = xhat * g + beta
        sig = _sigmoid(un)
        d_un = ds * (sig * (1 + un * (1 - sig)))
        dxhat = d_un * g
        dz = r * (dxhat - jnp.mean(dxhat, axis=-1, keepdims=True)
                  - xhat * jnp.mean(dxhat * xhat, axis=-1, keepdims=True))
        return dz, d_un * xhat, d_un, dz
    dz, d_ln_g, d_ln_b, d_b_dw = rowwise(ln_bwd, [z, ds], [p["ln_g"], p["ln_b"]], [(D, F32)], [D, D, D],
                                         "conv_ln_bwd")
    du, d_w_dw = conv_bwd(dz, u, p["w_dw"])
    ba, bg = p["b_pw1"][:, :D], p["b_pw1"][:, D:]

    def glu_bwd(a, gt, du, ba, bg):
        sg = _sigmoid(gt + bg)
        da = du * sg
        dg = du * (a + ba) * (sg * (1 - sg))
        dpre = jnp.concatenate([da, dg], axis=1)
        return dpre.astype(BF16), dpre
    dpre, d_b_pw1 = rowwise(glu_bwd, [(pre, D, 0), (pre, D, 1), du], [ba, bg], [(2 * D, BF16)], [2 * D],
                            "conv_glu_bwd")
    d_w_pw1 = mm(hn, dpre, "tn", "conv_pw1_dw")
    dhn = mm(dpre, p["w_pw1"], "nt", "conv_pw1_dx")
    dh_in, d_sh, d_sc, d_g = norm_mod_bwd(h, dhn, dh_out, g, sc, "norm_mod_bwd")
    grads = dict(w_pw1=d_w_pw1, b_pw1=d_b_pw1, w_dw=d_w_dw, b_dw=d_b_dw, ln_g=d_ln_g, ln_b=d_ln_b,
                 w_pw2=d_w_pw2, b_pw2=d_b_pw2)
    return dh_in, (d_sh, d_sc, d_gate, d_g), grads


def _rope(x, c, s1, s2):
    n = x.shape[1]
    return x * c + pltpu.roll(x, n - QK_ROPE // 2, 1) * s1 + pltpu.roll(x, QK_ROPE // 2, 1) * s2


def _rope_t(dy, c, s1, s2):
    n = dy.shape[1]
    return dy * c + pltpu.roll(dy * s1, QK_ROPE // 2, 1) + pltpu.roll(dy * s2, n - QK_ROPE // 2, 1)


def rope_tables(positions):
    inv_freq = ROPE_THETA ** (-jnp.arange(0, QK_ROPE, 2, dtype=F32) / QK_ROPE)
    ang = positions.astype(F32)[:, None] * inv_freq
    cos, sin = jnp.cos(ang), jnp.sin(ang)
    S = positions.shape[0]
    one = jnp.ones((S, QK_NOPE), F32)
    z16 = jnp.zeros((S, QK_ROPE // 2), F32)
    zn = jnp.zeros((S, QK_NOPE), F32)
    zt = jnp.zeros((S, HEAD_PAD - QK_NOPE - QK_ROPE), F32)
    c = jnp.concatenate([one, cos, cos, zt], axis=1)
    s1 = jnp.concatenate([zn, -sin, z16, zt], axis=1)
    s2 = jnp.concatenate([zn, z16, sin, zt], axis=1)
    return c, s1, s2


def attn_fwd(qr, kv, kpe, n_heads):
    S = qr.shape[0]
    H = n_heads
    tq = _tile(S, (512, 256, 128))
    nq = S // tq
    scale = (QK_NOPE + QK_ROPE) ** -0.5
    nt = (((1,), (1,)), ((), ()))

    def body(q_ref, k_ref, v_ref, kpe_ref, o_ref, lse_ref, kf_ref, m_ref, l_ref, acc_ref):
        qi = pl.program_id(1)

        @pl.when(qi == 0)
        def _():
            kf_ref[...] = k_ref[...] + kpe_ref[...]

        q = q_ref[...]
        m_ref[...] = jnp.full((tq, 1), -jnp.inf, F32)
        l_ref[...] = jnp.zeros((tq, 1), F32)
        acc_ref[...] = jnp.zeros((tq, HEAD_PAD), F32)

        def tile(j, masked):
            start = pl.multiple_of(j * tq, tq)
            k = kf_ref[pl.ds(start, tq), :]
            v = v_ref[pl.ds(start, tq), :]
            s = lax.dot_general(q, k, nt, preferred_element_type=F32) * scale
            if masked:
                row = lax.broadcasted_iota(jnp.int32, (tq, tq), 0)
                col = lax.broadcasted_iota(jnp.int32, (tq, tq), 1)
                s = jnp.where(col <= row, s, NEG)
            m_old = m_ref[...]
            m_new = jnp.maximum(m_old, jnp.max(s, axis=1, keepdims=True))
            alpha = jnp.exp(m_old - m_new)
            p = jnp.exp(s - m_new)
            l_ref[...] = alpha * l_ref[...] + jnp.sum(p, axis=1, keepdims=True)
            acc_ref[...] = alpha * acc_ref[...] + jnp.dot(p.astype(BF16), v, preferred_element_type=F32)
            m_ref[...] = m_new

        def unmasked(j, carry):
            tile(j, False)
            return carry

        lax.fori_loop(0, qi, unmasked, 0)
        tile(qi, True)
        l = l_ref[...]
        o_ref[...] = acc_ref[...] / l
        lse_ref[...] = m_ref[...] + jnp.log(l)

    return pl.pallas_call(
        body, name="attn_fwd",
        grid=(H, nq),
        in_specs=[pl.BlockSpec((tq, HEAD_PAD), lambda h, i: (i, h)),
                  pl.BlockSpec((S, HEAD_PAD), lambda h, i: (0, h)),
                  pl.BlockSpec((S, HEAD_PAD), lambda h, i: (0, H + h)),
                  pl.BlockSpec((S, HEAD_PAD), lambda h, i: (0, 0))],
        out_specs=[pl.BlockSpec((tq, HEAD_PAD), lambda h, i: (i, h)),
                   pl.BlockSpec((None, tq, 1), lambda h, i: (h, i, 0))],
        out_shape=[jax.ShapeDtypeStruct((S, H * HEAD_PAD), F32), jax.ShapeDtypeStruct((H, S, 1), F32)],
        scratch_shapes=[pltpu.VMEM((S, HEAD_PAD), BF16), pltpu.VMEM((tq, 1), F32), pltpu.VMEM((tq, 1), F32),
                        pltpu.VMEM((tq, HEAD_PAD), F32)],
        compiler_params=_params(("parallel", "arbitrary")),
    )(qr, kv, kv, kpe)


def attn_delta(o, do, n_heads):
    S = o.shape[0]
    H = n_heads
    tq = _tile(S, (512, 256, 128))

    def body(o_ref, do_ref, d_ref):
        d_ref[...] = jnp.sum(o_ref[...] * do_ref[...].astype(F32), axis=1, keepdims=True)

    return pl.pallas_call(
        body, name="attn_delta",
        grid=(H, S // tq),
        in_specs=[pl.BlockSpec((tq, HEAD_PAD), lambda h, i: (i, h)),
                  pl.BlockSpec((tq, HEAD_PAD), lambda h, i: (i, h))],
        out_specs=pl.BlockSpec((None, tq, 1), lambda h, i: (h, i, 0)),
        out_shape=jax.ShapeDtypeStruct((H, S, 1), F32),
        compiler_params=_params(("parallel", "parallel")),
    )(o, do)


def attn_bwd(qr, kv, kpe, do, lse, delta, n_heads):
    S = qr.shape[0]
    H = n_heads
    tq = _tile(S, (512, 256, 128))
    nq = S // tq
    scale = (QK_NOPE + QK_ROPE) ** -0.5
    nt = (((1,), (1,)), ((), ()))
    tn = (((0,), (0,)), ((), ()))
    lse4 = lse.reshape(H, nq, 1, tq)
    delta4 = delta.reshape(H, nq, 1, tq)

    def body(k_ref, v_ref, kpe_ref, q_ref, do_ref, lse_ref, dl_ref, dq_ref, dk_ref, dv_ref, dka_ref, dva_ref):
        kj = pl.program_id(1)
        k = k_ref[...] + kpe_ref[...]
        v = v_ref[...]

        @pl.when(kj == 0)
        def _():
            dq_ref[...] = jnp.zeros_like(dq_ref)

        dka_ref[...] = jnp.zeros_like(dka_ref)
        dva_ref[...] = jnp.zeros_like(dva_ref)

        def tile(i, masked):
            start = pl.multiple_of(i * tq, tq)
            q = q_ref[pl.ds(start, tq), :]
            do = do_ref[pl.ds(start, tq), :]
            st = lax.dot_general(k, q, nt, preferred_element_type=F32) * scale
            if masked:
                krow = lax.broadcasted_iota(jnp.int32, (tq, tq), 0)
                qcol = lax.broadcasted_iota(jnp.int32, (tq, tq), 1)
                st = jnp.where(krow <= qcol, st, NEG)
            pt = jnp.exp(st - lse_ref[i])
            dva_ref[...] += jnp.dot(pt.astype(BF16), do, preferred_element_type=F32)
            dpt = lax.dot_general(v, do, nt, preferred_element_type=F32)
            dst = (pt * (dpt - dl_ref[i]) * scale).astype(BF16)
            dka_ref[...] += jnp.dot(dst, q, preferred_element_type=F32)
            dq_ref[pl.ds(start, tq), :] += lax.dot_general(dst, k, tn, preferred_element_type=F32)

        tile(kj, True)

        def unmasked(i, carry):
            tile(i, False)
            return carry

        lax.fori_loop(kj + 1, nq, unmasked, 0)
        dk_ref[...] = dka_ref[...]
        dv_ref[...] = dva_ref[...]

    blk = pl.BlockSpec((tq, HEAD_PAD), lambda h, j: (j, h))
    whole = pl.BlockSpec((S, HEAD_PAD), lambda h, j: (0, h))
    stat = pl.BlockSpec((None, nq, 1, tq), lambda h, j: (h, 0, 0, 0))
    shp = jax.ShapeDtypeStruct((S, H * HEAD_PAD), F32)
    return pl.pallas_call(
        body, name="attn_bwd",
        grid=(H, nq),
        in_specs=[blk, pl.BlockSpec((tq, HEAD_PAD), lambda h, j: (j, H + h)),
                  pl.BlockSpec((tq, HEAD_PAD), lambda h, j: (j, 0)), whole, whole, stat, stat],
        out_specs=[whole, blk, blk],
        out_shape=[shp, shp, shp],
        scratch_shapes=[pltpu.VMEM((tq, HEAD_PAD), F32), pltpu.VMEM((tq, HEAD_PAD), F32)],
        compiler_params=_params(("parallel", "arbitrary")),
    )(kv, kv, kpe, qr, do, lse4, delta4)


def _pad_heads(w, width):
    R = w.shape[0]
    w3 = w.reshape(R, -1, width)
    return jnp.pad(w3, ((0, 0), (0, 0), (0, HEAD_PAD - width))).reshape(R, -1)


def _unpad_heads(w, width):
    R = w.shape[0]
    return w.reshape(R, -1, HEAD_PAD)[:, :, :width].reshape(R, -1)


def mla_pad_weights(p):
    H = N_HEADS
    w_q_b = _pad_heads(p["w_q_b"], QK_NOPE + QK_ROPE)
    kvb = p["w_kv_b"].reshape(KV_LORA, H, QK_NOPE + V_HEAD)
    wk = _pad_heads(kvb[:, :, :QK_NOPE].reshape(KV_LORA, -1), QK_NOPE)
    wv = _pad_heads(kvb[:, :, QK_NOPE:].reshape(KV_LORA, -1), V_HEAD)
    D = p["w_kv_a"].shape[0]
    a = p["w_kv_a"]
    w_kv_a = jnp.concatenate([a[:, :KV_LORA], jnp.zeros((D, QK_NOPE), a.dtype), a[:, KV_LORA:],
                              jnp.zeros((D, HEAD_PAD - QK_NOPE - QK_ROPE), a.dtype)], axis=1)
    wo = p["w_o"].reshape(H, V_HEAD, -1)
    w_o = jnp.pad(wo, ((0, 0), (0, HEAD_PAD - V_HEAD), (0, 0))).reshape(H * HEAD_PAD, -1)
    return dict(w_q_a=p["w_q_a"], w_q_b=w_q_b, w_kv_b=jnp.concatenate([wk, wv], axis=1), w_kv_a=w_kv_a, w_o=w_o)


def mla_kv_fwd(h, g, sh, sc, kv_a_norm_g, pw, tabs):
    hkv = norm_mod(h, g, sh, sc, "kv_norm_mod")
    ckvp = mm(hkv, pw["w_kv_a"], "nn", "kv_a")

    def f(ckv, kpe, c, s1, s2, g):
        xhat, _ = _rms(ckv)
        return (xhat * g).astype(BF16), _rope(kpe, c, s1, s2).astype(BF16)
    ckv_n, kpe_r = rowwise(f, [(ckvp, KV_LORA, 0), (ckvp, HEAD_PAD, KV_LORA // HEAD_PAD), *tabs], [kv_a_norm_g],
                           [(KV_LORA, BF16), (HEAD_PAD, BF16)], [], "kv_a_norm_rope")
    kv = mm(ckv_n, pw["w_kv_b"], "nn", "kv_b", out_dtype=BF16)
    return kv, kpe_r, (h, hkv, ckvp, ckv_n)


def mla_kv_bwd(dk, dv, saved, g, sc, kv_a_norm_g, pw, tabs):
    h, hkv, ckvp, ckv_n = saved
    H = N_HEADS
    lane = jnp.arange(HEAD_PAD)
    pe_mask = ((lane >= QK_NOPE) & (lane < QK_NOPE + QK_ROPE)).astype(F32)[None, :]

    def f(dk, dv, c, s1, s2, mask):
        tot = dk[:, :HEAD_PAD]
        for hh in range(1, H):
            tot = tot + dk[:, hh * HEAD_PAD:(hh + 1) * HEAD_PAD]
        dkpe = _rope_t(tot * mask, c, s1, s2) * mask
        return jnp.concatenate([dk, dv], axis=1).astype(BF16), dkpe
    dkv, dkpe = rowwise(f, [dk, dv, *tabs], [pe_mask], [(2 * H * HEAD_PAD, BF16), (HEAD_PAD, F32)], [],
                        "kv_split_bwd")
    d_w_kv_b = mm(ckv_n, dkv, "tn", "kv_b_dw")
    dckv_n = mm(dkv, pw["w_kv_b"], "nt", "kv_b_dx")

    def f2(ckv, dn, dkpe, g):
        xhat, r = _rms(ckv)
        dx = _rms_bwd(xhat, r, dn * g)
        return jnp.concatenate([dx, dkpe], axis=1).astype(BF16), dn * xhat
    dckvp, d_kv_a_g = rowwise(f2, [(ckvp, KV_LORA, 0), dckv_n, dkpe], [kv_a_norm_g],
                              [(KV_LORA + HEAD_PAD, BF16)], [KV_LORA], "kv_a_norm_bwd")
    d_w_kv_a = mm(hkv, dckvp, "tn", "kv_a_dw")
    dhkv = mm(dckvp, pw["w_kv_a"], "nt", "kv_a_dx")
    dh, d_sh, d_sc, d_g = norm_mod_bwd(h, dhkv, None, g, sc, "norm_mod_bwd_nores")
    return dh, (d_sh, d_sc, d_g), d_kv_a_g, d_w_kv_a, d_w_kv_b


def mla_fwd(h, g, sh, sc, gate, q_a_norm_g, pw, kv, kpe_r, tabs):
    H = N_HEADS
    hn = norm_mod(h, g, sh, sc, "mla_norm_mod")
    qa = mm(hn, pw["w_q_a"], "nn", "q_a")

    def f(qa, g):
        xhat, _ = _rms(qa)
        return (xhat * g).astype(BF16)
    qa_n = rowwise(f, [qa], [q_a_norm_g], [(qa.shape[1], BF16)], [], "q_a_norm")[0]
    qp = mm(qa_n, pw["w_q_b"], "nn", "q_b")

    def frope(q, c, s1, s2):
        return jnp.concatenate([_rope(q[:, hh * HEAD_PAD:(hh + 1) * HEAD_PAD], c, s1, s2) for hh in range(H)],
                               axis=1).astype(BF16)
    qr = rowwise(frope, [qp, *tabs], [], [(H * HEAD_PAD, BF16)], [], "q_rope")[0]
    o, lse = attn_fwd(qr, kv, kpe_r, H)
    y = mm(o, pw["w_o"], "nn", "w_o")
    h_out, _ = residual(h, y, gate, 1.0, "mla_residual")
    return h_out, (h, hn, qa, qa_n, qr, o, lse, y)


def mla_bwd(dh_out, saved, g, sc, gate, q_a_norm_g, pw, kv, kpe_r, tabs):
    h, hn, qa, qa_n, qr, o, lse, y = saved
    H = N_HEADS
    dy, d_gate = residual_bwd(dh_out, y, gate, 1.0, "mla_residual_bwd")
    d_w_o = mm(o, dy, "tn", "w_o_dw")
    do = mm(dy, pw["w_o"], "nt", "w_o_dx", out_dtype=BF16)
    delta = attn_delta(o, do, H)
    dqr, dk, dv = attn_bwd(qr, kv, kpe_r, do, lse, delta, H)

    def frope_t(dq, c, s1, s2):
        return jnp.concatenate([_rope_t(dq[:, hh * HEAD_PAD:(hh + 1) * HEAD_PAD], c, s1, s2) for hh in range(H)],
                               axis=1).astype(BF16)
    dqp = rowwise(frope_t, [dqr, *tabs], [], [(H * HEAD_PAD, BF16)], [], "q_rope_bwd")[0]
    d_w_q_b = mm(qa_n, dqp, "tn", "q_b_dw")
    dqa_n = mm(dqp, pw["w_q_b"], "nt", "q_b_dx")

    def f(qa, dn, g):
        xhat, r = _rms(qa)
        return _rms_bwd(xhat, r, dn * g).astype(BF16), dn * xhat
    dqa, d_q_a_g = rowwise(f, [qa, dqa_n], [q_a_norm_g], [(qa.shape[1], BF16)], [qa.shape[1]], "q_a_norm_bwd")
    d_w_q_a = mm(hn, dqa, "tn", "q_a_dw")
    dhn = mm(dqa, pw["w_q_a"], "nt", "q_a_dx")
    dh_in, d_sh, d_sc, d_g = norm_mod_bwd(h, dhn, dh_out, g, sc, "norm_mod_bwd")
    grads = dict(w_q_a=d_w_q_a, q_a_norm_g=d_q_a_g, w_q_b=d_w_q_b, w_o=d_w_o)
    return dh_in, (d_sh, d_sc, d_gate, d_g), grads, dk, dv


def loss_head(h, target, g):
    D = h.shape[1]

    def f(h, t, g):
        xhat, r = _rms(h)
        err = xhat * g - t
        dy = err * (1.0 / D)
        dh = _rms_bwd(xhat, r, dy * g)
        return dh, (0.5 / D) * err * err, dy * xhat
    return rowwise(f, [h, target], [g], [(D, F32)], [D, D], "loss_head")


def _place():
    x, y, c = lax.axis_index("x"), lax.axis_index("y"), lax.axis_index("c")
    chips = [(1 - x, y), (x, 1 - y), (1 - x, 1 - y)]
    return x, y, c, chips


HBM_SPEC = pl.BlockSpec(memory_space=pltpu.HBM)


def all_gather8(v):
    m, n = v.shape

    def body(x_ref, out_ref, send_sems, recv_sems, local_sem):
        x, y, c, chips = _place()
        me, sibling = (x, y, c), (x, y, 1 - c)

        def rows(px, py, pc):
            return out_ref.at[4 * px + 2 * py + pc]

        def copy(k, block, to, src=None):
            return pltpu.make_async_remote_copy(
                src_ref=rows(*block) if src is None else src, dst_ref=rows(*block),
                send_sem=send_sems.at[k], recv_sem=recv_sems.at[k], device_id=to, device_id_type=MESH)

        mine = pltpu.make_async_copy(x_ref, rows(*me), local_sem)
        mine.start()
        first = [copy(0, me, sibling, src=x_ref)]
        first += [copy(1 + j, me, (*chip, c), src=x_ref) for j, chip in enumerate(chips)]
        for cp in first:
            cp.start()
        passed = [copy(4 + j, (*chip, c), sibling) for j, chip in enumerate(chips)]
        for j, chip in enumerate(chips):
            copy(1 + j, (*chip, c), me).wait_recv()
            passed[j].start()
        copy(0, sibling, me).wait_recv()
        for j, chip in enumerate(chips):
            copy(4 + j, (*chip, 1 - c), me).wait_recv()
        for cp in first + passed:
            cp.wait_send()
        mine.wait()

    return pl.pallas_call(
        body, name="all_gather8",
        out_shape=jax.ShapeDtypeStruct((8, m, n), v.dtype),
        in_specs=[pl.BlockSpec(memory_space=pltpu.VMEM)],
        out_specs=pl.BlockSpec(memory_space=pltpu.VMEM),
        scratch_shapes=[pltpu.SemaphoreType.DMA((7,)), pltpu.SemaphoreType.DMA((7,)), pltpu.SemaphoreType.DMA],
        compiler_params=pltpu.CompilerParams(vmem_limit_bytes=VMEM_LIMIT_BYTES),
    )(v)


def gather_weights(ws):
    n = len(ws)

    def body(*refs):
        ins, outs = refs[:n], refs[n:2 * n]
        send_sems, recv_sems, local_sems = refs[2 * n:]
        x, y, c, chips = _place()
        sibling = (x, y, 1 - c)
        me = 2 * x + y

        def idx(chip):
            return 2 * chip[0] + chip[1]

        def copy(w, k, src, dst, to):
            return pltpu.make_async_remote_copy(src_ref=src, dst_ref=dst, send_sem=send_sems.at[6 * w + k],
                                                recv_sem=recv_sems.at[6 * w + k], device_id=to, device_id_type=MESH)

        local = [pltpu.make_async_copy(ins[w], outs[w].at[me], local_sems.at[w]) for w in range(n)]
        for cp in local:
            cp.start()
        first = [copy(w, j, ins[w].at[c], outs[w].at[me, c], (*chip, c))
                 for w in range(n) for j, chip in enumerate(chips)]
        for cp in first:
            cp.start()
        passed = []
        for w in range(n):
            for j, chip in enumerate(chips):
                landed = outs[w].at[idx(chip), c]
                copy(w, j, landed, landed, (*chip, c)).wait_recv()
                fwd = copy(w, 3 + j, landed, landed, sibling)
                fwd.start()
                passed.append(fwd)
        for w in range(n):
            for j, chip in enumerate(chips):
                other = outs[w].at[idx(chip), 1 - c]
                copy(w, 3 + j, other, other, sibling).wait_recv()
        for cp in first + passed:
            cp.wait_send()
        for cp in local:
            cp.wait()

    return pl.pallas_call(
        body, name="gather_weights",
        out_shape=[jax.ShapeDtypeStruct((4,) + w.shape, w.dtype) for w in ws],
        in_specs=[HBM_SPEC] * n, out_specs=[HBM_SPEC] * n,
        scratch_shapes=[pltpu.SemaphoreType.DMA((6 * n,)), pltpu.SemaphoreType.DMA((6 * n,)),
                        pltpu.SemaphoreType.DMA((n,))],
    )(*ws)


def exchange_halves(gs):
    n = len(gs)

    def body(*refs):
        ins, mine, theirs = refs[:n], refs[n:2 * n], refs[2 * n:3 * n]
        send_sems, recv_sems, local_sems = refs[3 * n:]
        x, y, c, _ = _place()
        sibling = (x, y, 1 - c)
        local = [pltpu.make_async_copy(ins[w].at[:, c], mine[w], local_sems.at[w]) for w in range(n)]
        for cp in local:
            cp.start()
        sends = [pltpu.make_async_remote_copy(src_ref=ins[w].at[:, 1 - c], dst_ref=theirs[w],
                                              send_sem=send_sems.at[w], recv_sem=recv_sems.at[w],
                                              device_id=sibling, device_id_type=MESH) for w in range(n)]
        for cp in sends:
            cp.start()
        for cp in sends:
            cp.wait()
        for cp in local:
            cp.wait()

    shapes = [jax.ShapeDtypeStruct((4,) + g.shape[2:], g.dtype) for g in gs]
    res = pl.pallas_call(
        body, name="exchange_halves",
        out_shape=shapes + shapes,
        in_specs=[HBM_SPEC] * n, out_specs=[HBM_SPEC] * (2 * n),
        scratch_shapes=[pltpu.SemaphoreType.DMA((n,)), pltpu.SemaphoreType.DMA((n,)), pltpu.SemaphoreType.DMA((n,))],
    )(*gs)
    return res[:n], res[n:]


def scatter_blocks(ps):
    n = len(ps)

    def body(*refs):
        ins, outs = refs[:n], refs[n:2 * n]
        send_sems, recv_sems, local_sems = refs[2 * n:]
        x, y, c, chips = _place()
        me = 2 * x + y

        def idx(chip):
            return 2 * chip[0] + chip[1]

        local = [pltpu.make_async_copy(ins[w].at[me], outs[w].at[me], local_sems.at[w]) for w in range(n)]
        for cp in local:
            cp.start()
        sends = [pltpu.make_async_remote_copy(src_ref=ins[w].at[idx(chip)], dst_ref=outs[w].at[me],
                                              send_sem=send_sems.at[3 * w + j], recv_sem=recv_sems.at[3 * w + j],
                                              device_id=(*chip, c), device_id_type=MESH)
                 for w in range(n) for j, chip in enumerate(chips)]
        for cp in sends:
            cp.start()
        for w in range(n):
            for j, chip in enumerate(chips):
                got = outs[w].at[idx(chip)]
                pltpu.make_async_remote_copy(src_ref=got, dst_ref=got, send_sem=send_sems.at[3 * w + j],
                                             recv_sem=recv_sems.at[3 * w + j], device_id=(*chip, c),
                                             device_id_type=MESH).wait_recv()
        for cp in sends:
            cp.wait_send()
        for cp in local:
            cp.wait()

    return pl.pallas_call(
        body, name="scatter_blocks",
        out_shape=[jax.ShapeDtypeStruct(p.shape, p.dtype) for p in ps],
        in_specs=[HBM_SPEC] * n, out_specs=[HBM_SPEC] * n,
        scratch_shapes=[pltpu.SemaphoreType.DMA((3 * n,)), pltpu.SemaphoreType.DMA((3 * n,)),
                        pltpu.SemaphoreType.DMA((n,))],
    )(*ps)


def join_halves(qs):
    n = len(qs)

    def body(*refs):
        ins, outs = refs[:n], refs[n:2 * n]
        send_sems, recv_sems, local_sems = refs[2 * n:]
        x, y, c, _ = _place()
        sibling = (x, y, 1 - c)
        local = [pltpu.make_async_copy(ins[w], outs[w].at[c], local_sems.at[w]) for w in range(n)]
        for cp in local:
            cp.start()
        sends = [pltpu.make_async_remote_copy(src_ref=ins[w], dst_ref=outs[w].at[c], send_sem=send_sems.at[w],
                                              recv_sem=recv_sems.at[w], device_id=sibling, device_id_type=MESH)
                 for w in range(n)]
        for cp in sends:
            cp.start()
        for w in range(n):
            other = outs[w].at[1 - c]
            pltpu.make_async_remote_copy(src_ref=other, dst_ref=other, send_sem=send_sems.at[w],
                                         recv_sem=recv_sems.at[w], device_id=sibling, device_id_type=MESH).wait_recv()
        for cp in sends:
            cp.wait_send()
        for cp in local:
            cp.wait()

    return pl.pallas_call(
        body, name="join_halves",
        out_shape=[jax.ShapeDtypeStruct((2,) + q.shape, q.dtype) for q in qs],
        in_specs=[HBM_SPEC] * n, out_specs=[HBM_SPEC] * n,
        scratch_shapes=[pltpu.SemaphoreType.DMA((n,)), pltpu.SemaphoreType.DMA((n,)), pltpu.SemaphoreType.DMA((n,))],
    )(*qs)


def sum_blocks(items, name):
    R, C = items[0][0].shape[1:]
    tm = R
    for t in (512, 256, 128, 64, 32, 16, 8):
        if R % t == 0:
            tm = t
            if t * C * 4 * (len(items) + 1) <= ROW_TILE_BUDGET:
                break
    n = len(items)

    def body(*refs):
        acc = refs[0][...].astype(F32)
        for r in refs[1:n]:
            acc = acc + r[...].astype(F32)
        refs[n][...] = acc

    return pl.pallas_call(
        body, name=name,
        grid=(R // tm,),
        in_specs=[pl.BlockSpec((None, tm, C), lambda i, j=j: (j, i, 0)) for _, j in items],
        out_specs=pl.BlockSpec((tm, C), lambda i: (i, 0)),
        out_shape=jax.ShapeDtypeStruct((R, C), F32),
        compiler_params=_params(("parallel",)),
    )(*[a for a, _ in items])


def reduce_scatter_grads(gs):
    mine, theirs = exchange_halves(gs)
    ps = []
    for a, b in zip(mine, theirs):
        k, R, C = a.shape
        p = sum_blocks([(a.reshape(1, k * R, C), 0), (b.reshape(1, k * R, C), 0)], "sum_siblings")
        ps.append(p.reshape(k, R, C))
    landed = scatter_blocks(ps)
    qs = [sum_blocks([(l, j) for j in range(4)], "sum_chips") for l in landed]
    joined = join_halves(qs)
    return [j.reshape(2 * j.shape[1], j.shape[2]) for j in joined]


def adamw(w, g, m, v):
    shape = w.shape
    C = shape[-1]
    R = w.size // C
    tm = R
    for t in (512, 256, 128, 64, 32, 16, 8):
        if R % t == 0:
            tm = t
            if t * C * 4 * 7 <= ROW_TILE_BUDGET:
                break

    def f(w, g, m, v):
        m = ADAM_B1 * m + (1.0 - ADAM_B1) * g
        v = ADAM_B2 * v + (1.0 - ADAM_B2) * (g * g)
        m_hat = m / (1.0 - ADAM_B1 ** ADAM_STEP)
        v_hat = v / (1.0 - ADAM_B2 ** ADAM_STEP)
        delta = -ADAM_LR * (m_hat / (jnp.sqrt(v_hat) + ADAM_EPS) + ADAM_WD * w)
        return delta, m, v

    d, nm, nv = rowwise(f, [a.reshape(R, C) for a in (w, g, m, v)], [], [(C, F32)] * 3, [], "adamw", tm=tm)
    return d.reshape(shape), nm.reshape(shape), nv.reshape(shape)


def _halves(w):
    C = w.shape[-1]
    return w.reshape(2, w.size // C // 2, C)


def _cast_bf16(w):
    C = w.shape[-1]
    w2 = w.reshape(-1, C)
    return rowwise(lambda a: a.astype(BF16), [w2], [], [(C, BF16)], [], "cast_bf16")[0]


def _pack(vs):
    flat = jnp.concatenate([v.reshape(-1) for v in vs])
    n = flat.shape[0]
    total = -(-n // 1024) * 1024
    return jnp.pad(flat, (0, total - n)).reshape(total // 128, 128)


def _unpack(flat, like):
    out, o = [], 0
    for shp in like:
        sz = 1
        for d in shp:
            sz *= d
        out.append(flat[o:o + sz].reshape(shp))
        o += sz
    return out


def _cols_to_blocks(g, n_chips=4):
    R, N = g.shape
    C = N // n_chips
    return g.reshape(R, n_chips, C).transpose(1, 0, 2).reshape(n_chips, 2, R // 2, C)


def _rows_to_blocks(g, n_chips=4):
    R, C = g.shape
    return g.reshape(n_chips, 2, R // n_chips // 2, C)


def kernel(x, c, positions, ada_w, ada_b, norm_g, ffn_w13, ffn_w2, conv_w_pw1, conv_b_pw1, conv_w_dw, conv_b_dw, conv_ln_g, conv_ln_b, conv_w_pw2, conv_b_pw2, kv_ada_w, kv_ada_b, kv_norm_g, w_kv_a, kv_a_norm_g, w_kv_b, w_q_a, q_a_norm_g, w_q_b, w_o, final_norm_g, loss_target, m_ada_w, m_ada_b, m_norm_g, m_ffn_w13, m_ffn_w2, m_conv_w_pw1, m_conv_b_pw1, m_conv_w_dw, m_conv_b_dw, m_conv_ln_g, m_conv_ln_b, m_conv_w_pw2, m_conv_b_pw2, m_kv_ada_w, m_kv_ada_b, m_kv_norm_g, m_w_kv_a, m_kv_a_norm_g, m_w_kv_b, m_w_q_a, m_q_a_norm_g, m_w_q_b, m_w_o, m_final_norm_g, v_ada_w, v_ada_b, v_norm_g, v_ffn_w13, v_ffn_w2, v_conv_w_pw1, v_conv_b_pw1, v_conv_w_dw, v_conv_b_dw, v_conv_ln_g, v_conv_ln_b, v_conv_w_pw2, v_conv_b_pw2, v_kv_ada_w, v_kv_ada_b, v_kv_norm_g, v_w_kv_a, v_kv_a_norm_g, v_w_kv_b, v_w_q_a, v_q_a_norm_g, v_w_q_b, v_w_o, v_final_norm_g):
    S, D = x.shape[1], x.shape[2]
    H = N_HEADS
    F = ffn_w2.shape[2] * 4
    xi, yi, ci = lax.axis_index("x"), lax.axis_index("y"), lax.axis_index("c")
    chip = 2 * xi + yi
    dev = 2 * chip + ci
    h0 = x[0]
    target = loss_target[0]

    silu_c = rowwise(lambda a: a * _sigmoid(a), [c], [], [(D, F32)], [], "silu_c")[0]
    silu_all = all_gather8(silu_c.reshape(8, D // 8)).reshape(8, D)
    n_ada = ada_w.shape[2]
    n_kv = kv_ada_w.shape[1]
    ada_b_mine = lax.dynamic_slice_in_dim(ada_b, chip * n_ada, n_ada, axis=1)
    kv_b_mine = lax.dynamic_slice_in_dim(kv_ada_b, chip * n_kv, n_kv, axis=0)[None, :]
    mods = [mm(silu_all, ada_w[l], "nn", "ada_rows", bias=ada_b_mine[l:l + 1]) for l in range(2)]
    mods.append(mm(silu_all, kv_ada_w, "nn", "kv_ada_rows", bias=kv_b_mine))
    n_mod_cols = 2 * n_ada + n_kv
    mod_pack = jnp.concatenate(mods, axis=1).reshape(-1, 128)
    mod_all = all_gather8(mod_pack).reshape(8, 8, n_mod_cols)[0::2]
    mod_mine = lax.dynamic_index_in_dim(mod_all, dev, axis=1, keepdims=False)
    mod = [mod_mine[:, l * n_ada:(l + 1) * n_ada].reshape(N_MOD, D) for l in range(2)]
    kv_mod = mod_mine[:, 2 * n_ada:].reshape(2, D)
    kv_shift, kv_scale = kv_mod[0:1], kv_mod[1:2]

    def mrow(l, k):
        return mod[l][k:k + 1]

    big = dict(ffn_w13=ffn_w13, ffn_w2=ffn_w2, conv_w_pw1=conv_w_pw1, conv_w_pw2=conv_w_pw2, w_kv_a=w_kv_a,
               w_kv_b=w_kv_b, w_q_a=w_q_a, w_q_b=w_q_b, w_o=w_o)
    names = list(big)
    gathered = gather_weights([_halves(_cast_bf16(big[k])) for k in names])
    gw = dict(zip(names, gathered))
    small_like = [norm_g.shape, conv_b_pw1.shape, conv_w_dw.shape, conv_b_dw.shape, conv_ln_g.shape,
                  conv_ln_b.shape, conv_b_pw2.shape]
    small_pack = _pack([norm_g, conv_b_pw1, conv_w_dw, conv_b_dw, conv_ln_g, conv_ln_b, conv_b_pw2])
    small_all = all_gather8(small_pack)[0::2].reshape(4, -1)
    per_chip = [_unpack(small_all[j], small_like) for j in range(4)]
    smalls = [jnp.concatenate([per_chip[j][k] for j in range(4)], axis=-1) for k in range(len(small_like))]
    norm_g_f, b_pw1_f, w_dw_f, b_dw_f, ln_g_f, ln_b_f, b_pw2_f = smalls

    w13 = gw["ffn_w13"].reshape(4, 2, 2, D, 2 * F // 4).transpose(1, 2, 3, 0, 4).reshape(2, 2, D, 2 * F)
    w2 = gw["ffn_w2"].reshape(4, 2, 2, F // 4, D).transpose(1, 2, 0, 3, 4).reshape(2, 2, F, D)
    conv_p = dict(
        w_pw1=gw["conv_w_pw1"].reshape(4, D, 2 * D // 4).transpose(1, 0, 2).reshape(D, 2 * D),
        b_pw1=b_pw1_f, w_dw=w_dw_f[0], b_dw=b_dw_f, ln_g=ln_g_f, ln_b=ln_b_f,
        w_pw2=gw["conv_w_pw2"].reshape(D, D), b_pw2=b_pw2_f)
    q_lora = w_q_a.shape[2]
    mla_p = dict(
        w_kv_a=gw["w_kv_a"].reshape(D, KV_LORA + QK_ROPE),
        w_kv_b=gw["w_kv_b"].reshape(4, KV_LORA, -1).transpose(1, 0, 2).reshape(KV_LORA, -1),
        w_q_a=gw["w_q_a"].reshape(D, q_lora),
        w_q_b=gw["w_q_b"].reshape(4, q_lora, -1).transpose(1, 0, 2).reshape(q_lora, -1),
        w_o=gw["w_o"].reshape(H * V_HEAD, D))
    pw = mla_pad_weights(mla_p)
    tabs = rope_tables(positions[0])

    def ng(l, k):
        return norm_g_f[l, k][None, :]

    h = h0
    h, s_f1_0 = ffn_fwd(h, ng(0, 0), mrow(0, 0), mrow(0, 1), mrow(0, 2), w13[0, 0], w2[0, 0])
    h, s_conv = conv_module_fwd(h, ng(0, 1), mrow(0, 3), mrow(0, 4), mrow(0, 5), conv_p)
    h, s_f2_0 = ffn_fwd(h, ng(0, 2), mrow(0, 6), mrow(0, 7), mrow(0, 8), w13[0, 1], w2[0, 1])
    kv_norm = kv_norm_g[None, :]
    kv_a_g = kv_a_norm_g[None, :]
    kv, kpe_r, s_kv = mla_kv_fwd(h, kv_norm, kv_shift, kv_scale, kv_a_g, pw, tabs)
    h, s_f1_1 = ffn_fwd(h, ng(1, 0), mrow(1, 0), mrow(1, 1), mrow(1, 2), w13[1, 0], w2[1, 0])
    h, s_mla = mla_fwd(h, ng(1, 1), mrow(1, 3), mrow(1, 4), mrow(1, 5), q_a_norm_g, pw, kv, kpe_r, tabs)
    h, s_f2_1 = ffn_fwd(h, ng(1, 2), mrow(1, 6), mrow(1, 7), mrow(1, 8), w13[1, 1], w2[1, 1])
    dh, loss_cols, d_final_g = loss_head(h, target, final_norm_g[None, :])

    dh, v_f2_1, dw13_11, dw2_11 = ffn_bwd(dh, s_f2_1, ng(1, 2), mrow(1, 7), mrow(1, 8), w13[1, 1], w2[1, 1])
    dh, v_mla, g_mla, dk, dv = mla_bwd(dh, s_mla, ng(1, 1), mrow(1, 4), mrow(1, 5), q_a_norm_g, pw, kv, kpe_r, tabs)
    dh, v_f1_1, dw13_10, dw2_10 = ffn_bwd(dh, s_f1_1, ng(1, 0), mrow(1, 1), mrow(1, 2), w13[1, 0], w2[1, 0])
    dh_kv, v_kv, d_kv_a_g, d_w_kv_a, d_w_kv_b = mla_kv_bwd(dk, dv, s_kv, kv_norm, kv_scale, kv_a_g, pw, tabs)
    dh = rowwise(lambda a, b: a + b, [dh, dh_kv], [], [(D, F32)], [], "add_stream")[0]
    dh, v_f2_0, dw13_01, dw2_01 = ffn_bwd(dh, s_f2_0, ng(0, 2), mrow(0, 7), mrow(0, 8), w13[0, 1], w2[0, 1])
    dh, v_conv, g_conv = conv_module_bwd(dh, s_conv, ng(0, 1), mrow(0, 4), mrow(0, 5), conv_p)
    dh, v_f1_0, dw13_00, dw2_00 = ffn_bwd(dh, s_f1_0, ng(0, 0), mrow(0, 1), mrow(0, 2), w13[0, 0], w2[0, 0])
    grad_x = dh[None]

    d_w_kv_a_u = jnp.concatenate([d_w_kv_a[:, :KV_LORA], d_w_kv_a[:, KV_LORA + QK_NOPE:KV_LORA + QK_NOPE + QK_ROPE]],
                                 axis=1)
    hk = H * HEAD_PAD
    dkb = jnp.concatenate([d_w_kv_b[:, :hk].reshape(KV_LORA, H, HEAD_PAD)[:, :, :QK_NOPE],
                           d_w_kv_b[:, hk:].reshape(KV_LORA, H, HEAD_PAD)[:, :, :V_HEAD]], axis=2).reshape(KV_LORA, -1)
    d_w_q_b_u = _unpad_heads(g_mla["w_q_b"], QK_NOPE + QK_ROPE)
    d_w_o_u = g_mla["w_o"].reshape(H, HEAD_PAD, D)[:, :V_HEAD].reshape(H * V_HEAD, D)
    full = [_cols_to_blocks(dw13_00), _cols_to_blocks(dw13_01), _cols_to_blocks(dw13_10), _cols_to_blocks(dw13_11),
            _rows_to_blocks(dw2_00), _rows_to_blocks(dw2_01), _rows_to_blocks(dw2_10), _rows_to_blocks(dw2_11),
            _cols_to_blocks(g_conv["w_pw1"]), _rows_to_blocks(g_conv["w_pw2"]), _rows_to_blocks(d_w_kv_a_u),
            _cols_to_blocks(dkb), _rows_to_blocks(g_mla["w_q_a"]), _cols_to_blocks(d_w_q_b_u),
            _rows_to_blocks(d_w_o_u)]
    red = reduce_scatter_grads(full)
    g_ffn_w13 = jnp.stack(red[0:4]).reshape(ffn_w13.shape)
    g_ffn_w2 = jnp.stack(red[4:8]).reshape(ffn_w2.shape)
    g_conv_w_pw1 = red[8].reshape(conv_w_pw1.shape)
    g_conv_w_pw2 = red[9].reshape(conv_w_pw2.shape)
    g_w_kv_a = red[10].reshape(w_kv_a.shape)
    g_w_kv_b = red[11].reshape(w_kv_b.shape)
    g_w_q_a = red[12].reshape(w_q_a.shape)
    g_w_q_b = red[13].reshape(w_q_b.shape)
    g_w_o = red[14].reshape(w_o.shape)

    def dmod(v1, vm, v2):
        return jnp.concatenate([v1[0], v1[1], v1[2], vm[0], vm[1], vm[2], v2[0], v2[1], v2[2]], axis=1)
    d_mod0 = dmod(v_f1_0, v_conv, v_f2_0)
    d_mod1 = dmod(v_f1_1, v_mla, v_f2_1)
    d_kv_mod = jnp.concatenate([v_kv[0], v_kv[1]], axis=1)
    d_norm_g = jnp.concatenate([v_f1_0[3], v_conv[3], v_f2_0[3], v_f1_1[3], v_mla[3], v_f2_1[3]], axis=0)
    vec_list = [d_mod0, d_mod1, d_kv_mod, d_norm_g, g_conv["b_pw1"], g_conv["w_dw"], g_conv["b_dw"], g_conv["ln_g"],
                g_conv["ln_b"], g_conv["b_pw2"], v_kv[2], d_kv_a_g, g_mla["q_a_norm_g"], d_final_g, loss_cols]
    vec_like = [v.shape for v in vec_list]
    vec_pack = _pack(vec_list)
    n_mod_rows = (2 * N_MOD * D + 2 * D) // 128
    vec_all = all_gather8(vec_pack)
    vec_sum = sum_blocks([(vec_all, d) for d in range(8)], "sum_devices").reshape(-1)
    (_, _, _, s_norm_g, s_b_pw1, s_w_dw, s_b_dw, s_ln_g, s_ln_b, s_b_pw2, s_kv_norm_g, s_kv_a_g, s_q_a_g,
     s_final_g, s_loss) = _unpack(vec_sum, vec_like)
    loss = jnp.sum(s_loss)
    dmod_all = vec_all[:, :n_mod_rows].reshape(8, 2 * N_MOD * D + 2 * D)
    dmod_sum = vec_sum[:2 * N_MOD * D + 2 * D]
    g_ada_b = dmod_sum[:2 * N_MOD * D].reshape(2, N_MOD * D)
    g_kv_ada_b = dmod_sum[2 * N_MOD * D:]
    g_ada_w = []
    for l in range(2):
        cols = lax.dynamic_slice_in_dim(dmod_all[:, l * N_MOD * D:(l + 1) * N_MOD * D], chip * n_ada, n_ada, axis=1)
        g_ada_w.append(mm(silu_all, cols, "tn", "ada_w_grad"))
    g_ada_w = jnp.stack(g_ada_w)
    kv_cols = lax.dynamic_slice_in_dim(dmod_all[:, 2 * N_MOD * D:], chip * n_kv, n_kv, axis=1)
    g_kv_ada_w = mm(silu_all, kv_cols, "tn", "kv_ada_w_grad")

    def shard(v, width):
        return lax.dynamic_slice_in_dim(v, chip * width, width, axis=v.ndim - 1)

    Dq = D // 4
    g_norm_g = shard(s_norm_g.reshape(2, 3, D), Dq)
    g_conv_b_pw1 = shard(s_b_pw1, 2 * D // 4)
    g_conv_w_dw = shard(s_w_dw, Dq)[None]
    g_conv_b_dw = shard(s_b_dw, Dq)
    g_conv_ln_g = shard(s_ln_g, Dq)
    g_conv_ln_b = shard(s_ln_b, Dq)
    g_conv_b_pw2 = shard(s_b_pw2, Dq)

    grads = [g_ada_w, g_ada_b, g_norm_g, g_ffn_w13, g_ffn_w2, g_conv_w_pw1, g_conv_b_pw1, g_conv_w_dw, g_conv_b_dw,
             g_conv_ln_g, g_conv_ln_b, g_conv_w_pw2, g_conv_b_pw2, g_kv_ada_w, g_kv_ada_b, s_kv_norm_g[0], g_w_kv_a,
             s_kv_a_g[0], g_w_kv_b, g_w_q_a, s_q_a_g, g_w_q_b, g_w_o, s_final_g[0]]
    weights = [ada_w, ada_b, norm_g, ffn_w13, ffn_w2, conv_w_pw1, conv_b_pw1, conv_w_dw, conv_b_dw, conv_ln_g,
               conv_ln_b, conv_w_pw2, conv_b_pw2, kv_ada_w, kv_ada_b, kv_norm_g, w_kv_a, kv_a_norm_g, w_kv_b, w_q_a,
               q_a_norm_g, w_q_b, w_o, final_norm_g]
    ms = [m_ada_w, m_ada_b, m_norm_g, m_ffn_w13, m_ffn_w2, m_conv_w_pw1, m_conv_b_pw1, m_conv_w_dw, m_conv_b_dw,
          m_conv_ln_g, m_conv_ln_b, m_conv_w_pw2, m_conv_b_pw2, m_kv_ada_w, m_kv_ada_b, m_kv_norm_g, m_w_kv_a,
          m_kv_a_norm_g, m_w_kv_b, m_w_q_a, m_q_a_norm_g, m_w_q_b, m_w_o, m_final_norm_g]
    vs = [v_ada_w, v_ada_b, v_norm_g, v_ffn_w13, v_ffn_w2, v_conv_w_pw1, v_conv_b_pw1, v_conv_w_dw, v_conv_b_dw,
          v_conv_ln_g, v_conv_ln_b, v_conv_w_pw2, v_conv_b_pw2, v_kv_ada_w, v_kv_ada_b, v_kv_norm_g, v_w_kv_a,
          v_kv_a_norm_g, v_w_kv_b, v_w_q_a, v_q_a_norm_g, v_w_q_b, v_w_o, v_final_norm_g]
    grads = [g.reshape(w.shape) for g, w in zip(grads, weights)]
    deltas, new_m, new_v = [], [], []
    for w, g, m, v in zip(weights, grads, ms, vs):
        d, nm, nv = adamw(w, g, m, v)
        deltas.append(d)
        new_m.append(nm)
        new_v.append(nv)
    return (loss, grad_x, *grads, *deltas, *new_m, *new_v)
```

```python
import jax
import jax.numpy as jnp
from jax import lax
from jax.experimental import pallas as pl
from jax.experimental.pallas import tpu as pltpu

F32 = jnp.float32
BF16 = jnp.bfloat16
MESH = pl.DeviceIdType.MESH

N_HEADS = 16
QK_NOPE = 64
QK_ROPE = 32
V_HEAD = 64
KV_LORA = 256
CONV_WIDTH = 31
ROPE_THETA = 10000.0
EPS = 1e-6
N_MOD = 9
HEAD_PAD = 128
CONV_HALO = 32

ADAM_LR = 0.001
ADAM_B1 = 0.9
ADAM_B2 = 0.999
ADAM_EPS = 1e-08
ADAM_WD = 0.01
ADAM_STEP = 10

VMEM_LIMIT_BYTES = 56 * 2 ** 20
ROW_TILE_BUDGET = 10 * 2 ** 20
NEG = float(jnp.finfo(jnp.float32).min)
LOG2_E = 1.4426950408889634


def _tile(n, prefs):
    for t in prefs:
        if n % t == 0:
            return t
    return n


def _params(sem):
    return pltpu.CompilerParams(dimension_semantics=sem, vmem_limit_bytes=VMEM_LIMIT_BYTES)


def mm(a, b, mode, name, out_dtype=F32, bias=None):
    if mode == "nn":
        (M, K), (K2, N) = a.shape, b.shape
        dims = (((1,), (0,)), ((), ()))
    elif mode == "nt":
        (M, K), (N, K2) = a.shape, b.shape
        dims = (((1,), (1,)), ((), ()))
    else:
        (K, M), (K2, N) = a.shape, b.shape
        dims = (((0,), (0,)), ((), ()))
    assert K == K2, (a.shape, b.shape, mode)
    tn = _tile(N, (512, 256, 128))
    if mode == "tn":
        tm = _tile(M, (1024, 512, 256, 128))
        tk = _tile(K, (1024, 512, 256, 128))
    else:
        tm = _tile(M, (1024, 512, 256, 128) if K <= 2816 else (512, 256, 128))
        tk = K
    nk = K // tk
    if mode == "tn":
        a_spec = pl.BlockSpec((tk, tm), lambda i, j, k: (k, i))
        b_spec = pl.BlockSpec((tk, tn), lambda i, j, k: (k, j))
    elif mode == "nn":
        a_spec = pl.BlockSpec((tm, tk), lambda i, j, k: (i, k))
        b_spec = pl.BlockSpec((tk, tn), lambda i, j, k: (k, j))
    else:
        a_spec = pl.BlockSpec((tm, tk), lambda i, j, k: (i, k))
        b_spec = pl.BlockSpec((tn, tk), lambda i, j, k: (j, k))
    in_specs = [a_spec, b_spec]
    operands = [a, b]
    if bias is not None:
        in_specs.append(pl.BlockSpec((1, tn), lambda i, j, k: (0, j)))
        operands.append(bias)
    has_bias = bias is not None

    def body(*refs):
        a_ref, b_ref = refs[0], refs[1]
        bias_ref = refs[2] if has_bias else None
        o_ref = refs[3] if has_bias else refs[2]
        prod = lax.dot_general(a_ref[...].astype(BF16), b_ref[...].astype(BF16), dims,
                               preferred_element_type=F32)
        if nk == 1:
            if has_bias:
                prod = prod + bias_ref[...]
            o_ref[...] = prod.astype(o_ref.dtype)
        else:
            acc_ref = refs[-1]
            k = pl.program_id(2)

            @pl.when(k == 0)
            def _():
                acc_ref[...] = jnp.zeros_like(acc_ref)

            acc_ref[...] += prod

            @pl.when(k == nk - 1)
            def _():
                out = acc_ref[...]
                if has_bias:
                    out = out + bias_ref[...]
                o_ref[...] = out.astype(o_ref.dtype)

    return pl.pallas_call(
        body, name=name,
        grid=(M // tm, N // tn, nk),
        in_specs=in_specs,
        out_specs=pl.BlockSpec((tm, tn), lambda i, j, k: (i, j)),
        out_shape=jax.ShapeDtypeStruct((M, N), out_dtype),
        scratch_shapes=[pltpu.VMEM((tm, tn), F32)] if nk > 1 else [],
        compiler_params=_params(("parallel", "parallel", "arbitrary")),
    )(*operands)


def rowwise(fn, rows, vecs, outs, sums, name, tm=None):
    norm = [(r, r.shape[1], 0) if not isinstance(r, tuple) else r for r in rows]
    S = norm[0][0].shape[0]
    if tm is None:
        per_row = sum(w * r.dtype.itemsize for r, w, _ in norm) + sum(n * jnp.dtype(dt).itemsize for n, dt in outs)
        tm = S
        for t in (512, 256, 128, 64, 32, 16, 8):
            if S % t == 0:
                tm = t
                if t * per_row <= ROW_TILE_BUDGET:
                    break
    n_rows, n_vecs, n_outs, n_sums = len(norm), len(vecs), len(outs), len(sums)
    in_specs = [pl.BlockSpec((tm, w), lambda i, cb=cb: (i, cb)) for _, w, cb in norm]
    in_specs += [pl.BlockSpec(v.shape, lambda i: (0, 0)) for v in vecs]
    out_specs = [pl.BlockSpec((tm, n), lambda i: (i, 0)) for n, _ in outs]
    out_specs += [pl.BlockSpec((1, n), lambda i: (0, 0)) for n in sums]
    out_shape = [jax.ShapeDtypeStruct((S, n), dt) for n, dt in outs]
    out_shape += [jax.ShapeDtypeStruct((1, n), F32) for n in sums]

    def body(*refs):
        ins = [r[...] for r in refs[:n_rows + n_vecs]]
        res = fn(*ins)
        if not isinstance(res, (tuple, list)):
            res = (res,)
        out_refs = refs[n_rows + n_vecs:]
        for o_ref, val in zip(out_refs[:n_outs], res[:n_outs]):
            o_ref[...] = val.astype(o_ref.dtype)
        if n_sums:
            i = pl.program_id(0)
            for s_ref, val in zip(out_refs[n_outs:], res[n_outs:]):
                part = jnp.sum(val.astype(F32), axis=0, keepdims=True)

                @pl.when(i == 0)
                def _(s_ref=s_ref, part=part):
                    s_ref[...] = part

                @pl.when(i != 0)
                def _(s_ref=s_ref, part=part):
                    s_ref[...] += part

    res = pl.pallas_call(
        body, name=name,
        grid=(S // tm,),
        in_specs=in_specs, out_specs=out_specs, out_shape=out_shape,
        compiler_params=_params(("arbitrary",) if n_sums else ("parallel",)),
    )(*[r for r, _, _ in norm], *vecs)
    return res


def _sigmoid(x):
    return jax.nn.sigmoid(x)


def _rms(x):
    r = lax.rsqrt(jnp.mean(x * x, axis=-1, keepdims=True) + EPS)
    return x * r, r


def _rms_bwd(xhat, r, dxhat):
    return r * (dxhat - xhat * jnp.mean(dxhat * xhat, axis=-1, keepdims=True))


def norm_mod(h, g, sh, sc, name):
    def f(h, g, sh, sc):
        xhat, _ = _rms(h)
        return ((xhat * g) * (1 + sc) + sh).astype(BF16)
    return rowwise(f, [h], [g, sh, sc], [(h.shape[1], BF16)], [], name)[0]


def norm_mod_bwd(h, dhn, dh_out, g, sc, name):
    D = h.shape[1]
    with_res = dh_out is not None

    def f(*a):
        if with_res:
            h, dhn, dres, g, sc = a
        else:
            h, dhn, g, sc = a
        xhat, r = _rms(h)
        xn = xhat * g
        dxn = dhn * (1 + sc)
        dh = _rms_bwd(xhat, r, dxn * g)
        if with_res:
            dh = dh + dres
        return dh, dhn, dhn * xn, dxn * xhat

    rows = [h, dhn] + ([dh_out] if with_res else [])
    return rowwise(f, rows, [g, sc], [(D, F32)], [D, D, D], name)


def residual(h, y, gate, coef, name, bias=None):
    D = h.shape[1]
    if bias is None:
        def f(h, y, gate):
            return h + (coef * gate) * y
        return rowwise(f, [h, y], [gate], [(D, F32)], [], name)[0], y

    def fb(h, y, gate, bias):
        yb = y + bias
        return h + (coef * gate) * yb, yb
    return rowwise(fb, [h, y], [gate, bias], [(D, F32), (D, F32)], [], name)


def residual_bwd(dh_out, y, gate, coef, name, with_bias_sum=False):
    D = y.shape[1]

    def f(dh, y, gate):
        dy = (coef * gate) * dh
        res = (dy.astype(BF16), coef * dh * y)
        return res + ((dy,) if with_bias_sum else ())
    return rowwise(f, [dh_out, y], [gate], [(D, BF16)], [D, D] if with_bias_sum else [D], name)


def ffn_fwd(h, g, sh, sc, gate, w13, w2):
    F = w2.shape[0]
    hn = norm_mod(h, g, sh, sc, "ffn_norm_mod")
    ab = mm(hn, w13, "nn", "ffn_w13")

    def act(a, b):
        return ((a * _sigmoid(a)) * b).astype(BF16)
    t = rowwise(act, [(ab, F, 0), (ab, F, 1)], [], [(F, BF16)], [], "ffn_act")[0]
    y = mm(t, w2, "nn", "ffn_w2")
    h_out, _ = residual(h, y, gate, 0.5, "ffn_residual")
    return h_out, (h, hn, ab, y)


def ffn_bwd(dh_out, saved, g, sc, gate, w13, w2):
    h, hn, ab, y = saved
    F = w2.shape[0]
    dy, d_gate = residual_bwd(dh_out, y, gate, 0.5, "ffn_residual_bwd")
    dt = mm(dy, w2, "nt", "ffn_w2_dx")

    def act_bwd(a, b, dt):
        sig = _sigmoid(a)
        sa = a * sig
        da = dt * b * (sig * (1 + a * (1 - sig)))
        db = dt * sa
        return (sa * b).astype(BF16), jnp.concatenate([da, db], axis=1).astype(BF16)
    t, dab = rowwise(act_bwd, [(ab, F, 0), (ab, F, 1), dt], [], [(F, BF16), (2 * F, BF16)], [], "ffn_act_bwd")
    dw2 = mm(t, dy, "tn", "ffn_w2_dw")
    dw13 = mm(hn, dab, "tn", "ffn_w13_dw")
    dhn = mm(dab, w13, "nt", "ffn_w13_dx")
    dh_in, d_sh, d_sc, d_g = norm_mod_bwd(h, dhn, dh_out, g, sc, "norm_mod_bwd")
    return dh_in, (d_sh, d_sc, d_gate, d_g), dw13, dw2


def _shifted(xbuf, n):
    return [xbuf] + [pltpu.roll(xbuf, n - b, 0) for b in range(1, 8)]


def conv_fwd(u, w_dw, b_dw, ln_g, ln_b):
    S, D = u.shape
    tm = _tile(S, (256, 128))
    rc = 32
    first_tap = CONV_HALO - (CONV_WIDTH - 1)
    w = jnp.concatenate([w_dw, jnp.zeros((CONV_HALO - CONV_WIDTH, D), F32)], axis=0)

    def body(cur_ref, prev_ref, w_ref, b_ref, g_ref, beta_ref, z_ref, s_ref):
        i = pl.program_id(0)
        prev = jnp.where(i == 0, jnp.zeros((CONV_HALO, D), F32), prev_ref[...])
        xs = _shifted(jnp.concatenate([prev, cur_ref[...]], axis=0), tm + CONV_HALO)
        for c0 in range(0, tm, rc):
            acc = jnp.zeros((rc, D), F32)
            for k in range(CONV_WIDTH):
                off = first_tap + k
                a8, b = off // 8 * 8, off % 8
                acc = acc + w_ref[k:k + 1, :] * xs[b][c0 + a8:c0 + a8 + rc, :]
            z_ref[c0:c0 + rc, :] = acc + b_ref[...]
        z = z_ref[...]
        mu = jnp.mean(z, axis=-1, keepdims=True)
        zc = z - mu
        r = lax.rsqrt(jnp.mean(zc * zc, axis=-1, keepdims=True) + EPS)
        un = zc * r * g_ref[...] + beta_ref[...]
        s_ref[...] = (un * _sigmoid(un)).astype(BF16)

    nb = tm // CONV_HALO
    vec = pl.BlockSpec((1, D), lambda i: (0, 0))
    return pl.pallas_call(
        body, name="conv_fwd",
        grid=(S // tm,),
        in_specs=[pl.BlockSpec((tm, D), lambda i: (i, 0)),
                  pl.BlockSpec((CONV_HALO, D), lambda i: (jnp.maximum(i * nb - 1, 0), 0)),
                  pl.BlockSpec((CONV_HALO, D), lambda i: (0, 0)), vec, vec, vec],
        out_specs=[pl.BlockSpec((tm, D), lambda i: (i, 0)), pl.BlockSpec((tm, D), lambda i: (i, 0))],
        out_shape=[jax.ShapeDtypeStruct((S, D), F32), jax.ShapeDtypeStruct((S, D), BF16)],
        compiler_params=_params(("parallel",)),
    )(u, u, w, b_dw, ln_g, ln_b)


def conv_bwd(dz, u, w_dw):
    S, D = u.shape
    tm = _tile(S, (256, 128))
    rc = 32
    first_tap = CONV_HALO - (CONV_WIDTH - 1)
    w = jnp.concatenate([w_dw, jnp.zeros((CONV_HALO - CONV_WIDTH, D), F32)], axis=0)
    n_tiles = S // tm
    nb = tm // CONV_HALO

    def body(dz_ref, dzn_ref, u_ref, up_ref, w_ref, du_ref, dw_ref):
        i = pl.program_id(0)
        nxt = jnp.where(i == n_tiles - 1, jnp.zeros((CONV_HALO, D), F32), dzn_ref[...])
        dzs = _shifted(jnp.concatenate([dz_ref[...], nxt], axis=0), tm + CONV_HALO)
        for c0 in range(0, tm, rc):
            acc = jnp.zeros((rc, D), F32)
            for m in range(CONV_WIDTH):
                a8, b = m // 8 * 8, m % 8
                acc = acc + w_ref[CONV_WIDTH - 1 - m:CONV_WIDTH - m, :] * dzs[b][c0 + a8:c0 + a8 + rc, :]
            du_ref[c0:c0 + rc, :] = acc
        prev = jnp.where(i == 0, jnp.zeros((CONV_HALO, D), F32), up_ref[...])
        us = _shifted(jnp.concatenate([prev, u_ref[...]], axis=0), tm + CONV_HALO)
        dz = dz_ref[...]

        @pl.when(i == 0)
        def _():
            dw_ref[...] = jnp.zeros_like(dw_ref)

        for k in range(CONV_WIDTH):
            off = first_tap + k
            a8, b = off // 8 * 8, off % 8
            dw_ref[k:k + 1, :] += jnp.sum(dz * us[b][a8:a8 + tm, :], axis=0, keepdims=True)

    last_blk = S // CONV_HALO - 1
    du, dw = pl.pallas_call(
        body, name="conv_bwd",
        grid=(n_tiles,),
        in_specs=[pl.BlockSpec((tm, D), lambda i: (i, 0)),
                  pl.BlockSpec((CONV_HALO, D), lambda i: (jnp.minimum((i + 1) * nb, last_blk), 0)),
                  pl.BlockSpec((tm, D), lambda i: (i, 0)),
                  pl.BlockSpec((CONV_HALO, D), lambda i: (jnp.maximum(i * nb - 1, 0), 0)),
                  pl.BlockSpec((CONV_HALO, D), lambda i: (0, 0))],
        out_specs=[pl.BlockSpec((tm, D), lambda i: (i, 0)), pl.BlockSpec((CONV_HALO, D), lambda i: (0, 0))],
        out_shape=[jax.ShapeDtypeStruct((S, D), F32), jax.ShapeDtypeStruct((CONV_HALO, D), F32)],
        compiler_params=_params(("arbitrary",)),
    )(dz, dz, u, u, w)
    return du, dw[:CONV_WIDTH]


def conv_module_fwd(h, g, sh, sc, gate, p):
    D = h.shape[1]
    hn = norm_mod(h, g, sh, sc, "conv_norm_mod")
    pre = mm(hn, p["w_pw1"], "nn", "conv_pw1")
    ba, bg = p["b_pw1"][:, :D], p["b_pw1"][:, D:]

    def glu(a, gt, ba, bg):
        return (a + ba) * _sigmoid(gt + bg)
    u = rowwise(glu, [(pre, D, 0), (pre, D, 1)], [ba, bg], [(D, F32)], [], "conv_glu")[0]
    z, s = conv_fwd(u, p["w_dw"], p["b_dw"], p["ln_g"], p["ln_b"])
    yraw = mm(s, p["w_pw2"], "nn", "conv_pw2")
    h_out, y = residual(h, yraw, gate, 1.0, "conv_residual", bias=p["b_pw2"])
    return h_out, (h, hn, pre, u, z, s, y)


def conv_module_bwd(dh_out, saved, g, sc, gate, p):
    h, hn, pre, u, z, s, y = saved
    D = h.shape[1]
    dy, d_gate, d_b_pw2 = residual_bwd(dh_out, y, gate, 1.0, "conv_residual_bwd", with_bias_sum=True)
    d_w_pw2 = mm(s, dy, "tn", "conv_pw2_dw")
    ds = mm(dy, p["w_pw2"], "nt", "conv_pw2_dx")

    def ln_bwd(z, ds, g, beta):
        mu = jnp.mean(z, axis=-1, keepdims=True)
        zc = z - mu
        r = lax.rsqrt(jnp.mean(zc * zc, axis=-1, keepdims=True) + EPS)
        xhat = zc * r
        un = xhat * g + beta
        sig = _sigmoid(un)
        d_un = ds * (sig * (1 + un * (1 - sig)))
        dxhat = d_un * g
        dz = r * (dxhat - jnp.mean(dxhat, axis=-1, keepdims=True)
                  - xhat * jnp.mean(dxhat * xhat, axis=-1, keepdims=True))
        return dz, d_un * xhat, d_un, dz
    dz, d_ln_g, d_ln_b, d_b_dw = rowwise(ln_bwd, [z, ds], [p["ln_g"], p["ln_b"]], [(D, F32)], [D, D, D],
                                         "conv_ln_bwd")
    du, d_w_dw = conv_bwd(dz, u, p["w_dw"])
    ba, bg = p["b_pw1"][:, :D], p["b_pw1"][:, D:]

    def glu_bwd(a, gt, du, ba, bg):
        sg = _sigmoid(gt + bg)
        da = du * sg
        dg = du * (a + ba) * (sg * (1 - sg))
        dpre = jnp.concatenate([da, dg], axis=1)
        return dpre.astype(BF16), dpre
    dpre, d_b_pw1 = rowwise(glu_bwd, [(pre, D, 0), (pre, D, 1), du], [ba, bg], [(2 * D, BF16)], [2 * D],
                            "conv_glu_bwd")
    d_w_pw1 = mm(hn, dpre, "tn", "conv_pw1_dw")
    dhn = mm(dpre, p["w_pw1"], "nt", "conv_pw1_dx")
    dh_in, d_sh, d_sc, d_g = norm_mod_bwd(h, dhn, dh_out, g, sc, "norm_mod_bwd")
    grads = dict(w_pw1=d_w_pw1, b_pw1=d_b_pw1, w_dw=d_w_dw, b_dw=d_b_dw, ln_g=d_ln_g, ln_b=d_ln_b,
                 w_pw2=d_w_pw2, b_pw2=d_b_pw2)
    return dh_in, (d_sh, d_sc, d_gate, d_g), grads


def _rope(x, c, s1, s2):
    n = x.shape[1]
    return x * c + pltpu.roll(x, n - QK_ROPE // 2, 1) * s1 + pltpu.roll(x, QK_ROPE // 2, 1) * s2


def _rope_t(dy, c, s1, s2):
    n = dy.shape[1]
    return dy * c + pltpu.roll(dy * s1, QK_ROPE // 2, 1) + pltpu.roll(dy * s2, n - QK_ROPE // 2, 1)


def rope_tables(positions):
    inv_freq = ROPE_THETA ** (-jnp.arange(0, QK_ROPE, 2, dtype=F32) / QK_ROPE)
    ang = positions.astype(F32)[:, None] * inv_freq
    cos, sin = jnp.cos(ang), jnp.sin(ang)
    S = positions.shape[0]
    one = jnp.ones((S, QK_NOPE), F32)
    z16 = jnp.zeros((S, QK_ROPE // 2), F32)
    zn = jnp.zeros((S, QK_NOPE), F32)
    zt = jnp.zeros((S, HEAD_PAD - QK_NOPE - QK_ROPE), F32)
    c = jnp.concatenate([one, cos, cos, zt], axis=1)
    s1 = jnp.concatenate([zn, -sin, z16, zt], axis=1)
    s2 = jnp.concatenate([zn, z16, sin, zt], axis=1)
    return c, s1, s2


def attn_fwd(qr, kv, kpe, n_heads):
    S = qr.shape[0]
    H = n_heads
    tq = _tile(S, (512, 256, 128))
    nq = S // tq
    c2 = (QK_NOPE + QK_ROPE) ** -0.5 * LOG2_E
    nt = (((1,), (1,)), ((), ()))
    tn = (((0,), (0,)), ((), ()))

    def lanes_to_rows(row):
        return jnp.transpose(jnp.broadcast_to(row, (HEAD_PAD, tq)))

    def body(q_ref, k_ref, v_ref, kpe_ref, o_ref, lse_ref, kf_ref, m_ref, l_ref, acc_ref):
        qi = pl.program_id(1)

        @pl.when(qi == 0)
        def _():
            kf_ref[...] = k_ref[...] + kpe_ref[...]

        q = q_ref[...]
        m_ref[...] = jnp.full((1, tq), -jnp.inf, F32)
        l_ref[...] = jnp.zeros((1, tq), F32)
        acc_ref[...] = jnp.zeros((tq, HEAD_PAD), F32)

        def tile(j, masked):
            start = pl.multiple_of(j * tq, tq)
            k = kf_ref[pl.ds(start, tq), :]
            v = v_ref[pl.ds(start, tq), :]
            t = lax.dot_general(k, q, nt, preferred_element_type=F32) * c2
            if masked:
                krow = lax.broadcasted_iota(jnp.int32, (tq, tq), 0)
                qcol = lax.broadcasted_iota(jnp.int32, (tq, tq), 1)
                t = jnp.where(krow <= qcol, t, NEG)
            m_old = m_ref[...]
            m_new = jnp.maximum(m_old, jnp.max(t, axis=0, keepdims=True))
            alpha = jnp.exp2(m_old - m_new)
            p = jnp.exp2(t - m_new)
            l_ref[...] = alpha * l_ref[...] + jnp.sum(p, axis=0, keepdims=True)
            pv = lax.dot_general(p.astype(BF16), v, tn, preferred_element_type=F32)
            acc_ref[...] = lanes_to_rows(alpha) * acc_ref[...] + pv
            m_ref[...] = m_new

        def unmasked(j, carry):
            tile(j, False)
            return carry

        lax.fori_loop(0, qi, unmasked, 0)
        tile(qi, True)
        l = l_ref[...]
        o_ref[...] = acc_ref[...] / lanes_to_rows(l)
        lse_ref[...] = m_ref[...] + jnp.log(l) * LOG2_E

    return pl.pallas_call(
        body, name="attn_fwd",
        grid=(H, nq),
        in_specs=[pl.BlockSpec((tq, HEAD_PAD), lambda h, i: (i, h)),
                  pl.BlockSpec((S, HEAD_PAD), lambda h, i: (0, h)),
                  pl.BlockSpec((S, HEAD_PAD), lambda h, i: (0, H + h)),
                  pl.BlockSpec((S, HEAD_PAD), lambda h, i: (0, 0))],
        out_specs=[pl.BlockSpec((tq, HEAD_PAD), lambda h, i: (i, h)),
                   pl.BlockSpec((None, None, 1, tq), lambda h, i: (h, i, 0, 0))],
        out_shape=[jax.ShapeDtypeStruct((S, H * HEAD_PAD), F32), jax.ShapeDtypeStruct((H, nq, 1, tq), F32)],
        scratch_shapes=[pltpu.VMEM((S, HEAD_PAD), BF16), pltpu.VMEM((1, tq), F32), pltpu.VMEM((1, tq), F32),
                        pltpu.VMEM((tq, HEAD_PAD), F32)],
        compiler_params=_params(("parallel", "arbitrary")),
    )(qr, kv, kv, kpe)


def attn_delta(o, do, n_heads):
    S = o.shape[0]
    H = n_heads
    tq = _tile(S, (512, 256, 128))

    def body(o_ref, do_ref, d_ref):
        d_ref[...] = jnp.sum(o_ref[...] * do_ref[...].astype(F32), axis=1, keepdims=True)

    return pl.pallas_call(
        body, name="attn_delta",
        grid=(H, S // tq),
        in_specs=[pl.BlockSpec((tq, HEAD_PAD), lambda h, i: (i, h)),
                  pl.BlockSpec((tq, HEAD_PAD), lambda h, i: (i, h))],
        out_specs=pl.BlockSpec((None, tq, 1), lambda h, i: (h, i, 0)),
        out_shape=jax.ShapeDtypeStruct((H, S, 1), F32),
        compiler_params=_params(("parallel", "parallel")),
    )(o, do)


def attn_bwd(qr, kv, kpe, do, lse2, delta, n_heads):
    S = qr.shape[0]
    H = n_heads
    tq = _tile(S, (512, 256, 128))
    nq = S // tq
    scale = (QK_NOPE + QK_ROPE) ** -0.5
    c2 = scale * LOG2_E
    nt = (((1,), (1,)), ((), ()))
    tn = (((0,), (0,)), ((), ()))
    delta4 = delta.reshape(H, nq, 1, tq)

    def body(k_ref, v_ref, kpe_ref, q_ref, do_ref, lse_ref, dl_ref, dq_ref, dk_ref, dv_ref, dka_ref, dva_ref):
        kj = pl.program_id(1)
        k = k_ref[...] + kpe_ref[...]
        v = v_ref[...]

        @pl.when(kj == 0)
        def _():
            dq_ref[...] = jnp.zeros_like(dq_ref)

        dka_ref[...] = jnp.zeros_like(dka_ref)
        dva_ref[...] = jnp.zeros_like(dva_ref)

        def tile(i, masked):
            start = pl.multiple_of(i * tq, tq)
            q = q_ref[pl.ds(start, tq), :]
            do = do_ref[pl.ds(start, tq), :]
            t = lax.dot_general(k, q, nt, preferred_element_type=F32) * c2
            if masked:
                krow = lax.broadcasted_iota(jnp.int32, (tq, tq), 0)
                qcol = lax.broadcasted_iota(jnp.int32, (tq, tq), 1)
                t = jnp.where(krow <= qcol, t, NEG)
            pt = jnp.exp2(t - lse_ref[i])
            dva_ref[...] += jnp.dot(pt.astype(BF16), do, preferred_element_type=F32)
            dpt = lax.dot_general(v, do, nt, preferred_element_type=F32)
            dst = (pt * (dpt - dl_ref[i]) * scale).astype(BF16)
            dka_ref[...] += jnp.dot(dst, q, preferred_element_type=F32)
            dq_ref[pl.ds(start, tq), :] += lax.dot_general(dst, k, tn, preferred_element_type=F32)

        tile(kj, True)

        def unmasked(i, carry):
            tile(i, False)
            return carry

        lax.fori_loop(kj + 1, nq, unmasked, 0)
        dk_ref[...] = dka_ref[...]
        dv_ref[...] = dva_ref[...]

    blk = pl.BlockSpec((tq, HEAD_PAD), lambda h, j: (j, h))
    whole = pl.BlockSpec((S, HEAD_PAD), lambda h, j: (0, h))
    stat = pl.BlockSpec((None, nq, 1, tq), lambda h, j: (h, 0, 0, 0))
    shp = jax.ShapeDtypeStruct((S, H * HEAD_PAD), F32)
    return pl.pallas_call(
        body, name="attn_bwd",
        grid=(H, nq),
        in_specs=[blk, pl.BlockSpec((tq, HEAD_PAD), lambda h, j: (j, H + h)),
                  pl.BlockSpec((tq, HEAD_PAD), lambda h, j: (j, 0)), whole, whole, stat, stat],
        out_specs=[whole, blk, blk],
        out_shape=[shp, shp, shp],
        scratch_shapes=[pltpu.VMEM((tq, HEAD_PAD), F32), pltpu.VMEM((tq, HEAD_PAD), F32)],
        compiler_params=_params(("parallel", "arbitrary")),
    )(kv, kv, kpe, qr, do, lse2, delta4)


def _pad_heads(w, width):
    R = w.shape[0]
    w3 = w.reshape(R, -1, width)
    return jnp.pad(w3, ((0, 0), (0, 0), (0, HEAD_PAD - width))).reshape(R, -1)


def _unpad_heads(w, width):
    R = w.shape[0]
    return w.reshape(R, -1, HEAD_PAD)[:, :, :width].reshape(R, -1)


def mla_pad_weights(p):
    H = N_HEADS
    w_q_b = _pad_heads(p["w_q_b"], QK_NOPE + QK_ROPE)
    kvb = p["w_kv_b"].reshape(KV_LORA, H, QK_NOPE + V_HEAD)
    wk = _pad_heads(kvb[:, :, :QK_NOPE].reshape(KV_LORA, -1), QK_NOPE)
    wv = _pad_heads(kvb[:, :, QK_NOPE:].reshape(KV_LORA, -1), V_HEAD)
    D = p["w_kv_a"].shape[0]
    a = p["w_kv_a"]
    w_kv_a = jnp.concatenate([a[:, :KV_LORA], jnp.zeros((D, QK_NOPE), a.dtype), a[:, KV_LORA:],
                              jnp.zeros((D, HEAD_PAD - QK_NOPE - QK_ROPE), a.dtype)], axis=1)
    wo = p["w_o"].reshape(H, V_HEAD, -1)
    w_o = jnp.pad(wo, ((0, 0), (0, HEAD_PAD - V_HEAD), (0, 0))).reshape(H * HEAD_PAD, -1)
    return dict(w_q_a=p["w_q_a"], w_q_b=w_q_b, w_kv_b=jnp.concatenate([wk, wv], axis=1), w_kv_a=w_kv_a, w_o=w_o)


def mla_kv_fwd(h, g, sh, sc, kv_a_norm_g, pw, tabs):
    hkv = norm_mod(h, g, sh, sc, "kv_norm_mod")
    ckvp = mm(hkv, pw["w_kv_a"], "nn", "kv_a")

    def f(ckv, kpe, c, s1, s2, g):
        xhat, _ = _rms(ckv)
        return (xhat * g).astype(BF16), _rope(kpe, c, s1, s2).astype(BF16)
    ckv_n, kpe_r = rowwise(f, [(ckvp, KV_LORA, 0), (ckvp, HEAD_PAD, KV_LORA // HEAD_PAD), *tabs], [kv_a_norm_g],
                           [(KV_LORA, BF16), (HEAD_PAD, BF16)], [], "kv_a_norm_rope")
    kv = mm(ckv_n, pw["w_kv_b"], "nn", "kv_b", out_dtype=BF16)
    return kv, kpe_r, (h, hkv, ckvp, ckv_n)


def mla_kv_bwd(dk, dv, saved, g, sc, kv_a_norm_g, pw, tabs):
    h, hkv, ckvp, ckv_n = saved
    H = N_HEADS
    lane = jnp.arange(HEAD_PAD)
    pe_mask = ((lane >= QK_NOPE) & (lane < QK_NOPE + QK_ROPE)).astype(F32)[None, :]

    def f(dk, dv, c, s1, s2, mask):
        tot = dk[:, :HEAD_PAD]
        for hh in range(1, H):
            tot = tot + dk[:, hh * HEAD_PAD:(hh + 1) * HEAD_PAD]
        dkpe = _rope_t(tot * mask, c, s1, s2) * mask
        return jnp.concatenate([dk, dv], axis=1).astype(BF16), dkpe
    dkv, dkpe = rowwise(f, [dk, dv, *tabs], [pe_mask], [(2 * H * HEAD_PAD, BF16), (HEAD_PAD, F32)], [],
                        "kv_split_bwd")
    d_w_kv_b = mm(ckv_n, dkv, "tn", "kv_b_dw")
    dckv_n = mm(dkv, pw["w_kv_b"], "nt", "kv_b_dx")

    def f2(ckv, dn, dkpe, g):
        xhat, r = _rms(ckv)
        dx = _rms_bwd(xhat, r, dn * g)
        return jnp.concatenate([dx, dkpe], axis=1).astype(BF16), dn * xhat
    dckvp, d_kv_a_g = rowwise(f2, [(ckvp, KV_LORA, 0), dckv_n, dkpe], [kv_a_norm_g],
                              [(KV_LORA + HEAD_PAD, BF16)], [KV_LORA], "kv_a_norm_bwd")
    d_w_kv_a = mm(hkv, dckvp, "tn", "kv_a_dw")
    dhkv = mm(dckvp, pw["w_kv_a"], "nt", "kv_a_dx")
    dh, d_sh, d_sc, d_g = norm_mod_bwd(h, dhkv, None, g, sc, "norm_mod_bwd_nores")
    return dh, (d_sh, d_sc, d_g), d_kv_a_g, d_w_kv_a, d_w_kv_b


def mla_fwd(h, g, sh, sc, gate, q_a_norm_g, pw, kv, kpe_r, tabs):
    H = N_HEADS
    hn = norm_mod(h, g, sh, sc, "mla_norm_mod")
    qa = mm(hn, pw["w_q_a"], "nn", "q_a")

    def f(qa, g):
        xhat, _ = _rms(qa)
        return (xhat * g).astype(BF16)
    qa_n = rowwise(f, [qa], [q_a_norm_g], [(qa.shape[1], BF16)], [], "q_a_norm")[0]
    qp = mm(qa_n, pw["w_q_b"], "nn", "q_b")

    def frope(q, c, s1, s2):
        return jnp.concatenate([_rope(q[:, hh * HEAD_PAD:(hh + 1) * HEAD_PAD], c, s1, s2) for hh in range(H)],
                               axis=1).astype(BF16)
    qr = rowwise(frope, [qp, *tabs], [], [(H * HEAD_PAD, BF16)], [], "q_rope")[0]
    o, lse = attn_fwd(qr, kv, kpe_r, H)
    y = mm(o, pw["w_o"], "nn", "w_o")
    h_out, _ = residual(h, y, gate, 1.0, "mla_residual")
    return h_out, (h, hn, qa, qa_n, qr, o, lse, y)


def mla_bwd(dh_out, saved, g, sc, gate, q_a_norm_g, pw, kv, kpe_r, tabs):
    h, hn, qa, qa_n, qr, o, lse, y = saved
    H = N_HEADS
    dy, d_gate = residual_bwd(dh_out, y, gate, 1.0, "mla_residual_bwd")
    d_w_o = mm(o, dy, "tn", "w_o_dw")
    do = mm(dy, pw["w_o"], "nt", "w_o_dx", out_dtype=BF16)
    delta = attn_delta(o, do, H)
    dqr, dk, dv = attn_bwd(qr, kv, kpe_r, do, lse, delta, H)

    def frope_t(dq, c, s1, s2):
        return jnp.concatenate([_rope_t(dq[:, hh * HEAD_PAD:(hh + 1) * HEAD_PAD], c, s1, s2) for hh in range(H)],
                               axis=1).astype(BF16)
    dqp = rowwise(frope_t, [dqr, *tabs], [], [(H * HEAD_PAD, BF16)], [], "q_rope_bwd")[0]
    d_w_q_b = mm(qa_n, dqp, "tn", "q_b_dw")
    dqa_n = mm(dqp, pw["w_q_b"], "nt", "q_b_dx")

    def f(qa, dn, g):
        xhat, r = _rms(qa)
        return _rms_bwd(xhat, r, dn * g).astype(BF16), dn * xhat
    dqa, d_q_a_g = rowwise(f, [qa, dqa_n], [q_a_norm_g], [(qa.shape[1], BF16)], [qa.shape[1]], "q_a_norm_bwd")
    d_w_q_a = mm(hn, dqa, "tn", "q_a_dw")
    dhn = mm(dqa, pw["w_q_a"], "nt", "q_a_dx")
    dh_in, d_sh, d_sc, d_g = norm_mod_bwd(h, dhn, dh_out, g, sc, "norm_mod_bwd")
    grads = dict(w_q_a=d_w_q_a, q_a_norm_g=d_q_a_g, w_q_b=d_w_q_b, w_o=d_w_o)
    return dh_in, (d_sh, d_sc, d_gate, d_g), grads, dk, dv


def loss_head(h, target, g):
    D = h.shape[1]

    def f(h, t, g):
        xhat, r = _rms(h)
        err = xhat * g - t
        dy = err * (1.0 / D)
        dh = _rms_bwd(xhat, r, dy * g)
        return dh, (0.5 / D) * err * err, dy * xhat
    return rowwise(f, [h, target], [g], [(D, F32)], [D, D], "loss_head")


def _place():
    x, y, c = lax.axis_index("x"), lax.axis_index("y"), lax.axis_index("c")
    chips = [(1 - x, y), (x, 1 - y), (1 - x, 1 - y)]
    return x, y, c, chips


HBM_SPEC = pl.BlockSpec(memory_space=pltpu.HBM)


def all_gather8(v):
    m, n = v.shape

    def body(x_ref, out_ref, send_sems, recv_sems, local_sem):
        x, y, c, chips = _place()
        me, sibling = (x, y, c), (x, y, 1 - c)

        def rows(px, py, pc):
            return out_ref.at[4 * px + 2 * py + pc]

        def copy(k, block, to, src=None):
            return pltpu.make_async_remote_copy(
                src_ref=rows(*block) if src is None else src, dst_ref=rows(*block),
                send_sem=send_sems.at[k], recv_sem=recv_sems.at[k], device_id=to, device_id_type=MESH)

        mine = pltpu.make_async_copy(x_ref, rows(*me), local_sem)
        mine.start()
        first = [copy(0, me, sibling, src=x_ref)]
        first += [copy(1 + j, me, (*chip, c), src=x_ref) for j, chip in enumerate(chips)]
        for cp in first:
            cp.start()
        passed = [copy(4 + j, (*chip, c), sibling) for j, chip in enumerate(chips)]
        for j, chip in enumerate(chips):
            copy(1 + j, (*chip, c), me).wait_recv()
            passed[j].start()
        copy(0, sibling, me).wait_recv()
        for j, chip in enumerate(chips):
            copy(4 + j, (*chip, 1 - c), me).wait_recv()
        for cp in first + passed:
            cp.wait_send()
        mine.wait()

    return pl.pallas_call(
        body, name="all_gather8",
        out_shape=jax.ShapeDtypeStruct((8, m, n), v.dtype),
        in_specs=[pl.BlockSpec(memory_space=pltpu.VMEM)],
        out_specs=pl.BlockSpec(memory_space=pltpu.VMEM),
        scratch_shapes=[pltpu.SemaphoreType.DMA((7,)), pltpu.SemaphoreType.DMA((7,)), pltpu.SemaphoreType.DMA],
        compiler_params=pltpu.CompilerParams(vmem_limit_bytes=VMEM_LIMIT_BYTES),
    )(v)


def gather_weights(bufs):
    n = len(bufs)

    def body(*refs):
        ins, outs = refs[:n], refs[n:2 * n]
        send_sems, recv_sems = refs[2 * n:]
        x, y, c, chips = _place()
        sibling = (x, y, 1 - c)
        me = 2 * x + y

        def idx(chip):
            return 2 * chip[0] + chip[1]

        def copy(w, k, src, dst, to):
            return pltpu.make_async_remote_copy(src_ref=src, dst_ref=dst, send_sem=send_sems.at[6 * w + k],
                                                recv_sem=recv_sems.at[6 * w + k], device_id=to, device_id_type=MESH)

        first = [copy(w, j, ins[w].at[me, c], outs[w].at[me, c], (*chip, c))
                 for w in range(n) for j, chip in enumerate(chips)]
        for cp in first:
            cp.start()
        passed = []
        for w in range(n):
            for j, chip in enumerate(chips):
                landed = outs[w].at[idx(chip), c]
                copy(w, j, landed, landed, (*chip, c)).wait_recv()
                fwd = copy(w, 3 + j, landed, landed, sibling)
                fwd.start()
                passed.append(fwd)
        for w in range(n):
            for j, chip in enumerate(chips):
                other = outs[w].at[idx(chip), 1 - c]
                copy(w, 3 + j, other, other, sibling).wait_recv()
        for cp in first + passed:
            cp.wait_send()

    return pl.pallas_call(
        body, name="gather_weights",
        out_shape=[jax.ShapeDtypeStruct(b.shape, b.dtype) for b in bufs],
        in_specs=[HBM_SPEC] * n, out_specs=[HBM_SPEC] * n,
        input_output_aliases={w: w for w in range(n)},
        scratch_shapes=[pltpu.SemaphoreType.DMA((6 * n,)), pltpu.SemaphoreType.DMA((6 * n,))],
    )(*bufs)


def exchange_halves(gs):
    n = len(gs)

    def body(*refs):
        ins, theirs = refs[:n], refs[n:2 * n]
        send_sems, recv_sems = refs[2 * n:]
        x, y, c, _ = _place()
        sends = [pltpu.make_async_remote_copy(src_ref=ins[w].at[:, 1 - c], dst_ref=theirs[w],
                                              send_sem=send_sems.at[w], recv_sem=recv_sems.at[w],
                                              device_id=(x, y, 1 - c), device_id_type=MESH) for w in range(n)]
        for cp in sends:
            cp.start()
        for cp in sends:
            cp.wait()

    return pl.pallas_call(
        body, name="exchange_halves",
        out_shape=[jax.ShapeDtypeStruct((4,) + g.shape[2:], g.dtype) for g in gs],
        in_specs=[HBM_SPEC] * n, out_specs=[HBM_SPEC] * n,
        scratch_shapes=[pltpu.SemaphoreType.DMA((n,)), pltpu.SemaphoreType.DMA((n,))],
    )(*gs)


def scatter_blocks(ps):
    n = len(ps)

    def body(*refs):
        ins, outs = refs[:n], refs[n:2 * n]
        send_sems, recv_sems = refs[2 * n:]
        x, y, c, chips = _place()
        sends = [pltpu.make_async_remote_copy(src_ref=ins[w].at[2 * chip[0] + chip[1]], dst_ref=outs[w].at[j],
                                              send_sem=send_sems.at[3 * w + j], recv_sem=recv_sems.at[3 * w + j],
                                              device_id=(*chip, c), device_id_type=MESH)
                 for w in range(n) for j, chip in enumerate(chips)]
        for cp in sends:
            cp.start()
        for cp in sends:
            cp.wait()

    return pl.pallas_call(
        body, name="scatter_blocks",
        out_shape=[jax.ShapeDtypeStruct((3,) + p.shape[1:], p.dtype) for p in ps],
        in_specs=[HBM_SPEC] * n, out_specs=[HBM_SPEC] * n,
        scratch_shapes=[pltpu.SemaphoreType.DMA((3 * n,)), pltpu.SemaphoreType.DMA((3 * n,))],
    )(*ps)


def join_halves(qs):
    n = len(qs)

    def body(*refs):
        ins, outs = refs[:n], refs[n:2 * n]
        send_sems, recv_sems = refs[2 * n:]
        x, y, c, _ = _place()
        sends = [pltpu.make_async_remote_copy(src_ref=ins[w].at[c], dst_ref=outs[w].at[c], send_sem=send_sems.at[w],
                                              recv_sem=recv_sems.at[w], device_id=(x, y, 1 - c), device_id_type=MESH)
                 for w in range(n)]
        for cp in sends:
            cp.start()
        for w in range(n):
            other = outs[w].at[1 - c]
            pltpu.make_async_remote_copy(src_ref=other, dst_ref=other, send_sem=send_sems.at[w],
                                         recv_sem=recv_sems.at[w], device_id=(x, y, 1 - c),
                                         device_id_type=MESH).wait_recv()
        for cp in sends:
            cp.wait_send()

    return pl.pallas_call(
        body, name="join_halves",
        out_shape=[jax.ShapeDtypeStruct(q.shape, q.dtype) for q in qs],
        in_specs=[HBM_SPEC] * n, out_specs=[HBM_SPEC] * n,
        input_output_aliases={w: w for w in range(n)},
        scratch_shapes=[pltpu.SemaphoreType.DMA((n,)), pltpu.SemaphoreType.DMA((n,))],
    )(*qs)


def _row_tile(R, row_bytes):
    tm = R
    for t in (512, 256, 128, 64, 32, 16, 8):
        if R % t == 0:
            tm = t
            if t * row_bytes <= ROW_TILE_BUDGET:
                break
    return tm


def sum_siblings(g, theirs, place):
    _, _, R, C = g.shape
    tm = _row_tile(R, 3 * C * 4)

    def body(place_ref, a_ref, b_ref, o_ref):
        o_ref[...] = a_ref[...] + b_ref[...]

    return pl.pallas_call(
        body, name="sum_siblings",
        grid_spec=pltpu.PrefetchScalarGridSpec(
            num_scalar_prefetch=1, grid=(4, R // tm),
            in_specs=[pl.BlockSpec((None, None, tm, C), lambda j, i, s: (j, s[1], i, 0)),
                      pl.BlockSpec((None, tm, C), lambda j, i, s: (j, i, 0))],
            out_specs=pl.BlockSpec((None, tm, C), lambda j, i, s: (j, i, 0))),
        out_shape=jax.ShapeDtypeStruct((4, R, C), F32),
        compiler_params=_params(("parallel", "parallel")),
    )(place, g, theirs)


def sum_chips(p, landed, place):
    _, R, C = p.shape
    tm = _row_tile(R, 5 * C * 4)

    def body(place_ref, p_ref, l0_ref, l1_ref, l2_ref, o_ref):
        o_ref[...] = ((p_ref[...] + l0_ref[...]) + l1_ref[...]) + l2_ref[...]

    return pl.pallas_call(
        body, name="sum_chips",
        grid_spec=pltpu.PrefetchScalarGridSpec(
            num_scalar_prefetch=1, grid=(R // tm,),
            in_specs=[pl.BlockSpec((None, tm, C), lambda i, s: (s[0], i, 0))]
            + [pl.BlockSpec((None, tm, C), lambda i, s, j=j: (j, i, 0)) for j in range(3)],
            out_specs=pl.BlockSpec((None, tm, C), lambda i, s: (s[1], i, 0))),
        out_shape=jax.ShapeDtypeStruct((2, R, C), F32),
        compiler_params=_params(("parallel",)),
    )(place, p, landed, landed, landed)


def sum_blocks(items, name):
    R, C = items[0][0].shape[1:]
    tm = R
    for t in (512, 256, 128, 64, 32, 16, 8):
        if R % t == 0:
            tm = t
            if t * C * 4 * (len(items) + 1) <= ROW_TILE_BUDGET:
                break
    n = len(items)

    def body(*refs):
        acc = refs[0][...].astype(F32)
        for r in refs[1:n]:
            acc = acc + r[...].astype(F32)
        refs[n][...] = acc

    return pl.pallas_call(
        body, name=name,
        grid=(R // tm,),
        in_specs=[pl.BlockSpec((None, tm, C), lambda i, j=j: (j, i, 0)) for _, j in items],
        out_specs=pl.BlockSpec((tm, C), lambda i: (i, 0)),
        out_shape=jax.ShapeDtypeStruct((R, C), F32),
        compiler_params=_params(("parallel",)),
    )(*[a for a, _ in items])


def reduce_scatter_grads(gs, place):
    theirs = exchange_halves(gs)
    ps = [sum_siblings(g, t, place) for g, t in zip(gs, theirs)]
    landed = scatter_blocks(ps)
    qs = [sum_chips(p, l, place) for p, l in zip(ps, landed)]
    joined = join_halves(qs)
    return [j.reshape(2 * j.shape[1], j.shape[2]) for j in joined]


def adamw(w, g, m, v):
    shape = w.shape
    C = shape[-1]
    R = w.size // C
    tm = R
    for t in (512, 256, 128, 64, 32, 16, 8):
        if R % t == 0:
            tm = t
            if t * C * 4 * 7 <= ROW_TILE_BUDGET:
                break

    def f(w, g, m, v):
        m = ADAM_B1 * m + (1.0 - ADAM_B1) * g
        v = ADAM_B2 * v + (1.0 - ADAM_B2) * (g * g)
        m_hat = m / (1.0 - ADAM_B1 ** ADAM_STEP)
        v_hat = v / (1.0 - ADAM_B2 ** ADAM_STEP)
        delta = -ADAM_LR * (m_hat / (jnp.sqrt(v_hat) + ADAM_EPS) + ADAM_WD * w)
        return delta, m, v

    d, nm, nv = rowwise(f, [a.reshape(R, C) for a in (w, g, m, v)], [], [(C, F32)] * 3, [], "adamw", tm=tm)
    return d.reshape(shape), nm.reshape(shape), nv.reshape(shape)


def _cast_into_slot(w, place):
    C = w.shape[-1]
    w2 = w.reshape(-1, C)
    R = w2.shape[0]
    tm = _row_tile(R, 6 * C)

    def body(place_ref, w_ref, o_ref):
        o_ref[...] = w_ref[...].astype(BF16)

    out = pl.pallas_call(
        body, name="cast_bf16",
        grid_spec=pltpu.PrefetchScalarGridSpec(
            num_scalar_prefetch=1, grid=(R // tm,),
            in_specs=[pl.BlockSpec((tm, C), lambda i, s: (i, 0))],
            out_specs=pl.BlockSpec((None, tm, C), lambda i, s: (s[0], i, 0))),
        out_shape=jax.ShapeDtypeStruct((4, R, C), BF16),
        compiler_params=_params(("parallel",)),
    )(place, w2)
    return out.reshape(4, 2, R // 2, C)


def _pack(vs):
    flat = jnp.concatenate([v.reshape(-1) for v in vs])
    n = flat.shape[0]
    total = -(-n // 1024) * 1024
    return jnp.pad(flat, (0, total - n)).reshape(total // 128, 128)


def _unpack(flat, like):
    out, o = [], 0
    for shp in like:
        sz = 1
        for d in shp:
            sz *= d
        out.append(flat[o:o + sz].reshape(shp))
        o += sz
    return out


def _cols_to_blocks(g, n_chips=4):
    R, N = g.shape
    C = N // n_chips
    return g.reshape(R, n_chips, C).transpose(1, 0, 2).reshape(n_chips, 2, R // 2, C)


def _rows_to_blocks(g, n_chips=4):
    R, C = g.shape
    return g.reshape(n_chips, 2, R // n_chips // 2, C)


def kernel(x, c, positions, ada_w, ada_b, norm_g, ffn_w13, ffn_w2, conv_w_pw1, conv_b_pw1, conv_w_dw, conv_b_dw, conv_ln_g, conv_ln_b, conv_w_pw2, conv_b_pw2, kv_ada_w, kv_ada_b, kv_norm_g, w_kv_a, kv_a_norm_g, w_kv_b, w_q_a, q_a_norm_g, w_q_b, w_o, final_norm_g, loss_target, m_ada_w, m_ada_b, m_norm_g, m_ffn_w13, m_ffn_w2, m_conv_w_pw1, m_conv_b_pw1, m_conv_w_dw, m_conv_b_dw, m_conv_ln_g, m_conv_ln_b, m_conv_w_pw2, m_conv_b_pw2, m_kv_ada_w, m_kv_ada_b, m_kv_norm_g, m_w_kv_a, m_kv_a_norm_g, m_w_kv_b, m_w_q_a, m_q_a_norm_g, m_w_q_b, m_w_o, m_final_norm_g, v_ada_w, v_ada_b, v_norm_g, v_ffn_w13, v_ffn_w2, v_conv_w_pw1, v_conv_b_pw1, v_conv_w_dw, v_conv_b_dw, v_conv_ln_g, v_conv_ln_b, v_conv_w_pw2, v_conv_b_pw2, v_kv_ada_w, v_kv_ada_b, v_kv_norm_g, v_w_kv_a, v_kv_a_norm_g, v_w_kv_b, v_w_q_a, v_q_a_norm_g, v_w_q_b, v_w_o, v_final_norm_g):
    S, D = x.shape[1], x.shape[2]
    H = N_HEADS
    F = ffn_w2.shape[2] * 4
    xi, yi, ci = lax.axis_index("x"), lax.axis_index("y"), lax.axis_index("c")
    chip = 2 * xi + yi
    dev = 2 * chip + ci
    place = jnp.stack([chip, ci]).astype(jnp.int32)
    h0 = x[0]
    target = loss_target[0]

    silu_c = rowwise(lambda a: a * _sigmoid(a), [c], [], [(D, F32)], [], "silu_c")[0]
    silu_all = all_gather8(silu_c.reshape(8, D // 8)).reshape(8, D)
    n_ada = ada_w.shape[2]
    n_kv = kv_ada_w.shape[1]
    ada_b_mine = lax.dynamic_slice_in_dim(ada_b, chip * n_ada, n_ada, axis=1)
    kv_b_mine = lax.dynamic_slice_in_dim(kv_ada_b, chip * n_kv, n_kv, axis=0)[None, :]
    mods = [mm(silu_all, ada_w[l], "nn", "ada_rows", bias=ada_b_mine[l:l + 1]) for l in range(2)]
    mods.append(mm(silu_all, kv_ada_w, "nn", "kv_ada_rows", bias=kv_b_mine))
    n_mod_cols = 2 * n_ada + n_kv
    mod_pack = jnp.concatenate(mods, axis=1).reshape(-1, 128)
    mod_all = all_gather8(mod_pack).reshape(8, 8, n_mod_cols)[0::2]
    mod_mine = lax.dynamic_index_in_dim(mod_all, dev, axis=1, keepdims=False)
    mod = [mod_mine[:, l * n_ada:(l + 1) * n_ada].reshape(N_MOD, D) for l in range(2)]
    kv_mod = mod_mine[:, 2 * n_ada:].reshape(2, D)
    kv_shift, kv_scale = kv_mod[0:1], kv_mod[1:2]

    def mrow(l, k):
        return mod[l][k:k + 1]

    big = dict(ffn_w13=ffn_w13, ffn_w2=ffn_w2, conv_w_pw1=conv_w_pw1, conv_w_pw2=conv_w_pw2, w_kv_a=w_kv_a,
               w_kv_b=w_kv_b, w_q_a=w_q_a, w_q_b=w_q_b, w_o=w_o)
    names = list(big)
    gathered = gather_weights([_cast_into_slot(big[k], place) for k in names])
    gw = dict(zip(names, gathered))
    small_like = [norm_g.shape, conv_b_pw1.shape, conv_w_dw.shape, conv_b_dw.shape, conv_ln_g.shape,
                  conv_ln_b.shape, conv_b_pw2.shape]
    small_pack = _pack([norm_g, conv_b_pw1, conv_w_dw, conv_b_dw, conv_ln_g, conv_ln_b, conv_b_pw2])
    small_all = all_gather8(small_pack)[0::2].reshape(4, -1)
    per_chip = [_unpack(small_all[j], small_like) for j in range(4)]
    smalls = [jnp.concatenate([per_chip[j][k] for j in range(4)], axis=-1) for k in range(len(small_like))]
    norm_g_f, b_pw1_f, w_dw_f, b_dw_f, ln_g_f, ln_b_f, b_pw2_f = smalls

    w13 = gw["ffn_w13"].reshape(4, 2, 2, D, 2 * F // 4).transpose(1, 2, 3, 0, 4).reshape(2, 2, D, 2 * F)
    w2 = gw["ffn_w2"].reshape(4, 2, 2, F // 4, D).transpose(1, 2, 0, 3, 4).reshape(2, 2, F, D)
    conv_p = dict(
        w_pw1=gw["conv_w_pw1"].reshape(4, D, 2 * D // 4).transpose(1, 0, 2).reshape(D, 2 * D),
        b_pw1=b_pw1_f, w_dw=w_dw_f[0], b_dw=b_dw_f, ln_g=ln_g_f, ln_b=ln_b_f,
        w_pw2=gw["conv_w_pw2"].reshape(D, D), b_pw2=b_pw2_f)
    q_lora = w_q_a.shape[2]
    mla_p = dict(
        w_kv_a=gw["w_kv_a"].reshape(D, KV_LORA + QK_ROPE),
        w_kv_b=gw["w_kv_b"].reshape(4, KV_LORA, -1).transpose(1, 0, 2).reshape(KV_LORA, -1),
        w_q_a=gw["w_q_a"].reshape(D, q_lora),
        w_q_b=gw["w_q_b"].reshape(4, q_lora, -1).transpose(1, 0, 2).reshape(q_lora, -1),
        w_o=gw["w_o"].reshape(H * V_HEAD, D))
    pw = mla_pad_weights(mla_p)
    tabs = rope_tables(positions[0])

    def ng(l, k):
        return norm_g_f[l, k][None, :]

    h = h0
    h, s_f1_0 = ffn_fwd(h, ng(0, 0), mrow(0, 0), mrow(0, 1), mrow(0, 2), w13[0, 0], w2[0, 0])
    h, s_conv = conv_module_fwd(h, ng(0, 1), mrow(0, 3), mrow(0, 4), mrow(0, 5), conv_p)
    h, s_f2_0 = ffn_fwd(h, ng(0, 2), mrow(0, 6), mrow(0, 7), mrow(0, 8), w13[0, 1], w2[0, 1])
    kv_norm = kv_norm_g[None, :]
    kv_a_g = kv_a_norm_g[None, :]
    kv, kpe_r, s_kv = mla_kv_fwd(h, kv_norm, kv_shift, kv_scale, kv_a_g, pw, tabs)
    h, s_f1_1 = ffn_fwd(h, ng(1, 0), mrow(1, 0), mrow(1, 1), mrow(1, 2), w13[1, 0], w2[1, 0])
    h, s_mla = mla_fwd(h, ng(1, 1), mrow(1, 3), mrow(1, 4), mrow(1, 5), q_a_norm_g, pw, kv, kpe_r, tabs)
    h, s_f2_1 = ffn_fwd(h, ng(1, 2), mrow(1, 6), mrow(1, 7), mrow(1, 8), w13[1, 1], w2[1, 1])
    dh, loss_cols, d_final_g = loss_head(h, target, final_norm_g[None, :])

    dh, v_f2_1, dw13_11, dw2_11 = ffn_bwd(dh, s_f2_1, ng(1, 2), mrow(1, 7), mrow(1, 8), w13[1, 1], w2[1, 1])
    dh, v_mla, g_mla, dk, dv = mla_bwd(dh, s_mla, ng(1, 1), mrow(1, 4), mrow(1, 5), q_a_norm_g, pw, kv, kpe_r, tabs)
    dh, v_f1_1, dw13_10, dw2_10 = ffn_bwd(dh, s_f1_1, ng(1, 0), mrow(1, 1), mrow(1, 2), w13[1, 0], w2[1, 0])
    dh_kv, v_kv, d_kv_a_g, d_w_kv_a, d_w_kv_b = mla_kv_bwd(dk, dv, s_kv, kv_norm, kv_scale, kv_a_g, pw, tabs)
    dh = rowwise(lambda a, b: a + b, [dh, dh_kv], [], [(D, F32)], [], "add_stream")[0]
    dh, v_f2_0, dw13_01, dw2_01 = ffn_bwd(dh, s_f2_0, ng(0, 2), mrow(0, 7), mrow(0, 8), w13[0, 1], w2[0, 1])
    dh, v_conv, g_conv = conv_module_bwd(dh, s_conv, ng(0, 1), mrow(0, 4), mrow(0, 5), conv_p)
    dh, v_f1_0, dw13_00, dw2_00 = ffn_bwd(dh, s_f1_0, ng(0, 0), mrow(0, 1), mrow(0, 2), w13[0, 0], w2[0, 0])
    grad_x = dh[None]

    d_w_kv_a_u = jnp.concatenate([d_w_kv_a[:, :KV_LORA], d_w_kv_a[:, KV_LORA + QK_NOPE:KV_LORA + QK_NOPE + QK_ROPE]],
                                 axis=1)
    hk = H * HEAD_PAD
    dkb = jnp.concatenate([d_w_kv_b[:, :hk].reshape(KV_LORA, H, HEAD_PAD)[:, :, :QK_NOPE],
                           d_w_kv_b[:, hk:].reshape(KV_LORA, H, HEAD_PAD)[:, :, :V_HEAD]], axis=2).reshape(KV_LORA, -1)
    d_w_q_b_u = _unpad_heads(g_mla["w_q_b"], QK_NOPE + QK_ROPE)
    d_w_o_u = g_mla["w_o"].reshape(H, HEAD_PAD, D)[:, :V_HEAD].reshape(H * V_HEAD, D)
    full = [_cols_to_blocks(dw13_00), _cols_to_blocks(dw13_01), _cols_to_blocks(dw13_10), _cols_to_blocks(dw13_11),
            _rows_to_blocks(dw2_00), _rows_to_blocks(dw2_01), _rows_to_blocks(dw2_10), _rows_to_blocks(dw2_11),
            _cols_to_blocks(g_conv["w_pw1"]), _rows_to_blocks(g_conv["w_pw2"]), _rows_to_blocks(d_w_kv_a_u),
            _cols_to_blocks(dkb), _rows_to_blocks(g_mla["w_q_a"]), _cols_to_blocks(d_w_q_b_u),
            _rows_to_blocks(d_w_o_u)]
    red = reduce_scatter_grads(full, place)
    g_ffn_w13 = jnp.stack(red[0:4]).reshape(ffn_w13.shape)
    g_ffn_w2 = jnp.stack(red[4:8]).reshape(ffn_w2.shape)
    g_conv_w_pw1 = red[8].reshape(conv_w_pw1.shape)
    g_conv_w_pw2 = red[9].reshape(conv_w_pw2.shape)
    g_w_kv_a = red[10].reshape(w_kv_a.shape)
    g_w_kv_b = red[11].reshape(w_kv_b.shape)
    g_w_q_a = red[12].reshape(w_q_a.shape)
    g_w_q_b = red[13].reshape(w_q_b.shape)
    g_w_o = red[14].reshape(w_o.shape)

    def dmod(v1, vm, v2):
        return jnp.concatenate([v1[0], v1[1], v1[2], vm[0], vm[1], vm[2], v2[0], v2[1], v2[2]], axis=1)
    d_mod0 = dmod(v_f1_0, v_conv, v_f2_0)
    d_mod1 = dmod(v_f1_1, v_mla, v_f2_1)
    d_kv_mod = jnp.concatenate([v_kv[0], v_kv[1]], axis=1)
    d_norm_g = jnp.concatenate([v_f1_0[3], v_conv[3], v_f2_0[3], v_f1_1[3], v_mla[3], v_f2_1[3]], axis=0)
    vec_list = [d_mod0, d_mod1, d_kv_mod, d_norm_g, g_conv["b_pw1"], g_conv["w_dw"], g_conv["b_dw"], g_conv["ln_g"],
                g_conv["ln_b"], g_conv["b_pw2"], v_kv[2], d_kv_a_g, g_mla["q_a_norm_g"], d_final_g, loss_cols]
    vec_like = [v.shape for v in vec_list]
    vec_pack = _pack(vec_list)
    n_mod_rows = (2 * N_MOD * D + 2 * D) // 128
    vec_all = all_gather8(vec_pack)
    vec_sum = sum_blocks([(vec_all, d) for d in range(8)], "sum_devices").reshape(-1)
    (_, _, _, s_norm_g, s_b_pw1, s_w_dw, s_b_dw, s_ln_g, s_ln_b, s_b_pw2, s_kv_norm_g, s_kv_a_g, s_q_a_g,
     s_final_g, s_loss) = _unpack(vec_sum, vec_like)
    loss = jnp.sum(s_loss)
    dmod_all = vec_all[:, :n_mod_rows].reshape(8, 2 * N_MOD * D + 2 * D)
    dmod_sum = vec_sum[:2 * N_MOD * D + 2 * D]
    g_ada_b = dmod_sum[:2 * N_MOD * D].reshape(2, N_MOD * D)
    g_kv_ada_b = dmod_sum[2 * N_MOD * D:]
    g_ada_w = []
    for l in range(2):
        cols = lax.dynamic_slice_in_dim(dmod_all[:, l * N_MOD * D:(l + 1) * N_MOD * D], chip * n_ada, n_ada, axis=1)
        g_ada_w.append(mm(silu_all, cols, "tn", "ada_w_grad"))
    g_ada_w = jnp.stack(g_ada_w)
    kv_cols = lax.dynamic_slice_in_dim(dmod_all[:, 2 * N_MOD * D:], chip * n_kv, n_kv, axis=1)
    g_kv_ada_w = mm(silu_all, kv_cols, "tn", "kv_ada_w_grad")

    def shard(v, width):
        return lax.dynamic_slice_in_dim(v, chip * width, width, axis=v.ndim - 1)

    Dq = D // 4
    g_norm_g = shard(s_norm_g.reshape(2, 3, D), Dq)
    g_conv_b_pw1 = shard(s_b_pw1, 2 * D // 4)
    g_conv_w_dw = shard(s_w_dw, Dq)[None]
    g_conv_b_dw = shard(s_b_dw, Dq)
    g_conv_ln_g = shard(s_ln_g, Dq)
    g_conv_ln_b = shard(s_ln_b, Dq)
    g_conv_b_pw2 = shard(s_b_pw2, Dq)

    grads = [g_ada_w, g_ada_b, g_norm_g, g_ffn_w13, g_ffn_w2, g_conv_w_pw1, g_conv_b_pw1, g_conv_w_dw, g_conv_b_dw,
             g_conv_ln_g, g_conv_ln_b, g_conv_w_pw2, g_conv_b_pw2, g_kv_ada_w, g_kv_ada_b, s_kv_norm_g[0], g_w_kv_a,
             s_kv_a_g[0], g_w_kv_b, g_w_q_a, s_q_a_g, g_w_q_b, g_w_o, s_final_g[0]]
    weights = [ada_w, ada_b, norm_g, ffn_w13, ffn_w2, conv_w_pw1, conv_b_pw1, conv_w_dw, conv_b_dw, conv_ln_g,
               conv_ln_b, conv_w_pw2, conv_b_pw2, kv_ada_w, kv_ada_b, kv_norm_g, w_kv_a, kv_a_norm_g, w_kv_b, w_q_a,
               q_a_norm_g, w_q_b, w_o, final_norm_g]
    ms = [m_ada_w, m_ada_b, m_norm_g, m_ffn_w13, m_ffn_w2, m_conv_w_pw1, m_conv_b_pw1, m_conv_w_dw, m_conv_b_dw,
          m_conv_ln_g, m_conv_ln_b, m_conv_w_pw2, m_conv_b_pw2, m_kv_ada_w, m_kv_ada_b, m_kv_norm_g, m_w_kv_a,
          m_kv_a_norm_g, m_w_kv_b, m_w_q_a, m_q_a_norm_g, m_w_q_b, m_w_o, m_final_norm_g]
    vs = [v_ada_w, v_ada_b, v_norm_g, v_ffn_w13, v_ffn_w2, v_conv_w_pw1, v_conv_b_pw1, v_conv_w_dw, v_conv_b_dw,
          v_conv_ln_g, v_conv_ln_b, v_conv_w_pw2, v_conv_b_pw2, v_kv_ada_w, v_kv_ada_b, v_kv_norm_g, v_w_kv_a,
          v_kv_a_norm_g, v_w_kv_b, v_w_q_a, v_q_a_norm_g, v_w_q_b, v_w_o, v_final_norm_g]
    grads = [g.reshape(w.shape) for g, w in zip(grads, weights)]
    deltas, new_m, new_v = [], [], []
    for w, g, m, v in zip(weights, grads, ms, vs):
        d, nm, nv = adamw(w, g, m, v)
        deltas.append(d)
        new_m.append(nm)
        new_v.append(nv)
    return (loss, grad_x, *grads, *deltas, *new_m, *new_v)
```

```python
import jax
import jax.numpy as jnp
from jax import lax
from jax.experimental import pallas as pl
from jax.experimental.pallas import tpu as pltpu

F32 = jnp.float32
BF16 = jnp.bfloat16
MESH = pl.DeviceIdType.MESH

N_HEADS = 16
QK_NOPE = 64
QK_ROPE = 32
V_HEAD = 64
KV_LORA = 256
CONV_WIDTH = 31
ROPE_THETA = 10000.0
EPS = 1e-6
N_MOD = 9
HEAD_PAD = 128
ATTN_TILE = 512
CONV_HALO = 32

ADAM_LR = 0.001
ADAM_B1 = 0.9
ADAM_B2 = 0.999
ADAM_EPS = 1e-08
ADAM_WD = 0.01
ADAM_STEP = 10

VMEM_LIMIT_BYTES = 56 * 2 ** 20
ROW_TILE_BUDGET = 10 * 2 ** 20
MM_VMEM_BUDGET = 40 * 2 ** 20
NEG = float(jnp.finfo(jnp.float32).min)
LOG2_E = 1.4426950408889634


def _tile(n, prefs):
    for t in prefs:
        if n % t == 0:
            return t
    return n


def _params(sem):
    return pltpu.CompilerParams(dimension_semantics=sem, vmem_limit_bytes=VMEM_LIMIT_BYTES)


def _mm_tiles(M, N, K, mode, a_bytes, b_bytes, o_bytes):
    if mode == "tn":
        tk = _tile(K, (512, 256, 128))
        tm_opts = ([M] if M <= 2816 else []) + [t for t in (1024, 512, 256, 128) if M % t == 0 and t < M]
    else:
        tk = K
        tm_opts = [t for t in (1024, 512, 256, 128) if M % t == 0] or [M]
    tn_opts = [t for t in (1408, 1024, 512, 384, 256, 128) if N % t == 0] or [N]

    def need(tm, tn):
        blocks = 2 * (tm * tk * a_bytes + tk * tn * b_bytes + tm * tn * o_bytes)
        return blocks + (tm * tn * 4 if mode == "tn" else 0)

    for tm in tm_opts:
        for tn in tn_opts:
            if need(tm, tn) <= MM_VMEM_BUDGET:
                return tm, tn, tk
    return tm_opts[-1], tn_opts[-1], tk


def mm(a, b, mode, name, out_dtype=F32, bias=None):
    if mode == "nn":
        (M, K), (K2, N) = a.shape, b.shape
        dims = (((1,), (0,)), ((), ()))
    elif mode == "nt":
        (M, K), (N, K2) = a.shape, b.shape
        dims = (((1,), (1,)), ((), ()))
    else:
        (K, M), (K2, N) = a.shape, b.shape
        dims = (((0,), (0,)), ((), ()))
    assert K == K2, (a.shape, b.shape, mode)
    tm, tn, tk = _mm_tiles(M, N, K, mode, a.dtype.itemsize, b.dtype.itemsize, jnp.dtype(out_dtype).itemsize)
    nk = K // tk
    if mode == "tn":
        a_spec = pl.BlockSpec((tk, tm), lambda i, j, k: (k, i))
        b_spec = pl.BlockSpec((tk, tn), lambda i, j, k: (k, j))
    elif mode == "nn":
        a_spec = pl.BlockSpec((tm, tk), lambda i, j, k: (i, k))
        b_spec = pl.BlockSpec((tk, tn), lambda i, j, k: (k, j))
    else:
        a_spec = pl.BlockSpec((tm, tk), lambda i, j, k: (i, k))
        b_spec = pl.BlockSpec((tn, tk), lambda i, j, k: (j, k))
    in_specs = [a_spec, b_spec]
    operands = [a, b]
    if bias is not None:
        in_specs.append(pl.BlockSpec((1, tn), lambda i, j, k: (0, j)))
        operands.append(bias)
    has_bias = bias is not None

    def body(*refs):
        a_ref, b_ref = refs[0], refs[1]
        bias_ref = refs[2] if has_bias else None
        o_ref = refs[3] if has_bias else refs[2]
        prod = lax.dot_general(a_ref[...].astype(BF16), b_ref[...].astype(BF16), dims,
                               preferred_element_type=F32)
        if nk == 1:
            if has_bias:
                prod = prod + bias_ref[...]
            o_ref[...] = prod.astype(o_ref.dtype)
        else:
            acc_ref = refs[-1]
            k = pl.program_id(2)

            @pl.when(k == 0)
            def _():
                acc_ref[...] = jnp.zeros_like(acc_ref)

            acc_ref[...] += prod

            @pl.when(k == nk - 1)
            def _():
                out = acc_ref[...]
                if has_bias:
                    out = out + bias_ref[...]
                o_ref[...] = out.astype(o_ref.dtype)

    return pl.pallas_call(
        body, name=name,
        grid=(M // tm, N // tn, nk),
        in_specs=in_specs,
        out_specs=pl.BlockSpec((tm, tn), lambda i, j, k: (i, j)),
        out_shape=jax.ShapeDtypeStruct((M, N), out_dtype),
        scratch_shapes=[pltpu.VMEM((tm, tn), F32)] if nk > 1 else [],
        compiler_params=_params(("parallel", "parallel", "arbitrary")),
    )(*operands)


def rowwise(fn, rows, vecs, outs, sums, name, tm=None):
    norm = [(r, r.shape[1], 0) if not isinstance(r, tuple) else r for r in rows]
    S = norm[0][0].shape[0]
    if tm is None:
        per_row = sum(w * r.dtype.itemsize for r, w, _ in norm) + sum(n * jnp.dtype(dt).itemsize for n, dt in outs)
        tm = S
        for t in (512, 256, 128, 64, 32, 16, 8):
            if S % t == 0:
                tm = t
                if t * per_row <= ROW_TILE_BUDGET:
                    break
    n_rows, n_vecs, n_outs, n_sums = len(norm), len(vecs), len(outs), len(sums)
    in_specs = [pl.BlockSpec((tm, w), lambda i, cb=cb: (i, cb)) for _, w, cb in norm]
    in_specs += [pl.BlockSpec(v.shape, lambda i: (0, 0)) for v in vecs]
    out_specs = [pl.BlockSpec((tm, n), lambda i: (i, 0)) for n, _ in outs]
    out_specs += [pl.BlockSpec((1, n), lambda i: (0, 0)) for n in sums]
    out_shape = [jax.ShapeDtypeStruct((S, n), dt) for n, dt in outs]
    out_shape += [jax.ShapeDtypeStruct((1, n), F32) for n in sums]

    def body(*refs):
        ins = [r[...] for r in refs[:n_rows + n_vecs]]
        res = fn(*ins)
        if not isinstance(res, (tuple, list)):
            res = (res,)
        out_refs = refs[n_rows + n_vecs:]
        for o_ref, val in zip(out_refs[:n_outs], res[:n_outs]):
            o_ref[...] = val.astype(o_ref.dtype)
        if n_sums:
            i = pl.program_id(0)
            for s_ref, val in zip(out_refs[n_outs:], res[n_outs:]):
                part = jnp.sum(val.astype(F32), axis=0, keepdims=True)

                @pl.when(i == 0)
                def _(s_ref=s_ref, part=part):
                    s_ref[...] = part

                @pl.when(i != 0)
                def _(s_ref=s_ref, part=part):
                    s_ref[...] += part

    res = pl.pallas_call(
        body, name=name,
        grid=(S // tm,),
        in_specs=in_specs, out_specs=out_specs, out_shape=out_shape,
        compiler_params=_params(("arbitrary",) if n_sums else ("parallel",)),
    )(*[r for r, _, _ in norm], *vecs)
    return res


def _sigmoid(x):
    return jax.nn.sigmoid(x)


def _rms(x):
    r = lax.rsqrt(jnp.mean(x * x, axis=-1, keepdims=True) + EPS)
    return x * r, r


def _rms_bwd(xhat, r, dxhat):
    return r * (dxhat - xhat * jnp.mean(dxhat * xhat, axis=-1, keepdims=True))


def norm_mod(h, g, sh, sc, name):
    def f(h, g, sh, sc):
        xhat, _ = _rms(h)
        return ((xhat * g) * (1 + sc) + sh).astype(BF16)
    return rowwise(f, [h], [g, sh, sc], [(h.shape[1], BF16)], [], name)[0]


def norm_mod_bwd(h, dhn, dh_out, g, sc, name):
    D = h.shape[1]
    with_res = dh_out is not None

    def f(*a):
        if with_res:
            h, dhn, dres, g, sc = a
        else:
            h, dhn, g, sc = a
        xhat, r = _rms(h)
        xn = xhat * g
        dxn = dhn * (1 + sc)
        dh = _rms_bwd(xhat, r, dxn * g)
        if with_res:
            dh = dh + dres
        return dh, dhn, dhn * xn, dxn * xhat

    rows = [h, dhn] + ([dh_out] if with_res else [])
    return rowwise(f, rows, [g, sc], [(D, F32)], [D, D, D], name)


def residual(h, y, gate, coef, name, bias=None):
    D = h.shape[1]
    if bias is None:
        def f(h, y, gate):
            return h + (coef * gate) * y
        return rowwise(f, [h, y], [gate], [(D, F32)], [], name)[0], y

    def fb(h, y, gate, bias):
        yb = y + bias
        return h + (coef * gate) * yb, yb
    return rowwise(fb, [h, y], [gate, bias], [(D, F32), (D, F32)], [], name)


def residual_bwd(dh_out, y, gate, coef, name, with_bias_sum=False):
    D = y.shape[1]

    def f(dh, y, gate):
        dy = (coef * gate) * dh
        res = (dy.astype(BF16), coef * dh * y)
        return res + ((dy,) if with_bias_sum else ())
    return rowwise(f, [dh_out, y], [gate], [(D, BF16)], [D, D] if with_bias_sum else [D], name)


def ffn_fwd(h, g, sh, sc, gate, w13, w2):
    F = w2.shape[0]
    hn = norm_mod(h, g, sh, sc, "ffn_norm_mod")
    ab = mm(hn, w13, "nn", "ffn_w13")

    def act(a, b):
        return ((a * _sigmoid(a)) * b).astype(BF16)
    t = rowwise(act, [(ab, F, 0), (ab, F, 1)], [], [(F, BF16)], [], "ffn_act")[0]
    y = mm(t, w2, "nn", "ffn_w2")
    h_out, _ = residual(h, y, gate, 0.5, "ffn_residual")
    return h_out, (h, hn, ab, y)


def ffn_bwd(dh_out, saved, g, sc, gate, w13, w2):
    h, hn, ab, y = saved
    F = w2.shape[0]
    dy, d_gate = residual_bwd(dh_out, y, gate, 0.5, "ffn_residual_bwd")
    dt = mm(dy, w2, "nt", "ffn_w2_dx")

    def act_bwd(a, b, dt):
        sig = _sigmoid(a)
        sa = a * sig
        da = dt * b * (sig * (1 + a * (1 - sig)))
        db = dt * sa
        return (sa * b).astype(BF16), jnp.concatenate([da, db], axis=1).astype(BF16)
    t, dab = rowwise(act_bwd, [(ab, F, 0), (ab, F, 1), dt], [], [(F, BF16), (2 * F, BF16)], [], "ffn_act_bwd")
    dw2 = mm(t, dy, "tn", "ffn_w2_dw")
    dw13 = mm(hn, dab, "tn", "ffn_w13_dw")
    dhn = mm(dab, w13, "nt", "ffn_w13_dx")
    dh_in, d_sh, d_sc, d_g = norm_mod_bwd(h, dhn, dh_out, g, sc, "norm_mod_bwd")
    return dh_in, (d_sh, d_sc, d_gate, d_g), dw13, dw2


def _shifted(xbuf, n):
    return [xbuf] + [pltpu.roll(xbuf, n - b, 0) for b in range(1, 8)]


def conv_fwd(u, w_dw, b_dw, ln_g, ln_b):
    S, D = u.shape
    tm = _tile(S, (256, 128))
    rc = 32
    first_tap = CONV_HALO - (CONV_WIDTH - 1)
    w = jnp.concatenate([w_dw, jnp.zeros((CONV_HALO - CONV_WIDTH, D), F32)], axis=0)

    def body(cur_ref, prev_ref, w_ref, b_ref, g_ref, beta_ref, z_ref, s_ref):
        i = pl.program_id(0)
        prev = jnp.where(i == 0, jnp.zeros((CONV_HALO, D), F32), prev_ref[...])
        xs = _shifted(jnp.concatenate([prev, cur_ref[...]], axis=0), tm + CONV_HALO)
        for c0 in range(0, tm, rc):
            acc = jnp.zeros((rc, D), F32)
            for k in range(CONV_WIDTH):
                off = first_tap + k
                a8, b = off // 8 * 8, off % 8
                acc = acc + w_ref[k:k + 1, :] * xs[b][c0 + a8:c0 + a8 + rc, :]
            z_ref[c0:c0 + rc, :] = acc + b_ref[...]
        z = z_ref[...]
        mu = jnp.mean(z, axis=-1, keepdims=True)
        zc = z - mu
        r = lax.rsqrt(jnp.mean(zc * zc, axis=-1, keepdims=True) + EPS)
        un = zc * r * g_ref[...] + beta_ref[...]
        s_ref[...] = (un * _sigmoid(un)).astype(BF16)

    nb = tm // CONV_HALO
    vec = pl.BlockSpec((1, D), lambda i: (0, 0))
    return pl.pallas_call(
        body, name="conv_fwd",
        grid=(S // tm,),
        in_specs=[pl.BlockSpec((tm, D), lambda i: (i, 0)),
                  pl.BlockSpec((CONV_HALO, D), lambda i: (jnp.maximum(i * nb - 1, 0), 0)),
                  pl.BlockSpec((CONV_HALO, D), lambda i: (0, 0)), vec, vec, vec],
        out_specs=[pl.BlockSpec((tm, D), lambda i: (i, 0)), pl.BlockSpec((tm, D), lambda i: (i, 0))],
        out_shape=[jax.ShapeDtypeStruct((S, D), F32), jax.ShapeDtypeStruct((S, D), BF16)],
        compiler_params=_params(("parallel",)),
    )(u, u, w, b_dw, ln_g, ln_b)


def conv_bwd(dz, u, w_dw):
    S, D = u.shape
    tm = _tile(S, (256, 128))
    rc = 32
    first_tap = CONV_HALO - (CONV_WIDTH - 1)
    w = jnp.concatenate([w_dw, jnp.zeros((CONV_HALO - CONV_WIDTH, D), F32)], axis=0)
    n_tiles = S // tm
    nb = tm // CONV_HALO

    def body(dz_ref, dzn_ref, u_ref, up_ref, w_ref, du_ref, dw_ref):
        i = pl.program_id(0)
        nxt = jnp.where(i == n_tiles - 1, jnp.zeros((CONV_HALO, D), F32), dzn_ref[...])
        dzs = _shifted(jnp.concatenate([dz_ref[...], nxt], axis=0), tm + CONV_HALO)
        for c0 in range(0, tm, rc):
            acc = jnp.zeros((rc, D), F32)
            for m in range(CONV_WIDTH):
                a8, b = m // 8 * 8, m % 8
                acc = acc + w_ref[CONV_WIDTH - 1 - m:CONV_WIDTH - m, :] * dzs[b][c0 + a8:c0 + a8 + rc, :]
            du_ref[c0:c0 + rc, :] = acc
        prev = jnp.where(i == 0, jnp.zeros((CONV_HALO, D), F32), up_ref[...])
        us = _shifted(jnp.concatenate([prev, u_ref[...]], axis=0), tm + CONV_HALO)
        dz = dz_ref[...]

        @pl.when(i == 0)
        def _():
            dw_ref[...] = jnp.zeros_like(dw_ref)

        for k in range(CONV_WIDTH):
            off = first_tap + k
            a8, b = off // 8 * 8, off % 8
            dw_ref[k:k + 1, :] += jnp.sum(dz * us[b][a8:a8 + tm, :], axis=0, keepdims=True)

    last_blk = S // CONV_HALO - 1
    du, dw = pl.pallas_call(
        body, name="conv_bwd",
        grid=(n_tiles,),
        in_specs=[pl.BlockSpec((tm, D), lambda i: (i, 0)),
                  pl.BlockSpec((CONV_HALO, D), lambda i: (jnp.minimum((i + 1) * nb, last_blk), 0)),
                  pl.BlockSpec((tm, D), lambda i: (i, 0)),
                  pl.BlockSpec((CONV_HALO, D), lambda i: (jnp.maximum(i * nb - 1, 0), 0)),
                  pl.BlockSpec((CONV_HALO, D), lambda i: (0, 0))],
        out_specs=[pl.BlockSpec((tm, D), lambda i: (i, 0)), pl.BlockSpec((CONV_HALO, D), lambda i: (0, 0))],
        out_shape=[jax.ShapeDtypeStruct((S, D), F32), jax.ShapeDtypeStruct((CONV_HALO, D), F32)],
        compiler_params=_params(("arbitrary",)),
    )(dz, dz, u, u, w)
    return du, dw[:CONV_WIDTH]


def conv_module_fwd(h, g, sh, sc, gate, p):
    D = h.shape[1]
    hn = norm_mod(h, g, sh, sc, "conv_norm_mod")
    pre = mm(hn, p["w_pw1"], "nn", "conv_pw1")
    ba, bg = p["b_pw1"][:, :D], p["b_pw1"][:, D:]

    def glu(a, gt, ba, bg):
        return (a + ba) * _sigmoid(gt + bg)
    u = rowwise(glu, [(pre, D, 0), (pre, D, 1)], [ba, bg], [(D, F32)], [], "conv_glu")[0]
    z, s = conv_fwd(u, p["w_dw"], p["b_dw"], p["ln_g"], p["ln_b"])
    yraw = mm(s, p["w_pw2"], "nn", "conv_pw2")
    h_out, y = residual(h, yraw, gate, 1.0, "conv_residual", bias=p["b_pw2"])
    return h_out, (h, hn, pre, u, z, s, y)


def conv_module_bwd(dh_out, saved, g, sc, gate, p):
    h, hn, pre, u, z, s, y = saved
    D = h.shape[1]
    dy, d_gate, d_b_pw2 = residual_bwd(dh_out, y, gate, 1.0, "conv_residual_bwd", with_bias_sum=True)
    d_w_pw2 = mm(s, dy, "tn", "conv_pw2_dw")
    ds = mm(dy, p["w_pw2"], "nt", "conv_pw2_dx")

    def ln_bwd(z, ds, g, beta):
        mu = jnp.mean(z, axis=-1, keepdims=True)
        zc = z - mu
        r = lax.rsqrt(jnp.mean(zc * zc, axis=-1, keepdims=True) + EPS)
        xhat = zc * r
        un = xhat * g + beta
        sig = _sigmoid(un)
        d_un = ds * (sig * (1 + un * (1 - sig)))
        dxhat = d_un * g
        dz = r * (dxhat - jnp.mean(dxhat, axis=-1, keepdims=True)
                  - xhat * jnp.mean(dxhat * xhat, axis=-1, keepdims=True))
        return dz, d_un * xhat, d_un, dz
    dz, d_ln_g, d_ln_b, d_b_dw = rowwise(ln_bwd, [z, ds], [p["ln_g"], p["ln_b"]], [(D, F32)], [D, D, D],
                                         "conv_ln_bwd")
    du, d_w_dw = conv_bwd(dz, u, p["w_dw"])
    ba, bg = p["b_pw1"][:, :D], p["b_pw1"][:, D:]

    def glu_bwd(a, gt, du, ba, bg):
        sg = _sigmoid(gt + bg)
        da = du * sg
        dg = du * (a + ba) * (sg * (1 - sg))
        dpre = jnp.concatenate([da, dg], axis=1)
        return dpre.astype(BF16), dpre
    dpre, d_b_pw1 = rowwise(glu_bwd, [(pre, D, 0), (pre, D, 1), du], [ba, bg], [(2 * D, BF16)], [2 * D],
                            "conv_glu_bwd")
    d_w_pw1 = mm(hn, dpre, "tn", "conv_pw1_dw")
    dhn = mm(dpre, p["w_pw1"], "nt", "conv_pw1_dx")
    dh_in, d_sh, d_sc, d_g = norm_mod_bwd(h, dhn, dh_out, g, sc, "norm_mod_bwd")
    grads = dict(w_pw1=d_w_pw1, b_pw1=d_b_pw1, w_dw=d_w_dw, b_dw=d_b_dw, ln_g=d_ln_g, ln_b=d_ln_b,
                 w_pw2=d_w_pw2, b_pw2=d_b_pw2)
    return dh_in, (d_sh, d_sc, d_gate, d_g), grads


def _rope(x, c, s1, s2):
    n = x.shape[1]
    return x * c + pltpu.roll(x, n - QK_ROPE // 2, 1) * s1 + pltpu.roll(x, QK_ROPE // 2, 1) * s2


def _rope_t(dy, c, s1, s2):
    n = dy.shape[1]
    return dy * c + pltpu.roll(dy * s1, QK_ROPE // 2, 1) + pltpu.roll(dy * s2, n - QK_ROPE // 2, 1)


def rope_tables(positions):
    inv_freq = ROPE_THETA ** (-jnp.arange(0, QK_ROPE, 2, dtype=F32) / QK_ROPE)
    ang = positions.astype(F32)[:, None] * inv_freq
    cos, sin = jnp.cos(ang), jnp.sin(ang)
    S = positions.shape[0]
    one = jnp.ones((S, QK_NOPE), F32)
    z16 = jnp.zeros((S, QK_ROPE // 2), F32)
    zn = jnp.zeros((S, QK_NOPE), F32)
    zt = jnp.zeros((S, HEAD_PAD - QK_NOPE - QK_ROPE), F32)
    c = jnp.concatenate([one, cos, cos, zt], axis=1)
    s1 = jnp.concatenate([zn, -sin, z16, zt], axis=1)
    s2 = jnp.concatenate([zn, z16, sin, zt], axis=1)
    return c, s1, s2


def attn_fwd(qr, kv, kpe, n_heads):
    S = qr.shape[0]
    H = n_heads
    tq = _tile(S, (ATTN_TILE,))
    nq = S // tq
    c2 = (QK_NOPE + QK_ROPE) ** -0.5 * LOG2_E
    nt = (((1,), (1,)), ((), ()))
    tn = (((0,), (0,)), ((), ()))

    def lanes_to_rows(row):
        return jnp.transpose(jnp.broadcast_to(row, (HEAD_PAD, tq)))

    def body(q_ref, k_ref, v_ref, kpe_ref, o_ref, lse_ref, kf_ref, m_ref, l_ref, acc_ref):
        qi = pl.program_id(1)

        @pl.when(qi == 0)
        def _():
            kf_ref[...] = k_ref[...] + kpe_ref[...]

        q = q_ref[...]
        m_ref[...] = jnp.full((1, tq), -jnp.inf, F32)
        l_ref[...] = jnp.zeros((1, tq), F32)
        acc_ref[...] = jnp.zeros((tq, HEAD_PAD), F32)

        def tile(j, masked):
            start = pl.multiple_of(j * tq, tq)
            k = kf_ref[pl.ds(start, tq), :]
            v = v_ref[pl.ds(start, tq), :]
            t = lax.dot_general(k, q, nt, preferred_element_type=F32) * c2
            if masked:
                krow = lax.broadcasted_iota(jnp.int32, (tq, tq), 0)
                qcol = lax.broadcasted_iota(jnp.int32, (tq, tq), 1)
                t = jnp.where(krow <= qcol, t, NEG)
            m_old = m_ref[...]
            m_new = jnp.maximum(m_old, jnp.max(t, axis=0, keepdims=True))
            alpha = jnp.exp2(m_old - m_new)
            p = jnp.exp2(t - m_new)
            l_ref[...] = alpha * l_ref[...] + jnp.sum(p, axis=0, keepdims=True)
            pv = lax.dot_general(p.astype(BF16), v, tn, preferred_element_type=F32)
            acc_ref[...] = lanes_to_rows(alpha) * acc_ref[...] + pv
            m_ref[...] = m_new

        def unmasked(j, carry):
            tile(j, False)
            return carry

        lax.fori_loop(0, qi, unmasked, 0)
        tile(qi, True)
        l = l_ref[...]
        o_ref[...] = acc_ref[...] / lanes_to_rows(l)
        lse_ref[...] = m_ref[...] + jnp.log(l) * LOG2_E

    return pl.pallas_call(
        body, name="attn_fwd",
        grid=(H, nq),
        in_specs=[pl.BlockSpec((tq, HEAD_PAD), lambda h, i: (i, h)),
                  pl.BlockSpec((S, HEAD_PAD), lambda h, i: (0, h)),
                  pl.BlockSpec((S, HEAD_PAD), lambda h, i: (0, H + h)),
                  pl.BlockSpec((S, HEAD_PAD), lambda h, i: (0, 0))],
        out_specs=[pl.BlockSpec((tq, HEAD_PAD), lambda h, i: (i, h)),
                   pl.BlockSpec((None, None, 1, tq), lambda h, i: (h, i, 0, 0))],
        out_shape=[jax.ShapeDtypeStruct((S, H * HEAD_PAD), F32), jax.ShapeDtypeStruct((H, nq, 1, tq), F32)],
        scratch_shapes=[pltpu.VMEM((S, HEAD_PAD), BF16), pltpu.VMEM((1, tq), F32), pltpu.VMEM((1, tq), F32),
                        pltpu.VMEM((tq, HEAD_PAD), F32)],
        compiler_params=_params(("parallel", "arbitrary")),
    )(qr, kv, kv, kpe)


def attn_delta(o, do, n_heads):
    S = o.shape[0]
    H = n_heads
    tq = _tile(S, (ATTN_TILE,))

    def body(o_ref, do_ref, d_ref):
        d_ref[...] = jnp.sum(o_ref[...] * do_ref[...].astype(F32), axis=1, keepdims=True)

    return pl.pallas_call(
        body, name="attn_delta",
        grid=(H, S // tq),
        in_specs=[pl.BlockSpec((tq, HEAD_PAD), lambda h, i: (i, h)),
                  pl.BlockSpec((tq, HEAD_PAD), lambda h, i: (i, h))],
        out_specs=pl.BlockSpec((None, tq, 1), lambda h, i: (h, i, 0)),
        out_shape=jax.ShapeDtypeStruct((H, S, 1), F32),
        compiler_params=_params(("parallel", "parallel")),
    )(o, do)


def attn_bwd(qr, kv, kpe, do, lse2, delta, n_heads):
    S = qr.shape[0]
    H = n_heads
    tq = _tile(S, (ATTN_TILE,))
    nq = S // tq
    scale = (QK_NOPE + QK_ROPE) ** -0.5
    c2 = scale * LOG2_E
    nt = (((1,), (1,)), ((), ()))
    tn = (((0,), (0,)), ((), ()))
    delta4 = delta.reshape(H, nq, 1, tq)

    def body(k_ref, v_ref, kpe_ref, q_ref, do_ref, lse_ref, dl_ref, dq_ref, dk_ref, dv_ref, dka_ref, dva_ref):
        kj = pl.program_id(1)
        k = k_ref[...] + kpe_ref[...]
        v = v_ref[...]

        @pl.when(kj == 0)
        def _():
            dq_ref[...] = jnp.zeros_like(dq_ref)

        dka_ref[...] = jnp.zeros_like(dka_ref)
        dva_ref[...] = jnp.zeros_like(dva_ref)

        def tile(i, masked):
            start = pl.multiple_of(i * tq, tq)
            q = q_ref[pl.ds(start, tq), :]
            do = do_ref[pl.ds(start, tq), :]
            t = lax.dot_general(k, q, nt, preferred_element_type=F32) * c2
            if masked:
                krow = lax.broadcasted_iota(jnp.int32, (tq, tq), 0)
                qcol = lax.broadcasted_iota(jnp.int32, (tq, tq), 1)
                t = jnp.where(krow <= qcol, t, NEG)
            pt = jnp.exp2(t - lse_ref[i])
            dva_ref[...] += jnp.dot(pt.astype(BF16), do, preferred_element_type=F32)
            dpt = lax.dot_general(v, do, nt, preferred_element_type=F32)
            dst = (pt * (dpt - dl_ref[i]) * scale).astype(BF16)
            dka_ref[...] += jnp.dot(dst, q, preferred_element_type=F32)
            dq_ref[pl.ds(start, tq), :] += lax.dot_general(dst, k, tn, preferred_element_type=F32)

        tile(kj, True)

        def unmasked(i, carry):
            tile(i, False)
            return carry

        lax.fori_loop(kj + 1, nq, unmasked, 0)
        dk_ref[...] = dka_ref[...]
        dv_ref[...] = dva_ref[...]

    blk = pl.BlockSpec((tq, HEAD_PAD), lambda h, j: (j, h))
    whole = pl.BlockSpec((S, HEAD_PAD), lambda h, j: (0, h))
    stat = pl.BlockSpec((None, nq, 1, tq), lambda h, j: (h, 0, 0, 0))
    shp = jax.ShapeDtypeStruct((S, H * HEAD_PAD), F32)
    return pl.pallas_call(
        body, name="attn_bwd",
        grid=(H, nq),
        in_specs=[blk, pl.BlockSpec((tq, HEAD_PAD), lambda h, j: (j, H + h)),
                  pl.BlockSpec((tq, HEAD_PAD), lambda h, j: (j, 0)), whole, whole, stat, stat],
        out_specs=[whole, blk, blk],
        out_shape=[shp, shp, shp],
        scratch_shapes=[pltpu.VMEM((tq, HEAD_PAD), F32), pltpu.VMEM((tq, HEAD_PAD), F32)],
        compiler_params=_params(("parallel", "arbitrary")),
    )(kv, kv, kpe, qr, do, lse2, delta4)


def _pad_heads(w, width):
    R = w.shape[0]
    w3 = w.reshape(R, -1, width)
    return jnp.pad(w3, ((0, 0), (0, 0), (0, HEAD_PAD - width))).reshape(R, -1)


def _unpad_heads(w, width):
    R = w.shape[0]
    return w.reshape(R, -1, HEAD_PAD)[:, :, :width].reshape(R, -1)


def mla_pad_weights(p):
    H = N_HEADS
    w_q_b = _pad_heads(p["w_q_b"], QK_NOPE + QK_ROPE)
    kvb = p["w_kv_b"].reshape(KV_LORA, H, QK_NOPE + V_HEAD)
    wk = _pad_heads(kvb[:, :, :QK_NOPE].reshape(KV_LORA, -1), QK_NOPE)
    wv = _pad_heads(kvb[:, :, QK_NOPE:].reshape(KV_LORA, -1), V_HEAD)
    D = p["w_kv_a"].shape[0]
    a = p["w_kv_a"]
    w_kv_a = jnp.concatenate([a[:, :KV_LORA], jnp.zeros((D, QK_NOPE), a.dtype), a[:, KV_LORA:],
                              jnp.zeros((D, HEAD_PAD - QK_NOPE - QK_ROPE), a.dtype)], axis=1)
    wo = p["w_o"].reshape(H, V_HEAD, -1)
    w_o = jnp.pad(wo, ((0, 0), (0, HEAD_PAD - V_HEAD), (0, 0))).reshape(H * HEAD_PAD, -1)
    return dict(w_q_a=p["w_q_a"], w_q_b=w_q_b, w_kv_b=jnp.concatenate([wk, wv], axis=1), w_kv_a=w_kv_a, w_o=w_o)


def mla_kv_fwd(h, g, sh, sc, kv_a_norm_g, pw, tabs):
    hkv = norm_mod(h, g, sh, sc, "kv_norm_mod")
    ckvp = mm(hkv, pw["w_kv_a"], "nn", "kv_a")

    def f(ckv, kpe, c, s1, s2, g):
        xhat, _ = _rms(ckv)
        return (xhat * g).astype(BF16), _rope(kpe, c, s1, s2).astype(BF16)
    ckv_n, kpe_r = rowwise(f, [(ckvp, KV_LORA, 0), (ckvp, HEAD_PAD, KV_LORA // HEAD_PAD), *tabs], [kv_a_norm_g],
                           [(KV_LORA, BF16), (HEAD_PAD, BF16)], [], "kv_a_norm_rope")
    kv = mm(ckv_n, pw["w_kv_b"], "nn", "kv_b", out_dtype=BF16)
    return kv, kpe_r, (h, hkv, ckvp, ckv_n)


def mla_kv_bwd(dk, dv, saved, g, sc, kv_a_norm_g, pw, tabs):
    h, hkv, ckvp, ckv_n = saved
    H = N_HEADS
    lane = jnp.arange(HEAD_PAD)
    pe_mask = ((lane >= QK_NOPE) & (lane < QK_NOPE + QK_ROPE)).astype(F32)[None, :]

    def f(dk, dv, c, s1, s2, mask):
        tot = dk[:, :HEAD_PAD]
        for hh in range(1, H):
            tot = tot + dk[:, hh * HEAD_PAD:(hh + 1) * HEAD_PAD]
        dkpe = _rope_t(tot * mask, c, s1, s2) * mask
        return jnp.concatenate([dk, dv], axis=1).astype(BF16), dkpe
    dkv, dkpe = rowwise(f, [dk, dv, *tabs], [pe_mask], [(2 * H * HEAD_PAD, BF16), (HEAD_PAD, F32)], [],
                        "kv_split_bwd")
    d_w_kv_b = mm(ckv_n, dkv, "tn", "kv_b_dw")
    dckv_n = mm(dkv, pw["w_kv_b"], "nt", "kv_b_dx")

    def f2(ckv, dn, dkpe, g):
        xhat, r = _rms(ckv)
        dx = _rms_bwd(xhat, r, dn * g)
        return jnp.concatenate([dx, dkpe], axis=1).astype(BF16), dn * xhat
    dckvp, d_kv_a_g = rowwise(f2, [(ckvp, KV_LORA, 0), dckv_n, dkpe], [kv_a_norm_g],
                              [(KV_LORA + HEAD_PAD, BF16)], [KV_LORA], "kv_a_norm_bwd")
    d_w_kv_a = mm(hkv, dckvp, "tn", "kv_a_dw")
    dhkv = mm(dckvp, pw["w_kv_a"], "nt", "kv_a_dx")
    dh, d_sh, d_sc, d_g = norm_mod_bwd(h, dhkv, None, g, sc, "norm_mod_bwd_nores")
    return dh, (d_sh, d_sc, d_g), d_kv_a_g, d_w_kv_a, d_w_kv_b


def mla_fwd(h, g, sh, sc, gate, q_a_norm_g, pw, kv, kpe_r, tabs):
    H = N_HEADS
    hn = norm_mod(h, g, sh, sc, "mla_norm_mod")
    qa = mm(hn, pw["w_q_a"], "nn", "q_a")

    def f(qa, g):
        xhat, _ = _rms(qa)
        return (xhat * g).astype(BF16)
    qa_n = rowwise(f, [qa], [q_a_norm_g], [(qa.shape[1], BF16)], [], "q_a_norm")[0]
    qp = mm(qa_n, pw["w_q_b"], "nn", "q_b")

    def frope(q, c, s1, s2):
        return jnp.concatenate([_rope(q[:, hh * HEAD_PAD:(hh + 1) * HEAD_PAD], c, s1, s2) for hh in range(H)],
                               axis=1).astype(BF16)
    qr = rowwise(frope, [qp, *tabs], [], [(H * HEAD_PAD, BF16)], [], "q_rope")[0]
    o, lse = attn_fwd(qr, kv, kpe_r, H)
    y = mm(o, pw["w_o"], "nn", "w_o")
    h_out, _ = residual(h, y, gate, 1.0, "mla_residual")
    return h_out, (h, hn, qa, qa_n, qr, o, lse, y)


def mla_bwd(dh_out, saved, g, sc, gate, q_a_norm_g, pw, kv, kpe_r, tabs):
    h, hn, qa, qa_n, qr, o, lse, y = saved
    H = N_HEADS
    dy, d_gate = residual_bwd(dh_out, y, gate, 1.0, "mla_residual_bwd")
    d_w_o = mm(o, dy, "tn", "w_o_dw")
    do = mm(dy, pw["w_o"], "nt", "w_o_dx", out_dtype=BF16)
    delta = attn_delta(o, do, H)
    dqr, dk, dv = attn_bwd(qr, kv, kpe_r, do, lse, delta, H)

    def frope_t(dq, c, s1, s2):
        return jnp.concatenate([_rope_t(dq[:, hh * HEAD_PAD:(hh + 1) * HEAD_PAD], c, s1, s2) for hh in range(H)],
                               axis=1).astype(BF16)
    dqp = rowwise(frope_t, [dqr, *tabs], [], [(H * HEAD_PAD, BF16)], [], "q_rope_bwd")[0]
    d_w_q_b = mm(qa_n, dqp, "tn", "q_b_dw")
    dqa_n = mm(dqp, pw["w_q_b"], "nt", "q_b_dx")

    def f(qa, dn, g):
        xhat, r = _rms(qa)
        return _rms_bwd(xhat, r, dn * g).astype(BF16), dn * xhat
    dqa, d_q_a_g = rowwise(f, [qa, dqa_n], [q_a_norm_g], [(qa.shape[1], BF16)], [qa.shape[1]], "q_a_norm_bwd")
    d_w_q_a = mm(hn, dqa, "tn", "q_a_dw")
    dhn = mm(dqa, pw["w_q_a"], "nt", "q_a_dx")
    dh_in, d_sh, d_sc, d_g = norm_mod_bwd(h, dhn, dh_out, g, sc, "norm_mod_bwd")
    grads = dict(w_q_a=d_w_q_a, q_a_norm_g=d_q_a_g, w_q_b=d_w_q_b, w_o=d_w_o)
    return dh_in, (d_sh, d_sc, d_gate, d_g), grads, dk, dv


def loss_head(h, target, g):
    D = h.shape[1]

    def f(h, t, g):
        xhat, r = _rms(h)
        err = xhat * g - t
        dy = err * (1.0 / D)
        dh = _rms_bwd(xhat, r, dy * g)
        return dh, (0.5 / D) * err * err, dy * xhat
    return rowwise(f, [h, target], [g], [(D, F32)], [D, D], "loss_head")


def _place():
    x, y, c = lax.axis_index("x"), lax.axis_index("y"), lax.axis_index("c")
    chips = [(1 - x, y), (x, 1 - y), (1 - x, 1 - y)]
    return x, y, c, chips


HBM_SPEC = pl.BlockSpec(memory_space=pltpu.HBM)


def all_gather8(v):
    m, n = v.shape

    def body(x_ref, out_ref, send_sems, recv_sems, local_sem):
        x, y, c, chips = _place()
        me, sibling = (x, y, c), (x, y, 1 - c)

        def rows(px, py, pc):
            return out_ref.at[4 * px + 2 * py + pc]

        def copy(k, block, to, src=None):
            return pltpu.make_async_remote_copy(
                src_ref=rows(*block) if src is None else src, dst_ref=rows(*block),
                send_sem=send_sems.at[k], recv_sem=recv_sems.at[k], device_id=to, device_id_type=MESH)

        mine = pltpu.make_async_copy(x_ref, rows(*me), local_sem)
        mine.start()
        first = [copy(0, me, sibling, src=x_ref)]
        first += [copy(1 + j, me, (*chip, c), src=x_ref) for j, chip in enumerate(chips)]
        for cp in first:
            cp.start()
        passed = [copy(4 + j, (*chip, c), sibling) for j, chip in enumerate(chips)]
        for j, chip in enumerate(chips):
            copy(1 + j, (*chip, c), me).wait_recv()
            passed[j].start()
        copy(0, sibling, me).wait_recv()
        for j, chip in enumerate(chips):
            copy(4 + j, (*chip, 1 - c), me).wait_recv()
        for cp in first + passed:
            cp.wait_send()
        mine.wait()

    return pl.pallas_call(
        body, name="all_gather8",
        out_shape=jax.ShapeDtypeStruct((8, m, n), v.dtype),
        in_specs=[pl.BlockSpec(memory_space=pltpu.VMEM)],
        out_specs=pl.BlockSpec(memory_space=pltpu.VMEM),
        scratch_shapes=[pltpu.SemaphoreType.DMA((7,)), pltpu.SemaphoreType.DMA((7,)), pltpu.SemaphoreType.DMA],
        compiler_params=pltpu.CompilerParams(vmem_limit_bytes=VMEM_LIMIT_BYTES),
    )(v)


def gather_weights(bufs):
    n = len(bufs)

    def body(*refs):
        ins, outs = refs[:n], refs[n:2 * n]
        send_sems, recv_sems = refs[2 * n:]
        x, y, c, chips = _place()
        sibling = (x, y, 1 - c)
        me = 2 * x + y

        def idx(chip):
            return 2 * chip[0] + chip[1]

        def copy(w, k, src, dst, to):
            return pltpu.make_async_remote_copy(src_ref=src, dst_ref=dst, send_sem=send_sems.at[6 * w + k],
                                                recv_sem=recv_sems.at[6 * w + k], device_id=to, device_id_type=MESH)

        first = [copy(w, j, ins[w].at[me, c], outs[w].at[me, c], (*chip, c))
                 for w in range(n) for j, chip in enumerate(chips)]
        for cp in first:
            cp.start()
        passed = []
        for w in range(n):
            for j, chip in enumerate(chips):
                landed = outs[w].at[idx(chip), c]
                copy(w, j, landed, landed, (*chip, c)).wait_recv()
                fwd = copy(w, 3 + j, landed, landed, sibling)
                fwd.start()
                passed.append(fwd)
        for w in range(n):
            for j, chip in enumerate(chips):
                other = outs[w].at[idx(chip), 1 - c]
                copy(w, 3 + j, other, other, sibling).wait_recv()
        for cp in first + passed:
            cp.wait_send()

    return pl.pallas_call(
        body, name="gather_weights",
        out_shape=[jax.ShapeDtypeStruct(b.shape, b.dtype) for b in bufs],
        in_specs=[HBM_SPEC] * n, out_specs=[HBM_SPEC] * n,
        input_output_aliases={w: w for w in range(n)},
        scratch_shapes=[pltpu.SemaphoreType.DMA((6 * n,)), pltpu.SemaphoreType.DMA((6 * n,))],
    )(*bufs)


def exchange_halves(gs):
    n = len(gs)

    def body(*refs):
        ins, theirs = refs[:n], refs[n:2 * n]
        send_sems, recv_sems = refs[2 * n:]
        x, y, c, _ = _place()
        sends = [pltpu.make_async_remote_copy(src_ref=ins[w].at[:, 1 - c], dst_ref=theirs[w],
                                              send_sem=send_sems.at[w], recv_sem=recv_sems.at[w],
                                              device_id=(x, y, 1 - c), device_id_type=MESH) for w in range(n)]
        for cp in sends:
            cp.start()
        for cp in sends:
            cp.wait()

    return pl.pallas_call(
        body, name="exchange_halves",
        out_shape=[jax.ShapeDtypeStruct((4,) + g.shape[2:], g.dtype) for g in gs],
        in_specs=[HBM_SPEC] * n, out_specs=[HBM_SPEC] * n,
        scratch_shapes=[pltpu.SemaphoreType.DMA((n,)), pltpu.SemaphoreType.DMA((n,))],
    )(*gs)


def scatter_blocks(ps):
    n = len(ps)

    def body(*refs):
        ins, outs = refs[:n], refs[n:2 * n]
        send_sems, recv_sems = refs[2 * n:]
        x, y, c, chips = _place()
        sends = [pltpu.make_async_remote_copy(src_ref=ins[w].at[2 * chip[0] + chip[1]], dst_ref=outs[w].at[j],
                                              send_sem=send_sems.at[3 * w + j], recv_sem=recv_sems.at[3 * w + j],
                                              device_id=(*chip, c), device_id_type=MESH)
                 for w in range(n) for j, chip in enumerate(chips)]
        for cp in sends:
            cp.start()
        for cp in sends:
            cp.wait()

    return pl.pallas_call(
        body, name="scatter_blocks",
        out_shape=[jax.ShapeDtypeStruct((3,) + p.shape[1:], p.dtype) for p in ps],
        in_specs=[HBM_SPEC] * n, out_specs=[HBM_SPEC] * n,
        scratch_shapes=[pltpu.SemaphoreType.DMA((3 * n,)), pltpu.SemaphoreType.DMA((3 * n,))],
    )(*ps)


def join_halves(qs):
    n = len(qs)

    def body(*refs):
        ins, outs = refs[:n], refs[n:2 * n]
        send_sems, recv_sems = refs[2 * n:]
        x, y, c, _ = _place()
        sends = [pltpu.make_async_remote_copy(src_ref=ins[w].at[c], dst_ref=outs[w].at[c], send_sem=send_sems.at[w],
                                              recv_sem=recv_sems.at[w], device_id=(x, y, 1 - c), device_id_type=MESH)
                 for w in range(n)]
        for cp in sends:
            cp.start()
        for w in range(n):
            other = outs[w].at[1 - c]
            pltpu.make_async_remote_copy(src_ref=other, dst_ref=other, send_sem=send_sems.at[w],
                                         recv_sem=recv_sems.at[w], device_id=(x, y, 1 - c),
                                         device_id_type=MESH).wait_recv()
        for cp in sends:
            cp.wait_send()

    return pl.pallas_call(
        body, name="join_halves",
        out_shape=[jax.ShapeDtypeStruct(q.shape, q.dtype) for q in qs],
        in_specs=[HBM_SPEC] * n, out_specs=[HBM_SPEC] * n,
        input_output_aliases={w: w for w in range(n)},
        scratch_shapes=[pltpu.SemaphoreType.DMA((n,)), pltpu.SemaphoreType.DMA((n,))],
    )(*qs)


def _row_tile(R, row_bytes):
    tm = R
    for t in (512, 256, 128, 64, 32, 16, 8):
        if R % t == 0:
            tm = t
            if t * row_bytes <= ROW_TILE_BUDGET:
                break
    return tm


def sum_siblings(g, theirs, place):
    _, _, R, C = g.shape
    tm = _row_tile(R, 3 * C * 4)

    def body(place_ref, a_ref, b_ref, o_ref):
        o_ref[...] = (a_ref[...] + b_ref[...]).astype(BF16)

    return pl.pallas_call(
        body, name="sum_siblings",
        grid_spec=pltpu.PrefetchScalarGridSpec(
            num_scalar_prefetch=1, grid=(4, R // tm),
            in_specs=[pl.BlockSpec((None, None, tm, C), lambda j, i, s: (j, s[1], i, 0)),
                      pl.BlockSpec((None, tm, C), lambda j, i, s: (j, i, 0))],
            out_specs=pl.BlockSpec((None, tm, C), lambda j, i, s: (j, i, 0))),
        out_shape=jax.ShapeDtypeStruct((4, R, C), BF16),
        compiler_params=_params(("parallel", "parallel")),
    )(place, g, theirs)


def sum_chips(p, landed, place):
    _, R, C = p.shape
    tm = _row_tile(R, 5 * C * 4)

    def body(place_ref, p_ref, l0_ref, l1_ref, l2_ref, o_ref):
        o_ref[...] = ((p_ref[...].astype(F32) + l0_ref[...].astype(F32)) + l1_ref[...].astype(F32)
                      ) + l2_ref[...].astype(F32)

    return pl.pallas_call(
        body, name="sum_chips",
        grid_spec=pltpu.PrefetchScalarGridSpec(
            num_scalar_prefetch=1, grid=(R // tm,),
            in_specs=[pl.BlockSpec((None, tm, C), lambda i, s: (s[0], i, 0))]
            + [pl.BlockSpec((None, tm, C), lambda i, s, j=j: (j, i, 0)) for j in range(3)],
            out_specs=pl.BlockSpec((None, tm, C), lambda i, s: (s[1], i, 0))),
        out_shape=jax.ShapeDtypeStruct((2, R, C), F32),
        compiler_params=_params(("parallel",)),
    )(place, p, landed, landed, landed)


def sum_blocks(items, name):
    R, C = items[0][0].shape[1:]
    tm = R
    for t in (512, 256, 128, 64, 32, 16, 8):
        if R % t == 0:
            tm = t
            if t * C * 4 * (len(items) + 1) <= ROW_TILE_BUDGET:
                break
    n = len(items)

    def body(*refs):
        acc = refs[0][...].astype(F32)
        for r in refs[1:n]:
            acc = acc + r[...].astype(F32)
        refs[n][...] = acc

    return pl.pallas_call(
        body, name=name,
        grid=(R // tm,),
        in_specs=[pl.BlockSpec((None, tm, C), lambda i, j=j: (j, i, 0)) for _, j in items],
        out_specs=pl.BlockSpec((tm, C), lambda i: (i, 0)),
        out_shape=jax.ShapeDtypeStruct((R, C), F32),
        compiler_params=_params(("parallel",)),
    )(*[a for a, _ in items])


def reduce_scatter_grads(gs, place):
    theirs = exchange_halves(gs)
    ps = [sum_siblings(g, t, place) for g, t in zip(gs, theirs)]
    landed = scatter_blocks(ps)
    qs = [sum_chips(p, l, place) for p, l in zip(ps, landed)]
    joined = join_halves(qs)
    return [j.reshape(2 * j.shape[1], j.shape[2]) for j in joined]


def adamw(w, g, m, v):
    shape = w.shape
    C = shape[-1]
    R = w.size // C
    tm = R
    for t in (512, 256, 128, 64, 32, 16, 8):
        if R % t == 0:
            tm = t
            if t * C * 4 * 7 <= ROW_TILE_BUDGET:
                break

    def f(w, g, m, v):
        m = ADAM_B1 * m + (1.0 - ADAM_B1) * g
        v = ADAM_B2 * v + (1.0 - ADAM_B2) * (g * g)
        m_hat = m / (1.0 - ADAM_B1 ** ADAM_STEP)
        v_hat = v / (1.0 - ADAM_B2 ** ADAM_STEP)
        delta = -ADAM_LR * (m_hat / (jnp.sqrt(v_hat) + ADAM_EPS) + ADAM_WD * w)
        return delta, m, v

    d, nm, nv = rowwise(f, [a.reshape(R, C) for a in (w, g, m, v)], [], [(C, F32)] * 3, [], "adamw", tm=tm)
    return d.reshape(shape), nm.reshape(shape), nv.reshape(shape)


def _cast_into_slot(w, place):
    C = w.shape[-1]
    w2 = w.reshape(-1, C)
    R = w2.shape[0]
    tm = _row_tile(R, 6 * C)

    def body(place_ref, w_ref, o_ref):
        o_ref[...] = w_ref[...].astype(BF16)

    out = pl.pallas_call(
        body, name="cast_bf16",
        grid_spec=pltpu.PrefetchScalarGridSpec(
            num_scalar_prefetch=1, grid=(R // tm,),
            in_specs=[pl.BlockSpec((tm, C), lambda i, s: (i, 0))],
            out_specs=pl.BlockSpec((None, tm, C), lambda i, s: (s[0], i, 0))),
        out_shape=jax.ShapeDtypeStruct((4, R, C), BF16),
        compiler_params=_params(("parallel",)),
    )(place, w2)
    return out.reshape(4, 2, R // 2, C)


def _pack(vs):
    flat = jnp.concatenate([v.reshape(-1) for v in vs])
    n = flat.shape[0]
    total = -(-n // 1024) * 1024
    return jnp.pad(flat, (0, total - n)).reshape(total // 128, 128)


def _unpack(flat, like):
    out, o = [], 0
    for shp in like:
        sz = 1
        for d in shp:
            sz *= d
        out.append(flat[o:o + sz].reshape(shp))
        o += sz
    return out


def _cols_to_blocks(g, n_chips=4):
    R, N = g.shape
    C = N // n_chips
    return g.reshape(R, n_chips, C).transpose(1, 0, 2).reshape(n_chips, 2, R // 2, C)


def _rows_to_blocks(g, n_chips=4):
    R, C = g.shape
    return g.reshape(n_chips, 2, R // n_chips // 2, C)


def kernel(x, c, positions, ada_w, ada_b, norm_g, ffn_w13, ffn_w2, conv_w_pw1, conv_b_pw1, conv_w_dw, conv_b_dw, conv_ln_g, conv_ln_b, conv_w_pw2, conv_b_pw2, kv_ada_w, kv_ada_b, kv_norm_g, w_kv_a, kv_a_norm_g, w_kv_b, w_q_a, q_a_norm_g, w_q_b, w_o, final_norm_g, loss_target, m_ada_w, m_ada_b, m_norm_g, m_ffn_w13, m_ffn_w2, m_conv_w_pw1, m_conv_b_pw1, m_conv_w_dw, m_conv_b_dw, m_conv_ln_g, m_conv_ln_b, m_conv_w_pw2, m_conv_b_pw2, m_kv_ada_w, m_kv_ada_b, m_kv_norm_g, m_w_kv_a, m_kv_a_norm_g, m_w_kv_b, m_w_q_a, m_q_a_norm_g, m_w_q_b, m_w_o, m_final_norm_g, v_ada_w, v_ada_b, v_norm_g, v_ffn_w13, v_ffn_w2, v_conv_w_pw1, v_conv_b_pw1, v_conv_w_dw, v_conv_b_dw, v_conv_ln_g, v_conv_ln_b, v_conv_w_pw2, v_conv_b_pw2, v_kv_ada_w, v_kv_ada_b, v_kv_norm_g, v_w_kv_a, v_kv_a_norm_g, v_w_kv_b, v_w_q_a, v_q_a_norm_g, v_w_q_b, v_w_o, v_final_norm_g):
    S, D = x.shape[1], x.shape[2]
    H = N_HEADS
    F = ffn_w2.shape[2] * 4
    xi, yi, ci = lax.axis_index("x"), lax.axis_index("y"), lax.axis_index("c")
    chip = 2 * xi + yi
    dev = 2 * chip + ci
    place = jnp.stack([chip, ci]).astype(jnp.int32)
    h0 = x[0]
    target = loss_target[0]

    silu_c = rowwise(lambda a: a * _sigmoid(a), [c], [], [(D, F32)], [], "silu_c")[0]
    silu_all = all_gather8(silu_c.reshape(8, D // 8)).reshape(8, D)
    n_ada = ada_w.shape[2]
    n_kv = kv_ada_w.shape[1]
    ada_b_mine = lax.dynamic_slice_in_dim(ada_b, chip * n_ada, n_ada, axis=1)
    kv_b_mine = lax.dynamic_slice_in_dim(kv_ada_b, chip * n_kv, n_kv, axis=0)[None, :]
    mods = [mm(silu_all, ada_w[l], "nn", "ada_rows", bias=ada_b_mine[l:l + 1]) for l in range(2)]
    mods.append(mm(silu_all, kv_ada_w, "nn", "kv_ada_rows", bias=kv_b_mine))
    n_mod_cols = 2 * n_ada + n_kv
    mod_pack = jnp.concatenate(mods, axis=1).reshape(-1, 128)
    mod_all = all_gather8(mod_pack).reshape(8, 8, n_mod_cols)[0::2]
    mod_mine = lax.dynamic_index_in_dim(mod_all, dev, axis=1, keepdims=False)
    mod = [mod_mine[:, l * n_ada:(l + 1) * n_ada].reshape(N_MOD, D) for l in range(2)]
    kv_mod = mod_mine[:, 2 * n_ada:].reshape(2, D)
    kv_shift, kv_scale = kv_mod[0:1], kv_mod[1:2]

    def mrow(l, k):
        return mod[l][k:k + 1]

    big = dict(ffn_w13=ffn_w13, ffn_w2=ffn_w2, conv_w_pw1=conv_w_pw1, conv_w_pw2=conv_w_pw2, w_kv_a=w_kv_a,
               w_kv_b=w_kv_b, w_q_a=w_q_a, w_q_b=w_q_b, w_o=w_o)
    names = list(big)
    gathered = gather_weights([_cast_into_slot(big[k], place) for k in names])
    gw = dict(zip(names, gathered))
    small_like = [norm_g.shape, conv_b_pw1.shape, conv_w_dw.shape, conv_b_dw.shape, conv_ln_g.shape,
                  conv_ln_b.shape, conv_b_pw2.shape]
    small_pack = _pack([norm_g, conv_b_pw1, conv_w_dw, conv_b_dw, conv_ln_g, conv_ln_b, conv_b_pw2])
    small_all = all_gather8(small_pack)[0::2].reshape(4, -1)
    per_chip = [_unpack(small_all[j], small_like) for j in range(4)]
    smalls = [jnp.concatenate([per_chip[j][k] for j in range(4)], axis=-1) for k in range(len(small_like))]
    norm_g_f, b_pw1_f, w_dw_f, b_dw_f, ln_g_f, ln_b_f, b_pw2_f = smalls

    w13 = gw["ffn_w13"].reshape(4, 2, 2, D, 2 * F // 4).transpose(1, 2, 3, 0, 4).reshape(2, 2, D, 2 * F)
    w2 = gw["ffn_w2"].reshape(4, 2, 2, F // 4, D).transpose(1, 2, 0, 3, 4).reshape(2, 2, F, D)
    conv_p = dict(
        w_pw1=gw["conv_w_pw1"].reshape(4, D, 2 * D // 4).transpose(1, 0, 2).reshape(D, 2 * D),
        b_pw1=b_pw1_f, w_dw=w_dw_f[0], b_dw=b_dw_f, ln_g=ln_g_f, ln_b=ln_b_f,
        w_pw2=gw["conv_w_pw2"].reshape(D, D), b_pw2=b_pw2_f)
    q_lora = w_q_a.shape[2]
    mla_p = dict(
        w_kv_a=gw["w_kv_a"].reshape(D, KV_LORA + QK_ROPE),
        w_kv_b=gw["w_kv_b"].reshape(4, KV_LORA, -1).transpose(1, 0, 2).reshape(KV_LORA, -1),
        w_q_a=gw["w_q_a"].reshape(D, q_lora),
        w_q_b=gw["w_q_b"].reshape(4, q_lora, -1).transpose(1, 0, 2).reshape(q_lora, -1),
        w_o=gw["w_o"].reshape(H * V_HEAD, D))
    pw = mla_pad_weights(mla_p)
    tabs = rope_tables(positions[0])

    def ng(l, k):
        return norm_g_f[l, k][None, :]

    h = h0
    h, s_f1_0 = ffn_fwd(h, ng(0, 0), mrow(0, 0), mrow(0, 1), mrow(0, 2), w13[0, 0], w2[0, 0])
    h, s_conv = conv_module_fwd(h, ng(0, 1), mrow(0, 3), mrow(0, 4), mrow(0, 5), conv_p)
    h, s_f2_0 = ffn_fwd(h, ng(0, 2), mrow(0, 6), mrow(0, 7), mrow(0, 8), w13[0, 1], w2[0, 1])
    kv_norm = kv_norm_g[None, :]
    kv_a_g = kv_a_norm_g[None, :]
    kv, kpe_r, s_kv = mla_kv_fwd(h, kv_norm, kv_shift, kv_scale, kv_a_g, pw, tabs)
    h, s_f1_1 = ffn_fwd(h, ng(1, 0), mrow(1, 0), mrow(1, 1), mrow(1, 2), w13[1, 0], w2[1, 0])
    h, s_mla = mla_fwd(h, ng(1, 1), mrow(1, 3), mrow(1, 4), mrow(1, 5), q_a_norm_g, pw, kv, kpe_r, tabs)
    h, s_f2_1 = ffn_fwd(h, ng(1, 2), mrow(1, 6), mrow(1, 7), mrow(1, 8), w13[1, 1], w2[1, 1])
    dh, loss_cols, d_final_g = loss_head(h, target, final_norm_g[None, :])

    dh, v_f2_1, dw13_11, dw2_11 = ffn_bwd(dh, s_f2_1, ng(1, 2), mrow(1, 7), mrow(1, 8), w13[1, 1], w2[1, 1])
    dh, v_mla, g_mla, dk, dv = mla_bwd(dh, s_mla, ng(1, 1), mrow(1, 4), mrow(1, 5), q_a_norm_g, pw, kv, kpe_r, tabs)
    dh, v_f1_1, dw13_10, dw2_10 = ffn_bwd(dh, s_f1_1, ng(1, 0), mrow(1, 1), mrow(1, 2), w13[1, 0], w2[1, 0])
    dh_kv, v_kv, d_kv_a_g, d_w_kv_a, d_w_kv_b = mla_kv_bwd(dk, dv, s_kv, kv_norm, kv_scale, kv_a_g, pw, tabs)
    dh = rowwise(lambda a, b: a + b, [dh, dh_kv], [], [(D, F32)], [], "add_stream")[0]
    dh, v_f2_0, dw13_01, dw2_01 = ffn_bwd(dh, s_f2_0, ng(0, 2), mrow(0, 7), mrow(0, 8), w13[0, 1], w2[0, 1])
    dh, v_conv, g_conv = conv_module_bwd(dh, s_conv, ng(0, 1), mrow(0, 4), mrow(0, 5), conv_p)
    dh, v_f1_0, dw13_00, dw2_00 = ffn_bwd(dh, s_f1_0, ng(0, 0), mrow(0, 1), mrow(0, 2), w13[0, 0], w2[0, 0])
    grad_x = dh[None]

    d_w_kv_a_u = jnp.concatenate([d_w_kv_a[:, :KV_LORA], d_w_kv_a[:, KV_LORA + QK_NOPE:KV_LORA + QK_NOPE + QK_ROPE]],
                                 axis=1)
    hk = H * HEAD_PAD
    dkb = jnp.concatenate([d_w_kv_b[:, :hk].reshape(KV_LORA, H, HEAD_PAD)[:, :, :QK_NOPE],
                           d_w_kv_b[:, hk:].reshape(KV_LORA, H, HEAD_PAD)[:, :, :V_HEAD]], axis=2).reshape(KV_LORA, -1)
    d_w_q_b_u = _unpad_heads(g_mla["w_q_b"], QK_NOPE + QK_ROPE)
    d_w_o_u = g_mla["w_o"].reshape(H, HEAD_PAD, D)[:, :V_HEAD].reshape(H * V_HEAD, D)
    full = [_cols_to_blocks(dw13_00), _cols_to_blocks(dw13_01), _cols_to_blocks(dw13_10), _cols_to_blocks(dw13_11),
            _rows_to_blocks(dw2_00), _rows_to_blocks(dw2_01), _rows_to_blocks(dw2_10), _rows_to_blocks(dw2_11),
            _cols_to_blocks(g_conv["w_pw1"]), _rows_to_blocks(g_conv["w_pw2"]), _rows_to_blocks(d_w_kv_a_u),
            _cols_to_blocks(dkb), _rows_to_blocks(g_mla["w_q_a"]), _cols_to_blocks(d_w_q_b_u),
            _rows_to_blocks(d_w_o_u)]
    red = reduce_scatter_grads(full, place)
    g_ffn_w13 = jnp.stack(red[0:4]).reshape(ffn_w13.shape)
    g_ffn_w2 = jnp.stack(red[4:8]).reshape(ffn_w2.shape)
    g_conv_w_pw1 = red[8].reshape(conv_w_pw1.shape)
    g_conv_w_pw2 = red[9].reshape(conv_w_pw2.shape)
    g_w_kv_a = red[10].reshape(w_kv_a.shape)
    g_w_kv_b = red[11].reshape(w_kv_b.shape)
    g_w_q_a = red[12].reshape(w_q_a.shape)
    g_w_q_b = red[13].reshape(w_q_b.shape)
    g_w_o = red[14].reshape(w_o.shape)

    def dmod(v1, vm, v2):
        return jnp.concatenate([v1[0], v1[1], v1[2], vm[0], vm[1], vm[2], v2[0], v2[1], v2[2]], axis=1)
    d_mod0 = dmod(v_f1_0, v_conv, v_f2_0)
    d_mod1 = dmod(v_f1_1, v_mla, v_f2_1)
    d_kv_mod = jnp.concatenate([v_kv[0], v_kv[1]], axis=1)
    d_norm_g = jnp.concatenate([v_f1_0[3], v_conv[3], v_f2_0[3], v_f1_1[3], v_mla[3], v_f2_1[3]], axis=0)
    vec_list = [d_mod0, d_mod1, d_kv_mod, d_norm_g, g_conv["b_pw1"], g_conv["w_dw"], g_conv["b_dw"], g_conv["ln_g"],
                g_conv["ln_b"], g_conv["b_pw2"], v_kv[2], d_kv_a_g, g_mla["q_a_norm_g"], d_final_g, loss_cols]
    vec_like = [v.shape for v in vec_list]
    vec_pack = _pack(vec_list)
    n_mod_rows = (2 * N_MOD * D + 2 * D) // 128
    vec_all = all_gather8(vec_pack)
    vec_sum = sum_blocks([(vec_all, d) for d in range(8)], "sum_devices").reshape(-1)
    (_, _, _, s_norm_g, s_b_pw1, s_w_dw, s_b_dw, s_ln_g, s_ln_b, s_b_pw2, s_kv_norm_g, s_kv_a_g, s_q_a_g,
     s_final_g, s_loss) = _unpack(vec_sum, vec_like)
    loss = jnp.sum(s_loss)
    dmod_all = vec_all[:, :n_mod_rows].reshape(8, 2 * N_MOD * D + 2 * D)
    dmod_sum = vec_sum[:2 * N_MOD * D + 2 * D]
    g_ada_b = dmod_sum[:2 * N_MOD * D].reshape(2, N_MOD * D)
    g_kv_ada_b = dmod_sum[2 * N_MOD * D:]
    g_ada_w = []
    for l in range(2):
        cols = lax.dynamic_slice_in_dim(dmod_all[:, l * N_MOD * D:(l + 1) * N_MOD * D], chip * n_ada, n_ada, axis=1)
        g_ada_w.append(mm(silu_all, cols, "tn", "ada_w_grad"))
    g_ada_w = jnp.stack(g_ada_w)
    kv_cols = lax.dynamic_slice_in_dim(dmod_all[:, 2 * N_MOD * D:], chip * n_kv, n_kv, axis=1)
    g_kv_ada_w = mm(silu_all, kv_cols, "tn", "kv_ada_w_grad")

    def shard(v, width):
        return lax.dynamic_slice_in_dim(v, chip * width, width, axis=v.ndim - 1)

    Dq = D // 4
    g_norm_g = shard(s_norm_g.reshape(2, 3, D), Dq)
    g_conv_b_pw1 = shard(s_b_pw1, 2 * D // 4)
    g_conv_w_dw = shard(s_w_dw, Dq)[None]
    g_conv_b_dw = shard(s_b_dw, Dq)
    g_conv_ln_g = shard(s_ln_g, Dq)
    g_conv_ln_b = shard(s_ln_b, Dq)
    g_conv_b_pw2 = shard(s_b_pw2, Dq)

    grads = [g_ada_w, g_ada_b, g_norm_g, g_ffn_w13, g_ffn_w2, g_conv_w_pw1, g_conv_b_pw1, g_conv_w_dw, g_conv_b_dw,
             g_conv_ln_g, g_conv_ln_b, g_conv_w_pw2, g_conv_b_pw2, g_kv_ada_w, g_kv_ada_b, s_kv_norm_g[0], g_w_kv_a,
             s_kv_a_g[0], g_w_kv_b, g_w_q_a, s_q_a_g, g_w_q_b, g_w_o, s_final_g[0]]
    weights = [ada_w, ada_b, norm_g, ffn_w13, ffn_w2, conv_w_pw1, conv_b_pw1, conv_w_dw, conv_b_dw, conv_ln_g,
               conv_ln_b, conv_w_pw2, conv_b_pw2, kv_ada_w, kv_ada_b, kv_norm_g, w_kv_a, kv_a_norm_g, w_kv_b, w_q_a,
               q_a_norm_g, w_q_b, w_o, final_norm_g]
    ms = [m_ada_w, m_ada_b, m_norm_g, m_ffn_w13, m_ffn_w2, m_conv_w_pw1, m_conv_b_pw1, m_conv_w_dw, m_conv_b_dw,
          m_conv_ln_g, m_conv_ln_b, m_conv_w_pw2, m_conv_b_pw2, m_kv_ada_w, m_kv_ada_b, m_kv_norm_g, m_w_kv_a,
          m_kv_a_norm_g, m_w_kv_b, m_w_q_a, m_q_a_norm_g, m_w_q_b, m_w_o, m_final_norm_g]
    vs = [v_ada_w, v_ada_b, v_norm_g, v_ffn_w13, v_ffn_w2, v_conv_w_pw1, v_conv_b_pw1, v_conv_w_dw, v_conv_b_dw,
          v_conv_ln_g, v_conv_ln_b, v_conv_w_pw2, v_conv_b_pw2, v_kv_ada_w, v_kv_ada_b, v_kv_norm_g, v_w_kv_a,
          v_kv_a_norm_g, v_w_kv_b, v_w_q_a, v_q_a_norm_g, v_w_q_b, v_w_o, v_final_norm_g]
    grads = [g.reshape(w.shape) for g, w in zip(grads, weights)]
    deltas, new_m, new_v = [], [], []
    for w, g, m, v in zip(weights, grads, ms, vs):
        d, nm, nv = adamw(w, g, m, v)
        deltas.append(d)
        new_m.append(nm)
        new_v.append(nv)
    return (loss, grad_x, *grads, *deltas, *new_m, *new_v)
```

```python
import jax
import jax.numpy as jnp
from jax import lax
from jax.experimental import pallas as pl
from jax.experimental.pallas import tpu as pltpu

F32 = jnp.float32
BF16 = jnp.bfloat16
MESH = pl.DeviceIdType.MESH

N_HEADS = 16
QK_NOPE = 64
QK_ROPE = 32
V_HEAD = 64
KV_LORA = 256
CONV_WIDTH = 31
ROPE_THETA = 10000.0
EPS = 1e-6
N_MOD = 9
HEAD_PAD = 128
ATTN_TILE = 512
CONV_HALO = 32

ADAM_LR = 0.001
ADAM_B1 = 0.9
ADAM_B2 = 0.999
ADAM_EPS = 1e-08
ADAM_WD = 0.01
ADAM_STEP = 10

VMEM_LIMIT_BYTES = 56 * 2 ** 20
ROW_TILE_BUDGET = 10 * 2 ** 20
MM_VMEM_BUDGET = 40 * 2 ** 20
NEG = float(jnp.finfo(jnp.float32).min)
LOG2_E = 1.4426950408889634


def _tile(n, prefs):
    for t in prefs:
        if n % t == 0:
            return t
    return n


def _params(sem):
    return pltpu.CompilerParams(dimension_semantics=sem, vmem_limit_bytes=VMEM_LIMIT_BYTES)


def _mm_tiles(M, N, K, mode, a_bytes, b_bytes, o_bytes):
    if mode == "tn":
        tk_opts = [t for t in (2048, 1024, 512, 256, 128) if K % t == 0] or [K]
        tm_opts = ([M] if M <= 2816 else []) + [t for t in (1024, 512, 256, 128) if M % t == 0 and t < M]
    else:
        tk_opts = [K]
        tm_opts = [t for t in (1024, 512, 256, 128) if M % t == 0] or [M]
    tn_opts = [t for t in (1408, 1024, 512, 384, 256, 128) if N % t == 0] or [N]

    def need(tm, tn, tk):
        blocks = 2 * (tm * tk * a_bytes + tk * tn * b_bytes + tm * tn * o_bytes)
        return blocks + (tm * tn * 4 if mode == "tn" else 0)

    tk_floor = next((t for t in tk_opts if t <= 512), tk_opts[-1])
    for tm in tm_opts:
        for tn in tn_opts:
            if need(tm, tn, tk_floor) <= MM_VMEM_BUDGET:
                return tm, tn, next(tk for tk in tk_opts if need(tm, tn, tk) <= MM_VMEM_BUDGET)
    return tm_opts[-1], tn_opts[-1], tk_opts[-1]


def mm(a, b, mode, name, out_dtype=F32, bias=None):
    if mode == "nn":
        (M, K), (K2, N) = a.shape, b.shape
        dims = (((1,), (0,)), ((), ()))
    elif mode == "nt":
        (M, K), (N, K2) = a.shape, b.shape
        dims = (((1,), (1,)), ((), ()))
    else:
        (K, M), (K2, N) = a.shape, b.shape
        dims = (((0,), (0,)), ((), ()))
    assert K == K2, (a.shape, b.shape, mode)
    tm, tn, tk = _mm_tiles(M, N, K, mode, a.dtype.itemsize, b.dtype.itemsize, jnp.dtype(out_dtype).itemsize)
    nk = K // tk
    if mode == "tn":
        a_spec = pl.BlockSpec((tk, tm), lambda i, j, k: (k, i))
        b_spec = pl.BlockSpec((tk, tn), lambda i, j, k: (k, j))
    elif mode == "nn":
        a_spec = pl.BlockSpec((tm, tk), lambda i, j, k: (i, k))
        b_spec = pl.BlockSpec((tk, tn), lambda i, j, k: (k, j))
    else:
        a_spec = pl.BlockSpec((tm, tk), lambda i, j, k: (i, k))
        b_spec = pl.BlockSpec((tn, tk), lambda i, j, k: (j, k))
    in_specs = [a_spec, b_spec]
    operands = [a, b]
    if bias is not None:
        in_specs.append(pl.BlockSpec((1, tn), lambda i, j, k: (0, j)))
        operands.append(bias)
    has_bias = bias is not None

    def body(*refs):
        a_ref, b_ref = refs[0], refs[1]
        bias_ref = refs[2] if has_bias else None
        o_ref = refs[3] if has_bias else refs[2]
        prod = lax.dot_general(a_ref[...].astype(BF16), b_ref[...].astype(BF16), dims,
                               preferred_element_type=F32)
        if nk == 1:
            if has_bias:
                prod = prod + bias_ref[...]
            o_ref[...] = prod.astype(o_ref.dtype)
        else:
            acc_ref = refs[-1]
            k = pl.program_id(2)

            @pl.when(k == 0)
            def _():
                acc_ref[...] = jnp.zeros_like(acc_ref)

            acc_ref[...] += prod

            @pl.when(k == nk - 1)
            def _():
                out = acc_ref[...]
                if has_bias:
                    out = out + bias_ref[...]
                o_ref[...] = out.astype(o_ref.dtype)

    return pl.pallas_call(
        body, name=name,
        grid=(M // tm, N // tn, nk),
        in_specs=in_specs,
        out_specs=pl.BlockSpec((tm, tn), lambda i, j, k: (i, j)),
        out_shape=jax.ShapeDtypeStruct((M, N), out_dtype),
        scratch_shapes=[pltpu.VMEM((tm, tn), F32)] if nk > 1 else [],
        compiler_params=_params(("parallel", "parallel", "arbitrary")),
    )(*operands)


def mm_fused(a, b, mode, name, tn, epi, epi_outs, pro=None, pro_rows=(), pro_vecs=(), pro_out=False, n_pro_sums=0,
             epi_rows=(), epi_vecs=()):
    M, K = a.shape
    N = b.shape[1] if mode == "nn" else b.shape[0]
    dims = (((1,), (0,)), ((), ())) if mode == "nn" else (((1,), (1,)), ((), ()))
    nj = N // tn
    row_bytes = 2 * (K * a.dtype.itemsize + sum(K * r.dtype.itemsize for r in pro_rows) + (2 * K if pro_out else 0)
                     + sum(w * r.dtype.itemsize for r, w in epi_rows)
                     + sum(w * jnp.dtype(dt).itemsize for w, dt in epi_outs)) + (2 * K if pro is not None else 0)
    fixed = 2 * K * tn * b.dtype.itemsize
    tm = next((t for t in (1024, 512, 256, 128) if M % t == 0 and t * row_bytes + fixed <= MM_VMEM_BUDGET), M)
    row = lambda i, j: (i, 0)
    tile = lambda i, j: (i, j)
    in_specs = [pl.BlockSpec((tm, K), row)] + [pl.BlockSpec((tm, K), row) for _ in pro_rows]
    in_specs += [pl.BlockSpec(v.shape, lambda i, j: (0, 0)) for v in pro_vecs]
    in_specs += [pl.BlockSpec((K, tn), lambda i, j: (0, j)) if mode == "nn" else pl.BlockSpec((tn, K), lambda i, j: (j, 0))]
    in_specs += [pl.BlockSpec((tm, w), tile) for _, w in epi_rows]
    in_specs += [pl.BlockSpec((1, tn), lambda i, j: (0, j)) for _ in epi_vecs]
    out_specs, out_shape = [], []
    if pro_out:
        out_specs.append(pl.BlockSpec((tm, K), row))
        out_shape.append(jax.ShapeDtypeStruct((M, K), BF16))
    for _ in range(n_pro_sums):
        out_specs.append(pl.BlockSpec((1, K), lambda i, j: (0, 0)))
        out_shape.append(jax.ShapeDtypeStruct((1, K), F32))
    for w, dt in epi_outs:
        out_specs.append(pl.BlockSpec((tm, w), tile))
        out_shape.append(jax.ShapeDtypeStruct((M, nj * w), dt))
    n_pr, n_pv, n_er, n_ev = len(pro_rows), len(pro_vecs), len(epi_rows), len(epi_vecs)
    n_in = 1 + n_pr + n_pv + 1 + n_er + n_ev
    n_po = 1 if pro_out else 0

    def body(*refs):
        i, j = pl.program_id(0), pl.program_id(1)
        a_ref = refs[0]
        b_ref = refs[1 + n_pr + n_pv]
        outs = refs[n_in:]
        if pro is not None:
            lhs_ref = refs[-1]

            @pl.when(j == 0)
            def _():
                res = pro(*[r[...] for r in refs[:1 + n_pr + n_pv]])
                if not isinstance(res, (tuple, list)):
                    res = (res,)
                lhs_ref[...] = res[0]
                if pro_out:
                    outs[0][...] = res[0]
                for s_ref, val in zip(outs[n_po:n_po + n_pro_sums], res[1:]):
                    part = jnp.sum(val.astype(F32), axis=0, keepdims=True)

                    @pl.when(i == 0)
                    def _(s_ref=s_ref, part=part):
                        s_ref[...] = part

                    @pl.when(i != 0)
                    def _(s_ref=s_ref, part=part):
                        s_ref[...] += part

            lhs = lhs_ref[...]
        else:
            lhs = a_ref[...].astype(BF16)
        acc = lax.dot_general(lhs, b_ref[...].astype(BF16), dims, preferred_element_type=F32)
        res = epi(acc, *[r[...] for r in refs[2 + n_pr + n_pv:n_in]])
        if not isinstance(res, (tuple, list)):
            res = (res,)
        for o_ref, val in zip(outs[n_po + n_pro_sums:], res):
            o_ref[...] = val.astype(o_ref.dtype)

    res = pl.pallas_call(
        body, name=name,
        grid=(M // tm, nj),
        in_specs=in_specs, out_specs=out_specs, out_shape=out_shape,
        scratch_shapes=[pltpu.VMEM((tm, K), BF16)] if pro is not None else [],
        compiler_params=_params(("arbitrary", "arbitrary")),
    )(a, *pro_rows, *pro_vecs, b, *[r for r, _ in epi_rows], *epi_vecs)
    return res


def rowwise(fn, rows, vecs, outs, sums, name, tm=None):
    norm = [(r, r.shape[1], 0) if not isinstance(r, tuple) else r for r in rows]
    S = norm[0][0].shape[0]
    if tm is None:
        per_row = sum(w * r.dtype.itemsize for r, w, _ in norm) + sum(n * jnp.dtype(dt).itemsize for n, dt in outs)
        tm = S
        for t in (512, 256, 128, 64, 32, 16, 8):
            if S % t == 0:
                tm = t
                if t * per_row <= ROW_TILE_BUDGET:
                    break
    n_rows, n_vecs, n_outs, n_sums = len(norm), len(vecs), len(outs), len(sums)
    in_specs = [pl.BlockSpec((tm, w), lambda i, cb=cb: (i, cb)) for _, w, cb in norm]
    in_specs += [pl.BlockSpec(v.shape, lambda i: (0, 0)) for v in vecs]
    out_specs = [pl.BlockSpec((tm, n), lambda i: (i, 0)) for n, _ in outs]
    out_specs += [pl.BlockSpec((1, n), lambda i: (0, 0)) for n in sums]
    out_shape = [jax.ShapeDtypeStruct((S, n), dt) for n, dt in outs]
    out_shape += [jax.ShapeDtypeStruct((1, n), F32) for n in sums]

    def body(*refs):
        ins = [r[...] for r in refs[:n_rows + n_vecs]]
        res = fn(*ins)
        if not isinstance(res, (tuple, list)):
            res = (res,)
        out_refs = refs[n_rows + n_vecs:]
        for o_ref, val in zip(out_refs[:n_outs], res[:n_outs]):
            o_ref[...] = val.astype(o_ref.dtype)
        if n_sums:
            i = pl.program_id(0)
            for s_ref, val in zip(out_refs[n_outs:], res[n_outs:]):
                part = jnp.sum(val.astype(F32), axis=0, keepdims=True)

                @pl.when(i == 0)
                def _(s_ref=s_ref, part=part):
                    s_ref[...] = part

                @pl.when(i != 0)
                def _(s_ref=s_ref, part=part):
                    s_ref[...] += part

    res = pl.pallas_call(
        body, name=name,
        grid=(S // tm,),
        in_specs=in_specs, out_specs=out_specs, out_shape=out_shape,
        compiler_params=_params(("arbitrary",) if n_sums else ("parallel",)),
    )(*[r for r, _, _ in norm], *vecs)
    return res


def _sigmoid(x):
    return jax.nn.sigmoid(x)


def _rms(x):
    r = lax.rsqrt(jnp.mean(x * x, axis=-1, keepdims=True) + EPS)
    return x * r, r


def _rms_bwd(xhat, r, dxhat):
    return r * (dxhat - xhat * jnp.mean(dxhat * xhat, axis=-1, keepdims=True))


def norm_mod(h, g, sh, sc, name):
    def f(h, g, sh, sc):
        xhat, _ = _rms(h)
        return ((xhat * g) * (1 + sc) + sh).astype(BF16)
    return rowwise(f, [h], [g, sh, sc], [(h.shape[1], BF16)], [], name)[0]


def norm_mod_bwd(h, dhn, dh_out, g, sc, name):
    D = h.shape[1]
    with_res = dh_out is not None

    def f(*a):
        if with_res:
            h, dhn, dres, g, sc = a
        else:
            h, dhn, g, sc = a
        xhat, r = _rms(h)
        xn = xhat * g
        dxn = dhn * (1 + sc)
        dh = _rms_bwd(xhat, r, dxn * g)
        if with_res:
            dh = dh + dres
        return dh, dhn, dhn * xn, dxn * xhat

    rows = [h, dhn] + ([dh_out] if with_res else [])
    return rowwise(f, rows, [g, sc], [(D, F32)], [D, D, D], name)


def residual(h, y, gate, coef, name, bias=None):
    D = h.shape[1]
    if bias is None:
        def f(h, y, gate):
            return h + (coef * gate) * y
        return rowwise(f, [h, y], [gate], [(D, F32)], [], name)[0], y

    def fb(h, y, gate, bias):
        yb = y + bias
        return h + (coef * gate) * yb, yb
    return rowwise(fb, [h, y], [gate, bias], [(D, F32), (D, F32)], [], name)


def residual_bwd(dh_out, y, gate, coef, name, with_bias_sum=False):
    D = y.shape[1]

    def f(dh, y, gate):
        dy = (coef * gate) * dh
        res = (dy.astype(BF16), coef * dh * y)
        return res + ((dy,) if with_bias_sum else ())
    return rowwise(f, [dh_out, y], [gate], [(D, BF16)], [D, D] if with_bias_sum else [D], name)


def _ffn_chunk(F):
    return _tile(F, (256, 128))


def pair_w13(w13, F):
    cf = _ffn_chunk(F)
    lead = w13.shape[:-1]
    n = len(lead)
    return w13.reshape(*lead, 2, F // cf, cf).swapaxes(n, n + 1).reshape(*lead, 2 * F)


def unpair_w13(w13p, F):
    cf = _ffn_chunk(F)
    lead = w13p.shape[:-1]
    n = len(lead)
    return w13p.reshape(*lead, F // cf, 2, cf).swapaxes(n, n + 1).reshape(*lead, 2 * F)


def ffn_fwd(h, g, sh, sc, gate, w13p, w2):
    F, D = w2.shape
    cf = _ffn_chunk(F)

    def norm(h, g, sh, sc):
        xhat, _ = _rms(h)
        return ((xhat * g) * (1 + sc) + sh).astype(BF16)

    def act(acc):
        a, b = acc[:, :cf], acc[:, cf:]
        return acc, (a * _sigmoid(a)) * b
    hn, ab, t = mm_fused(h, w13p, "nn", "ffn_w13", 2 * cf, act, [(2 * cf, F32), (cf, BF16)],
                         pro=norm, pro_vecs=[g, sh, sc], pro_out=True)

    def res(acc, h, gate):
        return h + (0.5 * gate) * acc, acc
    h_out, y = mm_fused(t, w2, "nn", "ffn_w2", D, res, [(D, F32), (D, F32)], epi_rows=[(h, D)], epi_vecs=[gate])
    return h_out, (h, hn, ab, y)


def ffn_bwd(dh_out, saved, g, sc, gate, w13p, w2):
    h, hn, ab, y = saved
    F, D = w2.shape
    cf = _ffn_chunk(F)

    def scale(dh, y, gate):
        return ((0.5 * gate) * dh).astype(BF16), 0.5 * dh * y

    def act_bwd(dt, ab):
        a, b = ab[:, :cf], ab[:, cf:]
        sig = _sigmoid(a)
        sa = a * sig
        da = dt * b * (sig * (1 + a * (1 - sig)))
        db = dt * sa
        return sa * b, jnp.concatenate([da, db], axis=1)
    dy, d_gate, t, dab = mm_fused(dh_out, w2, "nt", "ffn_w2_dx", cf, act_bwd, [(cf, BF16), (2 * cf, BF16)],
                                  pro=scale, pro_rows=[y], pro_vecs=[gate], pro_out=True, n_pro_sums=1,
                                  epi_rows=[(ab, 2 * cf)])
    dw2 = mm(t, dy, "tn", "ffn_w2_dw")
    dw13 = mm(hn, dab, "tn", "ffn_w13_dw")
    dhn = mm(dab, w13p, "nt", "ffn_w13_dx")
    dh_in, d_sh, d_sc, d_g = norm_mod_bwd(h, dhn, dh_out, g, sc, "norm_mod_bwd")
    return dh_in, (d_sh, d_sc, d_gate, d_g), dw13, dw2


def _shifted(xbuf, n):
    return [xbuf] + [pltpu.roll(xbuf, n - b, 0) for b in range(1, 8)]


def conv_fwd(u, w_dw, b_dw, ln_g, ln_b):
    S, D = u.shape
    tm = _tile(S, (256, 128))
    rc = 32
    first_tap = CONV_HALO - (CONV_WIDTH - 1)
    w = jnp.concatenate([w_dw, jnp.zeros((CONV_HALO - CONV_WIDTH, D), F32)], axis=0)

    def body(cur_ref, prev_ref, w_ref, b_ref, g_ref, beta_ref, z_ref, s_ref):
        i = pl.program_id(0)
        prev = jnp.where(i == 0, jnp.zeros((CONV_HALO, D), F32), prev_ref[...])
        xs = _shifted(jnp.concatenate([prev, cur_ref[...]], axis=0), tm + CONV_HALO)
        for c0 in range(0, tm, rc):
            acc = jnp.zeros((rc, D), F32)
            for k in range(CONV_WIDTH):
                off = first_tap + k
                a8, b = off // 8 * 8, off % 8
                acc = acc + w_ref[k:k + 1, :] * xs[b][c0 + a8:c0 + a8 + rc, :]
            z_ref[c0:c0 + rc, :] = acc + b_ref[...]
        z = z_ref[...]
        mu = jnp.mean(z, axis=-1, keepdims=True)
        zc = z - mu
        r = lax.rsqrt(jnp.mean(zc * zc, axis=-1, keepdims=True) + EPS)
        un = zc * r * g_ref[...] + beta_ref[...]
        s_ref[...] = (un * _sigmoid(un)).astype(BF16)

    nb = tm // CONV_HALO
    vec = pl.BlockSpec((1, D), lambda i: (0, 0))
    return pl.pallas_call(
        body, name="conv_fwd",
        grid=(S // tm,),
        in_specs=[pl.BlockSpec((tm, D), lambda i: (i, 0)),
                  pl.BlockSpec((CONV_HALO, D), lambda i: (jnp.maximum(i * nb - 1, 0), 0)),
                  pl.BlockSpec((CONV_HALO, D), lambda i: (0, 0)), vec, vec, vec],
        out_specs=[pl.BlockSpec((tm, D), lambda i: (i, 0)), pl.BlockSpec((tm, D), lambda i: (i, 0))],
        out_shape=[jax.ShapeDtypeStruct((S, D), F32), jax.ShapeDtypeStruct((S, D), BF16)],
        compiler_params=_params(("parallel",)),
    )(u, u, w, b_dw, ln_g, ln_b)


def conv_bwd(dz, u, w_dw):
    S, D = u.shape
    tm = _tile(S, (256, 128))
    rc = 32
    first_tap = CONV_HALO - (CONV_WIDTH - 1)
    w = jnp.concatenate([w_dw, jnp.zeros((CONV_HALO - CONV_WIDTH, D), F32)], axis=0)
    n_tiles = S // tm
    nb = tm // CONV_HALO

    def body(dz_ref, dzn_ref, u_ref, up_ref, w_ref, du_ref, dw_ref):
        i = pl.program_id(0)
        nxt = jnp.where(i == n_tiles - 1, jnp.zeros((CONV_HALO, D), F32), dzn_ref[...])
        dzs = _shifted(jnp.concatenate([dz_ref[...], nxt], axis=0), tm + CONV_HALO)
        for c0 in range(0, tm, rc):
            acc = jnp.zeros((rc, D), F32)
            for m in range(CONV_WIDTH):
                a8, b = m // 8 * 8, m % 8
                acc = acc + w_ref[CONV_WIDTH - 1 - m:CONV_WIDTH - m, :] * dzs[b][c0 + a8:c0 + a8 + rc, :]
            du_ref[c0:c0 + rc, :] = acc
        prev = jnp.where(i == 0, jnp.zeros((CONV_HALO, D), F32), up_ref[...])
        us = _shifted(jnp.concatenate([prev, u_ref[...]], axis=0), tm + CONV_HALO)
        dz = dz_ref[...]

        @pl.when(i == 0)
        def _():
            dw_ref[...] = jnp.zeros_like(dw_ref)

        for k in range(CONV_WIDTH):
            off = first_tap + k
            a8, b = off // 8 * 8, off % 8
            dw_ref[k:k + 1, :] += jnp.sum(dz * us[b][a8:a8 + tm, :], axis=0, keepdims=True)

    last_blk = S // CONV_HALO - 1
    du, dw = pl.pallas_call(
        body, name="conv_bwd",
        grid=(n_tiles,),
        in_specs=[pl.BlockSpec((tm, D), lambda i: (i, 0)),
                  pl.BlockSpec((CONV_HALO, D), lambda i: (jnp.minimum((i + 1) * nb, last_blk), 0)),
                  pl.BlockSpec((tm, D), lambda i: (i, 0)),
                  pl.BlockSpec((CONV_HALO, D), lambda i: (jnp.maximum(i * nb - 1, 0), 0)),
                  pl.BlockSpec((CONV_HALO, D), lambda i: (0, 0))],
        out_specs=[pl.BlockSpec((tm, D), lambda i: (i, 0)), pl.BlockSpec((CONV_HALO, D), lambda i: (0, 0))],
        out_shape=[jax.ShapeDtypeStruct((S, D), F32), jax.ShapeDtypeStruct((CONV_HALO, D), F32)],
        compiler_params=_params(("arbitrary",)),
    )(dz, dz, u, u, w)
    return du, dw[:CONV_WIDTH]


def conv_module_fwd(h, g, sh, sc, gate, p):
    D = h.shape[1]
    hn = norm_mod(h, g, sh, sc, "conv_norm_mod")
    pre = mm(hn, p["w_pw1"], "nn", "conv_pw1")
    ba, bg = p["b_pw1"][:, :D], p["b_pw1"][:, D:]

    def glu(a, gt, ba, bg):
        return (a + ba) * _sigmoid(gt + bg)
    u = rowwise(glu, [(pre, D, 0), (pre, D, 1)], [ba, bg], [(D, F32)], [], "conv_glu")[0]
    z, s = conv_fwd(u, p["w_dw"], p["b_dw"], p["ln_g"], p["ln_b"])
    yraw = mm(s, p["w_pw2"], "nn", "conv_pw2")
    h_out, y = residual(h, yraw, gate, 1.0, "conv_residual", bias=p["b_pw2"])
    return h_out, (h, hn, pre, u, z, s, y)


def conv_module_bwd(dh_out, saved, g, sc, gate, p):
    h, hn, pre, u, z, s, y = saved
    D = h.shape[1]
    dy, d_gate, d_b_pw2 = residual_bwd(dh_out, y, gate, 1.0, "conv_residual_bwd", with_bias_sum=True)
    d_w_pw2 = mm(s, dy, "tn", "conv_pw2_dw")
    ds = mm(dy, p["w_pw2"], "nt", "conv_pw2_dx")

    def ln_bwd(z, ds, g, beta):
        mu = jnp.mean(z, axis=-1, keepdims=True)
        zc = z - mu
        r = lax.rsqrt(jnp.mean(zc * zc, axis=-1, keepdims=True) + EPS)
        xhat = zc * r
        un = xhat * g + beta
        sig = _sigmoid(un)
        d_un = ds * (sig * (1 + un * (1 - sig)))
        dxhat = d_un * g
        dz = r * (dxhat - jnp.mean(dxhat, axis=-1, keepdims=True)
                  - xhat * jnp.mean(dxhat * xhat, axis=-1, keepdims=True))
        return dz, d_un * xhat, d_un, dz
    dz, d_ln_g, d_ln_b, d_b_dw = rowwise(ln_bwd, [z, ds], [p["ln_g"], p["ln_b"]], [(D, F32)], [D, D, D],
                                         "conv_ln_bwd")
    du, d_w_dw = conv_bwd(dz, u, p["w_dw"])
    ba, bg = p["b_pw1"][:, :D], p["b_pw1"][:, D:]

    def glu_bwd(a, gt, du, ba, bg):
        sg = _sigmoid(gt + bg)
        da = du * sg
        dg = du * (a + ba) * (sg * (1 - sg))
        dpre = jnp.concatenate([da, dg], axis=1)
        return dpre.astype(BF16), dpre
    dpre, d_b_pw1 = rowwise(glu_bwd, [(pre, D, 0), (pre, D, 1), du], [ba, bg], [(2 * D, BF16)], [2 * D],
                            "conv_glu_bwd")
    d_w_pw1 = mm(hn, dpre, "tn", "conv_pw1_dw")
    dhn = mm(dpre, p["w_pw1"], "nt", "conv_pw1_dx")
    dh_in, d_sh, d_sc, d_g = norm_mod_bwd(h, dhn, dh_out, g, sc, "norm_mod_bwd")
    grads = dict(w_pw1=d_w_pw1, b_pw1=d_b_pw1, w_dw=d_w_dw, b_dw=d_b_dw, ln_g=d_ln_g, ln_b=d_ln_b,
                 w_pw2=d_w_pw2, b_pw2=d_b_pw2)
    return dh_in, (d_sh, d_sc, d_gate, d_g), grads


def _rope(x, c, s1, s2):
    n = x.shape[1]
    return x * c + pltpu.roll(x, n - QK_ROPE // 2, 1) * s1 + pltpu.roll(x, QK_ROPE // 2, 1) * s2


def _rope_t(dy, c, s1, s2):
    n = dy.shape[1]
    return dy * c + pltpu.roll(dy * s1, QK_ROPE // 2, 1) + pltpu.roll(dy * s2, n - QK_ROPE // 2, 1)


def rope_tables(positions):
    inv_freq = ROPE_THETA ** (-jnp.arange(0, QK_ROPE, 2, dtype=F32) / QK_ROPE)
    ang = positions.astype(F32)[:, None] * inv_freq
    cos, sin = jnp.cos(ang), jnp.sin(ang)
    S = positions.shape[0]
    one = jnp.ones((S, QK_NOPE), F32)
    z16 = jnp.zeros((S, QK_ROPE // 2), F32)
    zn = jnp.zeros((S, QK_NOPE), F32)
    zt = jnp.zeros((S, HEAD_PAD - QK_NOPE - QK_ROPE), F32)
    c = jnp.concatenate([one, cos, cos, zt], axis=1)
    s1 = jnp.concatenate([zn, -sin, z16, zt], axis=1)
    s2 = jnp.concatenate([zn, z16, sin, zt], axis=1)
    return c, s1, s2


def attn_fwd(qr, kv, kpe, n_heads):
    S = qr.shape[0]
    H = n_heads
    tq = _tile(S, (ATTN_TILE,))
    nq = S // tq
    c2 = (QK_NOPE + QK_ROPE) ** -0.5 * LOG2_E
    nt = (((1,), (1,)), ((), ()))
    tn = (((0,), (0,)), ((), ()))

    def lanes_to_rows(row):
        return jnp.transpose(jnp.broadcast_to(row, (HEAD_PAD, tq)))

    def body(q_ref, k_ref, v_ref, kpe_ref, o_ref, lse_ref, kf_ref, m_ref, l_ref, acc_ref):
        qi = pl.program_id(1)

        @pl.when(qi == 0)
        def _():
            kf_ref[...] = k_ref[...] + kpe_ref[...]

        q = q_ref[...]
        m_ref[...] = jnp.full((1, tq), -jnp.inf, F32)
        l_ref[...] = jnp.zeros((1, tq), F32)
        acc_ref[...] = jnp.zeros((tq, HEAD_PAD), F32)

        def tile(j, masked):
            start = pl.multiple_of(j * tq, tq)
            k = kf_ref[pl.ds(start, tq), :]
            v = v_ref[pl.ds(start, tq), :]
            t = lax.dot_general(k, q, nt, preferred_element_type=F32) * c2
            if masked:
                krow = lax.broadcasted_iota(jnp.int32, (tq, tq), 0)
                qcol = lax.broadcasted_iota(jnp.int32, (tq, tq), 1)
                t = jnp.where(krow <= qcol, t, NEG)
            m_old = m_ref[...]
            m_new = jnp.maximum(m_old, jnp.max(t, axis=0, keepdims=True))
            alpha = jnp.exp2(m_old - m_new)
            p = jnp.exp2(t - m_new)
            l_ref[...] = alpha * l_ref[...] + jnp.sum(p, axis=0, keepdims=True)
            pv = lax.dot_general(p.astype(BF16), v, tn, preferred_element_type=F32)
            acc_ref[...] = lanes_to_rows(alpha) * acc_ref[...] + pv
            m_ref[...] = m_new

        def unmasked(j, carry):
            tile(j, False)
            return carry

        lax.fori_loop(0, qi, unmasked, 0)
        tile(qi, True)
        l = l_ref[...]
        o_ref[...] = acc_ref[...] / lanes_to_rows(l)
        lse_ref[...] = m_ref[...] + jnp.log(l) * LOG2_E

    return pl.pallas_call(
        body, name="attn_fwd",
        grid=(H, nq),
        in_specs=[pl.BlockSpec((tq, HEAD_PAD), lambda h, i: (i, h)),
                  pl.BlockSpec((S, HEAD_PAD), lambda h, i: (0, h)),
                  pl.BlockSpec((S, HEAD_PAD), lambda h, i: (0, H + h)),
                  pl.BlockSpec((S, HEAD_PAD), lambda h, i: (0, 0))],
        out_specs=[pl.BlockSpec((tq, HEAD_PAD), lambda h, i: (i, h)),
                   pl.BlockSpec((None, None, 1, tq), lambda h, i: (h, i, 0, 0))],
        out_shape=[jax.ShapeDtypeStruct((S, H * HEAD_PAD), F32), jax.ShapeDtypeStruct((H, nq, 1, tq), F32)],
        scratch_shapes=[pltpu.VMEM((S, HEAD_PAD), BF16), pltpu.VMEM((1, tq), F32), pltpu.VMEM((1, tq), F32),
                        pltpu.VMEM((tq, HEAD_PAD), F32)],
        compiler_params=_params(("parallel", "arbitrary")),
    )(qr, kv, kv, kpe)


def attn_delta(o, do, n_heads):
    S = o.shape[0]
    H = n_heads
    tq = _tile(S, (ATTN_TILE,))

    def body(o_ref, do_ref, d_ref):
        d_ref[...] = jnp.sum(o_ref[...] * do_ref[...].astype(F32), axis=1, keepdims=True)

    return pl.pallas_call(
        body, name="attn_delta",
        grid=(H, S // tq),
        in_specs=[pl.BlockSpec((tq, HEAD_PAD), lambda h, i: (i, h)),
                  pl.BlockSpec((tq, HEAD_PAD), lambda h, i: (i, h))],
        out_specs=pl.BlockSpec((None, tq, 1), lambda h, i: (h, i, 0)),
        out_shape=jax.ShapeDtypeStruct((H, S, 1), F32),
        compiler_params=_params(("parallel", "parallel")),
    )(o, do)


def attn_bwd(qr, kv, kpe, do, lse2, delta, n_heads):
    S = qr.shape[0]
    H = n_heads
    tq = _tile(S, (ATTN_TILE,))
    nq = S // tq
    scale = (QK_NOPE + QK_ROPE) ** -0.5
    c2 = scale * LOG2_E
    nt = (((1,), (1,)), ((), ()))
    tn = (((0,), (0,)), ((), ()))
    delta4 = delta.reshape(H, nq, 1, tq)

    def body(k_ref, v_ref, kpe_ref, q_ref, do_ref, lse_ref, dl_ref, dq_ref, dk_ref, dv_ref, dka_ref, dva_ref):
        kj = pl.program_id(1)
        k = k_ref[...] + kpe_ref[...]
        v = v_ref[...]

        @pl.when(kj == 0)
        def _():
            dq_ref[...] = jnp.zeros_like(dq_ref)

        dka_ref[...] = jnp.zeros_like(dka_ref)
        dva_ref[...] = jnp.zeros_like(dva_ref)

        def tile(i, masked):
            start = pl.multiple_of(i * tq, tq)
            q = q_ref[pl.ds(start, tq), :]
            do = do_ref[pl.ds(start, tq), :]
            t = lax.dot_general(k, q, nt, preferred_element_type=F32) * c2
            if masked:
                krow = lax.broadcasted_iota(jnp.int32, (tq, tq), 0)
                qcol = lax.broadcasted_iota(jnp.int32, (tq, tq), 1)
                t = jnp.where(krow <= qcol, t, NEG)
            pt = jnp.exp2(t - lse_ref[i])
            dva_ref[...] += jnp.dot(pt.astype(BF16), do, preferred_element_type=F32)
            dpt = lax.dot_general(v, do, nt, preferred_element_type=F32)
            dst = (pt * (dpt - dl_ref[i]) * scale).astype(BF16)
            dka_ref[...] += jnp.dot(dst, q, preferred_element_type=F32)
            dq_ref[pl.ds(start, tq), :] += lax.dot_general(dst, k, tn, preferred_element_type=F32)

        tile(kj, True)

        def unmasked(i, carry):
            tile(i, False)
            return carry

        lax.fori_loop(kj + 1, nq, unmasked, 0)
        dk_ref[...] = dka_ref[...]
        dv_ref[...] = dva_ref[...]

    blk = pl.BlockSpec((tq, HEAD_PAD), lambda h, j: (j, h))
    whole = pl.BlockSpec((S, HEAD_PAD), lambda h, j: (0, h))
    stat = pl.BlockSpec((None, nq, 1, tq), lambda h, j: (h, 0, 0, 0))
    shp = jax.ShapeDtypeStruct((S, H * HEAD_PAD), F32)
    return pl.pallas_call(
        body, name="attn_bwd",
        grid=(H, nq),
        in_specs=[blk, pl.BlockSpec((tq, HEAD_PAD), lambda h, j: (j, H + h)),
                  pl.BlockSpec((tq, HEAD_PAD), lambda h, j: (j, 0)), whole, whole, stat, stat],
        out_specs=[whole, blk, blk],
        out_shape=[shp, shp, shp],
        scratch_shapes=[pltpu.VMEM((tq, HEAD_PAD), F32), pltpu.VMEM((tq, HEAD_PAD), F32)],
        compiler_params=_params(("parallel", "arbitrary")),
    )(kv, kv, kpe, qr, do, lse2, delta4)


def _pad_heads(w, width):
    R = w.shape[0]
    w3 = w.reshape(R, -1, width)
    return jnp.pad(w3, ((0, 0), (0, 0), (0, HEAD_PAD - width))).reshape(R, -1)


def _unpad_heads(w, width):
    R = w.shape[0]
    return w.reshape(R, -1, HEAD_PAD)[:, :, :width].reshape(R, -1)


def mla_pad_weights(p):
    H = N_HEADS
    w_q_b = _pad_heads(p["w_q_b"], QK_NOPE + QK_ROPE)
    kvb = p["w_kv_b"].reshape(KV_LORA, H, QK_NOPE + V_HEAD)
    wk = _pad_heads(kvb[:, :, :QK_NOPE].reshape(KV_LORA, -1), QK_NOPE)
    wv = _pad_heads(kvb[:, :, QK_NOPE:].reshape(KV_LORA, -1), V_HEAD)
    D = p["w_kv_a"].shape[0]
    a = p["w_kv_a"]
    w_kv_a = jnp.concatenate([a[:, :KV_LORA], jnp.zeros((D, QK_NOPE), a.dtype), a[:, KV_LORA:],
                              jnp.zeros((D, HEAD_PAD - QK_NOPE - QK_ROPE), a.dtype)], axis=1)
    wo = p["w_o"].reshape(H, V_HEAD, -1)
    w_o = jnp.pad(wo, ((0, 0), (0, HEAD_PAD - V_HEAD), (0, 0))).reshape(H * HEAD_PAD, -1)
    return dict(w_q_a=p["w_q_a"], w_q_b=w_q_b, w_kv_b=jnp.concatenate([wk, wv], axis=1), w_kv_a=w_kv_a, w_o=w_o)


def mla_kv_fwd(h, g, sh, sc, kv_a_norm_g, pw, tabs):
    hkv = norm_mod(h, g, sh, sc, "kv_norm_mod")
    ckvp = mm(hkv, pw["w_kv_a"], "nn", "kv_a")

    def f(ckv, kpe, c, s1, s2, g):
        xhat, _ = _rms(ckv)
        return (xhat * g).astype(BF16), _rope(kpe, c, s1, s2).astype(BF16)
    ckv_n, kpe_r = rowwise(f, [(ckvp, KV_LORA, 0), (ckvp, HEAD_PAD, KV_LORA // HEAD_PAD), *tabs], [kv_a_norm_g],
                           [(KV_LORA, BF16), (HEAD_PAD, BF16)], [], "kv_a_norm_rope")
    kv = mm(ckv_n, pw["w_kv_b"], "nn", "kv_b", out_dtype=BF16)
    return kv, kpe_r, (h, hkv, ckvp, ckv_n)


def mla_kv_bwd(dk, dv, saved, g, sc, kv_a_norm_g, pw, tabs):
    h, hkv, ckvp, ckv_n = saved
    H = N_HEADS
    lane = jnp.arange(HEAD_PAD)
    pe_mask = ((lane >= QK_NOPE) & (lane < QK_NOPE + QK_ROPE)).astype(F32)[None, :]

    def f(dk, dv, c, s1, s2, mask):
        tot = dk[:, :HEAD_PAD]
        for hh in range(1, H):
            tot = tot + dk[:, hh * HEAD_PAD:(hh + 1) * HEAD_PAD]
        dkpe = _rope_t(tot * mask, c, s1, s2) * mask
        return jnp.concatenate([dk, dv], axis=1).astype(BF16), dkpe
    dkv, dkpe = rowwise(f, [dk, dv, *tabs], [pe_mask], [(2 * H * HEAD_PAD, BF16), (HEAD_PAD, F32)], [],
                        "kv_split_bwd")
    d_w_kv_b = mm(ckv_n, dkv, "tn", "kv_b_dw")
    dckv_n = mm(dkv, pw["w_kv_b"], "nt", "kv_b_dx")

    def f2(ckv, dn, dkpe, g):
        xhat, r = _rms(ckv)
        dx = _rms_bwd(xhat, r, dn * g)
        return jnp.concatenate([dx, dkpe], axis=1).astype(BF16), dn * xhat
    dckvp, d_kv_a_g = rowwise(f2, [(ckvp, KV_LORA, 0), dckv_n, dkpe], [kv_a_norm_g],
                              [(KV_LORA + HEAD_PAD, BF16)], [KV_LORA], "kv_a_norm_bwd")
    d_w_kv_a = mm(hkv, dckvp, "tn", "kv_a_dw")
    dhkv = mm(dckvp, pw["w_kv_a"], "nt", "kv_a_dx")
    dh, d_sh, d_sc, d_g = norm_mod_bwd(h, dhkv, None, g, sc, "norm_mod_bwd_nores")
    return dh, (d_sh, d_sc, d_g), d_kv_a_g, d_w_kv_a, d_w_kv_b


def mla_fwd(h, g, sh, sc, gate, q_a_norm_g, pw, kv, kpe_r, tabs):
    H = N_HEADS
    hn = norm_mod(h, g, sh, sc, "mla_norm_mod")
    qa = mm(hn, pw["w_q_a"], "nn", "q_a")

    def f(qa, g):
        xhat, _ = _rms(qa)
        return (xhat * g).astype(BF16)
    qa_n = rowwise(f, [qa], [q_a_norm_g], [(qa.shape[1], BF16)], [], "q_a_norm")[0]
    qp = mm(qa_n, pw["w_q_b"], "nn", "q_b")

    def frope(q, c, s1, s2):
        return jnp.concatenate([_rope(q[:, hh * HEAD_PAD:(hh + 1) * HEAD_PAD], c, s1, s2) for hh in range(H)],
                               axis=1).astype(BF16)
    qr = rowwise(frope, [qp, *tabs], [], [(H * HEAD_PAD, BF16)], [], "q_rope")[0]
    o, lse = attn_fwd(qr, kv, kpe_r, H)
    y = mm(o, pw["w_o"], "nn", "w_o")
    h_out, _ = residual(h, y, gate, 1.0, "mla_residual")
    return h_out, (h, hn, qa, qa_n, qr, o, lse, y)


def mla_bwd(dh_out, saved, g, sc, gate, q_a_norm_g, pw, kv, kpe_r, tabs):
    h, hn, qa, qa_n, qr, o, lse, y = saved
    H = N_HEADS
    dy, d_gate = residual_bwd(dh_out, y, gate, 1.0, "mla_residual_bwd")
    d_w_o = mm(o, dy, "tn", "w_o_dw")
    do = mm(dy, pw["w_o"], "nt", "w_o_dx", out_dtype=BF16)
    delta = attn_delta(o, do, H)
    dqr, dk, dv = attn_bwd(qr, kv, kpe_r, do, lse, delta, H)

    def frope_t(dq, c, s1, s2):
        return jnp.concatenate([_rope_t(dq[:, hh * HEAD_PAD:(hh + 1) * HEAD_PAD], c, s1, s2) for hh in range(H)],
                               axis=1).astype(BF16)
    dqp = rowwise(frope_t, [dqr, *tabs], [], [(H * HEAD_PAD, BF16)], [], "q_rope_bwd")[0]
    d_w_q_b = mm(qa_n, dqp, "tn", "q_b_dw")
    dqa_n = mm(dqp, pw["w_q_b"], "nt", "q_b_dx")

    def f(qa, dn, g):
        xhat, r = _rms(qa)
        return _rms_bwd(xhat, r, dn * g).astype(BF16), dn * xhat
    dqa, d_q_a_g = rowwise(f, [qa, dqa_n], [q_a_norm_g], [(qa.shape[1], BF16)], [qa.shape[1]], "q_a_norm_bwd")
    d_w_q_a = mm(hn, dqa, "tn", "q_a_dw")
    dhn = mm(dqa, pw["w_q_a"], "nt", "q_a_dx")
    dh_in, d_sh, d_sc, d_g = norm_mod_bwd(h, dhn, dh_out, g, sc, "norm_mod_bwd")
    grads = dict(w_q_a=d_w_q_a, q_a_norm_g=d_q_a_g, w_q_b=d_w_q_b, w_o=d_w_o)
    return dh_in, (d_sh, d_sc, d_gate, d_g), grads, dk, dv


def loss_head(h, target, g):
    D = h.shape[1]

    def f(h, t, g):
        xhat, r = _rms(h)
        err = xhat * g - t
        dy = err * (1.0 / D)
        dh = _rms_bwd(xhat, r, dy * g)
        return dh, (0.5 / D) * err * err, dy * xhat
    return rowwise(f, [h, target], [g], [(D, F32)], [D, D], "loss_head")


def _place():
    x, y, c = lax.axis_index("x"), lax.axis_index("y"), lax.axis_index("c")
    chips = [(1 - x, y), (x, 1 - y), (1 - x, 1 - y)]
    return x, y, c, chips


HBM_SPEC = pl.BlockSpec(memory_space=pltpu.HBM)


def all_gather8(v):
    m, n = v.shape

    def body(x_ref, out_ref, send_sems, recv_sems, local_sem):
        x, y, c, chips = _place()
        me, sibling = (x, y, c), (x, y, 1 - c)

        def rows(px, py, pc):
            return out_ref.at[4 * px + 2 * py + pc]

        def copy(k, block, to, src=None):
            return pltpu.make_async_remote_copy(
                src_ref=rows(*block) if src is None else src, dst_ref=rows(*block),
                send_sem=send_sems.at[k], recv_sem=recv_sems.at[k], device_id=to, device_id_type=MESH)

        mine = pltpu.make_async_copy(x_ref, rows(*me), local_sem)
        mine.start()
        first = [copy(0, me, sibling, src=x_ref)]
        first += [copy(1 + j, me, (*chip, c), src=x_ref) for j, chip in enumerate(chips)]
        for cp in first:
            cp.start()
        passed = [copy(4 + j, (*chip, c), sibling) for j, chip in enumerate(chips)]
        for j, chip in enumerate(chips):
            copy(1 + j, (*chip, c), me).wait_recv()
            passed[j].start()
        copy(0, sibling, me).wait_recv()
        for j, chip in enumerate(chips):
            copy(4 + j, (*chip, 1 - c), me).wait_recv()
        for cp in first + passed:
            cp.wait_send()
        mine.wait()

    return pl.pallas_call(
        body, name="all_gather8",
        out_shape=jax.ShapeDtypeStruct((8, m, n), v.dtype),
        in_specs=[pl.BlockSpec(memory_space=pltpu.VMEM)],
        out_specs=pl.BlockSpec(memory_space=pltpu.VMEM),
        scratch_shapes=[pltpu.SemaphoreType.DMA((7,)), pltpu.SemaphoreType.DMA((7,)), pltpu.SemaphoreType.DMA],
        compiler_params=pltpu.CompilerParams(vmem_limit_bytes=VMEM_LIMIT_BYTES),
    )(v)


def gather_weights(bufs):
    n = len(bufs)

    def body(*refs):
        ins, outs = refs[:n], refs[n:2 * n]
        send_sems, recv_sems = refs[2 * n:]
        x, y, c, chips = _place()
        sibling = (x, y, 1 - c)
        me = 2 * x + y

        def idx(chip):
            return 2 * chip[0] + chip[1]

        def copy(w, k, src, dst, to):
            return pltpu.make_async_remote_copy(src_ref=src, dst_ref=dst, send_sem=send_sems.at[6 * w + k],
                                                recv_sem=recv_sems.at[6 * w + k], device_id=to, device_id_type=MESH)

        first = [copy(w, j, ins[w].at[me, c], outs[w].at[me, c], (*chip, c))
                 for w in range(n) for j, chip in enumerate(chips)]
        for cp in first:
            cp.start()
        passed = []
        for w in range(n):
            for j, chip in enumerate(chips):
                landed = outs[w].at[idx(chip), c]
                copy(w, j, landed, landed, (*chip, c)).wait_recv()
                fwd = copy(w, 3 + j, landed, landed, sibling)
                fwd.start()
                passed.append(fwd)
        for w in range(n):
            for j, chip in enumerate(chips):
                other = outs[w].at[idx(chip), 1 - c]
                copy(w, 3 + j, other, other, sibling).wait_recv()
        for cp in first + passed:
            cp.wait_send()

    return pl.pallas_call(
        body, name="gather_weights",
        out_shape=[jax.ShapeDtypeStruct(b.shape, b.dtype) for b in bufs],
        in_specs=[HBM_SPEC] * n, out_specs=[HBM_SPEC] * n,
        input_output_aliases={w: w for w in range(n)},
        scratch_shapes=[pltpu.SemaphoreType.DMA((6 * n,)), pltpu.SemaphoreType.DMA((6 * n,))],
    )(*bufs)


def exchange_halves(gs):
    n = len(gs)

    def body(*refs):
        ins, theirs = refs[:n], refs[n:2 * n]
        send_sems, recv_sems = refs[2 * n:]
        x, y, c, _ = _place()
        sends = [pltpu.make_async_remote_copy(src_ref=ins[w].at[:, 1 - c], dst_ref=theirs[w],
                                              send_sem=send_sems.at[w], recv_sem=recv_sems.at[w],
                                              device_id=(x, y, 1 - c), device_id_type=MESH) for w in range(n)]
        for cp in sends:
            cp.start()
        for cp in sends:
            cp.wait()

    return pl.pallas_call(
        body, name="exchange_halves",
        out_shape=[jax.ShapeDtypeStruct((4,) + g.shape[2:], g.dtype) for g in gs],
        in_specs=[HBM_SPEC] * n, out_specs=[HBM_SPEC] * n,
        scratch_shapes=[pltpu.SemaphoreType.DMA((n,)), pltpu.SemaphoreType.DMA((n,))],
    )(*gs)


def scatter_blocks(ps):
    n = len(ps)

    def body(*refs):
        ins, outs = refs[:n], refs[n:2 * n]
        send_sems, recv_sems = refs[2 * n:]
        x, y, c, chips = _place()
        sends = [pltpu.make_async_remote_copy(src_ref=ins[w].at[2 * chip[0] + chip[1]], dst_ref=outs[w].at[j],
                                              send_sem=send_sems.at[3 * w + j], recv_sem=recv_sems.at[3 * w + j],
                                              device_id=(*chip, c), device_id_type=MESH)
                 for w in range(n) for j, chip in enumerate(chips)]
        for cp in sends:
            cp.start()
        for cp in sends:
            cp.wait()

    return pl.pallas_call(
        body, name="scatter_blocks",
        out_shape=[jax.ShapeDtypeStruct((3,) + p.shape[1:], p.dtype) for p in ps],
        in_specs=[HBM_SPEC] * n, out_specs=[HBM_SPEC] * n,
        scratch_shapes=[pltpu.SemaphoreType.DMA((3 * n,)), pltpu.SemaphoreType.DMA((3 * n,))],
    )(*ps)


def join_halves(qs):
    n = len(qs)

    def body(*refs):
        ins, outs = refs[:n], refs[n:2 * n]
        send_sems, recv_sems = refs[2 * n:]
        x, y, c, _ = _place()
        sends = [pltpu.make_async_remote_copy(src_ref=ins[w].at[c], dst_ref=outs[w].at[c], send_sem=send_sems.at[w],
                                              recv_sem=recv_sems.at[w], device_id=(x, y, 1 - c), device_id_type=MESH)
                 for w in range(n)]
        for cp in sends:
            cp.start()
        for w in range(n):
            other = outs[w].at[1 - c]
            pltpu.make_async_remote_copy(src_ref=other, dst_ref=other, send_sem=send_sems.at[w],
                                         recv_sem=recv_sems.at[w], device_id=(x, y, 1 - c),
                                         device_id_type=MESH).wait_recv()
        for cp in sends:
            cp.wait_send()

    return pl.pallas_call(
        body, name="join_halves",
        out_shape=[jax.ShapeDtypeStruct(q.shape, q.dtype) for q in qs],
        in_specs=[HBM_SPEC] * n, out_specs=[HBM_SPEC] * n,
        input_output_aliases={w: w for w in range(n)},
        scratch_shapes=[pltpu.SemaphoreType.DMA((n,)), pltpu.SemaphoreType.DMA((n,))],
    )(*qs)


def _row_tile(R, row_bytes):
    tm = R
    for t in (512, 256, 128, 64, 32, 16, 8):
        if R % t == 0:
            tm = t
            if t * row_bytes <= ROW_TILE_BUDGET:
                break
    return tm


def sum_siblings(g, theirs, place):
    _, _, R, C = g.shape
    tm = _row_tile(R, 3 * C * 4)

    def body(place_ref, a_ref, b_ref, o_ref):
        o_ref[...] = (a_ref[...] + b_ref[...]).astype(BF16)

    return pl.pallas_call(
        body, name="sum_siblings",
        grid_spec=pltpu.PrefetchScalarGridSpec(
            num_scalar_prefetch=1, grid=(4, R // tm),
            in_specs=[pl.BlockSpec((None, None, tm, C), lambda j, i, s: (j, s[1], i, 0)),
                      pl.BlockSpec((None, tm, C), lambda j, i, s: (j, i, 0))],
            out_specs=pl.BlockSpec((None, tm, C), lambda j, i, s: (j, i, 0))),
        out_shape=jax.ShapeDtypeStruct((4, R, C), BF16),
        compiler_params=_params(("parallel", "parallel")),
    )(place, g, theirs)


def sum_chips(p, landed, place):
    _, R, C = p.shape
    tm = _row_tile(R, 5 * C * 4)

    def body(place_ref, p_ref, l0_ref, l1_ref, l2_ref, o_ref):
        o_ref[...] = ((p_ref[...].astype(F32) + l0_ref[...].astype(F32)) + l1_ref[...].astype(F32)
                      ) + l2_ref[...].astype(F32)

    return pl.pallas_call(
        body, name="sum_chips",
        grid_spec=pltpu.PrefetchScalarGridSpec(
            num_scalar_prefetch=1, grid=(R // tm,),
            in_specs=[pl.BlockSpec((None, tm, C), lambda i, s: (s[0], i, 0))]
            + [pl.BlockSpec((None, tm, C), lambda i, s, j=j: (j, i, 0)) for j in range(3)],
            out_specs=pl.BlockSpec((None, tm, C), lambda i, s: (s[1], i, 0))),
        out_shape=jax.ShapeDtypeStruct((2, R, C), F32),
        compiler_params=_params(("parallel",)),
    )(place, p, landed, landed, landed)


def sum_blocks(items, name):
    R, C = items[0][0].shape[1:]
    tm = R
    for t in (512, 256, 128, 64, 32, 16, 8):
        if R % t == 0:
            tm = t
            if t * C * 4 * (len(items) + 1) <= ROW_TILE_BUDGET:
                break
    n = len(items)

    def body(*refs):
        acc = refs[0][...].astype(F32)
        for r in refs[1:n]:
            acc = acc + r[...].astype(F32)
        refs[n][...] = acc

    return pl.pallas_call(
        body, name=name,
        grid=(R // tm,),
        in_specs=[pl.BlockSpec((None, tm, C), lambda i, j=j: (j, i, 0)) for _, j in items],
        out_specs=pl.BlockSpec((tm, C), lambda i: (i, 0)),
        out_shape=jax.ShapeDtypeStruct((R, C), F32),
        compiler_params=_params(("parallel",)),
    )(*[a for a, _ in items])


def reduce_scatter_grads(gs, place):
    theirs = exchange_halves(gs)
    ps = [sum_siblings(g, t, place) for g, t in zip(gs, theirs)]
    landed = scatter_blocks(ps)
    qs = [sum_chips(p, l, place) for p, l in zip(ps, landed)]
    joined = join_halves(qs)
    return [j.reshape(2 * j.shape[1], j.shape[2]) for j in joined]


def adamw(w, g, m, v):
    shape = w.shape
    C = shape[-1]
    R = w.size // C
    tm = R
    for t in (512, 256, 128, 64, 32, 16, 8):
        if R % t == 0:
            tm = t
            if t * C * 4 * 7 <= ROW_TILE_BUDGET:
                break

    def f(w, g, m, v):
        m = ADAM_B1 * m + (1.0 - ADAM_B1) * g
        v = ADAM_B2 * v + (1.0 - ADAM_B2) * (g * g)
        m_hat = m / (1.0 - ADAM_B1 ** ADAM_STEP)
        v_hat = v / (1.0 - ADAM_B2 ** ADAM_STEP)
        delta = -ADAM_LR * (m_hat / (jnp.sqrt(v_hat) + ADAM_EPS) + ADAM_WD * w)
        return delta, m, v

    d, nm, nv = rowwise(f, [a.reshape(R, C) for a in (w, g, m, v)], [], [(C, F32)] * 3, [], "adamw", tm=tm)
    return d.reshape(shape), nm.reshape(shape), nv.reshape(shape)


def _cast_into_slot(w, place):
    C = w.shape[-1]
    w2 = w.reshape(-1, C)
    R = w2.shape[0]
    tm = _row_tile(R, 6 * C)

    def body(place_ref, w_ref, o_ref):
        o_ref[...] = w_ref[...].astype(BF16)

    out = pl.pallas_call(
        body, name="cast_bf16",
        grid_spec=pltpu.PrefetchScalarGridSpec(
            num_scalar_prefetch=1, grid=(R // tm,),
            in_specs=[pl.BlockSpec((tm, C), lambda i, s: (i, 0))],
            out_specs=pl.BlockSpec((None, tm, C), lambda i, s: (s[0], i, 0))),
        out_shape=jax.ShapeDtypeStruct((4, R, C), BF16),
        compiler_params=_params(("parallel",)),
    )(place, w2)
    return out.reshape(4, 2, R // 2, C)


def _pack(vs):
    flat = jnp.concatenate([v.reshape(-1) for v in vs])
    n = flat.shape[0]
    total = -(-n // 1024) * 1024
    return jnp.pad(flat, (0, total - n)).reshape(total // 128, 128)


def _unpack(flat, like):
    out, o = [], 0
    for shp in like:
        sz = 1
        for d in shp:
            sz *= d
        out.append(flat[o:o + sz].reshape(shp))
        o += sz
    return out


def _cols_to_blocks(g, n_chips=4):
    R, N = g.shape
    C = N // n_chips
    return g.reshape(R, n_chips, C).transpose(1, 0, 2).reshape(n_chips, 2, R // 2, C)


def _rows_to_blocks(g, n_chips=4):
    R, C = g.shape
    return g.reshape(n_chips, 2, R // n_chips // 2, C)


def kernel(x, c, positions, ada_w, ada_b, norm_g, ffn_w13, ffn_w2, conv_w_pw1, conv_b_pw1, conv_w_dw, conv_b_dw, conv_ln_g, conv_ln_b, conv_w_pw2, conv_b_pw2, kv_ada_w, kv_ada_b, kv_norm_g, w_kv_a, kv_a_norm_g, w_kv_b, w_q_a, q_a_norm_g, w_q_b, w_o, final_norm_g, loss_target, m_ada_w, m_ada_b, m_norm_g, m_ffn_w13, m_ffn_w2, m_conv_w_pw1, m_conv_b_pw1, m_conv_w_dw, m_conv_b_dw, m_conv_ln_g, m_conv_ln_b, m_conv_w_pw2, m_conv_b_pw2, m_kv_ada_w, m_kv_ada_b, m_kv_norm_g, m_w_kv_a, m_kv_a_norm_g, m_w_kv_b, m_w_q_a, m_q_a_norm_g, m_w_q_b, m_w_o, m_final_norm_g, v_ada_w, v_ada_b, v_norm_g, v_ffn_w13, v_ffn_w2, v_conv_w_pw1, v_conv_b_pw1, v_conv_w_dw, v_conv_b_dw, v_conv_ln_g, v_conv_ln_b, v_conv_w_pw2, v_conv_b_pw2, v_kv_ada_w, v_kv_ada_b, v_kv_norm_g, v_w_kv_a, v_kv_a_norm_g, v_w_kv_b, v_w_q_a, v_q_a_norm_g, v_w_q_b, v_w_o, v_final_norm_g):
    S, D = x.shape[1], x.shape[2]
    H = N_HEADS
    F = ffn_w2.shape[2] * 4
    xi, yi, ci = lax.axis_index("x"), lax.axis_index("y"), lax.axis_index("c")
    chip = 2 * xi + yi
    dev = 2 * chip + ci
    place = jnp.stack([chip, ci]).astype(jnp.int32)
    h0 = x[0]
    target = loss_target[0]

    silu_c = rowwise(lambda a: a * _sigmoid(a), [c], [], [(D, F32)], [], "silu_c")[0]
    silu_all = all_gather8(silu_c.reshape(8, D // 8)).reshape(8, D)
    n_ada = ada_w.shape[2]
    n_kv = kv_ada_w.shape[1]
    ada_b_mine = lax.dynamic_slice_in_dim(ada_b, chip * n_ada, n_ada, axis=1)
    kv_b_mine = lax.dynamic_slice_in_dim(kv_ada_b, chip * n_kv, n_kv, axis=0)[None, :]
    mods = [mm(silu_all, ada_w[l], "nn", "ada_rows", bias=ada_b_mine[l:l + 1]) for l in range(2)]
    mods.append(mm(silu_all, kv_ada_w, "nn", "kv_ada_rows", bias=kv_b_mine))
    n_mod_cols = 2 * n_ada + n_kv
    mod_pack = jnp.concatenate(mods, axis=1).reshape(-1, 128)
    mod_all = all_gather8(mod_pack).reshape(8, 8, n_mod_cols)[0::2]
    mod_mine = lax.dynamic_index_in_dim(mod_all, dev, axis=1, keepdims=False)
    mod = [mod_mine[:, l * n_ada:(l + 1) * n_ada].reshape(N_MOD, D) for l in range(2)]
    kv_mod = mod_mine[:, 2 * n_ada:].reshape(2, D)
    kv_shift, kv_scale = kv_mod[0:1], kv_mod[1:2]

    def mrow(l, k):
        return mod[l][k:k + 1]

    big = dict(ffn_w13=ffn_w13, ffn_w2=ffn_w2, conv_w_pw1=conv_w_pw1, conv_w_pw2=conv_w_pw2, w_kv_a=w_kv_a,
               w_kv_b=w_kv_b, w_q_a=w_q_a, w_q_b=w_q_b, w_o=w_o)
    names = list(big)
    gathered = gather_weights([_cast_into_slot(big[k], place) for k in names])
    gw = dict(zip(names, gathered))
    small_like = [norm_g.shape, conv_b_pw1.shape, conv_w_dw.shape, conv_b_dw.shape, conv_ln_g.shape,
                  conv_ln_b.shape, conv_b_pw2.shape]
    small_pack = _pack([norm_g, conv_b_pw1, conv_w_dw, conv_b_dw, conv_ln_g, conv_ln_b, conv_b_pw2])
    small_all = all_gather8(small_pack)[0::2].reshape(4, -1)
    per_chip = [_unpack(small_all[j], small_like) for j in range(4)]
    smalls = [jnp.concatenate([per_chip[j][k] for j in range(4)], axis=-1) for k in range(len(small_like))]
    norm_g_f, b_pw1_f, w_dw_f, b_dw_f, ln_g_f, ln_b_f, b_pw2_f = smalls

    w13 = pair_w13(gw["ffn_w13"].reshape(4, 2, 2, D, 2 * F // 4).transpose(1, 2, 3, 0, 4).reshape(2, 2, D, 2 * F), F)
    w2 = gw["ffn_w2"].reshape(4, 2, 2, F // 4, D).transpose(1, 2, 0, 3, 4).reshape(2, 2, F, D)
    conv_p = dict(
        w_pw1=gw["conv_w_pw1"].reshape(4, D, 2 * D // 4).transpose(1, 0, 2).reshape(D, 2 * D),
        b_pw1=b_pw1_f, w_dw=w_dw_f[0], b_dw=b_dw_f, ln_g=ln_g_f, ln_b=ln_b_f,
        w_pw2=gw["conv_w_pw2"].reshape(D, D), b_pw2=b_pw2_f)
    q_lora = w_q_a.shape[2]
    mla_p = dict(
        w_kv_a=gw["w_kv_a"].reshape(D, KV_LORA + QK_ROPE),
        w_kv_b=gw["w_kv_b"].reshape(4, KV_LORA, -1).transpose(1, 0, 2).reshape(KV_LORA, -1),
        w_q_a=gw["w_q_a"].reshape(D, q_lora),
        w_q_b=gw["w_q_b"].reshape(4, q_lora, -1).transpose(1, 0, 2).reshape(q_lora, -1),
        w_o=gw["w_o"].reshape(H * V_HEAD, D))
    pw = mla_pad_weights(mla_p)
    tabs = rope_tables(positions[0])

    def ng(l, k):
        return norm_g_f[l, k][None, :]

    h = h0
    h, s_f1_0 = ffn_fwd(h, ng(0, 0), mrow(0, 0), mrow(0, 1), mrow(0, 2), w13[0, 0], w2[0, 0])
    h, s_conv = conv_module_fwd(h, ng(0, 1), mrow(0, 3), mrow(0, 4), mrow(0, 5), conv_p)
    h, s_f2_0 = ffn_fwd(h, ng(0, 2), mrow(0, 6), mrow(0, 7), mrow(0, 8), w13[0, 1], w2[0, 1])
    kv_norm = kv_norm_g[None, :]
    kv_a_g = kv_a_norm_g[None, :]
    kv, kpe_r, s_kv = mla_kv_fwd(h, kv_norm, kv_shift, kv_scale, kv_a_g, pw, tabs)
    h, s_f1_1 = ffn_fwd(h, ng(1, 0), mrow(1, 0), mrow(1, 1), mrow(1, 2), w13[1, 0], w2[1, 0])
    h, s_mla = mla_fwd(h, ng(1, 1), mrow(1, 3), mrow(1, 4), mrow(1, 5), q_a_norm_g, pw, kv, kpe_r, tabs)
    h, s_f2_1 = ffn_fwd(h, ng(1, 2), mrow(1, 6), mrow(1, 7), mrow(1, 8), w13[1, 1], w2[1, 1])
    dh, loss_cols, d_final_g = loss_head(h, target, final_norm_g[None, :])

    dh, v_f2_1, dw13_11, dw2_11 = ffn_bwd(dh, s_f2_1, ng(1, 2), mrow(1, 7), mrow(1, 8), w13[1, 1], w2[1, 1])
    dh, v_mla, g_mla, dk, dv = mla_bwd(dh, s_mla, ng(1, 1), mrow(1, 4), mrow(1, 5), q_a_norm_g, pw, kv, kpe_r, tabs)
    dh, v_f1_1, dw13_10, dw2_10 = ffn_bwd(dh, s_f1_1, ng(1, 0), mrow(1, 1), mrow(1, 2), w13[1, 0], w2[1, 0])
    dh_kv, v_kv, d_kv_a_g, d_w_kv_a, d_w_kv_b = mla_kv_bwd(dk, dv, s_kv, kv_norm, kv_scale, kv_a_g, pw, tabs)
    dh = rowwise(lambda a, b: a + b, [dh, dh_kv], [], [(D, F32)], [], "add_stream")[0]
    dh, v_f2_0, dw13_01, dw2_01 = ffn_bwd(dh, s_f2_0, ng(0, 2), mrow(0, 7), mrow(0, 8), w13[0, 1], w2[0, 1])
    dh, v_conv, g_conv = conv_module_bwd(dh, s_conv, ng(0, 1), mrow(0, 4), mrow(0, 5), conv_p)
    dh, v_f1_0, dw13_00, dw2_00 = ffn_bwd(dh, s_f1_0, ng(0, 0), mrow(0, 1), mrow(0, 2), w13[0, 0], w2[0, 0])
    grad_x = dh[None]

    d_w_kv_a_u = jnp.concatenate([d_w_kv_a[:, :KV_LORA], d_w_kv_a[:, KV_LORA + QK_NOPE:KV_LORA + QK_NOPE + QK_ROPE]],
                                 axis=1)
    hk = H * HEAD_PAD
    dkb = jnp.concatenate([d_w_kv_b[:, :hk].reshape(KV_LORA, H, HEAD_PAD)[:, :, :QK_NOPE],
                           d_w_kv_b[:, hk:].reshape(KV_LORA, H, HEAD_PAD)[:, :, :V_HEAD]], axis=2).reshape(KV_LORA, -1)
    d_w_q_b_u = _unpad_heads(g_mla["w_q_b"], QK_NOPE + QK_ROPE)
    d_w_o_u = g_mla["w_o"].reshape(H, HEAD_PAD, D)[:, :V_HEAD].reshape(H * V_HEAD, D)
    full = [_cols_to_blocks(unpair_w13(dw, F)) for dw in (dw13_00, dw13_01, dw13_10, dw13_11)] + [
            _rows_to_blocks(dw2_00), _rows_to_blocks(dw2_01), _rows_to_blocks(dw2_10), _rows_to_blocks(dw2_11),
            _cols_to_blocks(g_conv["w_pw1"]), _rows_to_blocks(g_conv["w_pw2"]), _rows_to_blocks(d_w_kv_a_u),
            _cols_to_blocks(dkb), _rows_to_blocks(g_mla["w_q_a"]), _cols_to_blocks(d_w_q_b_u),
            _rows_to_blocks(d_w_o_u)]
    red = reduce_scatter_grads(full, place)
    g_ffn_w13 = jnp.stack(red[0:4]).reshape(ffn_w13.shape)
    g_ffn_w2 = jnp.stack(red[4:8]).reshape(ffn_w2.shape)
    g_conv_w_pw1 = red[8].reshape(conv_w_pw1.shape)
    g_conv_w_pw2 = red[9].reshape(conv_w_pw2.shape)
    g_w_kv_a = red[10].reshape(w_kv_a.shape)
    g_w_kv_b = red[11].reshape(w_kv_b.shape)
    g_w_q_a = red[12].reshape(w_q_a.shape)
    g_w_q_b = red[13].reshape(w_q_b.shape)
    g_w_o = red[14].reshape(w_o.shape)

    def dmod(v1, vm, v2):
        return jnp.concatenate([v1[0], v1[1], v1[2], vm[0], vm[1], vm[2], v2[0], v2[1], v2[2]], axis=1)
    d_mod0 = dmod(v_f1_0, v_conv, v_f2_0)
    d_mod1 = dmod(v_f1_1, v_mla, v_f2_1)
    d_kv_mod = jnp.concatenate([v_kv[0], v_kv[1]], axis=1)
    d_norm_g = jnp.concatenate([v_f1_0[3], v_conv[3], v_f2_0[3], v_f1_1[3], v_mla[3], v_f2_1[3]], axis=0)
    vec_list = [d_mod0, d_mod1, d_kv_mod, d_norm_g, g_conv["b_pw1"], g_conv["w_dw"], g_conv["b_dw"], g_conv["ln_g"],
                g_conv["ln_b"], g_conv["b_pw2"], v_kv[2], d_kv_a_g, g_mla["q_a_norm_g"], d_final_g, loss_cols]
    vec_like = [v.shape for v in vec_list]
    vec_pack = _pack(vec_list)
    n_mod_rows = (2 * N_MOD * D + 2 * D) // 128
    vec_all = all_gather8(vec_pack)
    vec_sum = sum_blocks([(vec_all, d) for d in range(8)], "sum_devices").reshape(-1)
    (_, _, _, s_norm_g, s_b_pw1, s_w_dw, s_b_dw, s_ln_g, s_ln_b, s_b_pw2, s_kv_norm_g, s_kv_a_g, s_q_a_g,
     s_final_g, s_loss) = _unpack(vec_sum, vec_like)
    loss = jnp.sum(s_loss)
    dmod_all = vec_all[:, :n_mod_rows].reshape(8, 2 * N_MOD * D + 2 * D)
    dmod_sum = vec_sum[:2 * N_MOD * D + 2 * D]
    g_ada_b = dmod_sum[:2 * N_MOD * D].reshape(2, N_MOD * D)
    g_kv_ada_b = dmod_sum[2 * N_MOD * D:]
    g_ada_w = []
    for l in range(2):
        cols = lax.dynamic_slice_in_dim(dmod_all[:, l * N_MOD * D:(l + 1) * N_MOD * D], chip * n_ada, n_ada, axis=1)
        g_ada_w.append(mm(silu_all, cols, "tn", "ada_w_grad"))
    g_ada_w = jnp.stack(g_ada_w)
    kv_cols = lax.dynamic_slice_in_dim(dmod_all[:, 2 * N_MOD * D:], chip * n_kv, n_kv, axis=1)
    g_kv_ada_w = mm(silu_all, kv_cols, "tn", "kv_ada_w_grad")

    def shard(v, width):
        return lax.dynamic_slice_in_dim(v, chip * width, width, axis=v.ndim - 1)

    Dq = D // 4
    g_norm_g = shard(s_norm_g.reshape(2, 3, D), Dq)
    g_conv_b_pw1 = shard(s_b_pw1, 2 * D // 4)
    g_conv_w_dw = shard(s_w_dw, Dq)[None]
    g_conv_b_dw = shard(s_b_dw, Dq)
    g_conv_ln_g = shard(s_ln_g, Dq)
    g_conv_ln_b = shard(s_ln_b, Dq)
    g_conv_b_pw2 = shard(s_b_pw2, Dq)

    grads = [g_ada_w, g_ada_b, g_norm_g, g_ffn_w13, g_ffn_w2, g_conv_w_pw1, g_conv_b_pw1, g_conv_w_dw, g_conv_b_dw,
             g_conv_ln_g, g_conv_ln_b, g_conv_w_pw2, g_conv_b_pw2, g_kv_ada_w, g_kv_ada_b, s_kv_norm_g[0], g_w_kv_a,
             s_kv_a_g[0], g_w_kv_b, g_w_q_a, s_q_a_g, g_w_q_b, g_w_o, s_final_g[0]]
    weights = [ada_w, ada_b, norm_g, ffn_w13, ffn_w2, conv_w_pw1, conv_b_pw1, conv_w_dw, conv_b_dw, conv_ln_g,
               conv_ln_b, conv_w_pw2, conv_b_pw2, kv_ada_w, kv_ada_b, kv_norm_g, w_kv_a, kv_a_norm_g, w_kv_b, w_q_a,
               q_a_norm_g, w_q_b, w_o, final_norm_g]
    ms = [m_ada_w, m_ada_b, m_norm_g, m_ffn_w13, m_ffn_w2, m_conv_w_pw1, m_conv_b_pw1, m_conv_w_dw, m_conv_b_dw,
          m_conv_ln_g, m_conv_ln_b, m_conv_w_pw2, m_conv_b_pw2, m_kv_ada_w, m_kv_ada_b, m_kv_norm_g, m_w_kv_a,
          m_kv_a_norm_g, m_w_kv_b, m_w_q_a, m_q_a_norm_g, m_w_q_b, m_w_o, m_final_norm_g]
    vs = [v_ada_w, v_ada_b, v_norm_g, v_ffn_w13, v_ffn_w2, v_conv_w_pw1, v_conv_b_pw1, v_conv_w_dw, v_conv_b_dw,
          v_conv_ln_g, v_conv_ln_b, v_conv_w_pw2, v_conv_b_pw2, v_kv_ada_w, v_kv_ada_b, v_kv_norm_g, v_w_kv_a,
          v_kv_a_norm_g, v_w_kv_b, v_w_q_a, v_q_a_norm_g, v_w_q_b, v_w_o, v_final_norm_g]
    grads = [g.reshape(w.shape) for g, w in zip(grads, weights)]
    deltas, new_m, new_v = [], [], []
    for w, g, m, v in zip(weights, grads, ms, vs):
        d, nm, nv = adamw(w, g, m, v)
        deltas.append(d)
        new_m.append(nm)
        new_v.append(nv)
    return (loss, grad_x, *grads, *deltas, *new_m, *new_v)
```

```python
import jax
import jax.numpy as jnp
from jax import lax
from jax.experimental import pallas as pl
from jax.experimental.pallas import tpu as pltpu

F32 = jnp.float32
BF16 = jnp.bfloat16
MESH = pl.DeviceIdType.MESH

N_HEADS = 16
QK_NOPE = 64
QK_ROPE = 32
V_HEAD = 64
KV_LORA = 256
CONV_WIDTH = 31
ROPE_THETA = 10000.0
EPS = 1e-6
N_MOD = 9
HEAD_PAD = 128
ATTN_TILE = 512
CONV_HALO = 32

ADAM_LR = 0.001
ADAM_B1 = 0.9
ADAM_B2 = 0.999
ADAM_EPS = 1e-08
ADAM_WD = 0.01
ADAM_STEP = 10

VMEM_LIMIT_BYTES = 56 * 2 ** 20
ROW_TILE_BUDGET = 10 * 2 ** 20
MM_VMEM_BUDGET = 40 * 2 ** 20
NEG = float(jnp.finfo(jnp.float32).min)
LOG2_E = 1.4426950408889634


def _tile(n, prefs):
    for t in prefs:
        if n % t == 0:
            return t
    return n


def _params(sem):
    return pltpu.CompilerParams(dimension_semantics=sem, vmem_limit_bytes=VMEM_LIMIT_BYTES)


def _mm_tiles(M, N, K, mode, a_bytes, b_bytes, o_bytes):
    if mode == "tn":
        tk_opts = [t for t in (2048, 1024, 512, 256, 128) if K % t == 0] or [K]
        tm_opts = ([M] if M <= 2816 else []) + [t for t in (1024, 512, 256, 128) if M % t == 0 and t < M]
    else:
        tk_opts = [K]
        tm_opts = [t for t in (1024, 512, 256, 128) if M % t == 0] or [M]
    tn_opts = [t for t in (1408, 1024, 512, 384, 256, 128) if N % t == 0] or [N]

    def need(tm, tn, tk):
        blocks = 2 * (tm * tk * a_bytes + tk * tn * b_bytes + tm * tn * o_bytes)
        return blocks + (tm * tn * 4 if mode == "tn" else 0)

    tk_floor = next((t for t in tk_opts if t <= 512), tk_opts[-1])
    for tm in tm_opts:
        for tn in tn_opts:
            if need(tm, tn, tk_floor) <= MM_VMEM_BUDGET:
                return tm, tn, next(tk for tk in tk_opts if need(tm, tn, tk) <= MM_VMEM_BUDGET)
    return tm_opts[-1], tn_opts[-1], tk_opts[-1]


def mm(a, b, mode, name, out_dtype=F32, bias=None):
    if mode == "nn":
        (M, K), (K2, N) = a.shape, b.shape
        dims = (((1,), (0,)), ((), ()))
    elif mode == "nt":
        (M, K), (N, K2) = a.shape, b.shape
        dims = (((1,), (1,)), ((), ()))
    else:
        (K, M), (K2, N) = a.shape, b.shape
        dims = (((0,), (0,)), ((), ()))
    assert K == K2, (a.shape, b.shape, mode)
    tm, tn, tk = _mm_tiles(M, N, K, mode, a.dtype.itemsize, b.dtype.itemsize, jnp.dtype(out_dtype).itemsize)
    nk = K // tk
    if mode == "tn":
        a_spec = pl.BlockSpec((tk, tm), lambda i, j, k: (k, i))
        b_spec = pl.BlockSpec((tk, tn), lambda i, j, k: (k, j))
    elif mode == "nn":
        a_spec = pl.BlockSpec((tm, tk), lambda i, j, k: (i, k))
        b_spec = pl.BlockSpec((tk, tn), lambda i, j, k: (k, j))
    else:
        a_spec = pl.BlockSpec((tm, tk), lambda i, j, k: (i, k))
        b_spec = pl.BlockSpec((tn, tk), lambda i, j, k: (j, k))
    in_specs = [a_spec, b_spec]
    operands = [a, b]
    if bias is not None:
        in_specs.append(pl.BlockSpec((1, tn), lambda i, j, k: (0, j)))
        operands.append(bias)
    has_bias = bias is not None

    def body(*refs):
        a_ref, b_ref = refs[0], refs[1]
        bias_ref = refs[2] if has_bias else None
        o_ref = refs[3] if has_bias else refs[2]
        prod = lax.dot_general(a_ref[...].astype(BF16), b_ref[...].astype(BF16), dims,
                               preferred_element_type=F32)
        if nk == 1:
            if has_bias:
                prod = prod + bias_ref[...]
            o_ref[...] = prod.astype(o_ref.dtype)
        else:
            acc_ref = refs[-1]
            k = pl.program_id(2)

            @pl.when(k == 0)
            def _():
                acc_ref[...] = jnp.zeros_like(acc_ref)

            acc_ref[...] += prod

            @pl.when(k == nk - 1)
            def _():
                out = acc_ref[...]
                if has_bias:
                    out = out + bias_ref[...]
                o_ref[...] = out.astype(o_ref.dtype)

    return pl.pallas_call(
        body, name=name,
        grid=(M // tm, N // tn, nk),
        in_specs=in_specs,
        out_specs=pl.BlockSpec((tm, tn), lambda i, j, k: (i, j)),
        out_shape=jax.ShapeDtypeStruct((M, N), out_dtype),
        scratch_shapes=[pltpu.VMEM((tm, tn), F32)] if nk > 1 else [],
        compiler_params=_params(("parallel", "parallel", "arbitrary")),
    )(*operands)


def mm_fused(a, b, mode, name, tn, epi, epi_outs, pro=None, pro_rows=(), pro_vecs=(), pro_out=False, n_pro_sums=0,
             epi_rows=(), epi_vecs=()):
    M, K = a.shape
    n_b = b.shape[0] if b.ndim == 3 else 1
    N = b.shape[-1] if mode == "nn" else b.shape[0]
    dims = (((1,), (0,)), ((), ())) if mode == "nn" else (((1,), (1,)), ((), ()))
    nj = N // tn
    epi_outs = [o if len(o) == 3 else (*o, None) for o in epi_outs]
    row_bytes = 2 * (K * a.dtype.itemsize + sum(K * r.dtype.itemsize for r in pro_rows) + (2 * K if pro_out else 0)
                     + sum(w * r.dtype.itemsize * (r.shape[0] if r.ndim == 3 else 1) for r, w in epi_rows)
                     + sum(w * jnp.dtype(dt).itemsize * (L or 1) for w, dt, L in epi_outs)
                     ) + (2 * K if pro is not None else 0)
    fixed = 2 * n_b * K * tn * b.dtype.itemsize
    tm = next((t for t in (1024, 512, 256, 128) if M % t == 0 and t * row_bytes + fixed <= MM_VMEM_BUDGET), M)
    row = lambda i, j: (i, 0)
    tile = lambda i, j: (i, j)
    stack = lambda i, j: (0, i, j)
    in_specs = [pl.BlockSpec((tm, K), row)] + [pl.BlockSpec((tm, K), row) for _ in pro_rows]
    in_specs += [pl.BlockSpec(v.shape, lambda i, j: (0, 0)) for v in pro_vecs]
    if b.ndim == 3:
        in_specs += [pl.BlockSpec((None, K, tn), lambda i, j, h=h: (h, 0, j)) for h in range(n_b)]
    elif mode == "nn":
        in_specs += [pl.BlockSpec((K, tn), lambda i, j: (0, j))]
    else:
        in_specs += [pl.BlockSpec((tn, K), lambda i, j: (j, 0))]
    in_specs += [pl.BlockSpec((r.shape[0], tm, w), stack) if r.ndim == 3 else pl.BlockSpec((tm, w), tile)
                 for r, w in epi_rows]
    in_specs += [pl.BlockSpec((1, tn), lambda i, j: (0, j)) for _ in epi_vecs]
    out_specs, out_shape = [], []
    if pro_out:
        out_specs.append(pl.BlockSpec((tm, K), row))
        out_shape.append(jax.ShapeDtypeStruct((M, K), BF16))
    for _ in range(n_pro_sums):
        out_specs.append(pl.BlockSpec((1, K), lambda i, j: (0, 0)))
        out_shape.append(jax.ShapeDtypeStruct((1, K), F32))
    for w, dt, L in epi_outs:
        out_specs.append(pl.BlockSpec((tm, w), tile) if L is None else pl.BlockSpec((L, tm, w), stack))
        out_shape.append(jax.ShapeDtypeStruct((M, nj * w) if L is None else (L, M, nj * w), dt))
    n_pr, n_pv, n_er, n_ev = len(pro_rows), len(pro_vecs), len(epi_rows), len(epi_vecs)
    n_a = 1 + n_pr + n_pv
    n_in = n_a + n_b + n_er + n_ev
    n_po = 1 if pro_out else 0

    def body(*refs):
        i, j = pl.program_id(0), pl.program_id(1)
        a_ref = refs[0]
        outs = refs[n_in:]
        if pro is not None:
            lhs_ref = refs[-1]

            @pl.when(j == 0)
            def _():
                res = pro(*[r[...] for r in refs[:1 + n_pr + n_pv]])
                if not isinstance(res, (tuple, list)):
                    res = (res,)
                lhs_ref[...] = res[0]
                if pro_out:
                    outs[0][...] = res[0]
                for s_ref, val in zip(outs[n_po:n_po + n_pro_sums], res[1:]):
                    part = jnp.sum(val.astype(F32), axis=0, keepdims=True)

                    @pl.when(i == 0)
                    def _(s_ref=s_ref, part=part):
                        s_ref[...] = part

                    @pl.when(i != 0)
                    def _(s_ref=s_ref, part=part):
                        s_ref[...] += part

            lhs = lhs_ref[...]
        else:
            lhs = a_ref[...].astype(BF16)
        accs = [lax.dot_general(lhs, b_ref[...].astype(BF16), dims, preferred_element_type=F32)
                for b_ref in refs[n_a:n_a + n_b]]
        res = epi(*accs, *[r[...] for r in refs[n_a + n_b:n_in]])
        if not isinstance(res, (tuple, list)):
            res = (res,)
        for o_ref, val in zip(outs[n_po + n_pro_sums:], res):
            if isinstance(val, (tuple, list)):
                for h, part in enumerate(val):
                    o_ref[h] = part.astype(o_ref.dtype)
            else:
                o_ref[...] = val.astype(o_ref.dtype)

    return pl.pallas_call(
        body, name=name,
        grid=(M // tm, nj),
        in_specs=in_specs, out_specs=out_specs, out_shape=out_shape,
        scratch_shapes=[pltpu.VMEM((tm, K), BF16)] if pro is not None else [],
        compiler_params=_params(("arbitrary", "arbitrary")),
    )(a, *pro_rows, *pro_vecs, *([b] * n_b), *[r for r, _ in epi_rows], *epi_vecs)


def rowwise(fn, rows, vecs, outs, sums, name, tm=None):
    norm = [(r, r.shape[1], 0) if not isinstance(r, tuple) else r for r in rows]
    S = norm[0][0].shape[0]
    if tm is None:
        per_row = sum(w * r.dtype.itemsize for r, w, _ in norm) + sum(n * jnp.dtype(dt).itemsize for n, dt in outs)
        tm = S
        for t in (512, 256, 128, 64, 32, 16, 8):
            if S % t == 0:
                tm = t
                if t * per_row <= ROW_TILE_BUDGET:
                    break
    n_rows, n_vecs, n_outs, n_sums = len(norm), len(vecs), len(outs), len(sums)
    in_specs = [pl.BlockSpec((tm, w), lambda i, cb=cb: (i, cb)) for _, w, cb in norm]
    in_specs += [pl.BlockSpec(v.shape, lambda i: (0, 0)) for v in vecs]
    out_specs = [pl.BlockSpec((tm, n), lambda i: (i, 0)) for n, _ in outs]
    out_specs += [pl.BlockSpec((1, n), lambda i: (0, 0)) for n in sums]
    out_shape = [jax.ShapeDtypeStruct((S, n), dt) for n, dt in outs]
    out_shape += [jax.ShapeDtypeStruct((1, n), F32) for n in sums]

    def body(*refs):
        ins = [r[...] for r in refs[:n_rows + n_vecs]]
        res = fn(*ins)
        if not isinstance(res, (tuple, list)):
            res = (res,)
        out_refs = refs[n_rows + n_vecs:]
        for o_ref, val in zip(out_refs[:n_outs], res[:n_outs]):
            o_ref[...] = val.astype(o_ref.dtype)
        if n_sums:
            i = pl.program_id(0)
            for s_ref, val in zip(out_refs[n_outs:], res[n_outs:]):
                part = jnp.sum(val.astype(F32), axis=0, keepdims=True)

                @pl.when(i == 0)
                def _(s_ref=s_ref, part=part):
                    s_ref[...] = part

                @pl.when(i != 0)
                def _(s_ref=s_ref, part=part):
                    s_ref[...] += part

    res = pl.pallas_call(
        body, name=name,
        grid=(S // tm,),
        in_specs=in_specs, out_specs=out_specs, out_shape=out_shape,
        compiler_params=_params(("arbitrary",) if n_sums else ("parallel",)),
    )(*[r for r, _, _ in norm], *vecs)
    return res


def _sigmoid(x):
    return jax.nn.sigmoid(x)


def _rms(x):
    r = lax.rsqrt(jnp.mean(x * x, axis=-1, keepdims=True) + EPS)
    return x * r, r


def _rms_bwd(xhat, r, dxhat):
    return r * (dxhat - xhat * jnp.mean(dxhat * xhat, axis=-1, keepdims=True))


def norm_mod(h, g, sh, sc, name):
    def f(h, g, sh, sc):
        xhat, _ = _rms(h)
        return ((xhat * g) * (1 + sc) + sh).astype(BF16)
    return rowwise(f, [h], [g, sh, sc], [(h.shape[1], BF16)], [], name)[0]


def norm_mod_bwd(h, dhn, dh_out, g, sc, name):
    D = h.shape[1]
    with_res = dh_out is not None

    def f(*a):
        if with_res:
            h, dhn, dres, g, sc = a
        else:
            h, dhn, g, sc = a
        xhat, r = _rms(h)
        xn = xhat * g
        dxn = dhn * (1 + sc)
        dh = _rms_bwd(xhat, r, dxn * g)
        if with_res:
            dh = dh + dres
        return dh, dhn, dhn * xn, dxn * xhat

    rows = [h, dhn] + ([dh_out] if with_res else [])
    return rowwise(f, rows, [g, sc], [(D, F32)], [D, D, D], name)


def residual(h, y, gate, coef, name, bias=None):
    D = h.shape[1]
    if bias is None:
        def f(h, y, gate):
            return h + (coef * gate) * y
        return rowwise(f, [h, y], [gate], [(D, F32)], [], name)[0], y

    def fb(h, y, gate, bias):
        yb = y + bias
        return h + (coef * gate) * yb, yb
    return rowwise(fb, [h, y], [gate, bias], [(D, F32), (D, F32)], [], name)


def residual_bwd(dh_out, y, gate, coef, name, with_bias_sum=False):
    D = y.shape[1]

    def f(dh, y, gate):
        dy = (coef * gate) * dh
        res = (dy.astype(BF16), coef * dh * y)
        return res + ((dy,) if with_bias_sum else ())
    return rowwise(f, [dh_out, y], [gate], [(D, BF16)], [D, D] if with_bias_sum else [D], name)


def mm_halves_nt(a, b, name):
    _, M, K = a.shape
    N = b.shape[1]
    tm = _tile(M, (1024, 512, 256, 128))
    tn = _tile(N, (1024, 512, 256, 128))
    nt = (((1,), (1,)), ((), ()))

    def body(a_ref, b_ref, o_ref, acc_ref):
        h = pl.program_id(2)
        prod = lax.dot_general(a_ref[...], b_ref[...], nt, preferred_element_type=F32)

        @pl.when(h == 0)
        def _():
            acc_ref[...] = prod

        @pl.when(h == 1)
        def _():
            o_ref[...] = acc_ref[...] + prod

    return pl.pallas_call(
        body, name=name,
        grid=(M // tm, N // tn, 2),
        in_specs=[pl.BlockSpec((None, tm, K), lambda i, j, h: (h, i, 0)),
                  pl.BlockSpec((None, tn, K), lambda i, j, h: (h, j, 0))],
        out_specs=pl.BlockSpec((tm, tn), lambda i, j, h: (i, j)),
        out_shape=jax.ShapeDtypeStruct((M, N), F32),
        scratch_shapes=[pltpu.VMEM((tm, tn), F32)],
        compiler_params=_params(("parallel", "parallel", "arbitrary")),
    )(a, b)


def ffn_w13_grad(hn, dab):
    S, D = hn.shape
    F = dab.shape[2]
    C = F // 2
    tk = next(t for t in (2048, 1024, 512, 256, 128) if S % t == 0)
    tn_dims = (((0,), (0,)), ((), ()))
    nk = S // tk

    def body(a_ref, b_ref, o_ref, acc_ref):
        k = pl.program_id(1)

        @pl.when(k == 0)
        def _():
            acc_ref[...] = jnp.zeros_like(acc_ref)

        acc_ref[...] += lax.dot_general(a_ref[...], b_ref[...], tn_dims, preferred_element_type=F32)

        @pl.when(k == nk - 1)
        def _():
            o_ref[...] = acc_ref[...]

    return pl.pallas_call(
        body, name="ffn_w13_dw",
        grid=(4, nk),
        in_specs=[pl.BlockSpec((tk, D), lambda j, k: (k, 0)),
                  pl.BlockSpec((None, tk, C), lambda j, k: (j // 2, k, j % 2))],
        out_specs=pl.BlockSpec((None, D, C), lambda j, k: (j, 0, 0)),
        out_shape=jax.ShapeDtypeStruct((4, D, C), F32),
        scratch_shapes=[pltpu.VMEM((D, C), F32)],
        compiler_params=_params(("parallel", "arbitrary")),
    )(hn, dab)


def _ffn_chunk(F):
    return _tile(F, (256, 128))


def ffn_fwd(h, g, sh, sc, gate, w13, w2):
    F, D = w2.shape
    cf = _ffn_chunk(F)

    def norm(h, g, sh, sc):
        xhat, _ = _rms(h)
        return ((xhat * g) * (1 + sc) + sh).astype(BF16)

    def act(a, b):
        return (a, b), (a * _sigmoid(a)) * b
    hn, ab, t = mm_fused(h, w13, "nn", "ffn_w13", cf, act, [(cf, F32, 2), (cf, BF16)],
                         pro=norm, pro_vecs=[g, sh, sc], pro_out=True)

    def res(acc, h, gate):
        return h + (0.5 * gate) * acc, acc
    h_out, y = mm_fused(t, w2, "nn", "ffn_w2", D, res, [(D, F32), (D, F32)], epi_rows=[(h, D)], epi_vecs=[gate])
    return h_out, (h, hn, ab, y)


def ffn_bwd(dh_out, saved, g, sc, gate, w13, w2):
    h, hn, ab, y = saved
    F, D = w2.shape
    cf = _ffn_chunk(F)

    def scale(dh, y, gate):
        return ((0.5 * gate) * dh).astype(BF16), 0.5 * dh * y

    def act_bwd(dt, ab):
        a, b = ab[0], ab[1]
        sig = _sigmoid(a)
        sa = a * sig
        da = dt * b * (sig * (1 + a * (1 - sig)))
        db = dt * sa
        return sa * b, (da, db)
    dy, d_gate, t, dab = mm_fused(dh_out, w2, "nt", "ffn_w2_dx", cf, act_bwd, [(cf, BF16), (cf, BF16, 2)],
                                  pro=scale, pro_rows=[y], pro_vecs=[gate], pro_out=True, n_pro_sums=1,
                                  epi_rows=[(ab, cf)])
    dw2 = mm(t, dy, "tn", "ffn_w2_dw")
    dw13 = ffn_w13_grad(hn, dab)
    dhn = mm_halves_nt(dab, w13, "ffn_w13_dx")
    dh_in, d_sh, d_sc, d_g = norm_mod_bwd(h, dhn, dh_out, g, sc, "norm_mod_bwd")
    return dh_in, (d_sh, d_sc, d_gate, d_g), dw13, dw2


def _shifted(xbuf, n):
    return [xbuf] + [pltpu.roll(xbuf, n - b, 0) for b in range(1, 8)]


def conv_fwd(u, w_dw, b_dw, ln_g, ln_b):
    S, D = u.shape
    tm = _tile(S, (256, 128))
    rc = 32
    first_tap = CONV_HALO - (CONV_WIDTH - 1)
    w = jnp.concatenate([w_dw, jnp.zeros((CONV_HALO - CONV_WIDTH, D), F32)], axis=0)

    def body(cur_ref, prev_ref, w_ref, b_ref, g_ref, beta_ref, z_ref, s_ref):
        i = pl.program_id(0)
        prev = jnp.where(i == 0, jnp.zeros((CONV_HALO, D), F32), prev_ref[...])
        xs = _shifted(jnp.concatenate([prev, cur_ref[...]], axis=0), tm + CONV_HALO)
        for c0 in range(0, tm, rc):
            acc = jnp.zeros((rc, D), F32)
            for k in range(CONV_WIDTH):
                off = first_tap + k
                a8, b = off // 8 * 8, off % 8
                acc = acc + w_ref[k:k + 1, :] * xs[b][c0 + a8:c0 + a8 + rc, :]
            z_ref[c0:c0 + rc, :] = acc + b_ref[...]
        z = z_ref[...]
        mu = jnp.mean(z, axis=-1, keepdims=True)
        zc = z - mu
        r = lax.rsqrt(jnp.mean(zc * zc, axis=-1, keepdims=True) + EPS)
        un = zc * r * g_ref[...] + beta_ref[...]
        s_ref[...] = (un * _sigmoid(un)).astype(BF16)

    nb = tm // CONV_HALO
    vec = pl.BlockSpec((1, D), lambda i: (0, 0))
    return pl.pallas_call(
        body, name="conv_fwd",
        grid=(S // tm,),
        in_specs=[pl.BlockSpec((tm, D), lambda i: (i, 0)),
                  pl.BlockSpec((CONV_HALO, D), lambda i: (jnp.maximum(i * nb - 1, 0), 0)),
                  pl.BlockSpec((CONV_HALO, D), lambda i: (0, 0)), vec, vec, vec],
        out_specs=[pl.BlockSpec((tm, D), lambda i: (i, 0)), pl.BlockSpec((tm, D), lambda i: (i, 0))],
        out_shape=[jax.ShapeDtypeStruct((S, D), F32), jax.ShapeDtypeStruct((S, D), BF16)],
        compiler_params=_params(("parallel",)),
    )(u, u, w, b_dw, ln_g, ln_b)


def conv_bwd(dz, u, w_dw):
    S, D = u.shape
    tm = _tile(S, (256, 128))
    rc = 32
    first_tap = CONV_HALO - (CONV_WIDTH - 1)
    w = jnp.concatenate([w_dw, jnp.zeros((CONV_HALO - CONV_WIDTH, D), F32)], axis=0)
    n_tiles = S // tm
    nb = tm // CONV_HALO

    def body(dz_ref, dzn_ref, u_ref, up_ref, w_ref, du_ref, dw_ref):
        i = pl.program_id(0)
        nxt = jnp.where(i == n_tiles - 1, jnp.zeros((CONV_HALO, D), F32), dzn_ref[...])
        dzs = _shifted(jnp.concatenate([dz_ref[...], nxt], axis=0), tm + CONV_HALO)
        for c0 in range(0, tm, rc):
            acc = jnp.zeros((rc, D), F32)
            for m in range(CONV_WIDTH):
                a8, b = m // 8 * 8, m % 8
                acc = acc + w_ref[CONV_WIDTH - 1 - m:CONV_WIDTH - m, :] * dzs[b][c0 + a8:c0 + a8 + rc, :]
            du_ref[c0:c0 + rc, :] = acc
        prev = jnp.where(i == 0, jnp.zeros((CONV_HALO, D), F32), up_ref[...])
        us = _shifted(jnp.concatenate([prev, u_ref[...]], axis=0), tm + CONV_HALO)
        dz = dz_ref[...]

        @pl.when(i == 0)
        def _():
            dw_ref[...] = jnp.zeros_like(dw_ref)

        for k in range(CONV_WIDTH):
            off = first_tap + k
            a8, b = off // 8 * 8, off % 8
            dw_ref[k:k + 1, :] += jnp.sum(dz * us[b][a8:a8 + tm, :], axis=0, keepdims=True)

    last_blk = S // CONV_HALO - 1
    du, dw = pl.pallas_call(
        body, name="conv_bwd",
        grid=(n_tiles,),
        in_specs=[pl.BlockSpec((tm, D), lambda i: (i, 0)),
                  pl.BlockSpec((CONV_HALO, D), lambda i: (jnp.minimum((i + 1) * nb, last_blk), 0)),
                  pl.BlockSpec((tm, D), lambda i: (i, 0)),
                  pl.BlockSpec((CONV_HALO, D), lambda i: (jnp.maximum(i * nb - 1, 0), 0)),
                  pl.BlockSpec((CONV_HALO, D), lambda i: (0, 0))],
        out_specs=[pl.BlockSpec((tm, D), lambda i: (i, 0)), pl.BlockSpec((CONV_HALO, D), lambda i: (0, 0))],
        out_shape=[jax.ShapeDtypeStruct((S, D), F32), jax.ShapeDtypeStruct((CONV_HALO, D), F32)],
        compiler_params=_params(("arbitrary",)),
    )(dz, dz, u, u, w)
    return du, dw[:CONV_WIDTH]


def conv_module_fwd(h, g, sh, sc, gate, p):
    D = h.shape[1]
    hn = norm_mod(h, g, sh, sc, "conv_norm_mod")
    pre = mm(hn, p["w_pw1"], "nn", "conv_pw1")
    ba, bg = p["b_pw1"][:, :D], p["b_pw1"][:, D:]

    def glu(a, gt, ba, bg):
        return (a + ba) * _sigmoid(gt + bg)
    u = rowwise(glu, [(pre, D, 0), (pre, D, 1)], [ba, bg], [(D, F32)], [], "conv_glu")[0]
    z, s = conv_fwd(u, p["w_dw"], p["b_dw"], p["ln_g"], p["ln_b"])
    yraw = mm(s, p["w_pw2"], "nn", "conv_pw2")
    h_out, y = residual(h, yraw, gate, 1.0, "conv_residual", bias=p["b_pw2"])
    return h_out, (h, hn, pre, u, z, s, y)


def conv_module_bwd(dh_out, saved, g, sc, gate, p):
    h, hn, pre, u, z, s, y = saved
    D = h.shape[1]
    dy, d_gate, d_b_pw2 = residual_bwd(dh_out, y, gate, 1.0, "conv_residual_bwd", with_bias_sum=True)
    d_w_pw2 = mm(s, dy, "tn", "conv_pw2_dw")
    ds = mm(dy, p["w_pw2"], "nt", "conv_pw2_dx")

    def ln_bwd(z, ds, g, beta):
        mu = jnp.mean(z, axis=-1, keepdims=True)
        zc = z - mu
        r = lax.rsqrt(jnp.mean(zc * zc, axis=-1, keepdims=True) + EPS)
        xhat = zc * r
        un = xhat * g + beta
        sig = _sigmoid(un)
        d_un = ds * (sig * (1 + un * (1 - sig)))
        dxhat = d_un * g
        dz = r * (dxhat - jnp.mean(dxhat, axis=-1, keepdims=True)
                  - xhat * jnp.mean(dxhat * xhat, axis=-1, keepdims=True))
        return dz, d_un * xhat, d_un, dz
    dz, d_ln_g, d_ln_b, d_b_dw = rowwise(ln_bwd, [z, ds], [p["ln_g"], p["ln_b"]], [(D, F32)], [D, D, D],
                                         "conv_ln_bwd")
    du, d_w_dw = conv_bwd(dz, u, p["w_dw"])
    ba, bg = p["b_pw1"][:, :D], p["b_pw1"][:, D:]

    def glu_bwd(a, gt, du, ba, bg):
        sg = _sigmoid(gt + bg)
        da = du * sg
        dg = du * (a + ba) * (sg * (1 - sg))
        dpre = jnp.concatenate([da, dg], axis=1)
        return dpre.astype(BF16), dpre
    dpre, d_b_pw1 = rowwise(glu_bwd, [(pre, D, 0), (pre, D, 1), du], [ba, bg], [(2 * D, BF16)], [2 * D],
                            "conv_glu_bwd")
    d_w_pw1 = mm(hn, dpre, "tn", "conv_pw1_dw")
    dhn = mm(dpre, p["w_pw1"], "nt", "conv_pw1_dx")
    dh_in, d_sh, d_sc, d_g = norm_mod_bwd(h, dhn, dh_out, g, sc, "norm_mod_bwd")
    grads = dict(w_pw1=d_w_pw1, b_pw1=d_b_pw1, w_dw=d_w_dw, b_dw=d_b_dw, ln_g=d_ln_g, ln_b=d_ln_b,
                 w_pw2=d_w_pw2, b_pw2=d_b_pw2)
    return dh_in, (d_sh, d_sc, d_gate, d_g), grads


def _rope(x, c, s1, s2):
    n = x.shape[1]
    return x * c + pltpu.roll(x, n - QK_ROPE // 2, 1) * s1 + pltpu.roll(x, QK_ROPE // 2, 1) * s2


def _rope_t(dy, c, s1, s2):
    n = dy.shape[1]
    return dy * c + pltpu.roll(dy * s1, QK_ROPE // 2, 1) + pltpu.roll(dy * s2, n - QK_ROPE // 2, 1)


def rope_tables(positions):
    inv_freq = ROPE_THETA ** (-jnp.arange(0, QK_ROPE, 2, dtype=F32) / QK_ROPE)
    ang = positions.astype(F32)[:, None] * inv_freq
    cos, sin = jnp.cos(ang), jnp.sin(ang)
    S = positions.shape[0]
    one = jnp.ones((S, QK_NOPE), F32)
    z16 = jnp.zeros((S, QK_ROPE // 2), F32)
    zn = jnp.zeros((S, QK_NOPE), F32)
    zt = jnp.zeros((S, HEAD_PAD - QK_NOPE - QK_ROPE), F32)
    c = jnp.concatenate([one, cos, cos, zt], axis=1)
    s1 = jnp.concatenate([zn, -sin, z16, zt], axis=1)
    s2 = jnp.concatenate([zn, z16, sin, zt], axis=1)
    return c, s1, s2


def attn_fwd(qr, kv, kpe, n_heads):
    S = qr.shape[0]
    H = n_heads
    tq = _tile(S, (ATTN_TILE,))
    nq = S // tq
    c2 = (QK_NOPE + QK_ROPE) ** -0.5 * LOG2_E
    nt = (((1,), (1,)), ((), ()))

    def body(q_ref, k_ref, v_ref, kpe_ref, o_ref, lse_ref, kf_ref, vt_ref, m_ref, l_ref, acc_ref):
        qi = pl.program_id(1)

        @pl.when(qi == 0)
        def _():
            kf_ref[...] = k_ref[...] + kpe_ref[...]
            for c in range(nq):
                vt_ref[c] = jnp.transpose(v_ref[c * tq:(c + 1) * tq, :].astype(F32)).astype(BF16)

        q = q_ref[...]
        m_ref[...] = jnp.full((1, tq), -jnp.inf, F32)
        l_ref[...] = jnp.zeros((1, tq), F32)
        acc_ref[...] = jnp.zeros((HEAD_PAD, tq), F32)

        def tile(j, masked):
            k = kf_ref[pl.ds(pl.multiple_of(j * tq, tq), tq), :]
            t = lax.dot_general(k, q, nt, preferred_element_type=F32) * c2
            if masked:
                krow = lax.broadcasted_iota(jnp.int32, (tq, tq), 0)
                qcol = lax.broadcasted_iota(jnp.int32, (tq, tq), 1)
                t = jnp.where(krow <= qcol, t, NEG)
            m_old = m_ref[...]
            m_new = jnp.maximum(m_old, jnp.max(t, axis=0, keepdims=True))
            alpha = jnp.exp2(m_old - m_new)
            p = jnp.exp2(t - m_new)
            l_ref[...] = alpha * l_ref[...] + jnp.sum(p, axis=0, keepdims=True)
            acc_ref[...] = alpha * acc_ref[...] + jnp.dot(vt_ref[j], p.astype(BF16), preferred_element_type=F32)
            m_ref[...] = m_new

        def unmasked(j, carry):
            tile(j, False)
            return carry

        lax.fori_loop(0, qi, unmasked, 0)
        tile(qi, True)
        l = l_ref[...]
        o_ref[...] = jnp.transpose(acc_ref[...] / l)
        lse_ref[...] = m_ref[...] + jnp.log(l) * LOG2_E

    return pl.pallas_call(
        body, name="attn_fwd",
        grid=(H, nq),
        in_specs=[pl.BlockSpec((tq, HEAD_PAD), lambda h, i: (i, h)),
                  pl.BlockSpec((S, HEAD_PAD), lambda h, i: (0, h)),
                  pl.BlockSpec((S, HEAD_PAD), lambda h, i: (0, H + h)),
                  pl.BlockSpec((S, HEAD_PAD), lambda h, i: (0, 0))],
        out_specs=[pl.BlockSpec((tq, HEAD_PAD), lambda h, i: (i, h)),
                   pl.BlockSpec((None, None, 1, tq), lambda h, i: (h, i, 0, 0))],
        out_shape=[jax.ShapeDtypeStruct((S, H * HEAD_PAD), F32), jax.ShapeDtypeStruct((H, nq, 1, tq), F32)],
        scratch_shapes=[pltpu.VMEM((S, HEAD_PAD), BF16), pltpu.VMEM((nq, HEAD_PAD, tq), BF16),
                        pltpu.VMEM((1, tq), F32), pltpu.VMEM((1, tq), F32), pltpu.VMEM((HEAD_PAD, tq), F32)],
        compiler_params=_params(("parallel", "arbitrary")),
    )(qr, kv, kv, kpe)


def attn_delta(o, do, n_heads):
    S = o.shape[0]
    H = n_heads
    tq = _tile(S, (ATTN_TILE,))

    def body(o_ref, do_ref, d_ref):
        d_ref[...] = jnp.sum(o_ref[...] * do_ref[...].astype(F32), axis=1, keepdims=True)

    return pl.pallas_call(
        body, name="attn_delta",
        grid=(H, S // tq),
        in_specs=[pl.BlockSpec((tq, HEAD_PAD), lambda h, i: (i, h)),
                  pl.BlockSpec((tq, HEAD_PAD), lambda h, i: (i, h))],
        out_specs=pl.BlockSpec((None, tq, 1), lambda h, i: (h, i, 0)),
        out_shape=jax.ShapeDtypeStruct((H, S, 1), F32),
        compiler_params=_params(("parallel", "parallel")),
    )(o, do)


def attn_bwd(qr, kv, kpe, do, lse2, delta, n_heads):
    S = qr.shape[0]
    H = n_heads
    tq = _tile(S, (ATTN_TILE,))
    nq = S // tq
    scale = (QK_NOPE + QK_ROPE) ** -0.5
    c2 = scale * LOG2_E
    nt = (((1,), (1,)), ((), ()))
    delta4 = delta.reshape(H, nq, 1, tq)

    def body(k_ref, v_ref, kpe_ref, q_ref, do_ref, lse_ref, dl_ref, dq_ref, dk_ref, dv_ref, dka_ref, dva_ref,
             dqt_ref):
        kj = pl.program_id(1)
        k = k_ref[...] + kpe_ref[...]
        kt = jnp.transpose(k.astype(F32)).astype(BF16)
        v = v_ref[...]

        @pl.when(kj == 0)
        def _():
            dqt_ref[...] = jnp.zeros_like(dqt_ref)

        dka_ref[...] = jnp.zeros_like(dka_ref)
        dva_ref[...] = jnp.zeros_like(dva_ref)

        def tile(i, masked):
            start = pl.multiple_of(i * tq, tq)
            q = q_ref[pl.ds(start, tq), :]
            do = do_ref[pl.ds(start, tq), :]
            t = lax.dot_general(k, q, nt, preferred_element_type=F32) * c2
            if masked:
                krow = lax.broadcasted_iota(jnp.int32, (tq, tq), 0)
                qcol = lax.broadcasted_iota(jnp.int32, (tq, tq), 1)
                t = jnp.where(krow <= qcol, t, NEG)
            pt = jnp.exp2(t - lse_ref[i])
            dva_ref[...] += jnp.dot(pt.astype(BF16), do, preferred_element_type=F32)
            dpt = lax.dot_general(v, do, nt, preferred_element_type=F32)
            dst = (pt * (dpt - dl_ref[i]) * scale).astype(BF16)
            dka_ref[...] += jnp.dot(dst, q, preferred_element_type=F32)
            dqt_ref[i] += jnp.dot(kt, dst, preferred_element_type=F32)

        tile(kj, True)

        def unmasked(i, carry):
            tile(i, False)
            return carry

        lax.fori_loop(kj + 1, nq, unmasked, 0)
        dk_ref[...] = dka_ref[...]
        dv_ref[...] = dva_ref[...]

        @pl.when(kj == nq - 1)
        def _():
            for c in range(nq):
                dq_ref[c * tq:(c + 1) * tq, :] = jnp.transpose(dqt_ref[c])

    blk = pl.BlockSpec((tq, HEAD_PAD), lambda h, j: (j, h))
    whole = pl.BlockSpec((S, HEAD_PAD), lambda h, j: (0, h))
    stat = pl.BlockSpec((None, nq, 1, tq), lambda h, j: (h, 0, 0, 0))
    shp = jax.ShapeDtypeStruct((S, H * HEAD_PAD), F32)
    return pl.pallas_call(
        body, name="attn_bwd",
        grid=(H, nq),
        in_specs=[blk, pl.BlockSpec((tq, HEAD_PAD), lambda h, j: (j, H + h)),
                  pl.BlockSpec((tq, HEAD_PAD), lambda h, j: (j, 0)), whole, whole, stat, stat],
        out_specs=[whole, blk, blk],
        out_shape=[shp, shp, shp],
        scratch_shapes=[pltpu.VMEM((tq, HEAD_PAD), F32), pltpu.VMEM((tq, HEAD_PAD), F32),
                        pltpu.VMEM((nq, HEAD_PAD, tq), F32)],
        compiler_params=_params(("parallel", "arbitrary")),
    )(kv, kv, kpe, qr, do, lse2, delta4)


def _pad_heads(w, width):
    R = w.shape[0]
    w3 = w.reshape(R, -1, width)
    return jnp.pad(w3, ((0, 0), (0, 0), (0, HEAD_PAD - width))).reshape(R, -1)


def _unpad_heads(w, width):
    R = w.shape[0]
    return w.reshape(R, -1, HEAD_PAD)[:, :, :width].reshape(R, -1)


def mla_pad_weights(p):
    H = N_HEADS
    w_q_b = _pad_heads(p["w_q_b"], QK_NOPE + QK_ROPE)
    kvb = p["w_kv_b"].reshape(KV_LORA, H, QK_NOPE + V_HEAD)
    wk = _pad_heads(kvb[:, :, :QK_NOPE].reshape(KV_LORA, -1), QK_NOPE)
    wv = _pad_heads(kvb[:, :, QK_NOPE:].reshape(KV_LORA, -1), V_HEAD)
    D = p["w_kv_a"].shape[0]
    a = p["w_kv_a"]
    w_kv_a = jnp.concatenate([a[:, :KV_LORA], jnp.zeros((D, QK_NOPE), a.dtype), a[:, KV_LORA:],
                              jnp.zeros((D, HEAD_PAD - QK_NOPE - QK_ROPE), a.dtype)], axis=1)
    wo = p["w_o"].reshape(H, V_HEAD, -1)
    w_o = jnp.pad(wo, ((0, 0), (0, HEAD_PAD - V_HEAD), (0, 0))).reshape(H * HEAD_PAD, -1)
    return dict(w_q_a=p["w_q_a"], w_q_b=w_q_b, w_kv_b=jnp.concatenate([wk, wv], axis=1), w_kv_a=w_kv_a, w_o=w_o)


def mla_kv_fwd(h, g, sh, sc, kv_a_norm_g, pw, tabs):
    hkv = norm_mod(h, g, sh, sc, "kv_norm_mod")
    ckvp = mm(hkv, pw["w_kv_a"], "nn", "kv_a")

    def f(ckv, kpe, c, s1, s2, g):
        xhat, _ = _rms(ckv)
        return (xhat * g).astype(BF16), _rope(kpe, c, s1, s2).astype(BF16)
    ckv_n, kpe_r = rowwise(f, [(ckvp, KV_LORA, 0), (ckvp, HEAD_PAD, KV_LORA // HEAD_PAD), *tabs], [kv_a_norm_g],
                           [(KV_LORA, BF16), (HEAD_PAD, BF16)], [], "kv_a_norm_rope")
    kv = mm(ckv_n, pw["w_kv_b"], "nn", "kv_b", out_dtype=BF16)
    return kv, kpe_r, (h, hkv, ckvp, ckv_n)


def mla_kv_bwd(dk, dv, saved, g, sc, kv_a_norm_g, pw, tabs):
    h, hkv, ckvp, ckv_n = saved
    H = N_HEADS
    lane = jnp.arange(HEAD_PAD)
    pe_mask = ((lane >= QK_NOPE) & (lane < QK_NOPE + QK_ROPE)).astype(F32)[None, :]

    def f(dk, dv, c, s1, s2, mask):
        tot = dk[:, :HEAD_PAD]
        for hh in range(1, H):
            tot = tot + dk[:, hh * HEAD_PAD:(hh + 1) * HEAD_PAD]
        dkpe = _rope_t(tot * mask, c, s1, s2) * mask
        return jnp.concatenate([dk, dv], axis=1).astype(BF16), dkpe
    dkv, dkpe = rowwise(f, [dk, dv, *tabs], [pe_mask], [(2 * H * HEAD_PAD, BF16), (HEAD_PAD, F32)], [],
                        "kv_split_bwd")
    d_w_kv_b = mm(ckv_n, dkv, "tn", "kv_b_dw")
    dckv_n = mm(dkv, pw["w_kv_b"], "nt", "kv_b_dx")

    def f2(ckv, dn, dkpe, g):
        xhat, r = _rms(ckv)
        dx = _rms_bwd(xhat, r, dn * g)
        return jnp.concatenate([dx, dkpe], axis=1).astype(BF16), dn * xhat
    dckvp, d_kv_a_g = rowwise(f2, [(ckvp, KV_LORA, 0), dckv_n, dkpe], [kv_a_norm_g],
                              [(KV_LORA + HEAD_PAD, BF16)], [KV_LORA], "kv_a_norm_bwd")
    d_w_kv_a = mm(hkv, dckvp, "tn", "kv_a_dw")
    dhkv = mm(dckvp, pw["w_kv_a"], "nt", "kv_a_dx")
    dh, d_sh, d_sc, d_g = norm_mod_bwd(h, dhkv, None, g, sc, "norm_mod_bwd_nores")
    return dh, (d_sh, d_sc, d_g), d_kv_a_g, d_w_kv_a, d_w_kv_b


def mla_fwd(h, g, sh, sc, gate, q_a_norm_g, pw, kv, kpe_r, tabs):
    H = N_HEADS
    hn = norm_mod(h, g, sh, sc, "mla_norm_mod")
    qa = mm(hn, pw["w_q_a"], "nn", "q_a")

    def f(qa, g):
        xhat, _ = _rms(qa)
        return (xhat * g).astype(BF16)
    qa_n = rowwise(f, [qa], [q_a_norm_g], [(qa.shape[1], BF16)], [], "q_a_norm")[0]
    qp = mm(qa_n, pw["w_q_b"], "nn", "q_b")

    def frope(q, c, s1, s2):
        return jnp.concatenate([_rope(q[:, hh * HEAD_PAD:(hh + 1) * HEAD_PAD], c, s1, s2) for hh in range(H)],
                               axis=1).astype(BF16)
    qr = rowwise(frope, [qp, *tabs], [], [(H * HEAD_PAD, BF16)], [], "q_rope")[0]
    o, lse = attn_fwd(qr, kv, kpe_r, H)
    y = mm(o, pw["w_o"], "nn", "w_o")
    h_out, _ = residual(h, y, gate, 1.0, "mla_residual")
    return h_out, (h, hn, qa, qa_n, qr, o, lse, y)


def mla_bwd(dh_out, saved, g, sc, gate, q_a_norm_g, pw, kv, kpe_r, tabs):
    h, hn, qa, qa_n, qr, o, lse, y = saved
    H = N_HEADS
    dy, d_gate = residual_bwd(dh_out, y, gate, 1.0, "mla_residual_bwd")
    d_w_o = mm(o, dy, "tn", "w_o_dw")
    do = mm(dy, pw["w_o"], "nt", "w_o_dx", out_dtype=BF16)
    delta = attn_delta(o, do, H)
    dqr, dk, dv = attn_bwd(qr, kv, kpe_r, do, lse, delta, H)

    def frope_t(dq, c, s1, s2):
        return jnp.concatenate([_rope_t(dq[:, hh * HEAD_PAD:(hh + 1) * HEAD_PAD], c, s1, s2) for hh in range(H)],
                               axis=1).astype(BF16)
    dqp = rowwise(frope_t, [dqr, *tabs], [], [(H * HEAD_PAD, BF16)], [], "q_rope_bwd")[0]
    d_w_q_b = mm(qa_n, dqp, "tn", "q_b_dw")
    dqa_n = mm(dqp, pw["w_q_b"], "nt", "q_b_dx")

    def f(qa, dn, g):
        xhat, r = _rms(qa)
        return _rms_bwd(xhat, r, dn * g).astype(BF16), dn * xhat
    dqa, d_q_a_g = rowwise(f, [qa, dqa_n], [q_a_norm_g], [(qa.shape[1], BF16)], [qa.shape[1]], "q_a_norm_bwd")
    d_w_q_a = mm(hn, dqa, "tn", "q_a_dw")
    dhn = mm(dqa, pw["w_q_a"], "nt", "q_a_dx")
    dh_in, d_sh, d_sc, d_g = norm_mod_bwd(h, dhn, dh_out, g, sc, "norm_mod_bwd")
    grads = dict(w_q_a=d_w_q_a, q_a_norm_g=d_q_a_g, w_q_b=d_w_q_b, w_o=d_w_o)
    return dh_in, (d_sh, d_sc, d_gate, d_g), grads, dk, dv


def loss_head(h, target, g):
    D = h.shape[1]

    def f(h, t, g):
        xhat, r = _rms(h)
        err = xhat * g - t
        dy = err * (1.0 / D)
        dh = _rms_bwd(xhat, r, dy * g)
        return dh, (0.5 / D) * err * err, dy * xhat
    return rowwise(f, [h, target], [g], [(D, F32)], [D, D], "loss_head")


def _place():
    x, y, c = lax.axis_index("x"), lax.axis_index("y"), lax.axis_index("c")
    chips = [(1 - x, y), (x, 1 - y), (1 - x, 1 - y)]
    return x, y, c, chips


HBM_SPEC = pl.BlockSpec(memory_space=pltpu.HBM)


def all_gather8(v):
    m, n = v.shape

    def body(x_ref, out_ref, send_sems, recv_sems, local_sem):
        x, y, c, chips = _place()
        me, sibling = (x, y, c), (x, y, 1 - c)

        def rows(px, py, pc):
            return out_ref.at[4 * px + 2 * py + pc]

        def copy(k, block, to, src=None):
            return pltpu.make_async_remote_copy(
                src_ref=rows(*block) if src is None else src, dst_ref=rows(*block),
                send_sem=send_sems.at[k], recv_sem=recv_sems.at[k], device_id=to, device_id_type=MESH)

        mine = pltpu.make_async_copy(x_ref, rows(*me), local_sem)
        mine.start()
        first = [copy(0, me, sibling, src=x_ref)]
        first += [copy(1 + j, me, (*chip, c), src=x_ref) for j, chip in enumerate(chips)]
        for cp in first:
            cp.start()
        passed = [copy(4 + j, (*chip, c), sibling) for j, chip in enumerate(chips)]
        for j, chip in enumerate(chips):
            copy(1 + j, (*chip, c), me).wait_recv()
            passed[j].start()
        copy(0, sibling, me).wait_recv()
        for j, chip in enumerate(chips):
            copy(4 + j, (*chip, 1 - c), me).wait_recv()
        for cp in first + passed:
            cp.wait_send()
        mine.wait()

    return pl.pallas_call(
        body, name="all_gather8",
        out_shape=jax.ShapeDtypeStruct((8, m, n), v.dtype),
        in_specs=[pl.BlockSpec(memory_space=pltpu.VMEM)],
        out_specs=pl.BlockSpec(memory_space=pltpu.VMEM),
        scratch_shapes=[pltpu.SemaphoreType.DMA((7,)), pltpu.SemaphoreType.DMA((7,)), pltpu.SemaphoreType.DMA],
        compiler_params=pltpu.CompilerParams(vmem_limit_bytes=VMEM_LIMIT_BYTES),
    )(v)


def gather_weights(bufs):
    n = len(bufs)

    def body(*refs):
        ins, outs = refs[:n], refs[n:2 * n]
        send_sems, recv_sems = refs[2 * n:]
        x, y, c, chips = _place()
        sibling = (x, y, 1 - c)
        me = 2 * x + y

        def idx(chip):
            return 2 * chip[0] + chip[1]

        def copy(w, k, src, dst, to):
            return pltpu.make_async_remote_copy(src_ref=src, dst_ref=dst, send_sem=send_sems.at[6 * w + k],
                                                recv_sem=recv_sems.at[6 * w + k], device_id=to, device_id_type=MESH)

        first = [copy(w, j, ins[w].at[me, c], outs[w].at[me, c], (*chip, c))
                 for w in range(n) for j, chip in enumerate(chips)]
        for cp in first:
            cp.start()
        passed = []
        for w in range(n):
            for j, chip in enumerate(chips):
                landed = outs[w].at[idx(chip), c]
                copy(w, j, landed, landed, (*chip, c)).wait_recv()
                fwd = copy(w, 3 + j, landed, landed, sibling)
                fwd.start()
                passed.append(fwd)
        for w in range(n):
            for j, chip in enumerate(chips):
                other = outs[w].at[idx(chip), 1 - c]
                copy(w, 3 + j, other, other, sibling).wait_recv()
        for cp in first + passed:
            cp.wait_send()

    return pl.pallas_call(
        body, name="gather_weights",
        out_shape=[jax.ShapeDtypeStruct(b.shape, b.dtype) for b in bufs],
        in_specs=[HBM_SPEC] * n, out_specs=[HBM_SPEC] * n,
        input_output_aliases={w: w for w in range(n)},
        scratch_shapes=[pltpu.SemaphoreType.DMA((6 * n,)), pltpu.SemaphoreType.DMA((6 * n,))],
    )(*bufs)


def exchange_halves(gs):
    n = len(gs)

    def body(*refs):
        ins, theirs = refs[:n], refs[n:2 * n]
        send_sems, recv_sems = refs[2 * n:]
        x, y, c, _ = _place()
        sends = [pltpu.make_async_remote_copy(src_ref=ins[w].at[:, 1 - c], dst_ref=theirs[w],
                                              send_sem=send_sems.at[w], recv_sem=recv_sems.at[w],
                                              device_id=(x, y, 1 - c), device_id_type=MESH) for w in range(n)]
        for cp in sends:
            cp.start()
        for cp in sends:
            cp.wait()

    return pl.pallas_call(
        body, name="exchange_halves",
        out_shape=[jax.ShapeDtypeStruct((4,) + g.shape[2:], g.dtype) for g in gs],
        in_specs=[HBM_SPEC] * n, out_specs=[HBM_SPEC] * n,
        scratch_shapes=[pltpu.SemaphoreType.DMA((n,)), pltpu.SemaphoreType.DMA((n,))],
    )(*gs)


def scatter_blocks(ps):
    n = len(ps)

    def body(*refs):
        ins, outs = refs[:n], refs[n:2 * n]
        send_sems, recv_sems = refs[2 * n:]
        x, y, c, chips = _place()
        sends = [pltpu.make_async_remote_copy(src_ref=ins[w].at[2 * chip[0] + chip[1]], dst_ref=outs[w].at[j],
                                              send_sem=send_sems.at[3 * w + j], recv_sem=recv_sems.at[3 * w + j],
                                              device_id=(*chip, c), device_id_type=MESH)
                 for w in range(n) for j, chip in enumerate(chips)]
        for cp in sends:
            cp.start()
        for cp in sends:
            cp.wait()

    return pl.pallas_call(
        body, name="scatter_blocks",
        out_shape=[jax.ShapeDtypeStruct((3,) + p.shape[1:], p.dtype) for p in ps],
        in_specs=[HBM_SPEC] * n, out_specs=[HBM_SPEC] * n,
        scratch_shapes=[pltpu.SemaphoreType.DMA((3 * n,)), pltpu.SemaphoreType.DMA((3 * n,))],
    )(*ps)


def join_halves(qs):
    n = len(qs)

    def body(*refs):
        ins, outs = refs[:n], refs[n:2 * n]
        send_sems, recv_sems = refs[2 * n:]
        x, y, c, _ = _place()
        sends = [pltpu.make_async_remote_copy(src_ref=ins[w].at[c], dst_ref=outs[w].at[c], send_sem=send_sems.at[w],
                                              recv_sem=recv_sems.at[w], device_id=(x, y, 1 - c), device_id_type=MESH)
                 for w in range(n)]
        for cp in sends:
            cp.start()
        for w in range(n):
            other = outs[w].at[1 - c]
            pltpu.make_async_remote_copy(src_ref=other, dst_ref=other, send_sem=send_sems.at[w],
                                         recv_sem=recv_sems.at[w], device_id=(x, y, 1 - c),
                                         device_id_type=MESH).wait_recv()
        for cp in sends:
            cp.wait_send()

    return pl.pallas_call(
        body, name="join_halves",
        out_shape=[jax.ShapeDtypeStruct(q.shape, q.dtype) for q in qs],
        in_specs=[HBM_SPEC] * n, out_specs=[HBM_SPEC] * n,
        input_output_aliases={w: w for w in range(n)},
        scratch_shapes=[pltpu.SemaphoreType.DMA((n,)), pltpu.SemaphoreType.DMA((n,))],
    )(*qs)


def _row_tile(R, row_bytes):
    tm = R
    for t in (512, 256, 128, 64, 32, 16, 8):
        if R % t == 0:
            tm = t
            if t * row_bytes <= ROW_TILE_BUDGET:
                break
    return tm


def sum_siblings(g, theirs, place):
    _, _, R, C = g.shape
    tm = _row_tile(R, 3 * C * 4)

    def body(place_ref, a_ref, b_ref, o_ref):
        o_ref[...] = (a_ref[...] + b_ref[...]).astype(BF16)

    return pl.pallas_call(
        body, name="sum_siblings",
        grid_spec=pltpu.PrefetchScalarGridSpec(
            num_scalar_prefetch=1, grid=(4, R // tm),
            in_specs=[pl.BlockSpec((None, None, tm, C), lambda j, i, s: (j, s[1], i, 0)),
                      pl.BlockSpec((None, tm, C), lambda j, i, s: (j, i, 0))],
            out_specs=pl.BlockSpec((None, tm, C), lambda j, i, s: (j, i, 0))),
        out_shape=jax.ShapeDtypeStruct((4, R, C), BF16),
        compiler_params=_params(("parallel", "parallel")),
    )(place, g, theirs)


def sum_chips(p, landed, place):
    _, R, C = p.shape
    tm = _row_tile(R, 5 * C * 4)

    def body(place_ref, p_ref, l0_ref, l1_ref, l2_ref, o_ref):
        o_ref[...] = ((p_ref[...].astype(F32) + l0_ref[...].astype(F32)) + l1_ref[...].astype(F32)
                      ) + l2_ref[...].astype(F32)

    return pl.pallas_call(
        body, name="sum_chips",
        grid_spec=pltpu.PrefetchScalarGridSpec(
            num_scalar_prefetch=1, grid=(R // tm,),
            in_specs=[pl.BlockSpec((None, tm, C), lambda i, s: (s[0], i, 0))]
            + [pl.BlockSpec((None, tm, C), lambda i, s, j=j: (j, i, 0)) for j in range(3)],
            out_specs=pl.BlockSpec((None, tm, C), lambda i, s: (s[1], i, 0))),
        out_shape=jax.ShapeDtypeStruct((2, R, C), F32),
        compiler_params=_params(("parallel",)),
    )(place, p, landed, landed, landed)


def sum_blocks(items, name):
    R, C = items[0][0].shape[1:]
    tm = R
    for t in (512, 256, 128, 64, 32, 16, 8):
        if R % t == 0:
            tm = t
            if t * C * 4 * (len(items) + 1) <= ROW_TILE_BUDGET:
                break
    n = len(items)

    def body(*refs):
        acc = refs[0][...].astype(F32)
        for r in refs[1:n]:
            acc = acc + r[...].astype(F32)
        refs[n][...] = acc

    return pl.pallas_call(
        body, name=name,
        grid=(R // tm,),
        in_specs=[pl.BlockSpec((None, tm, C), lambda i, j=j: (j, i, 0)) for _, j in items],
        out_specs=pl.BlockSpec((tm, C), lambda i: (i, 0)),
        out_shape=jax.ShapeDtypeStruct((R, C), F32),
        compiler_params=_params(("parallel",)),
    )(*[a for a, _ in items])


def reduce_scatter_grads(gs, place):
    theirs = exchange_halves(gs)
    ps = [sum_siblings(g, t, place) for g, t in zip(gs, theirs)]
    landed = scatter_blocks(ps)
    qs = [sum_chips(p, l, place) for p, l in zip(ps, landed)]
    joined = join_halves(qs)
    return [j.reshape(2 * j.shape[1], j.shape[2]) for j in joined]


def adamw(w, g, m, v):
    shape = w.shape
    C = shape[-1]
    R = w.size // C
    tm = R
    for t in (512, 256, 128, 64, 32, 16, 8):
        if R % t == 0:
            tm = t
            if t * C * 4 * 7 <= ROW_TILE_BUDGET:
                break

    def f(w, g, m, v):
        m = ADAM_B1 * m + (1.0 - ADAM_B1) * g
        v = ADAM_B2 * v + (1.0 - ADAM_B2) * (g * g)
        m_hat = m / (1.0 - ADAM_B1 ** ADAM_STEP)
        v_hat = v / (1.0 - ADAM_B2 ** ADAM_STEP)
        delta = -ADAM_LR * (m_hat / (jnp.sqrt(v_hat) + ADAM_EPS) + ADAM_WD * w)
        return delta, m, v

    d, nm, nv = rowwise(f, [a.reshape(R, C) for a in (w, g, m, v)], [], [(C, F32)] * 3, [], "adamw", tm=tm)
    return d.reshape(shape), nm.reshape(shape), nv.reshape(shape)


def _cast_into_slot(w, place):
    C = w.shape[-1]
    w2 = w.reshape(-1, C)
    R = w2.shape[0]
    tm = _row_tile(R, 6 * C)

    def body(place_ref, w_ref, o_ref):
        o_ref[...] = w_ref[...].astype(BF16)

    out = pl.pallas_call(
        body, name="cast_bf16",
        grid_spec=pltpu.PrefetchScalarGridSpec(
            num_scalar_prefetch=1, grid=(R // tm,),
            in_specs=[pl.BlockSpec((tm, C), lambda i, s: (i, 0))],
            out_specs=pl.BlockSpec((None, tm, C), lambda i, s: (s[0], i, 0))),
        out_shape=jax.ShapeDtypeStruct((4, R, C), BF16),
        compiler_params=_params(("parallel",)),
    )(place, w2)
    return out.reshape(4, 2, R // 2, C)


def _pack(vs):
    flat = jnp.concatenate([v.reshape(-1) for v in vs])
    n = flat.shape[0]
    total = -(-n // 1024) * 1024
    return jnp.pad(flat, (0, total - n)).reshape(total // 128, 128)


def _unpack(flat, like):
    out, o = [], 0
    for shp in like:
        sz = 1
        for d in shp:
            sz *= d
        out.append(flat[o:o + sz].reshape(shp))
        o += sz
    return out


def _cols_to_blocks(g, n_chips=4):
    R, N = g.shape
    C = N // n_chips
    return g.reshape(R, n_chips, C).transpose(1, 0, 2).reshape(n_chips, 2, R // 2, C)


def _rows_to_blocks(g, n_chips=4):
    R, C = g.shape
    return g.reshape(n_chips, 2, R // n_chips // 2, C)


def kernel(x, c, positions, ada_w, ada_b, norm_g, ffn_w13, ffn_w2, conv_w_pw1, conv_b_pw1, conv_w_dw, conv_b_dw, conv_ln_g, conv_ln_b, conv_w_pw2, conv_b_pw2, kv_ada_w, kv_ada_b, kv_norm_g, w_kv_a, kv_a_norm_g, w_kv_b, w_q_a, q_a_norm_g, w_q_b, w_o, final_norm_g, loss_target, m_ada_w, m_ada_b, m_norm_g, m_ffn_w13, m_ffn_w2, m_conv_w_pw1, m_conv_b_pw1, m_conv_w_dw, m_conv_b_dw, m_conv_ln_g, m_conv_ln_b, m_conv_w_pw2, m_conv_b_pw2, m_kv_ada_w, m_kv_ada_b, m_kv_norm_g, m_w_kv_a, m_kv_a_norm_g, m_w_kv_b, m_w_q_a, m_q_a_norm_g, m_w_q_b, m_w_o, m_final_norm_g, v_ada_w, v_ada_b, v_norm_g, v_ffn_w13, v_ffn_w2, v_conv_w_pw1, v_conv_b_pw1, v_conv_w_dw, v_conv_b_dw, v_conv_ln_g, v_conv_ln_b, v_conv_w_pw2, v_conv_b_pw2, v_kv_ada_w, v_kv_ada_b, v_kv_norm_g, v_w_kv_a, v_kv_a_norm_g, v_w_kv_b, v_w_q_a, v_q_a_norm_g, v_w_q_b, v_w_o, v_final_norm_g):
    S, D = x.shape[1], x.shape[2]
    H = N_HEADS
    F = ffn_w2.shape[2] * 4
    xi, yi, ci = lax.axis_index("x"), lax.axis_index("y"), lax.axis_index("c")
    chip = 2 * xi + yi
    dev = 2 * chip + ci
    place = jnp.stack([chip, ci]).astype(jnp.int32)
    h0 = x[0]
    target = loss_target[0]

    silu_c = rowwise(lambda a: a * _sigmoid(a), [c], [], [(D, F32)], [], "silu_c")[0]
    silu_all = all_gather8(silu_c.reshape(8, D // 8)).reshape(8, D)
    n_ada = ada_w.shape[2]
    n_kv = kv_ada_w.shape[1]
    ada_b_mine = lax.dynamic_slice_in_dim(ada_b, chip * n_ada, n_ada, axis=1)
    kv_b_mine = lax.dynamic_slice_in_dim(kv_ada_b, chip * n_kv, n_kv, axis=0)[None, :]
    mods = [mm(silu_all, ada_w[l], "nn", "ada_rows", bias=ada_b_mine[l:l + 1]) for l in range(2)]
    mods.append(mm(silu_all, kv_ada_w, "nn", "kv_ada_rows", bias=kv_b_mine))
    n_mod_cols = 2 * n_ada + n_kv
    mod_pack = jnp.concatenate(mods, axis=1).reshape(-1, 128)
    mod_all = all_gather8(mod_pack).reshape(8, 8, n_mod_cols)[0::2]
    mod_mine = lax.dynamic_index_in_dim(mod_all, dev, axis=1, keepdims=False)
    mod = [mod_mine[:, l * n_ada:(l + 1) * n_ada].reshape(N_MOD, D) for l in range(2)]
    kv_mod = mod_mine[:, 2 * n_ada:].reshape(2, D)
    kv_shift, kv_scale = kv_mod[0:1], kv_mod[1:2]

    def mrow(l, k):
        return mod[l][k:k + 1]

    big = dict(ffn_w13=ffn_w13, ffn_w2=ffn_w2, conv_w_pw1=conv_w_pw1, conv_w_pw2=conv_w_pw2, w_kv_a=w_kv_a,
               w_kv_b=w_kv_b, w_q_a=w_q_a, w_q_b=w_q_b, w_o=w_o)
    names = list(big)
    gathered = gather_weights([_cast_into_slot(big[k], place) for k in names])
    gw = dict(zip(names, gathered))
    small_like = [norm_g.shape, conv_b_pw1.shape, conv_w_dw.shape, conv_b_dw.shape, conv_ln_g.shape,
                  conv_ln_b.shape, conv_b_pw2.shape]
    small_pack = _pack([norm_g, conv_b_pw1, conv_w_dw, conv_b_dw, conv_ln_g, conv_ln_b, conv_b_pw2])
    small_all = all_gather8(small_pack)[0::2].reshape(4, -1)
    per_chip = [_unpack(small_all[j], small_like) for j in range(4)]
    smalls = [jnp.concatenate([per_chip[j][k] for j in range(4)], axis=-1) for k in range(len(small_like))]
    norm_g_f, b_pw1_f, w_dw_f, b_dw_f, ln_g_f, ln_b_f, b_pw2_f = smalls

    w13 = gw["ffn_w13"].reshape(2, 2, 2, 2, D, F // 2).transpose(2, 3, 0, 4, 1, 5).reshape(2, 2, 2, D, F)
    w2 = gw["ffn_w2"].reshape(4, 2, 2, F // 4, D).transpose(1, 2, 0, 3, 4).reshape(2, 2, F, D)
    conv_p = dict(
        w_pw1=gw["conv_w_pw1"].reshape(4, D, 2 * D // 4).transpose(1, 0, 2).reshape(D, 2 * D),
        b_pw1=b_pw1_f, w_dw=w_dw_f[0], b_dw=b_dw_f, ln_g=ln_g_f, ln_b=ln_b_f,
        w_pw2=gw["conv_w_pw2"].reshape(D, D), b_pw2=b_pw2_f)
    q_lora = w_q_a.shape[2]
    mla_p = dict(
        w_kv_a=gw["w_kv_a"].reshape(D, KV_LORA + QK_ROPE),
        w_kv_b=gw["w_kv_b"].reshape(4, KV_LORA, -1).transpose(1, 0, 2).reshape(KV_LORA, -1),
        w_q_a=gw["w_q_a"].reshape(D, q_lora),
        w_q_b=gw["w_q_b"].reshape(4, q_lora, -1).transpose(1, 0, 2).reshape(q_lora, -1),
        w_o=gw["w_o"].reshape(H * V_HEAD, D))
    pw = mla_pad_weights(mla_p)
    tabs = rope_tables(positions[0])

    def ng(l, k):
        return norm_g_f[l, k][None, :]

    h = h0
    h, s_f1_0 = ffn_fwd(h, ng(0, 0), mrow(0, 0), mrow(0, 1), mrow(0, 2), w13[0, 0], w2[0, 0])
    h, s_conv = conv_module_fwd(h, ng(0, 1), mrow(0, 3), mrow(0, 4), mrow(0, 5), conv_p)
    h, s_f2_0 = ffn_fwd(h, ng(0, 2), mrow(0, 6), mrow(0, 7), mrow(0, 8), w13[0, 1], w2[0, 1])
    kv_norm = kv_norm_g[None, :]
    kv_a_g = kv_a_norm_g[None, :]
    kv, kpe_r, s_kv = mla_kv_fwd(h, kv_norm, kv_shift, kv_scale, kv_a_g, pw, tabs)
    h, s_f1_1 = ffn_fwd(h, ng(1, 0), mrow(1, 0), mrow(1, 1), mrow(1, 2), w13[1, 0], w2[1, 0])
    h, s_mla = mla_fwd(h, ng(1, 1), mrow(1, 3), mrow(1, 4), mrow(1, 5), q_a_norm_g, pw, kv, kpe_r, tabs)
    h, s_f2_1 = ffn_fwd(h, ng(1, 2), mrow(1, 6), mrow(1, 7), mrow(1, 8), w13[1, 1], w2[1, 1])
    dh, loss_cols, d_final_g = loss_head(h, target, final_norm_g[None, :])

    dh, v_f2_1, dw13_11, dw2_11 = ffn_bwd(dh, s_f2_1, ng(1, 2), mrow(1, 7), mrow(1, 8), w13[1, 1], w2[1, 1])
    dh, v_mla, g_mla, dk, dv = mla_bwd(dh, s_mla, ng(1, 1), mrow(1, 4), mrow(1, 5), q_a_norm_g, pw, kv, kpe_r, tabs)
    dh, v_f1_1, dw13_10, dw2_10 = ffn_bwd(dh, s_f1_1, ng(1, 0), mrow(1, 1), mrow(1, 2), w13[1, 0], w2[1, 0])
    dh_kv, v_kv, d_kv_a_g, d_w_kv_a, d_w_kv_b = mla_kv_bwd(dk, dv, s_kv, kv_norm, kv_scale, kv_a_g, pw, tabs)
    dh = rowwise(lambda a, b: a + b, [dh, dh_kv], [], [(D, F32)], [], "add_stream")[0]
    dh, v_f2_0, dw13_01, dw2_01 = ffn_bwd(dh, s_f2_0, ng(0, 2), mrow(0, 7), mrow(0, 8), w13[0, 1], w2[0, 1])
    dh, v_conv, g_conv = conv_module_bwd(dh, s_conv, ng(0, 1), mrow(0, 4), mrow(0, 5), conv_p)
    dh, v_f1_0, dw13_00, dw2_00 = ffn_bwd(dh, s_f1_0, ng(0, 0), mrow(0, 1), mrow(0, 2), w13[0, 0], w2[0, 0])
    grad_x = dh[None]

    d_w_kv_a_u = jnp.concatenate([d_w_kv_a[:, :KV_LORA], d_w_kv_a[:, KV_LORA + QK_NOPE:KV_LORA + QK_NOPE + QK_ROPE]],
                                 axis=1)
    hk = H * HEAD_PAD
    dkb = jnp.concatenate([d_w_kv_b[:, :hk].reshape(KV_LORA, H, HEAD_PAD)[:, :, :QK_NOPE],
                           d_w_kv_b[:, hk:].reshape(KV_LORA, H, HEAD_PAD)[:, :, :V_HEAD]], axis=2).reshape(KV_LORA, -1)
    d_w_q_b_u = _unpad_heads(g_mla["w_q_b"], QK_NOPE + QK_ROPE)
    d_w_o_u = g_mla["w_o"].reshape(H, HEAD_PAD, D)[:, :V_HEAD].reshape(H * V_HEAD, D)
    full = [dw.reshape(4, 2, D // 2, F // 2) for dw in (dw13_00, dw13_01, dw13_10, dw13_11)] + [
            _rows_to_blocks(dw2_00), _rows_to_blocks(dw2_01), _rows_to_blocks(dw2_10), _rows_to_blocks(dw2_11),
            _cols_to_blocks(g_conv["w_pw1"]), _rows_to_blocks(g_conv["w_pw2"]), _rows_to_blocks(d_w_kv_a_u),
            _cols_to_blocks(dkb), _rows_to_blocks(g_mla["w_q_a"]), _cols_to_blocks(d_w_q_b_u),
            _rows_to_blocks(d_w_o_u)]
    red = reduce_scatter_grads(full, place)
    g_ffn_w13 = jnp.stack(red[0:4]).reshape(ffn_w13.shape)
    g_ffn_w2 = jnp.stack(red[4:8]).reshape(ffn_w2.shape)
    g_conv_w_pw1 = red[8].reshape(conv_w_pw1.shape)
    g_conv_w_pw2 = red[9].reshape(conv_w_pw2.shape)
    g_w_kv_a = red[10].reshape(w_kv_a.shape)
    g_w_kv_b = red[11].reshape(w_kv_b.shape)
    g_w_q_a = red[12].reshape(w_q_a.shape)
    g_w_q_b = red[13].reshape(w_q_b.shape)
    g_w_o = red[14].reshape(w_o.shape)

    def dmod(v1, vm, v2):
        return jnp.concatenate([v1[0], v1[1], v1[2], vm[0], vm[1], vm[2], v2[0], v2[1], v2[2]], axis=1)
    d_mod0 = dmod(v_f1_0, v_conv, v_f2_0)
    d_mod1 = dmod(v_f1_1, v_mla, v_f2_1)
    d_kv_mod = jnp.concatenate([v_kv[0], v_kv[1]], axis=1)
    d_norm_g = jnp.concatenate([v_f1_0[3], v_conv[3], v_f2_0[3], v_f1_1[3], v_mla[3], v_f2_1[3]], axis=0)
    vec_list = [d_mod0, d_mod1, d_kv_mod, d_norm_g, g_conv["b_pw1"], g_conv["w_dw"], g_conv["b_dw"], g_conv["ln_g"],
                g_conv["ln_b"], g_conv["b_pw2"], v_kv[2], d_kv_a_g, g_mla["q_a_norm_g"], d_final_g, loss_cols]
    vec_like = [v.shape for v in vec_list]
    vec_pack = _pack(vec_list)
    n_mod_rows = (2 * N_MOD * D + 2 * D) // 128
    vec_all = all_gather8(vec_pack)
    vec_sum = sum_blocks([(vec_all, d) for d in range(8)], "sum_devices").reshape(-1)
    (_, _, _, s_norm_g, s_b_pw1, s_w_dw, s_b_dw, s_ln_g, s_ln_b, s_b_pw2, s_kv_norm_g, s_kv_a_g, s_q_a_g,
     s_final_g, s_loss) = _unpack(vec_sum, vec_like)
    loss = jnp.sum(s_loss)
    dmod_all = vec_all[:, :n_mod_rows].reshape(8, 2 * N_MOD * D + 2 * D)
    dmod_sum = vec_sum[:2 * N_MOD * D + 2 * D]
    g_ada_b = dmod_sum[:2 * N_MOD * D].reshape(2, N_MOD * D)
    g_kv_ada_b = dmod_sum[2 * N_MOD * D:]
    g_ada_w = []
    for l in range(2):
        cols = lax.dynamic_slice_in_dim(dmod_all[:, l * N_MOD * D:(l + 1) * N_MOD * D], chip * n_ada, n_ada, axis=1)
        g_ada_w.append(mm(silu_all, cols, "tn", "ada_w_grad"))
    g_ada_w = jnp.stack(g_ada_w)
    kv_cols = lax.dynamic_slice_in_dim(dmod_all[:, 2 * N_MOD * D:], chip * n_kv, n_kv, axis=1)
    g_kv_ada_w = mm(silu_all, kv_cols, "tn", "kv_ada_w_grad")

    def shard(v, width):
        return lax.dynamic_slice_in_dim(v, chip * width, width, axis=v.ndim - 1)

    Dq = D // 4
    g_norm_g = shard(s_norm_g.reshape(2, 3, D), Dq)
    g_conv_b_pw1 = shard(s_b_pw1, 2 * D // 4)
    g_conv_w_dw = shard(s_w_dw, Dq)[None]
    g_conv_b_dw = shard(s_b_dw, Dq)
    g_conv_ln_g = shard(s_ln_g, Dq)
    g_conv_ln_b = shard(s_ln_b, Dq)
    g_conv_b_pw2 = shard(s_b_pw2, Dq)

    grads = [g_ada_w, g_ada_b, g_norm_g, g_ffn_w13, g_ffn_w2, g_conv_w_pw1, g_conv_b_pw1, g_conv_w_dw, g_conv_b_dw,
             g_conv_ln_g, g_conv_ln_b, g_conv_w_pw2, g_conv_b_pw2, g_kv_ada_w, g_kv_ada_b, s_kv_norm_g[0], g_w_kv_a,
             s_kv_a_g[0], g_w_kv_b, g_w_q_a, s_q_a_g, g_w_q_b, g_w_o, s_final_g[0]]
    weights = [ada_w, ada_b, norm_g, ffn_w13, ffn_w2, conv_w_pw1, conv_b_pw1, conv_w_dw, conv_b_dw, conv_ln_g,
               conv_ln_b, conv_w_pw2, conv_b_pw2, kv_ada_w, kv_ada_b, kv_norm_g, w_kv_a, kv_a_norm_g, w_kv_b, w_q_a,
               q_a_norm_g, w_q_b, w_o, final_norm_g]
    ms = [m_ada_w, m_ada_b, m_norm_g, m_ffn_w13, m_ffn_w2, m_conv_w_pw1, m_conv_b_pw1, m_conv_w_dw, m_conv_b_dw,
          m_conv_ln_g, m_conv_ln_b, m_conv_w_pw2, m_conv_b_pw2, m_kv_ada_w, m_kv_ada_b, m_kv_norm_g, m_w_kv_a,
          m_kv_a_norm_g, m_w_kv_b, m_w_q_a, m_q_a_norm_g, m_w_q_b, m_w_o, m_final_norm_g]
    vs = [v_ada_w, v_ada_b, v_norm_g, v_ffn_w13, v_ffn_w2, v_conv_w_pw1, v_conv_b_pw1, v_conv_w_dw, v_conv_b_dw,
          v_conv_ln_g, v_conv_ln_b, v_conv_w_pw2, v_conv_b_pw2, v_kv_ada_w, v_kv_ada_b, v_kv_norm_g, v_w_kv_a,
          v_kv_a_norm_g, v_w_kv_b, v_w_q_a, v_q_a_norm_g, v_w_q_b, v_w_o, v_final_norm_g]
    grads = [g.reshape(w.shape) for g, w in zip(grads, weights)]
    deltas, new_m, new_v = [], [], []
    for w, g, m, v in zip(weights, grads, ms, vs):
        d, nm, nv = adamw(w, g, m, v)
        deltas.append(d)
        new_m.append(nm)
        new_v.append(nv)
    return (loss, grad_x, *grads, *deltas, *new_m, *new_v)
```

```python
import jax
import jax.numpy as jnp
from jax import lax
from jax.experimental import pallas as pl
from jax.experimental.pallas import tpu as pltpu

F32 = jnp.float32
BF16 = jnp.bfloat16
MESH = pl.DeviceIdType.MESH

N_HEADS = 16
QK_NOPE = 64
QK_ROPE = 32
V_HEAD = 64
KV_LORA = 256
CONV_WIDTH = 31
ROPE_THETA = 10000.0
EPS = 1e-6
N_MOD = 9
HEAD_PAD = 128
ATTN_TILE = 512
CONV_HALO = 32

ADAM_LR = 0.001
ADAM_B1 = 0.9
ADAM_B2 = 0.999
ADAM_EPS = 1e-08
ADAM_WD = 0.01
ADAM_STEP = 10

VMEM_LIMIT_BYTES = 56 * 2 ** 20
ROW_TILE_BUDGET = 10 * 2 ** 20
MM_VMEM_BUDGET = 40 * 2 ** 20
NEG = float(jnp.finfo(jnp.float32).min)
LOG2_E = 1.4426950408889634


def _tile(n, prefs):
    for t in prefs:
        if n % t == 0:
            return t
    return n


def _params(sem):
    return pltpu.CompilerParams(dimension_semantics=sem, vmem_limit_bytes=VMEM_LIMIT_BYTES)


def _mm_tiles(M, N, K, mode, a_bytes, b_bytes, o_bytes):
    if mode == "tn":
        tk_opts = [t for t in (2048, 1024, 512, 256, 128) if K % t == 0] or [K]
        tm_opts = ([M] if M <= 2816 else []) + [t for t in (1024, 512, 256, 128) if M % t == 0 and t < M]
    else:
        tk_opts = [K]
        tm_opts = [t for t in (1024, 512, 256, 128) if M % t == 0] or [M]
    tn_opts = [t for t in (1408, 1024, 512, 384, 256, 128) if N % t == 0] or [N]

    def need(tm, tn, tk):
        blocks = 2 * (tm * tk * a_bytes + tk * tn * b_bytes + tm * tn * o_bytes)
        return blocks + (tm * tn * 4 if mode == "tn" else 0)

    tk_floor = next((t for t in tk_opts if t <= 512), tk_opts[-1])
    for tm in tm_opts:
        for tn in tn_opts:
            if need(tm, tn, tk_floor) <= MM_VMEM_BUDGET:
                return tm, tn, next(tk for tk in tk_opts if need(tm, tn, tk) <= MM_VMEM_BUDGET)
    return tm_opts[-1], tn_opts[-1], tk_opts[-1]


def mm(a, b, mode, name, out_dtype=F32, bias=None):
    if mode == "nn":
        (M, K), (K2, N) = a.shape, b.shape
        dims = (((1,), (0,)), ((), ()))
    elif mode == "nt":
        (M, K), (N, K2) = a.shape, b.shape
        dims = (((1,), (1,)), ((), ()))
    else:
        (K, M), (K2, N) = a.shape, b.shape
        dims = (((0,), (0,)), ((), ()))
    assert K == K2, (a.shape, b.shape, mode)
    tm, tn, tk = _mm_tiles(M, N, K, mode, a.dtype.itemsize, b.dtype.itemsize, jnp.dtype(out_dtype).itemsize)
    nk = K // tk
    if mode == "tn":
        a_spec = pl.BlockSpec((tk, tm), lambda i, j, k: (k, i))
        b_spec = pl.BlockSpec((tk, tn), lambda i, j, k: (k, j))
    elif mode == "nn":
        a_spec = pl.BlockSpec((tm, tk), lambda i, j, k: (i, k))
        b_spec = pl.BlockSpec((tk, tn), lambda i, j, k: (k, j))
    else:
        a_spec = pl.BlockSpec((tm, tk), lambda i, j, k: (i, k))
        b_spec = pl.BlockSpec((tn, tk), lambda i, j, k: (j, k))
    in_specs = [a_spec, b_spec]
    operands = [a, b]
    if bias is not None:
        in_specs.append(pl.BlockSpec((1, tn), lambda i, j, k: (0, j)))
        operands.append(bias)
    has_bias = bias is not None

    def body(*refs):
        a_ref, b_ref = refs[0], refs[1]
        bias_ref = refs[2] if has_bias else None
        o_ref = refs[3] if has_bias else refs[2]
        prod = lax.dot_general(a_ref[...].astype(BF16), b_ref[...].astype(BF16), dims,
                               preferred_element_type=F32)
        if nk == 1:
            if has_bias:
                prod = prod + bias_ref[...]
            o_ref[...] = prod.astype(o_ref.dtype)
        else:
            acc_ref = refs[-1]
            k = pl.program_id(2)

            @pl.when(k == 0)
            def _():
                acc_ref[...] = jnp.zeros_like(acc_ref)

            acc_ref[...] += prod

            @pl.when(k == nk - 1)
            def _():
                out = acc_ref[...]
                if has_bias:
                    out = out + bias_ref[...]
                o_ref[...] = out.astype(o_ref.dtype)

    return pl.pallas_call(
        body, name=name,
        grid=(M // tm, N // tn, nk),
        in_specs=in_specs,
        out_specs=pl.BlockSpec((tm, tn), lambda i, j, k: (i, j)),
        out_shape=jax.ShapeDtypeStruct((M, N), out_dtype),
        scratch_shapes=[pltpu.VMEM((tm, tn), F32)] if nk > 1 else [],
        compiler_params=_params(("parallel", "parallel", "arbitrary")),
    )(*operands)


def mm_fused(a, b, mode, name, tn, epi, epi_outs, pro=None, pro_rows=(), pro_vecs=(), pro_out=False, n_pro_sums=0,
             epi_rows=(), epi_vecs=(), b_blocks=None, n_cols=None):
    M, K = a.shape
    if b_blocks is not None:
        n_b, N = len(b_blocks), n_cols
    else:
        n_b = b.shape[0] if b.ndim == 3 else 1
        N = b.shape[-1] if mode == "nn" else b.shape[0]
    dims = (((1,), (0,)), ((), ())) if mode == "nn" else (((1,), (1,)), ((), ()))
    nj = N // tn
    epi_outs = [o if len(o) == 3 else (*o, None) for o in epi_outs]
    row_bytes = 2 * (K * a.dtype.itemsize + sum(K * r.dtype.itemsize for r in pro_rows) + (2 * K if pro_out else 0)
                     + sum(w * r.dtype.itemsize * (r.shape[0] if r.ndim == 3 else 1) for r, w in epi_rows)
                     + sum(w * jnp.dtype(dt).itemsize * (L or 1) for w, dt, L in epi_outs)
                     ) + (2 * K if pro is not None else 0)
    fixed = 2 * n_b * K * tn * b.dtype.itemsize
    tm = next((t for t in (1024, 512, 256, 128) if M % t == 0 and t * row_bytes + fixed <= MM_VMEM_BUDGET), M)
    row = lambda i, j: (i, 0)
    tile = lambda i, j: (i, j)
    stack = lambda i, j: (0, i, j)
    in_specs = [pl.BlockSpec((tm, K), row)] + [pl.BlockSpec((tm, K), row) for _ in pro_rows]
    in_specs += [pl.BlockSpec(v.shape, lambda i, j: (0, 0)) for v in pro_vecs]
    if b_blocks is not None:
        in_specs += [pl.BlockSpec(shape, imap) for shape, imap in b_blocks]
    elif b.ndim == 3:
        in_specs += [pl.BlockSpec((None, K, tn), lambda i, j, h=h: (h, 0, j)) for h in range(n_b)]
    elif mode == "nn":
        in_specs += [pl.BlockSpec((K, tn), lambda i, j: (0, j))]
    else:
        in_specs += [pl.BlockSpec((tn, K), lambda i, j: (j, 0))]
    in_specs += [pl.BlockSpec((r.shape[0], tm, w), stack) if r.ndim == 3 else pl.BlockSpec((tm, w), tile)
                 for r, w in epi_rows]
    in_specs += [pl.BlockSpec((1, tn), lambda i, j: (0, j)) for _ in epi_vecs]
    out_specs, out_shape = [], []
    if pro_out:
        out_specs.append(pl.BlockSpec((tm, K), row))
        out_shape.append(jax.ShapeDtypeStruct((M, K), BF16))
    for _ in range(n_pro_sums):
        out_specs.append(pl.BlockSpec((1, K), lambda i, j: (0, 0)))
        out_shape.append(jax.ShapeDtypeStruct((1, K), F32))
    for w, dt, L in epi_outs:
        out_specs.append(pl.BlockSpec((tm, w), tile) if L is None else pl.BlockSpec((L, tm, w), stack))
        out_shape.append(jax.ShapeDtypeStruct((M, nj * w) if L is None else (L, M, nj * w), dt))
    n_pr, n_pv, n_er, n_ev = len(pro_rows), len(pro_vecs), len(epi_rows), len(epi_vecs)
    n_a = 1 + n_pr + n_pv
    n_in = n_a + n_b + n_er + n_ev
    n_po = 1 if pro_out else 0

    def body(*refs):
        i, j = pl.program_id(0), pl.program_id(1)
        a_ref = refs[0]
        outs = refs[n_in:]
        if pro is not None:
            lhs_ref = refs[-1]

            @pl.when(j == 0)
            def _():
                res = pro(*[r[...] for r in refs[:1 + n_pr + n_pv]])
                if not isinstance(res, (tuple, list)):
                    res = (res,)
                lhs_ref[...] = res[0]
                if pro_out:
                    outs[0][...] = res[0]
                for s_ref, val in zip(outs[n_po:n_po + n_pro_sums], res[1:]):
                    part = jnp.sum(val.astype(F32), axis=0, keepdims=True)

                    @pl.when(i == 0)
                    def _(s_ref=s_ref, part=part):
                        s_ref[...] = part

                    @pl.when(i != 0)
                    def _(s_ref=s_ref, part=part):
                        s_ref[...] += part

            lhs = lhs_ref[...]
        else:
            lhs = a_ref[...].astype(BF16)
        accs = [lax.dot_general(lhs, b_ref[...].astype(BF16), dims, preferred_element_type=F32)
                for b_ref in refs[n_a:n_a + n_b]]
        res = epi(*accs, *[r[...] for r in refs[n_a + n_b:n_in]])
        if not isinstance(res, (tuple, list)):
            res = (res,)
        for o_ref, val in zip(outs[n_po + n_pro_sums:], res):
            if isinstance(val, (tuple, list)):
                for h, part in enumerate(val):
                    o_ref[h] = part.astype(o_ref.dtype)
            else:
                o_ref[...] = val.astype(o_ref.dtype)

    return pl.pallas_call(
        body, name=name,
        grid=(M // tm, nj),
        in_specs=in_specs, out_specs=out_specs, out_shape=out_shape,
        scratch_shapes=[pltpu.VMEM((tm, K), BF16)] if pro is not None else [],
        compiler_params=_params(("arbitrary", "arbitrary")),
    )(a, *pro_rows, *pro_vecs, *([b] * n_b), *[r for r, _ in epi_rows], *epi_vecs)


def rowwise(fn, rows, vecs, outs, sums, name, tm=None):
    norm = [(r, r.shape[1], 0) if not isinstance(r, tuple) else r for r in rows]
    S = norm[0][0].shape[0]
    if tm is None:
        per_row = sum(w * r.dtype.itemsize for r, w, _ in norm) + sum(n * jnp.dtype(dt).itemsize for n, dt in outs)
        tm = S
        for t in (512, 256, 128, 64, 32, 16, 8):
            if S % t == 0:
                tm = t
                if t * per_row <= ROW_TILE_BUDGET:
                    break
    n_rows, n_vecs, n_outs, n_sums = len(norm), len(vecs), len(outs), len(sums)
    in_specs = [pl.BlockSpec((tm, w), lambda i, cb=cb: (i, cb)) for _, w, cb in norm]
    in_specs += [pl.BlockSpec(v.shape, lambda i: (0, 0)) for v in vecs]
    out_specs = [pl.BlockSpec((tm, n), lambda i: (i, 0)) for n, _ in outs]
    out_specs += [pl.BlockSpec((1, n), lambda i: (0, 0)) for n in sums]
    out_shape = [jax.ShapeDtypeStruct((S, n), dt) for n, dt in outs]
    out_shape += [jax.ShapeDtypeStruct((1, n), F32) for n in sums]

    def body(*refs):
        ins = [r[...] for r in refs[:n_rows + n_vecs]]
        res = fn(*ins)
        if not isinstance(res, (tuple, list)):
            res = (res,)
        out_refs = refs[n_rows + n_vecs:]
        for o_ref, val in zip(out_refs[:n_outs], res[:n_outs]):
            o_ref[...] = val.astype(o_ref.dtype)
        if n_sums:
            i = pl.program_id(0)
            for s_ref, val in zip(out_refs[n_outs:], res[n_outs:]):
                part = jnp.sum(val.astype(F32), axis=0, keepdims=True)

                @pl.when(i == 0)
                def _(s_ref=s_ref, part=part):
                    s_ref[...] = part

                @pl.when(i != 0)
                def _(s_ref=s_ref, part=part):
                    s_ref[...] += part

    res = pl.pallas_call(
        body, name=name,
        grid=(S // tm,),
        in_specs=in_specs, out_specs=out_specs, out_shape=out_shape,
        compiler_params=_params(("arbitrary",) if n_sums else ("parallel",)),
    )(*[r for r, _, _ in norm], *vecs)
    return res


def _sigmoid(x):
    return jax.nn.sigmoid(x)


def _rms(x):
    r = lax.rsqrt(jnp.mean(x * x, axis=-1, keepdims=True) + EPS)
    return x * r, r


def _rms_bwd(xhat, r, dxhat):
    return r * (dxhat - xhat * jnp.mean(dxhat * xhat, axis=-1, keepdims=True))


def norm_mod(h, g, sh, sc, name):
    def f(h, g, sh, sc):
        xhat, _ = _rms(h)
        return ((xhat * g) * (1 + sc) + sh).astype(BF16)
    return rowwise(f, [h], [g, sh, sc], [(h.shape[1], BF16)], [], name)[0]


def norm_mod_bwd(h, dhn, dh_out, g, sc, name):
    D = h.shape[1]
    with_res = dh_out is not None

    def f(*a):
        if with_res:
            h, dhn, dres, g, sc = a
        else:
            h, dhn, g, sc = a
        xhat, r = _rms(h)
        xn = xhat * g
        dxn = dhn * (1 + sc)
        dh = _rms_bwd(xhat, r, dxn * g)
        if with_res:
            dh = dh + dres
        return dh, dhn, dhn * xn, dxn * xhat

    rows = [h, dhn] + ([dh_out] if with_res else [])
    return rowwise(f, rows, [g, sc], [(D, F32)], [D, D, D], name)


def residual(h, y, gate, coef, name, bias=None):
    D = h.shape[1]
    if bias is None:
        def f(h, y, gate):
            return h + (coef * gate) * y
        return rowwise(f, [h, y], [gate], [(D, F32)], [], name)[0], y

    def fb(h, y, gate, bias):
        yb = y + bias
        return h + (coef * gate) * yb, yb
    return rowwise(fb, [h, y], [gate, bias], [(D, F32), (D, F32)], [], name)


def residual_bwd(dh_out, y, gate, coef, name, with_bias_sum=False):
    D = y.shape[1]

    def f(dh, y, gate):
        dy = (coef * gate) * dh
        res = (dy.astype(BF16), coef * dh * y)
        return res + ((dy,) if with_bias_sum else ())
    return rowwise(f, [dh_out, y], [gate], [(D, BF16)], [D, D] if with_bias_sum else [D], name)


def ffn_w13_dx(dab, gw13, l, i, h, dh_out, g, sc):
    _, S, F = dab.shape
    D, C = gw13.shape[3:]
    tm = _tile(S, (512, 256, 128))
    nt = (((1,), (1,)), ((), ()))

    def body(a_ref, b_ref, h_ref, dh_ref, g_ref, sc_ref, o_ref, dsh_ref, dsc_ref, dg_ref, acc_ref):
        r, k = pl.program_id(0), pl.program_id(1)
        prod = lax.dot_general(a_ref[...], b_ref[...], nt, preferred_element_type=F32)

        @pl.when(k == 0)
        def _():
            acc_ref[...] = prod

        @pl.when((k > 0) & (k < 3))
        def _():
            acc_ref[...] += prod

        @pl.when(k == 3)
        def _():
            dhn = acc_ref[...] + prod
            xhat, rinv = _rms(h_ref[...])
            gain = g_ref[...]
            dxn = dhn * (1 + sc_ref[...])
            o_ref[...] = dh_ref[...] + _rms_bwd(xhat, rinv, dxn * gain)
            parts = [(dsh_ref, dhn), (dsc_ref, dhn * (xhat * gain)), (dg_ref, dxn * xhat)]
            for s_ref, val in parts:
                part = jnp.sum(val, axis=0, keepdims=True)

                @pl.when(r == 0)
                def _(s_ref=s_ref, part=part):
                    s_ref[...] = part

                @pl.when(r != 0)
                def _(s_ref=s_ref, part=part):
                    s_ref[...] += part

    rows = pl.BlockSpec((tm, D), lambda r, k: (r, 0))
    vec = pl.BlockSpec((1, D), lambda r, k: (0, 0))
    return pl.pallas_call(
        body, name="ffn_w13_dx",
        grid=(S // tm, 4),
        in_specs=[pl.BlockSpec((None, tm, C), lambda r, k: (k // 2, r, k % 2)),
                  pl.BlockSpec((None, None, None, D, C), lambda r, k: (k, l, i, 0, 0)), rows, rows, vec, vec],
        out_specs=[rows, vec, vec, vec],
        out_shape=[jax.ShapeDtypeStruct((S, D), F32)] + [jax.ShapeDtypeStruct((1, D), F32)] * 3,
        scratch_shapes=[pltpu.VMEM((tm, D), F32)],
        compiler_params=_params(("arbitrary", "arbitrary")),
    )(dab, gw13, h, dh_out, g, sc)


def ffn_w13_grad(hn, dab):
    S, D = hn.shape
    F = dab.shape[2]
    C = F // 2
    tk = next(t for t in (2048, 1024, 512, 256, 128) if S % t == 0)
    tn_dims = (((0,), (0,)), ((), ()))
    nk = S // tk

    def body(a_ref, b_ref, o_ref, acc_ref):
        k = pl.program_id(1)

        @pl.when(k == 0)
        def _():
            acc_ref[...] = jnp.zeros_like(acc_ref)

        acc_ref[...] += lax.dot_general(a_ref[...], b_ref[...], tn_dims, preferred_element_type=F32)

        @pl.when(k == nk - 1)
        def _():
            o_ref[...] = acc_ref[...]

    return pl.pallas_call(
        body, name="ffn_w13_dw",
        grid=(4, nk),
        in_specs=[pl.BlockSpec((tk, D), lambda j, k: (k, 0)),
                  pl.BlockSpec((None, tk, C), lambda j, k: (j // 2, k, j % 2))],
        out_specs=pl.BlockSpec((None, D, C), lambda j, k: (j, 0, 0)),
        out_shape=jax.ShapeDtypeStruct((4, D, C), F32),
        scratch_shapes=[pltpu.VMEM((D, C), F32)],
        compiler_params=_params(("parallel", "arbitrary")),
    )(hn, dab)


def _ffn_chunk(F):
    return _tile(F, (256, 128))


def ffn_fwd(h, g, sh, sc, gate, gw13, l, i, w2):
    F, D = w2.shape
    C = F // 2

    def norm(h, g, sh, sc):
        xhat, _ = _rms(h)
        return ((xhat * g) * (1 + sc) + sh).astype(BF16)

    def act(a, b):
        return (a, b), (a * _sigmoid(a)) * b
    blocks = [((None, None, None, D, C), lambda r, j, half=half: (2 * half + j, l, i, 0, 0)) for half in range(2)]
    hn, ab, t = mm_fused(h, gw13, "nn", "ffn_w13", C, act, [(C, F32, 2), (C, BF16)],
                         pro=norm, pro_vecs=[g, sh, sc], pro_out=True, b_blocks=blocks, n_cols=F)

    def res(acc, h, gate):
        return h + (0.5 * gate) * acc, acc
    h_out, y = mm_fused(t, w2, "nn", "ffn_w2", D, res, [(D, F32), (D, F32)], epi_rows=[(h, D)], epi_vecs=[gate])
    return h_out, (h, hn, ab, y)


def ffn_bwd(dh_out, saved, g, sc, gate, gw13, l, i, w2):
    h, hn, ab, y = saved
    F, D = w2.shape
    cf = _ffn_chunk(F)

    def scale(dh, y, gate):
        return ((0.5 * gate) * dh).astype(BF16), 0.5 * dh * y

    def act_bwd(dt, ab):
        a, b = ab[0], ab[1]
        sig = _sigmoid(a)
        sa = a * sig
        da = dt * b * (sig * (1 + a * (1 - sig)))
        db = dt * sa
        return sa * b, (da, db)
    dy, d_gate, t, dab = mm_fused(dh_out, w2, "nt", "ffn_w2_dx", cf, act_bwd, [(cf, BF16), (cf, BF16, 2)],
                                  pro=scale, pro_rows=[y], pro_vecs=[gate], pro_out=True, n_pro_sums=1,
                                  epi_rows=[(ab, cf)])
    dw2 = mm(t, dy, "tn", "ffn_w2_dw")
    dw13 = ffn_w13_grad(hn, dab)
    dh_in, d_sh, d_sc, d_g = ffn_w13_dx(dab, gw13, l, i, h, dh_out, g, sc)
    return dh_in, (d_sh, d_sc, d_gate, d_g), dw13, dw2


def _shifted(xbuf, n):
    return [xbuf] + [pltpu.roll(xbuf, n - b, 0) for b in range(1, 8)]


def conv_fwd(u, w_dw, b_dw, ln_g, ln_b):
    S, D = u.shape
    tm = _tile(S, (256, 128))
    rc = 32
    first_tap = CONV_HALO - (CONV_WIDTH - 1)
    w = jnp.concatenate([w_dw, jnp.zeros((CONV_HALO - CONV_WIDTH, D), F32)], axis=0)

    def body(cur_ref, prev_ref, w_ref, b_ref, g_ref, beta_ref, z_ref, s_ref):
        i = pl.program_id(0)
        prev = jnp.where(i == 0, jnp.zeros((CONV_HALO, D), F32), prev_ref[...])
        xs = _shifted(jnp.concatenate([prev, cur_ref[...]], axis=0), tm + CONV_HALO)
        for c0 in range(0, tm, rc):
            acc = jnp.zeros((rc, D), F32)
            for k in range(CONV_WIDTH):
                off = first_tap + k
                a8, b = off // 8 * 8, off % 8
                acc = acc + w_ref[k:k + 1, :] * xs[b][c0 + a8:c0 + a8 + rc, :]
            z_ref[c0:c0 + rc, :] = acc + b_ref[...]
        z = z_ref[...]
        mu = jnp.mean(z, axis=-1, keepdims=True)
        zc = z - mu
        r = lax.rsqrt(jnp.mean(zc * zc, axis=-1, keepdims=True) + EPS)
        un = zc * r * g_ref[...] + beta_ref[...]
        s_ref[...] = (un * _sigmoid(un)).astype(BF16)

    nb = tm // CONV_HALO
    vec = pl.BlockSpec((1, D), lambda i: (0, 0))
    return pl.pallas_call(
        body, name="conv_fwd",
        grid=(S // tm,),
        in_specs=[pl.BlockSpec((tm, D), lambda i: (i, 0)),
                  pl.BlockSpec((CONV_HALO, D), lambda i: (jnp.maximum(i * nb - 1, 0), 0)),
                  pl.BlockSpec((CONV_HALO, D), lambda i: (0, 0)), vec, vec, vec],
        out_specs=[pl.BlockSpec((tm, D), lambda i: (i, 0)), pl.BlockSpec((tm, D), lambda i: (i, 0))],
        out_shape=[jax.ShapeDtypeStruct((S, D), F32), jax.ShapeDtypeStruct((S, D), BF16)],
        compiler_params=_params(("parallel",)),
    )(u, u, w, b_dw, ln_g, ln_b)


def conv_bwd(dz, u, w_dw):
    S, D = u.shape
    tm = _tile(S, (256, 128))
    rc = 32
    first_tap = CONV_HALO - (CONV_WIDTH - 1)
    w = jnp.concatenate([w_dw, jnp.zeros((CONV_HALO - CONV_WIDTH, D), F32)], axis=0)
    n_tiles = S // tm
    nb = tm // CONV_HALO

    def body(dz_ref, dzn_ref, u_ref, up_ref, w_ref, du_ref, dw_ref):
        i = pl.program_id(0)
        nxt = jnp.where(i == n_tiles - 1, jnp.zeros((CONV_HALO, D), F32), dzn_ref[...])
        dzs = _shifted(jnp.concatenate([dz_ref[...], nxt], axis=0), tm + CONV_HALO)
        for c0 in range(0, tm, rc):
            acc = jnp.zeros((rc, D), F32)
            for m in range(CONV_WIDTH):
                a8, b = m // 8 * 8, m % 8
                acc = acc + w_ref[CONV_WIDTH - 1 - m:CONV_WIDTH - m, :] * dzs[b][c0 + a8:c0 + a8 + rc, :]
            du_ref[c0:c0 + rc, :] = acc
        prev = jnp.where(i == 0, jnp.zeros((CONV_HALO, D), F32), up_ref[...])
        us = _shifted(jnp.concatenate([prev, u_ref[...]], axis=0), tm + CONV_HALO)
        dz = dz_ref[...]

        @pl.when(i == 0)
        def _():
            dw_ref[...] = jnp.zeros_like(dw_ref)

        for k in range(CONV_WIDTH):
            off = first_tap + k
            a8, b = off // 8 * 8, off % 8
            dw_ref[k:k + 1, :] += jnp.sum(dz * us[b][a8:a8 + tm, :], axis=0, keepdims=True)

    last_blk = S // CONV_HALO - 1
    du, dw = pl.pallas_call(
        body, name="conv_bwd",
        grid=(n_tiles,),
        in_specs=[pl.BlockSpec((tm, D), lambda i: (i, 0)),
                  pl.BlockSpec((CONV_HALO, D), lambda i: (jnp.minimum((i + 1) * nb, last_blk), 0)),
                  pl.BlockSpec((tm, D), lambda i: (i, 0)),
                  pl.BlockSpec((CONV_HALO, D), lambda i: (jnp.maximum(i * nb - 1, 0), 0)),
                  pl.BlockSpec((CONV_HALO, D), lambda i: (0, 0))],
        out_specs=[pl.BlockSpec((tm, D), lambda i: (i, 0)), pl.BlockSpec((CONV_HALO, D), lambda i: (0, 0))],
        out_shape=[jax.ShapeDtypeStruct((S, D), F32), jax.ShapeDtypeStruct((CONV_HALO, D), F32)],
        compiler_params=_params(("arbitrary",)),
    )(dz, dz, u, u, w)
    return du, dw[:CONV_WIDTH]


def conv_module_fwd(h, g, sh, sc, gate, p):
    D = h.shape[1]
    hn = norm_mod(h, g, sh, sc, "conv_norm_mod")
    pre = mm(hn, p["w_pw1"], "nn", "conv_pw1")
    ba, bg = p["b_pw1"][:, :D], p["b_pw1"][:, D:]

    def glu(a, gt, ba, bg):
        return (a + ba) * _sigmoid(gt + bg)
    u = rowwise(glu, [(pre, D, 0), (pre, D, 1)], [ba, bg], [(D, F32)], [], "conv_glu")[0]
    z, s = conv_fwd(u, p["w_dw"], p["b_dw"], p["ln_g"], p["ln_b"])
    yraw = mm(s, p["w_pw2"], "nn", "conv_pw2")
    h_out, y = residual(h, yraw, gate, 1.0, "conv_residual", bias=p["b_pw2"])
    return h_out, (h, hn, pre, u, z, s, y)


def conv_module_bwd(dh_out, saved, g, sc, gate, p):
    h, hn, pre, u, z, s, y = saved
    D = h.shape[1]
    dy, d_gate, d_b_pw2 = residual_bwd(dh_out, y, gate, 1.0, "conv_residual_bwd", with_bias_sum=True)
    d_w_pw2 = mm(s, dy, "tn", "conv_pw2_dw")
    ds = mm(dy, p["w_pw2"], "nt", "conv_pw2_dx")

    def ln_bwd(z, ds, g, beta):
        mu = jnp.mean(z, axis=-1, keepdims=True)
        zc = z - mu
        r = lax.rsqrt(jnp.mean(zc * zc, axis=-1, keepdims=True) + EPS)
        xhat = zc * r
        un = xhat * g + beta
        sig = _sigmoid(un)
        d_un = ds * (sig * (1 + un * (1 - sig)))
        dxhat = d_un * g
        dz = r * (dxhat - jnp.mean(dxhat, axis=-1, keepdims=True)
                  - xhat * jnp.mean(dxhat * xhat, axis=-1, keepdims=True))
        return dz, d_un * xhat, d_un, dz
    dz, d_ln_g, d_ln_b, d_b_dw = rowwise(ln_bwd, [z, ds], [p["ln_g"], p["ln_b"]], [(D, F32)], [D, D, D],
                                         "conv_ln_bwd")
    du, d_w_dw = conv_bwd(dz, u, p["w_dw"])
    ba, bg = p["b_pw1"][:, :D], p["b_pw1"][:, D:]

    def glu_bwd(a, gt, du, ba, bg):
        sg = _sigmoid(gt + bg)
        da = du * sg
        dg = du * (a + ba) * (sg * (1 - sg))
        dpre = jnp.concatenate([da, dg], axis=1)
        return dpre.astype(BF16), dpre
    dpre, d_b_pw1 = rowwise(glu_bwd, [(pre, D, 0), (pre, D, 1), du], [ba, bg], [(2 * D, BF16)], [2 * D],
                            "conv_glu_bwd")
    d_w_pw1 = mm(hn, dpre, "tn", "conv_pw1_dw")
    dhn = mm(dpre, p["w_pw1"], "nt", "conv_pw1_dx")
    dh_in, d_sh, d_sc, d_g = norm_mod_bwd(h, dhn, dh_out, g, sc, "norm_mod_bwd")
    grads = dict(w_pw1=d_w_pw1, b_pw1=d_b_pw1, w_dw=d_w_dw, b_dw=d_b_dw, ln_g=d_ln_g, ln_b=d_ln_b,
                 w_pw2=d_w_pw2, b_pw2=d_b_pw2)
    return dh_in, (d_sh, d_sc, d_gate, d_g), grads


def _rope(x, c, s1, s2):
    n = x.shape[1]
    return x * c + pltpu.roll(x, n - QK_ROPE // 2, 1) * s1 + pltpu.roll(x, QK_ROPE // 2, 1) * s2


def _rope_t(dy, c, s1, s2):
    n = dy.shape[1]
    return dy * c + pltpu.roll(dy * s1, QK_ROPE // 2, 1) + pltpu.roll(dy * s2, n - QK_ROPE // 2, 1)


def rope_tables(positions):
    inv_freq = ROPE_THETA ** (-jnp.arange(0, QK_ROPE, 2, dtype=F32) / QK_ROPE)
    ang = positions.astype(F32)[:, None] * inv_freq
    cos, sin = jnp.cos(ang), jnp.sin(ang)
    S = positions.shape[0]
    one = jnp.ones((S, QK_NOPE), F32)
    z16 = jnp.zeros((S, QK_ROPE // 2), F32)
    zn = jnp.zeros((S, QK_NOPE), F32)
    zt = jnp.zeros((S, HEAD_PAD - QK_NOPE - QK_ROPE), F32)
    c = jnp.concatenate([one, cos, cos, zt], axis=1)
    s1 = jnp.concatenate([zn, -sin, z16, zt], axis=1)
    s2 = jnp.concatenate([zn, z16, sin, zt], axis=1)
    return c, s1, s2


def attn_fwd(qr, kv, kpe, n_heads):
    S = qr.shape[0]
    H = n_heads
    tq = _tile(S, (ATTN_TILE,))
    nq = S // tq
    c2 = (QK_NOPE + QK_ROPE) ** -0.5 * LOG2_E
    nt = (((1,), (1,)), ((), ()))

    def body(q_ref, k_ref, v_ref, kpe_ref, o_ref, lse_ref, kf_ref, vt_ref, m_ref, l_ref, acc_ref):
        qi = pl.program_id(1)

        @pl.when(qi == 0)
        def _():
            kf_ref[...] = k_ref[...] + kpe_ref[...]
            for c in range(nq):
                vt_ref[c] = jnp.transpose(v_ref[c * tq:(c + 1) * tq, :].astype(F32)).astype(BF16)

        q = q_ref[...]
        m_ref[...] = jnp.full((1, tq), -jnp.inf, F32)
        l_ref[...] = jnp.zeros((1, tq), F32)
        acc_ref[...] = jnp.zeros((HEAD_PAD, tq), F32)

        def tile(j, masked):
            k = kf_ref[pl.ds(pl.multiple_of(j * tq, tq), tq), :]
            t = lax.dot_general(k, q, nt, preferred_element_type=F32) * c2
            if masked:
                krow = lax.broadcasted_iota(jnp.int32, (tq, tq), 0)
                qcol = lax.broadcasted_iota(jnp.int32, (tq, tq), 1)
                t = jnp.where(krow <= qcol, t, NEG)
            m_old = m_ref[...]
            m_new = jnp.maximum(m_old, jnp.max(t, axis=0, keepdims=True))
            alpha = jnp.exp2(m_old - m_new)
            p = jnp.exp2(t - m_new)
            l_ref[...] = alpha * l_ref[...] + jnp.sum(p, axis=0, keepdims=True)
            acc_ref[...] = alpha * acc_ref[...] + jnp.dot(vt_ref[j], p.astype(BF16), preferred_element_type=F32)
            m_ref[...] = m_new

        def unmasked(j, carry):
            tile(j, False)
            return carry

        lax.fori_loop(0, qi, unmasked, 0)
        tile(qi, True)
        l = l_ref[...]
        o_ref[...] = jnp.transpose(acc_ref[...] / l)
        lse_ref[...] = m_ref[...] + jnp.log(l) * LOG2_E

    return pl.pallas_call(
        body, name="attn_fwd",
        grid=(H, nq),
        in_specs=[pl.BlockSpec((tq, HEAD_PAD), lambda h, i: (i, h)),
                  pl.BlockSpec((S, HEAD_PAD), lambda h, i: (0, h)),
                  pl.BlockSpec((S, HEAD_PAD), lambda h, i: (0, H + h)),
                  pl.BlockSpec((S, HEAD_PAD), lambda h, i: (0, 0))],
        out_specs=[pl.BlockSpec((tq, HEAD_PAD), lambda h, i: (i, h)),
                   pl.BlockSpec((None, None, 1, tq), lambda h, i: (h, i, 0, 0))],
        out_shape=[jax.ShapeDtypeStruct((S, H * HEAD_PAD), F32), jax.ShapeDtypeStruct((H, nq, 1, tq), F32)],
        scratch_shapes=[pltpu.VMEM((S, HEAD_PAD), BF16), pltpu.VMEM((nq, HEAD_PAD, tq), BF16),
                        pltpu.VMEM((1, tq), F32), pltpu.VMEM((1, tq), F32), pltpu.VMEM((HEAD_PAD, tq), F32)],
        compiler_params=_params(("parallel", "arbitrary")),
    )(qr, kv, kv, kpe)


def attn_delta(o, do, n_heads):
    S = o.shape[0]
    H = n_heads
    tq = _tile(S, (ATTN_TILE,))
    nq = S // tq

    def body(o_ref, do_ref, d_ref):
        for c in range(nq):
            rows = slice(c * tq, (c + 1) * tq)
            prod = o_ref[rows, :] * do_ref[rows, :].astype(F32)
            d_ref[c] = jnp.sum(jnp.transpose(prod), axis=0, keepdims=True)

    return pl.pallas_call(
        body, name="attn_delta",
        grid=(H,),
        in_specs=[pl.BlockSpec((S, HEAD_PAD), lambda h: (0, h)), pl.BlockSpec((S, HEAD_PAD), lambda h: (0, h))],
        out_specs=pl.BlockSpec((None, nq, 1, tq), lambda h: (h, 0, 0, 0)),
        out_shape=jax.ShapeDtypeStruct((H, nq, 1, tq), F32),
        compiler_params=_params(("parallel",)),
    )(o, do)


def attn_bwd(qr, kv, kpe, do, lse2, delta, n_heads):
    S = qr.shape[0]
    H = n_heads
    tq = _tile(S, (ATTN_TILE,))
    nq = S // tq
    scale = (QK_NOPE + QK_ROPE) ** -0.5
    c2 = scale * LOG2_E
    nt = (((1,), (1,)), ((), ()))
    delta4 = delta

    def body(k_ref, v_ref, kpe_ref, q_ref, do_ref, lse_ref, dl_ref, dq_ref, dk_ref, dv_ref, dka_ref, dva_ref,
             dqt_ref):
        kj = pl.program_id(1)
        k = k_ref[...] + kpe_ref[...]
        kt = jnp.transpose(k.astype(F32)).astype(BF16)
        v = v_ref[...]

        @pl.when(kj == 0)
        def _():
            dqt_ref[...] = jnp.zeros_like(dqt_ref)

        dka_ref[...] = jnp.zeros_like(dka_ref)
        dva_ref[...] = jnp.zeros_like(dva_ref)

        def tile(i, masked):
            start = pl.multiple_of(i * tq, tq)
            q = q_ref[pl.ds(start, tq), :]
            do = do_ref[pl.ds(start, tq), :]
            t = lax.dot_general(k, q, nt, preferred_element_type=F32) * c2
            if masked:
                krow = lax.broadcasted_iota(jnp.int32, (tq, tq), 0)
                qcol = lax.broadcasted_iota(jnp.int32, (tq, tq), 1)
                t = jnp.where(krow <= qcol, t, NEG)
            pt = jnp.exp2(t - lse_ref[i])
            dva_ref[...] += jnp.dot(pt.astype(BF16), do, preferred_element_type=F32)
            dpt = lax.dot_general(v, do, nt, preferred_element_type=F32)
            dst = (pt * (dpt - dl_ref[i]) * scale).astype(BF16)
            dka_ref[...] += jnp.dot(dst, q, preferred_element_type=F32)
            dqt_ref[i] += jnp.dot(kt, dst, preferred_element_type=F32)

        tile(kj, True)

        def unmasked(i, carry):
            tile(i, False)
            return carry

        lax.fori_loop(kj + 1, nq, unmasked, 0)
        dk_ref[...] = dka_ref[...]
        dv_ref[...] = dva_ref[...]

        @pl.when(kj == nq - 1)
        def _():
            for c in range(nq):
                dq_ref[c * tq:(c + 1) * tq, :] = jnp.transpose(dqt_ref[c])

    blk = pl.BlockSpec((tq, HEAD_PAD), lambda h, j: (j, h))
    whole = pl.BlockSpec((S, HEAD_PAD), lambda h, j: (0, h))
    stat = pl.BlockSpec((None, nq, 1, tq), lambda h, j: (h, 0, 0, 0))
    shp = jax.ShapeDtypeStruct((S, H * HEAD_PAD), F32)
    return pl.pallas_call(
        body, name="attn_bwd",
        grid=(H, nq),
        in_specs=[blk, pl.BlockSpec((tq, HEAD_PAD), lambda h, j: (j, H + h)),
                  pl.BlockSpec((tq, HEAD_PAD), lambda h, j: (j, 0)), whole, whole, stat, stat],
        out_specs=[whole, blk, blk],
        out_shape=[shp, shp, shp],
        scratch_shapes=[pltpu.VMEM((tq, HEAD_PAD), F32), pltpu.VMEM((tq, HEAD_PAD), F32),
                        pltpu.VMEM((nq, HEAD_PAD, tq), F32)],
        compiler_params=_params(("parallel", "arbitrary")),
    )(kv, kv, kpe, qr, do, lse2, delta4)


def _pad_heads(w, width):
    R = w.shape[0]
    w3 = w.reshape(R, -1, width)
    return jnp.pad(w3, ((0, 0), (0, 0), (0, HEAD_PAD - width))).reshape(R, -1)


def _unpad_heads(w, width):
    R = w.shape[0]
    return w.reshape(R, -1, HEAD_PAD)[:, :, :width].reshape(R, -1)


def mla_pad_weights(p):
    H = N_HEADS
    w_q_b = _pad_heads(p["w_q_b"], QK_NOPE + QK_ROPE)
    kvb = p["w_kv_b"].reshape(KV_LORA, H, QK_NOPE + V_HEAD)
    wk = _pad_heads(kvb[:, :, :QK_NOPE].reshape(KV_LORA, -1), QK_NOPE)
    wv = _pad_heads(kvb[:, :, QK_NOPE:].reshape(KV_LORA, -1), V_HEAD)
    D = p["w_kv_a"].shape[0]
    a = p["w_kv_a"]
    w_kv_a = jnp.concatenate([a[:, :KV_LORA], jnp.zeros((D, QK_NOPE), a.dtype), a[:, KV_LORA:],
                              jnp.zeros((D, HEAD_PAD - QK_NOPE - QK_ROPE), a.dtype)], axis=1)
    wo = p["w_o"].reshape(H, V_HEAD, -1)
    w_o = jnp.pad(wo, ((0, 0), (0, HEAD_PAD - V_HEAD), (0, 0))).reshape(H * HEAD_PAD, -1)
    return dict(w_q_a=p["w_q_a"], w_q_b=w_q_b, w_kv_b=jnp.concatenate([wk, wv], axis=1), w_kv_a=w_kv_a, w_o=w_o)


def mla_kv_fwd(h, g, sh, sc, kv_a_norm_g, pw, tabs):
    hkv = norm_mod(h, g, sh, sc, "kv_norm_mod")
    ckvp = mm(hkv, pw["w_kv_a"], "nn", "kv_a")

    def f(ckv, kpe, c, s1, s2, g):
        xhat, _ = _rms(ckv)
        return (xhat * g).astype(BF16), _rope(kpe, c, s1, s2).astype(BF16)
    ckv_n, kpe_r = rowwise(f, [(ckvp, KV_LORA, 0), (ckvp, HEAD_PAD, KV_LORA // HEAD_PAD), *tabs], [kv_a_norm_g],
                           [(KV_LORA, BF16), (HEAD_PAD, BF16)], [], "kv_a_norm_rope")
    kv = mm(ckv_n, pw["w_kv_b"], "nn", "kv_b", out_dtype=BF16)
    return kv, kpe_r, (h, hkv, ckvp, ckv_n)


def mla_kv_bwd(dh_stream, dk, dv, saved, g, sc, kv_a_norm_g, pw, tabs):
    h, hkv, ckvp, ckv_n = saved
    H = N_HEADS
    lane = jnp.arange(HEAD_PAD)
    pe_mask = ((lane >= QK_NOPE) & (lane < QK_NOPE + QK_ROPE)).astype(F32)[None, :]

    def f(dk, dv, c, s1, s2, mask):
        tot = dk[:, :HEAD_PAD]
        for hh in range(1, H):
            tot = tot + dk[:, hh * HEAD_PAD:(hh + 1) * HEAD_PAD]
        dkpe = _rope_t(tot * mask, c, s1, s2) * mask
        return jnp.concatenate([dk, dv], axis=1).astype(BF16), dkpe
    dkv, dkpe = rowwise(f, [dk, dv, *tabs], [pe_mask], [(2 * H * HEAD_PAD, BF16), (HEAD_PAD, F32)], [],
                        "kv_split_bwd")
    d_w_kv_b = mm(ckv_n, dkv, "tn", "kv_b_dw")
    dckv_n = mm(dkv, pw["w_kv_b"], "nt", "kv_b_dx")

    def f2(ckv, dn, dkpe, g):
        xhat, r = _rms(ckv)
        dx = _rms_bwd(xhat, r, dn * g)
        return jnp.concatenate([dx, dkpe], axis=1).astype(BF16), dn * xhat
    dckvp, d_kv_a_g = rowwise(f2, [(ckvp, KV_LORA, 0), dckv_n, dkpe], [kv_a_norm_g],
                              [(KV_LORA + HEAD_PAD, BF16)], [KV_LORA], "kv_a_norm_bwd")
    d_w_kv_a = mm(hkv, dckvp, "tn", "kv_a_dw")
    dhkv = mm(dckvp, pw["w_kv_a"], "nt", "kv_a_dx")
    dh, d_sh, d_sc, d_g = norm_mod_bwd(h, dhkv, dh_stream, g, sc, "norm_mod_bwd")
    return dh, (d_sh, d_sc, d_g), d_kv_a_g, d_w_kv_a, d_w_kv_b


def mla_fwd(h, g, sh, sc, gate, q_a_norm_g, pw, kv, kpe_r, tabs):
    H = N_HEADS
    hn = norm_mod(h, g, sh, sc, "mla_norm_mod")
    qa = mm(hn, pw["w_q_a"], "nn", "q_a")

    def f(qa, g):
        xhat, _ = _rms(qa)
        return (xhat * g).astype(BF16)
    qa_n = rowwise(f, [qa], [q_a_norm_g], [(qa.shape[1], BF16)], [], "q_a_norm")[0]
    qp = mm(qa_n, pw["w_q_b"], "nn", "q_b")

    def frope(q, c, s1, s2):
        return jnp.concatenate([_rope(q[:, hh * HEAD_PAD:(hh + 1) * HEAD_PAD], c, s1, s2) for hh in range(H)],
                               axis=1).astype(BF16)
    qr = rowwise(frope, [qp, *tabs], [], [(H * HEAD_PAD, BF16)], [], "q_rope")[0]
    o, lse = attn_fwd(qr, kv, kpe_r, H)
    y = mm(o, pw["w_o"], "nn", "w_o")
    h_out, _ = residual(h, y, gate, 1.0, "mla_residual")
    return h_out, (h, hn, qa, qa_n, qr, o, lse, y)


def mla_bwd(dh_out, saved, g, sc, gate, q_a_norm_g, pw, kv, kpe_r, tabs):
    h, hn, qa, qa_n, qr, o, lse, y = saved
    H = N_HEADS
    dy, d_gate = residual_bwd(dh_out, y, gate, 1.0, "mla_residual_bwd")
    d_w_o = mm(o, dy, "tn", "w_o_dw")
    do = mm(dy, pw["w_o"], "nt", "w_o_dx", out_dtype=BF16)
    delta = attn_delta(o, do, H)
    dqr, dk, dv = attn_bwd(qr, kv, kpe_r, do, lse, delta, H)

    def frope_t(dq, c, s1, s2):
        return jnp.concatenate([_rope_t(dq[:, hh * HEAD_PAD:(hh + 1) * HEAD_PAD], c, s1, s2) for hh in range(H)],
                               axis=1).astype(BF16)
    dqp = rowwise(frope_t, [dqr, *tabs], [], [(H * HEAD_PAD, BF16)], [], "q_rope_bwd")[0]
    d_w_q_b = mm(qa_n, dqp, "tn", "q_b_dw")
    dqa_n = mm(dqp, pw["w_q_b"], "nt", "q_b_dx")

    def f(qa, dn, g):
        xhat, r = _rms(qa)
        return _rms_bwd(xhat, r, dn * g).astype(BF16), dn * xhat
    dqa, d_q_a_g = rowwise(f, [qa, dqa_n], [q_a_norm_g], [(qa.shape[1], BF16)], [qa.shape[1]], "q_a_norm_bwd")
    d_w_q_a = mm(hn, dqa, "tn", "q_a_dw")
    dhn = mm(dqa, pw["w_q_a"], "nt", "q_a_dx")
    dh_in, d_sh, d_sc, d_g = norm_mod_bwd(h, dhn, dh_out, g, sc, "norm_mod_bwd")
    grads = dict(w_q_a=d_w_q_a, q_a_norm_g=d_q_a_g, w_q_b=d_w_q_b, w_o=d_w_o)
    return dh_in, (d_sh, d_sc, d_gate, d_g), grads, dk, dv


def loss_head(h, target, g):
    D = h.shape[1]

    def f(h, t, g):
        xhat, r = _rms(h)
        err = xhat * g - t
        dy = err * (1.0 / D)
        dh = _rms_bwd(xhat, r, dy * g)
        return dh, (0.5 / D) * err * err, dy * xhat
    return rowwise(f, [h, target], [g], [(D, F32)], [D, D], "loss_head")


def _place():
    x, y, c = lax.axis_index("x"), lax.axis_index("y"), lax.axis_index("c")
    chips = [(1 - x, y), (x, 1 - y), (1 - x, 1 - y)]
    return x, y, c, chips


HBM_SPEC = pl.BlockSpec(memory_space=pltpu.HBM)


def all_gather8(v):
    m, n = v.shape

    def body(x_ref, out_ref, send_sems, recv_sems, local_sem):
        x, y, c, chips = _place()
        me, sibling = (x, y, c), (x, y, 1 - c)

        def rows(px, py, pc):
            return out_ref.at[4 * px + 2 * py + pc]

        def copy(k, block, to, src=None):
            return pltpu.make_async_remote_copy(
                src_ref=rows(*block) if src is None else src, dst_ref=rows(*block),
                send_sem=send_sems.at[k], recv_sem=recv_sems.at[k], device_id=to, device_id_type=MESH)

        mine = pltpu.make_async_copy(x_ref, rows(*me), local_sem)
        mine.start()
        first = [copy(0, me, sibling, src=x_ref)]
        first += [copy(1 + j, me, (*chip, c), src=x_ref) for j, chip in enumerate(chips)]
        for cp in first:
            cp.start()
        passed = [copy(4 + j, (*chip, c), sibling) for j, chip in enumerate(chips)]
        for j, chip in enumerate(chips):
            copy(1 + j, (*chip, c), me).wait_recv()
            passed[j].start()
        copy(0, sibling, me).wait_recv()
        for j, chip in enumerate(chips):
            copy(4 + j, (*chip, 1 - c), me).wait_recv()
        for cp in first + passed:
            cp.wait_send()
        mine.wait()

    return pl.pallas_call(
        body, name="all_gather8",
        out_shape=jax.ShapeDtypeStruct((8, m, n), v.dtype),
        in_specs=[pl.BlockSpec(memory_space=pltpu.VMEM)],
        out_specs=pl.BlockSpec(memory_space=pltpu.VMEM),
        scratch_shapes=[pltpu.SemaphoreType.DMA((7,)), pltpu.SemaphoreType.DMA((7,)), pltpu.SemaphoreType.DMA],
        compiler_params=pltpu.CompilerParams(vmem_limit_bytes=VMEM_LIMIT_BYTES),
    )(v)


def gather_weights(bufs):
    n = len(bufs)

    def body(*refs):
        ins, outs = refs[:n], refs[n:2 * n]
        send_sems, recv_sems = refs[2 * n:]
        x, y, c, chips = _place()
        sibling = (x, y, 1 - c)
        me = 2 * x + y

        def idx(chip):
            return 2 * chip[0] + chip[1]

        def copy(w, k, src, dst, to):
            return pltpu.make_async_remote_copy(src_ref=src, dst_ref=dst, send_sem=send_sems.at[6 * w + k],
                                                recv_sem=recv_sems.at[6 * w + k], device_id=to, device_id_type=MESH)

        first = [copy(w, j, ins[w].at[me, c], outs[w].at[me, c], (*chip, c))
                 for w in range(n) for j, chip in enumerate(chips)]
        for cp in first:
            cp.start()
        passed = []
        for w in range(n):
            for j, chip in enumerate(chips):
                landed = outs[w].at[idx(chip), c]
                copy(w, j, landed, landed, (*chip, c)).wait_recv()
                fwd = copy(w, 3 + j, landed, landed, sibling)
                fwd.start()
                passed.append(fwd)
        for w in range(n):
            for j, chip in enumerate(chips):
                other = outs[w].at[idx(chip), 1 - c]
                copy(w, 3 + j, other, other, sibling).wait_recv()
        for cp in first + passed:
            cp.wait_send()

    return pl.pallas_call(
        body, name="gather_weights",
        out_shape=[jax.ShapeDtypeStruct(b.shape, b.dtype) for b in bufs],
        in_specs=[HBM_SPEC] * n, out_specs=[HBM_SPEC] * n,
        input_output_aliases={w: w for w in range(n)},
        scratch_shapes=[pltpu.SemaphoreType.DMA((6 * n,)), pltpu.SemaphoreType.DMA((6 * n,))],
    )(*bufs)


def exchange_halves(gs):
    n = len(gs)

    def body(*refs):
        ins, theirs = refs[:n], refs[n:2 * n]
        send_sems, recv_sems = refs[2 * n:]
        x, y, c, _ = _place()
        sends = [pltpu.make_async_remote_copy(src_ref=ins[w].at[:, 1 - c], dst_ref=theirs[w],
                                              send_sem=send_sems.at[w], recv_sem=recv_sems.at[w],
                                              device_id=(x, y, 1 - c), device_id_type=MESH) for w in range(n)]
        for cp in sends:
            cp.start()
        for cp in sends:
            cp.wait()

    return pl.pallas_call(
        body, name="exchange_halves",
        out_shape=[jax.ShapeDtypeStruct((4,) + g.shape[2:], g.dtype) for g in gs],
        in_specs=[HBM_SPEC] * n, out_specs=[HBM_SPEC] * n,
        scratch_shapes=[pltpu.SemaphoreType.DMA((n,)), pltpu.SemaphoreType.DMA((n,))],
    )(*gs)


def scatter_blocks(ps):
    n = len(ps)

    def body(*refs):
        ins, outs = refs[:n], refs[n:2 * n]
        send_sems, recv_sems = refs[2 * n:]
        x, y, c, chips = _place()
        sends = [pltpu.make_async_remote_copy(src_ref=ins[w].at[2 * chip[0] + chip[1]], dst_ref=outs[w].at[j],
                                              send_sem=send_sems.at[3 * w + j], recv_sem=recv_sems.at[3 * w + j],
                                              device_id=(*chip, c), device_id_type=MESH)
                 for w in range(n) for j, chip in enumerate(chips)]
        for cp in sends:
            cp.start()
        for cp in sends:
            cp.wait()

    return pl.pallas_call(
        body, name="scatter_blocks",
        out_shape=[jax.ShapeDtypeStruct((3,) + p.shape[1:], p.dtype) for p in ps],
        in_specs=[HBM_SPEC] * n, out_specs=[HBM_SPEC] * n,
        scratch_shapes=[pltpu.SemaphoreType.DMA((3 * n,)), pltpu.SemaphoreType.DMA((3 * n,))],
    )(*ps)


def join_halves(qs):
    n = len(qs)

    def body(*refs):
        ins, outs = refs[:n], refs[n:2 * n]
        send_sems, recv_sems = refs[2 * n:]
        x, y, c, _ = _place()
        sends = [pltpu.make_async_remote_copy(src_ref=ins[w].at[c], dst_ref=outs[w].at[c], send_sem=send_sems.at[w],
                                              recv_sem=recv_sems.at[w], device_id=(x, y, 1 - c), device_id_type=MESH)
                 for w in range(n)]
        for cp in sends:
            cp.start()
        for w in range(n):
            other = outs[w].at[1 - c]
            pltpu.make_async_remote_copy(src_ref=other, dst_ref=other, send_sem=send_sems.at[w],
                                         recv_sem=recv_sems.at[w], device_id=(x, y, 1 - c),
                                         device_id_type=MESH).wait_recv()
        for cp in sends:
            cp.wait_send()

    return pl.pallas_call(
        body, name="join_halves",
        out_shape=[jax.ShapeDtypeStruct(q.shape, q.dtype) for q in qs],
        in_specs=[HBM_SPEC] * n, out_specs=[HBM_SPEC] * n,
        input_output_aliases={w: w for w in range(n)},
        scratch_shapes=[pltpu.SemaphoreType.DMA((n,)), pltpu.SemaphoreType.DMA((n,))],
    )(*qs)


def _row_tile(R, row_bytes):
    tm = R
    for t in (512, 256, 128, 64, 32, 16, 8):
        if R % t == 0:
            tm = t
            if t * row_bytes <= ROW_TILE_BUDGET:
                break
    return tm


def sum_siblings(g, theirs, place):
    _, _, R, C = g.shape
    tm = _row_tile(R, 4 * 3 * C * 4)

    def body(place_ref, a_ref, b_ref, o_ref):
        o_ref[...] = (a_ref[...] + b_ref[...]).astype(BF16)

    return pl.pallas_call(
        body, name="sum_siblings",
        grid_spec=pltpu.PrefetchScalarGridSpec(
            num_scalar_prefetch=1, grid=(4, R // tm),
            in_specs=[pl.BlockSpec((None, None, tm, C), lambda j, i, s: (j, s[1], i, 0)),
                      pl.BlockSpec((None, tm, C), lambda j, i, s: (j, i, 0))],
            out_specs=pl.BlockSpec((None, tm, C), lambda j, i, s: (j, i, 0))),
        out_shape=jax.ShapeDtypeStruct((4, R, C), BF16),
        compiler_params=_params(("parallel", "parallel")),
    )(place, g, theirs)


def sum_chips(p, landed, place):
    _, R, C = p.shape
    tm = _row_tile(R, 4 * 5 * C * 4)

    def body(place_ref, p_ref, l0_ref, l1_ref, l2_ref, o_ref):
        o_ref[...] = ((p_ref[...].astype(F32) + l0_ref[...].astype(F32)) + l1_ref[...].astype(F32)
                      ) + l2_ref[...].astype(F32)

    return pl.pallas_call(
        body, name="sum_chips",
        grid_spec=pltpu.PrefetchScalarGridSpec(
            num_scalar_prefetch=1, grid=(R // tm,),
            in_specs=[pl.BlockSpec((None, tm, C), lambda i, s: (s[0], i, 0))]
            + [pl.BlockSpec((None, tm, C), lambda i, s, j=j: (j, i, 0)) for j in range(3)],
            out_specs=pl.BlockSpec((None, tm, C), lambda i, s: (s[1], i, 0))),
        out_shape=jax.ShapeDtypeStruct((2, R, C), F32),
        compiler_params=_params(("parallel",)),
    )(place, p, landed, landed, landed)


def sum_blocks(items, name):
    R, C = items[0][0].shape[1:]
    tm = R
    for t in (512, 256, 128, 64, 32, 16, 8):
        if R % t == 0:
            tm = t
            if t * C * 4 * (len(items) + 1) <= ROW_TILE_BUDGET:
                break
    n = len(items)

    def body(*refs):
        acc = refs[0][...].astype(F32)
        for r in refs[1:n]:
            acc = acc + r[...].astype(F32)
        refs[n][...] = acc

    return pl.pallas_call(
        body, name=name,
        grid=(R // tm,),
        in_specs=[pl.BlockSpec((None, tm, C), lambda i, j=j: (j, i, 0)) for _, j in items],
        out_specs=pl.BlockSpec((tm, C), lambda i: (i, 0)),
        out_shape=jax.ShapeDtypeStruct((R, C), F32),
        compiler_params=_params(("parallel",)),
    )(*[a for a, _ in items])


def reduce_scatter_grads(gs, place):
    theirs = exchange_halves(gs)
    ps = [sum_siblings(g, t, place) for g, t in zip(gs, theirs)]
    landed = scatter_blocks(ps)
    qs = [sum_chips(p, l, place) for p, l in zip(ps, landed)]
    joined = join_halves(qs)
    return [j.reshape(2 * j.shape[1], j.shape[2]) for j in joined]


def adamw(w, g, m, v):
    shape = w.shape
    C = shape[-1]
    R = w.size // C
    tm = R
    for t in (512, 256, 128, 64, 32, 16, 8):
        if R % t == 0:
            tm = t
            if t * C * 4 * 7 <= ROW_TILE_BUDGET:
                break

    def f(w, g, m, v):
        m = ADAM_B1 * m + (1.0 - ADAM_B1) * g
        v = ADAM_B2 * v + (1.0 - ADAM_B2) * (g * g)
        m_hat = m / (1.0 - ADAM_B1 ** ADAM_STEP)
        v_hat = v / (1.0 - ADAM_B2 ** ADAM_STEP)
        delta = -ADAM_LR * (m_hat / (jnp.sqrt(v_hat) + ADAM_EPS) + ADAM_WD * w)
        return delta, m, v

    d, nm, nv = rowwise(f, [a.reshape(R, C) for a in (w, g, m, v)], [], [(C, F32)] * 3, [], "adamw", tm=tm)
    return d.reshape(shape), nm.reshape(shape), nv.reshape(shape)


def _cast_into_slot(w, place):
    C = w.shape[-1]
    w2 = w.reshape(-1, C)
    R = w2.shape[0]
    tm = _row_tile(R, 6 * C)

    def body(place_ref, w_ref, o_ref):
        o_ref[...] = w_ref[...].astype(BF16)

    out = pl.pallas_call(
        body, name="cast_bf16",
        grid_spec=pltpu.PrefetchScalarGridSpec(
            num_scalar_prefetch=1, grid=(R // tm,),
            in_specs=[pl.BlockSpec((tm, C), lambda i, s: (i, 0))],
            out_specs=pl.BlockSpec((None, tm, C), lambda i, s: (s[0], i, 0))),
        out_shape=jax.ShapeDtypeStruct((4, R, C), BF16),
        compiler_params=_params(("parallel",)),
    )(place, w2)
    return out.reshape(4, 2, R // 2, C)


def _pack(vs):
    flat = jnp.concatenate([v.reshape(-1) for v in vs])
    n = flat.shape[0]
    total = -(-n // 1024) * 1024
    return jnp.pad(flat, (0, total - n)).reshape(total // 128, 128)


def _unpack(flat, like):
    out, o = [], 0
    for shp in like:
        sz = 1
        for d in shp:
            sz *= d
        out.append(flat[o:o + sz].reshape(shp))
        o += sz
    return out


def _cols_to_blocks(g, n_chips=4):
    R, N = g.shape
    C = N // n_chips
    return g.reshape(R, n_chips, C).transpose(1, 0, 2).reshape(n_chips, 2, R // 2, C)


def _rows_to_blocks(g, n_chips=4):
    R, C = g.shape
    return g.reshape(n_chips, 2, R // n_chips // 2, C)


def kernel(x, c, positions, ada_w, ada_b, norm_g, ffn_w13, ffn_w2, conv_w_pw1, conv_b_pw1, conv_w_dw, conv_b_dw, conv_ln_g, conv_ln_b, conv_w_pw2, conv_b_pw2, kv_ada_w, kv_ada_b, kv_norm_g, w_kv_a, kv_a_norm_g, w_kv_b, w_q_a, q_a_norm_g, w_q_b, w_o, final_norm_g, loss_target, m_ada_w, m_ada_b, m_norm_g, m_ffn_w13, m_ffn_w2, m_conv_w_pw1, m_conv_b_pw1, m_conv_w_dw, m_conv_b_dw, m_conv_ln_g, m_conv_ln_b, m_conv_w_pw2, m_conv_b_pw2, m_kv_ada_w, m_kv_ada_b, m_kv_norm_g, m_w_kv_a, m_kv_a_norm_g, m_w_kv_b, m_w_q_a, m_q_a_norm_g, m_w_q_b, m_w_o, m_final_norm_g, v_ada_w, v_ada_b, v_norm_g, v_ffn_w13, v_ffn_w2, v_conv_w_pw1, v_conv_b_pw1, v_conv_w_dw, v_conv_b_dw, v_conv_ln_g, v_conv_ln_b, v_conv_w_pw2, v_conv_b_pw2, v_kv_ada_w, v_kv_ada_b, v_kv_norm_g, v_w_kv_a, v_kv_a_norm_g, v_w_kv_b, v_w_q_a, v_q_a_norm_g, v_w_q_b, v_w_o, v_final_norm_g):
    S, D = x.shape[1], x.shape[2]
    H = N_HEADS
    F = ffn_w2.shape[2] * 4
    xi, yi, ci = lax.axis_index("x"), lax.axis_index("y"), lax.axis_index("c")
    chip = 2 * xi + yi
    dev = 2 * chip + ci
    place = jnp.stack([chip, ci]).astype(jnp.int32)
    h0 = x[0]
    target = loss_target[0]

    silu_c = rowwise(lambda a: a * _sigmoid(a), [c], [], [(D, F32)], [], "silu_c")[0]
    silu_all = all_gather8(silu_c.reshape(8, D // 8)).reshape(8, D)
    n_ada = ada_w.shape[2]
    n_kv = kv_ada_w.shape[1]
    ada_b_mine = lax.dynamic_slice_in_dim(ada_b, chip * n_ada, n_ada, axis=1)
    kv_b_mine = lax.dynamic_slice_in_dim(kv_ada_b, chip * n_kv, n_kv, axis=0)[None, :]
    mods = [mm(silu_all, ada_w[l], "nn", "ada_rows", bias=ada_b_mine[l:l + 1]) for l in range(2)]
    mods.append(mm(silu_all, kv_ada_w, "nn", "kv_ada_rows", bias=kv_b_mine))
    n_mod_cols = 2 * n_ada + n_kv
    mod_pack = jnp.concatenate(mods, axis=1).reshape(-1, 128)
    mod_all = all_gather8(mod_pack).reshape(8, 8, n_mod_cols)[0::2]
    mod_mine = lax.dynamic_index_in_dim(mod_all, dev, axis=1, keepdims=False)
    mod = [mod_mine[:, l * n_ada:(l + 1) * n_ada].reshape(N_MOD, D) for l in range(2)]
    kv_mod = mod_mine[:, 2 * n_ada:].reshape(2, D)
    kv_shift, kv_scale = kv_mod[0:1], kv_mod[1:2]

    def mrow(l, k):
        return mod[l][k:k + 1]

    big = dict(ffn_w13=ffn_w13, ffn_w2=ffn_w2, conv_w_pw1=conv_w_pw1, conv_w_pw2=conv_w_pw2, w_kv_a=w_kv_a,
               w_kv_b=w_kv_b, w_q_a=w_q_a, w_q_b=w_q_b, w_o=w_o)
    names = list(big)
    gathered = gather_weights([_cast_into_slot(big[k], place) for k in names])
    gw = dict(zip(names, gathered))
    small_like = [norm_g.shape, conv_b_pw1.shape, conv_w_dw.shape, conv_b_dw.shape, conv_ln_g.shape,
                  conv_ln_b.shape, conv_b_pw2.shape]
    small_pack = _pack([norm_g, conv_b_pw1, conv_w_dw, conv_b_dw, conv_ln_g, conv_ln_b, conv_b_pw2])
    small_all = all_gather8(small_pack)[0::2].reshape(4, -1)
    per_chip = [_unpack(small_all[j], small_like) for j in range(4)]
    smalls = [jnp.concatenate([per_chip[j][k] for j in range(4)], axis=-1) for k in range(len(small_like))]
    norm_g_f, b_pw1_f, w_dw_f, b_dw_f, ln_g_f, ln_b_f, b_pw2_f = smalls

    gw13 = gw["ffn_w13"].reshape(4, 2, 2, D, F // 2)
    w2 = gw["ffn_w2"].reshape(4, 2, 2, F // 4, D).transpose(1, 2, 0, 3, 4).reshape(2, 2, F, D)
    conv_p = dict(
        w_pw1=gw["conv_w_pw1"].reshape(4, D, 2 * D // 4).transpose(1, 0, 2).reshape(D, 2 * D),
        b_pw1=b_pw1_f, w_dw=w_dw_f[0], b_dw=b_dw_f, ln_g=ln_g_f, ln_b=ln_b_f,
        w_pw2=gw["conv_w_pw2"].reshape(D, D), b_pw2=b_pw2_f)
    q_lora = w_q_a.shape[2]
    mla_p = dict(
        w_kv_a=gw["w_kv_a"].reshape(D, KV_LORA + QK_ROPE),
        w_kv_b=gw["w_kv_b"].reshape(4, KV_LORA, -1).transpose(1, 0, 2).reshape(KV_LORA, -1),
        w_q_a=gw["w_q_a"].reshape(D, q_lora),
        w_q_b=gw["w_q_b"].reshape(4, q_lora, -1).transpose(1, 0, 2).reshape(q_lora, -1),
        w_o=gw["w_o"].reshape(H * V_HEAD, D))
    pw = mla_pad_weights(mla_p)
    tabs = rope_tables(positions[0])

    def ng(l, k):
        return norm_g_f[l, k][None, :]

    h = h0
    h, s_f1_0 = ffn_fwd(h, ng(0, 0), mrow(0, 0), mrow(0, 1), mrow(0, 2), gw13, 0, 0, w2[0, 0])
    h, s_conv = conv_module_fwd(h, ng(0, 1), mrow(0, 3), mrow(0, 4), mrow(0, 5), conv_p)
    h, s_f2_0 = ffn_fwd(h, ng(0, 2), mrow(0, 6), mrow(0, 7), mrow(0, 8), gw13, 0, 1, w2[0, 1])
    kv_norm = kv_norm_g[None, :]
    kv_a_g = kv_a_norm_g[None, :]
    kv, kpe_r, s_kv = mla_kv_fwd(h, kv_norm, kv_shift, kv_scale, kv_a_g, pw, tabs)
    h, s_f1_1 = ffn_fwd(h, ng(1, 0), mrow(1, 0), mrow(1, 1), mrow(1, 2), gw13, 1, 0, w2[1, 0])
    h, s_mla = mla_fwd(h, ng(1, 1), mrow(1, 3), mrow(1, 4), mrow(1, 5), q_a_norm_g, pw, kv, kpe_r, tabs)
    h, s_f2_1 = ffn_fwd(h, ng(1, 2), mrow(1, 6), mrow(1, 7), mrow(1, 8), gw13, 1, 1, w2[1, 1])
    dh, loss_cols, d_final_g = loss_head(h, target, final_norm_g[None, :])

    dh, v_f2_1, dw13_11, dw2_11 = ffn_bwd(dh, s_f2_1, ng(1, 2), mrow(1, 7), mrow(1, 8), gw13, 1, 1, w2[1, 1])
    dh, v_mla, g_mla, dk, dv = mla_bwd(dh, s_mla, ng(1, 1), mrow(1, 4), mrow(1, 5), q_a_norm_g, pw, kv, kpe_r, tabs)
    dh, v_f1_1, dw13_10, dw2_10 = ffn_bwd(dh, s_f1_1, ng(1, 0), mrow(1, 1), mrow(1, 2), gw13, 1, 0, w2[1, 0])
    dh, v_kv, d_kv_a_g, d_w_kv_a, d_w_kv_b = mla_kv_bwd(dh, dk, dv, s_kv, kv_norm, kv_scale, kv_a_g, pw, tabs)
    dh, v_f2_0, dw13_01, dw2_01 = ffn_bwd(dh, s_f2_0, ng(0, 2), mrow(0, 7), mrow(0, 8), gw13, 0, 1, w2[0, 1])
    dh, v_conv, g_conv = conv_module_bwd(dh, s_conv, ng(0, 1), mrow(0, 4), mrow(0, 5), conv_p)
    dh, v_f1_0, dw13_00, dw2_00 = ffn_bwd(dh, s_f1_0, ng(0, 0), mrow(0, 1), mrow(0, 2), gw13, 0, 0, w2[0, 0])
    grad_x = dh[None]

    d_w_kv_a_u = jnp.concatenate([d_w_kv_a[:, :KV_LORA], d_w_kv_a[:, KV_LORA + QK_NOPE:KV_LORA + QK_NOPE + QK_ROPE]],
                                 axis=1)
    hk = H * HEAD_PAD
    dkb = jnp.concatenate([d_w_kv_b[:, :hk].reshape(KV_LORA, H, HEAD_PAD)[:, :, :QK_NOPE],
                           d_w_kv_b[:, hk:].reshape(KV_LORA, H, HEAD_PAD)[:, :, :V_HEAD]], axis=2).reshape(KV_LORA, -1)
    d_w_q_b_u = _unpad_heads(g_mla["w_q_b"], QK_NOPE + QK_ROPE)
    d_w_o_u = g_mla["w_o"].reshape(H, HEAD_PAD, D)[:, :V_HEAD].reshape(H * V_HEAD, D)
    full = [dw.reshape(4, 2, D // 2, F // 2) for dw in (dw13_00, dw13_01, dw13_10, dw13_11)] + [
            _rows_to_blocks(dw2_00), _rows_to_blocks(dw2_01), _rows_to_blocks(dw2_10), _rows_to_blocks(dw2_11),
            _cols_to_blocks(g_conv["w_pw1"]), _rows_to_blocks(g_conv["w_pw2"]), _rows_to_blocks(d_w_kv_a_u),
            _cols_to_blocks(dkb), _rows_to_blocks(g_mla["w_q_a"]), _cols_to_blocks(d_w_q_b_u),
            _rows_to_blocks(d_w_o_u)]
    red = reduce_scatter_grads(full, place)
    g_ffn_w13 = jnp.stack(red[0:4]).reshape(ffn_w13.shape)
    g_ffn_w2 = jnp.stack(red[4:8]).reshape(ffn_w2.shape)
    g_conv_w_pw1 = red[8].reshape(conv_w_pw1.shape)
    g_conv_w_pw2 = red[9].reshape(conv_w_pw2.shape)
    g_w_kv_a = red[10].reshape(w_kv_a.shape)
    g_w_kv_b = red[11].reshape(w_kv_b.shape)
    g_w_q_a = red[12].reshape(w_q_a.shape)
    g_w_q_b = red[13].reshape(w_q_b.shape)
    g_w_o = red[14].reshape(w_o.shape)

    def dmod(v1, vm, v2):
        return jnp.concatenate([v1[0], v1[1], v1[2], vm[0], vm[1], vm[2], v2[0], v2[1], v2[2]], axis=1)
    d_mod0 = dmod(v_f1_0, v_conv, v_f2_0)
    d_mod1 = dmod(v_f1_1, v_mla, v_f2_1)
    d_kv_mod = jnp.concatenate([v_kv[0], v_kv[1]], axis=1)
    d_norm_g = jnp.concatenate([v_f1_0[3], v_conv[3], v_f2_0[3], v_f1_1[3], v_mla[3], v_f2_1[3]], axis=0)
    vec_list = [d_mod0, d_mod1, d_kv_mod, d_norm_g, g_conv["b_pw1"], g_conv["w_dw"], g_conv["b_dw"], g_conv["ln_g"],
                g_conv["ln_b"], g_conv["b_pw2"], v_kv[2], d_kv_a_g, g_mla["q_a_norm_g"], d_final_g, loss_cols]
    vec_like = [v.shape for v in vec_list]
    vec_pack = _pack(vec_list)
    n_mod_rows = (2 * N_MOD * D + 2 * D) // 128
    vec_all = all_gather8(vec_pack)
    vec_sum = sum_blocks([(vec_all, d) for d in range(8)], "sum_devices").reshape(-1)
    (_, _, _, s_norm_g, s_b_pw1, s_w_dw, s_b_dw, s_ln_g, s_ln_b, s_b_pw2, s_kv_norm_g, s_kv_a_g, s_q_a_g,
     s_final_g, s_loss) = _unpack(vec_sum, vec_like)
    loss = jnp.sum(s_loss)
    dmod_all = vec_all[:, :n_mod_rows].reshape(8, 2 * N_MOD * D + 2 * D)
    dmod_sum = vec_sum[:2 * N_MOD * D + 2 * D]
    g_ada_b = dmod_sum[:2 * N_MOD * D].reshape(2, N_MOD * D)
    g_kv_ada_b = dmod_sum[2 * N_MOD * D:]
    g_ada_w = []
    for l in range(2):
        cols = lax.dynamic_slice_in_dim(dmod_all[:, l * N_MOD * D:(l + 1) * N_MOD * D], chip * n_ada, n_ada, axis=1)
        g_ada_w.append(mm(silu_all, cols, "tn", "ada_w_grad"))
    g_ada_w = jnp.stack(g_ada_w)
    kv_cols = lax.dynamic_slice_in_dim(dmod_all[:, 2 * N_MOD * D:], chip * n_kv, n_kv, axis=1)
    g_kv_ada_w = mm(silu_all, kv_cols, "tn", "kv_ada_w_grad")

    def shard(v, width):
        return lax.dynamic_slice_in_dim(v, chip * width, width, axis=v.ndim - 1)

    Dq = D // 4
    g_norm_g = shard(s_norm_g.reshape(2, 3, D), Dq)
    g_conv_b_pw1 = shard(s_b_pw1, 2 * D // 4)
    g_conv_w_dw = shard(s_w_dw, Dq)[None]
    g_conv_b_dw = shard(s_b_dw, Dq)
    g_conv_ln_g = shard(s_ln_g, Dq)
    g_conv_ln_b = shard(s_ln_b, Dq)
    g_conv_b_pw2 = shard(s_b_pw2, Dq)

    grads = [g_ada_w, g_ada_b, g_norm_g, g_ffn_w13, g_ffn_w2, g_conv_w_pw1, g_conv_b_pw1, g_conv_w_dw, g_conv_b_dw,
             g_conv_ln_g, g_conv_ln_b, g_conv_w_pw2, g_conv_b_pw2, g_kv_ada_w, g_kv_ada_b, s_kv_norm_g[0], g_w_kv_a,
             s_kv_a_g[0], g_w_kv_b, g_w_q_a, s_q_a_g, g_w_q_b, g_w_o, s_final_g[0]]
    weights = [ada_w, ada_b, norm_g, ffn_w13, ffn_w2, conv_w_pw1, conv_b_pw1, conv_w_dw, conv_b_dw, conv_ln_g,
               conv_ln_b, conv_w_pw2, conv_b_pw2, kv_ada_w, kv_ada_b, kv_norm_g, w_kv_a, kv_a_norm_g, w_kv_b, w_q_a,
               q_a_norm_g, w_q_b, w_o, final_norm_g]
    ms = [m_ada_w, m_ada_b, m_norm_g, m_ffn_w13, m_ffn_w2, m_conv_w_pw1, m_conv_b_pw1, m_conv_w_dw, m_conv_b_dw,
          m_conv_ln_g, m_conv_ln_b, m_conv_w_pw2, m_conv_b_pw2, m_kv_ada_w, m_kv_ada_b, m_kv_norm_g, m_w_kv_a,
          m_kv_a_norm_g, m_w_kv_b, m_w_q_a, m_q_a_norm_g, m_w_q_b, m_w_o, m_final_norm_g]
    vs = [v_ada_w, v_ada_b, v_norm_g, v_ffn_w13, v_ffn_w2, v_conv_w_pw1, v_conv_b_pw1, v_conv_w_dw, v_conv_b_dw,
          v_conv_ln_g, v_conv_ln_b, v_conv_w_pw2, v_conv_b_pw2, v_kv_ada_w, v_kv_ada_b, v_kv_norm_g, v_w_kv_a,
          v_kv_a_norm_g, v_w_kv_b, v_w_q_a, v_q_a_norm_g, v_w_q_b, v_w_o, v_final_norm_g]
    grads = [g.reshape(w.shape) for g, w in zip(grads, weights)]
    deltas, new_m, new_v = [], [], []
    for w, g, m, v in zip(weights, grads, ms, vs):
        d, nm, nv = adamw(w, g, m, v)
        deltas.append(d)
        new_m.append(nm)
        new_v.append(nv)
    return (loss, grad_x, *grads, *deltas, *new_m, *new_v)
```

```python
import jax
import jax.numpy as jnp
from jax import lax
from jax.experimental import pallas as pl
from jax.experimental.pallas import tpu as pltpu

F32 = jnp.float32
BF16 = jnp.bfloat16
MESH = pl.DeviceIdType.MESH

N_HEADS = 16
QK_NOPE = 64
QK_ROPE = 32
V_HEAD = 64
KV_LORA = 256
CONV_WIDTH = 31
ROPE_THETA = 10000.0
EPS = 1e-6
N_MOD = 9
HEAD_PAD = 128
ATTN_TILE = 512
CONV_HALO = 32

ADAM_LR = 0.001
ADAM_B1 = 0.9
ADAM_B2 = 0.999
ADAM_EPS = 1e-08
ADAM_WD = 0.01
ADAM_STEP = 10

VMEM_LIMIT_BYTES = 56 * 2 ** 20
ROW_TILE_BUDGET = 10 * 2 ** 20
MM_VMEM_BUDGET = 40 * 2 ** 20
NEG = float(jnp.finfo(jnp.float32).min)
LOG2_E = 1.4426950408889634


def _tile(n, prefs):
    for t in prefs:
        if n % t == 0:
            return t
    return n


def _params(sem):
    return pltpu.CompilerParams(dimension_semantics=sem, vmem_limit_bytes=VMEM_LIMIT_BYTES)


def _mm_tiles(M, N, K, mode, a_bytes, b_bytes, o_bytes):
    if mode == "tn":
        tk_opts = [t for t in (2048, 1024, 512, 256, 128) if K % t == 0] or [K]
        tm_opts = ([M] if M <= 2816 else []) + [t for t in (1024, 512, 256, 128) if M % t == 0 and t < M]
    else:
        tk_opts = [K]
        tm_opts = [t for t in (1024, 512, 256, 128) if M % t == 0] or [M]
    tn_opts = [t for t in (1408, 1024, 512, 384, 256, 128) if N % t == 0] or [N]

    def need(tm, tn, tk):
        blocks = 2 * (tm * tk * a_bytes + tk * tn * b_bytes + tm * tn * o_bytes)
        return blocks + (tm * tn * 4 if mode == "tn" else 0)

    tk_floor = next((t for t in tk_opts if t <= 512), tk_opts[-1])
    for tm in tm_opts:
        for tn in tn_opts:
            if need(tm, tn, tk_floor) <= MM_VMEM_BUDGET:
                return tm, tn, next(tk for tk in tk_opts if need(tm, tn, tk) <= MM_VMEM_BUDGET)
    return tm_opts[-1], tn_opts[-1], tk_opts[-1]


def mm(a, b, mode, name, out_dtype=F32, bias=None):
    if mode == "nn":
        (M, K), (K2, N) = a.shape, b.shape
        dims = (((1,), (0,)), ((), ()))
    elif mode == "nt":
        (M, K), (N, K2) = a.shape, b.shape
        dims = (((1,), (1,)), ((), ()))
    else:
        (K, M), (K2, N) = a.shape, b.shape
        dims = (((0,), (0,)), ((), ()))
    assert K == K2, (a.shape, b.shape, mode)
    tm, tn, tk = _mm_tiles(M, N, K, mode, a.dtype.itemsize, b.dtype.itemsize, jnp.dtype(out_dtype).itemsize)
    nk = K // tk
    if mode == "tn":
        a_spec = pl.BlockSpec((tk, tm), lambda i, j, k: (k, i))
        b_spec = pl.BlockSpec((tk, tn), lambda i, j, k: (k, j))
    elif mode == "nn":
        a_spec = pl.BlockSpec((tm, tk), lambda i, j, k: (i, k))
        b_spec = pl.BlockSpec((tk, tn), lambda i, j, k: (k, j))
    else:
        a_spec = pl.BlockSpec((tm, tk), lambda i, j, k: (i, k))
        b_spec = pl.BlockSpec((tn, tk), lambda i, j, k: (j, k))
    in_specs = [a_spec, b_spec]
    operands = [a, b]
    if bias is not None:
        in_specs.append(pl.BlockSpec((1, tn), lambda i, j, k: (0, j)))
        operands.append(bias)
    has_bias = bias is not None

    def body(*refs):
        a_ref, b_ref = refs[0], refs[1]
        bias_ref = refs[2] if has_bias else None
        o_ref = refs[3] if has_bias else refs[2]
        prod = lax.dot_general(a_ref[...].astype(BF16), b_ref[...].astype(BF16), dims,
                               preferred_element_type=F32)
        if nk == 1:
            if has_bias:
                prod = prod + bias_ref[...]
            o_ref[...] = prod.astype(o_ref.dtype)
        else:
            acc_ref = refs[-1]
            k = pl.program_id(2)

            @pl.when(k == 0)
            def _():
                acc_ref[...] = jnp.zeros_like(acc_ref)

            acc_ref[...] += prod

            @pl.when(k == nk - 1)
            def _():
                out = acc_ref[...]
                if has_bias:
                    out = out + bias_ref[...]
                o_ref[...] = out.astype(o_ref.dtype)

    return pl.pallas_call(
        body, name=name,
        grid=(M // tm, N // tn, nk),
        in_specs=in_specs,
        out_specs=pl.BlockSpec((tm, tn), lambda i, j, k: (i, j)),
        out_shape=jax.ShapeDtypeStruct((M, N), out_dtype),
        scratch_shapes=[pltpu.VMEM((tm, tn), F32)] if nk > 1 else [],
        compiler_params=_params(("parallel", "parallel", "arbitrary")),
    )(*operands)


def mm_fused(a, b, mode, name, tn, epi, epi_outs, pro=None, pro_rows=(), pro_vecs=(), pro_out=False, n_pro_sums=0,
             epi_rows=(), epi_vecs=(), b_blocks=None, n_cols=None):
    M, K = a.shape
    if b_blocks is not None:
        n_b, N = len(b_blocks), n_cols
    else:
        n_b = b.shape[0] if b.ndim == 3 else 1
        N = b.shape[-1] if mode == "nn" else b.shape[0]
    dims = (((1,), (0,)), ((), ())) if mode == "nn" else (((1,), (1,)), ((), ()))
    nj = N // tn
    epi_outs = [o if len(o) == 3 else (*o, None) for o in epi_outs]
    row_bytes = 2 * (K * a.dtype.itemsize + sum(K * r.dtype.itemsize for r in pro_rows) + (2 * K if pro_out else 0)
                     + sum(w * r.dtype.itemsize * (r.shape[0] if r.ndim == 3 else 1) for r, w in epi_rows)
                     + sum(w * jnp.dtype(dt).itemsize * (L or 1) for w, dt, L in epi_outs)
                     ) + (2 * K if pro is not None else 0)
    fixed = 2 * n_b * K * tn * b.dtype.itemsize
    tm = next((t for t in (1024, 512, 256, 128) if M % t == 0 and t * row_bytes + fixed <= MM_VMEM_BUDGET), M)
    row = lambda i, j: (i, 0)
    tile = lambda i, j: (i, j)
    stack = lambda i, j: (0, i, j)
    in_specs = [pl.BlockSpec((tm, K), row)] + [pl.BlockSpec((tm, K), row) for _ in pro_rows]
    in_specs += [pl.BlockSpec(v.shape, lambda i, j: (0, 0)) for v in pro_vecs]
    if b_blocks is not None:
        in_specs += [pl.BlockSpec(shape, imap) for shape, imap in b_blocks]
    elif b.ndim == 3:
        in_specs += [pl.BlockSpec((None, K, tn), lambda i, j, h=h: (h, 0, j)) for h in range(n_b)]
    elif mode == "nn":
        in_specs += [pl.BlockSpec((K, tn), lambda i, j: (0, j))]
    else:
        in_specs += [pl.BlockSpec((tn, K), lambda i, j: (j, 0))]
    in_specs += [pl.BlockSpec((r.shape[0], tm, w), stack) if r.ndim == 3 else pl.BlockSpec((tm, w), tile)
                 for r, w in epi_rows]
    in_specs += [pl.BlockSpec((1, tn), lambda i, j: (0, j)) for _ in epi_vecs]
    out_specs, out_shape = [], []
    if pro_out:
        out_specs.append(pl.BlockSpec((tm, K), row))
        out_shape.append(jax.ShapeDtypeStruct((M, K), BF16))
    for _ in range(n_pro_sums):
        out_specs.append(pl.BlockSpec((1, K), lambda i, j: (0, 0)))
        out_shape.append(jax.ShapeDtypeStruct((1, K), F32))
    for w, dt, L in epi_outs:
        out_specs.append(pl.BlockSpec((tm, w), tile) if L is None else pl.BlockSpec((L, tm, w), stack))
        out_shape.append(jax.ShapeDtypeStruct((M, nj * w) if L is None else (L, M, nj * w), dt))
    n_pr, n_pv, n_er, n_ev = len(pro_rows), len(pro_vecs), len(epi_rows), len(epi_vecs)
    n_a = 1 + n_pr + n_pv
    n_in = n_a + n_b + n_er + n_ev
    n_po = 1 if pro_out else 0

    def body(*refs):
        i, j = pl.program_id(0), pl.program_id(1)
        a_ref = refs[0]
        outs = refs[n_in:]
        if pro is not None:
            lhs_ref = refs[-1]

            @pl.when(j == 0)
            def _():
                res = pro(*[r[...] for r in refs[:1 + n_pr + n_pv]])
                if not isinstance(res, (tuple, list)):
                    res = (res,)
                lhs_ref[...] = res[0]
                if pro_out:
                    outs[0][...] = res[0]
                for s_ref, val in zip(outs[n_po:n_po + n_pro_sums], res[1:]):
                    part = jnp.sum(val.astype(F32), axis=0, keepdims=True)

                    @pl.when(i == 0)
                    def _(s_ref=s_ref, part=part):
                        s_ref[...] = part

                    @pl.when(i != 0)
                    def _(s_ref=s_ref, part=part):
                        s_ref[...] += part

            lhs = lhs_ref[...]
        else:
            lhs = a_ref[...].astype(BF16)
        accs = [lax.dot_general(lhs, b_ref[...].astype(BF16), dims, preferred_element_type=F32)
                for b_ref in refs[n_a:n_a + n_b]]
        res = epi(*accs, *[r[...] for r in refs[n_a + n_b:n_in]])
        if not isinstance(res, (tuple, list)):
            res = (res,)
        for o_ref, val in zip(outs[n_po + n_pro_sums:], res):
            if isinstance(val, (tuple, list)):
                for h, part in enumerate(val):
                    o_ref[h] = part.astype(o_ref.dtype)
            else:
                o_ref[...] = val.astype(o_ref.dtype)

    return pl.pallas_call(
        body, name=name,
        grid=(M // tm, nj),
        in_specs=in_specs, out_specs=out_specs, out_shape=out_shape,
        scratch_shapes=[pltpu.VMEM((tm, K), BF16)] if pro is not None else [],
        compiler_params=_params(("arbitrary", "arbitrary")),
    )(a, *pro_rows, *pro_vecs, *([b] * n_b), *[r for r, _ in epi_rows], *epi_vecs)


def rowwise(fn, rows, vecs, outs, sums, name, tm=None):
    norm = [(r, r.shape[1], 0) if not isinstance(r, tuple) else r for r in rows]
    S = norm[0][0].shape[0]
    if tm is None:
        per_row = sum(w * r.dtype.itemsize for r, w, _ in norm) + sum(n * jnp.dtype(dt).itemsize for n, dt in outs)
        tm = S
        for t in (512, 256, 128, 64, 32, 16, 8):
            if S % t == 0:
                tm = t
                if t * per_row <= ROW_TILE_BUDGET:
                    break
    n_rows, n_vecs, n_outs, n_sums = len(norm), len(vecs), len(outs), len(sums)
    in_specs = [pl.BlockSpec((tm, w), lambda i, cb=cb: (i, cb)) for _, w, cb in norm]
    in_specs += [pl.BlockSpec(v.shape, lambda i: (0, 0)) for v in vecs]
    out_specs = [pl.BlockSpec((tm, n), lambda i: (i, 0)) for n, _ in outs]
    out_specs += [pl.BlockSpec((1, n), lambda i: (0, 0)) for n in sums]
    out_shape = [jax.ShapeDtypeStruct((S, n), dt) for n, dt in outs]
    out_shape += [jax.ShapeDtypeStruct((1, n), F32) for n in sums]

    def body(*refs):
        ins = [r[...] for r in refs[:n_rows + n_vecs]]
        res = fn(*ins)
        if not isinstance(res, (tuple, list)):
            res = (res,)
        out_refs = refs[n_rows + n_vecs:]
        for o_ref, val in zip(out_refs[:n_outs], res[:n_outs]):
            o_ref[...] = val.astype(o_ref.dtype)
        if n_sums:
            i = pl.program_id(0)
            for s_ref, val in zip(out_refs[n_outs:], res[n_outs:]):
                part = jnp.sum(val.astype(F32), axis=0, keepdims=True)

                @pl.when(i == 0)
                def _(s_ref=s_ref, part=part):
                    s_ref[...] = part

                @pl.when(i != 0)
                def _(s_ref=s_ref, part=part):
                    s_ref[...] += part

    res = pl.pallas_call(
        body, name=name,
        grid=(S // tm,),
        in_specs=in_specs, out_specs=out_specs, out_shape=out_shape,
        compiler_params=_params(("arbitrary",) if n_sums else ("parallel",)),
    )(*[r for r, _, _ in norm], *vecs)
    return res


def _sigmoid(x):
    return jax.nn.sigmoid(x)


def _rms(x):
    r = lax.rsqrt(jnp.mean(x * x, axis=-1, keepdims=True) + EPS)
    return x * r, r


def _rms_bwd(xhat, r, dxhat):
    return r * (dxhat - xhat * jnp.mean(dxhat * xhat, axis=-1, keepdims=True))


def norm_mod(h, g, sh, sc, name):
    def f(h, g, sh, sc):
        xhat, _ = _rms(h)
        return ((xhat * g) * (1 + sc) + sh).astype(BF16)
    return rowwise(f, [h], [g, sh, sc], [(h.shape[1], BF16)], [], name)[0]


def norm_mod_bwd(h, dhn, dh_out, g, sc, name):
    D = h.shape[1]
    with_res = dh_out is not None

    def f(*a):
        if with_res:
            h, dhn, dres, g, sc = a
        else:
            h, dhn, g, sc = a
        xhat, r = _rms(h)
        xn = xhat * g
        dxn = dhn * (1 + sc)
        dh = _rms_bwd(xhat, r, dxn * g)
        if with_res:
            dh = dh + dres
        return dh, dhn, dhn * xn, dxn * xhat

    rows = [h, dhn] + ([dh_out] if with_res else [])
    return rowwise(f, rows, [g, sc], [(D, F32)], [D, D, D], name)


def residual(h, y, gate, coef, name, bias=None):
    D = h.shape[1]
    if bias is None:
        def f(h, y, gate):
            return h + (coef * gate) * y
        return rowwise(f, [h, y], [gate], [(D, F32)], [], name)[0], y

    def fb(h, y, gate, bias):
        yb = y + bias
        return h + (coef * gate) * yb, yb
    return rowwise(fb, [h, y], [gate, bias], [(D, F32), (D, F32)], [], name)


def residual_bwd(dh_out, y, gate, coef, name, with_bias_sum=False):
    D = y.shape[1]

    def f(dh, y, gate):
        dy = (coef * gate) * dh
        res = (dy.astype(BF16), coef * dh * y)
        return res + ((dy,) if with_bias_sum else ())
    return rowwise(f, [dh_out, y], [gate], [(D, BF16)], [D, D] if with_bias_sum else [D], name)


def ffn_w13_dx(dab, gw13, l, i):
    _, S, F = dab.shape
    D, C = gw13.shape[3:]
    tm = _tile(S, (1024, 512, 256, 128))
    nt = (((1,), (1,)), ((), ()))

    def body(a_ref, b_ref, o_ref, acc_ref):
        k = pl.program_id(1)
        prod = lax.dot_general(a_ref[...], b_ref[...], nt, preferred_element_type=F32)

        @pl.when(k == 0)
        def _():
            acc_ref[...] = prod

        @pl.when((k > 0) & (k < 3))
        def _():
            acc_ref[...] += prod

        @pl.when(k == 3)
        def _():
            o_ref[...] = acc_ref[...] + prod

    return pl.pallas_call(
        body, name="ffn_w13_dx",
        grid=(S // tm, 4),
        in_specs=[pl.BlockSpec((None, tm, C), lambda r, k: (k // 2, r, k % 2)),
                  pl.BlockSpec((None, None, None, D, C), lambda r, k: (k, l, i, 0, 0))],
        out_specs=pl.BlockSpec((tm, D), lambda r, k: (r, 0)),
        out_shape=jax.ShapeDtypeStruct((S, D), F32),
        scratch_shapes=[pltpu.VMEM((tm, D), F32)],
        compiler_params=_params(("parallel", "arbitrary")),
    )(dab, gw13)


def ffn_w13_grad(hn, dab):
    S, D = hn.shape
    F = dab.shape[2]
    C = F // 2
    tk = next(t for t in (2048, 1024, 512, 256, 128) if S % t == 0)
    tn_dims = (((0,), (0,)), ((), ()))
    nk = S // tk

    def body(a_ref, b_ref, o_ref, acc_ref):
        k = pl.program_id(1)

        @pl.when(k == 0)
        def _():
            acc_ref[...] = jnp.zeros_like(acc_ref)

        acc_ref[...] += lax.dot_general(a_ref[...], b_ref[...], tn_dims, preferred_element_type=F32)

        @pl.when(k == nk - 1)
        def _():
            o_ref[...] = acc_ref[...]

    return pl.pallas_call(
        body, name="ffn_w13_dw",
        grid=(4, nk),
        in_specs=[pl.BlockSpec((tk, D), lambda j, k: (k, 0)),
                  pl.BlockSpec((None, tk, C), lambda j, k: (j // 2, k, j % 2))],
        out_specs=pl.BlockSpec((None, D, C), lambda j, k: (j, 0, 0)),
        out_shape=jax.ShapeDtypeStruct((4, D, C), F32),
        scratch_shapes=[pltpu.VMEM((D, C), F32)],
        compiler_params=_params(("parallel", "arbitrary")),
    )(hn, dab)


def _ffn_chunk(F):
    return _tile(F, (256, 128))


def ffn_fwd(h, g, sh, sc, gate, gw13, l, i, w2):
    F, D = w2.shape
    C = F // 2

    def norm(h, g, sh, sc):
        xhat, _ = _rms(h)
        return ((xhat * g) * (1 + sc) + sh).astype(BF16)

    def act(a, b):
        return (a, b), (a * _sigmoid(a)) * b
    blocks = [((None, None, None, D, C), lambda r, j, half=half: (2 * half + j, l, i, 0, 0)) for half in range(2)]
    hn, ab, t = mm_fused(h, gw13, "nn", "ffn_w13", C, act, [(C, F32, 2), (C, BF16)],
                         pro=norm, pro_vecs=[g, sh, sc], pro_out=True, b_blocks=blocks, n_cols=F)

    def res(acc, h, gate):
        return h + (0.5 * gate) * acc, acc
    h_out, y = mm_fused(t, w2, "nn", "ffn_w2", D, res, [(D, F32), (D, F32)], epi_rows=[(h, D)], epi_vecs=[gate])
    return h_out, (h, hn, ab, y)


def ffn_bwd(dh_out, saved, g, sc, gate, gw13, l, i, w2):
    h, hn, ab, y = saved
    F, D = w2.shape
    cf = _ffn_chunk(F)

    def scale(dh, y, gate):
        return ((0.5 * gate) * dh).astype(BF16), 0.5 * dh * y

    def act_bwd(dt, ab):
        a, b = ab[0], ab[1]
        sig = _sigmoid(a)
        sa = a * sig
        da = dt * b * (sig * (1 + a * (1 - sig)))
        db = dt * sa
        return sa * b, (da, db)
    dy, d_gate, t, dab = mm_fused(dh_out, w2, "nt", "ffn_w2_dx", cf, act_bwd, [(cf, BF16), (cf, BF16, 2)],
                                  pro=scale, pro_rows=[y], pro_vecs=[gate], pro_out=True, n_pro_sums=1,
                                  epi_rows=[(ab, cf)])
    dw2 = mm(t, dy, "tn", "ffn_w2_dw")
    dw13 = ffn_w13_grad(hn, dab)
    dhn = ffn_w13_dx(dab, gw13, l, i)
    dh_in, d_sh, d_sc, d_g = norm_mod_bwd(h, dhn, dh_out, g, sc, "norm_mod_bwd")
    return dh_in, (d_sh, d_sc, d_gate, d_g), dw13, dw2


def _shifted(xbuf, n):
    return [xbuf] + [pltpu.roll(xbuf, n - b, 0) for b in range(1, 8)]


def conv_fwd(u, w_dw, b_dw, ln_g, ln_b):
    S, D = u.shape
    tm = _tile(S, (256, 128))
    rc = 32
    first_tap = CONV_HALO - (CONV_WIDTH - 1)
    w = jnp.concatenate([w_dw, jnp.zeros((CONV_HALO - CONV_WIDTH, D), F32)], axis=0)

    def body(cur_ref, prev_ref, w_ref, b_ref, g_ref, beta_ref, z_ref, s_ref):
        i = pl.program_id(0)
        prev = jnp.where(i == 0, jnp.zeros((CONV_HALO, D), F32), prev_ref[...])
        xs = _shifted(jnp.concatenate([prev, cur_ref[...]], axis=0), tm + CONV_HALO)
        for c0 in range(0, tm, rc):
            acc = jnp.zeros((rc, D), F32)
            for k in range(CONV_WIDTH):
                off = first_tap + k
                a8, b = off // 8 * 8, off % 8
                acc = acc + w_ref[k:k + 1, :] * xs[b][c0 + a8:c0 + a8 + rc, :]
            z_ref[c0:c0 + rc, :] = acc + b_ref[...]
        z = z_ref[...]
        mu = jnp.mean(z, axis=-1, keepdims=True)
        zc = z - mu
        r = lax.rsqrt(jnp.mean(zc * zc, axis=-1, keepdims=True) + EPS)
        un = zc * r * g_ref[...] + beta_ref[...]
        s_ref[...] = (un * _sigmoid(un)).astype(BF16)

    nb = tm // CONV_HALO
    vec = pl.BlockSpec((1, D), lambda i: (0, 0))
    return pl.pallas_call(
        body, name="conv_fwd",
        grid=(S // tm,),
        in_specs=[pl.BlockSpec((tm, D), lambda i: (i, 0)),
                  pl.BlockSpec((CONV_HALO, D), lambda i: (jnp.maximum(i * nb - 1, 0), 0)),
                  pl.BlockSpec((CONV_HALO, D), lambda i: (0, 0)), vec, vec, vec],
        out_specs=[pl.BlockSpec((tm, D), lambda i: (i, 0)), pl.BlockSpec((tm, D), lambda i: (i, 0))],
        out_shape=[jax.ShapeDtypeStruct((S, D), F32), jax.ShapeDtypeStruct((S, D), BF16)],
        compiler_params=_params(("parallel",)),
    )(u, u, w, b_dw, ln_g, ln_b)


def conv_bwd(dz, u, w_dw):
    S, D = u.shape
    tm = _tile(S, (256, 128))
    rc = 32
    first_tap = CONV_HALO - (CONV_WIDTH - 1)
    w = jnp.concatenate([w_dw, jnp.zeros((CONV_HALO - CONV_WIDTH, D), F32)], axis=0)
    n_tiles = S // tm
    nb = tm // CONV_HALO

    def body(dz_ref, dzn_ref, u_ref, up_ref, w_ref, du_ref, dw_ref):
        i = pl.program_id(0)
        nxt = jnp.where(i == n_tiles - 1, jnp.zeros((CONV_HALO, D), F32), dzn_ref[...])
        dzs = _shifted(jnp.concatenate([dz_ref[...], nxt], axis=0), tm + CONV_HALO)
        for c0 in range(0, tm, rc):
            acc = jnp.zeros((rc, D), F32)
            for m in range(CONV_WIDTH):
                a8, b = m // 8 * 8, m % 8
                acc = acc + w_ref[CONV_WIDTH - 1 - m:CONV_WIDTH - m, :] * dzs[b][c0 + a8:c0 + a8 + rc, :]
            du_ref[c0:c0 + rc, :] = acc
        prev = jnp.where(i == 0, jnp.zeros((CONV_HALO, D), F32), up_ref[...])
        us = _shifted(jnp.concatenate([prev, u_ref[...]], axis=0), tm + CONV_HALO)
        dz = dz_ref[...]

        @pl.when(i == 0)
        def _():
            dw_ref[...] = jnp.zeros_like(dw_ref)

        for k in range(CONV_WIDTH):
            off = first_tap + k
            a8, b = off // 8 * 8, off % 8
            dw_ref[k:k + 1, :] += jnp.sum(dz * us[b][a8:a8 + tm, :], axis=0, keepdims=True)

    last_blk = S // CONV_HALO - 1
    du, dw = pl.pallas_call(
        body, name="conv_bwd",
        grid=(n_tiles,),
        in_specs=[pl.BlockSpec((tm, D), lambda i: (i, 0)),
                  pl.BlockSpec((CONV_HALO, D), lambda i: (jnp.minimum((i + 1) * nb, last_blk), 0)),
                  pl.BlockSpec((tm, D), lambda i: (i, 0)),
                  pl.BlockSpec((CONV_HALO, D), lambda i: (jnp.maximum(i * nb - 1, 0), 0)),
                  pl.BlockSpec((CONV_HALO, D), lambda i: (0, 0))],
        out_specs=[pl.BlockSpec((tm, D), lambda i: (i, 0)), pl.BlockSpec((CONV_HALO, D), lambda i: (0, 0))],
        out_shape=[jax.ShapeDtypeStruct((S, D), F32), jax.ShapeDtypeStruct((CONV_HALO, D), F32)],
        compiler_params=_params(("arbitrary",)),
    )(dz, dz, u, u, w)
    return du, dw[:CONV_WIDTH]


def conv_module_fwd(h, g, sh, sc, gate, p):
    D = h.shape[1]
    hn = norm_mod(h, g, sh, sc, "conv_norm_mod")
    pre = mm(hn, p["w_pw1"], "nn", "conv_pw1")
    ba, bg = p["b_pw1"][:, :D], p["b_pw1"][:, D:]

    def glu(a, gt, ba, bg):
        return (a + ba) * _sigmoid(gt + bg)
    u = rowwise(glu, [(pre, D, 0), (pre, D, 1)], [ba, bg], [(D, F32)], [], "conv_glu")[0]
    z, s = conv_fwd(u, p["w_dw"], p["b_dw"], p["ln_g"], p["ln_b"])
    yraw = mm(s, p["w_pw2"], "nn", "conv_pw2")
    h_out, y = residual(h, yraw, gate, 1.0, "conv_residual", bias=p["b_pw2"])
    return h_out, (h, hn, pre, u, z, s, y)


def conv_module_bwd(dh_out, saved, g, sc, gate, p):
    h, hn, pre, u, z, s, y = saved
    D = h.shape[1]
    dy, d_gate, d_b_pw2 = residual_bwd(dh_out, y, gate, 1.0, "conv_residual_bwd", with_bias_sum=True)
    d_w_pw2 = mm(s, dy, "tn", "conv_pw2_dw")
    ds = mm(dy, p["w_pw2"], "nt", "conv_pw2_dx")

    def ln_bwd(z, ds, g, beta):
        mu = jnp.mean(z, axis=-1, keepdims=True)
        zc = z - mu
        r = lax.rsqrt(jnp.mean(zc * zc, axis=-1, keepdims=True) + EPS)
        xhat = zc * r
        un = xhat * g + beta
        sig = _sigmoid(un)
        d_un = ds * (sig * (1 + un * (1 - sig)))
        dxhat = d_un * g
        dz = r * (dxhat - jnp.mean(dxhat, axis=-1, keepdims=True)
                  - xhat * jnp.mean(dxhat * xhat, axis=-1, keepdims=True))
        return dz, d_un * xhat, d_un, dz
    dz, d_ln_g, d_ln_b, d_b_dw = rowwise(ln_bwd, [z, ds], [p["ln_g"], p["ln_b"]], [(D, F32)], [D, D, D],
                                         "conv_ln_bwd")
    du, d_w_dw = conv_bwd(dz, u, p["w_dw"])
    ba, bg = p["b_pw1"][:, :D], p["b_pw1"][:, D:]

    def glu_bwd(a, gt, du, ba, bg):
        sg = _sigmoid(gt + bg)
        da = du * sg
        dg = du * (a + ba) * (sg * (1 - sg))
        dpre = jnp.concatenate([da, dg], axis=1)
        return dpre.astype(BF16), dpre
    dpre, d_b_pw1 = rowwise(glu_bwd, [(pre, D, 0), (pre, D, 1), du], [ba, bg], [(2 * D, BF16)], [2 * D],
                            "conv_glu_bwd")
    d_w_pw1 = mm(hn, dpre, "tn", "conv_pw1_dw")
    dhn = mm(dpre, p["w_pw1"], "nt", "conv_pw1_dx")
    dh_in, d_sh, d_sc, d_g = norm_mod_bwd(h, dhn, dh_out, g, sc, "norm_mod_bwd")
    grads = dict(w_pw1=d_w_pw1, b_pw1=d_b_pw1, w_dw=d_w_dw, b_dw=d_b_dw, ln_g=d_ln_g, ln_b=d_ln_b,
                 w_pw2=d_w_pw2, b_pw2=d_b_pw2)
    return dh_in, (d_sh, d_sc, d_gate, d_g), grads


def _rope(x, c, s1, s2):
    n = x.shape[1]
    return x * c + pltpu.roll(x, n - QK_ROPE // 2, 1) * s1 + pltpu.roll(x, QK_ROPE // 2, 1) * s2


def _rope_t(dy, c, s1, s2):
    n = dy.shape[1]
    return dy * c + pltpu.roll(dy * s1, QK_ROPE // 2, 1) + pltpu.roll(dy * s2, n - QK_ROPE // 2, 1)


def rope_tables(positions):
    inv_freq = ROPE_THETA ** (-jnp.arange(0, QK_ROPE, 2, dtype=F32) / QK_ROPE)
    ang = positions.astype(F32)[:, None] * inv_freq
    cos, sin = jnp.cos(ang), jnp.sin(ang)
    S = positions.shape[0]
    one = jnp.ones((S, QK_NOPE), F32)
    z16 = jnp.zeros((S, QK_ROPE // 2), F32)
    zn = jnp.zeros((S, QK_NOPE), F32)
    zt = jnp.zeros((S, HEAD_PAD - QK_NOPE - QK_ROPE), F32)
    c = jnp.concatenate([one, cos, cos, zt], axis=1)
    s1 = jnp.concatenate([zn, -sin, z16, zt], axis=1)
    s2 = jnp.concatenate([zn, z16, sin, zt], axis=1)
    return c, s1, s2


def attn_fwd(qr, kv, kpe, n_heads):
    S = qr.shape[0]
    H = n_heads
    tk = _tile(S, (ATTN_TILE,))
    nk = S // tk
    w = 2 if nk % 2 == 0 else 1
    tq = w * tk
    c2 = (QK_NOPE + QK_ROPE) ** -0.5 * LOG2_E
    nt = (((1,), (1,)), ((), ()))

    def body(q_ref, k_ref, v_ref, kpe_ref, o_ref, lse_ref, kf_ref, vt_ref, m_ref, l_ref, acc_ref):
        qi = pl.program_id(1)

        @pl.when(qi == 0)
        def _():
            kf_ref[...] = k_ref[...] + kpe_ref[...]
            for c in range(nk):
                vt_ref[c] = jnp.transpose(v_ref[c * tk:(c + 1) * tk, :].astype(F32)).astype(BF16)

        q = q_ref[...]
        m_ref[...] = jnp.full((1, tq), -jnp.inf, F32)
        l_ref[...] = jnp.zeros((1, tq), F32)
        acc_ref[...] = jnp.zeros((HEAD_PAD, tq), F32)

        def tile(j, first_visible):
            k = kf_ref[pl.ds(pl.multiple_of(j * tk, tk), tk), :]
            t = lax.dot_general(k, q, nt, preferred_element_type=F32) * c2
            if first_visible is not None:
                krow = lax.broadcasted_iota(jnp.int32, (tk, tq), 0)
                qcol = lax.broadcasted_iota(jnp.int32, (tk, tq), 1)
                t = jnp.where(krow + first_visible <= qcol, t, NEG)
            m_old = m_ref[...]
            m_new = jnp.maximum(m_old, jnp.max(t, axis=0, keepdims=True))
            alpha = jnp.exp2(m_old - m_new)
            p = jnp.exp2(t - m_new)
            l_ref[...] = alpha * l_ref[...] + jnp.sum(p, axis=0, keepdims=True)
            acc_ref[...] = alpha * acc_ref[...] + jnp.dot(vt_ref[j], p.astype(BF16), preferred_element_type=F32)
            m_ref[...] = m_new

        def unmasked(j, carry):
            tile(j, None)
            return carry

        lax.fori_loop(0, w * qi, unmasked, 0)
        for u in range(w):
            tile(w * qi + u, u * tk)
        l = l_ref[...]
        o_ref[...] = jnp.transpose(acc_ref[...] / l)
        lse = m_ref[...] + jnp.log(l) * LOG2_E
        for u in range(w):
            lse_ref[u] = lse[:, u * tk:(u + 1) * tk]

    return pl.pallas_call(
        body, name="attn_fwd",
        grid=(H, S // tq),
        in_specs=[pl.BlockSpec((tq, HEAD_PAD), lambda h, i: (i, h)),
                  pl.BlockSpec((S, HEAD_PAD), lambda h, i: (0, h)),
                  pl.BlockSpec((S, HEAD_PAD), lambda h, i: (0, H + h)),
                  pl.BlockSpec((S, HEAD_PAD), lambda h, i: (0, 0))],
        out_specs=[pl.BlockSpec((tq, HEAD_PAD), lambda h, i: (i, h)),
                   pl.BlockSpec((None, w, 1, tk), lambda h, i: (h, i, 0, 0))],
        out_shape=[jax.ShapeDtypeStruct((S, H * HEAD_PAD), F32), jax.ShapeDtypeStruct((H, nk, 1, tk), F32)],
        scratch_shapes=[pltpu.VMEM((S, HEAD_PAD), BF16), pltpu.VMEM((nk, HEAD_PAD, tk), BF16),
                        pltpu.VMEM((1, tq), F32), pltpu.VMEM((1, tq), F32), pltpu.VMEM((HEAD_PAD, tq), F32)],
        compiler_params=_params(("parallel", "arbitrary")),
    )(qr, kv, kv, kpe)


def attn_delta(o, do, n_heads):
    S = o.shape[0]
    H = n_heads
    tq = _tile(S, (ATTN_TILE,))
    nq = S // tq

    def body(o_ref, do_ref, d_ref):
        for c in range(nq):
            rows = slice(c * tq, (c + 1) * tq)
            prod = o_ref[rows, :] * do_ref[rows, :].astype(F32)
            d_ref[c] = jnp.sum(jnp.transpose(prod), axis=0, keepdims=True)

    return pl.pallas_call(
        body, name="attn_delta",
        grid=(H,),
        in_specs=[pl.BlockSpec((S, HEAD_PAD), lambda h: (0, h)), pl.BlockSpec((S, HEAD_PAD), lambda h: (0, h))],
        out_specs=pl.BlockSpec((None, nq, 1, tq), lambda h: (h, 0, 0, 0)),
        out_shape=jax.ShapeDtypeStruct((H, nq, 1, tq), F32),
        compiler_params=_params(("parallel",)),
    )(o, do)


def attn_bwd(qr, kv, kpe, do, lse2, delta, n_heads):
    S = qr.shape[0]
    H = n_heads
    tk = _tile(S, (ATTN_TILE,))
    nk = S // tk
    w = 2 if nk % 2 == 0 else 1
    tq = w * tk
    nq = S // tq
    scale = (QK_NOPE + QK_ROPE) ** -0.5
    c2 = scale * LOG2_E
    nt = (((1,), (1,)), ((), ()))
    lse2 = lse2.reshape(H, nq, 1, tq)
    delta4 = delta.reshape(H, nq, 1, tq)

    def body(k_ref, v_ref, kpe_ref, q_ref, do_ref, lse_ref, dl_ref, dq_ref, dk_ref, dv_ref, dka_ref, dva_ref,
             dqt_ref):
        kj = pl.program_id(1)
        k = k_ref[...] + kpe_ref[...]
        kt = jnp.transpose(k.astype(F32)).astype(BF16)
        v = v_ref[...]

        @pl.when(kj == 0)
        def _():
            dqt_ref[...] = jnp.zeros_like(dqt_ref)

        dka_ref[...] = jnp.zeros_like(dka_ref)
        dva_ref[...] = jnp.zeros_like(dva_ref)

        def tile(i, masked):
            start = pl.multiple_of(i * tq, tq)
            q = q_ref[pl.ds(start, tq), :]
            do = do_ref[pl.ds(start, tq), :]
            t = lax.dot_general(k, q, nt, preferred_element_type=F32) * c2
            if masked:
                krow = lax.broadcasted_iota(jnp.int32, (tk, tq), 0)
                qcol = lax.broadcasted_iota(jnp.int32, (tk, tq), 1)
                t = jnp.where(krow + (kj % w) * tk <= qcol, t, NEG)
            pt = jnp.exp2(t - lse_ref[i])
            dva_ref[...] += jnp.dot(pt.astype(BF16), do, preferred_element_type=F32)
            dpt = lax.dot_general(v, do, nt, preferred_element_type=F32)
            dst = (pt * (dpt - dl_ref[i]) * scale).astype(BF16)
            dka_ref[...] += jnp.dot(dst, q, preferred_element_type=F32)
            dqt_ref[i] += jnp.dot(kt, dst, preferred_element_type=F32)

        tile(kj // w, True)

        def unmasked(i, carry):
            tile(i, False)
            return carry

        lax.fori_loop(kj // w + 1, nq, unmasked, 0)
        dk_ref[...] = dka_ref[...]
        dv_ref[...] = dva_ref[...]

        @pl.when(kj == nk - 1)
        def _():
            for c in range(nq):
                dq_ref[c * tq:(c + 1) * tq, :] = jnp.transpose(dqt_ref[c])

    blk = pl.BlockSpec((tk, HEAD_PAD), lambda h, j: (j, h))
    whole = pl.BlockSpec((S, HEAD_PAD), lambda h, j: (0, h))
    stat = pl.BlockSpec((None, nq, 1, tq), lambda h, j: (h, 0, 0, 0))
    shp = jax.ShapeDtypeStruct((S, H * HEAD_PAD), F32)
    return pl.pallas_call(
        body, name="attn_bwd",
        grid=(H, nk),
        in_specs=[blk, pl.BlockSpec((tk, HEAD_PAD), lambda h, j: (j, H + h)),
                  pl.BlockSpec((tk, HEAD_PAD), lambda h, j: (j, 0)), whole, whole, stat, stat],
        out_specs=[whole, blk, blk],
        out_shape=[shp, shp, shp],
        scratch_shapes=[pltpu.VMEM((tk, HEAD_PAD), F32), pltpu.VMEM((tk, HEAD_PAD), F32),
                        pltpu.VMEM((nq, HEAD_PAD, tq), F32)],
        compiler_params=_params(("parallel", "arbitrary")),
    )(kv, kv, kpe, qr, do, lse2, delta4)


def _pad_heads(w, width):
    R = w.shape[0]
    w3 = w.reshape(R, -1, width)
    return jnp.pad(w3, ((0, 0), (0, 0), (0, HEAD_PAD - width))).reshape(R, -1)


def _unpad_heads(w, width):
    R = w.shape[0]
    return w.reshape(R, -1, HEAD_PAD)[:, :, :width].reshape(R, -1)


def mla_pad_weights(p):
    H = N_HEADS
    w_q_b = _pad_heads(p["w_q_b"], QK_NOPE + QK_ROPE)
    kvb = p["w_kv_b"].reshape(KV_LORA, H, QK_NOPE + V_HEAD)
    wk = _pad_heads(kvb[:, :, :QK_NOPE].reshape(KV_LORA, -1), QK_NOPE)
    wv = _pad_heads(kvb[:, :, QK_NOPE:].reshape(KV_LORA, -1), V_HEAD)
    D = p["w_kv_a"].shape[0]
    a = p["w_kv_a"]
    w_kv_a = jnp.concatenate([a[:, :KV_LORA], jnp.zeros((D, QK_NOPE), a.dtype), a[:, KV_LORA:],
                              jnp.zeros((D, HEAD_PAD - QK_NOPE - QK_ROPE), a.dtype)], axis=1)
    wo = p["w_o"].reshape(H, V_HEAD, -1)
    w_o = jnp.pad(wo, ((0, 0), (0, HEAD_PAD - V_HEAD), (0, 0))).reshape(H * HEAD_PAD, -1)
    return dict(w_q_a=p["w_q_a"], w_q_b=w_q_b, w_kv_b=jnp.concatenate([wk, wv], axis=1), w_kv_a=w_kv_a, w_o=w_o)


def mla_kv_fwd(h, g, sh, sc, kv_a_norm_g, pw, tabs):
    hkv = norm_mod(h, g, sh, sc, "kv_norm_mod")
    ckvp = mm(hkv, pw["w_kv_a"], "nn", "kv_a")

    def f(ckv, kpe, c, s1, s2, g):
        xhat, _ = _rms(ckv)
        return (xhat * g).astype(BF16), _rope(kpe, c, s1, s2).astype(BF16)
    ckv_n, kpe_r = rowwise(f, [(ckvp, KV_LORA, 0), (ckvp, HEAD_PAD, KV_LORA // HEAD_PAD), *tabs], [kv_a_norm_g],
                           [(KV_LORA, BF16), (HEAD_PAD, BF16)], [], "kv_a_norm_rope")
    kv = mm(ckv_n, pw["w_kv_b"], "nn", "kv_b", out_dtype=BF16)
    return kv, kpe_r, (h, hkv, ckvp, ckv_n)


def mla_kv_bwd(dh_stream, dk, dv, saved, g, sc, kv_a_norm_g, pw, tabs):
    h, hkv, ckvp, ckv_n = saved
    H = N_HEADS
    lane = jnp.arange(HEAD_PAD)
    pe_mask = ((lane >= QK_NOPE) & (lane < QK_NOPE + QK_ROPE)).astype(F32)[None, :]

    def f(dk, dv, c, s1, s2, mask):
        tot = dk[:, :HEAD_PAD]
        for hh in range(1, H):
            tot = tot + dk[:, hh * HEAD_PAD:(hh + 1) * HEAD_PAD]
        dkpe = _rope_t(tot * mask, c, s1, s2) * mask
        return jnp.concatenate([dk, dv], axis=1).astype(BF16), dkpe
    dkv, dkpe = rowwise(f, [dk, dv, *tabs], [pe_mask], [(2 * H * HEAD_PAD, BF16), (HEAD_PAD, F32)], [],
                        "kv_split_bwd")
    d_w_kv_b = mm(ckv_n, dkv, "tn", "kv_b_dw")
    dckv_n = mm(dkv, pw["w_kv_b"], "nt", "kv_b_dx")

    def f2(ckv, dn, dkpe, g):
        xhat, r = _rms(ckv)
        dx = _rms_bwd(xhat, r, dn * g)
        return jnp.concatenate([dx, dkpe], axis=1).astype(BF16), dn * xhat
    dckvp, d_kv_a_g = rowwise(f2, [(ckvp, KV_LORA, 0), dckv_n, dkpe], [kv_a_norm_g],
                              [(KV_LORA + HEAD_PAD, BF16)], [KV_LORA], "kv_a_norm_bwd")
    d_w_kv_a = mm(hkv, dckvp, "tn", "kv_a_dw")
    dhkv = mm(dckvp, pw["w_kv_a"], "nt", "kv_a_dx")
    dh, d_sh, d_sc, d_g = norm_mod_bwd(h, dhkv, dh_stream, g, sc, "norm_mod_bwd")
    return dh, (d_sh, d_sc, d_g), d_kv_a_g, d_w_kv_a, d_w_kv_b


def mla_fwd(h, g, sh, sc, gate, q_a_norm_g, pw, kv, kpe_r, tabs):
    H = N_HEADS
    hn = norm_mod(h, g, sh, sc, "mla_norm_mod")
    qa = mm(hn, pw["w_q_a"], "nn", "q_a")

    def f(qa, g):
        xhat, _ = _rms(qa)
        return (xhat * g).astype(BF16)
    qa_n = rowwise(f, [qa], [q_a_norm_g], [(qa.shape[1], BF16)], [], "q_a_norm")[0]
    qp = mm(qa_n, pw["w_q_b"], "nn", "q_b")

    def frope(q, c, s1, s2):
        return jnp.concatenate([_rope(q[:, hh * HEAD_PAD:(hh + 1) * HEAD_PAD], c, s1, s2) for hh in range(H)],
                               axis=1).astype(BF16)
    qr = rowwise(frope, [qp, *tabs], [], [(H * HEAD_PAD, BF16)], [], "q_rope")[0]
    o, lse = attn_fwd(qr, kv, kpe_r, H)
    y = mm(o, pw["w_o"], "nn", "w_o")
    h_out, _ = residual(h, y, gate, 1.0, "mla_residual")
    return h_out, (h, hn, qa, qa_n, qr, o, lse, y)


def mla_bwd(dh_out, saved, g, sc, gate, q_a_norm_g, pw, kv, kpe_r, tabs):
    h, hn, qa, qa_n, qr, o, lse, y = saved
    H = N_HEADS
    dy, d_gate = residual_bwd(dh_out, y, gate, 1.0, "mla_residual_bwd")
    d_w_o = mm(o, dy, "tn", "w_o_dw")
    do = mm(dy, pw["w_o"], "nt", "w_o_dx", out_dtype=BF16)
    delta = attn_delta(o, do, H)
    dqr, dk, dv = attn_bwd(qr, kv, kpe_r, do, lse, delta, H)

    def frope_t(dq, c, s1, s2):
        return jnp.concatenate([_rope_t(dq[:, hh * HEAD_PAD:(hh + 1) * HEAD_PAD], c, s1, s2) for hh in range(H)],
                               axis=1).astype(BF16)
    dqp = rowwise(frope_t, [dqr, *tabs], [], [(H * HEAD_PAD, BF16)], [], "q_rope_bwd")[0]
    d_w_q_b = mm(qa_n, dqp, "tn", "q_b_dw")
    dqa_n = mm(dqp, pw["w_q_b"], "nt", "q_b_dx")

    def f(qa, dn, g):
        xhat, r = _rms(qa)
        return _rms_bwd(xhat, r, dn * g).astype(BF16), dn * xhat
    dqa, d_q_a_g = rowwise(f, [qa, dqa_n], [q_a_norm_g], [(qa.shape[1], BF16)], [qa.shape[1]], "q_a_norm_bwd")
    d_w_q_a = mm(hn, dqa, "tn", "q_a_dw")
    dhn = mm(dqa, pw["w_q_a"], "nt", "q_a_dx")
    dh_in, d_sh, d_sc, d_g = norm_mod_bwd(h, dhn, dh_out, g, sc, "norm_mod_bwd")
    grads = dict(w_q_a=d_w_q_a, q_a_norm_g=d_q_a_g, w_q_b=d_w_q_b, w_o=d_w_o)
    return dh_in, (d_sh, d_sc, d_gate, d_g), grads, dk, dv


def loss_head(h, target, g):
    D = h.shape[1]

    def f(h, t, g):
        xhat, r = _rms(h)
        err = xhat * g - t
        dy = err * (1.0 / D)
        dh = _rms_bwd(xhat, r, dy * g)
        return dh, (0.5 / D) * err * err, dy * xhat
    return rowwise(f, [h, target], [g], [(D, F32)], [D, D], "loss_head")


def _place():
    x, y, c = lax.axis_index("x"), lax.axis_index("y"), lax.axis_index("c")
    chips = [(1 - x, y), (x, 1 - y), (1 - x, 1 - y)]
    return x, y, c, chips


HBM_SPEC = pl.BlockSpec(memory_space=pltpu.HBM)


def all_gather8(v):
    m, n = v.shape

    def body(x_ref, out_ref, send_sems, recv_sems, local_sem):
        x, y, c, chips = _place()
        me, sibling = (x, y, c), (x, y, 1 - c)

        def rows(px, py, pc):
            return out_ref.at[4 * px + 2 * py + pc]

        def copy(k, block, to, src=None):
            return pltpu.make_async_remote_copy(
                src_ref=rows(*block) if src is None else src, dst_ref=rows(*block),
                send_sem=send_sems.at[k], recv_sem=recv_sems.at[k], device_id=to, device_id_type=MESH)

        mine = pltpu.make_async_copy(x_ref, rows(*me), local_sem)
        mine.start()
        first = [copy(0, me, sibling, src=x_ref)]
        first += [copy(1 + j, me, (*chip, c), src=x_ref) for j, chip in enumerate(chips)]
        for cp in first:
            cp.start()
        passed = [copy(4 + j, (*chip, c), sibling) for j, chip in enumerate(chips)]
        for j, chip in enumerate(chips):
            copy(1 + j, (*chip, c), me).wait_recv()
            passed[j].start()
        copy(0, sibling, me).wait_recv()
        for j, chip in enumerate(chips):
            copy(4 + j, (*chip, 1 - c), me).wait_recv()
        for cp in first + passed:
            cp.wait_send()
        mine.wait()

    return pl.pallas_call(
        body, name="all_gather8",
        out_shape=jax.ShapeDtypeStruct((8, m, n), v.dtype),
        in_specs=[pl.BlockSpec(memory_space=pltpu.VMEM)],
        out_specs=pl.BlockSpec(memory_space=pltpu.VMEM),
        scratch_shapes=[pltpu.SemaphoreType.DMA((7,)), pltpu.SemaphoreType.DMA((7,)), pltpu.SemaphoreType.DMA],
        compiler_params=pltpu.CompilerParams(vmem_limit_bytes=VMEM_LIMIT_BYTES),
    )(v)


def gather_weights(bufs):
    n = len(bufs)

    def body(*refs):
        ins, outs = refs[:n], refs[n:2 * n]
        send_sems, recv_sems = refs[2 * n:]
        x, y, c, chips = _place()
        sibling = (x, y, 1 - c)
        me = 2 * x + y

        def idx(chip):
            return 2 * chip[0] + chip[1]

        def copy(w, k, src, dst, to):
            return pltpu.make_async_remote_copy(src_ref=src, dst_ref=dst, send_sem=send_sems.at[6 * w + k],
                                                recv_sem=recv_sems.at[6 * w + k], device_id=to, device_id_type=MESH)

        first = [copy(w, j, ins[w].at[me, c], outs[w].at[me, c], (*chip, c))
                 for w in range(n) for j, chip in enumerate(chips)]
        for cp in first:
            cp.start()
        passed = []
        for w in range(n):
            for j, chip in enumerate(chips):
                landed = outs[w].at[idx(chip), c]
                copy(w, j, landed, landed, (*chip, c)).wait_recv()
                fwd = copy(w, 3 + j, landed, landed, sibling)
                fwd.start()
                passed.append(fwd)
        for w in range(n):
            for j, chip in enumerate(chips):
                other = outs[w].at[idx(chip), 1 - c]
                copy(w, 3 + j, other, other, sibling).wait_recv()
        for cp in first + passed:
            cp.wait_send()

    return pl.pallas_call(
        body, name="gather_weights",
        out_shape=[jax.ShapeDtypeStruct(b.shape, b.dtype) for b in bufs],
        in_specs=[HBM_SPEC] * n, out_specs=[HBM_SPEC] * n,
        input_output_aliases={w: w for w in range(n)},
        scratch_shapes=[pltpu.SemaphoreType.DMA((6 * n,)), pltpu.SemaphoreType.DMA((6 * n,))],
    )(*bufs)


def exchange_halves(gs):
    n = len(gs)

    def body(*refs):
        ins, theirs = refs[:n], refs[n:2 * n]
        send_sems, recv_sems = refs[2 * n:]
        x, y, c, _ = _place()
        sends = [pltpu.make_async_remote_copy(src_ref=ins[w].at[:, 1 - c], dst_ref=theirs[w],
                                              send_sem=send_sems.at[w], recv_sem=recv_sems.at[w],
                                              device_id=(x, y, 1 - c), device_id_type=MESH) for w in range(n)]
        for cp in sends:
            cp.start()
        for cp in sends:
            cp.wait()

    return pl.pallas_call(
        body, name="exchange_halves",
        out_shape=[jax.ShapeDtypeStruct((4,) + g.shape[2:], g.dtype) for g in gs],
        in_specs=[HBM_SPEC] * n, out_specs=[HBM_SPEC] * n,
        scratch_shapes=[pltpu.SemaphoreType.DMA((n,)), pltpu.SemaphoreType.DMA((n,))],
    )(*gs)


def scatter_blocks(ps):
    n = len(ps)

    def body(*refs):
        ins, outs = refs[:n], refs[n:2 * n]
        send_sems, recv_sems = refs[2 * n:]
        x, y, c, chips = _place()
        sends = [pltpu.make_async_remote_copy(src_ref=ins[w].at[2 * chip[0] + chip[1]], dst_ref=outs[w].at[j],
                                              send_sem=send_sems.at[3 * w + j], recv_sem=recv_sems.at[3 * w + j],
                                              device_id=(*chip, c), device_id_type=MESH)
                 for w in range(n) for j, chip in enumerate(chips)]
        for cp in sends:
            cp.start()
        for cp in sends:
            cp.wait()

    return pl.pallas_call(
        body, name="scatter_blocks",
        out_shape=[jax.ShapeDtypeStruct((3,) + p.shape[1:], p.dtype) for p in ps],
        in_specs=[HBM_SPEC] * n, out_specs=[HBM_SPEC] * n,
        scratch_shapes=[pltpu.SemaphoreType.DMA((3 * n,)), pltpu.SemaphoreType.DMA((3 * n,))],
    )(*ps)


def join_halves(qs):
    n = len(qs)

    def body(*refs):
        ins, outs = refs[:n], refs[n:2 * n]
        send_sems, recv_sems = refs[2 * n:]
        x, y, c, _ = _place()
        sends = [pltpu.make_async_remote_copy(src_ref=ins[w].at[c], dst_ref=outs[w].at[c], send_sem=send_sems.at[w],
                                              recv_sem=recv_sems.at[w], device_id=(x, y, 1 - c), device_id_type=MESH)
                 for w in range(n)]
        for cp in sends:
            cp.start()
        for w in range(n):
            other = outs[w].at[1 - c]
            pltpu.make_async_remote_copy(src_ref=other, dst_ref=other, send_sem=send_sems.at[w],
                                         recv_sem=recv_sems.at[w], device_id=(x, y, 1 - c),
                                         device_id_type=MESH).wait_recv()
        for cp in sends:
            cp.wait_send()

    return pl.pallas_call(
        body, name="join_halves",
        out_shape=[jax.ShapeDtypeStruct(q.shape, q.dtype) for q in qs],
        in_specs=[HBM_SPEC] * n, out_specs=[HBM_SPEC] * n,
        input_output_aliases={w: w for w in range(n)},
        scratch_shapes=[pltpu.SemaphoreType.DMA((n,)), pltpu.SemaphoreType.DMA((n,))],
    )(*qs)


def _row_tile(R, row_bytes):
    tm = R
    for t in (512, 256, 128, 64, 32, 16, 8):
        if R % t == 0:
            tm = t
            if t * row_bytes <= ROW_TILE_BUDGET:
                break
    return tm


def sum_siblings(g, theirs, place):
    _, _, R, C = g.shape
    tm = _row_tile(R, 3 * C * 4)

    def body(place_ref, a_ref, b_ref, o_ref):
        o_ref[...] = (a_ref[...] + b_ref[...]).astype(BF16)

    return pl.pallas_call(
        body, name="sum_siblings",
        grid_spec=pltpu.PrefetchScalarGridSpec(
            num_scalar_prefetch=1, grid=(4, R // tm),
            in_specs=[pl.BlockSpec((None, None, tm, C), lambda j, i, s: (j, s[1], i, 0)),
                      pl.BlockSpec((None, tm, C), lambda j, i, s: (j, i, 0))],
            out_specs=pl.BlockSpec((None, tm, C), lambda j, i, s: (j, i, 0))),
        out_shape=jax.ShapeDtypeStruct((4, R, C), BF16),
        compiler_params=_params(("parallel", "parallel")),
    )(place, g, theirs)


def sum_chips(p, landed, place):
    _, R, C = p.shape
    tm = _row_tile(R, 5 * C * 4)

    def body(place_ref, p_ref, l0_ref, l1_ref, l2_ref, o_ref):
        o_ref[...] = ((p_ref[...].astype(F32) + l0_ref[...].astype(F32)) + l1_ref[...].astype(F32)
                      ) + l2_ref[...].astype(F32)

    return pl.pallas_call(
        body, name="sum_chips",
        grid_spec=pltpu.PrefetchScalarGridSpec(
            num_scalar_prefetch=1, grid=(R // tm,),
            in_specs=[pl.BlockSpec((None, tm, C), lambda i, s: (s[0], i, 0))]
            + [pl.BlockSpec((None, tm, C), lambda i, s, j=j: (j, i, 0)) for j in range(3)],
            out_specs=pl.BlockSpec((None, tm, C), lambda i, s: (s[1], i, 0))),
        out_shape=jax.ShapeDtypeStruct((2, R, C), F32),
        compiler_params=_params(("parallel",)),
    )(place, p, landed, landed, landed)


def sum_blocks(items, name):
    R, C = items[0][0].shape[1:]
    tm = R
    for t in (512, 256, 128, 64, 32, 16, 8):
        if R % t == 0:
            tm = t
            if t * C * 4 * (len(items) + 1) <= ROW_TILE_BUDGET:
                break
    n = len(items)

    def body(*refs):
        acc = refs[0][...].astype(F32)
        for r in refs[1:n]:
            acc = acc + r[...].astype(F32)
        refs[n][...] = acc

    return pl.pallas_call(
        body, name=name,
        grid=(R // tm,),
        in_specs=[pl.BlockSpec((None, tm, C), lambda i, j=j: (j, i, 0)) for _, j in items],
        out_specs=pl.BlockSpec((tm, C), lambda i: (i, 0)),
        out_shape=jax.ShapeDtypeStruct((R, C), F32),
        compiler_params=_params(("parallel",)),
    )(*[a for a, _ in items])


def reduce_scatter_grads(gs, place):
    theirs = exchange_halves(gs)
    ps = [sum_siblings(g, t, place) for g, t in zip(gs, theirs)]
    landed = scatter_blocks(ps)
    qs = [sum_chips(p, l, place) for p, l in zip(ps, landed)]
    joined = join_halves(qs)
    return [j.reshape(2 * j.shape[1], j.shape[2]) for j in joined]


def adamw(w, g, m, v):
    shape = w.shape
    C = shape[-1]
    R = w.size // C
    tm = R
    for t in (512, 256, 128, 64, 32, 16, 8):
        if R % t == 0:
            tm = t
            if t * C * 4 * 7 <= ROW_TILE_BUDGET:
                break

    def f(w, g, m, v):
        m = ADAM_B1 * m + (1.0 - ADAM_B1) * g
        v = ADAM_B2 * v + (1.0 - ADAM_B2) * (g * g)
        m_hat = m / (1.0 - ADAM_B1 ** ADAM_STEP)
        v_hat = v / (1.0 - ADAM_B2 ** ADAM_STEP)
        delta = -ADAM_LR * (m_hat / (jnp.sqrt(v_hat) + ADAM_EPS) + ADAM_WD * w)
        return delta, m, v

    d, nm, nv = rowwise(f, [a.reshape(R, C) for a in (w, g, m, v)], [], [(C, F32)] * 3, [], "adamw", tm=tm)
    return d.reshape(shape), nm.reshape(shape), nv.reshape(shape)


def _cast_into_slot(w, place):
    C = w.shape[-1]
    w2 = w.reshape(-1, C)
    R = w2.shape[0]
    tm = _row_tile(R, 6 * C)

    def body(place_ref, w_ref, o_ref):
        o_ref[...] = w_ref[...].astype(BF16)

    out = pl.pallas_call(
        body, name="cast_bf16",
        grid_spec=pltpu.PrefetchScalarGridSpec(
            num_scalar_prefetch=1, grid=(R // tm,),
            in_specs=[pl.BlockSpec((tm, C), lambda i, s: (i, 0))],
            out_specs=pl.BlockSpec((None, tm, C), lambda i, s: (s[0], i, 0))),
        out_shape=jax.ShapeDtypeStruct((4, R, C), BF16),
        compiler_params=_params(("parallel",)),
    )(place, w2)
    return out.reshape(4, 2, R // 2, C)


def _pack(vs):
    flat = jnp.concatenate([v.reshape(-1) for v in vs])
    n = flat.shape[0]
    total = -(-n // 1024) * 1024
    return jnp.pad(flat, (0, total - n)).reshape(total // 128, 128)


def _unpack(flat, like):
    out, o = [], 0
    for shp in like:
        sz = 1
        for d in shp:
            sz *= d
        out.append(flat[o:o + sz].reshape(shp))
        o += sz
    return out


def _cols_to_blocks(g, n_chips=4):
    R, N = g.shape
    C = N // n_chips
    return g.reshape(R, n_chips, C).transpose(1, 0, 2).reshape(n_chips, 2, R // 2, C)


def _rows_to_blocks(g, n_chips=4):
    R, C = g.shape
    return g.reshape(n_chips, 2, R // n_chips // 2, C)


def kernel(x, c, positions, ada_w, ada_b, norm_g, ffn_w13, ffn_w2, conv_w_pw1, conv_b_pw1, conv_w_dw, conv_b_dw, conv_ln_g, conv_ln_b, conv_w_pw2, conv_b_pw2, kv_ada_w, kv_ada_b, kv_norm_g, w_kv_a, kv_a_norm_g, w_kv_b, w_q_a, q_a_norm_g, w_q_b, w_o, final_norm_g, loss_target, m_ada_w, m_ada_b, m_norm_g, m_ffn_w13, m_ffn_w2, m_conv_w_pw1, m_conv_b_pw1, m_conv_w_dw, m_conv_b_dw, m_conv_ln_g, m_conv_ln_b, m_conv_w_pw2, m_conv_b_pw2, m_kv_ada_w, m_kv_ada_b, m_kv_norm_g, m_w_kv_a, m_kv_a_norm_g, m_w_kv_b, m_w_q_a, m_q_a_norm_g, m_w_q_b, m_w_o, m_final_norm_g, v_ada_w, v_ada_b, v_norm_g, v_ffn_w13, v_ffn_w2, v_conv_w_pw1, v_conv_b_pw1, v_conv_w_dw, v_conv_b_dw, v_conv_ln_g, v_conv_ln_b, v_conv_w_pw2, v_conv_b_pw2, v_kv_ada_w, v_kv_ada_b, v_kv_norm_g, v_w_kv_a, v_kv_a_norm_g, v_w_kv_b, v_w_q_a, v_q_a_norm_g, v_w_q_b, v_w_o, v_final_norm_g):
    S, D = x.shape[1], x.shape[2]
    H = N_HEADS
    F = ffn_w2.shape[2] * 4
    xi, yi, ci = lax.axis_index("x"), lax.axis_index("y"), lax.axis_index("c")
    chip = 2 * xi + yi
    dev = 2 * chip + ci
    place = jnp.stack([chip, ci]).astype(jnp.int32)
    h0 = x[0]
    target = loss_target[0]

    silu_c = rowwise(lambda a: a * _sigmoid(a), [c], [], [(D, F32)], [], "silu_c")[0]
    silu_all = all_gather8(silu_c.reshape(8, D // 8)).reshape(8, D)
    n_ada = ada_w.shape[2]
    n_kv = kv_ada_w.shape[1]
    ada_b_mine = lax.dynamic_slice_in_dim(ada_b, chip * n_ada, n_ada, axis=1)
    kv_b_mine = lax.dynamic_slice_in_dim(kv_ada_b, chip * n_kv, n_kv, axis=0)[None, :]
    mods = [mm(silu_all, ada_w[l], "nn", "ada_rows", bias=ada_b_mine[l:l + 1]) for l in range(2)]
    mods.append(mm(silu_all, kv_ada_w, "nn", "kv_ada_rows", bias=kv_b_mine))
    n_mod_cols = 2 * n_ada + n_kv
    mod_pack = jnp.concatenate(mods, axis=1).reshape(-1, 128)
    mod_all = all_gather8(mod_pack).reshape(8, 8, n_mod_cols)[0::2]
    mod_mine = lax.dynamic_index_in_dim(mod_all, dev, axis=1, keepdims=False)
    mod = [mod_mine[:, l * n_ada:(l + 1) * n_ada].reshape(N_MOD, D) for l in range(2)]
    kv_mod = mod_mine[:, 2 * n_ada:].reshape(2, D)
    kv_shift, kv_scale = kv_mod[0:1], kv_mod[1:2]

    def mrow(l, k):
        return mod[l][k:k + 1]

    big = dict(ffn_w13=ffn_w13, ffn_w2=ffn_w2, conv_w_pw1=conv_w_pw1, conv_w_pw2=conv_w_pw2, w_kv_a=w_kv_a,
               w_kv_b=w_kv_b, w_q_a=w_q_a, w_q_b=w_q_b, w_o=w_o)
    names = list(big)
    gathered = gather_weights([_cast_into_slot(big[k], place) for k in names])
    gw = dict(zip(names, gathered))
    small_like = [norm_g.shape, conv_b_pw1.shape, conv_w_dw.shape, conv_b_dw.shape, conv_ln_g.shape,
                  conv_ln_b.shape, conv_b_pw2.shape]
    small_pack = _pack([norm_g, conv_b_pw1, conv_w_dw, conv_b_dw, conv_ln_g, conv_ln_b, conv_b_pw2])
    small_all = all_gather8(small_pack)[0::2].reshape(4, -1)
    per_chip = [_unpack(small_all[j], small_like) for j in range(4)]
    smalls = [jnp.concatenate([per_chip[j][k] for j in range(4)], axis=-1) for k in range(len(small_like))]
    norm_g_f, b_pw1_f, w_dw_f, b_dw_f, ln_g_f, ln_b_f, b_pw2_f = smalls

    gw13 = gw["ffn_w13"].reshape(4, 2, 2, D, F // 2)
    w2 = gw["ffn_w2"].reshape(4, 2, 2, F // 4, D).transpose(1, 2, 0, 3, 4).reshape(2, 2, F, D)
    conv_p = dict(
        w_pw1=gw["conv_w_pw1"].reshape(4, D, 2 * D // 4).transpose(1, 0, 2).reshape(D, 2 * D),
        b_pw1=b_pw1_f, w_dw=w_dw_f[0], b_dw=b_dw_f, ln_g=ln_g_f, ln_b=ln_b_f,
        w_pw2=gw["conv_w_pw2"].reshape(D, D), b_pw2=b_pw2_f)
    q_lora = w_q_a.shape[2]
    mla_p = dict(
        w_kv_a=gw["w_kv_a"].reshape(D, KV_LORA + QK_ROPE),
        w_kv_b=gw["w_kv_b"].reshape(4, KV_LORA, -1).transpose(1, 0, 2).reshape(KV_LORA, -1),
        w_q_a=gw["w_q_a"].reshape(D, q_lora),
        w_q_b=gw["w_q_b"].reshape(4, q_lora, -1).transpose(1, 0, 2).reshape(q_lora, -1),
        w_o=gw["w_o"].reshape(H * V_HEAD, D))
    pw = mla_pad_weights(mla_p)
    tabs = rope_tables(positions[0])

    def ng(l, k):
        return norm_g_f[l, k][None, :]

    h = h0
    h, s_f1_0 = ffn_fwd(h, ng(0, 0), mrow(0, 0), mrow(0, 1), mrow(0, 2), gw13, 0, 0, w2[0, 0])
    h, s_conv = conv_module_fwd(h, ng(0, 1), mrow(0, 3), mrow(0, 4), mrow(0, 5), conv_p)
    h, s_f2_0 = ffn_fwd(h, ng(0, 2), mrow(0, 6), mrow(0, 7), mrow(0, 8), gw13, 0, 1, w2[0, 1])
    kv_norm = kv_norm_g[None, :]
    kv_a_g = kv_a_norm_g[None, :]
    kv, kpe_r, s_kv = mla_kv_fwd(h, kv_norm, kv_shift, kv_scale, kv_a_g, pw, tabs)
    h, s_f1_1 = ffn_fwd(h, ng(1, 0), mrow(1, 0), mrow(1, 1), mrow(1, 2), gw13, 1, 0, w2[1, 0])
    h, s_mla = mla_fwd(h, ng(1, 1), mrow(1, 3), mrow(1, 4), mrow(1, 5), q_a_norm_g, pw, kv, kpe_r, tabs)
    h, s_f2_1 = ffn_fwd(h, ng(1, 2), mrow(1, 6), mrow(1, 7), mrow(1, 8), gw13, 1, 1, w2[1, 1])
    dh, loss_cols, d_final_g = loss_head(h, target, final_norm_g[None, :])

    dh, v_f2_1, dw13_11, dw2_11 = ffn_bwd(dh, s_f2_1, ng(1, 2), mrow(1, 7), mrow(1, 8), gw13, 1, 1, w2[1, 1])
    dh, v_mla, g_mla, dk, dv = mla_bwd(dh, s_mla, ng(1, 1), mrow(1, 4), mrow(1, 5), q_a_norm_g, pw, kv, kpe_r, tabs)
    dh, v_f1_1, dw13_10, dw2_10 = ffn_bwd(dh, s_f1_1, ng(1, 0), mrow(1, 1), mrow(1, 2), gw13, 1, 0, w2[1, 0])
    dh, v_kv, d_kv_a_g, d_w_kv_a, d_w_kv_b = mla_kv_bwd(dh, dk, dv, s_kv, kv_norm, kv_scale, kv_a_g, pw, tabs)
    dh, v_f2_0, dw13_01, dw2_01 = ffn_bwd(dh, s_f2_0, ng(0, 2), mrow(0, 7), mrow(0, 8), gw13, 0, 1, w2[0, 1])
    dh, v_conv, g_conv = conv_module_bwd(dh, s_conv, ng(0, 1), mrow(0, 4), mrow(0, 5), conv_p)
    dh, v_f1_0, dw13_00, dw2_00 = ffn_bwd(dh, s_f1_0, ng(0, 0), mrow(0, 1), mrow(0, 2), gw13, 0, 0, w2[0, 0])
    grad_x = dh[None]

    d_w_kv_a_u = jnp.concatenate([d_w_kv_a[:, :KV_LORA], d_w_kv_a[:, KV_LORA + QK_NOPE:KV_LORA + QK_NOPE + QK_ROPE]],
                                 axis=1)
    hk = H * HEAD_PAD
    dkb = jnp.concatenate([d_w_kv_b[:, :hk].reshape(KV_LORA, H, HEAD_PAD)[:, :, :QK_NOPE],
                           d_w_kv_b[:, hk:].reshape(KV_LORA, H, HEAD_PAD)[:, :, :V_HEAD]], axis=2).reshape(KV_LORA, -1)
    d_w_q_b_u = _unpad_heads(g_mla["w_q_b"], QK_NOPE + QK_ROPE)
    d_w_o_u = g_mla["w_o"].reshape(H, HEAD_PAD, D)[:, :V_HEAD].reshape(H * V_HEAD, D)
    full = [dw.reshape(4, 2, D // 2, F // 2) for dw in (dw13_00, dw13_01, dw13_10, dw13_11)] + [
            _rows_to_blocks(dw2_00), _rows_to_blocks(dw2_01), _rows_to_blocks(dw2_10), _rows_to_blocks(dw2_11),
            _cols_to_blocks(g_conv["w_pw1"]), _rows_to_blocks(g_conv["w_pw2"]), _rows_to_blocks(d_w_kv_a_u),
            _cols_to_blocks(dkb), _rows_to_blocks(g_mla["w_q_a"]), _cols_to_blocks(d_w_q_b_u),
            _rows_to_blocks(d_w_o_u)]
    red = reduce_scatter_grads(full, place)
    g_ffn_w13 = jnp.stack(red[0:4]).reshape(ffn_w13.shape)
    g_ffn_w2 = jnp.stack(red[4:8]).reshape(ffn_w2.shape)
    g_conv_w_pw1 = red[8].reshape(conv_w_pw1.shape)
    g_conv_w_pw2 = red[9].reshape(conv_w_pw2.shape)
    g_w_kv_a = red[10].reshape(w_kv_a.shape)
    g_w_kv_b = red[11].reshape(w_kv_b.shape)
    g_w_q_a = red[12].reshape(w_q_a.shape)
    g_w_q_b = red[13].reshape(w_q_b.shape)
    g_w_o = red[14].reshape(w_o.shape)

    def dmod(v1, vm, v2):
        return jnp.concatenate([v1[0], v1[1], v1[2], vm[0], vm[1], vm[2], v2[0], v2[1], v2[2]], axis=1)
    d_mod0 = dmod(v_f1_0, v_conv, v_f2_0)
    d_mod1 = dmod(v_f1_1, v_mla, v_f2_1)
    d_kv_mod = jnp.concatenate([v_kv[0], v_kv[1]], axis=1)
    d_norm_g = jnp.concatenate([v_f1_0[3], v_conv[3], v_f2_0[3], v_f1_1[3], v_mla[3], v_f2_1[3]], axis=0)
    vec_list = [d_mod0, d_mod1, d_kv_mod, d_norm_g, g_conv["b_pw1"], g_conv["w_dw"], g_conv["b_dw"], g_conv["ln_g"],
                g_conv["ln_b"], g_conv["b_pw2"], v_kv[2], d_kv_a_g, g_mla["q_a_norm_g"], d_final_g, loss_cols]
    vec_like = [v.shape for v in vec_list]
    vec_pack = _pack(vec_list)
    n_mod_rows = (2 * N_MOD * D + 2 * D) // 128
    vec_all = all_gather8(vec_pack)
    vec_sum = sum_blocks([(vec_all, d) for d in range(8)], "sum_devices").reshape(-1)
    (_, _, _, s_norm_g, s_b_pw1, s_w_dw, s_b_dw, s_ln_g, s_ln_b, s_b_pw2, s_kv_norm_g, s_kv_a_g, s_q_a_g,
     s_final_g, s_loss) = _unpack(vec_sum, vec_like)
    loss = jnp.sum(s_loss)
    dmod_all = vec_all[:, :n_mod_rows].reshape(8, 2 * N_MOD * D + 2 * D)
    dmod_sum = vec_sum[:2 * N_MOD * D + 2 * D]
    g_ada_b = dmod_sum[:2 * N_MOD * D].reshape(2, N_MOD * D)
    g_kv_ada_b = dmod_sum[2 * N_MOD * D:]
    g_ada_w = []
    for l in range(2):
        cols = lax.dynamic_slice_in_dim(dmod_all[:, l * N_MOD * D:(l + 1) * N_MOD * D], chip * n_ada, n_ada, axis=1)
        g_ada_w.append(mm(silu_all, cols, "tn", "ada_w_grad"))
    g_ada_w = jnp.stack(g_ada_w)
    kv_cols = lax.dynamic_slice_in_dim(dmod_all[:, 2 * N_MOD * D:], chip * n_kv, n_kv, axis=1)
    g_kv_ada_w = mm(silu_all, kv_cols, "tn", "kv_ada_w_grad")

    def shard(v, width):
        return lax.dynamic_slice_in_dim(v, chip * width, width, axis=v.ndim - 1)

    Dq = D // 4
    g_norm_g = shard(s_norm_g.reshape(2, 3, D), Dq)
    g_conv_b_pw1 = shard(s_b_pw1, 2 * D // 4)
    g_conv_w_dw = shard(s_w_dw, Dq)[None]
    g_conv_b_dw = shard(s_b_dw, Dq)
    g_conv_ln_g = shard(s_ln_g, Dq)
    g_conv_ln_b = shard(s_ln_b, Dq)
    g_conv_b_pw2 = shard(s_b_pw2, Dq)

    grads = [g_ada_w, g_ada_b, g_norm_g, g_ffn_w13, g_ffn_w2, g_conv_w_pw1, g_conv_b_pw1, g_conv_w_dw, g_conv_b_dw,
             g_conv_ln_g, g_conv_ln_b, g_conv_w_pw2, g_conv_b_pw2, g_kv_ada_w, g_kv_ada_b, s_kv_norm_g[0], g_w_kv_a,
             s_kv_a_g[0], g_w_kv_b, g_w_q_a, s_q_a_g, g_w_q_b, g_w_o, s_final_g[0]]
    weights = [ada_w, ada_b, norm_g, ffn_w13, ffn_w2, conv_w_pw1, conv_b_pw1, conv_w_dw, conv_b_dw, conv_ln_g,
               conv_ln_b, conv_w_pw2, conv_b_pw2, kv_ada_w, kv_ada_b, kv_norm_g, w_kv_a, kv_a_norm_g, w_kv_b, w_q_a,
               q_a_norm_g, w_q_b, w_o, final_norm_g]
    ms = [m_ada_w, m_ada_b, m_norm_g, m_ffn_w13, m_ffn_w2, m_conv_w_pw1, m_conv_b_pw1, m_conv_w_dw, m_conv_b_dw,
          m_conv_ln_g, m_conv_ln_b, m_conv_w_pw2, m_conv_b_pw2, m_kv_ada_w, m_kv_ada_b, m_kv_norm_g, m_w_kv_a,
          m_kv_a_norm_g, m_w_kv_b, m_w_q_a, m_q_a_norm_g, m_w_q_b, m_w_o, m_final_norm_g]
    vs = [v_ada_w, v_ada_b, v_norm_g, v_ffn_w13, v_ffn_w2, v_conv_w_pw1, v_conv_b_pw1, v_conv_w_dw, v_conv_b_dw,
          v_conv_ln_g, v_conv_ln_b, v_conv_w_pw2, v_conv_b_pw2, v_kv_ada_w, v_kv_ada_b, v_kv_norm_g, v_w_kv_a,
          v_kv_a_norm_g, v_w_kv_b, v_w_q_a, v_q_a_norm_g, v_w_q_b, v_w_o, v_final_norm_g]
    grads = [g.reshape(w.shape) for g, w in zip(grads, weights)]
    deltas, new_m, new_v = [], [], []
    for w, g, m, v in zip(weights, grads, ms, vs):
        d, nm, nv = adamw(w, g, m, v)
        deltas.append(d)
        new_m.append(nm)
        new_v.append(nv)
    return (loss, grad_x, *grads, *deltas, *new_m, *new_v)
```

```python
import jax
import jax.numpy as jnp
from jax import lax
from jax.experimental import pallas as pl
from jax.experimental.pallas import tpu as pltpu

F32 = jnp.float32
BF16 = jnp.bfloat16
MESH = pl.DeviceIdType.MESH

N_HEADS = 16
QK_NOPE = 64
QK_ROPE = 32
V_HEAD = 64
KV_LORA = 256
CONV_WIDTH = 31
ROPE_THETA = 10000.0
EPS = 1e-6
N_MOD = 9
HEAD_PAD = 128
ATTN_TILE = 512
CONV_HALO = 32

ADAM_LR = 0.001
ADAM_B1 = 0.9
ADAM_B2 = 0.999
ADAM_EPS = 1e-08
ADAM_WD = 0.01
ADAM_STEP = 10

VMEM_LIMIT_BYTES = 56 * 2 ** 20
ROW_TILE_BUDGET = 10 * 2 ** 20
MM_VMEM_BUDGET = 40 * 2 ** 20
NEG = float(jnp.finfo(jnp.float32).min)
LOG2_E = 1.4426950408889634


def _tile(n, prefs):
    for t in prefs:
        if n % t == 0:
            return t
    return n


def _params(sem):
    return pltpu.CompilerParams(dimension_semantics=sem, vmem_limit_bytes=VMEM_LIMIT_BYTES)


def _mm_tiles(M, N, K, mode, a_bytes, b_bytes, o_bytes):
    if mode == "tn":
        tk_opts = [t for t in (2048, 1024, 512, 256, 128) if K % t == 0] or [K]
        tm_opts = ([M] if M <= 2816 else []) + [t for t in (1024, 512, 256, 128) if M % t == 0 and t < M]
    else:
        tk_opts = [K]
        tm_opts = [t for t in (1024, 512, 256, 128) if M % t == 0] or [M]
    tn_opts = [t for t in (1408, 1024, 512, 384, 256, 128) if N % t == 0] or [N]

    def need(tm, tn, tk):
        blocks = 2 * (tm * tk * a_bytes + tk * tn * b_bytes + tm * tn * o_bytes)
        return blocks + (tm * tn * 4 if mode == "tn" else 0)

    tk_floor = next((t for t in tk_opts if t <= 512), tk_opts[-1])
    for tm in tm_opts:
        for tn in tn_opts:
            if need(tm, tn, tk_floor) <= MM_VMEM_BUDGET:
                return tm, tn, next(tk for tk in tk_opts if need(tm, tn, tk) <= MM_VMEM_BUDGET)
    return tm_opts[-1], tn_opts[-1], tk_opts[-1]


def mm(a, b, mode, name, out_dtype=F32, bias=None):
    if mode == "nn":
        (M, K), (K2, N) = a.shape, b.shape
        dims = (((1,), (0,)), ((), ()))
    elif mode == "nt":
        (M, K), (N, K2) = a.shape, b.shape
        dims = (((1,), (1,)), ((), ()))
    else:
        (K, M), (K2, N) = a.shape, b.shape
        dims = (((0,), (0,)), ((), ()))
    assert K == K2, (a.shape, b.shape, mode)
    tm, tn, tk = _mm_tiles(M, N, K, mode, a.dtype.itemsize, b.dtype.itemsize, jnp.dtype(out_dtype).itemsize)
    nk = K // tk
    if mode == "tn":
        a_spec = pl.BlockSpec((tk, tm), lambda i, j, k: (k, i))
        b_spec = pl.BlockSpec((tk, tn), lambda i, j, k: (k, j))
    elif mode == "nn":
        a_spec = pl.BlockSpec((tm, tk), lambda i, j, k: (i, k))
        b_spec = pl.BlockSpec((tk, tn), lambda i, j, k: (k, j))
    else:
        a_spec = pl.BlockSpec((tm, tk), lambda i, j, k: (i, k))
        b_spec = pl.BlockSpec((tn, tk), lambda i, j, k: (j, k))
    in_specs = [a_spec, b_spec]
    operands = [a, b]
    if bias is not None:
        in_specs.append(pl.BlockSpec((1, tn), lambda i, j, k: (0, j)))
        operands.append(bias)
    has_bias = bias is not None

    def body(*refs):
        a_ref, b_ref = refs[0], refs[1]
        bias_ref = refs[2] if has_bias else None
        o_ref = refs[3] if has_bias else refs[2]
        prod = lax.dot_general(a_ref[...].astype(BF16), b_ref[...].astype(BF16), dims,
                               preferred_element_type=F32)
        if nk == 1:
            if has_bias:
                prod = prod + bias_ref[...]
            o_ref[...] = prod.astype(o_ref.dtype)
        else:
            acc_ref = refs[-1]
            k = pl.program_id(2)

            @pl.when(k == 0)
            def _():
                acc_ref[...] = jnp.zeros_like(acc_ref)

            acc_ref[...] += prod

            @pl.when(k == nk - 1)
            def _():
                out = acc_ref[...]
                if has_bias:
                    out = out + bias_ref[...]
                o_ref[...] = out.astype(o_ref.dtype)

    return pl.pallas_call(
        body, name=name,
        grid=(M // tm, N // tn, nk),
        in_specs=in_specs,
        out_specs=pl.BlockSpec((tm, tn), lambda i, j, k: (i, j)),
        out_shape=jax.ShapeDtypeStruct((M, N), out_dtype),
        scratch_shapes=[pltpu.VMEM((tm, tn), F32)] if nk > 1 else [],
        compiler_params=_params(("parallel", "parallel", "arbitrary")),
    )(*operands)


def mm_fused(a, b, mode, name, tn, epi, epi_outs, pro=None, pro_rows=(), pro_vecs=(), pro_out=False, n_pro_sums=0,
             epi_rows=(), epi_vecs=(), b_blocks=None, n_cols=None):
    M, K = a.shape
    if b_blocks is not None:
        n_b, N = len(b_blocks), n_cols
    else:
        n_b = b.shape[0] if b.ndim == 3 else 1
        N = b.shape[-1] if mode == "nn" else b.shape[0]
    dims = (((1,), (0,)), ((), ())) if mode == "nn" else (((1,), (1,)), ((), ()))
    nj = N // tn
    epi_outs = [o if len(o) == 3 else (*o, None) for o in epi_outs]
    row_bytes = 2 * (K * a.dtype.itemsize + sum(K * r.dtype.itemsize for r in pro_rows) + (2 * K if pro_out else 0)
                     + sum(w * r.dtype.itemsize * (r.shape[0] if r.ndim == 3 else 1) for r, w in epi_rows)
                     + sum(w * jnp.dtype(dt).itemsize * (L or 1) for w, dt, L in epi_outs)
                     ) + (2 * K if pro is not None else 0)
    fixed = 2 * n_b * K * tn * b.dtype.itemsize
    tm = next((t for t in (1024, 512, 256, 128) if M % t == 0 and t * row_bytes + fixed <= MM_VMEM_BUDGET), M)
    row = lambda i, j: (i, 0)
    tile = lambda i, j: (i, j)
    stack = lambda i, j: (0, i, j)
    in_specs = [pl.BlockSpec((tm, K), row)] + [pl.BlockSpec((tm, K), row) for _ in pro_rows]
    in_specs += [pl.BlockSpec(v.shape, lambda i, j: (0, 0)) for v in pro_vecs]
    if b_blocks is not None:
        in_specs += [pl.BlockSpec(shape, imap) for shape, imap in b_blocks]
    elif b.ndim == 3:
        in_specs += [pl.BlockSpec((None, K, tn), lambda i, j, h=h: (h, 0, j)) for h in range(n_b)]
    elif mode == "nn":
        in_specs += [pl.BlockSpec((K, tn), lambda i, j: (0, j))]
    else:
        in_specs += [pl.BlockSpec((tn, K), lambda i, j: (j, 0))]
    in_specs += [pl.BlockSpec((r.shape[0], tm, w), stack) if r.ndim == 3 else pl.BlockSpec((tm, w), tile)
                 for r, w in epi_rows]
    in_specs += [pl.BlockSpec((1, tn), lambda i, j: (0, j)) for _ in epi_vecs]
    out_specs, out_shape = [], []
    if pro_out:
        out_specs.append(pl.BlockSpec((tm, K), row))
        out_shape.append(jax.ShapeDtypeStruct((M, K), BF16))
    for _ in range(n_pro_sums):
        out_specs.append(pl.BlockSpec((1, K), lambda i, j: (0, 0)))
        out_shape.append(jax.ShapeDtypeStruct((1, K), F32))
    for w, dt, L in epi_outs:
        out_specs.append(pl.BlockSpec((tm, w), tile) if L is None else pl.BlockSpec((L, tm, w), stack))
        out_shape.append(jax.ShapeDtypeStruct((M, nj * w) if L is None else (L, M, nj * w), dt))
    n_pr, n_pv, n_er, n_ev = len(pro_rows), len(pro_vecs), len(epi_rows), len(epi_vecs)
    n_a = 1 + n_pr + n_pv
    n_in = n_a + n_b + n_er + n_ev
    n_po = 1 if pro_out else 0

    def body(*refs):
        i, j = pl.program_id(0), pl.program_id(1)
        a_ref = refs[0]
        outs = refs[n_in:]
        if pro is not None:
            lhs_ref = refs[-1]

            @pl.when(j == 0)
            def _():
                res = pro(*[r[...] for r in refs[:1 + n_pr + n_pv]])
                if not isinstance(res, (tuple, list)):
                    res = (res,)
                lhs_ref[...] = res[0]
                if pro_out:
                    outs[0][...] = res[0]
                for s_ref, val in zip(outs[n_po:n_po + n_pro_sums], res[1:]):
                    part = jnp.sum(val.astype(F32), axis=0, keepdims=True)

                    @pl.when(i == 0)
                    def _(s_ref=s_ref, part=part):
                        s_ref[...] = part

                    @pl.when(i != 0)
                    def _(s_ref=s_ref, part=part):
                        s_ref[...] += part

            lhs = lhs_ref[...]
        else:
            lhs = a_ref[...].astype(BF16)
        accs = [lax.dot_general(lhs, b_ref[...].astype(BF16), dims, preferred_element_type=F32)
                for b_ref in refs[n_a:n_a + n_b]]
        res = epi(*accs, *[r[...] for r in refs[n_a + n_b:n_in]])
        if not isinstance(res, (tuple, list)):
            res = (res,)
        for o_ref, val in zip(outs[n_po + n_pro_sums:], res):
            if isinstance(val, (tuple, list)):
                for h, part in enumerate(val):
                    o_ref[h] = part.astype(o_ref.dtype)
            else:
                o_ref[...] = val.astype(o_ref.dtype)

    return pl.pallas_call(
        body, name=name,
        grid=(M // tm, nj),
        in_specs=in_specs, out_specs=out_specs, out_shape=out_shape,
        scratch_shapes=[pltpu.VMEM((tm, K), BF16)] if pro is not None else [],
        compiler_params=_params(("arbitrary", "arbitrary")),
    )(a, *pro_rows, *pro_vecs, *([b] * n_b), *[r for r, _ in epi_rows], *epi_vecs)


def rowwise(fn, rows, vecs, outs, sums, name, tm=None):
    norm = [(r, r.shape[1], 0) if not isinstance(r, tuple) else r for r in rows]
    S = norm[0][0].shape[0]
    if tm is None:
        per_row = sum(w * r.dtype.itemsize for r, w, _ in norm) + sum(n * jnp.dtype(dt).itemsize for n, dt in outs)
        tm = S
        for t in (512, 256, 128, 64, 32, 16, 8):
            if S % t == 0:
                tm = t
                if t * per_row <= ROW_TILE_BUDGET:
                    break
    n_rows, n_vecs, n_outs, n_sums = len(norm), len(vecs), len(outs), len(sums)
    in_specs = [pl.BlockSpec((tm, w), lambda i, cb=cb: (i, cb)) for _, w, cb in norm]
    in_specs += [pl.BlockSpec(v.shape, lambda i: (0, 0)) for v in vecs]
    out_specs = [pl.BlockSpec((tm, n), lambda i: (i, 0)) for n, _ in outs]
    out_specs += [pl.BlockSpec((1, n), lambda i: (0, 0)) for n in sums]
    out_shape = [jax.ShapeDtypeStruct((S, n), dt) for n, dt in outs]
    out_shape += [jax.ShapeDtypeStruct((1, n), F32) for n in sums]

    def body(*refs):
        ins = [r[...] for r in refs[:n_rows + n_vecs]]
        res = fn(*ins)
        if not isinstance(res, (tuple, list)):
            res = (res,)
        out_refs = refs[n_rows + n_vecs:]
        for o_ref, val in zip(out_refs[:n_outs], res[:n_outs]):
            o_ref[...] = val.astype(o_ref.dtype)
        if n_sums:
            i = pl.program_id(0)
            for s_ref, val in zip(out_refs[n_outs:], res[n_outs:]):
                part = jnp.sum(val.astype(F32), axis=0, keepdims=True)

                @pl.when(i == 0)
                def _(s_ref=s_ref, part=part):
                    s_ref[...] = part

                @pl.when(i != 0)
                def _(s_ref=s_ref, part=part):
                    s_ref[...] += part

    res = pl.pallas_call(
        body, name=name,
        grid=(S // tm,),
        in_specs=in_specs, out_specs=out_specs, out_shape=out_shape,
        compiler_params=_params(("arbitrary",) if n_sums else ("parallel",)),
    )(*[r for r, _, _ in norm], *vecs)
    return res


def _sigmoid(x):
    return jax.nn.sigmoid(x)


def _rms(x):
    r = lax.rsqrt(jnp.mean(x * x, axis=-1, keepdims=True) + EPS)
    return x * r, r


def _rms_bwd(xhat, r, dxhat):
    return r * (dxhat - xhat * jnp.mean(dxhat * xhat, axis=-1, keepdims=True))


def norm_mod(h, g, sh, sc, name):
    def f(h, g, sh, sc):
        xhat, _ = _rms(h)
        return ((xhat * g) * (1 + sc) + sh).astype(BF16)
    return rowwise(f, [h], [g, sh, sc], [(h.shape[1], BF16)], [], name)[0]


def norm_mod_bwd(h, dhn, dh_out, g, sc, name):
    D = h.shape[1]
    with_res = dh_out is not None

    def f(*a):
        if with_res:
            h, dhn, dres, g, sc = a
        else:
            h, dhn, g, sc = a
        xhat, r = _rms(h)
        xn = xhat * g
        dxn = dhn * (1 + sc)
        dh = _rms_bwd(xhat, r, dxn * g)
        if with_res:
            dh = dh + dres
        return dh, dhn, dhn * xn, dxn * xhat

    rows = [h, dhn] + ([dh_out] if with_res else [])
    return rowwise(f, rows, [g, sc], [(D, F32)], [D, D, D], name)


def residual(h, y, gate, coef, name, bias=None):
    D = h.shape[1]
    if bias is None:
        def f(h, y, gate):
            return h + (coef * gate) * y
        return rowwise(f, [h, y], [gate], [(D, F32)], [], name)[0], y

    def fb(h, y, gate, bias):
        yb = y + bias
        return h + (coef * gate) * yb, yb
    return rowwise(fb, [h, y], [gate, bias], [(D, F32), (D, F32)], [], name)


def residual_bwd(dh_out, y, gate, coef, name, with_bias_sum=False):
    D = y.shape[1]

    def f(dh, y, gate):
        dy = (coef * gate) * dh
        res = (dy.astype(BF16), coef * dh * y)
        return res + ((dy,) if with_bias_sum else ())
    return rowwise(f, [dh_out, y], [gate], [(D, BF16)], [D, D] if with_bias_sum else [D], name)


def ffn_w13_dx(dab, gw13, l, i):
    _, S, F = dab.shape
    D, C = gw13.shape[3:]
    tm = _tile(S, (1024, 512, 256, 128))
    nt = (((1,), (1,)), ((), ()))

    def body(a_ref, b_ref, o_ref, acc_ref):
        k = pl.program_id(1)
        prod = lax.dot_general(a_ref[...], b_ref[...], nt, preferred_element_type=F32)

        @pl.when(k == 0)
        def _():
            acc_ref[...] = prod

        @pl.when((k > 0) & (k < 3))
        def _():
            acc_ref[...] += prod

        @pl.when(k == 3)
        def _():
            o_ref[...] = acc_ref[...] + prod

    return pl.pallas_call(
        body, name="ffn_w13_dx",
        grid=(S // tm, 4),
        in_specs=[pl.BlockSpec((None, tm, C), lambda r, k: (k // 2, r, k % 2)),
                  pl.BlockSpec((None, None, None, D, C), lambda r, k: (k, l, i, 0, 0))],
        out_specs=pl.BlockSpec((tm, D), lambda r, k: (r, 0)),
        out_shape=jax.ShapeDtypeStruct((S, D), F32),
        scratch_shapes=[pltpu.VMEM((tm, D), F32)],
        compiler_params=_params(("parallel", "arbitrary")),
    )(dab, gw13)


def ffn_w13_grad(hn, dab):
    S, D = hn.shape
    F = dab.shape[2]
    C = F // 2
    tk = next(t for t in (2048, 1024, 512, 256, 128) if S % t == 0)
    tn_dims = (((0,), (0,)), ((), ()))
    nk = S // tk

    def body(a_ref, b_ref, o_ref, acc_ref):
        k = pl.program_id(1)

        @pl.when(k == 0)
        def _():
            acc_ref[...] = jnp.zeros_like(acc_ref)

        acc_ref[...] += lax.dot_general(a_ref[...], b_ref[...], tn_dims, preferred_element_type=F32)

        @pl.when(k == nk - 1)
        def _():
            o_ref[...] = acc_ref[...]

    return pl.pallas_call(
        body, name="ffn_w13_dw",
        grid=(4, nk),
        in_specs=[pl.BlockSpec((tk, D), lambda j, k: (k, 0)),
                  pl.BlockSpec((None, tk, C), lambda j, k: (j // 2, k, j % 2))],
        out_specs=pl.BlockSpec((None, D, C), lambda j, k: (j, 0, 0)),
        out_shape=jax.ShapeDtypeStruct((4, D, C), F32),
        scratch_shapes=[pltpu.VMEM((D, C), F32)],
        compiler_params=_params(("parallel", "arbitrary")),
    )(hn, dab)


def ffn_fwd(h, g, sh, sc, gate, gw13, l, i, w2):
    F, D = w2.shape
    C = F // 2

    def norm(h, g, sh, sc):
        xhat, _ = _rms(h)
        return ((xhat * g) * (1 + sc) + sh).astype(BF16)

    def act(a, b):
        sig = _sigmoid(a)
        sa = a * sig
        return (b * (sig * (1 + a * (1 - sig))), sa), sa * b
    blocks = [((None, None, None, D, C), lambda r, j, half=half: (2 * half + j, l, i, 0, 0)) for half in range(2)]
    hn, dt_dab, t = mm_fused(h, gw13, "nn", "ffn_w13", C, act, [(C, BF16, 2), (C, BF16)],
                             pro=norm, pro_vecs=[g, sh, sc], pro_out=True, b_blocks=blocks, n_cols=F)

    def res(acc, h, gate):
        return h + (0.5 * gate) * acc, acc
    h_out, y = mm_fused(t, w2, "nn", "ffn_w2", D, res, [(D, F32), (D, F32)], epi_rows=[(h, D)], epi_vecs=[gate])
    return h_out, (h, hn, dt_dab, t, y)


def ffn_bwd(dh_out, saved, g, sc, gate, gw13, l, i, w2):
    h, hn, dt_dab, t, y = saved
    F, D = w2.shape
    C = F // 2

    def scale(dh, y, gate):
        return ((0.5 * gate) * dh).astype(BF16), 0.5 * dh * y

    def act_bwd(dt, f):
        return ((dt * f[0].astype(F32), dt * f[1].astype(F32)),)
    dy, d_gate, dab = mm_fused(dh_out, w2, "nt", "ffn_w2_dx", C, act_bwd, [(C, BF16, 2)],
                               pro=scale, pro_rows=[y], pro_vecs=[gate], pro_out=True, n_pro_sums=1,
                               epi_rows=[(dt_dab, C)])
    dw2 = mm(t, dy, "tn", "ffn_w2_dw")
    dw13 = ffn_w13_grad(hn, dab)
    dhn = ffn_w13_dx(dab, gw13, l, i)
    dh_in, d_sh, d_sc, d_g = norm_mod_bwd(h, dhn, dh_out, g, sc, "norm_mod_bwd")
    return dh_in, (d_sh, d_sc, d_gate, d_g), dw13, dw2


def _shifted(xbuf, n):
    return [xbuf] + [pltpu.roll(xbuf, n - b, 0) for b in range(1, 8)]


def conv_fwd(u, w_dw, b_dw, ln_g, ln_b):
    S, D = u.shape
    tm = _tile(S, (256, 128))
    rc = 32
    first_tap = CONV_HALO - (CONV_WIDTH - 1)
    w = jnp.concatenate([w_dw, jnp.zeros((CONV_HALO - CONV_WIDTH, D), F32)], axis=0)

    def body(cur_ref, prev_ref, w_ref, b_ref, g_ref, beta_ref, z_ref, s_ref):
        i = pl.program_id(0)
        prev = jnp.where(i == 0, jnp.zeros((CONV_HALO, D), F32), prev_ref[...])
        xs = _shifted(jnp.concatenate([prev, cur_ref[...]], axis=0), tm + CONV_HALO)
        for c0 in range(0, tm, rc):
            acc = jnp.zeros((rc, D), F32)
            for k in range(CONV_WIDTH):
                off = first_tap + k
                a8, b = off // 8 * 8, off % 8
                acc = acc + w_ref[k:k + 1, :] * xs[b][c0 + a8:c0 + a8 + rc, :]
            z_ref[c0:c0 + rc, :] = acc + b_ref[...]
        z = z_ref[...]
        mu = jnp.mean(z, axis=-1, keepdims=True)
        zc = z - mu
        r = lax.rsqrt(jnp.mean(zc * zc, axis=-1, keepdims=True) + EPS)
        un = zc * r * g_ref[...] + beta_ref[...]
        s_ref[...] = (un * _sigmoid(un)).astype(BF16)

    nb = tm // CONV_HALO
    vec = pl.BlockSpec((1, D), lambda i: (0, 0))
    return pl.pallas_call(
        body, name="conv_fwd",
        grid=(S // tm,),
        in_specs=[pl.BlockSpec((tm, D), lambda i: (i, 0)),
                  pl.BlockSpec((CONV_HALO, D), lambda i: (jnp.maximum(i * nb - 1, 0), 0)),
                  pl.BlockSpec((CONV_HALO, D), lambda i: (0, 0)), vec, vec, vec],
        out_specs=[pl.BlockSpec((tm, D), lambda i: (i, 0)), pl.BlockSpec((tm, D), lambda i: (i, 0))],
        out_shape=[jax.ShapeDtypeStruct((S, D), F32), jax.ShapeDtypeStruct((S, D), BF16)],
        compiler_params=_params(("parallel",)),
    )(u, u, w, b_dw, ln_g, ln_b)


def conv_bwd(dz, u, w_dw):
    S, D = u.shape
    tm = _tile(S, (256, 128))
    rc = 32
    first_tap = CONV_HALO - (CONV_WIDTH - 1)
    w = jnp.concatenate([w_dw, jnp.zeros((CONV_HALO - CONV_WIDTH, D), F32)], axis=0)
    n_tiles = S // tm
    nb = tm // CONV_HALO

    def body(dz_ref, dzn_ref, u_ref, up_ref, w_ref, du_ref, dw_ref):
        i = pl.program_id(0)
        nxt = jnp.where(i == n_tiles - 1, jnp.zeros((CONV_HALO, D), F32), dzn_ref[...])
        dzs = _shifted(jnp.concatenate([dz_ref[...], nxt], axis=0), tm + CONV_HALO)
        for c0 in range(0, tm, rc):
            acc = jnp.zeros((rc, D), F32)
            for m in range(CONV_WIDTH):
                a8, b = m // 8 * 8, m % 8
                acc = acc + w_ref[CONV_WIDTH - 1 - m:CONV_WIDTH - m, :] * dzs[b][c0 + a8:c0 + a8 + rc, :]
            du_ref[c0:c0 + rc, :] = acc
        prev = jnp.where(i == 0, jnp.zeros((CONV_HALO, D), F32), up_ref[...])
        us = _shifted(jnp.concatenate([prev, u_ref[...]], axis=0), tm + CONV_HALO)
        dz = dz_ref[...]

        @pl.when(i == 0)
        def _():
            dw_ref[...] = jnp.zeros_like(dw_ref)

        for k in range(CONV_WIDTH):
            off = first_tap + k
            a8, b = off // 8 * 8, off % 8
            dw_ref[k:k + 1, :] += jnp.sum(dz * us[b][a8:a8 + tm, :], axis=0, keepdims=True)

    last_blk = S // CONV_HALO - 1
    du, dw = pl.pallas_call(
        body, name="conv_bwd",
        grid=(n_tiles,),
        in_specs=[pl.BlockSpec((tm, D), lambda i: (i, 0)),
                  pl.BlockSpec((CONV_HALO, D), lambda i: (jnp.minimum((i + 1) * nb, last_blk), 0)),
                  pl.BlockSpec((tm, D), lambda i: (i, 0)),
                  pl.BlockSpec((CONV_HALO, D), lambda i: (jnp.maximum(i * nb - 1, 0), 0)),
                  pl.BlockSpec((CONV_HALO, D), lambda i: (0, 0))],
        out_specs=[pl.BlockSpec((tm, D), lambda i: (i, 0)), pl.BlockSpec((CONV_HALO, D), lambda i: (0, 0))],
        out_shape=[jax.ShapeDtypeStruct((S, D), F32), jax.ShapeDtypeStruct((CONV_HALO, D), F32)],
        compiler_params=_params(("arbitrary",)),
    )(dz, dz, u, u, w)
    return du, dw[:CONV_WIDTH]


def conv_module_fwd(h, g, sh, sc, gate, p):
    D = h.shape[1]
    hn = norm_mod(h, g, sh, sc, "conv_norm_mod")
    pre = mm(hn, p["w_pw1"], "nn", "conv_pw1")
    ba, bg = p["b_pw1"][:, :D], p["b_pw1"][:, D:]

    def glu(a, gt, ba, bg):
        return (a + ba) * _sigmoid(gt + bg)
    u = rowwise(glu, [(pre, D, 0), (pre, D, 1)], [ba, bg], [(D, F32)], [], "conv_glu")[0]
    z, s = conv_fwd(u, p["w_dw"], p["b_dw"], p["ln_g"], p["ln_b"])
    yraw = mm(s, p["w_pw2"], "nn", "conv_pw2")
    h_out, y = residual(h, yraw, gate, 1.0, "conv_residual", bias=p["b_pw2"])
    return h_out, (h, hn, pre, u, z, s, y)


def conv_module_bwd(dh_out, saved, g, sc, gate, p):
    h, hn, pre, u, z, s, y = saved
    D = h.shape[1]
    dy, d_gate, d_b_pw2 = residual_bwd(dh_out, y, gate, 1.0, "conv_residual_bwd", with_bias_sum=True)
    d_w_pw2 = mm(s, dy, "tn", "conv_pw2_dw")
    ds = mm(dy, p["w_pw2"], "nt", "conv_pw2_dx")

    def ln_bwd(z, ds, g, beta):
        mu = jnp.mean(z, axis=-1, keepdims=True)
        zc = z - mu
        r = lax.rsqrt(jnp.mean(zc * zc, axis=-1, keepdims=True) + EPS)
        xhat = zc * r
        un = xhat * g + beta
        sig = _sigmoid(un)
        d_un = ds * (sig * (1 + un * (1 - sig)))
        dxhat = d_un * g
        dz = r * (dxhat - jnp.mean(dxhat, axis=-1, keepdims=True)
                  - xhat * jnp.mean(dxhat * xhat, axis=-1, keepdims=True))
        return dz, d_un * xhat, d_un, dz
    dz, d_ln_g, d_ln_b, d_b_dw = rowwise(ln_bwd, [z, ds], [p["ln_g"], p["ln_b"]], [(D, F32)], [D, D, D],
                                         "conv_ln_bwd")
    du, d_w_dw = conv_bwd(dz, u, p["w_dw"])
    ba, bg = p["b_pw1"][:, :D], p["b_pw1"][:, D:]

    def glu_bwd(a, gt, du, ba, bg):
        sg = _sigmoid(gt + bg)
        da = du * sg
        dg = du * (a + ba) * (sg * (1 - sg))
        dpre = jnp.concatenate([da, dg], axis=1)
        return dpre.astype(BF16), dpre
    dpre, d_b_pw1 = rowwise(glu_bwd, [(pre, D, 0), (pre, D, 1), du], [ba, bg], [(2 * D, BF16)], [2 * D],
                            "conv_glu_bwd")
    d_w_pw1 = mm(hn, dpre, "tn", "conv_pw1_dw")
    dhn = mm(dpre, p["w_pw1"], "nt", "conv_pw1_dx")
    dh_in, d_sh, d_sc, d_g = norm_mod_bwd(h, dhn, dh_out, g, sc, "norm_mod_bwd")
    grads = dict(w_pw1=d_w_pw1, b_pw1=d_b_pw1, w_dw=d_w_dw, b_dw=d_b_dw, ln_g=d_ln_g, ln_b=d_ln_b,
                 w_pw2=d_w_pw2, b_pw2=d_b_pw2)
    return dh_in, (d_sh, d_sc, d_gate, d_g), grads


def _rope(x, c, s1, s2):
    n = x.shape[1]
    return x * c + pltpu.roll(x, n - QK_ROPE // 2, 1) * s1 + pltpu.roll(x, QK_ROPE // 2, 1) * s2


def _rope_t(dy, c, s1, s2):
    n = dy.shape[1]
    return dy * c + pltpu.roll(dy * s1, QK_ROPE // 2, 1) + pltpu.roll(dy * s2, n - QK_ROPE // 2, 1)


def rope_tables(positions):
    inv_freq = ROPE_THETA ** (-jnp.arange(0, QK_ROPE, 2, dtype=F32) / QK_ROPE)
    ang = positions.astype(F32)[:, None] * inv_freq
    cos, sin = jnp.cos(ang), jnp.sin(ang)
    S = positions.shape[0]
    one = jnp.ones((S, QK_NOPE), F32)
    z16 = jnp.zeros((S, QK_ROPE // 2), F32)
    zn = jnp.zeros((S, QK_NOPE), F32)
    zt = jnp.zeros((S, HEAD_PAD - QK_NOPE - QK_ROPE), F32)
    c = jnp.concatenate([one, cos, cos, zt], axis=1)
    s1 = jnp.concatenate([zn, -sin, z16, zt], axis=1)
    s2 = jnp.concatenate([zn, z16, sin, zt], axis=1)
    return c, s1, s2


def attn_fwd(qr, kv, kpe, n_heads):
    S = qr.shape[0]
    H = n_heads
    tk = _tile(S, (ATTN_TILE,))
    nk = S // tk
    w = 2 if nk % 2 == 0 else 1
    tq = w * tk
    c2 = (QK_NOPE + QK_ROPE) ** -0.5 * LOG2_E
    nt = (((1,), (1,)), ((), ()))

    def body(q_ref, k_ref, v_ref, kpe_ref, o_ref, lse_ref, kf_ref, vt_ref, m_ref, l_ref, acc_ref):
        qi = pl.program_id(1)

        @pl.when(qi == 0)
        def _():
            kf_ref[...] = k_ref[...] + kpe_ref[...]
            for c in range(nk):
                vt_ref[c] = jnp.transpose(v_ref[c * tk:(c + 1) * tk, :].astype(F32)).astype(BF16)

        q = q_ref[...]
        m_ref[...] = jnp.full((1, tq), -jnp.inf, F32)
        l_ref[...] = jnp.zeros((1, tq), F32)
        acc_ref[...] = jnp.zeros((HEAD_PAD, tq), F32)

        def tile(j, first_visible):
            k = kf_ref[pl.ds(pl.multiple_of(j * tk, tk), tk), :]
            t = lax.dot_general(k, q, nt, preferred_element_type=F32) * c2
            if first_visible is not None:
                krow = lax.broadcasted_iota(jnp.int32, (tk, tq), 0)
                qcol = lax.broadcasted_iota(jnp.int32, (tk, tq), 1)
                t = jnp.where(krow + first_visible <= qcol, t, NEG)
            m_old = m_ref[...]
            m_new = jnp.maximum(m_old, jnp.max(t, axis=0, keepdims=True))
            alpha = jnp.exp2(m_old - m_new)
            p = jnp.exp2(t - m_new)
            l_ref[...] = alpha * l_ref[...] + jnp.sum(p, axis=0, keepdims=True)
            acc_ref[...] = alpha * acc_ref[...] + jnp.dot(vt_ref[j], p.astype(BF16), preferred_element_type=F32)
            m_ref[...] = m_new

        def unmasked(j, carry):
            tile(j, None)
            return carry

        lax.fori_loop(0, w * qi, unmasked, 0)
        for u in range(w):
            tile(w * qi + u, u * tk)
        l = l_ref[...]
        o_ref[...] = jnp.transpose(acc_ref[...] / l)
        lse = m_ref[...] + jnp.log(l) * LOG2_E
        for u in range(w):
            lse_ref[u] = lse[:, u * tk:(u + 1) * tk]

    return pl.pallas_call(
        body, name="attn_fwd",
        grid=(H, S // tq),
        in_specs=[pl.BlockSpec((tq, HEAD_PAD), lambda h, i: (i, h)),
                  pl.BlockSpec((S, HEAD_PAD), lambda h, i: (0, h)),
                  pl.BlockSpec((S, HEAD_PAD), lambda h, i: (0, H + h)),
                  pl.BlockSpec((S, HEAD_PAD), lambda h, i: (0, 0))],
        out_specs=[pl.BlockSpec((tq, HEAD_PAD), lambda h, i: (i, h)),
                   pl.BlockSpec((None, w, 1, tk), lambda h, i: (h, i, 0, 0))],
        out_shape=[jax.ShapeDtypeStruct((S, H * HEAD_PAD), F32), jax.ShapeDtypeStruct((H, nk, 1, tk), F32)],
        scratch_shapes=[pltpu.VMEM((S, HEAD_PAD), BF16), pltpu.VMEM((nk, HEAD_PAD, tk), BF16),
                        pltpu.VMEM((1, tq), F32), pltpu.VMEM((1, tq), F32), pltpu.VMEM((HEAD_PAD, tq), F32)],
        compiler_params=_params(("parallel", "arbitrary")),
    )(qr, kv, kv, kpe)


def attn_delta(o, do, n_heads):
    S = o.shape[0]
    H = n_heads
    tq = _tile(S, (ATTN_TILE,))
    nq = S // tq

    def body(o_ref, do_ref, d_ref):
        for c in range(nq):
            rows = slice(c * tq, (c + 1) * tq)
            prod = o_ref[rows, :] * do_ref[rows, :].astype(F32)
            d_ref[c] = jnp.sum(jnp.transpose(prod), axis=0, keepdims=True)

    return pl.pallas_call(
        body, name="attn_delta",
        grid=(H,),
        in_specs=[pl.BlockSpec((S, HEAD_PAD), lambda h: (0, h)), pl.BlockSpec((S, HEAD_PAD), lambda h: (0, h))],
        out_specs=pl.BlockSpec((None, nq, 1, tq), lambda h: (h, 0, 0, 0)),
        out_shape=jax.ShapeDtypeStruct((H, nq, 1, tq), F32),
        compiler_params=_params(("parallel",)),
    )(o, do)


def attn_bwd(qr, kv, kpe, do, lse2, delta, n_heads):
    S = qr.shape[0]
    H = n_heads
    tk = _tile(S, (ATTN_TILE,))
    nk = S // tk
    w = 2 if nk % 2 == 0 else 1
    tq = w * tk
    nq = S // tq
    scale = (QK_NOPE + QK_ROPE) ** -0.5
    c2 = scale * LOG2_E
    nt = (((1,), (1,)), ((), ()))
    lse2 = lse2.reshape(H, nq, 1, tq)
    delta4 = delta.reshape(H, nq, 1, tq)

    def body(k_ref, v_ref, kpe_ref, q_ref, do_ref, lse_ref, dl_ref, dq_ref, dk_ref, dv_ref, dka_ref, dva_ref,
             dqt_ref):
        kj = pl.program_id(1)
        k = k_ref[...] + kpe_ref[...]
        kt = jnp.transpose(k.astype(F32)).astype(BF16)
        v = v_ref[...]

        @pl.when(kj == 0)
        def _():
            dqt_ref[...] = jnp.zeros_like(dqt_ref)

        dka_ref[...] = jnp.zeros_like(dka_ref)
        dva_ref[...] = jnp.zeros_like(dva_ref)

        def tile(i, masked):
            start = pl.multiple_of(i * tq, tq)
            q = q_ref[pl.ds(start, tq), :]
            do = do_ref[pl.ds(start, tq), :]
            t = lax.dot_general(k, q, nt, preferred_element_type=F32) * c2
            if masked:
                krow = lax.broadcasted_iota(jnp.int32, (tk, tq), 0)
                qcol = lax.broadcasted_iota(jnp.int32, (tk, tq), 1)
                t = jnp.where(krow + (kj % w) * tk <= qcol, t, NEG)
            pt = jnp.exp2(t - lse_ref[i])
            dva_ref[...] += jnp.dot(pt.astype(BF16), do, preferred_element_type=F32)
            dpt = lax.dot_general(v, do, nt, preferred_element_type=F32)
            dst = (pt * (dpt - dl_ref[i]) * scale).astype(BF16)
            dka_ref[...] += jnp.dot(dst, q, preferred_element_type=F32)
            dqt_ref[i] += jnp.dot(kt, dst, preferred_element_type=F32)

        tile(kj // w, True)

        def unmasked(i, carry):
            tile(i, False)
            return carry

        lax.fori_loop(kj // w + 1, nq, unmasked, 0)
        dk_ref[...] = dka_ref[...]
        dv_ref[...] = dva_ref[...]

        @pl.when(kj == nk - 1)
        def _():
            for c in range(nq):
                dq_ref[c * tq:(c + 1) * tq, :] = jnp.transpose(dqt_ref[c])

    blk = pl.BlockSpec((tk, HEAD_PAD), lambda h, j: (j, h))
    whole = pl.BlockSpec((S, HEAD_PAD), lambda h, j: (0, h))
    stat = pl.BlockSpec((None, nq, 1, tq), lambda h, j: (h, 0, 0, 0))
    shp = jax.ShapeDtypeStruct((S, H * HEAD_PAD), F32)
    return pl.pallas_call(
        body, name="attn_bwd",
        grid=(H, nk),
        in_specs=[blk, pl.BlockSpec((tk, HEAD_PAD), lambda h, j: (j, H + h)),
                  pl.BlockSpec((tk, HEAD_PAD), lambda h, j: (j, 0)), whole, whole, stat, stat],
        out_specs=[whole, blk, blk],
        out_shape=[shp, shp, shp],
        scratch_shapes=[pltpu.VMEM((tk, HEAD_PAD), F32), pltpu.VMEM((tk, HEAD_PAD), F32),
                        pltpu.VMEM((nq, HEAD_PAD, tq), F32)],
        compiler_params=_params(("parallel", "arbitrary")),
    )(kv, kv, kpe, qr, do, lse2, delta4)


def _pad_heads(w, width):
    R = w.shape[0]
    w3 = w.reshape(R, -1, width)
    return jnp.pad(w3, ((0, 0), (0, 0), (0, HEAD_PAD - width))).reshape(R, -1)


def _unpad_heads(w, width):
    R = w.shape[0]
    return w.reshape(R, -1, HEAD_PAD)[:, :, :width].reshape(R, -1)


def mla_pad_weights(p):
    H = N_HEADS
    w_q_b = _pad_heads(p["w_q_b"], QK_NOPE + QK_ROPE)
    kvb = p["w_kv_b"].reshape(KV_LORA, H, QK_NOPE + V_HEAD)
    wk = _pad_heads(kvb[:, :, :QK_NOPE].reshape(KV_LORA, -1), QK_NOPE)
    wv = _pad_heads(kvb[:, :, QK_NOPE:].reshape(KV_LORA, -1), V_HEAD)
    D = p["w_kv_a"].shape[0]
    a = p["w_kv_a"]
    w_kv_a = jnp.concatenate([a[:, :KV_LORA], jnp.zeros((D, QK_NOPE), a.dtype), a[:, KV_LORA:],
                              jnp.zeros((D, HEAD_PAD - QK_NOPE - QK_ROPE), a.dtype)], axis=1)
    wo = p["w_o"].reshape(H, V_HEAD, -1)
    w_o = jnp.pad(wo, ((0, 0), (0, HEAD_PAD - V_HEAD), (0, 0))).reshape(H * HEAD_PAD, -1)
    return dict(w_q_a=p["w_q_a"], w_q_b=w_q_b, w_kv_b=jnp.concatenate([wk, wv], axis=1), w_kv_a=w_kv_a, w_o=w_o)


def mla_kv_fwd(h, g, sh, sc, kv_a_norm_g, pw, tabs):
    hkv = norm_mod(h, g, sh, sc, "kv_norm_mod")
    ckvp = mm(hkv, pw["w_kv_a"], "nn", "kv_a")

    def f(ckv, kpe, c, s1, s2, g):
        xhat, _ = _rms(ckv)
        return (xhat * g).astype(BF16), _rope(kpe, c, s1, s2).astype(BF16)
    ckv_n, kpe_r = rowwise(f, [(ckvp, KV_LORA, 0), (ckvp, HEAD_PAD, KV_LORA // HEAD_PAD), *tabs], [kv_a_norm_g],
                           [(KV_LORA, BF16), (HEAD_PAD, BF16)], [], "kv_a_norm_rope")
    kv = mm(ckv_n, pw["w_kv_b"], "nn", "kv_b", out_dtype=BF16)
    return kv, kpe_r, (h, hkv, ckvp, ckv_n)


def mla_kv_bwd(dh_stream, dk, dv, saved, g, sc, kv_a_norm_g, pw, tabs):
    h, hkv, ckvp, ckv_n = saved
    H = N_HEADS
    lane = jnp.arange(HEAD_PAD)
    pe_mask = ((lane >= QK_NOPE) & (lane < QK_NOPE + QK_ROPE)).astype(F32)[None, :]

    def f(dk, dv, c, s1, s2, mask):
        tot = dk[:, :HEAD_PAD]
        for hh in range(1, H):
            tot = tot + dk[:, hh * HEAD_PAD:(hh + 1) * HEAD_PAD]
        dkpe = _rope_t(tot * mask, c, s1, s2) * mask
        return jnp.concatenate([dk, dv], axis=1).astype(BF16), dkpe
    dkv, dkpe = rowwise(f, [dk, dv, *tabs], [pe_mask], [(2 * H * HEAD_PAD, BF16), (HEAD_PAD, F32)], [],
                        "kv_split_bwd")
    d_w_kv_b = mm(ckv_n, dkv, "tn", "kv_b_dw")
    dckv_n = mm(dkv, pw["w_kv_b"], "nt", "kv_b_dx")

    def f2(ckv, dn, dkpe, g):
        xhat, r = _rms(ckv)
        dx = _rms_bwd(xhat, r, dn * g)
        return jnp.concatenate([dx, dkpe], axis=1).astype(BF16), dn * xhat
    dckvp, d_kv_a_g = rowwise(f2, [(ckvp, KV_LORA, 0), dckv_n, dkpe], [kv_a_norm_g],
                              [(KV_LORA + HEAD_PAD, BF16)], [KV_LORA], "kv_a_norm_bwd")
    d_w_kv_a = mm(hkv, dckvp, "tn", "kv_a_dw")
    dhkv = mm(dckvp, pw["w_kv_a"], "nt", "kv_a_dx")
    dh, d_sh, d_sc, d_g = norm_mod_bwd(h, dhkv, dh_stream, g, sc, "norm_mod_bwd")
    return dh, (d_sh, d_sc, d_g), d_kv_a_g, d_w_kv_a, d_w_kv_b


def mla_fwd(h, g, sh, sc, gate, q_a_norm_g, pw, kv, kpe_r, tabs):
    H = N_HEADS
    hn = norm_mod(h, g, sh, sc, "mla_norm_mod")
    qa = mm(hn, pw["w_q_a"], "nn", "q_a")

    def f(qa, g):
        xhat, _ = _rms(qa)
        return (xhat * g).astype(BF16)
    qa_n = rowwise(f, [qa], [q_a_norm_g], [(qa.shape[1], BF16)], [], "q_a_norm")[0]
    qp = mm(qa_n, pw["w_q_b"], "nn", "q_b")

    def frope(q, c, s1, s2):
        return jnp.concatenate([_rope(q[:, hh * HEAD_PAD:(hh + 1) * HEAD_PAD], c, s1, s2) for hh in range(H)],
                               axis=1).astype(BF16)
    qr = rowwise(frope, [qp, *tabs], [], [(H * HEAD_PAD, BF16)], [], "q_rope")[0]
    o, lse = attn_fwd(qr, kv, kpe_r, H)
    y = mm(o, pw["w_o"], "nn", "w_o")
    h_out, _ = residual(h, y, gate, 1.0, "mla_residual")
    return h_out, (h, hn, qa, qa_n, qr, o, lse, y)


def mla_bwd(dh_out, saved, g, sc, gate, q_a_norm_g, pw, kv, kpe_r, tabs):
    h, hn, qa, qa_n, qr, o, lse, y = saved
    H = N_HEADS
    dy, d_gate = residual_bwd(dh_out, y, gate, 1.0, "mla_residual_bwd")
    d_w_o = mm(o, dy, "tn", "w_o_dw")
    do = mm(dy, pw["w_o"], "nt", "w_o_dx", out_dtype=BF16)
    delta = attn_delta(o, do, H)
    dqr, dk, dv = attn_bwd(qr, kv, kpe_r, do, lse, delta, H)

    def frope_t(dq, c, s1, s2):
        return jnp.concatenate([_rope_t(dq[:, hh * HEAD_PAD:(hh + 1) * HEAD_PAD], c, s1, s2) for hh in range(H)],
                               axis=1).astype(BF16)
    dqp = rowwise(frope_t, [dqr, *tabs], [], [(H * HEAD_PAD, BF16)], [], "q_rope_bwd")[0]
    d_w_q_b = mm(qa_n, dqp, "tn", "q_b_dw")
    dqa_n = mm(dqp, pw["w_q_b"], "nt", "q_b_dx")

    def f(qa, dn, g):
        xhat, r = _rms(qa)
        return _rms_bwd(xhat, r, dn * g).astype(BF16), dn * xhat
    dqa, d_q_a_g = rowwise(f, [qa, dqa_n], [q_a_norm_g], [(qa.shape[1], BF16)], [qa.shape[1]], "q_a_norm_bwd")
    d_w_q_a = mm(hn, dqa, "tn", "q_a_dw")
    dhn = mm(dqa, pw["w_q_a"], "nt", "q_a_dx")
    dh_in, d_sh, d_sc, d_g = norm_mod_bwd(h, dhn, dh_out, g, sc, "norm_mod_bwd")
    grads = dict(w_q_a=d_w_q_a, q_a_norm_g=d_q_a_g, w_q_b=d_w_q_b, w_o=d_w_o)
    return dh_in, (d_sh, d_sc, d_gate, d_g), grads, dk, dv


def loss_head(h, target, g):
    D = h.shape[1]

    def f(h, t, g):
        xhat, r = _rms(h)
        err = xhat * g - t
        dy = err * (1.0 / D)
        dh = _rms_bwd(xhat, r, dy * g)
        return dh, (0.5 / D) * err * err, dy * xhat
    return rowwise(f, [h, target], [g], [(D, F32)], [D, D], "loss_head")


def _place():
    x, y, c = lax.axis_index("x"), lax.axis_index("y"), lax.axis_index("c")
    chips = [(1 - x, y), (x, 1 - y), (1 - x, 1 - y)]
    return x, y, c, chips


HBM_SPEC = pl.BlockSpec(memory_space=pltpu.HBM)


def all_gather8(v):
    m, n = v.shape

    def body(x_ref, out_ref, send_sems, recv_sems, local_sem):
        x, y, c, chips = _place()
        me, sibling = (x, y, c), (x, y, 1 - c)

        def rows(px, py, pc):
            return out_ref.at[4 * px + 2 * py + pc]

        def copy(k, block, to, src=None):
            return pltpu.make_async_remote_copy(
                src_ref=rows(*block) if src is None else src, dst_ref=rows(*block),
                send_sem=send_sems.at[k], recv_sem=recv_sems.at[k], device_id=to, device_id_type=MESH)

        mine = pltpu.make_async_copy(x_ref, rows(*me), local_sem)
        mine.start()
        first = [copy(0, me, sibling, src=x_ref)]
        first += [copy(1 + j, me, (*chip, c), src=x_ref) for j, chip in enumerate(chips)]
        for cp in first:
            cp.start()
        passed = [copy(4 + j, (*chip, c), sibling) for j, chip in enumerate(chips)]
        for j, chip in enumerate(chips):
            copy(1 + j, (*chip, c), me).wait_recv()
            passed[j].start()
        copy(0, sibling, me).wait_recv()
        for j, chip in enumerate(chips):
            copy(4 + j, (*chip, 1 - c), me).wait_recv()
        for cp in first + passed:
            cp.wait_send()
        mine.wait()

    return pl.pallas_call(
        body, name="all_gather8",
        out_shape=jax.ShapeDtypeStruct((8, m, n), v.dtype),
        in_specs=[pl.BlockSpec(memory_space=pltpu.VMEM)],
        out_specs=pl.BlockSpec(memory_space=pltpu.VMEM),
        scratch_shapes=[pltpu.SemaphoreType.DMA((7,)), pltpu.SemaphoreType.DMA((7,)), pltpu.SemaphoreType.DMA],
        compiler_params=pltpu.CompilerParams(vmem_limit_bytes=VMEM_LIMIT_BYTES),
    )(v)


def gather_weights(bufs):
    n = len(bufs)

    def body(*refs):
        ins, outs = refs[:n], refs[n:2 * n]
        send_sems, recv_sems = refs[2 * n:]
        x, y, c, chips = _place()
        sibling = (x, y, 1 - c)
        me = 2 * x + y

        def idx(chip):
            return 2 * chip[0] + chip[1]

        def copy(w, k, src, dst, to):
            return pltpu.make_async_remote_copy(src_ref=src, dst_ref=dst, send_sem=send_sems.at[6 * w + k],
                                                recv_sem=recv_sems.at[6 * w + k], device_id=to, device_id_type=MESH)

        first = [copy(w, j, ins[w].at[me, c], outs[w].at[me, c], (*chip, c))
                 for w in range(n) for j, chip in enumerate(chips)]
        for cp in first:
            cp.start()
        passed = []
        for w in range(n):
            for j, chip in enumerate(chips):
                landed = outs[w].at[idx(chip), c]
                copy(w, j, landed, landed, (*chip, c)).wait_recv()
                fwd = copy(w, 3 + j, landed, landed, sibling)
                fwd.start()
                passed.append(fwd)
        for w in range(n):
            for j, chip in enumerate(chips):
                other = outs[w].at[idx(chip), 1 - c]
                copy(w, 3 + j, other, other, sibling).wait_recv()
        for cp in first + passed:
            cp.wait_send()

    return pl.pallas_call(
        body, name="gather_weights",
        out_shape=[jax.ShapeDtypeStruct(b.shape, b.dtype) for b in bufs],
        in_specs=[HBM_SPEC] * n, out_specs=[HBM_SPEC] * n,
        input_output_aliases={w: w for w in range(n)},
        scratch_shapes=[pltpu.SemaphoreType.DMA((6 * n,)), pltpu.SemaphoreType.DMA((6 * n,))],
    )(*bufs)


def exchange_halves(gs):
    n = len(gs)

    def body(*refs):
        ins, theirs = refs[:n], refs[n:2 * n]
        send_sems, recv_sems = refs[2 * n:]
        x, y, c, _ = _place()
        sends = [pltpu.make_async_remote_copy(src_ref=ins[w].at[:, 1 - c], dst_ref=theirs[w],
                                              send_sem=send_sems.at[w], recv_sem=recv_sems.at[w],
                                              device_id=(x, y, 1 - c), device_id_type=MESH) for w in range(n)]
        for cp in sends:
            cp.start()
        for cp in sends:
            cp.wait()

    return pl.pallas_call(
        body, name="exchange_halves",
        out_shape=[jax.ShapeDtypeStruct((4,) + g.shape[2:], g.dtype) for g in gs],
        in_specs=[HBM_SPEC] * n, out_specs=[HBM_SPEC] * n,
        scratch_shapes=[pltpu.SemaphoreType.DMA((n,)), pltpu.SemaphoreType.DMA((n,))],
    )(*gs)


def scatter_blocks(ps):
    n = len(ps)

    def body(*refs):
        ins, outs = refs[:n], refs[n:2 * n]
        send_sems, recv_sems = refs[2 * n:]
        x, y, c, chips = _place()
        sends = [pltpu.make_async_remote_copy(src_ref=ins[w].at[2 * chip[0] + chip[1]], dst_ref=outs[w].at[j],
                                              send_sem=send_sems.at[3 * w + j], recv_sem=recv_sems.at[3 * w + j],
                                              device_id=(*chip, c), device_id_type=MESH)
                 for w in range(n) for j, chip in enumerate(chips)]
        for cp in sends:
            cp.start()
        for cp in sends:
            cp.wait()

    return pl.pallas_call(
        body, name="scatter_blocks",
        out_shape=[jax.ShapeDtypeStruct((3,) + p.shape[1:], p.dtype) for p in ps],
        in_specs=[HBM_SPEC] * n, out_specs=[HBM_SPEC] * n,
        scratch_shapes=[pltpu.SemaphoreType.DMA((3 * n,)), pltpu.SemaphoreType.DMA((3 * n,))],
    )(*ps)


def join_halves(qs):
    n = len(qs)

    def body(*refs):
        ins, outs = refs[:n], refs[n:2 * n]
        send_sems, recv_sems = refs[2 * n:]
        x, y, c, _ = _place()
        sends = [pltpu.make_async_remote_copy(src_ref=ins[w].at[c], dst_ref=outs[w].at[c], send_sem=send_sems.at[w],
                                              recv_sem=recv_sems.at[w], device_id=(x, y, 1 - c), device_id_type=MESH)
                 for w in range(n)]
        for cp in sends:
            cp.start()
        for w in range(n):
            other = outs[w].at[1 - c]
            pltpu.make_async_remote_copy(src_ref=other, dst_ref=other, send_sem=send_sems.at[w],
                                         recv_sem=recv_sems.at[w], device_id=(x, y, 1 - c),
                                         device_id_type=MESH).wait_recv()
        for cp in sends:
            cp.wait_send()

    return pl.pallas_call(
        body, name="join_halves",
        out_shape=[jax.ShapeDtypeStruct(q.shape, q.dtype) for q in qs],
        in_specs=[HBM_SPEC] * n, out_specs=[HBM_SPEC] * n,
        input_output_aliases={w: w for w in range(n)},
        scratch_shapes=[pltpu.SemaphoreType.DMA((n,)), pltpu.SemaphoreType.DMA((n,))],
    )(*qs)


def _row_tile(R, row_bytes):
    tm = R
    for t in (512, 256, 128, 64, 32, 16, 8):
        if R % t == 0:
            tm = t
            if t * row_bytes <= ROW_TILE_BUDGET:
                break
    return tm


def sum_siblings(g, theirs, place):
    _, _, R, C = g.shape
    tm = _row_tile(R, 3 * C * 4)

    def body(place_ref, a_ref, b_ref, o_ref):
        o_ref[...] = (a_ref[...] + b_ref[...]).astype(BF16)

    return pl.pallas_call(
        body, name="sum_siblings",
        grid_spec=pltpu.PrefetchScalarGridSpec(
            num_scalar_prefetch=1, grid=(4, R // tm),
            in_specs=[pl.BlockSpec((None, None, tm, C), lambda j, i, s: (j, s[1], i, 0)),
                      pl.BlockSpec((None, tm, C), lambda j, i, s: (j, i, 0))],
            out_specs=pl.BlockSpec((None, tm, C), lambda j, i, s: (j, i, 0))),
        out_shape=jax.ShapeDtypeStruct((4, R, C), BF16),
        compiler_params=_params(("parallel", "parallel")),
    )(place, g, theirs)


def sum_chips(p, landed, place):
    _, R, C = p.shape
    tm = _row_tile(R, 5 * C * 4)

    def body(place_ref, p_ref, l0_ref, l1_ref, l2_ref, o_ref):
        o_ref[...] = ((p_ref[...].astype(F32) + l0_ref[...].astype(F32)) + l1_ref[...].astype(F32)
                      ) + l2_ref[...].astype(F32)

    return pl.pallas_call(
        body, name="sum_chips",
        grid_spec=pltpu.PrefetchScalarGridSpec(
            num_scalar_prefetch=1, grid=(R // tm,),
            in_specs=[pl.BlockSpec((None, tm, C), lambda i, s: (s[0], i, 0))]
            + [pl.BlockSpec((None, tm, C), lambda i, s, j=j: (j, i, 0)) for j in range(3)],
            out_specs=pl.BlockSpec((None, tm, C), lambda i, s: (s[1], i, 0))),
        out_shape=jax.ShapeDtypeStruct((2, R, C), F32),
        compiler_params=_params(("parallel",)),
    )(place, p, landed, landed, landed)


def sum_blocks(items, name):
    R, C = items[0][0].shape[1:]
    tm = R
    for t in (512, 256, 128, 64, 32, 16, 8):
        if R % t == 0:
            tm = t
            if t * C * 4 * (len(items) + 1) <= ROW_TILE_BUDGET:
                break
    n = len(items)

    def body(*refs):
        acc = refs[0][...].astype(F32)
        for r in refs[1:n]:
            acc = acc + r[...].astype(F32)
        refs[n][...] = acc

    return pl.pallas_call(
        body, name=name,
        grid=(R // tm,),
        in_specs=[pl.BlockSpec((None, tm, C), lambda i, j=j: (j, i, 0)) for _, j in items],
        out_specs=pl.BlockSpec((tm, C), lambda i: (i, 0)),
        out_shape=jax.ShapeDtypeStruct((R, C), F32),
        compiler_params=_params(("parallel",)),
    )(*[a for a, _ in items])


def reduce_scatter_grads(gs, place):
    theirs = exchange_halves(gs)
    ps = [sum_siblings(g, t, place) for g, t in zip(gs, theirs)]
    landed = scatter_blocks(ps)
    qs = [sum_chips(p, l, place) for p, l in zip(ps, landed)]
    joined = join_halves(qs)
    return [j.reshape(2 * j.shape[1], j.shape[2]) for j in joined]


def adamw(w, g, m, v):
    shape = w.shape
    C = shape[-1]
    R = w.size // C
    tm = R
    for t in (512, 256, 128, 64, 32, 16, 8):
        if R % t == 0:
            tm = t
            if t * C * 4 * 7 <= ROW_TILE_BUDGET:
                break

    def f(w, g, m, v):
        m = ADAM_B1 * m + (1.0 - ADAM_B1) * g
        v = ADAM_B2 * v + (1.0 - ADAM_B2) * (g * g)
        m_hat = m / (1.0 - ADAM_B1 ** ADAM_STEP)
        v_hat = v / (1.0 - ADAM_B2 ** ADAM_STEP)
        delta = -ADAM_LR * (m_hat / (jnp.sqrt(v_hat) + ADAM_EPS) + ADAM_WD * w)
        return delta, m, v

    d, nm, nv = rowwise(f, [a.reshape(R, C) for a in (w, g, m, v)], [], [(C, F32)] * 3, [], "adamw", tm=tm)
    return d.reshape(shape), nm.reshape(shape), nv.reshape(shape)


def _cast_into_slot(w, place):
    C = w.shape[-1]
    w2 = w.reshape(-1, C)
    R = w2.shape[0]
    tm = _row_tile(R, 6 * C)

    def body(place_ref, w_ref, o_ref):
        o_ref[...] = w_ref[...].astype(BF16)

    out = pl.pallas_call(
        body, name="cast_bf16",
        grid_spec=pltpu.PrefetchScalarGridSpec(
            num_scalar_prefetch=1, grid=(R // tm,),
            in_specs=[pl.BlockSpec((tm, C), lambda i, s: (i, 0))],
            out_specs=pl.BlockSpec((None, tm, C), lambda i, s: (s[0], i, 0))),
        out_shape=jax.ShapeDtypeStruct((4, R, C), BF16),
        compiler_params=_params(("parallel",)),
    )(place, w2)
    return out.reshape(4, 2, R // 2, C)


def _pack(vs):
    flat = jnp.concatenate([v.reshape(-1) for v in vs])
    n = flat.shape[0]
    total = -(-n // 1024) * 1024
    return jnp.pad(flat, (0, total - n)).reshape(total // 128, 128)


def _unpack(flat, like):
    out, o = [], 0
    for shp in like:
        sz = 1
        for d in shp:
            sz *= d
        out.append(flat[o:o + sz].reshape(shp))
        o += sz
    return out


def _cols_to_blocks(g, n_chips=4):
    R, N = g.shape
    C = N // n_chips
    return g.reshape(R, n_chips, C).transpose(1, 0, 2).reshape(n_chips, 2, R // 2, C)


def _rows_to_blocks(g, n_chips=4):
    R, C = g.shape
    return g.reshape(n_chips, 2, R // n_chips // 2, C)


def kernel(x, c, positions, ada_w, ada_b, norm_g, ffn_w13, ffn_w2, conv_w_pw1, conv_b_pw1, conv_w_dw, conv_b_dw, conv_ln_g, conv_ln_b, conv_w_pw2, conv_b_pw2, kv_ada_w, kv_ada_b, kv_norm_g, w_kv_a, kv_a_norm_g, w_kv_b, w_q_a, q_a_norm_g, w_q_b, w_o, final_norm_g, loss_target, m_ada_w, m_ada_b, m_norm_g, m_ffn_w13, m_ffn_w2, m_conv_w_pw1, m_conv_b_pw1, m_conv_w_dw, m_conv_b_dw, m_conv_ln_g, m_conv_ln_b, m_conv_w_pw2, m_conv_b_pw2, m_kv_ada_w, m_kv_ada_b, m_kv_norm_g, m_w_kv_a, m_kv_a_norm_g, m_w_kv_b, m_w_q_a, m_q_a_norm_g, m_w_q_b, m_w_o, m_final_norm_g, v_ada_w, v_ada_b, v_norm_g, v_ffn_w13, v_ffn_w2, v_conv_w_pw1, v_conv_b_pw1, v_conv_w_dw, v_conv_b_dw, v_conv_ln_g, v_conv_ln_b, v_conv_w_pw2, v_conv_b_pw2, v_kv_ada_w, v_kv_ada_b, v_kv_norm_g, v_w_kv_a, v_kv_a_norm_g, v_w_kv_b, v_w_q_a, v_q_a_norm_g, v_w_q_b, v_w_o, v_final_norm_g):
    S, D = x.shape[1], x.shape[2]
    H = N_HEADS
    F = ffn_w2.shape[2] * 4
    xi, yi, ci = lax.axis_index("x"), lax.axis_index("y"), lax.axis_index("c")
    chip = 2 * xi + yi
    dev = 2 * chip + ci
    place = jnp.stack([chip, ci]).astype(jnp.int32)
    h0 = x[0]
    target = loss_target[0]

    silu_c = rowwise(lambda a: a * _sigmoid(a), [c], [], [(D, F32)], [], "silu_c")[0]
    silu_all = all_gather8(silu_c.reshape(8, D // 8)).reshape(8, D)
    n_ada = ada_w.shape[2]
    n_kv = kv_ada_w.shape[1]
    ada_b_mine = lax.dynamic_slice_in_dim(ada_b, chip * n_ada, n_ada, axis=1)
    kv_b_mine = lax.dynamic_slice_in_dim(kv_ada_b, chip * n_kv, n_kv, axis=0)[None, :]
    mods = [mm(silu_all, ada_w[l], "nn", "ada_rows", bias=ada_b_mine[l:l + 1]) for l in range(2)]
    mods.append(mm(silu_all, kv_ada_w, "nn", "kv_ada_rows", bias=kv_b_mine))
    n_mod_cols = 2 * n_ada + n_kv
    mod_pack = jnp.concatenate(mods, axis=1).reshape(-1, 128)
    mod_all = all_gather8(mod_pack).reshape(8, 8, n_mod_cols)[0::2]
    mod_mine = lax.dynamic_index_in_dim(mod_all, dev, axis=1, keepdims=False)
    mod = [mod_mine[:, l * n_ada:(l + 1) * n_ada].reshape(N_MOD, D) for l in range(2)]
    kv_mod = mod_mine[:, 2 * n_ada:].reshape(2, D)
    kv_shift, kv_scale = kv_mod[0:1], kv_mod[1:2]

    def mrow(l, k):
        return mod[l][k:k + 1]

    big = dict(ffn_w13=ffn_w13, ffn_w2=ffn_w2, conv_w_pw1=conv_w_pw1, conv_w_pw2=conv_w_pw2, w_kv_a=w_kv_a,
               w_kv_b=w_kv_b, w_q_a=w_q_a, w_q_b=w_q_b, w_o=w_o)
    names = list(big)
    gathered = gather_weights([_cast_into_slot(big[k], place) for k in names])
    gw = dict(zip(names, gathered))
    small_like = [norm_g.shape, conv_b_pw1.shape, conv_w_dw.shape, conv_b_dw.shape, conv_ln_g.shape,
                  conv_ln_b.shape, conv_b_pw2.shape]
    small_pack = _pack([norm_g, conv_b_pw1, conv_w_dw, conv_b_dw, conv_ln_g, conv_ln_b, conv_b_pw2])
    small_all = all_gather8(small_pack)[0::2].reshape(4, -1)
    per_chip = [_unpack(small_all[j], small_like) for j in range(4)]
    smalls = [jnp.concatenate([per_chip[j][k] for j in range(4)], axis=-1) for k in range(len(small_like))]
    norm_g_f, b_pw1_f, w_dw_f, b_dw_f, ln_g_f, ln_b_f, b_pw2_f = smalls

    gw13 = gw["ffn_w13"].reshape(4, 2, 2, D, F // 2)
    w2 = gw["ffn_w2"].reshape(4, 2, 2, F // 4, D).transpose(1, 2, 0, 3, 4).reshape(2, 2, F, D)
    conv_p = dict(
        w_pw1=gw["conv_w_pw1"].reshape(4, D, 2 * D // 4).transpose(1, 0, 2).reshape(D, 2 * D),
        b_pw1=b_pw1_f, w_dw=w_dw_f[0], b_dw=b_dw_f, ln_g=ln_g_f, ln_b=ln_b_f,
        w_pw2=gw["conv_w_pw2"].reshape(D, D), b_pw2=b_pw2_f)
    q_lora = w_q_a.shape[2]
    mla_p = dict(
        w_kv_a=gw["w_kv_a"].reshape(D, KV_LORA + QK_ROPE),
        w_kv_b=gw["w_kv_b"].reshape(4, KV_LORA, -1).transpose(1, 0, 2).reshape(KV_LORA, -1),
        w_q_a=gw["w_q_a"].reshape(D, q_lora),
        w_q_b=gw["w_q_b"].reshape(4, q_lora, -1).transpose(1, 0, 2).reshape(q_lora, -1),
        w_o=gw["w_o"].reshape(H * V_HEAD, D))
    pw = mla_pad_weights(mla_p)
    tabs = rope_tables(positions[0])

    def ng(l, k):
        return norm_g_f[l, k][None, :]

    h = h0
    h, s_f1_0 = ffn_fwd(h, ng(0, 0), mrow(0, 0), mrow(0, 1), mrow(0, 2), gw13, 0, 0, w2[0, 0])
    h, s_conv = conv_module_fwd(h, ng(0, 1), mrow(0, 3), mrow(0, 4), mrow(0, 5), conv_p)
    h, s_f2_0 = ffn_fwd(h, ng(0, 2), mrow(0, 6), mrow(0, 7), mrow(0, 8), gw13, 0, 1, w2[0, 1])
    kv_norm = kv_norm_g[None, :]
    kv_a_g = kv_a_norm_g[None, :]
    kv, kpe_r, s_kv = mla_kv_fwd(h, kv_norm, kv_shift, kv_scale, kv_a_g, pw, tabs)
    h, s_f1_1 = ffn_fwd(h, ng(1, 0), mrow(1, 0), mrow(1, 1), mrow(1, 2), gw13, 1, 0, w2[1, 0])
    h, s_mla = mla_fwd(h, ng(1, 1), mrow(1, 3), mrow(1, 4), mrow(1, 5), q_a_norm_g, pw, kv, kpe_r, tabs)
    h, s_f2_1 = ffn_fwd(h, ng(1, 2), mrow(1, 6), mrow(1, 7), mrow(1, 8), gw13, 1, 1, w2[1, 1])
    dh, loss_cols, d_final_g = loss_head(h, target, final_norm_g[None, :])

    dh, v_f2_1, dw13_11, dw2_11 = ffn_bwd(dh, s_f2_1, ng(1, 2), mrow(1, 7), mrow(1, 8), gw13, 1, 1, w2[1, 1])
    dh, v_mla, g_mla, dk, dv = mla_bwd(dh, s_mla, ng(1, 1), mrow(1, 4), mrow(1, 5), q_a_norm_g, pw, kv, kpe_r, tabs)
    dh, v_f1_1, dw13_10, dw2_10 = ffn_bwd(dh, s_f1_1, ng(1, 0), mrow(1, 1), mrow(1, 2), gw13, 1, 0, w2[1, 0])
    dh, v_kv, d_kv_a_g, d_w_kv_a, d_w_kv_b = mla_kv_bwd(dh, dk, dv, s_kv, kv_norm, kv_scale, kv_a_g, pw, tabs)
    dh, v_f2_0, dw13_01, dw2_01 = ffn_bwd(dh, s_f2_0, ng(0, 2), mrow(0, 7), mrow(0, 8), gw13, 0, 1, w2[0, 1])
    dh, v_conv, g_conv = conv_module_bwd(dh, s_conv, ng(0, 1), mrow(0, 4), mrow(0, 5), conv_p)
    dh, v_f1_0, dw13_00, dw2_00 = ffn_bwd(dh, s_f1_0, ng(0, 0), mrow(0, 1), mrow(0, 2), gw13, 0, 0, w2[0, 0])
    grad_x = dh[None]

    d_w_kv_a_u = jnp.concatenate([d_w_kv_a[:, :KV_LORA], d_w_kv_a[:, KV_LORA + QK_NOPE:KV_LORA + QK_NOPE + QK_ROPE]],
                                 axis=1)
    hk = H * HEAD_PAD
    dkb = jnp.concatenate([d_w_kv_b[:, :hk].reshape(KV_LORA, H, HEAD_PAD)[:, :, :QK_NOPE],
                           d_w_kv_b[:, hk:].reshape(KV_LORA, H, HEAD_PAD)[:, :, :V_HEAD]], axis=2).reshape(KV_LORA, -1)
    d_w_q_b_u = _unpad_heads(g_mla["w_q_b"], QK_NOPE + QK_ROPE)
    d_w_o_u = g_mla["w_o"].reshape(H, HEAD_PAD, D)[:, :V_HEAD].reshape(H * V_HEAD, D)
    full = [dw.reshape(4, 2, D // 2, F // 2) for dw in (dw13_00, dw13_01, dw13_10, dw13_11)] + [
            _rows_to_blocks(dw2_00), _rows_to_blocks(dw2_01), _rows_to_blocks(dw2_10), _rows_to_blocks(dw2_11),
            _cols_to_blocks(g_conv["w_pw1"]), _rows_to_blocks(g_conv["w_pw2"]), _rows_to_blocks(d_w_kv_a_u),
            _cols_to_blocks(dkb), _rows_to_blocks(g_mla["w_q_a"]), _cols_to_blocks(d_w_q_b_u),
            _rows_to_blocks(d_w_o_u)]
    red = reduce_scatter_grads(full, place)
    g_ffn_w13 = jnp.stack(red[0:4]).reshape(ffn_w13.shape)
    g_ffn_w2 = jnp.stack(red[4:8]).reshape(ffn_w2.shape)
    g_conv_w_pw1 = red[8].reshape(conv_w_pw1.shape)
    g_conv_w_pw2 = red[9].reshape(conv_w_pw2.shape)
    g_w_kv_a = red[10].reshape(w_kv_a.shape)
    g_w_kv_b = red[11].reshape(w_kv_b.shape)
    g_w_q_a = red[12].reshape(w_q_a.shape)
    g_w_q_b = red[13].reshape(w_q_b.shape)
    g_w_o = red[14].reshape(w_o.shape)

    def dmod(v1, vm, v2):
        return jnp.concatenate([v1[0], v1[1], v1[2], vm[0], vm[1], vm[2], v2[0], v2[1], v2[2]], axis=1)
    d_mod0 = dmod(v_f1_0, v_conv, v_f2_0)
    d_mod1 = dmod(v_f1_1, v_mla, v_f2_1)
    d_kv_mod = jnp.concatenate([v_kv[0], v_kv[1]], axis=1)
    d_norm_g = jnp.concatenate([v_f1_0[3], v_conv[3], v_f2_0[3], v_f1_1[3], v_mla[3], v_f2_1[3]], axis=0)
    vec_list = [d_mod0, d_mod1, d_kv_mod, d_norm_g, g_conv["b_pw1"], g_conv["w_dw"], g_conv["b_dw"], g_conv["ln_g"],
                g_conv["ln_b"], g_conv["b_pw2"], v_kv[2], d_kv_a_g, g_mla["q_a_norm_g"], d_final_g, loss_cols]
    vec_like = [v.shape for v in vec_list]
    vec_pack = _pack(vec_list)
    n_mod_rows = (2 * N_MOD * D + 2 * D) // 128
    vec_all = all_gather8(vec_pack)
    vec_sum = sum_blocks([(vec_all, d) for d in range(8)], "sum_devices").reshape(-1)
    (_, _, _, s_norm_g, s_b_pw1, s_w_dw, s_b_dw, s_ln_g, s_ln_b, s_b_pw2, s_kv_norm_g, s_kv_a_g, s_q_a_g,
     s_final_g, s_loss) = _unpack(vec_sum, vec_like)
    loss = jnp.sum(s_loss)
    dmod_all = vec_all[:, :n_mod_rows].reshape(8, 2 * N_MOD * D + 2 * D)
    dmod_sum = vec_sum[:2 * N_MOD * D + 2 * D]
    g_ada_b = dmod_sum[:2 * N_MOD * D].reshape(2, N_MOD * D)
    g_kv_ada_b = dmod_sum[2 * N_MOD * D:]
    g_ada_w = []
    for l in range(2):
        cols = lax.dynamic_slice_in_dim(dmod_all[:, l * N_MOD * D:(l + 1) * N_MOD * D], chip * n_ada, n_ada, axis=1)
        g_ada_w.append(mm(silu_all, cols, "tn", "ada_w_grad"))
    g_ada_w = jnp.stack(g_ada_w)
    kv_cols = lax.dynamic_slice_in_dim(dmod_all[:, 2 * N_MOD * D:], chip * n_kv, n_kv, axis=1)
    g_kv_ada_w = mm(silu_all, kv_cols, "tn", "kv_ada_w_grad")

    def shard(v, width):
        return lax.dynamic_slice_in_dim(v, chip * width, width, axis=v.ndim - 1)

    Dq = D // 4
    g_norm_g = shard(s_norm_g.reshape(2, 3, D), Dq)
    g_conv_b_pw1 = shard(s_b_pw1, 2 * D // 4)
    g_conv_w_dw = shard(s_w_dw, Dq)[None]
    g_conv_b_dw = shard(s_b_dw, Dq)
    g_conv_ln_g = shard(s_ln_g, Dq)
    g_conv_ln_b = shard(s_ln_b, Dq)
    g_conv_b_pw2 = shard(s_b_pw2, Dq)

    grads = [g_ada_w, g_ada_b, g_norm_g, g_ffn_w13, g_ffn_w2, g_conv_w_pw1, g_conv_b_pw1, g_conv_w_dw, g_conv_b_dw,
             g_conv_ln_g, g_conv_ln_b, g_conv_w_pw2, g_conv_b_pw2, g_kv_ada_w, g_kv_ada_b, s_kv_norm_g[0], g_w_kv_a,
             s_kv_a_g[0], g_w_kv_b, g_w_q_a, s_q_a_g, g_w_q_b, g_w_o, s_final_g[0]]
    weights = [ada_w, ada_b, norm_g, ffn_w13, ffn_w2, conv_w_pw1, conv_b_pw1, conv_w_dw, conv_b_dw, conv_ln_g,
               conv_ln_b, conv_w_pw2, conv_b_pw2, kv_ada_w, kv_ada_b, kv_norm_g, w_kv_a, kv_a_norm_g, w_kv_b, w_q_a,
               q_a_norm_g, w_q_b, w_o, final_norm_g]
    ms = [m_ada_w, m_ada_b, m_norm_g, m_ffn_w13, m_ffn_w2, m_conv_w_pw1, m_conv_b_pw1, m_conv_w_dw, m_conv_b_dw,
          m_conv_ln_g, m_conv_ln_b, m_conv_w_pw2, m_conv_b_pw2, m_kv_ada_w, m_kv_ada_b, m_kv_norm_g, m_w_kv_a,
          m_kv_a_norm_g, m_w_kv_b, m_w_q_a, m_q_a_norm_g, m_w_q_b, m_w_o, m_final_norm_g]
    vs = [v_ada_w, v_ada_b, v_norm_g, v_ffn_w13, v_ffn_w2, v_conv_w_pw1, v_conv_b_pw1, v_conv_w_dw, v_conv_b_dw,
          v_conv_ln_g, v_conv_ln_b, v_conv_w_pw2, v_conv_b_pw2, v_kv_ada_w, v_kv_ada_b, v_kv_norm_g, v_w_kv_a,
          v_kv_a_norm_g, v_w_kv_b, v_w_q_a, v_q_a_norm_g, v_w_q_b, v_w_o, v_final_norm_g]
    grads = [g.reshape(w.shape) for g, w in zip(grads, weights)]
    deltas, new_m, new_v = [], [], []
    for w, g, m, v in zip(weights, grads, ms, vs):
        d, nm, nv = adamw(w, g, m, v)
        deltas.append(d)
        new_m.append(nm)
        new_v.append(nv)
    return (loss, grad_x, *grads, *deltas, *new_m, *new_v)
```

```python
import jax
import jax.numpy as jnp
from jax import lax
from jax.experimental import pallas as pl
from jax.experimental.pallas import tpu as pltpu

F32 = jnp.float32
BF16 = jnp.bfloat16
MESH = pl.DeviceIdType.MESH

N_HEADS = 16
QK_NOPE = 64
QK_ROPE = 32
V_HEAD = 64
KV_LORA = 256
CONV_WIDTH = 31
ROPE_THETA = 10000.0
EPS = 1e-6
N_MOD = 9
HEAD_PAD = 128
ATTN_TILE = 512
CONV_HALO = 32

ADAM_LR = 0.001
ADAM_B1 = 0.9
ADAM_B2 = 0.999
ADAM_EPS = 1e-08
ADAM_WD = 0.01
ADAM_STEP = 10

VMEM_LIMIT_BYTES = 56 * 2 ** 20
ROW_TILE_BUDGET = 10 * 2 ** 20
MM_VMEM_BUDGET = 40 * 2 ** 20
NEG = float(jnp.finfo(jnp.float32).min)
LOG2_E = 1.4426950408889634


def _tile(n, prefs):
    for t in prefs:
        if n % t == 0:
            return t
    return n


def _params(sem):
    return pltpu.CompilerParams(dimension_semantics=sem, vmem_limit_bytes=VMEM_LIMIT_BYTES)


def _mm_tiles(M, N, K, mode, a_bytes, b_bytes, o_bytes):
    if mode == "tn":
        tk_opts = [t for t in (2048, 1024, 512, 256, 128) if K % t == 0] or [K]
        tm_opts = ([M] if M <= 2816 else []) + [t for t in (1024, 512, 256, 128) if M % t == 0 and t < M]
    else:
        tk_opts = [K]
        tm_opts = [t for t in (1024, 512, 256, 128) if M % t == 0] or [M]
    tn_opts = [t for t in (1408, 1024, 512, 384, 256, 128) if N % t == 0] or [N]

    def need(tm, tn, tk):
        blocks = 2 * (tm * tk * a_bytes + tk * tn * b_bytes + tm * tn * o_bytes)
        return blocks + (tm * tn * 4 if mode == "tn" else 0)

    tk_floor = next((t for t in tk_opts if t <= 512), tk_opts[-1])
    for tm in tm_opts:
        for tn in tn_opts:
            if need(tm, tn, tk_floor) <= MM_VMEM_BUDGET:
                return tm, tn, next(tk for tk in tk_opts if need(tm, tn, tk) <= MM_VMEM_BUDGET)
    return tm_opts[-1], tn_opts[-1], tk_opts[-1]


def mm(a, b, mode, name, out_dtype=F32, bias=None):
    if mode == "nn":
        (M, K), (K2, N) = a.shape, b.shape
        dims = (((1,), (0,)), ((), ()))
    elif mode == "nt":
        (M, K), (N, K2) = a.shape, b.shape
        dims = (((1,), (1,)), ((), ()))
    else:
        (K, M), (K2, N) = a.shape, b.shape
        dims = (((0,), (0,)), ((), ()))
    assert K == K2, (a.shape, b.shape, mode)
    tm, tn, tk = _mm_tiles(M, N, K, mode, a.dtype.itemsize, b.dtype.itemsize, jnp.dtype(out_dtype).itemsize)
    nk = K // tk
    if mode == "tn":
        a_spec = pl.BlockSpec((tk, tm), lambda i, j, k: (k, i))
        b_spec = pl.BlockSpec((tk, tn), lambda i, j, k: (k, j))
    elif mode == "nn":
        a_spec = pl.BlockSpec((tm, tk), lambda i, j, k: (i, k))
        b_spec = pl.BlockSpec((tk, tn), lambda i, j, k: (k, j))
    else:
        a_spec = pl.BlockSpec((tm, tk), lambda i, j, k: (i, k))
        b_spec = pl.BlockSpec((tn, tk), lambda i, j, k: (j, k))
    in_specs = [a_spec, b_spec]
    operands = [a, b]
    if bias is not None:
        in_specs.append(pl.BlockSpec((1, tn), lambda i, j, k: (0, j)))
        operands.append(bias)
    has_bias = bias is not None

    def body(*refs):
        a_ref, b_ref = refs[0], refs[1]
        bias_ref = refs[2] if has_bias else None
        o_ref = refs[3] if has_bias else refs[2]
        prod = lax.dot_general(a_ref[...].astype(BF16), b_ref[...].astype(BF16), dims,
                               preferred_element_type=F32)
        if nk == 1:
            if has_bias:
                prod = prod + bias_ref[...]
            o_ref[...] = prod.astype(o_ref.dtype)
        else:
            acc_ref = refs[-1]
            k = pl.program_id(2)

            @pl.when(k == 0)
            def _():
                acc_ref[...] = jnp.zeros_like(acc_ref)

            acc_ref[...] += prod

            @pl.when(k == nk - 1)
            def _():
                out = acc_ref[...]
                if has_bias:
                    out = out + bias_ref[...]
                o_ref[...] = out.astype(o_ref.dtype)

    return pl.pallas_call(
        body, name=name,
        grid=(M // tm, N // tn, nk),
        in_specs=in_specs,
        out_specs=pl.BlockSpec((tm, tn), lambda i, j, k: (i, j)),
        out_shape=jax.ShapeDtypeStruct((M, N), out_dtype),
        scratch_shapes=[pltpu.VMEM((tm, tn), F32)] if nk > 1 else [],
        compiler_params=_params(("parallel", "parallel", "arbitrary")),
    )(*operands)


def mm_fused(a, b, mode, name, tn, epi, epi_outs, pro=None, pro_rows=(), pro_vecs=(), pro_out=False, n_pro_sums=0,
             epi_rows=(), epi_vecs=(), b_blocks=None, n_cols=None):
    M, K = a.shape
    if b_blocks is not None:
        n_b, N = len(b_blocks), n_cols
    else:
        n_b = b.shape[0] if b.ndim == 3 else 1
        N = b.shape[-1] if mode == "nn" else b.shape[0]
    dims = (((1,), (0,)), ((), ())) if mode == "nn" else (((1,), (1,)), ((), ()))
    nj = N // tn
    epi_outs = [o if len(o) == 3 else (*o, None) for o in epi_outs]
    row_bytes = 2 * (K * a.dtype.itemsize + sum(K * r.dtype.itemsize for r in pro_rows) + (2 * K if pro_out else 0)
                     + sum(w * r.dtype.itemsize * (r.shape[0] if r.ndim == 3 else 1) for r, w in epi_rows)
                     + sum(w * jnp.dtype(dt).itemsize * (L or 1) for w, dt, L in epi_outs)
                     ) + (2 * K if pro is not None else 0)
    fixed = 2 * n_b * K * tn * b.dtype.itemsize
    tm = next((t for t in (1024, 512, 256, 128) if M % t == 0 and t * row_bytes + fixed <= MM_VMEM_BUDGET), M)
    row = lambda i, j: (i, 0)
    tile = lambda i, j: (i, j)
    stack = lambda i, j: (0, i, j)
    in_specs = [pl.BlockSpec((tm, K), row)] + [pl.BlockSpec((tm, K), row) for _ in pro_rows]
    in_specs += [pl.BlockSpec(v.shape, lambda i, j: (0, 0)) for v in pro_vecs]
    if b_blocks is not None:
        in_specs += [pl.BlockSpec(shape, imap) for shape, imap in b_blocks]
    elif b.ndim == 3:
        in_specs += [pl.BlockSpec((None, K, tn), lambda i, j, h=h: (h, 0, j)) for h in range(n_b)]
    elif mode == "nn":
        in_specs += [pl.BlockSpec((K, tn), lambda i, j: (0, j))]
    else:
        in_specs += [pl.BlockSpec((tn, K), lambda i, j: (j, 0))]
    in_specs += [pl.BlockSpec((r.shape[0], tm, w), stack) if r.ndim == 3 else pl.BlockSpec((tm, w), tile)
                 for r, w in epi_rows]
    in_specs += [pl.BlockSpec((1, tn), lambda i, j: (0, j)) for _ in epi_vecs]
    out_specs, out_shape = [], []
    if pro_out:
        out_specs.append(pl.BlockSpec((tm, K), row))
        out_shape.append(jax.ShapeDtypeStruct((M, K), BF16))
    for _ in range(n_pro_sums):
        out_specs.append(pl.BlockSpec((1, K), lambda i, j: (0, 0)))
        out_shape.append(jax.ShapeDtypeStruct((1, K), F32))
    for w, dt, L in epi_outs:
        out_specs.append(pl.BlockSpec((tm, w), tile) if L is None else pl.BlockSpec((L, tm, w), stack))
        out_shape.append(jax.ShapeDtypeStruct((M, nj * w) if L is None else (L, M, nj * w), dt))
    n_pr, n_pv, n_er, n_ev = len(pro_rows), len(pro_vecs), len(epi_rows), len(epi_vecs)
    n_a = 1 + n_pr + n_pv
    n_in = n_a + n_b + n_er + n_ev
    n_po = 1 if pro_out else 0

    def body(*refs):
        i, j = pl.program_id(0), pl.program_id(1)
        a_ref = refs[0]
        outs = refs[n_in:]
        if pro is not None:
            lhs_ref = refs[-1]

            @pl.when(j == 0)
            def _():
                res = pro(*[r[...] for r in refs[:1 + n_pr + n_pv]])
                if not isinstance(res, (tuple, list)):
                    res = (res,)
                lhs_ref[...] = res[0]
                if pro_out:
                    outs[0][...] = res[0]
                for s_ref, val in zip(outs[n_po:n_po + n_pro_sums], res[1:]):
                    part = jnp.sum(val.astype(F32), axis=0, keepdims=True)

                    @pl.when(i == 0)
                    def _(s_ref=s_ref, part=part):
                        s_ref[...] = part

                    @pl.when(i != 0)
                    def _(s_ref=s_ref, part=part):
                        s_ref[...] += part

            lhs = lhs_ref[...]
        else:
            lhs = a_ref[...].astype(BF16)
        accs = [lax.dot_general(lhs, b_ref[...].astype(BF16), dims, preferred_element_type=F32)
                for b_ref in refs[n_a:n_a + n_b]]
        res = epi(*accs, *[r[...] for r in refs[n_a + n_b:n_in]])
        if not isinstance(res, (tuple, list)):
            res = (res,)
        for o_ref, val in zip(outs[n_po + n_pro_sums:], res):
            if isinstance(val, (tuple, list)):
                for h, part in enumerate(val):
                    o_ref[h] = part.astype(o_ref.dtype)
            else:
                o_ref[...] = val.astype(o_ref.dtype)

    return pl.pallas_call(
        body, name=name,
        grid=(M // tm, nj),
        in_specs=in_specs, out_specs=out_specs, out_shape=out_shape,
        scratch_shapes=[pltpu.VMEM((tm, K), BF16)] if pro is not None else [],
        compiler_params=_params(("arbitrary", "arbitrary")),
    )(a, *pro_rows, *pro_vecs, *([b] * n_b), *[r for r, _ in epi_rows], *epi_vecs)


def rowwise(fn, rows, vecs, outs, sums, name, tm=None):
    norm = [(r, r.shape[1], 0) if not isinstance(r, tuple) else r for r in rows]
    S = norm[0][0].shape[0]
    if tm is None:
        per_row = sum(w * r.dtype.itemsize for r, w, _ in norm) + sum(n * jnp.dtype(dt).itemsize for n, dt in outs)
        tm = S
        for t in (512, 256, 128, 64, 32, 16, 8):
            if S % t == 0:
                tm = t
                if t * per_row <= ROW_TILE_BUDGET:
                    break
    n_rows, n_vecs, n_outs, n_sums = len(norm), len(vecs), len(outs), len(sums)
    in_specs = [pl.BlockSpec((tm, w), lambda i, cb=cb: (i, cb)) for _, w, cb in norm]
    in_specs += [pl.BlockSpec(v.shape, lambda i: (0, 0)) for v in vecs]
    out_specs = [pl.BlockSpec((tm, n), lambda i: (i, 0)) for n, _ in outs]
    out_specs += [pl.BlockSpec((1, n), lambda i: (0, 0)) for n in sums]
    out_shape = [jax.ShapeDtypeStruct((S, n), dt) for n, dt in outs]
    out_shape += [jax.ShapeDtypeStruct((1, n), F32) for n in sums]

    def body(*refs):
        ins = [r[...] for r in refs[:n_rows + n_vecs]]
        res = fn(*ins)
        if not isinstance(res, (tuple, list)):
            res = (res,)
        out_refs = refs[n_rows + n_vecs:]
        for o_ref, val in zip(out_refs[:n_outs], res[:n_outs]):
            o_ref[...] = val.astype(o_ref.dtype)
        if n_sums:
            i = pl.program_id(0)
            for s_ref, val in zip(out_refs[n_outs:], res[n_outs:]):
                part = jnp.sum(val.astype(F32), axis=0, keepdims=True)

                @pl.when(i == 0)
                def _(s_ref=s_ref, part=part):
                    s_ref[...] = part

                @pl.when(i != 0)
                def _(s_ref=s_ref, part=part):
                    s_ref[...] += part

    res = pl.pallas_call(
        body, name=name,
        grid=(S // tm,),
        in_specs=in_specs, out_specs=out_specs, out_shape=out_shape,
        compiler_params=_params(("arbitrary",) if n_sums else ("parallel",)),
    )(*[r for r, _, _ in norm], *vecs)
    return res


def _sigmoid(x):
    return jax.nn.sigmoid(x)


def _rms(x):
    r = lax.rsqrt(jnp.mean(x * x, axis=-1, keepdims=True) + EPS)
    return x * r, r


def _rms_bwd(xhat, r, dxhat):
    return r * (dxhat - xhat * jnp.mean(dxhat * xhat, axis=-1, keepdims=True))


def norm_mod(h, g, sh, sc, name):
    def f(h, g, sh, sc):
        xhat, _ = _rms(h)
        return ((xhat * g) * (1 + sc) + sh).astype(BF16)
    return rowwise(f, [h], [g, sh, sc], [(h.shape[1], BF16)], [], name)[0]


def norm_mod_bwd(h, dhn, dh_out, g, sc, name):
    D = h.shape[1]
    with_res = dh_out is not None

    def f(*a):
        if with_res:
            h, dhn, dres, g, sc = a
        else:
            h, dhn, g, sc = a
        xhat, r = _rms(h)
        xn = xhat * g
        dxn = dhn * (1 + sc)
        dh = _rms_bwd(xhat, r, dxn * g)
        if with_res:
            dh = dh + dres
        return dh, dhn, dhn * xn, dxn * xhat

    rows = [h, dhn] + ([dh_out] if with_res else [])
    return rowwise(f, rows, [g, sc], [(D, F32)], [D, D, D], name)


def residual(h, y, gate, coef, name, bias=None):
    D = h.shape[1]
    if bias is None:
        def f(h, y, gate):
            return h + (coef * gate) * y
        return rowwise(f, [h, y], [gate], [(D, F32)], [], name)[0], y

    def fb(h, y, gate, bias):
        yb = y + bias
        return h + (coef * gate) * yb, yb
    return rowwise(fb, [h, y], [gate, bias], [(D, F32), (D, F32)], [], name)


def residual_bwd(dh_out, y, gate, coef, name, with_bias_sum=False):
    D = y.shape[1]

    def f(dh, y, gate):
        dy = (coef * gate) * dh
        res = (dy.astype(BF16), coef * dh * y)
        return res + ((dy,) if with_bias_sum else ())
    return rowwise(f, [dh_out, y], [gate], [(D, BF16)], [D, D] if with_bias_sum else [D], name)


def ffn_w13_dx(dab, gw13, l, i):
    _, S, F = dab.shape
    D, C = gw13.shape[3:]
    tm = _tile(S, (1024, 512, 256, 128))
    nt = (((1,), (1,)), ((), ()))

    def body(a_ref, b_ref, o_ref, acc_ref):
        k = pl.program_id(1)
        prod = lax.dot_general(a_ref[...], b_ref[...], nt, preferred_element_type=F32)

        @pl.when(k == 0)
        def _():
            acc_ref[...] = prod

        @pl.when((k > 0) & (k < 3))
        def _():
            acc_ref[...] += prod

        @pl.when(k == 3)
        def _():
            o_ref[...] = acc_ref[...] + prod

    return pl.pallas_call(
        body, name="ffn_w13_dx",
        grid=(S // tm, 4),
        in_specs=[pl.BlockSpec((None, tm, C), lambda r, k: (k // 2, r, k % 2)),
                  pl.BlockSpec((None, None, None, D, C), lambda r, k: (k, l, i, 0, 0))],
        out_specs=pl.BlockSpec((tm, D), lambda r, k: (r, 0)),
        out_shape=jax.ShapeDtypeStruct((S, D), F32),
        scratch_shapes=[pltpu.VMEM((tm, D), F32)],
        compiler_params=_params(("parallel", "arbitrary")),
    )(dab, gw13)


def ffn_w13_grad(hn, dab):
    S, D = hn.shape
    F = dab.shape[2]
    C = F // 2
    tk = next(t for t in (2048, 1024, 512, 256, 128) if S % t == 0)
    tn_dims = (((0,), (0,)), ((), ()))
    nk = S // tk

    def body(a_ref, b_ref, o_ref, acc_ref):
        k = pl.program_id(1)

        @pl.when(k == 0)
        def _():
            acc_ref[...] = jnp.zeros_like(acc_ref)

        acc_ref[...] += lax.dot_general(a_ref[...], b_ref[...], tn_dims, preferred_element_type=F32)

        @pl.when(k == nk - 1)
        def _():
            o_ref[...] = acc_ref[...]

    return pl.pallas_call(
        body, name="ffn_w13_dw",
        grid=(4, nk),
        in_specs=[pl.BlockSpec((tk, D), lambda j, k: (k, 0)),
                  pl.BlockSpec((None, tk, C), lambda j, k: (j // 2, k, j % 2))],
        out_specs=pl.BlockSpec((None, D, C), lambda j, k: (j, 0, 0)),
        out_shape=jax.ShapeDtypeStruct((4, D, C), F32),
        scratch_shapes=[pltpu.VMEM((D, C), F32)],
        compiler_params=_params(("parallel", "arbitrary")),
    )(hn, dab)


def ffn_fwd(h, g, sh, sc, gate, gw13, l, i, w2):
    F, D = w2.shape
    C = F // 2

    def norm(h, g, sh, sc):
        xhat, _ = _rms(h)
        return ((xhat * g) * (1 + sc) + sh).astype(BF16)

    def act(a, b):
        sig = _sigmoid(a)
        sa = a * sig
        return (b * (sig * (1 + a * (1 - sig))), sa), sa * b
    blocks = [((None, None, None, D, C), lambda r, j, half=half: (2 * half + j, l, i, 0, 0)) for half in range(2)]
    hn, dt_dab, t = mm_fused(h, gw13, "nn", "ffn_w13", C, act, [(C, BF16, 2), (C, BF16)],
                             pro=norm, pro_vecs=[g, sh, sc], pro_out=True, b_blocks=blocks, n_cols=F)

    def res(acc, h, gate):
        return h + (0.5 * gate) * acc, acc
    h_out, y = mm_fused(t, w2, "nn", "ffn_w2", D, res, [(D, F32), (D, F32)], epi_rows=[(h, D)], epi_vecs=[gate])
    return h_out, (h, hn, dt_dab, t, y)


def ffn_bwd(dh_out, saved, g, sc, gate, gw13, l, i, w2):
    h, hn, dt_dab, t, y = saved
    F, D = w2.shape
    C = F // 2

    def scale(dh, y, gate):
        return ((0.5 * gate) * dh).astype(BF16), 0.5 * dh * y

    def act_bwd(dt, f):
        return ((dt * f[0].astype(F32), dt * f[1].astype(F32)),)
    dy, d_gate, dab = mm_fused(dh_out, w2, "nt", "ffn_w2_dx", C, act_bwd, [(C, BF16, 2)],
                               pro=scale, pro_rows=[y], pro_vecs=[gate], pro_out=True, n_pro_sums=1,
                               epi_rows=[(dt_dab, C)])
    dw2 = mm(t, dy, "tn", "ffn_w2_dw")
    dw13 = ffn_w13_grad(hn, dab)
    dhn = ffn_w13_dx(dab, gw13, l, i)
    dh_in, d_sh, d_sc, d_g = norm_mod_bwd(h, dhn, dh_out, g, sc, "norm_mod_bwd")
    return dh_in, (d_sh, d_sc, d_gate, d_g), dw13, dw2


def _shifted(xbuf, n):
    return [xbuf] + [pltpu.roll(xbuf, n - b, 0) for b in range(1, 8)]


def conv_fwd(u, w_dw, b_dw, ln_g, ln_b):
    S, D = u.shape
    tm = _tile(S, (256, 128))
    rc = 32
    first_tap = CONV_HALO - (CONV_WIDTH - 1)
    w = jnp.concatenate([w_dw, jnp.zeros((CONV_HALO - CONV_WIDTH, D), F32)], axis=0)

    def body(cur_ref, prev_ref, w_ref, b_ref, g_ref, beta_ref, z_ref, s_ref):
        i = pl.program_id(0)
        prev = jnp.where(i == 0, jnp.zeros((CONV_HALO, D), F32), prev_ref[...])
        xs = _shifted(jnp.concatenate([prev, cur_ref[...]], axis=0), tm + CONV_HALO)
        for c0 in range(0, tm, rc):
            acc = jnp.zeros((rc, D), F32)
            for k in range(CONV_WIDTH):
                off = first_tap + k
                a8, b = off // 8 * 8, off % 8
                acc = acc + w_ref[k:k + 1, :] * xs[b][c0 + a8:c0 + a8 + rc, :]
            z_ref[c0:c0 + rc, :] = acc + b_ref[...]
        z = z_ref[...]
        mu = jnp.mean(z, axis=-1, keepdims=True)
        zc = z - mu
        r = lax.rsqrt(jnp.mean(zc * zc, axis=-1, keepdims=True) + EPS)
        un = zc * r * g_ref[...] + beta_ref[...]
        s_ref[...] = (un * _sigmoid(un)).astype(BF16)

    nb = tm // CONV_HALO
    vec = pl.BlockSpec((1, D), lambda i: (0, 0))
    return pl.pallas_call(
        body, name="conv_fwd",
        grid=(S // tm,),
        in_specs=[pl.BlockSpec((tm, D), lambda i: (i, 0)),
                  pl.BlockSpec((CONV_HALO, D), lambda i: (jnp.maximum(i * nb - 1, 0), 0)),
                  pl.BlockSpec((CONV_HALO, D), lambda i: (0, 0)), vec, vec, vec],
        out_specs=[pl.BlockSpec((tm, D), lambda i: (i, 0)), pl.BlockSpec((tm, D), lambda i: (i, 0))],
        out_shape=[jax.ShapeDtypeStruct((S, D), F32), jax.ShapeDtypeStruct((S, D), BF16)],
        compiler_params=_params(("parallel",)),
    )(u, u, w, b_dw, ln_g, ln_b)


def conv_bwd(dz, u, w_dw):
    S, D = u.shape
    tm = _tile(S, (256, 128))
    rc = 32
    first_tap = CONV_HALO - (CONV_WIDTH - 1)
    w = jnp.concatenate([w_dw, jnp.zeros((CONV_HALO - CONV_WIDTH, D), F32)], axis=0)
    n_tiles = S // tm
    nb = tm // CONV_HALO

    def body(dz_ref, dzn_ref, u_ref, up_ref, w_ref, du_ref, dw_ref):
        i = pl.program_id(0)
        nxt = jnp.where(i == n_tiles - 1, jnp.zeros((CONV_HALO, D), F32), dzn_ref[...])
        dzs = _shifted(jnp.concatenate([dz_ref[...], nxt], axis=0), tm + CONV_HALO)
        for c0 in range(0, tm, rc):
            acc = jnp.zeros((rc, D), F32)
            for m in range(CONV_WIDTH):
                a8, b = m // 8 * 8, m % 8
                acc = acc + w_ref[CONV_WIDTH - 1 - m:CONV_WIDTH - m, :] * dzs[b][c0 + a8:c0 + a8 + rc, :]
            du_ref[c0:c0 + rc, :] = acc
        prev = jnp.where(i == 0, jnp.zeros((CONV_HALO, D), F32), up_ref[...])
        us = _shifted(jnp.concatenate([prev, u_ref[...]], axis=0), tm + CONV_HALO)
        dz = dz_ref[...]

        @pl.when(i == 0)
        def _():
            dw_ref[...] = jnp.zeros_like(dw_ref)

        for k in range(CONV_WIDTH):
            off = first_tap + k
            a8, b = off // 8 * 8, off % 8
            dw_ref[k:k + 1, :] += jnp.sum(dz * us[b][a8:a8 + tm, :], axis=0, keepdims=True)

    last_blk = S // CONV_HALO - 1
    du, dw = pl.pallas_call(
        body, name="conv_bwd",
        grid=(n_tiles,),
        in_specs=[pl.BlockSpec((tm, D), lambda i: (i, 0)),
                  pl.BlockSpec((CONV_HALO, D), lambda i: (jnp.minimum((i + 1) * nb, last_blk), 0)),
                  pl.BlockSpec((tm, D), lambda i: (i, 0)),
                  pl.BlockSpec((CONV_HALO, D), lambda i: (jnp.maximum(i * nb - 1, 0), 0)),
                  pl.BlockSpec((CONV_HALO, D), lambda i: (0, 0))],
        out_specs=[pl.BlockSpec((tm, D), lambda i: (i, 0)), pl.BlockSpec((CONV_HALO, D), lambda i: (0, 0))],
        out_shape=[jax.ShapeDtypeStruct((S, D), F32), jax.ShapeDtypeStruct((CONV_HALO, D), F32)],
        compiler_params=_params(("arbitrary",)),
    )(dz, dz, u, u, w)
    return du, dw[:CONV_WIDTH]


def conv_module_fwd(h, g, sh, sc, gate, p):
    D = h.shape[1]
    hn = norm_mod(h, g, sh, sc, "conv_norm_mod")
    pre = mm(hn, p["w_pw1"], "nn", "conv_pw1")
    ba, bg = p["b_pw1"][:, :D], p["b_pw1"][:, D:]

    def glu(a, gt, ba, bg):
        return (a + ba) * _sigmoid(gt + bg)
    u = rowwise(glu, [(pre, D, 0), (pre, D, 1)], [ba, bg], [(D, F32)], [], "conv_glu")[0]
    z, s = conv_fwd(u, p["w_dw"], p["b_dw"], p["ln_g"], p["ln_b"])
    yraw = mm(s, p["w_pw2"], "nn", "conv_pw2")
    h_out, y = residual(h, yraw, gate, 1.0, "conv_residual", bias=p["b_pw2"])
    return h_out, (h, hn, pre, u, z, s, y)


def conv_module_bwd(dh_out, saved, g, sc, gate, p):
    h, hn, pre, u, z, s, y = saved
    D = h.shape[1]
    dy, d_gate, d_b_pw2 = residual_bwd(dh_out, y, gate, 1.0, "conv_residual_bwd", with_bias_sum=True)
    d_w_pw2 = mm(s, dy, "tn", "conv_pw2_dw")
    ds = mm(dy, p["w_pw2"], "nt", "conv_pw2_dx")

    def ln_bwd(z, ds, g, beta):
        mu = jnp.mean(z, axis=-1, keepdims=True)
        zc = z - mu
        r = lax.rsqrt(jnp.mean(zc * zc, axis=-1, keepdims=True) + EPS)
        xhat = zc * r
        un = xhat * g + beta
        sig = _sigmoid(un)
        d_un = ds * (sig * (1 + un * (1 - sig)))
        dxhat = d_un * g
        dz = r * (dxhat - jnp.mean(dxhat, axis=-1, keepdims=True)
                  - xhat * jnp.mean(dxhat * xhat, axis=-1, keepdims=True))
        return dz, d_un * xhat, d_un, dz
    dz, d_ln_g, d_ln_b, d_b_dw = rowwise(ln_bwd, [z, ds], [p["ln_g"], p["ln_b"]], [(D, F32)], [D, D, D],
                                         "conv_ln_bwd")
    du, d_w_dw = conv_bwd(dz, u, p["w_dw"])
    ba, bg = p["b_pw1"][:, :D], p["b_pw1"][:, D:]

    def glu_bwd(a, gt, du, ba, bg):
        sg = _sigmoid(gt + bg)
        da = du * sg
        dg = du * (a + ba) * (sg * (1 - sg))
        dpre = jnp.concatenate([da, dg], axis=1)
        return dpre.astype(BF16), dpre
    dpre, d_b_pw1 = rowwise(glu_bwd, [(pre, D, 0), (pre, D, 1), du], [ba, bg], [(2 * D, BF16)], [2 * D],
                            "conv_glu_bwd")
    d_w_pw1 = mm(hn, dpre, "tn", "conv_pw1_dw")
    dhn = mm(dpre, p["w_pw1"], "nt", "conv_pw1_dx")
    dh_in, d_sh, d_sc, d_g = norm_mod_bwd(h, dhn, dh_out, g, sc, "norm_mod_bwd")
    grads = dict(w_pw1=d_w_pw1, b_pw1=d_b_pw1, w_dw=d_w_dw, b_dw=d_b_dw, ln_g=d_ln_g, ln_b=d_ln_b,
                 w_pw2=d_w_pw2, b_pw2=d_b_pw2)
    return dh_in, (d_sh, d_sc, d_gate, d_g), grads


def _rope(x, c, s1, s2):
    n = x.shape[1]
    return x * c + pltpu.roll(x, n - QK_ROPE // 2, 1) * s1 + pltpu.roll(x, QK_ROPE // 2, 1) * s2


def _rope_t(dy, c, s1, s2):
    n = dy.shape[1]
    return dy * c + pltpu.roll(dy * s1, QK_ROPE // 2, 1) + pltpu.roll(dy * s2, n - QK_ROPE // 2, 1)


def rope_tables(positions):
    inv_freq = ROPE_THETA ** (-jnp.arange(0, QK_ROPE, 2, dtype=F32) / QK_ROPE)
    ang = positions.astype(F32)[:, None] * inv_freq
    cos, sin = jnp.cos(ang), jnp.sin(ang)
    S = positions.shape[0]
    one = jnp.ones((S, QK_NOPE), F32)
    z16 = jnp.zeros((S, QK_ROPE // 2), F32)
    zn = jnp.zeros((S, QK_NOPE), F32)
    zt = jnp.zeros((S, HEAD_PAD - QK_NOPE - QK_ROPE), F32)
    c = jnp.concatenate([one, cos, cos, zt], axis=1)
    s1 = jnp.concatenate([zn, -sin, z16, zt], axis=1)
    s2 = jnp.concatenate([zn, z16, sin, zt], axis=1)
    return c, s1, s2


def attn_fwd(qr, kv, kpe, n_heads):
    S = qr.shape[0]
    H = n_heads
    tk = _tile(S, (ATTN_TILE,))
    nk = S // tk
    w = 2 if nk % 2 == 0 else 1
    tq = w * tk
    c2 = (QK_NOPE + QK_ROPE) ** -0.5 * LOG2_E
    nt = (((1,), (1,)), ((), ()))

    def body(q_ref, k_ref, v_ref, kpe_ref, o_ref, lse_ref, kf_ref, vt_ref, m_ref, l_ref, acc_ref):
        qi = pl.program_id(1)

        @pl.when(qi == 0)
        def _():
            kf_ref[...] = k_ref[...] + kpe_ref[...]
            for c in range(nk):
                vt_ref[c] = jnp.transpose(v_ref[c * tk:(c + 1) * tk, :].astype(F32)).astype(BF16)

        q = q_ref[...]
        m_ref[...] = jnp.full((1, tq), -jnp.inf, F32)
        l_ref[...] = jnp.zeros((1, tq), F32)
        acc_ref[...] = jnp.zeros((HEAD_PAD, tq), F32)

        def tile(j, first_visible):
            k = kf_ref[pl.ds(pl.multiple_of(j * tk, tk), tk), :]
            t = lax.dot_general(k, q, nt, preferred_element_type=F32) * c2
            if first_visible is not None:
                krow = lax.broadcasted_iota(jnp.int32, (tk, tq), 0)
                qcol = lax.broadcasted_iota(jnp.int32, (tk, tq), 1)
                t = jnp.where(krow + first_visible <= qcol, t, NEG)
            m_old = m_ref[...]
            m_new = jnp.maximum(m_old, jnp.max(t, axis=0, keepdims=True))
            alpha = jnp.exp2(m_old - m_new)
            p = jnp.exp2(t - m_new)
            l_ref[...] = alpha * l_ref[...] + jnp.sum(p, axis=0, keepdims=True)
            acc_ref[...] = alpha * acc_ref[...] + jnp.dot(vt_ref[j], p.astype(BF16), preferred_element_type=F32)
            m_ref[...] = m_new

        def unmasked(j, carry):
            tile(j, None)
            return carry

        lax.fori_loop(0, w * qi, unmasked, 0)
        for u in range(w):
            tile(w * qi + u, u * tk)
        l = l_ref[...]
        o_ref[...] = jnp.transpose(acc_ref[...] / l)
        lse = m_ref[...] + jnp.log(l) * LOG2_E
        for u in range(w):
            lse_ref[u] = lse[:, u * tk:(u + 1) * tk]

    return pl.pallas_call(
        body, name="attn_fwd",
        grid=(H, S // tq),
        in_specs=[pl.BlockSpec((tq, HEAD_PAD), lambda h, i: (i, h)),
                  pl.BlockSpec((S, HEAD_PAD), lambda h, i: (0, h)),
                  pl.BlockSpec((S, HEAD_PAD), lambda h, i: (0, H + h)),
                  pl.BlockSpec((S, HEAD_PAD), lambda h, i: (0, 0))],
        out_specs=[pl.BlockSpec((tq, HEAD_PAD), lambda h, i: (i, h)),
                   pl.BlockSpec((None, w, 1, tk), lambda h, i: (h, i, 0, 0))],
        out_shape=[jax.ShapeDtypeStruct((S, H * HEAD_PAD), F32), jax.ShapeDtypeStruct((H, nk, 1, tk), F32)],
        scratch_shapes=[pltpu.VMEM((S, HEAD_PAD), BF16), pltpu.VMEM((nk, HEAD_PAD, tk), BF16),
                        pltpu.VMEM((1, tq), F32), pltpu.VMEM((1, tq), F32), pltpu.VMEM((HEAD_PAD, tq), F32)],
        compiler_params=_params(("parallel", "arbitrary")),
    )(qr, kv, kv, kpe)


def attn_delta(o, do, n_heads):
    S = o.shape[0]
    H = n_heads
    tq = _tile(S, (ATTN_TILE,))
    nq = S // tq

    def body(o_ref, do_ref, d_ref):
        for c in range(nq):
            rows = slice(c * tq, (c + 1) * tq)
            prod = o_ref[rows, :] * do_ref[rows, :].astype(F32)
            d_ref[c] = jnp.sum(jnp.transpose(prod), axis=0, keepdims=True)

    return pl.pallas_call(
        body, name="attn_delta",
        grid=(H,),
        in_specs=[pl.BlockSpec((S, HEAD_PAD), lambda h: (0, h)), pl.BlockSpec((S, HEAD_PAD), lambda h: (0, h))],
        out_specs=pl.BlockSpec((None, nq, 1, tq), lambda h: (h, 0, 0, 0)),
        out_shape=jax.ShapeDtypeStruct((H, nq, 1, tq), F32),
        compiler_params=_params(("parallel",)),
    )(o, do)


def attn_bwd(qr, kv, kpe, do, lse2, delta, n_heads):
    S = qr.shape[0]
    H = n_heads
    tk = _tile(S, (ATTN_TILE,))
    nk = S // tk
    w = 2 if nk % 2 == 0 else 1
    tq = w * tk
    nq = S // tq
    scale = (QK_NOPE + QK_ROPE) ** -0.5
    c2 = scale * LOG2_E
    nt = (((1,), (1,)), ((), ()))
    lse2 = lse2.reshape(H, nq, 1, tq)
    delta4 = delta.reshape(H, nq, 1, tq)

    def body(k_ref, v_ref, kpe_ref, q_ref, do_ref, lse_ref, dl_ref, dq_ref, dk_ref, dv_ref, dka_ref, dva_ref,
             dqt_ref):
        kj = pl.program_id(1)
        k = k_ref[...] + kpe_ref[...]
        kt = jnp.transpose(k.astype(F32)).astype(BF16)
        v = v_ref[...]

        @pl.when(kj == 0)
        def _():
            dqt_ref[...] = jnp.zeros_like(dqt_ref)

        dka_ref[...] = jnp.zeros_like(dka_ref)
        dva_ref[...] = jnp.zeros_like(dva_ref)

        def tile(i, masked):
            start = pl.multiple_of(i * tq, tq)
            q = q_ref[pl.ds(start, tq), :]
            do = do_ref[pl.ds(start, tq), :]
            t = lax.dot_general(k, q, nt, preferred_element_type=F32) * c2
            if masked:
                krow = lax.broadcasted_iota(jnp.int32, (tk, tq), 0)
                qcol = lax.broadcasted_iota(jnp.int32, (tk, tq), 1)
                t = jnp.where(krow + (kj % w) * tk <= qcol, t, NEG)
            pt = jnp.exp2(t - lse_ref[i])
            dva_ref[...] += jnp.dot(pt.astype(BF16), do, preferred_element_type=F32)
            dpt = lax.dot_general(v, do, nt, preferred_element_type=F32)
            dst = (pt * (dpt - dl_ref[i]) * scale).astype(BF16)
            dka_ref[...] += jnp.dot(dst, q, preferred_element_type=F32)
            dqt_ref[i] += jnp.dot(kt, dst, preferred_element_type=F32)

        tile(kj // w, True)

        def unmasked(i, carry):
            tile(i, False)
            return carry

        lax.fori_loop(kj // w + 1, nq, unmasked, 0)
        dk_ref[...] = dka_ref[...]
        dv_ref[...] = dva_ref[...]

        @pl.when(kj == nk - 1)
        def _():
            for c in range(nq):
                dq_ref[c * tq:(c + 1) * tq, :] = jnp.transpose(dqt_ref[c])

    blk = pl.BlockSpec((tk, HEAD_PAD), lambda h, j: (j, h))
    whole = pl.BlockSpec((S, HEAD_PAD), lambda h, j: (0, h))
    stat = pl.BlockSpec((None, nq, 1, tq), lambda h, j: (h, 0, 0, 0))
    shp = jax.ShapeDtypeStruct((S, H * HEAD_PAD), F32)
    return pl.pallas_call(
        body, name="attn_bwd",
        grid=(H, nk),
        in_specs=[blk, pl.BlockSpec((tk, HEAD_PAD), lambda h, j: (j, H + h)),
                  pl.BlockSpec((tk, HEAD_PAD), lambda h, j: (j, 0)), whole, whole, stat, stat],
        out_specs=[whole, blk, blk],
        out_shape=[shp, shp, shp],
        scratch_shapes=[pltpu.VMEM((tk, HEAD_PAD), F32), pltpu.VMEM((tk, HEAD_PAD), F32),
                        pltpu.VMEM((nq, HEAD_PAD, tq), F32)],
        compiler_params=_params(("parallel", "arbitrary")),
    )(kv, kv, kpe, qr, do, lse2, delta4)


def _pad_heads(w, width):
    R = w.shape[0]
    w3 = w.reshape(R, -1, width)
    return jnp.pad(w3, ((0, 0), (0, 0), (0, HEAD_PAD - width))).reshape(R, -1)


def _unpad_heads(w, width):
    R = w.shape[0]
    return w.reshape(R, -1, HEAD_PAD)[:, :, :width].reshape(R, -1)


def mla_pad_weights(p):
    H = N_HEADS
    w_q_b = _pad_heads(p["w_q_b"], QK_NOPE + QK_ROPE)
    kvb = p["w_kv_b"].reshape(KV_LORA, H, QK_NOPE + V_HEAD)
    wk = _pad_heads(kvb[:, :, :QK_NOPE].reshape(KV_LORA, -1), QK_NOPE)
    wv = _pad_heads(kvb[:, :, QK_NOPE:].reshape(KV_LORA, -1), V_HEAD)
    D = p["w_kv_a"].shape[0]
    a = p["w_kv_a"]
    w_kv_a = jnp.concatenate([a[:, :KV_LORA], jnp.zeros((D, QK_NOPE), a.dtype), a[:, KV_LORA:],
                              jnp.zeros((D, HEAD_PAD - QK_NOPE - QK_ROPE), a.dtype)], axis=1)
    wo = p["w_o"].reshape(H, V_HEAD, -1)
    w_o = jnp.pad(wo, ((0, 0), (0, HEAD_PAD - V_HEAD), (0, 0))).reshape(H * HEAD_PAD, -1)
    return dict(w_q_a=p["w_q_a"], w_q_b=w_q_b, w_kv_b=jnp.concatenate([wk, wv], axis=1), w_kv_a=w_kv_a, w_o=w_o)


def mla_kv_fwd(h, g, sh, sc, kv_a_norm_g, pw, tabs):
    hkv = norm_mod(h, g, sh, sc, "kv_norm_mod")
    ckvp = mm(hkv, pw["w_kv_a"], "nn", "kv_a")

    def f(ckv, kpe, c, s1, s2, g):
        xhat, _ = _rms(ckv)
        return (xhat * g).astype(BF16), _rope(kpe, c, s1, s2).astype(BF16)
    ckv_n, kpe_r = rowwise(f, [(ckvp, KV_LORA, 0), (ckvp, HEAD_PAD, KV_LORA // HEAD_PAD), *tabs], [kv_a_norm_g],
                           [(KV_LORA, BF16), (HEAD_PAD, BF16)], [], "kv_a_norm_rope")
    kv = mm(ckv_n, pw["w_kv_b"], "nn", "kv_b", out_dtype=BF16)
    return kv, kpe_r, (h, hkv, ckvp, ckv_n)


def mla_kv_bwd(dh_stream, dk, dv, saved, g, sc, kv_a_norm_g, pw, tabs):
    h, hkv, ckvp, ckv_n = saved
    H = N_HEADS
    lane = jnp.arange(HEAD_PAD)
    pe_mask = ((lane >= QK_NOPE) & (lane < QK_NOPE + QK_ROPE)).astype(F32)[None, :]

    def f(dk, dv, c, s1, s2, mask):
        tot = dk[:, :HEAD_PAD]
        for hh in range(1, H):
            tot = tot + dk[:, hh * HEAD_PAD:(hh + 1) * HEAD_PAD]
        dkpe = _rope_t(tot * mask, c, s1, s2) * mask
        return jnp.concatenate([dk, dv], axis=1).astype(BF16), dkpe
    dkv, dkpe = rowwise(f, [dk, dv, *tabs], [pe_mask], [(2 * H * HEAD_PAD, BF16), (HEAD_PAD, F32)], [],
                        "kv_split_bwd")
    d_w_kv_b = mm(ckv_n, dkv, "tn", "kv_b_dw")
    dckv_n = mm(dkv, pw["w_kv_b"], "nt", "kv_b_dx")

    def f2(ckv, dn, dkpe, g):
        xhat, r = _rms(ckv)
        dx = _rms_bwd(xhat, r, dn * g)
        return jnp.concatenate([dx, dkpe], axis=1).astype(BF16), dn * xhat
    dckvp, d_kv_a_g = rowwise(f2, [(ckvp, KV_LORA, 0), dckv_n, dkpe], [kv_a_norm_g],
                              [(KV_LORA + HEAD_PAD, BF16)], [KV_LORA], "kv_a_norm_bwd")
    d_w_kv_a = mm(hkv, dckvp, "tn", "kv_a_dw")
    dhkv = mm(dckvp, pw["w_kv_a"], "nt", "kv_a_dx")
    dh, d_sh, d_sc, d_g = norm_mod_bwd(h, dhkv, dh_stream, g, sc, "norm_mod_bwd")
    return dh, (d_sh, d_sc, d_g), d_kv_a_g, d_w_kv_a, d_w_kv_b


def mla_fwd(h, g, sh, sc, gate, q_a_norm_g, pw, kv, kpe_r, tabs):
    H = N_HEADS
    hn = norm_mod(h, g, sh, sc, "mla_norm_mod")
    qa = mm(hn, pw["w_q_a"], "nn", "q_a")

    def f(qa, g):
        xhat, _ = _rms(qa)
        return (xhat * g).astype(BF16)
    qa_n = rowwise(f, [qa], [q_a_norm_g], [(qa.shape[1], BF16)], [], "q_a_norm")[0]
    qp = mm(qa_n, pw["w_q_b"], "nn", "q_b")

    def frope(q, c, s1, s2):
        return jnp.concatenate([_rope(q[:, hh * HEAD_PAD:(hh + 1) * HEAD_PAD], c, s1, s2) for hh in range(H)],
                               axis=1).astype(BF16)
    qr = rowwise(frope, [qp, *tabs], [], [(H * HEAD_PAD, BF16)], [], "q_rope")[0]
    o, lse = attn_fwd(qr, kv, kpe_r, H)
    y = mm(o, pw["w_o"], "nn", "w_o")
    h_out, _ = residual(h, y, gate, 1.0, "mla_residual")
    return h_out, (h, hn, qa, qa_n, qr, o, lse, y)


def mla_bwd(dh_out, saved, g, sc, gate, q_a_norm_g, pw, kv, kpe_r, tabs):
    h, hn, qa, qa_n, qr, o, lse, y = saved
    H = N_HEADS
    dy, d_gate = residual_bwd(dh_out, y, gate, 1.0, "mla_residual_bwd")
    d_w_o = mm(o, dy, "tn", "w_o_dw")
    do = mm(dy, pw["w_o"], "nt", "w_o_dx", out_dtype=BF16)
    delta = attn_delta(o, do, H)
    dqr, dk, dv = attn_bwd(qr, kv, kpe_r, do, lse, delta, H)

    def frope_t(dq, c, s1, s2):
        return jnp.concatenate([_rope_t(dq[:, hh * HEAD_PAD:(hh + 1) * HEAD_PAD], c, s1, s2) for hh in range(H)],
                               axis=1).astype(BF16)
    dqp = rowwise(frope_t, [dqr, *tabs], [], [(H * HEAD_PAD, BF16)], [], "q_rope_bwd")[0]
    d_w_q_b = mm(qa_n, dqp, "tn", "q_b_dw")
    dqa_n = mm(dqp, pw["w_q_b"], "nt", "q_b_dx")

    def f(qa, dn, g):
        xhat, r = _rms(qa)
        return _rms_bwd(xhat, r, dn * g).astype(BF16), dn * xhat
    dqa, d_q_a_g = rowwise(f, [qa, dqa_n], [q_a_norm_g], [(qa.shape[1], BF16)], [qa.shape[1]], "q_a_norm_bwd")
    d_w_q_a = mm(hn, dqa, "tn", "q_a_dw")
    dhn = mm(dqa, pw["w_q_a"], "nt", "q_a_dx")
    dh_in, d_sh, d_sc, d_g = norm_mod_bwd(h, dhn, dh_out, g, sc, "norm_mod_bwd")
    grads = dict(w_q_a=d_w_q_a, q_a_norm_g=d_q_a_g, w_q_b=d_w_q_b, w_o=d_w_o)
    return dh_in, (d_sh, d_sc, d_gate, d_g), grads, dk, dv


def loss_head(h, target, g):
    D = h.shape[1]

    def f(h, t, g):
        xhat, r = _rms(h)
        err = xhat * g - t
        dy = err * (1.0 / D)
        dh = _rms_bwd(xhat, r, dy * g)
        return dh, (0.5 / D) * err * err, dy * xhat
    return rowwise(f, [h, target], [g], [(D, F32)], [D, D], "loss_head")


def _place():
    x, y, c = lax.axis_index("x"), lax.axis_index("y"), lax.axis_index("c")
    chips = [(1 - x, y), (x, 1 - y), (1 - x, 1 - y)]
    return x, y, c, chips


HBM_SPEC = pl.BlockSpec(memory_space=pltpu.HBM)


def all_gather8(v):
    m, n = v.shape

    def body(x_ref, out_ref, send_sems, recv_sems, local_sem):
        x, y, c, chips = _place()
        me, sibling = (x, y, c), (x, y, 1 - c)

        def rows(px, py, pc):
            return out_ref.at[4 * px + 2 * py + pc]

        def copy(k, block, to, src=None):
            return pltpu.make_async_remote_copy(
                src_ref=rows(*block) if src is None else src, dst_ref=rows(*block),
                send_sem=send_sems.at[k], recv_sem=recv_sems.at[k], device_id=to, device_id_type=MESH)

        mine = pltpu.make_async_copy(x_ref, rows(*me), local_sem)
        mine.start()
        first = [copy(0, me, sibling, src=x_ref)]
        first += [copy(1 + j, me, (*chip, c), src=x_ref) for j, chip in enumerate(chips)]
        for cp in first:
            cp.start()
        passed = [copy(4 + j, (*chip, c), sibling) for j, chip in enumerate(chips)]
        for j, chip in enumerate(chips):
            copy(1 + j, (*chip, c), me).wait_recv()
            passed[j].start()
        copy(0, sibling, me).wait_recv()
        for j, chip in enumerate(chips):
            copy(4 + j, (*chip, 1 - c), me).wait_recv()
        for cp in first + passed:
            cp.wait_send()
        mine.wait()

    return pl.pallas_call(
        body, name="all_gather8",
        out_shape=jax.ShapeDtypeStruct((8, m, n), v.dtype),
        in_specs=[pl.BlockSpec(memory_space=pltpu.VMEM)],
        out_specs=pl.BlockSpec(memory_space=pltpu.VMEM),
        scratch_shapes=[pltpu.SemaphoreType.DMA((7,)), pltpu.SemaphoreType.DMA((7,)), pltpu.SemaphoreType.DMA],
        compiler_params=pltpu.CompilerParams(vmem_limit_bytes=VMEM_LIMIT_BYTES),
    )(v)


def gather_weights(bufs):
    n = len(bufs)

    def body(*refs):
        ins, outs = refs[:n], refs[n:2 * n]
        send_sems, recv_sems = refs[2 * n:]
        x, y, c, chips = _place()
        across_x, across_y, across_both = chips
        sibling = (x, y, 1 - c)
        me = 2 * x + y
        via_in = (x + (1 - c) * (1 - 2 * x), y + c * (1 - 2 * y))
        via_out = (x + c * (1 - 2 * x), y + (1 - c) * (1 - 2 * y))

        def idx(chip):
            return 2 * chip[0] + chip[1]

        def copy(w, k, src, dst, to):
            return pltpu.make_async_remote_copy(src_ref=src, dst_ref=dst, send_sem=send_sems.at[6 * w + k],
                                                recv_sem=recv_sems.at[6 * w + k], device_id=to, device_id_type=MESH)

        def landed(w, k, chip):
            blk = outs[w].at[idx(chip), c]
            copy(w, k, blk, blk, (*chip, c)).wait_recv()
            return blk

        sends = [copy(w, j, ins[w].at[me, c], outs[w].at[me, c], (*chip, c))
                 for w in range(n) for j, chip in enumerate((across_x, across_y))]
        for cp in sends:
            cp.start()
        for w in range(n):
            blk = landed(w, c, via_in)
            sends += [copy(w, 2, blk, blk, (*via_out, c)), copy(w, 3 + c, blk, blk, sibling)]
            sends[-2].start()
            sends[-1].start()
        for w in range(n):
            blk = landed(w, 1 - c, via_out)
            sends.append(copy(w, 4 - c, blk, blk, sibling))
            sends[-1].start()
        for w in range(n):
            blk = landed(w, 2, across_both)
            sends.append(copy(w, 5, blk, blk, sibling))
            sends[-1].start()
        for w in range(n):
            for j, chip in enumerate(chips):
                other = outs[w].at[idx(chip), 1 - c]
                copy(w, 3 + j, other, other, sibling).wait_recv()
        for cp in sends:
            cp.wait_send()

    return pl.pallas_call(
        body, name="gather_weights",
        out_shape=[jax.ShapeDtypeStruct(b.shape, b.dtype) for b in bufs],
        in_specs=[HBM_SPEC] * n, out_specs=[HBM_SPEC] * n,
        input_output_aliases={w: w for w in range(n)},
        scratch_shapes=[pltpu.SemaphoreType.DMA((6 * n,)), pltpu.SemaphoreType.DMA((6 * n,))],
    )(*bufs)


def exchange_halves(gs):
    n = len(gs)

    def body(*refs):
        ins, theirs = refs[:n], refs[n:2 * n]
        send_sems, recv_sems = refs[2 * n:]
        x, y, c, _ = _place()
        sends = [pltpu.make_async_remote_copy(src_ref=ins[w].at[:, 1 - c], dst_ref=theirs[w],
                                              send_sem=send_sems.at[w], recv_sem=recv_sems.at[w],
                                              device_id=(x, y, 1 - c), device_id_type=MESH) for w in range(n)]
        for cp in sends:
            cp.start()
        for cp in sends:
            cp.wait()

    return pl.pallas_call(
        body, name="exchange_halves",
        out_shape=[jax.ShapeDtypeStruct((4,) + g.shape[2:], g.dtype) for g in gs],
        in_specs=[HBM_SPEC] * n, out_specs=[HBM_SPEC] * n,
        scratch_shapes=[pltpu.SemaphoreType.DMA((n,)), pltpu.SemaphoreType.DMA((n,))],
    )(*gs)


def scatter_blocks(ps):
    n = len(ps)

    def body(*refs):
        ins, outs = refs[:n], refs[n:2 * n]
        send_sems, recv_sems = refs[2 * n:]
        x, y, c, chips = _place()
        sends = [pltpu.make_async_remote_copy(src_ref=ins[w].at[2 * chip[0] + chip[1]], dst_ref=outs[w].at[j],
                                              send_sem=send_sems.at[3 * w + j], recv_sem=recv_sems.at[3 * w + j],
                                              device_id=(*chip, c), device_id_type=MESH)
                 for w in range(n) for j, chip in enumerate(chips)]
        for cp in sends:
            cp.start()
        for cp in sends:
            cp.wait()

    return pl.pallas_call(
        body, name="scatter_blocks",
        out_shape=[jax.ShapeDtypeStruct((3,) + p.shape[1:], p.dtype) for p in ps],
        in_specs=[HBM_SPEC] * n, out_specs=[HBM_SPEC] * n,
        scratch_shapes=[pltpu.SemaphoreType.DMA((3 * n,)), pltpu.SemaphoreType.DMA((3 * n,))],
    )(*ps)


def join_halves(qs):
    n = len(qs)

    def body(*refs):
        ins, outs = refs[:n], refs[n:2 * n]
        send_sems, recv_sems = refs[2 * n:]
        x, y, c, _ = _place()
        sends = [pltpu.make_async_remote_copy(src_ref=ins[w].at[c], dst_ref=outs[w].at[c], send_sem=send_sems.at[w],
                                              recv_sem=recv_sems.at[w], device_id=(x, y, 1 - c), device_id_type=MESH)
                 for w in range(n)]
        for cp in sends:
            cp.start()
        for w in range(n):
            other = outs[w].at[1 - c]
            pltpu.make_async_remote_copy(src_ref=other, dst_ref=other, send_sem=send_sems.at[w],
                                         recv_sem=recv_sems.at[w], device_id=(x, y, 1 - c),
                                         device_id_type=MESH).wait_recv()
        for cp in sends:
            cp.wait_send()

    return pl.pallas_call(
        body, name="join_halves",
        out_shape=[jax.ShapeDtypeStruct(q.shape, q.dtype) for q in qs],
        in_specs=[HBM_SPEC] * n, out_specs=[HBM_SPEC] * n,
        input_output_aliases={w: w for w in range(n)},
        scratch_shapes=[pltpu.SemaphoreType.DMA((n,)), pltpu.SemaphoreType.DMA((n,))],
    )(*qs)


def _row_tile(R, row_bytes):
    tm = R
    for t in (512, 256, 128, 64, 32, 16, 8):
        if R % t == 0:
            tm = t
            if t * row_bytes <= ROW_TILE_BUDGET:
                break
    return tm


def sum_siblings(g, theirs, place):
    _, _, R, C = g.shape
    tm = _row_tile(R, 3 * C * 4)

    def body(place_ref, a_ref, b_ref, o_ref):
        o_ref[...] = (a_ref[...] + b_ref[...]).astype(BF16)

    return pl.pallas_call(
        body, name="sum_siblings",
        grid_spec=pltpu.PrefetchScalarGridSpec(
            num_scalar_prefetch=1, grid=(4, R // tm),
            in_specs=[pl.BlockSpec((None, None, tm, C), lambda j, i, s: (j, s[1], i, 0)),
                      pl.BlockSpec((None, tm, C), lambda j, i, s: (j, i, 0))],
            out_specs=pl.BlockSpec((None, tm, C), lambda j, i, s: (j, i, 0))),
        out_shape=jax.ShapeDtypeStruct((4, R, C), BF16),
        compiler_params=_params(("parallel", "parallel")),
    )(place, g, theirs)


def sum_chips(p, landed, place):
    _, R, C = p.shape
    tm = _row_tile(R, 5 * C * 4)

    def body(place_ref, p_ref, l0_ref, l1_ref, l2_ref, o_ref):
        o_ref[...] = ((p_ref[...].astype(F32) + l0_ref[...].astype(F32)) + l1_ref[...].astype(F32)
                      ) + l2_ref[...].astype(F32)

    return pl.pallas_call(
        body, name="sum_chips",
        grid_spec=pltpu.PrefetchScalarGridSpec(
            num_scalar_prefetch=1, grid=(R // tm,),
            in_specs=[pl.BlockSpec((None, tm, C), lambda i, s: (s[0], i, 0))]
            + [pl.BlockSpec((None, tm, C), lambda i, s, j=j: (j, i, 0)) for j in range(3)],
            out_specs=pl.BlockSpec((None, tm, C), lambda i, s: (s[1], i, 0))),
        out_shape=jax.ShapeDtypeStruct((2, R, C), F32),
        compiler_params=_params(("parallel",)),
    )(place, p, landed, landed, landed)


def sum_blocks(items, name):
    R, C = items[0][0].shape[1:]
    tm = R
    for t in (512, 256, 128, 64, 32, 16, 8):
        if R % t == 0:
            tm = t
            if t * C * 4 * (len(items) + 1) <= ROW_TILE_BUDGET:
                break
    n = len(items)

    def body(*refs):
        acc = refs[0][...].astype(F32)
        for r in refs[1:n]:
            acc = acc + r[...].astype(F32)
        refs[n][...] = acc

    return pl.pallas_call(
        body, name=name,
        grid=(R // tm,),
        in_specs=[pl.BlockSpec((None, tm, C), lambda i, j=j: (j, i, 0)) for _, j in items],
        out_specs=pl.BlockSpec((tm, C), lambda i: (i, 0)),
        out_shape=jax.ShapeDtypeStruct((R, C), F32),
        compiler_params=_params(("parallel",)),
    )(*[a for a, _ in items])


def reduce_scatter_grads(gs, place):
    theirs = exchange_halves(gs)
    ps = [sum_siblings(g, t, place) for g, t in zip(gs, theirs)]
    landed = scatter_blocks(ps)
    qs = [sum_chips(p, l, place) for p, l in zip(ps, landed)]
    joined = join_halves(qs)
    return [j.reshape(2 * j.shape[1], j.shape[2]) for j in joined]


def adamw(w, g, m, v):
    shape = w.shape
    C = shape[-1]
    R = w.size // C
    tm = R
    for t in (512, 256, 128, 64, 32, 16, 8):
        if R % t == 0:
            tm = t
            if t * C * 4 * 7 <= ROW_TILE_BUDGET:
                break

    def f(w, g, m, v):
        m = ADAM_B1 * m + (1.0 - ADAM_B1) * g
        v = ADAM_B2 * v + (1.0 - ADAM_B2) * (g * g)
        m_hat = m / (1.0 - ADAM_B1 ** ADAM_STEP)
        v_hat = v / (1.0 - ADAM_B2 ** ADAM_STEP)
        delta = -ADAM_LR * (m_hat / (jnp.sqrt(v_hat) + ADAM_EPS) + ADAM_WD * w)
        return delta, m, v

    d, nm, nv = rowwise(f, [a.reshape(R, C) for a in (w, g, m, v)], [], [(C, F32)] * 3, [], "adamw", tm=tm)
    return d.reshape(shape), nm.reshape(shape), nv.reshape(shape)


def _cast_into_slot(w, place):
    C = w.shape[-1]
    w2 = w.reshape(-1, C)
    R = w2.shape[0]
    tm = _row_tile(R, 6 * C)

    def body(place_ref, w_ref, o_ref):
        o_ref[...] = w_ref[...].astype(BF16)

    out = pl.pallas_call(
        body, name="cast_bf16",
        grid_spec=pltpu.PrefetchScalarGridSpec(
            num_scalar_prefetch=1, grid=(R // tm,),
            in_specs=[pl.BlockSpec((tm, C), lambda i, s: (i, 0))],
            out_specs=pl.BlockSpec((None, tm, C), lambda i, s: (s[0], i, 0))),
        out_shape=jax.ShapeDtypeStruct((4, R, C), BF16),
        compiler_params=_params(("parallel",)),
    )(place, w2)
    return out.reshape(4, 2, R // 2, C)


def _pack(vs):
    flat = jnp.concatenate([v.reshape(-1) for v in vs])
    n = flat.shape[0]
    total = -(-n // 1024) * 1024
    return jnp.pad(flat, (0, total - n)).reshape(total // 128, 128)


def _unpack(flat, like):
    out, o = [], 0
    for shp in like:
        sz = 1
        for d in shp:
            sz *= d
        out.append(flat[o:o + sz].reshape(shp))
        o += sz
    return out


def _cols_to_blocks(g, n_chips=4):
    R, N = g.shape
    C = N // n_chips
    return g.reshape(R, n_chips, C).transpose(1, 0, 2).reshape(n_chips, 2, R // 2, C)


def _rows_to_blocks(g, n_chips=4):
    R, C = g.shape
    return g.reshape(n_chips, 2, R // n_chips // 2, C)


def kernel(x, c, positions, ada_w, ada_b, norm_g, ffn_w13, ffn_w2, conv_w_pw1, conv_b_pw1, conv_w_dw, conv_b_dw, conv_ln_g, conv_ln_b, conv_w_pw2, conv_b_pw2, kv_ada_w, kv_ada_b, kv_norm_g, w_kv_a, kv_a_norm_g, w_kv_b, w_q_a, q_a_norm_g, w_q_b, w_o, final_norm_g, loss_target, m_ada_w, m_ada_b, m_norm_g, m_ffn_w13, m_ffn_w2, m_conv_w_pw1, m_conv_b_pw1, m_conv_w_dw, m_conv_b_dw, m_conv_ln_g, m_conv_ln_b, m_conv_w_pw2, m_conv_b_pw2, m_kv_ada_w, m_kv_ada_b, m_kv_norm_g, m_w_kv_a, m_kv_a_norm_g, m_w_kv_b, m_w_q_a, m_q_a_norm_g, m_w_q_b, m_w_o, m_final_norm_g, v_ada_w, v_ada_b, v_norm_g, v_ffn_w13, v_ffn_w2, v_conv_w_pw1, v_conv_b_pw1, v_conv_w_dw, v_conv_b_dw, v_conv_ln_g, v_conv_ln_b, v_conv_w_pw2, v_conv_b_pw2, v_kv_ada_w, v_kv_ada_b, v_kv_norm_g, v_w_kv_a, v_kv_a_norm_g, v_w_kv_b, v_w_q_a, v_q_a_norm_g, v_w_q_b, v_w_o, v_final_norm_g):
    S, D = x.shape[1], x.shape[2]
    H = N_HEADS
    F = ffn_w2.shape[2] * 4
    xi, yi, ci = lax.axis_index("x"), lax.axis_index("y"), lax.axis_index("c")
    chip = 2 * xi + yi
    dev = 2 * chip + ci
    place = jnp.stack([chip, ci]).astype(jnp.int32)
    h0 = x[0]
    target = loss_target[0]

    silu_c = rowwise(lambda a: a * _sigmoid(a), [c], [], [(D, F32)], [], "silu_c")[0]
    silu_all = all_gather8(silu_c.reshape(8, D // 8)).reshape(8, D)
    n_ada = ada_w.shape[2]
    n_kv = kv_ada_w.shape[1]
    ada_b_mine = lax.dynamic_slice_in_dim(ada_b, chip * n_ada, n_ada, axis=1)
    kv_b_mine = lax.dynamic_slice_in_dim(kv_ada_b, chip * n_kv, n_kv, axis=0)[None, :]
    mods = [mm(silu_all, ada_w[l], "nn", "ada_rows", bias=ada_b_mine[l:l + 1]) for l in range(2)]
    mods.append(mm(silu_all, kv_ada_w, "nn", "kv_ada_rows", bias=kv_b_mine))
    n_mod_cols = 2 * n_ada + n_kv
    mod_pack = jnp.concatenate(mods, axis=1).reshape(-1, 128)
    mod_all = all_gather8(mod_pack).reshape(8, 8, n_mod_cols)[0::2]
    mod_mine = lax.dynamic_index_in_dim(mod_all, dev, axis=1, keepdims=False)
    mod = [mod_mine[:, l * n_ada:(l + 1) * n_ada].reshape(N_MOD, D) for l in range(2)]
    kv_mod = mod_mine[:, 2 * n_ada:].reshape(2, D)
    kv_shift, kv_scale = kv_mod[0:1], kv_mod[1:2]

    def mrow(l, k):
        return mod[l][k:k + 1]

    big = dict(ffn_w13=ffn_w13, ffn_w2=ffn_w2, conv_w_pw1=conv_w_pw1, conv_w_pw2=conv_w_pw2, w_kv_a=w_kv_a,
               w_kv_b=w_kv_b, w_q_a=w_q_a, w_q_b=w_q_b, w_o=w_o)
    names = list(big)
    gathered = gather_weights([_cast_into_slot(big[k], place) for k in names])
    gw = dict(zip(names, gathered))
    small_like = [norm_g.shape, conv_b_pw1.shape, conv_w_dw.shape, conv_b_dw.shape, conv_ln_g.shape,
                  conv_ln_b.shape, conv_b_pw2.shape]
    small_pack = _pack([norm_g, conv_b_pw1, conv_w_dw, conv_b_dw, conv_ln_g, conv_ln_b, conv_b_pw2])
    small_all = all_gather8(small_pack)[0::2].reshape(4, -1)
    per_chip = [_unpack(small_all[j], small_like) for j in range(4)]
    smalls = [jnp.concatenate([per_chip[j][k] for j in range(4)], axis=-1) for k in range(len(small_like))]
    norm_g_f, b_pw1_f, w_dw_f, b_dw_f, ln_g_f, ln_b_f, b_pw2_f = smalls

    gw13 = gw["ffn_w13"].reshape(4, 2, 2, D, F // 2)
    w2 = gw["ffn_w2"].reshape(4, 2, 2, F // 4, D).transpose(1, 2, 0, 3, 4).reshape(2, 2, F, D)
    conv_p = dict(
        w_pw1=gw["conv_w_pw1"].reshape(4, D, 2 * D // 4).transpose(1, 0, 2).reshape(D, 2 * D),
        b_pw1=b_pw1_f, w_dw=w_dw_f[0], b_dw=b_dw_f, ln_g=ln_g_f, ln_b=ln_b_f,
        w_pw2=gw["conv_w_pw2"].reshape(D, D), b_pw2=b_pw2_f)
    q_lora = w_q_a.shape[2]
    mla_p = dict(
        w_kv_a=gw["w_kv_a"].reshape(D, KV_LORA + QK_ROPE),
        w_kv_b=gw["w_kv_b"].reshape(4, KV_LORA, -1).transpose(1, 0, 2).reshape(KV_LORA, -1),
        w_q_a=gw["w_q_a"].reshape(D, q_lora),
        w_q_b=gw["w_q_b"].reshape(4, q_lora, -1).transpose(1, 0, 2).reshape(q_lora, -1),
        w_o=gw["w_o"].reshape(H * V_HEAD, D))
    pw = mla_pad_weights(mla_p)
    tabs = rope_tables(positions[0])

    def ng(l, k):
        return norm_g_f[l, k][None, :]

    h = h0
    h, s_f1_0 = ffn_fwd(h, ng(0, 0), mrow(0, 0), mrow(0, 1), mrow(0, 2), gw13, 0, 0, w2[0, 0])
    h, s_conv = conv_module_fwd(h, ng(0, 1), mrow(0, 3), mrow(0, 4), mrow(0, 5), conv_p)
    h, s_f2_0 = ffn_fwd(h, ng(0, 2), mrow(0, 6), mrow(0, 7), mrow(0, 8), gw13, 0, 1, w2[0, 1])
    kv_norm = kv_norm_g[None, :]
    kv_a_g = kv_a_norm_g[None, :]
    kv, kpe_r, s_kv = mla_kv_fwd(h, kv_norm, kv_shift, kv_scale, kv_a_g, pw, tabs)
    h, s_f1_1 = ffn_fwd(h, ng(1, 0), mrow(1, 0), mrow(1, 1), mrow(1, 2), gw13, 1, 0, w2[1, 0])
    h, s_mla = mla_fwd(h, ng(1, 1), mrow(1, 3), mrow(1, 4), mrow(1, 5), q_a_norm_g, pw, kv, kpe_r, tabs)
    h, s_f2_1 = ffn_fwd(h, ng(1, 2), mrow(1, 6), mrow(1, 7), mrow(1, 8), gw13, 1, 1, w2[1, 1])
    dh, loss_cols, d_final_g = loss_head(h, target, final_norm_g[None, :])

    dh, v_f2_1, dw13_11, dw2_11 = ffn_bwd(dh, s_f2_1, ng(1, 2), mrow(1, 7), mrow(1, 8), gw13, 1, 1, w2[1, 1])
    dh, v_mla, g_mla, dk, dv = mla_bwd(dh, s_mla, ng(1, 1), mrow(1, 4), mrow(1, 5), q_a_norm_g, pw, kv, kpe_r, tabs)
    dh, v_f1_1, dw13_10, dw2_10 = ffn_bwd(dh, s_f1_1, ng(1, 0), mrow(1, 1), mrow(1, 2), gw13, 1, 0, w2[1, 0])
    dh, v_kv, d_kv_a_g, d_w_kv_a, d_w_kv_b = mla_kv_bwd(dh, dk, dv, s_kv, kv_norm, kv_scale, kv_a_g, pw, tabs)
    dh, v_f2_0, dw13_01, dw2_01 = ffn_bwd(dh, s_f2_0, ng(0, 2), mrow(0, 7), mrow(0, 8), gw13, 0, 1, w2[0, 1])
    dh, v_conv, g_conv = conv_module_bwd(dh, s_conv, ng(0, 1), mrow(0, 4), mrow(0, 5), conv_p)
    dh, v_f1_0, dw13_00, dw2_00 = ffn_bwd(dh, s_f1_0, ng(0, 0), mrow(0, 1), mrow(0, 2), gw13, 0, 0, w2[0, 0])
    grad_x = dh[None]

    d_w_kv_a_u = jnp.concatenate([d_w_kv_a[:, :KV_LORA], d_w_kv_a[:, KV_LORA + QK_NOPE:KV_LORA + QK_NOPE + QK_ROPE]],
                                 axis=1)
    hk = H * HEAD_PAD
    dkb = jnp.concatenate([d_w_kv_b[:, :hk].reshape(KV_LORA, H, HEAD_PAD)[:, :, :QK_NOPE],
                           d_w_kv_b[:, hk:].reshape(KV_LORA, H, HEAD_PAD)[:, :, :V_HEAD]], axis=2).reshape(KV_LORA, -1)
    d_w_q_b_u = _unpad_heads(g_mla["w_q_b"], QK_NOPE + QK_ROPE)
    d_w_o_u = g_mla["w_o"].reshape(H, HEAD_PAD, D)[:, :V_HEAD].reshape(H * V_HEAD, D)
    full = [dw.reshape(4, 2, D // 2, F // 2) for dw in (dw13_00, dw13_01, dw13_10, dw13_11)] + [
            _rows_to_blocks(dw2_00), _rows_to_blocks(dw2_01), _rows_to_blocks(dw2_10), _rows_to_blocks(dw2_11),
            _cols_to_blocks(g_conv["w_pw1"]), _rows_to_blocks(g_conv["w_pw2"]), _rows_to_blocks(d_w_kv_a_u),
            _cols_to_blocks(dkb), _rows_to_blocks(g_mla["w_q_a"]), _cols_to_blocks(d_w_q_b_u),
            _rows_to_blocks(d_w_o_u)]
    red = reduce_scatter_grads(full, place)
    g_ffn_w13 = jnp.stack(red[0:4]).reshape(ffn_w13.shape)
    g_ffn_w2 = jnp.stack(red[4:8]).reshape(ffn_w2.shape)
    g_conv_w_pw1 = red[8].reshape(conv_w_pw1.shape)
    g_conv_w_pw2 = red[9].reshape(conv_w_pw2.shape)
    g_w_kv_a = red[10].reshape(w_kv_a.shape)
    g_w_kv_b = red[11].reshape(w_kv_b.shape)
    g_w_q_a = red[12].reshape(w_q_a.shape)
    g_w_q_b = red[13].reshape(w_q_b.shape)
    g_w_o = red[14].reshape(w_o.shape)

    def dmod(v1, vm, v2):
        return jnp.concatenate([v1[0], v1[1], v1[2], vm[0], vm[1], vm[2], v2[0], v2[1], v2[2]], axis=1)
    d_mod0 = dmod(v_f1_0, v_conv, v_f2_0)
    d_mod1 = dmod(v_f1_1, v_mla, v_f2_1)
    d_kv_mod = jnp.concatenate([v_kv[0], v_kv[1]], axis=1)
    d_norm_g = jnp.concatenate([v_f1_0[3], v_conv[3], v_f2_0[3], v_f1_1[3], v_mla[3], v_f2_1[3]], axis=0)
    vec_list = [d_mod0, d_mod1, d_kv_mod, d_norm_g, g_conv["b_pw1"], g_conv["w_dw"], g_conv["b_dw"], g_conv["ln_g"],
                g_conv["ln_b"], g_conv["b_pw2"], v_kv[2], d_kv_a_g, g_mla["q_a_norm_g"], d_final_g, loss_cols]
    vec_like = [v.shape for v in vec_list]
    vec_pack = _pack(vec_list)
    n_mod_rows = (2 * N_MOD * D + 2 * D) // 128
    vec_all = all_gather8(vec_pack)
    vec_sum = sum_blocks([(vec_all, d) for d in range(8)], "sum_devices").reshape(-1)
    (_, _, _, s_norm_g, s_b_pw1, s_w_dw, s_b_dw, s_ln_g, s_ln_b, s_b_pw2, s_kv_norm_g, s_kv_a_g, s_q_a_g,
     s_final_g, s_loss) = _unpack(vec_sum, vec_like)
    loss = jnp.sum(s_loss)
    dmod_all = vec_all[:, :n_mod_rows].reshape(8, 2 * N_MOD * D + 2 * D)
    dmod_sum = vec_sum[:2 * N_MOD * D + 2 * D]
    g_ada_b = dmod_sum[:2 * N_MOD * D].reshape(2, N_MOD * D)
    g_kv_ada_b = dmod_sum[2 * N_MOD * D:]
    g_ada_w = []
    for l in range(2):
        cols = lax.dynamic_slice_in_dim(dmod_all[:, l * N_MOD * D:(l + 1) * N_MOD * D], chip * n_ada, n_ada, axis=1)
        g_ada_w.append(mm(silu_all, cols, "tn", "ada_w_grad"))
    g_ada_w = jnp.stack(g_ada_w)
    kv_cols = lax.dynamic_slice_in_dim(dmod_all[:, 2 * N_MOD * D:], chip * n_kv, n_kv, axis=1)
    g_kv_ada_w = mm(silu_all, kv_cols, "tn", "kv_ada_w_grad")

    def shard(v, width):
        return lax.dynamic_slice_in_dim(v, chip * width, width, axis=v.ndim - 1)

    Dq = D // 4
    g_norm_g = shard(s_norm_g.reshape(2, 3, D), Dq)
    g_conv_b_pw1 = shard(s_b_pw1, 2 * D // 4)
    g_conv_w_dw = shard(s_w_dw, Dq)[None]
    g_conv_b_dw = shard(s_b_dw, Dq)
    g_conv_ln_g = shard(s_ln_g, Dq)
    g_conv_ln_b = shard(s_ln_b, Dq)
    g_conv_b_pw2 = shard(s_b_pw2, Dq)

    grads = [g_ada_w, g_ada_b, g_norm_g, g_ffn_w13, g_ffn_w2, g_conv_w_pw1, g_conv_b_pw1, g_conv_w_dw, g_conv_b_dw,
             g_conv_ln_g, g_conv_ln_b, g_conv_w_pw2, g_conv_b_pw2, g_kv_ada_w, g_kv_ada_b, s_kv_norm_g[0], g_w_kv_a,
             s_kv_a_g[0], g_w_kv_b, g_w_q_a, s_q_a_g, g_w_q_b, g_w_o, s_final_g[0]]
    weights = [ada_w, ada_b, norm_g, ffn_w13, ffn_w2, conv_w_pw1, conv_b_pw1, conv_w_dw, conv_b_dw, conv_ln_g,
               conv_ln_b, conv_w_pw2, conv_b_pw2, kv_ada_w, kv_ada_b, kv_norm_g, w_kv_a, kv_a_norm_g, w_kv_b, w_q_a,
               q_a_norm_g, w_q_b, w_o, final_norm_g]
    ms = [m_ada_w, m_ada_b, m_norm_g, m_ffn_w13, m_ffn_w2, m_conv_w_pw1, m_conv_b_pw1, m_conv_w_dw, m_conv_b_dw,
          m_conv_ln_g, m_conv_ln_b, m_conv_w_pw2, m_conv_b_pw2, m_kv_ada_w, m_kv_ada_b, m_kv_norm_g, m_w_kv_a,
          m_kv_a_norm_g, m_w_kv_b, m_w_q_a, m_q_a_norm_g, m_w_q_b, m_w_o, m_final_norm_g]
    vs = [v_ada_w, v_ada_b, v_norm_g, v_ffn_w13, v_ffn_w2, v_conv_w_pw1, v_conv_b_pw1, v_conv_w_dw, v_conv_b_dw,
          v_conv_ln_g, v_conv_ln_b, v_conv_w_pw2, v_conv_b_pw2, v_kv_ada_w, v_kv_ada_b, v_kv_norm_g, v_w_kv_a,
          v_kv_a_norm_g, v_w_kv_b, v_w_q_a, v_q_a_norm_g, v_w_q_b, v_w_o, v_final_norm_g]
    grads = [g.reshape(w.shape) for g, w in zip(grads, weights)]
    deltas, new_m, new_v = [], [], []
    for w, g, m, v in zip(weights, grads, ms, vs):
        d, nm, nv = adamw(w, g, m, v)
        deltas.append(d)
        new_m.append(nm)
        new_v.append(nv)
    return (loss, grad_x, *grads, *deltas, *new_m, *new_v)
```

```python
import jax
import jax.numpy as jnp
from jax import lax
from jax.experimental import pallas as pl
from jax.experimental.pallas import tpu as pltpu

F32 = jnp.float32
BF16 = jnp.bfloat16
MESH = pl.DeviceIdType.MESH

N_HEADS = 16
QK_NOPE = 64
QK_ROPE = 32
V_HEAD = 64
KV_LORA = 256
CONV_WIDTH = 31
ROPE_THETA = 10000.0
EPS = 1e-6
N_MOD = 9
HEAD_PAD = 128
ATTN_TILE = 512
CONV_HALO = 32

ADAM_LR = 0.001
ADAM_B1 = 0.9
ADAM_B2 = 0.999
ADAM_EPS = 1e-08
ADAM_WD = 0.01
ADAM_STEP = 10

VMEM_LIMIT_BYTES = 56 * 2 ** 20
ROW_TILE_BUDGET = 10 * 2 ** 20
MM_VMEM_BUDGET = 40 * 2 ** 20
NEG = float(jnp.finfo(jnp.float32).min)
LOG2_E = 1.4426950408889634


def _tile(n, prefs):
    for t in prefs:
        if n % t == 0:
            return t
    return n


def _params(sem):
    return pltpu.CompilerParams(dimension_semantics=sem, vmem_limit_bytes=VMEM_LIMIT_BYTES)


def _mm_tiles(M, N, K, mode, a_bytes, b_bytes, o_bytes):
    if mode == "tn":
        tk_opts = [t for t in (2048, 1024, 512, 256, 128) if K % t == 0] or [K]
        tm_opts = ([M] if M <= 2816 else []) + [t for t in (1024, 512, 256, 128) if M % t == 0 and t < M]
    else:
        tk_opts = [K]
        tm_opts = [t for t in (1024, 512, 256, 128) if M % t == 0] or [M]
    tn_opts = [t for t in (1408, 1024, 512, 384, 256, 128) if N % t == 0] or [N]

    def need(tm, tn, tk):
        blocks = 2 * (tm * tk * a_bytes + tk * tn * b_bytes + tm * tn * o_bytes)
        return blocks + (tm * tn * 4 if mode == "tn" else 0)

    tk_floor = next((t for t in tk_opts if t <= 512), tk_opts[-1])
    for tm in tm_opts:
        for tn in tn_opts:
            if need(tm, tn, tk_floor) <= MM_VMEM_BUDGET:
                return tm, tn, next(tk for tk in tk_opts if need(tm, tn, tk) <= MM_VMEM_BUDGET)
    return tm_opts[-1], tn_opts[-1], tk_opts[-1]


def mm(a, b, mode, name, out_dtype=F32, bias=None):
    if mode == "nn":
        (M, K), (K2, N) = a.shape, b.shape
        dims = (((1,), (0,)), ((), ()))
    elif mode == "nt":
        (M, K), (N, K2) = a.shape, b.shape
        dims = (((1,), (1,)), ((), ()))
    else:
        (K, M), (K2, N) = a.shape, b.shape
        dims = (((0,), (0,)), ((), ()))
    assert K == K2, (a.shape, b.shape, mode)
    tm, tn, tk = _mm_tiles(M, N, K, mode, a.dtype.itemsize, b.dtype.itemsize, jnp.dtype(out_dtype).itemsize)
    nk = K // tk
    if mode == "tn":
        a_spec = pl.BlockSpec((tk, tm), lambda i, j, k: (k, i))
        b_spec = pl.BlockSpec((tk, tn), lambda i, j, k: (k, j))
    elif mode == "nn":
        a_spec = pl.BlockSpec((tm, tk), lambda i, j, k: (i, k))
        b_spec = pl.BlockSpec((tk, tn), lambda i, j, k: (k, j))
    else:
        a_spec = pl.BlockSpec((tm, tk), lambda i, j, k: (i, k))
        b_spec = pl.BlockSpec((tn, tk), lambda i, j, k: (j, k))
    in_specs = [a_spec, b_spec]
    operands = [a, b]
    if bias is not None:
        in_specs.append(pl.BlockSpec((1, tn), lambda i, j, k: (0, j)))
        operands.append(bias)
    has_bias = bias is not None

    def body(*refs):
        a_ref, b_ref = refs[0], refs[1]
        bias_ref = refs[2] if has_bias else None
        o_ref = refs[3] if has_bias else refs[2]
        prod = lax.dot_general(a_ref[...].astype(BF16), b_ref[...].astype(BF16), dims,
                               preferred_element_type=F32)
        if nk == 1:
            if has_bias:
                prod = prod + bias_ref[...]
            o_ref[...] = prod.astype(o_ref.dtype)
        else:
            acc_ref = refs[-1]
            k = pl.program_id(2)

            @pl.when(k == 0)
            def _():
                acc_ref[...] = jnp.zeros_like(acc_ref)

            acc_ref[...] += prod

            @pl.when(k == nk - 1)
            def _():
                out = acc_ref[...]
                if has_bias:
                    out = out + bias_ref[...]
                o_ref[...] = out.astype(o_ref.dtype)

    return pl.pallas_call(
        body, name=name,
        grid=(M // tm, N // tn, nk),
        in_specs=in_specs,
        out_specs=pl.BlockSpec((tm, tn), lambda i, j, k: (i, j)),
        out_shape=jax.ShapeDtypeStruct((M, N), out_dtype),
        scratch_shapes=[pltpu.VMEM((tm, tn), F32)] if nk > 1 else [],
        compiler_params=_params(("parallel", "parallel", "arbitrary")),
    )(*operands)


def mm_fused(a, b, mode, name, tn, epi, epi_outs, pro=None, pro_rows=(), pro_vecs=(), pro_out=False, n_pro_sums=0,
             epi_rows=(), epi_vecs=(), b_blocks=None, n_cols=None):
    M, K = a.shape
    if b_blocks is not None:
        n_b, N = len(b_blocks), n_cols
    else:
        n_b = b.shape[0] if b.ndim == 3 else 1
        N = b.shape[-1] if mode == "nn" else b.shape[0]
    dims = (((1,), (0,)), ((), ())) if mode == "nn" else (((1,), (1,)), ((), ()))
    nj = N // tn
    epi_outs = [o if len(o) == 3 else (*o, None) for o in epi_outs]
    row_bytes = 2 * (K * a.dtype.itemsize + sum(K * r.dtype.itemsize for r in pro_rows) + (2 * K if pro_out else 0)
                     + sum(w * r.dtype.itemsize * (r.shape[0] if r.ndim == 3 else 1) for r, w in epi_rows)
                     + sum(w * jnp.dtype(dt).itemsize * (L or 1) for w, dt, L in epi_outs)
                     ) + (2 * K if pro is not None else 0)
    fixed = 2 * n_b * K * tn * b.dtype.itemsize
    tm = next((t for t in (1024, 512, 256, 128) if M % t == 0 and t * row_bytes + fixed <= MM_VMEM_BUDGET), M)
    row = lambda i, j: (i, 0)
    tile = lambda i, j: (i, j)
    stack = lambda i, j: (0, i, j)
    in_specs = [pl.BlockSpec((tm, K), row)] + [pl.BlockSpec((tm, K), row) for _ in pro_rows]
    in_specs += [pl.BlockSpec(v.shape, lambda i, j: (0, 0)) for v in pro_vecs]
    if b_blocks is not None:
        in_specs += [pl.BlockSpec(shape, imap) for shape, imap in b_blocks]
    elif b.ndim == 3:
        in_specs += [pl.BlockSpec((None, K, tn), lambda i, j, h=h: (h, 0, j)) for h in range(n_b)]
    elif mode == "nn":
        in_specs += [pl.BlockSpec((K, tn), lambda i, j: (0, j))]
    else:
        in_specs += [pl.BlockSpec((tn, K), lambda i, j: (j, 0))]
    in_specs += [pl.BlockSpec((r.shape[0], tm, w), stack) if r.ndim == 3 else pl.BlockSpec((tm, w), tile)
                 for r, w in epi_rows]
    in_specs += [pl.BlockSpec((1, tn), lambda i, j: (0, j)) for _ in epi_vecs]
    out_specs, out_shape = [], []
    if pro_out:
        out_specs.append(pl.BlockSpec((tm, K), row))
        out_shape.append(jax.ShapeDtypeStruct((M, K), BF16))
    for _ in range(n_pro_sums):
        out_specs.append(pl.BlockSpec((1, K), lambda i, j: (0, 0)))
        out_shape.append(jax.ShapeDtypeStruct((1, K), F32))
    for w, dt, L in epi_outs:
        out_specs.append(pl.BlockSpec((tm, w), tile) if L is None else pl.BlockSpec((L, tm, w), stack))
        out_shape.append(jax.ShapeDtypeStruct((M, nj * w) if L is None else (L, M, nj * w), dt))
    n_pr, n_pv, n_er, n_ev = len(pro_rows), len(pro_vecs), len(epi_rows), len(epi_vecs)
    n_a = 1 + n_pr + n_pv
    n_in = n_a + n_b + n_er + n_ev
    n_po = 1 if pro_out else 0

    def body(*refs):
        i, j = pl.program_id(0), pl.program_id(1)
        a_ref = refs[0]
        outs = refs[n_in:]
        if pro is not None:
            lhs_ref = refs[-1]

            @pl.when(j == 0)
            def _():
                res = pro(*[r[...] for r in refs[:1 + n_pr + n_pv]])
                if not isinstance(res, (tuple, list)):
                    res = (res,)
                lhs_ref[...] = res[0]
                if pro_out:
                    outs[0][...] = res[0]
                for s_ref, val in zip(outs[n_po:n_po + n_pro_sums], res[1:]):
                    part = jnp.sum(val.astype(F32), axis=0, keepdims=True)

                    @pl.when(i == 0)
                    def _(s_ref=s_ref, part=part):
                        s_ref[...] = part

                    @pl.when(i != 0)
                    def _(s_ref=s_ref, part=part):
                        s_ref[...] += part

            lhs = lhs_ref[...]
        else:
            lhs = a_ref[...].astype(BF16)
        accs = [lax.dot_general(lhs, b_ref[...].astype(BF16), dims, preferred_element_type=F32)
                for b_ref in refs[n_a:n_a + n_b]]
        res = epi(*accs, *[r[...] for r in refs[n_a + n_b:n_in]])
        if not isinstance(res, (tuple, list)):
            res = (res,)
        for o_ref, val in zip(outs[n_po + n_pro_sums:], res):
            if isinstance(val, (tuple, list)):
                for h, part in enumerate(val):
                    o_ref[h] = part.astype(o_ref.dtype)
            else:
                o_ref[...] = val.astype(o_ref.dtype)

    return pl.pallas_call(
        body, name=name,
        grid=(M // tm, nj),
        in_specs=in_specs, out_specs=out_specs, out_shape=out_shape,
        scratch_shapes=[pltpu.VMEM((tm, K), BF16)] if pro is not None else [],
        compiler_params=_params(("arbitrary", "arbitrary")),
    )(a, *pro_rows, *pro_vecs, *([b] * n_b), *[r for r, _ in epi_rows], *epi_vecs)


def rowwise(fn, rows, vecs, outs, sums, name, tm=None):
    norm = [(r, r.shape[1], 0) if not isinstance(r, tuple) else r for r in rows]
    S = norm[0][0].shape[0]
    if tm is None:
        per_row = sum(w * r.dtype.itemsize for r, w, _ in norm) + sum(n * jnp.dtype(dt).itemsize for n, dt in outs)
        tm = S
        for t in (512, 256, 128, 64, 32, 16, 8):
            if S % t == 0:
                tm = t
                if t * per_row <= ROW_TILE_BUDGET:
                    break
    n_rows, n_vecs, n_outs, n_sums = len(norm), len(vecs), len(outs), len(sums)
    in_specs = [pl.BlockSpec((tm, w), lambda i, cb=cb: (i, cb)) for _, w, cb in norm]
    in_specs += [pl.BlockSpec(v.shape, lambda i: (0, 0)) for v in vecs]
    out_specs = [pl.BlockSpec((tm, n), lambda i: (i, 0)) for n, _ in outs]
    out_specs += [pl.BlockSpec((1, n), lambda i: (0, 0)) for n in sums]
    out_shape = [jax.ShapeDtypeStruct((S, n), dt) for n, dt in outs]
    out_shape += [jax.ShapeDtypeStruct((1, n), F32) for n in sums]

    def body(*refs):
        ins = [r[...] for r in refs[:n_rows + n_vecs]]
        res = fn(*ins)
        if not isinstance(res, (tuple, list)):
            res = (res,)
        out_refs = refs[n_rows + n_vecs:]
        for o_ref, val in zip(out_refs[:n_outs], res[:n_outs]):
            o_ref[...] = val.astype(o_ref.dtype)
        if n_sums:
            i = pl.program_id(0)
            for s_ref, val in zip(out_refs[n_outs:], res[n_outs:]):
                part = jnp.sum(val.astype(F32), axis=0, keepdims=True)

                @pl.when(i == 0)
                def _(s_ref=s_ref, part=part):
                    s_ref[...] = part

                @pl.when(i != 0)
                def _(s_ref=s_ref, part=part):
                    s_ref[...] += part

    res = pl.pallas_call(
        body, name=name,
        grid=(S // tm,),
        in_specs=in_specs, out_specs=out_specs, out_shape=out_shape,
        compiler_params=_params(("arbitrary",) if n_sums else ("parallel",)),
    )(*[r for r, _, _ in norm], *vecs)
    return res


def _sigmoid(x):
    return jax.nn.sigmoid(x)


def _rms(x):
    r = lax.rsqrt(jnp.mean(x * x, axis=-1, keepdims=True) + EPS)
    return x * r, r


def _rms_bwd(xhat, r, dxhat):
    return r * (dxhat - xhat * jnp.mean(dxhat * xhat, axis=-1, keepdims=True))


def norm_mod(h, g, sh, sc, name):
    def f(h, g, sh, sc):
        xhat, _ = _rms(h)
        return ((xhat * g) * (1 + sc) + sh).astype(BF16)
    return rowwise(f, [h], [g, sh, sc], [(h.shape[1], BF16)], [], name)[0]


def norm_mod_bwd(h, dhn, dh_out, g, sc, name):
    D = h.shape[1]
    with_res = dh_out is not None

    def f(*a):
        if with_res:
            h, dhn, dres, g, sc = a
        else:
            h, dhn, g, sc = a
        xhat, r = _rms(h)
        xn = xhat * g
        dxn = dhn * (1 + sc)
        dh = _rms_bwd(xhat, r, dxn * g)
        if with_res:
            dh = dh + dres
        return dh, dhn, dhn * xn, dxn * xhat

    rows = [h, dhn] + ([dh_out] if with_res else [])
    return rowwise(f, rows, [g, sc], [(D, F32)], [D, D, D], name)


def residual(h, y, gate, coef, name, bias=None):
    D = h.shape[1]
    if bias is None:
        def f(h, y, gate):
            return h + (coef * gate) * y
        return rowwise(f, [h, y], [gate], [(D, F32)], [], name)[0], y

    def fb(h, y, gate, bias):
        yb = y + bias
        return h + (coef * gate) * yb, yb
    return rowwise(fb, [h, y], [gate, bias], [(D, F32), (D, F32)], [], name)


def residual_bwd(dh_out, y, gate, coef, name, with_bias_sum=False):
    D = y.shape[1]

    def f(dh, y, gate):
        dy = (coef * gate) * dh
        res = (dy.astype(BF16), coef * dh * y)
        return res + ((dy,) if with_bias_sum else ())
    return rowwise(f, [dh_out, y], [gate], [(D, BF16)], [D, D] if with_bias_sum else [D], name)


def ffn_w13_dx(dab, gw13, l, i):
    _, S, F = dab.shape
    D, C = gw13.shape[3:]
    tm = _tile(S, (1024, 512, 256, 128))
    nt = (((1,), (1,)), ((), ()))

    def body(a_ref, b_ref, o_ref, acc_ref):
        k = pl.program_id(1)
        prod = lax.dot_general(a_ref[...], b_ref[...], nt, preferred_element_type=F32)

        @pl.when(k == 0)
        def _():
            acc_ref[...] = prod

        @pl.when((k > 0) & (k < 3))
        def _():
            acc_ref[...] += prod

        @pl.when(k == 3)
        def _():
            o_ref[...] = acc_ref[...] + prod

    return pl.pallas_call(
        body, name="ffn_w13_dx",
        grid=(S // tm, 4),
        in_specs=[pl.BlockSpec((None, tm, C), lambda r, k: (k // 2, r, k % 2)),
                  pl.BlockSpec((None, None, None, D, C), lambda r, k: (k, l, i, 0, 0))],
        out_specs=pl.BlockSpec((tm, D), lambda r, k: (r, 0)),
        out_shape=jax.ShapeDtypeStruct((S, D), F32),
        scratch_shapes=[pltpu.VMEM((tm, D), F32)],
        compiler_params=_params(("parallel", "arbitrary")),
    )(dab, gw13)


def ffn_w13_grad(hn, dab):
    S, D = hn.shape
    F = dab.shape[2]
    C = F // 2
    tk = next(t for t in (2048, 1024, 512, 256, 128) if S % t == 0)
    tn_dims = (((0,), (0,)), ((), ()))
    nk = S // tk

    def body(a_ref, b_ref, o_ref, acc_ref):
        k = pl.program_id(1)

        @pl.when(k == 0)
        def _():
            acc_ref[...] = jnp.zeros_like(acc_ref)

        acc_ref[...] += lax.dot_general(a_ref[...], b_ref[...], tn_dims, preferred_element_type=F32)

        @pl.when(k == nk - 1)
        def _():
            o_ref[...] = acc_ref[...]

    return pl.pallas_call(
        body, name="ffn_w13_dw",
        grid=(4, nk),
        in_specs=[pl.BlockSpec((tk, D), lambda j, k: (k, 0)),
                  pl.BlockSpec((None, tk, C), lambda j, k: (j // 2, k, j % 2))],
        out_specs=pl.BlockSpec((None, D, C), lambda j, k: (j, 0, 0)),
        out_shape=jax.ShapeDtypeStruct((4, D, C), F32),
        scratch_shapes=[pltpu.VMEM((D, C), F32)],
        compiler_params=_params(("parallel", "arbitrary")),
    )(hn, dab)


def ffn_fwd(h, g, sh, sc, gate, gw13, l, i, w2):
    F, D = w2.shape
    C = F // 2

    def norm(h, g, sh, sc):
        xhat, _ = _rms(h)
        return ((xhat * g) * (1 + sc) + sh).astype(BF16)

    def act(a, b):
        sig = _sigmoid(a)
        sa = a * sig
        return (b * (sig * (1 + a * (1 - sig))), sa), sa * b
    blocks = [((None, None, None, D, C), lambda r, j, half=half: (2 * half + j, l, i, 0, 0)) for half in range(2)]
    hn, dt_dab, t = mm_fused(h, gw13, "nn", "ffn_w13", C, act, [(C, BF16, 2), (C, BF16)],
                             pro=norm, pro_vecs=[g, sh, sc], pro_out=True, b_blocks=blocks, n_cols=F)

    def res(acc, h, gate):
        return h + (0.5 * gate) * acc, acc
    h_out, y = mm_fused(t, w2, "nn", "ffn_w2", D, res, [(D, F32), (D, F32)], epi_rows=[(h, D)], epi_vecs=[gate])
    return h_out, (h, hn, dt_dab, t, y)


def ffn_bwd(dh_out, saved, g, sc, gate, gw13, l, i, w2):
    h, hn, dt_dab, t, y = saved
    F, D = w2.shape
    C = F // 2

    def scale(dh, y, gate):
        return ((0.5 * gate) * dh).astype(BF16), 0.5 * dh * y

    def act_bwd(dt, f):
        return ((dt * f[0].astype(F32), dt * f[1].astype(F32)),)
    dy, d_gate, dab = mm_fused(dh_out, w2, "nt", "ffn_w2_dx", C, act_bwd, [(C, BF16, 2)],
                               pro=scale, pro_rows=[y], pro_vecs=[gate], pro_out=True, n_pro_sums=1,
                               epi_rows=[(dt_dab, C)])
    dw2 = mm(t, dy, "tn", "ffn_w2_dw")
    dw13 = ffn_w13_grad(hn, dab)
    dhn = ffn_w13_dx(dab, gw13, l, i)
    dh_in, d_sh, d_sc, d_g = norm_mod_bwd(h, dhn, dh_out, g, sc, "norm_mod_bwd")
    return dh_in, (d_sh, d_sc, d_gate, d_g), dw13, dw2


def _shifted(xbuf, n):
    return [xbuf] + [pltpu.roll(xbuf, n - b, 0) for b in range(1, 8)]


def conv_fwd(u, w_dw, b_dw, ln_g, ln_b):
    S, D = u.shape
    tm = _tile(S, (256, 128))
    rc = 32
    first_tap = CONV_HALO - (CONV_WIDTH - 1)
    w = jnp.concatenate([w_dw, jnp.zeros((CONV_HALO - CONV_WIDTH, D), F32)], axis=0)

    def body(cur_ref, prev_ref, w_ref, b_ref, g_ref, beta_ref, z_ref, s_ref):
        i = pl.program_id(0)
        prev = jnp.where(i == 0, jnp.zeros((CONV_HALO, D), F32), prev_ref[...])
        xs = _shifted(jnp.concatenate([prev, cur_ref[...]], axis=0), tm + CONV_HALO)
        for c0 in range(0, tm, rc):
            acc = jnp.zeros((rc, D), F32)
            for k in range(CONV_WIDTH):
                off = first_tap + k
                a8, b = off // 8 * 8, off % 8
                acc = acc + w_ref[k:k + 1, :] * xs[b][c0 + a8:c0 + a8 + rc, :]
            z_ref[c0:c0 + rc, :] = acc + b_ref[...]
        z = z_ref[...]
        mu = jnp.mean(z, axis=-1, keepdims=True)
        zc = z - mu
        r = lax.rsqrt(jnp.mean(zc * zc, axis=-1, keepdims=True) + EPS)
        un = zc * r * g_ref[...] + beta_ref[...]
        s_ref[...] = (un * _sigmoid(un)).astype(BF16)

    nb = tm // CONV_HALO
    vec = pl.BlockSpec((1, D), lambda i: (0, 0))
    return pl.pallas_call(
        body, name="conv_fwd",
        grid=(S // tm,),
        in_specs=[pl.BlockSpec((tm, D), lambda i: (i, 0)),
                  pl.BlockSpec((CONV_HALO, D), lambda i: (jnp.maximum(i * nb - 1, 0), 0)),
                  pl.BlockSpec((CONV_HALO, D), lambda i: (0, 0)), vec, vec, vec],
        out_specs=[pl.BlockSpec((tm, D), lambda i: (i, 0)), pl.BlockSpec((tm, D), lambda i: (i, 0))],
        out_shape=[jax.ShapeDtypeStruct((S, D), F32), jax.ShapeDtypeStruct((S, D), BF16)],
        compiler_params=_params(("parallel",)),
    )(u, u, w, b_dw, ln_g, ln_b)


def conv_bwd(dz, u, w_dw):
    S, D = u.shape
    tm = _tile(S, (256, 128))
    rc = 32
    first_tap = CONV_HALO - (CONV_WIDTH - 1)
    w = jnp.concatenate([w_dw, jnp.zeros((CONV_HALO - CONV_WIDTH, D), F32)], axis=0)
    n_tiles = S // tm
    nb = tm // CONV_HALO

    def body(dz_ref, dzn_ref, u_ref, up_ref, w_ref, du_ref, dw_ref):
        i = pl.program_id(0)
        nxt = jnp.where(i == n_tiles - 1, jnp.zeros((CONV_HALO, D), F32), dzn_ref[...])
        dzs = _shifted(jnp.concatenate([dz_ref[...], nxt], axis=0), tm + CONV_HALO)
        for c0 in range(0, tm, rc):
            acc = jnp.zeros((rc, D), F32)
            for m in range(CONV_WIDTH):
                a8, b = m // 8 * 8, m % 8
                acc = acc + w_ref[CONV_WIDTH - 1 - m:CONV_WIDTH - m, :] * dzs[b][c0 + a8:c0 + a8 + rc, :]
            du_ref[c0:c0 + rc, :] = acc
        prev = jnp.where(i == 0, jnp.zeros((CONV_HALO, D), F32), up_ref[...])
        us = _shifted(jnp.concatenate([prev, u_ref[...]], axis=0), tm + CONV_HALO)
        dz = dz_ref[...]

        @pl.when(i == 0)
        def _():
            dw_ref[...] = jnp.zeros_like(dw_ref)

        for k in range(CONV_WIDTH):
            off = first_tap + k
            a8, b = off // 8 * 8, off % 8
            dw_ref[k:k + 1, :] += jnp.sum(dz * us[b][a8:a8 + tm, :], axis=0, keepdims=True)

    last_blk = S // CONV_HALO - 1
    du, dw = pl.pallas_call(
        body, name="conv_bwd",
        grid=(n_tiles,),
        in_specs=[pl.BlockSpec((tm, D), lambda i: (i, 0)),
                  pl.BlockSpec((CONV_HALO, D), lambda i: (jnp.minimum((i + 1) * nb, last_blk), 0)),
                  pl.BlockSpec((tm, D), lambda i: (i, 0)),
                  pl.BlockSpec((CONV_HALO, D), lambda i: (jnp.maximum(i * nb - 1, 0), 0)),
                  pl.BlockSpec((CONV_HALO, D), lambda i: (0, 0))],
        out_specs=[pl.BlockSpec((tm, D), lambda i: (i, 0)), pl.BlockSpec((CONV_HALO, D), lambda i: (0, 0))],
        out_shape=[jax.ShapeDtypeStruct((S, D), F32), jax.ShapeDtypeStruct((CONV_HALO, D), F32)],
        compiler_params=_params(("arbitrary",)),
    )(dz, dz, u, u, w)
    return du, dw[:CONV_WIDTH]


def conv_module_fwd(h, g, sh, sc, gate, p):
    D = h.shape[1]
    hn = norm_mod(h, g, sh, sc, "conv_norm_mod")
    pre = mm(hn, p["w_pw1"], "nn", "conv_pw1")
    ba, bg = p["b_pw1"][:, :D], p["b_pw1"][:, D:]

    def glu(a, gt, ba, bg):
        return (a + ba) * _sigmoid(gt + bg)
    u = rowwise(glu, [(pre, D, 0), (pre, D, 1)], [ba, bg], [(D, F32)], [], "conv_glu")[0]
    z, s = conv_fwd(u, p["w_dw"], p["b_dw"], p["ln_g"], p["ln_b"])
    yraw = mm(s, p["w_pw2"], "nn", "conv_pw2")
    h_out, y = residual(h, yraw, gate, 1.0, "conv_residual", bias=p["b_pw2"])
    return h_out, (h, hn, pre, u, z, s, y)


def conv_module_bwd(dh_out, saved, g, sc, gate, p):
    h, hn, pre, u, z, s, y = saved
    D = h.shape[1]
    dy, d_gate, d_b_pw2 = residual_bwd(dh_out, y, gate, 1.0, "conv_residual_bwd", with_bias_sum=True)
    d_w_pw2 = mm(s, dy, "tn", "conv_pw2_dw")
    ds = mm(dy, p["w_pw2"], "nt", "conv_pw2_dx")

    def ln_bwd(z, ds, g, beta):
        mu = jnp.mean(z, axis=-1, keepdims=True)
        zc = z - mu
        r = lax.rsqrt(jnp.mean(zc * zc, axis=-1, keepdims=True) + EPS)
        xhat = zc * r
        un = xhat * g + beta
        sig = _sigmoid(un)
        d_un = ds * (sig * (1 + un * (1 - sig)))
        dxhat = d_un * g
        dz = r * (dxhat - jnp.mean(dxhat, axis=-1, keepdims=True)
                  - xhat * jnp.mean(dxhat * xhat, axis=-1, keepdims=True))
        return dz, d_un * xhat, d_un, dz
    dz, d_ln_g, d_ln_b, d_b_dw = rowwise(ln_bwd, [z, ds], [p["ln_g"], p["ln_b"]], [(D, F32)], [D, D, D],
                                         "conv_ln_bwd")
    du, d_w_dw = conv_bwd(dz, u, p["w_dw"])
    ba, bg = p["b_pw1"][:, :D], p["b_pw1"][:, D:]

    def glu_bwd(a, gt, du, ba, bg):
        sg = _sigmoid(gt + bg)
        da = du * sg
        dg = du * (a + ba) * (sg * (1 - sg))
        dpre = jnp.concatenate([da, dg], axis=1)
        return dpre.astype(BF16), dpre
    dpre, d_b_pw1 = rowwise(glu_bwd, [(pre, D, 0), (pre, D, 1), du], [ba, bg], [(2 * D, BF16)], [2 * D],
                            "conv_glu_bwd")
    d_w_pw1 = mm(hn, dpre, "tn", "conv_pw1_dw")
    dhn = mm(dpre, p["w_pw1"], "nt", "conv_pw1_dx")
    dh_in, d_sh, d_sc, d_g = norm_mod_bwd(h, dhn, dh_out, g, sc, "norm_mod_bwd")
    grads = dict(w_pw1=d_w_pw1, b_pw1=d_b_pw1, w_dw=d_w_dw, b_dw=d_b_dw, ln_g=d_ln_g, ln_b=d_ln_b,
                 w_pw2=d_w_pw2, b_pw2=d_b_pw2)
    return dh_in, (d_sh, d_sc, d_gate, d_g), grads


def _rope(x, c, s1, s2):
    n = x.shape[1]
    return x * c + pltpu.roll(x, n - QK_ROPE // 2, 1) * s1 + pltpu.roll(x, QK_ROPE // 2, 1) * s2


def _rope_t(dy, c, s1, s2):
    n = dy.shape[1]
    return dy * c + pltpu.roll(dy * s1, QK_ROPE // 2, 1) + pltpu.roll(dy * s2, n - QK_ROPE // 2, 1)


def rope_tables(positions):
    inv_freq = ROPE_THETA ** (-jnp.arange(0, QK_ROPE, 2, dtype=F32) / QK_ROPE)
    ang = positions.astype(F32)[:, None] * inv_freq
    cos, sin = jnp.cos(ang), jnp.sin(ang)
    S = positions.shape[0]
    one = jnp.ones((S, QK_NOPE), F32)
    z16 = jnp.zeros((S, QK_ROPE // 2), F32)
    zn = jnp.zeros((S, QK_NOPE), F32)
    zt = jnp.zeros((S, HEAD_PAD - QK_NOPE - QK_ROPE), F32)
    c = jnp.concatenate([one, cos, cos, zt], axis=1)
    s1 = jnp.concatenate([zn, -sin, z16, zt], axis=1)
    s2 = jnp.concatenate([zn, z16, sin, zt], axis=1)
    return c, s1, s2


def attn_fwd(qr, kv, kpe, n_heads):
    S = qr.shape[0]
    H = n_heads
    tk = _tile(S, (ATTN_TILE,))
    nk = S // tk
    w = 2 if nk % 2 == 0 else 1
    tq = w * tk
    c2 = (QK_NOPE + QK_ROPE) ** -0.5 * LOG2_E
    nt = (((1,), (1,)), ((), ()))

    assert V_HEAD < HEAD_PAD
    ones_row = HEAD_PAD - 1

    def body(q_ref, k_ref, v_ref, kpe_ref, o_ref, lse_ref, kf_ref, vt_ref, m_ref, acc_ref):
        qi = pl.program_id(1)
        feature = lax.broadcasted_iota(jnp.int32, (HEAD_PAD, tk), 0)

        @pl.when(qi == 0)
        def _():
            kf_ref[...] = k_ref[...] + kpe_ref[...]
            for c in range(nk):
                vt = jnp.transpose(v_ref[c * tk:(c + 1) * tk, :].astype(F32))
                vt_ref[c] = jnp.where(feature == ones_row, 1.0, vt).astype(BF16)

        q = q_ref[...]
        m_ref[...] = jnp.full((1, tq), -jnp.inf, F32)
        acc_ref[...] = jnp.zeros((HEAD_PAD, tq), F32)

        def tile(j, first_visible):
            k = kf_ref[pl.ds(pl.multiple_of(j * tk, tk), tk), :]
            t = lax.dot_general(k, q, nt, preferred_element_type=F32) * c2
            if first_visible is not None:
                krow = lax.broadcasted_iota(jnp.int32, (tk, tq), 0)
                qcol = lax.broadcasted_iota(jnp.int32, (tk, tq), 1)
                t = jnp.where(krow + first_visible <= qcol, t, NEG)
            m_old = m_ref[...]
            m_new = jnp.maximum(m_old, jnp.max(t, axis=0, keepdims=True))
            alpha = jnp.exp2(m_old - m_new)
            p = jnp.exp2(t - m_new)
            acc_ref[...] = alpha * acc_ref[...] + jnp.dot(vt_ref[j], p.astype(BF16), preferred_element_type=F32)
            m_ref[...] = m_new

        def unmasked(j, carry):
            tile(j, None)
            return carry

        lax.fori_loop(0, w * qi, unmasked, 0)
        for u in range(w):
            tile(w * qi + u, u * tk)
        acc = acc_ref[...]
        l = acc_ref[ones_row:ones_row + 1, :]
        out_feature = lax.broadcasted_iota(jnp.int32, (HEAD_PAD, tq), 0)
        o_ref[...] = jnp.transpose(jnp.where(out_feature == ones_row, 0.0, acc / l))
        lse = m_ref[...] + jnp.log(l) * LOG2_E
        for u in range(w):
            lse_ref[u] = lse[:, u * tk:(u + 1) * tk]

    return pl.pallas_call(
        body, name="attn_fwd",
        grid=(H, S // tq),
        in_specs=[pl.BlockSpec((tq, HEAD_PAD), lambda h, i: (i, h)),
                  pl.BlockSpec((S, HEAD_PAD), lambda h, i: (0, h)),
                  pl.BlockSpec((S, HEAD_PAD), lambda h, i: (0, H + h)),
                  pl.BlockSpec((S, HEAD_PAD), lambda h, i: (0, 0))],
        out_specs=[pl.BlockSpec((tq, HEAD_PAD), lambda h, i: (i, h)),
                   pl.BlockSpec((None, w, 1, tk), lambda h, i: (h, i, 0, 0))],
        out_shape=[jax.ShapeDtypeStruct((S, H * HEAD_PAD), F32), jax.ShapeDtypeStruct((H, nk, 1, tk), F32)],
        scratch_shapes=[pltpu.VMEM((S, HEAD_PAD), BF16), pltpu.VMEM((nk, HEAD_PAD, tk), BF16),
                        pltpu.VMEM((1, tq), F32), pltpu.VMEM((HEAD_PAD, tq), F32)],
        compiler_params=_params(("parallel", "arbitrary")),
    )(qr, kv, kv, kpe)


def attn_delta(o, do, n_heads):
    S = o.shape[0]
    H = n_heads
    tq = _tile(S, (ATTN_TILE,))
    nq = S // tq

    def body(o_ref, do_ref, d_ref):
        for c in range(nq):
            rows = slice(c * tq, (c + 1) * tq)
            prod = o_ref[rows, :] * do_ref[rows, :].astype(F32)
            d_ref[c] = jnp.sum(jnp.transpose(prod), axis=0, keepdims=True)

    return pl.pallas_call(
        body, name="attn_delta",
        grid=(H,),
        in_specs=[pl.BlockSpec((S, HEAD_PAD), lambda h: (0, h)), pl.BlockSpec((S, HEAD_PAD), lambda h: (0, h))],
        out_specs=pl.BlockSpec((None, nq, 1, tq), lambda h: (h, 0, 0, 0)),
        out_shape=jax.ShapeDtypeStruct((H, nq, 1, tq), F32),
        compiler_params=_params(("parallel",)),
    )(o, do)


def attn_bwd(qr, kv, kpe, do, lse2, delta, n_heads):
    S = qr.shape[0]
    H = n_heads
    tk = _tile(S, (ATTN_TILE,))
    nk = S // tk
    w = 2 if nk % 2 == 0 else 1
    tq = w * tk
    nq = S // tq
    scale = (QK_NOPE + QK_ROPE) ** -0.5
    c2 = scale * LOG2_E
    nt = (((1,), (1,)), ((), ()))
    lse2 = lse2.reshape(H, nq, 1, tq)
    delta4 = delta.reshape(H, nq, 1, tq)

    def body(k_ref, v_ref, kpe_ref, q_ref, do_ref, lse_ref, dl_ref, dq_ref, dk_ref, dv_ref, dka_ref, dva_ref,
             dqt_ref):
        kj = pl.program_id(1)
        k = k_ref[...] + kpe_ref[...]
        kt = jnp.transpose(k.astype(F32)).astype(BF16)
        v = v_ref[...]

        @pl.when(kj == 0)
        def _():
            dqt_ref[...] = jnp.zeros_like(dqt_ref)

        dka_ref[...] = jnp.zeros_like(dka_ref)
        dva_ref[...] = jnp.zeros_like(dva_ref)

        def tile(i, masked):
            start = pl.multiple_of(i * tq, tq)
            q = q_ref[pl.ds(start, tq), :]
            do = do_ref[pl.ds(start, tq), :]
            t = lax.dot_general(k, q, nt, preferred_element_type=F32) * c2
            if masked:
                krow = lax.broadcasted_iota(jnp.int32, (tk, tq), 0)
                qcol = lax.broadcasted_iota(jnp.int32, (tk, tq), 1)
                t = jnp.where(krow + (kj % w) * tk <= qcol, t, NEG)
            pt = jnp.exp2(t - lse_ref[i])
            dva_ref[...] += jnp.dot(pt.astype(BF16), do, preferred_element_type=F32)
            dpt = lax.dot_general(v, do, nt, preferred_element_type=F32)
            dst = (pt * (dpt - dl_ref[i]) * scale).astype(BF16)
            dka_ref[...] += jnp.dot(dst, q, preferred_element_type=F32)
            dqt_ref[i] += jnp.dot(kt, dst, preferred_element_type=F32)

        tile(kj // w, True)

        def unmasked(i, carry):
            tile(i, False)
            return carry

        lax.fori_loop(kj // w + 1, nq, unmasked, 0)
        dk_ref[...] = dka_ref[...]
        dv_ref[...] = dva_ref[...]

        @pl.when(kj == nk - 1)
        def _():
            for c in range(nq):
                dq_ref[c * tq:(c + 1) * tq, :] = jnp.transpose(dqt_ref[c])

    blk = pl.BlockSpec((tk, HEAD_PAD), lambda h, j: (j, h))
    whole = pl.BlockSpec((S, HEAD_PAD), lambda h, j: (0, h))
    stat = pl.BlockSpec((None, nq, 1, tq), lambda h, j: (h, 0, 0, 0))
    shp = jax.ShapeDtypeStruct((S, H * HEAD_PAD), F32)
    return pl.pallas_call(
        body, name="attn_bwd",
        grid=(H, nk),
        in_specs=[blk, pl.BlockSpec((tk, HEAD_PAD), lambda h, j: (j, H + h)),
                  pl.BlockSpec((tk, HEAD_PAD), lambda h, j: (j, 0)), whole, whole, stat, stat],
        out_specs=[whole, blk, blk],
        out_shape=[shp, shp, shp],
        scratch_shapes=[pltpu.VMEM((tk, HEAD_PAD), F32), pltpu.VMEM((tk, HEAD_PAD), F32),
                        pltpu.VMEM((nq, HEAD_PAD, tq), F32)],
        compiler_params=_params(("parallel", "arbitrary")),
    )(kv, kv, kpe, qr, do, lse2, delta4)


def _pad_heads(w, width):
    R = w.shape[0]
    w3 = w.reshape(R, -1, width)
    return jnp.pad(w3, ((0, 0), (0, 0), (0, HEAD_PAD - width))).reshape(R, -1)


def _unpad_heads(w, width):
    R = w.shape[0]
    return w.reshape(R, -1, HEAD_PAD)[:, :, :width].reshape(R, -1)


def mla_pad_weights(p):
    H = N_HEADS
    w_q_b = _pad_heads(p["w_q_b"], QK_NOPE + QK_ROPE)
    kvb = p["w_kv_b"].reshape(KV_LORA, H, QK_NOPE + V_HEAD)
    wk = _pad_heads(kvb[:, :, :QK_NOPE].reshape(KV_LORA, -1), QK_NOPE)
    wv = _pad_heads(kvb[:, :, QK_NOPE:].reshape(KV_LORA, -1), V_HEAD)
    D = p["w_kv_a"].shape[0]
    a = p["w_kv_a"]
    w_kv_a = jnp.concatenate([a[:, :KV_LORA], jnp.zeros((D, QK_NOPE), a.dtype), a[:, KV_LORA:],
                              jnp.zeros((D, HEAD_PAD - QK_NOPE - QK_ROPE), a.dtype)], axis=1)
    wo = p["w_o"].reshape(H, V_HEAD, -1)
    w_o = jnp.pad(wo, ((0, 0), (0, HEAD_PAD - V_HEAD), (0, 0))).reshape(H * HEAD_PAD, -1)
    return dict(w_q_a=p["w_q_a"], w_q_b=w_q_b, w_kv_b=jnp.concatenate([wk, wv], axis=1), w_kv_a=w_kv_a, w_o=w_o)


def mla_kv_fwd(h, g, sh, sc, kv_a_norm_g, pw, tabs):
    hkv = norm_mod(h, g, sh, sc, "kv_norm_mod")
    ckvp = mm(hkv, pw["w_kv_a"], "nn", "kv_a")

    def f(ckv, kpe, c, s1, s2, g):
        xhat, _ = _rms(ckv)
        return (xhat * g).astype(BF16), _rope(kpe, c, s1, s2).astype(BF16)
    ckv_n, kpe_r = rowwise(f, [(ckvp, KV_LORA, 0), (ckvp, HEAD_PAD, KV_LORA // HEAD_PAD), *tabs], [kv_a_norm_g],
                           [(KV_LORA, BF16), (HEAD_PAD, BF16)], [], "kv_a_norm_rope")
    kv = mm(ckv_n, pw["w_kv_b"], "nn", "kv_b", out_dtype=BF16)
    return kv, kpe_r, (h, hkv, ckvp, ckv_n)


def mla_kv_bwd(dh_stream, dk, dv, saved, g, sc, kv_a_norm_g, pw, tabs):
    h, hkv, ckvp, ckv_n = saved
    H = N_HEADS
    lane = jnp.arange(HEAD_PAD)
    pe_mask = ((lane >= QK_NOPE) & (lane < QK_NOPE + QK_ROPE)).astype(F32)[None, :]

    def f(dk, dv, c, s1, s2, mask):
        tot = dk[:, :HEAD_PAD]
        for hh in range(1, H):
            tot = tot + dk[:, hh * HEAD_PAD:(hh + 1) * HEAD_PAD]
        dkpe = _rope_t(tot * mask, c, s1, s2) * mask
        return jnp.concatenate([dk, dv], axis=1).astype(BF16), dkpe
    dkv, dkpe = rowwise(f, [dk, dv, *tabs], [pe_mask], [(2 * H * HEAD_PAD, BF16), (HEAD_PAD, F32)], [],
                        "kv_split_bwd")
    d_w_kv_b = mm(ckv_n, dkv, "tn", "kv_b_dw")
    dckv_n = mm(dkv, pw["w_kv_b"], "nt", "kv_b_dx")

    def f2(ckv, dn, dkpe, g):
        xhat, r = _rms(ckv)
        dx = _rms_bwd(xhat, r, dn * g)
        return jnp.concatenate([dx, dkpe], axis=1).astype(BF16), dn * xhat
    dckvp, d_kv_a_g = rowwise(f2, [(ckvp, KV_LORA, 0), dckv_n, dkpe], [kv_a_norm_g],
                              [(KV_LORA + HEAD_PAD, BF16)], [KV_LORA], "kv_a_norm_bwd")
    d_w_kv_a = mm(hkv, dckvp, "tn", "kv_a_dw")
    dhkv = mm(dckvp, pw["w_kv_a"], "nt", "kv_a_dx")
    dh, d_sh, d_sc, d_g = norm_mod_bwd(h, dhkv, dh_stream, g, sc, "norm_mod_bwd")
    return dh, (d_sh, d_sc, d_g), d_kv_a_g, d_w_kv_a, d_w_kv_b


def mla_fwd(h, g, sh, sc, gate, q_a_norm_g, pw, kv, kpe_r, tabs):
    H = N_HEADS
    hn = norm_mod(h, g, sh, sc, "mla_norm_mod")
    qa = mm(hn, pw["w_q_a"], "nn", "q_a")

    def f(qa, g):
        xhat, _ = _rms(qa)
        return (xhat * g).astype(BF16)
    qa_n = rowwise(f, [qa], [q_a_norm_g], [(qa.shape[1], BF16)], [], "q_a_norm")[0]
    qp = mm(qa_n, pw["w_q_b"], "nn", "q_b")

    def frope(q, c, s1, s2):
        return jnp.concatenate([_rope(q[:, hh * HEAD_PAD:(hh + 1) * HEAD_PAD], c, s1, s2) for hh in range(H)],
                               axis=1).astype(BF16)
    qr = rowwise(frope, [qp, *tabs], [], [(H * HEAD_PAD, BF16)], [], "q_rope")[0]
    o, lse = attn_fwd(qr, kv, kpe_r, H)
    y = mm(o, pw["w_o"], "nn", "w_o")
    h_out, _ = residual(h, y, gate, 1.0, "mla_residual")
    return h_out, (h, hn, qa, qa_n, qr, o, lse, y)


def mla_bwd(dh_out, saved, g, sc, gate, q_a_norm_g, pw, kv, kpe_r, tabs):
    h, hn, qa, qa_n, qr, o, lse, y = saved
    H = N_HEADS
    dy, d_gate = residual_bwd(dh_out, y, gate, 1.0, "mla_residual_bwd")
    d_w_o = mm(o, dy, "tn", "w_o_dw")
    do = mm(dy, pw["w_o"], "nt", "w_o_dx", out_dtype=BF16)
    delta = attn_delta(o, do, H)
    dqr, dk, dv = attn_bwd(qr, kv, kpe_r, do, lse, delta, H)

    def frope_t(dq, c, s1, s2):
        return jnp.concatenate([_rope_t(dq[:, hh * HEAD_PAD:(hh + 1) * HEAD_PAD], c, s1, s2) for hh in range(H)],
                               axis=1).astype(BF16)
    dqp = rowwise(frope_t, [dqr, *tabs], [], [(H * HEAD_PAD, BF16)], [], "q_rope_bwd")[0]
    d_w_q_b = mm(qa_n, dqp, "tn", "q_b_dw")
    dqa_n = mm(dqp, pw["w_q_b"], "nt", "q_b_dx")

    def f(qa, dn, g):
        xhat, r = _rms(qa)
        return _rms_bwd(xhat, r, dn * g).astype(BF16), dn * xhat
    dqa, d_q_a_g = rowwise(f, [qa, dqa_n], [q_a_norm_g], [(qa.shape[1], BF16)], [qa.shape[1]], "q_a_norm_bwd")
    d_w_q_a = mm(hn, dqa, "tn", "q_a_dw")
    dhn = mm(dqa, pw["w_q_a"], "nt", "q_a_dx")
    dh_in, d_sh, d_sc, d_g = norm_mod_bwd(h, dhn, dh_out, g, sc, "norm_mod_bwd")
    grads = dict(w_q_a=d_w_q_a, q_a_norm_g=d_q_a_g, w_q_b=d_w_q_b, w_o=d_w_o)
    return dh_in, (d_sh, d_sc, d_gate, d_g), grads, dk, dv


def loss_head(h, target, g):
    D = h.shape[1]

    def f(h, t, g):
        xhat, r = _rms(h)
        err = xhat * g - t
        dy = err * (1.0 / D)
        dh = _rms_bwd(xhat, r, dy * g)
        return dh, (0.5 / D) * err * err, dy * xhat
    return rowwise(f, [h, target], [g], [(D, F32)], [D, D], "loss_head")


def _place():
    x, y, c = lax.axis_index("x"), lax.axis_index("y"), lax.axis_index("c")
    chips = [(1 - x, y), (x, 1 - y), (1 - x, 1 - y)]
    return x, y, c, chips


HBM_SPEC = pl.BlockSpec(memory_space=pltpu.HBM)


def all_gather8(v):
    m, n = v.shape

    def body(x_ref, out_ref, send_sems, recv_sems, local_sem):
        x, y, c, chips = _place()
        me, sibling = (x, y, c), (x, y, 1 - c)

        def rows(px, py, pc):
            return out_ref.at[4 * px + 2 * py + pc]

        def copy(k, block, to, src=None):
            return pltpu.make_async_remote_copy(
                src_ref=rows(*block) if src is None else src, dst_ref=rows(*block),
                send_sem=send_sems.at[k], recv_sem=recv_sems.at[k], device_id=to, device_id_type=MESH)

        mine = pltpu.make_async_copy(x_ref, rows(*me), local_sem)
        mine.start()
        first = [copy(0, me, sibling, src=x_ref)]
        first += [copy(1 + j, me, (*chip, c), src=x_ref) for j, chip in enumerate(chips)]
        for cp in first:
            cp.start()
        passed = [copy(4 + j, (*chip, c), sibling) for j, chip in enumerate(chips)]
        for j, chip in enumerate(chips):
            copy(1 + j, (*chip, c), me).wait_recv()
            passed[j].start()
        copy(0, sibling, me).wait_recv()
        for j, chip in enumerate(chips):
            copy(4 + j, (*chip, 1 - c), me).wait_recv()
        for cp in first + passed:
            cp.wait_send()
        mine.wait()

    return pl.pallas_call(
        body, name="all_gather8",
        out_shape=jax.ShapeDtypeStruct((8, m, n), v.dtype),
        in_specs=[pl.BlockSpec(memory_space=pltpu.VMEM)],
        out_specs=pl.BlockSpec(memory_space=pltpu.VMEM),
        scratch_shapes=[pltpu.SemaphoreType.DMA((7,)), pltpu.SemaphoreType.DMA((7,)), pltpu.SemaphoreType.DMA],
        compiler_params=pltpu.CompilerParams(vmem_limit_bytes=VMEM_LIMIT_BYTES),
    )(v)


def gather_weights(bufs):
    n = len(bufs)

    def body(*refs):
        ins, outs = refs[:n], refs[n:2 * n]
        send_sems, recv_sems = refs[2 * n:]
        x, y, c, chips = _place()
        across_x, across_y, across_both = chips
        sibling = (x, y, 1 - c)
        me = 2 * x + y
        via_in = (x + (1 - c) * (1 - 2 * x), y + c * (1 - 2 * y))
        via_out = (x + c * (1 - 2 * x), y + (1 - c) * (1 - 2 * y))

        def idx(chip):
            return 2 * chip[0] + chip[1]

        def copy(w, k, src, dst, to):
            return pltpu.make_async_remote_copy(src_ref=src, dst_ref=dst, send_sem=send_sems.at[6 * w + k],
                                                recv_sem=recv_sems.at[6 * w + k], device_id=to, device_id_type=MESH)

        def landed(w, k, chip):
            blk = outs[w].at[idx(chip), c]
            copy(w, k, blk, blk, (*chip, c)).wait_recv()
            return blk

        sends = [copy(w, j, ins[w].at[me, c], outs[w].at[me, c], (*chip, c))
                 for w in range(n) for j, chip in enumerate((across_x, across_y))]
        for cp in sends:
            cp.start()
        for w in range(n):
            blk = landed(w, c, via_in)
            sends += [copy(w, 2, blk, blk, (*via_out, c)), copy(w, 3 + c, blk, blk, sibling)]
            sends[-2].start()
            sends[-1].start()
        for w in range(n):
            blk = landed(w, 1 - c, via_out)
            sends.append(copy(w, 4 - c, blk, blk, sibling))
            sends[-1].start()
        for w in range(n):
            blk = landed(w, 2, across_both)
            sends.append(copy(w, 5, blk, blk, sibling))
            sends[-1].start()
        for w in range(n):
            for j, chip in enumerate(chips):
                other = outs[w].at[idx(chip), 1 - c]
                copy(w, 3 + j, other, other, sibling).wait_recv()
        for cp in sends:
            cp.wait_send()

    return pl.pallas_call(
        body, name="gather_weights",
        out_shape=[jax.ShapeDtypeStruct(b.shape, b.dtype) for b in bufs],
        in_specs=[HBM_SPEC] * n, out_specs=[HBM_SPEC] * n,
        input_output_aliases={w: w for w in range(n)},
        scratch_shapes=[pltpu.SemaphoreType.DMA((6 * n,)), pltpu.SemaphoreType.DMA((6 * n,))],
    )(*bufs)


def exchange_halves(gs):
    n = len(gs)

    def body(*refs):
        ins, theirs = refs[:n], refs[n:2 * n]
        send_sems, recv_sems = refs[2 * n:]
        x, y, c, _ = _place()
        sends = [pltpu.make_async_remote_copy(src_ref=ins[w].at[:, 1 - c], dst_ref=theirs[w],
                                              send_sem=send_sems.at[w], recv_sem=recv_sems.at[w],
                                              device_id=(x, y, 1 - c), device_id_type=MESH) for w in range(n)]
        for cp in sends:
            cp.start()
        for cp in sends:
            cp.wait()

    return pl.pallas_call(
        body, name="exchange_halves",
        out_shape=[jax.ShapeDtypeStruct((4,) + g.shape[2:], g.dtype) for g in gs],
        in_specs=[HBM_SPEC] * n, out_specs=[HBM_SPEC] * n,
        scratch_shapes=[pltpu.SemaphoreType.DMA((n,)), pltpu.SemaphoreType.DMA((n,))],
    )(*gs)


def scatter_blocks(ps):
    n = len(ps)

    def body(*refs):
        ins, outs = refs[:n], refs[n:2 * n]
        send_sems, recv_sems = refs[2 * n:]
        x, y, c, chips = _place()
        sends = [pltpu.make_async_remote_copy(src_ref=ins[w].at[2 * chip[0] + chip[1]], dst_ref=outs[w].at[j],
                                              send_sem=send_sems.at[3 * w + j], recv_sem=recv_sems.at[3 * w + j],
                                              device_id=(*chip, c), device_id_type=MESH)
                 for w in range(n) for j, chip in enumerate(chips)]
        for cp in sends:
            cp.start()
        for cp in sends:
            cp.wait()

    return pl.pallas_call(
        body, name="scatter_blocks",
        out_shape=[jax.ShapeDtypeStruct((3,) + p.shape[1:], p.dtype) for p in ps],
        in_specs=[HBM_SPEC] * n, out_specs=[HBM_SPEC] * n,
        scratch_shapes=[pltpu.SemaphoreType.DMA((3 * n,)), pltpu.SemaphoreType.DMA((3 * n,))],
    )(*ps)


def join_halves(qs):
    n = len(qs)

    def body(*refs):
        ins, outs = refs[:n], refs[n:2 * n]
        send_sems, recv_sems = refs[2 * n:]
        x, y, c, _ = _place()
        sends = [pltpu.make_async_remote_copy(src_ref=ins[w].at[c], dst_ref=outs[w].at[c], send_sem=send_sems.at[w],
                                              recv_sem=recv_sems.at[w], device_id=(x, y, 1 - c), device_id_type=MESH)
                 for w in range(n)]
        for cp in sends:
            cp.start()
        for w in range(n):
            other = outs[w].at[1 - c]
            pltpu.make_async_remote_copy(src_ref=other, dst_ref=other, send_sem=send_sems.at[w],
                                         recv_sem=recv_sems.at[w], device_id=(x, y, 1 - c),
                                         device_id_type=MESH).wait_recv()
        for cp in sends:
            cp.wait_send()

    return pl.pallas_call(
        body, name="join_halves",
        out_shape=[jax.ShapeDtypeStruct(q.shape, q.dtype) for q in qs],
        in_specs=[HBM_SPEC] * n, out_specs=[HBM_SPEC] * n,
        input_output_aliases={w: w for w in range(n)},
        scratch_shapes=[pltpu.SemaphoreType.DMA((n,)), pltpu.SemaphoreType.DMA((n,))],
    )(*qs)


def _row_tile(R, row_bytes):
    tm = R
    for t in (512, 256, 128, 64, 32, 16, 8):
        if R % t == 0:
            tm = t
            if t * row_bytes <= ROW_TILE_BUDGET:
                break
    return tm


def sum_siblings(g, theirs, place):
    _, _, R, C = g.shape
    tm = _row_tile(R, 3 * C * 4)

    def body(place_ref, a_ref, b_ref, o_ref):
        o_ref[...] = (a_ref[...] + b_ref[...]).astype(BF16)

    return pl.pallas_call(
        body, name="sum_siblings",
        grid_spec=pltpu.PrefetchScalarGridSpec(
            num_scalar_prefetch=1, grid=(4, R // tm),
            in_specs=[pl.BlockSpec((None, None, tm, C), lambda j, i, s: (j, s[1], i, 0)),
                      pl.BlockSpec((None, tm, C), lambda j, i, s: (j, i, 0))],
            out_specs=pl.BlockSpec((None, tm, C), lambda j, i, s: (j, i, 0))),
        out_shape=jax.ShapeDtypeStruct((4, R, C), BF16),
        compiler_params=_params(("parallel", "parallel")),
    )(place, g, theirs)


def sum_chips(p, landed, place):
    _, R, C = p.shape
    tm = _row_tile(R, 5 * C * 4)

    def body(place_ref, p_ref, l0_ref, l1_ref, l2_ref, o_ref):
        o_ref[...] = ((p_ref[...].astype(F32) + l0_ref[...].astype(F32)) + l1_ref[...].astype(F32)
                      ) + l2_ref[...].astype(F32)

    return pl.pallas_call(
        body, name="sum_chips",
        grid_spec=pltpu.PrefetchScalarGridSpec(
            num_scalar_prefetch=1, grid=(R // tm,),
            in_specs=[pl.BlockSpec((None, tm, C), lambda i, s: (s[0], i, 0))]
            + [pl.BlockSpec((None, tm, C), lambda i, s, j=j: (j, i, 0)) for j in range(3)],
            out_specs=pl.BlockSpec((None, tm, C), lambda i, s: (s[1], i, 0))),
        out_shape=jax.ShapeDtypeStruct((2, R, C), F32),
        compiler_params=_params(("parallel",)),
    )(place, p, landed, landed, landed)


def sum_blocks(items, name):
    R, C = items[0][0].shape[1:]
    tm = R
    for t in (512, 256, 128, 64, 32, 16, 8):
        if R % t == 0:
            tm = t
            if t * C * 4 * (len(items) + 1) <= ROW_TILE_BUDGET:
                break
    n = len(items)

    def body(*refs):
        acc = refs[0][...].astype(F32)
        for r in refs[1:n]:
            acc = acc + r[...].astype(F32)
        refs[n][...] = acc

    return pl.pallas_call(
        body, name=name,
        grid=(R // tm,),
        in_specs=[pl.BlockSpec((None, tm, C), lambda i, j=j: (j, i, 0)) for _, j in items],
        out_specs=pl.BlockSpec((tm, C), lambda i: (i, 0)),
        out_shape=jax.ShapeDtypeStruct((R, C), F32),
        compiler_params=_params(("parallel",)),
    )(*[a for a, _ in items])


def reduce_scatter_grads(gs, place):
    theirs = exchange_halves(gs)
    ps = [sum_siblings(g, t, place) for g, t in zip(gs, theirs)]
    landed = scatter_blocks(ps)
    qs = [sum_chips(p, l, place) for p, l in zip(ps, landed)]
    joined = join_halves(qs)
    return [j.reshape(2 * j.shape[1], j.shape[2]) for j in joined]


def adamw(w, g, m, v):
    shape = w.shape
    C = shape[-1]
    R = w.size // C
    tm = R
    for t in (512, 256, 128, 64, 32, 16, 8):
        if R % t == 0:
            tm = t
            if t * C * 4 * 7 <= ROW_TILE_BUDGET:
                break

    def f(w, g, m, v):
        m = ADAM_B1 * m + (1.0 - ADAM_B1) * g
        v = ADAM_B2 * v + (1.0 - ADAM_B2) * (g * g)
        m_hat = m / (1.0 - ADAM_B1 ** ADAM_STEP)
        v_hat = v / (1.0 - ADAM_B2 ** ADAM_STEP)
        delta = -ADAM_LR * (m_hat / (jnp.sqrt(v_hat) + ADAM_EPS) + ADAM_WD * w)
        return delta, m, v

    d, nm, nv = rowwise(f, [a.reshape(R, C) for a in (w, g, m, v)], [], [(C, F32)] * 3, [], "adamw", tm=tm)
    return d.reshape(shape), nm.reshape(shape), nv.reshape(shape)


def _cast_into_slot(w, place):
    C = w.shape[-1]
    w2 = w.reshape(-1, C)
    R = w2.shape[0]
    tm = _row_tile(R, 6 * C)

    def body(place_ref, w_ref, o_ref):
        o_ref[...] = w_ref[...].astype(BF16)

    out = pl.pallas_call(
        body, name="cast_bf16",
        grid_spec=pltpu.PrefetchScalarGridSpec(
            num_scalar_prefetch=1, grid=(R // tm,),
            in_specs=[pl.BlockSpec((tm, C), lambda i, s: (i, 0))],
            out_specs=pl.BlockSpec((None, tm, C), lambda i, s: (s[0], i, 0))),
        out_shape=jax.ShapeDtypeStruct((4, R, C), BF16),
        compiler_params=_params(("parallel",)),
    )(place, w2)
    return out.reshape(4, 2, R // 2, C)


def _pack(vs):
    flat = jnp.concatenate([v.reshape(-1) for v in vs])
    n = flat.shape[0]
    total = -(-n // 1024) * 1024
    return jnp.pad(flat, (0, total - n)).reshape(total // 128, 128)


def _unpack(flat, like):
    out, o = [], 0
    for shp in like:
        sz = 1
        for d in shp:
            sz *= d
        out.append(flat[o:o + sz].reshape(shp))
        o += sz
    return out


def _cols_to_blocks(g, n_chips=4):
    R, N = g.shape
    C = N // n_chips
    return g.reshape(R, n_chips, C).transpose(1, 0, 2).reshape(n_chips, 2, R // 2, C)


def _rows_to_blocks(g, n_chips=4):
    R, C = g.shape
    return g.reshape(n_chips, 2, R // n_chips // 2, C)


def kernel(x, c, positions, ada_w, ada_b, norm_g, ffn_w13, ffn_w2, conv_w_pw1, conv_b_pw1, conv_w_dw, conv_b_dw, conv_ln_g, conv_ln_b, conv_w_pw2, conv_b_pw2, kv_ada_w, kv_ada_b, kv_norm_g, w_kv_a, kv_a_norm_g, w_kv_b, w_q_a, q_a_norm_g, w_q_b, w_o, final_norm_g, loss_target, m_ada_w, m_ada_b, m_norm_g, m_ffn_w13, m_ffn_w2, m_conv_w_pw1, m_conv_b_pw1, m_conv_w_dw, m_conv_b_dw, m_conv_ln_g, m_conv_ln_b, m_conv_w_pw2, m_conv_b_pw2, m_kv_ada_w, m_kv_ada_b, m_kv_norm_g, m_w_kv_a, m_kv_a_norm_g, m_w_kv_b, m_w_q_a, m_q_a_norm_g, m_w_q_b, m_w_o, m_final_norm_g, v_ada_w, v_ada_b, v_norm_g, v_ffn_w13, v_ffn_w2, v_conv_w_pw1, v_conv_b_pw1, v_conv_w_dw, v_conv_b_dw, v_conv_ln_g, v_conv_ln_b, v_conv_w_pw2, v_conv_b_pw2, v_kv_ada_w, v_kv_ada_b, v_kv_norm_g, v_w_kv_a, v_kv_a_norm_g, v_w_kv_b, v_w_q_a, v_q_a_norm_g, v_w_q_b, v_w_o, v_final_norm_g):
    S, D = x.shape[1], x.shape[2]
    H = N_HEADS
    F = ffn_w2.shape[2] * 4
    xi, yi, ci = lax.axis_index("x"), lax.axis_index("y"), lax.axis_index("c")
    chip = 2 * xi + yi
    dev = 2 * chip + ci
    place = jnp.stack([chip, ci]).astype(jnp.int32)
    h0 = x[0]
    target = loss_target[0]

    silu_c = rowwise(lambda a: a * _sigmoid(a), [c], [], [(D, F32)], [], "silu_c")[0]
    silu_all = all_gather8(silu_c.reshape(8, D // 8)).reshape(8, D)
    n_ada = ada_w.shape[2]
    n_kv = kv_ada_w.shape[1]
    ada_b_mine = lax.dynamic_slice_in_dim(ada_b, chip * n_ada, n_ada, axis=1)
    kv_b_mine = lax.dynamic_slice_in_dim(kv_ada_b, chip * n_kv, n_kv, axis=0)[None, :]
    mods = [mm(silu_all, ada_w[l], "nn", "ada_rows", bias=ada_b_mine[l:l + 1]) for l in range(2)]
    mods.append(mm(silu_all, kv_ada_w, "nn", "kv_ada_rows", bias=kv_b_mine))
    n_mod_cols = 2 * n_ada + n_kv
    mod_pack = jnp.concatenate(mods, axis=1).reshape(-1, 128)
    mod_all = all_gather8(mod_pack).reshape(8, 8, n_mod_cols)[0::2]
    mod_mine = lax.dynamic_index_in_dim(mod_all, dev, axis=1, keepdims=False)
    mod = [mod_mine[:, l * n_ada:(l + 1) * n_ada].reshape(N_MOD, D) for l in range(2)]
    kv_mod = mod_mine[:, 2 * n_ada:].reshape(2, D)
    kv_shift, kv_scale = kv_mod[0:1], kv_mod[1:2]

    def mrow(l, k):
        return mod[l][k:k + 1]

    big = dict(ffn_w13=ffn_w13, ffn_w2=ffn_w2, conv_w_pw1=conv_w_pw1, conv_w_pw2=conv_w_pw2, w_kv_a=w_kv_a,
               w_kv_b=w_kv_b, w_q_a=w_q_a, w_q_b=w_q_b, w_o=w_o)
    names = list(big)
    gathered = gather_weights([_cast_into_slot(big[k], place) for k in names])
    gw = dict(zip(names, gathered))
    small_like = [norm_g.shape, conv_b_pw1.shape, conv_w_dw.shape, conv_b_dw.shape, conv_ln_g.shape,
                  conv_ln_b.shape, conv_b_pw2.shape]
    small_pack = _pack([norm_g, conv_b_pw1, conv_w_dw, conv_b_dw, conv_ln_g, conv_ln_b, conv_b_pw2])
    small_all = all_gather8(small_pack)[0::2].reshape(4, -1)
    per_chip = [_unpack(small_all[j], small_like) for j in range(4)]
    smalls = [jnp.concatenate([per_chip[j][k] for j in range(4)], axis=-1) for k in range(len(small_like))]
    norm_g_f, b_pw1_f, w_dw_f, b_dw_f, ln_g_f, ln_b_f, b_pw2_f = smalls

    gw13 = gw["ffn_w13"].reshape(4, 2, 2, D, F // 2)
    w2 = gw["ffn_w2"].reshape(4, 2, 2, F // 4, D).transpose(1, 2, 0, 3, 4).reshape(2, 2, F, D)
    conv_p = dict(
        w_pw1=gw["conv_w_pw1"].reshape(4, D, 2 * D // 4).transpose(1, 0, 2).reshape(D, 2 * D),
        b_pw1=b_pw1_f, w_dw=w_dw_f[0], b_dw=b_dw_f, ln_g=ln_g_f, ln_b=ln_b_f,
        w_pw2=gw["conv_w_pw2"].reshape(D, D), b_pw2=b_pw2_f)
    q_lora = w_q_a.shape[2]
    mla_p = dict(
        w_kv_a=gw["w_kv_a"].reshape(D, KV_LORA + QK_ROPE),
        w_kv_b=gw["w_kv_b"].reshape(4, KV_LORA, -1).transpose(1, 0, 2).reshape(KV_LORA, -1),
        w_q_a=gw["w_q_a"].reshape(D, q_lora),
        w_q_b=gw["w_q_b"].reshape(4, q_lora, -1).transpose(1, 0, 2).reshape(q_lora, -1),
        w_o=gw["w_o"].reshape(H * V_HEAD, D))
    pw = mla_pad_weights(mla_p)
    tabs = rope_tables(positions[0])

    def ng(l, k):
        return norm_g_f[l, k][None, :]

    h = h0
    h, s_f1_0 = ffn_fwd(h, ng(0, 0), mrow(0, 0), mrow(0, 1), mrow(0, 2), gw13, 0, 0, w2[0, 0])
    h, s_conv = conv_module_fwd(h, ng(0, 1), mrow(0, 3), mrow(0, 4), mrow(0, 5), conv_p)
    h, s_f2_0 = ffn_fwd(h, ng(0, 2), mrow(0, 6), mrow(0, 7), mrow(0, 8), gw13, 0, 1, w2[0, 1])
    kv_norm = kv_norm_g[None, :]
    kv_a_g = kv_a_norm_g[None, :]
    kv, kpe_r, s_kv = mla_kv_fwd(h, kv_norm, kv_shift, kv_scale, kv_a_g, pw, tabs)
    h, s_f1_1 = ffn_fwd(h, ng(1, 0), mrow(1, 0), mrow(1, 1), mrow(1, 2), gw13, 1, 0, w2[1, 0])
    h, s_mla = mla_fwd(h, ng(1, 1), mrow(1, 3), mrow(1, 4), mrow(1, 5), q_a_norm_g, pw, kv, kpe_r, tabs)
    h, s_f2_1 = ffn_fwd(h, ng(1, 2), mrow(1, 6), mrow(1, 7), mrow(1, 8), gw13, 1, 1, w2[1, 1])
    dh, loss_cols, d_final_g = loss_head(h, target, final_norm_g[None, :])

    dh, v_f2_1, dw13_11, dw2_11 = ffn_bwd(dh, s_f2_1, ng(1, 2), mrow(1, 7), mrow(1, 8), gw13, 1, 1, w2[1, 1])
    dh, v_mla, g_mla, dk, dv = mla_bwd(dh, s_mla, ng(1, 1), mrow(1, 4), mrow(1, 5), q_a_norm_g, pw, kv, kpe_r, tabs)
    dh, v_f1_1, dw13_10, dw2_10 = ffn_bwd(dh, s_f1_1, ng(1, 0), mrow(1, 1), mrow(1, 2), gw13, 1, 0, w2[1, 0])
    dh, v_kv, d_kv_a_g, d_w_kv_a, d_w_kv_b = mla_kv_bwd(dh, dk, dv, s_kv, kv_norm, kv_scale, kv_a_g, pw, tabs)
    dh, v_f2_0, dw13_01, dw2_01 = ffn_bwd(dh, s_f2_0, ng(0, 2), mrow(0, 7), mrow(0, 8), gw13, 0, 1, w2[0, 1])
    dh, v_conv, g_conv = conv_module_bwd(dh, s_conv, ng(0, 1), mrow(0, 4), mrow(0, 5), conv_p)
    dh, v_f1_0, dw13_00, dw2_00 = ffn_bwd(dh, s_f1_0, ng(0, 0), mrow(0, 1), mrow(0, 2), gw13, 0, 0, w2[0, 0])
    grad_x = dh[None]

    d_w_kv_a_u = jnp.concatenate([d_w_kv_a[:, :KV_LORA], d_w_kv_a[:, KV_LORA + QK_NOPE:KV_LORA + QK_NOPE + QK_ROPE]],
                                 axis=1)
    hk = H * HEAD_PAD
    dkb = jnp.concatenate([d_w_kv_b[:, :hk].reshape(KV_LORA, H, HEAD_PAD)[:, :, :QK_NOPE],
                           d_w_kv_b[:, hk:].reshape(KV_LORA, H, HEAD_PAD)[:, :, :V_HEAD]], axis=2).reshape(KV_LORA, -1)
    d_w_q_b_u = _unpad_heads(g_mla["w_q_b"], QK_NOPE + QK_ROPE)
    d_w_o_u = g_mla["w_o"].reshape(H, HEAD_PAD, D)[:, :V_HEAD].reshape(H * V_HEAD, D)
    full = [dw.reshape(4, 2, D // 2, F // 2) for dw in (dw13_00, dw13_01, dw13_10, dw13_11)] + [
            _rows_to_blocks(dw2_00), _rows_to_blocks(dw2_01), _rows_to_blocks(dw2_10), _rows_to_blocks(dw2_11),
            _cols_to_blocks(g_conv["w_pw1"]), _rows_to_blocks(g_conv["w_pw2"]), _rows_to_blocks(d_w_kv_a_u),
            _cols_to_blocks(dkb), _rows_to_blocks(g_mla["w_q_a"]), _cols_to_blocks(d_w_q_b_u),
            _rows_to_blocks(d_w_o_u)]
    red = reduce_scatter_grads(full, place)
    g_ffn_w13 = jnp.stack(red[0:4]).reshape(ffn_w13.shape)
    g_ffn_w2 = jnp.stack(red[4:8]).reshape(ffn_w2.shape)
    g_conv_w_pw1 = red[8].reshape(conv_w_pw1.shape)
    g_conv_w_pw2 = red[9].reshape(conv_w_pw2.shape)
    g_w_kv_a = red[10].reshape(w_kv_a.shape)
    g_w_kv_b = red[11].reshape(w_kv_b.shape)
    g_w_q_a = red[12].reshape(w_q_a.shape)
    g_w_q_b = red[13].reshape(w_q_b.shape)
    g_w_o = red[14].reshape(w_o.shape)

    def dmod(v1, vm, v2):
        return jnp.concatenate([v1[0], v1[1], v1[2], vm[0], vm[1], vm[2], v2[0], v2[1], v2[2]], axis=1)
    d_mod0 = dmod(v_f1_0, v_conv, v_f2_0)
    d_mod1 = dmod(v_f1_1, v_mla, v_f2_1)
    d_kv_mod = jnp.concatenate([v_kv[0], v_kv[1]], axis=1)
    d_norm_g = jnp.concatenate([v_f1_0[3], v_conv[3], v_f2_0[3], v_f1_1[3], v_mla[3], v_f2_1[3]], axis=0)
    vec_list = [d_mod0, d_mod1, d_kv_mod, d_norm_g, g_conv["b_pw1"], g_conv["w_dw"], g_conv["b_dw"], g_conv["ln_g"],
                g_conv["ln_b"], g_conv["b_pw2"], v_kv[2], d_kv_a_g, g_mla["q_a_norm_g"], d_final_g, loss_cols]
    vec_like = [v.shape for v in vec_list]
    vec_pack = _pack(vec_list)
    n_mod_rows = (2 * N_MOD * D + 2 * D) // 128
    vec_all = all_gather8(vec_pack)
    vec_sum = sum_blocks([(vec_all, d) for d in range(8)], "sum_devices").reshape(-1)
    (_, _, _, s_norm_g, s_b_pw1, s_w_dw, s_b_dw, s_ln_g, s_ln_b, s_b_pw2, s_kv_norm_g, s_kv_a_g, s_q_a_g,
     s_final_g, s_loss) = _unpack(vec_sum, vec_like)
    loss = jnp.sum(s_loss)
    dmod_all = vec_all[:, :n_mod_rows].reshape(8, 2 * N_MOD * D + 2 * D)
    dmod_sum = vec_sum[:2 * N_MOD * D + 2 * D]
    g_ada_b = dmod_sum[:2 * N_MOD * D].reshape(2, N_MOD * D)
    g_kv_ada_b = dmod_sum[2 * N_MOD * D:]
    g_ada_w = []
    for l in range(2):
        cols = lax.dynamic_slice_in_dim(dmod_all[:, l * N_MOD * D:(l + 1) * N_MOD * D], chip * n_ada, n_ada, axis=1)
        g_ada_w.append(mm(silu_all, cols, "tn", "ada_w_grad"))
    g_ada_w = jnp.stack(g_ada_w)
    kv_cols = lax.dynamic_slice_in_dim(dmod_all[:, 2 * N_MOD * D:], chip * n_kv, n_kv, axis=1)
    g_kv_ada_w = mm(silu_all, kv_cols, "tn", "kv_ada_w_grad")

    def shard(v, width):
        return lax.dynamic_slice_in_dim(v, chip * width, width, axis=v.ndim - 1)

    Dq = D // 4
    g_norm_g = shard(s_norm_g.reshape(2, 3, D), Dq)
    g_conv_b_pw1 = shard(s_b_pw1, 2 * D // 4)
    g_conv_w_dw = shard(s_w_dw, Dq)[None]
    g_conv_b_dw = shard(s_b_dw, Dq)
    g_conv_ln_g = shard(s_ln_g, Dq)
    g_conv_ln_b = shard(s_ln_b, Dq)
    g_conv_b_pw2 = shard(s_b_pw2, Dq)

    grads = [g_ada_w, g_ada_b, g_norm_g, g_ffn_w13, g_ffn_w2, g_conv_w_pw1, g_conv_b_pw1, g_conv_w_dw, g_conv_b_dw,
             g_conv_ln_g, g_conv_ln_b, g_conv_w_pw2, g_conv_b_pw2, g_kv_ada_w, g_kv_ada_b, s_kv_norm_g[0], g_w_kv_a,
             s_kv_a_g[0], g_w_kv_b, g_w_q_a, s_q_a_g, g_w_q_b, g_w_o, s_final_g[0]]
    weights = [ada_w, ada_b, norm_g, ffn_w13, ffn_w2, conv_w_pw1, conv_b_pw1, conv_w_dw, conv_b_dw, conv_ln_g,
               conv_ln_b, conv_w_pw2, conv_b_pw2, kv_ada_w, kv_ada_b, kv_norm_g, w_kv_a, kv_a_norm_g, w_kv_b, w_q_a,
               q_a_norm_g, w_q_b, w_o, final_norm_g]
    ms = [m_ada_w, m_ada_b, m_norm_g, m_ffn_w13, m_ffn_w2, m_conv_w_pw1, m_conv_b_pw1, m_conv_w_dw, m_conv_b_dw,
          m_conv_ln_g, m_conv_ln_b, m_conv_w_pw2, m_conv_b_pw2, m_kv_ada_w, m_kv_ada_b, m_kv_norm_g, m_w_kv_a,
          m_kv_a_norm_g, m_w_kv_b, m_w_q_a, m_q_a_norm_g, m_w_q_b, m_w_o, m_final_norm_g]
    vs = [v_ada_w, v_ada_b, v_norm_g, v_ffn_w13, v_ffn_w2, v_conv_w_pw1, v_conv_b_pw1, v_conv_w_dw, v_conv_b_dw,
          v_conv_ln_g, v_conv_ln_b, v_conv_w_pw2, v_conv_b_pw2, v_kv_ada_w, v_kv_ada_b, v_kv_norm_g, v_w_kv_a,
          v_kv_a_norm_g, v_w_kv_b, v_w_q_a, v_q_a_norm_g, v_w_q_b, v_w_o, v_final_norm_g]
    grads = [g.reshape(w.shape) for g, w in zip(grads, weights)]
    deltas, new_m, new_v = [], [], []
    for w, g, m, v in zip(weights, grads, ms, vs):
        d, nm, nv = adamw(w, g, m, v)
        deltas.append(d)
        new_m.append(nm)
        new_v.append(nv)
    return (loss, grad_x, *grads, *deltas, *new_m, *new_v)
```

```python
import jax
import jax.numpy as jnp
from jax import lax
from jax.experimental import pallas as pl
from jax.experimental.pallas import tpu as pltpu

F32 = jnp.float32
BF16 = jnp.bfloat16
MESH = pl.DeviceIdType.MESH

N_HEADS = 16
QK_NOPE = 64
QK_ROPE = 32
V_HEAD = 64
KV_LORA = 256
CONV_WIDTH = 31
ROPE_THETA = 10000.0
EPS = 1e-6
N_MOD = 9
HEAD_PAD = 128
ATTN_TILE = 512
CONV_HALO = 32

ADAM_LR = 0.001
ADAM_B1 = 0.9
ADAM_B2 = 0.999
ADAM_EPS = 1e-08
ADAM_WD = 0.01
ADAM_STEP = 10

VMEM_LIMIT_BYTES = 56 * 2 ** 20
ROW_TILE_BUDGET = 10 * 2 ** 20
MM_VMEM_BUDGET = 40 * 2 ** 20
LANES = 128
F32_TILE = 8 * LANES
NEG = float(jnp.finfo(jnp.float32).min)
LOG2_E = 1.4426950408889634


def _tile(n, prefs):
    for t in prefs:
        if n % t == 0:
            return t
    return n


def _params(sem):
    return pltpu.CompilerParams(dimension_semantics=sem, vmem_limit_bytes=VMEM_LIMIT_BYTES)


def _mm_tiles(M, N, K, mode, a_bytes, b_bytes, o_bytes):
    if mode == "tn":
        tk_opts = [t for t in (2048, 1024, 512, 256, 128) if K % t == 0] or [K]
        tm_opts = ([M] if M <= 2816 else []) + [t for t in (1024, 512, 256, 128) if M % t == 0 and t < M]
    else:
        tk_opts = [K]
        tm_opts = [t for t in (1024, 512, 256, 128) if M % t == 0] or [M]
    tn_opts = [t for t in (1408, 1024, 512, 384, 256, 128) if N % t == 0] or [N]

    def need(tm, tn, tk):
        blocks = 2 * (tm * tk * a_bytes + tk * tn * b_bytes + tm * tn * o_bytes)
        return blocks + (tm * tn * 4 if mode == "tn" else 0)

    tk_floor = next((t for t in tk_opts if t <= 512), tk_opts[-1])
    for tm in tm_opts:
        for tn in tn_opts:
            if need(tm, tn, tk_floor) <= MM_VMEM_BUDGET:
                return tm, tn, next(tk for tk in tk_opts if need(tm, tn, tk) <= MM_VMEM_BUDGET)
    return tm_opts[-1], tn_opts[-1], tk_opts[-1]


def mm(a, b, mode, name, out_dtype=F32, bias=None):
    if mode == "nn":
        (M, K), (K2, N) = a.shape, b.shape
        dims = (((1,), (0,)), ((), ()))
    elif mode == "nt":
        (M, K), (N, K2) = a.shape, b.shape
        dims = (((1,), (1,)), ((), ()))
    else:
        (K, M), (K2, N) = a.shape, b.shape
        dims = (((0,), (0,)), ((), ()))
    assert K == K2, (a.shape, b.shape, mode)
    tm, tn, tk = _mm_tiles(M, N, K, mode, a.dtype.itemsize, b.dtype.itemsize, jnp.dtype(out_dtype).itemsize)
    nk = K // tk
    if mode == "tn":
        a_spec = pl.BlockSpec((tk, tm), lambda i, j, k: (k, i))
        b_spec = pl.BlockSpec((tk, tn), lambda i, j, k: (k, j))
    elif mode == "nn":
        a_spec = pl.BlockSpec((tm, tk), lambda i, j, k: (i, k))
        b_spec = pl.BlockSpec((tk, tn), lambda i, j, k: (k, j))
    else:
        a_spec = pl.BlockSpec((tm, tk), lambda i, j, k: (i, k))
        b_spec = pl.BlockSpec((tn, tk), lambda i, j, k: (j, k))
    in_specs = [a_spec, b_spec]
    operands = [a, b]
    if bias is not None:
        in_specs.append(pl.BlockSpec((1, tn), lambda i, j, k: (0, j)))
        operands.append(bias)
    has_bias = bias is not None

    def body(*refs):
        a_ref, b_ref = refs[0], refs[1]
        bias_ref = refs[2] if has_bias else None
        o_ref = refs[3] if has_bias else refs[2]
        prod = lax.dot_general(a_ref[...].astype(BF16), b_ref[...].astype(BF16), dims,
                               preferred_element_type=F32)
        if nk == 1:
            if has_bias:
                prod = prod + bias_ref[...]
            o_ref[...] = prod.astype(o_ref.dtype)
        else:
            acc_ref = refs[-1]
            k = pl.program_id(2)

            @pl.when(k == 0)
            def _():
                acc_ref[...] = jnp.zeros_like(acc_ref)

            acc_ref[...] += prod

            @pl.when(k == nk - 1)
            def _():
                out = acc_ref[...]
                if has_bias:
                    out = out + bias_ref[...]
                o_ref[...] = out.astype(o_ref.dtype)

    return pl.pallas_call(
        body, name=name,
        grid=(M // tm, N // tn, nk),
        in_specs=in_specs,
        out_specs=pl.BlockSpec((tm, tn), lambda i, j, k: (i, j)),
        out_shape=jax.ShapeDtypeStruct((M, N), out_dtype),
        scratch_shapes=[pltpu.VMEM((tm, tn), F32)] if nk > 1 else [],
        compiler_params=_params(("parallel", "parallel", "arbitrary")),
    )(*operands)


def mm_fused(a, b, mode, name, tn, epi, epi_outs, pro=None, pro_rows=(), pro_vecs=(), pro_out=False, n_pro_sums=0,
             epi_rows=(), epi_vecs=(), b_blocks=None, n_cols=None):
    M, K = a.shape
    if b_blocks is not None:
        n_b, N = len(b_blocks), n_cols
    else:
        n_b = b.shape[0] if b.ndim == 3 else 1
        N = b.shape[-1] if mode == "nn" else b.shape[0]
    dims = (((1,), (0,)), ((), ())) if mode == "nn" else (((1,), (1,)), ((), ()))
    nj = N // tn
    epi_outs = [o if len(o) == 3 else (*o, None) for o in epi_outs]
    row_bytes = 2 * (K * a.dtype.itemsize + sum(K * r.dtype.itemsize for r in pro_rows) + (2 * K if pro_out else 0)
                     + sum(w * r.dtype.itemsize * (r.shape[0] if r.ndim == 3 else 1) for r, w in epi_rows)
                     + sum(w * jnp.dtype(dt).itemsize * (L or 1) for w, dt, L in epi_outs)
                     ) + (2 * K if pro is not None else 0)
    fixed = 2 * n_b * K * tn * b.dtype.itemsize
    tm = next((t for t in (1024, 512, 256, 128) if M % t == 0 and t * row_bytes + fixed <= MM_VMEM_BUDGET), M)
    row = lambda i, j: (i, 0)
    tile = lambda i, j: (i, j)
    stack = lambda i, j: (0, i, j)
    in_specs = [pl.BlockSpec((tm, K), row)] + [pl.BlockSpec((tm, K), row) for _ in pro_rows]
    in_specs += [pl.BlockSpec(v.shape, lambda i, j: (0, 0)) for v in pro_vecs]
    if b_blocks is not None:
        in_specs += [pl.BlockSpec(shape, imap) for shape, imap in b_blocks]
    elif b.ndim == 3:
        in_specs += [pl.BlockSpec((None, K, tn), lambda i, j, h=h: (h, 0, j)) for h in range(n_b)]
    elif mode == "nn":
        in_specs += [pl.BlockSpec((K, tn), lambda i, j: (0, j))]
    else:
        in_specs += [pl.BlockSpec((tn, K), lambda i, j: (j, 0))]
    in_specs += [pl.BlockSpec((r.shape[0], tm, w), stack) if r.ndim == 3 else pl.BlockSpec((tm, w), tile)
                 for r, w in epi_rows]
    in_specs += [pl.BlockSpec((1, tn), lambda i, j: (0, j)) for _ in epi_vecs]
    out_specs, out_shape = [], []
    if pro_out:
        out_specs.append(pl.BlockSpec((tm, K), row))
        out_shape.append(jax.ShapeDtypeStruct((M, K), BF16))
    for _ in range(n_pro_sums):
        out_specs.append(pl.BlockSpec((1, K), lambda i, j: (0, 0)))
        out_shape.append(jax.ShapeDtypeStruct((1, K), F32))
    for w, dt, L in epi_outs:
        out_specs.append(pl.BlockSpec((tm, w), tile) if L is None else pl.BlockSpec((L, tm, w), stack))
        out_shape.append(jax.ShapeDtypeStruct((M, nj * w) if L is None else (L, M, nj * w), dt))
    n_pr, n_pv, n_er, n_ev = len(pro_rows), len(pro_vecs), len(epi_rows), len(epi_vecs)
    n_a = 1 + n_pr + n_pv
    n_in = n_a + n_b + n_er + n_ev
    n_po = 1 if pro_out else 0

    def body(*refs):
        i, j = pl.program_id(0), pl.program_id(1)
        a_ref = refs[0]
        outs = refs[n_in:]
        if pro is not None:
            lhs_ref = refs[-1]

            @pl.when(j == 0)
            def _():
                res = pro(*[r[...] for r in refs[:1 + n_pr + n_pv]])
                if not isinstance(res, (tuple, list)):
                    res = (res,)
                lhs_ref[...] = res[0]
                if pro_out:
                    outs[0][...] = res[0]
                for s_ref, val in zip(outs[n_po:n_po + n_pro_sums], res[1:]):
                    part = jnp.sum(val.astype(F32), axis=0, keepdims=True)

                    @pl.when(i == 0)
                    def _(s_ref=s_ref, part=part):
                        s_ref[...] = part

                    @pl.when(i != 0)
                    def _(s_ref=s_ref, part=part):
                        s_ref[...] += part

            lhs = lhs_ref[...]
        else:
            lhs = a_ref[...].astype(BF16)
        accs = [lax.dot_general(lhs, b_ref[...].astype(BF16), dims, preferred_element_type=F32)
                for b_ref in refs[n_a:n_a + n_b]]
        res = epi(*accs, *[r[...] for r in refs[n_a + n_b:n_in]])
        if not isinstance(res, (tuple, list)):
            res = (res,)
        for o_ref, val in zip(outs[n_po + n_pro_sums:], res):
            if isinstance(val, (tuple, list)):
                for h, part in enumerate(val):
                    o_ref[h] = part.astype(o_ref.dtype)
            else:
                o_ref[...] = val.astype(o_ref.dtype)

    return pl.pallas_call(
        body, name=name,
        grid=(M // tm, nj),
        in_specs=in_specs, out_specs=out_specs, out_shape=out_shape,
        scratch_shapes=[pltpu.VMEM((tm, K), BF16)] if pro is not None else [],
        compiler_params=_params(("arbitrary", "arbitrary")),
    )(a, *pro_rows, *pro_vecs, *([b] * n_b), *[r for r, _ in epi_rows], *epi_vecs)


def rowwise(fn, rows, vecs, outs, sums, name, tm=None):
    norm = [(r, r.shape[1], 0) if not isinstance(r, tuple) else r for r in rows]
    S = norm[0][0].shape[0]
    if tm is None:
        tm = _row_tile(S, sum(w * r.dtype.itemsize for r, w, _ in norm)
                       + sum(n * jnp.dtype(dt).itemsize for n, dt in outs))
    n_rows, n_vecs, n_outs, n_sums = len(norm), len(vecs), len(outs), len(sums)
    in_specs = [pl.BlockSpec((tm, w), lambda i, cb=cb: (i, cb)) for _, w, cb in norm]
    in_specs += [pl.BlockSpec(v.shape, lambda i: (0, 0)) for v in vecs]
    out_specs = [pl.BlockSpec((tm, n), lambda i: (i, 0)) for n, _ in outs]
    out_specs += [pl.BlockSpec((1, n), lambda i: (0, 0)) for n in sums]
    out_shape = [jax.ShapeDtypeStruct((S, n), dt) for n, dt in outs]
    out_shape += [jax.ShapeDtypeStruct((1, n), F32) for n in sums]

    def body(*refs):
        ins = [r[...] for r in refs[:n_rows + n_vecs]]
        res = fn(*ins)
        if not isinstance(res, (tuple, list)):
            res = (res,)
        out_refs = refs[n_rows + n_vecs:]
        for o_ref, val in zip(out_refs[:n_outs], res[:n_outs]):
            o_ref[...] = val.astype(o_ref.dtype)
        if n_sums:
            i = pl.program_id(0)
            for s_ref, val in zip(out_refs[n_outs:], res[n_outs:]):
                part = jnp.sum(val.astype(F32), axis=0, keepdims=True)

                @pl.when(i == 0)
                def _(s_ref=s_ref, part=part):
                    s_ref[...] = part

                @pl.when(i != 0)
                def _(s_ref=s_ref, part=part):
                    s_ref[...] += part

    res = pl.pallas_call(
        body, name=name,
        grid=(S // tm,),
        in_specs=in_specs, out_specs=out_specs, out_shape=out_shape,
        compiler_params=_params(("arbitrary",) if n_sums else ("parallel",)),
    )(*[r for r, _, _ in norm], *vecs)
    return res


def _sigmoid(x):
    return jax.nn.sigmoid(x)


def _rms(x):
    r = lax.rsqrt(jnp.mean(x * x, axis=-1, keepdims=True) + EPS)
    return x * r, r


def _rms_bwd(xhat, r, dxhat):
    return r * (dxhat - xhat * jnp.mean(dxhat * xhat, axis=-1, keepdims=True))


def norm_mod(h, g, sh, sc, name):
    def f(h, g, sh, sc):
        xhat, _ = _rms(h)
        return ((xhat * g) * (1 + sc) + sh).astype(BF16)
    return rowwise(f, [h], [g, sh, sc], [(h.shape[1], BF16)], [], name)[0]


def norm_mod_bwd(h, dhn, dh_out, g, sc, name):
    D = h.shape[1]

    def f(h, dhn, dres, g, sc):
        xhat, r = _rms(h)
        dxn = dhn * (1 + sc)
        return _rms_bwd(xhat, r, dxn * g) + dres, dhn, dhn * (xhat * g), dxn * xhat

    return rowwise(f, [h, dhn, dh_out], [g, sc], [(D, F32)], [D, D, D], name)


def residual(h, y, gate, coef, name, bias=None):
    D = h.shape[1]
    if bias is None:
        def f(h, y, gate):
            return h + (coef * gate) * y
        return rowwise(f, [h, y], [gate], [(D, F32)], [], name)[0], y

    def fb(h, y, gate, bias):
        yb = y + bias
        return h + (coef * gate) * yb, yb
    return rowwise(fb, [h, y], [gate, bias], [(D, F32), (D, F32)], [], name)


def residual_bwd(dh_out, y, gate, coef, name, with_bias_sum=False):
    D = y.shape[1]

    def f(dh, y, gate):
        dy = (coef * gate) * dh
        res = (dy.astype(BF16), coef * dh * y)
        return res + ((dy,) if with_bias_sum else ())
    return rowwise(f, [dh_out, y], [gate], [(D, BF16)], [D, D] if with_bias_sum else [D], name)


def ffn_w13_dx(dab, gw13, l, i):
    _, S, F = dab.shape
    D, C = gw13.shape[3:]
    tm = _tile(S, (1024, 512, 256, 128))
    nt = (((1,), (1,)), ((), ()))

    def body(a_ref, b_ref, o_ref, acc_ref):
        k = pl.program_id(1)
        prod = lax.dot_general(a_ref[...], b_ref[...], nt, preferred_element_type=F32)

        @pl.when(k == 0)
        def _():
            acc_ref[...] = prod

        @pl.when((k > 0) & (k < 3))
        def _():
            acc_ref[...] += prod

        @pl.when(k == 3)
        def _():
            o_ref[...] = acc_ref[...] + prod

    return pl.pallas_call(
        body, name="ffn_w13_dx",
        grid=(S // tm, 4),
        in_specs=[pl.BlockSpec((None, tm, C), lambda r, k: (k // 2, r, k % 2)),
                  pl.BlockSpec((None, None, None, D, C), lambda r, k: (k, l, i, 0, 0))],
        out_specs=pl.BlockSpec((tm, D), lambda r, k: (r, 0)),
        out_shape=jax.ShapeDtypeStruct((S, D), F32),
        scratch_shapes=[pltpu.VMEM((tm, D), F32)],
        compiler_params=_params(("parallel", "arbitrary")),
    )(dab, gw13)


def ffn_w13_grad(hn, dab):
    S, D = hn.shape
    F = dab.shape[2]
    C = F // 2
    tk = next(t for t in (2048, 1024, 512, 256, 128) if S % t == 0)
    tn_dims = (((0,), (0,)), ((), ()))
    nk = S // tk

    def body(a_ref, b_ref, o_ref, acc_ref):
        k = pl.program_id(1)

        @pl.when(k == 0)
        def _():
            acc_ref[...] = jnp.zeros_like(acc_ref)

        acc_ref[...] += lax.dot_general(a_ref[...], b_ref[...], tn_dims, preferred_element_type=F32)

        @pl.when(k == nk - 1)
        def _():
            o_ref[...] = acc_ref[...]

    return pl.pallas_call(
        body, name="ffn_w13_dw",
        grid=(4, nk),
        in_specs=[pl.BlockSpec((tk, D), lambda j, k: (k, 0)),
                  pl.BlockSpec((None, tk, C), lambda j, k: (j // 2, k, j % 2))],
        out_specs=pl.BlockSpec((None, D, C), lambda j, k: (j, 0, 0)),
        out_shape=jax.ShapeDtypeStruct((4, D, C), F32),
        scratch_shapes=[pltpu.VMEM((D, C), F32)],
        compiler_params=_params(("parallel", "arbitrary")),
    )(hn, dab)


def ffn_fwd(h, g, sh, sc, gate, gw13, l, i, w2):
    F, D = w2.shape
    C = F // 2

    def norm(h, g, sh, sc):
        xhat, _ = _rms(h)
        return ((xhat * g) * (1 + sc) + sh).astype(BF16)

    def act(a, b):
        sig = _sigmoid(a)
        sa = a * sig
        return (b * (sig + sa * (1 - sig)), sa), sa * b
    blocks = [((None, None, None, D, C), lambda r, j, half=half: (2 * half + j, l, i, 0, 0)) for half in range(2)]
    hn, dt_dab, t = mm_fused(h, gw13, "nn", "ffn_w13", C, act, [(C, BF16, 2), (C, BF16)],
                             pro=norm, pro_vecs=[g, sh, sc], pro_out=True, b_blocks=blocks, n_cols=F)

    def res(acc, h, gate):
        return h + (0.5 * gate) * acc, acc
    h_out, y = mm_fused(t, w2, "nn", "ffn_w2", D, res, [(D, F32), (D, F32)], epi_rows=[(h, D)], epi_vecs=[gate])
    return h_out, (h, hn, dt_dab, t, y)


def ffn_bwd(dh_out, saved, g, sc, gate, gw13, l, i, w2):
    h, hn, dt_dab, t, y = saved
    F, D = w2.shape
    C = F // 2

    def scale(dh, y, gate):
        return ((0.5 * gate) * dh).astype(BF16), 0.5 * dh * y

    def act_bwd(dt, f):
        return ((dt * f[0].astype(F32), dt * f[1].astype(F32)),)
    dy, d_gate, dab = mm_fused(dh_out, w2, "nt", "ffn_w2_dx", C, act_bwd, [(C, BF16, 2)],
                               pro=scale, pro_rows=[y], pro_vecs=[gate], pro_out=True, n_pro_sums=1,
                               epi_rows=[(dt_dab, C)])
    dw2 = mm(t, dy, "tn", "ffn_w2_dw")
    dw13 = ffn_w13_grad(hn, dab)
    dhn = ffn_w13_dx(dab, gw13, l, i)
    dh_in, d_sh, d_sc, d_g = norm_mod_bwd(h, dhn, dh_out, g, sc, "norm_mod_bwd")
    return dh_in, (d_sh, d_sc, d_gate, d_g), dw13, dw2


def _shifted(xbuf, n):
    return [xbuf] + [pltpu.roll(xbuf, n - b, 0) for b in range(1, 8)]


def conv_fwd(u, w_dw, b_dw, ln_g, ln_b):
    S, D = u.shape
    tm = _tile(S, (256, 128))
    rc = 32
    first_tap = CONV_HALO - (CONV_WIDTH - 1)
    w = jnp.concatenate([w_dw, jnp.zeros((CONV_HALO - CONV_WIDTH, D), F32)], axis=0)

    def body(cur_ref, prev_ref, w_ref, b_ref, g_ref, beta_ref, z_ref, s_ref):
        i = pl.program_id(0)
        prev = jnp.where(i == 0, jnp.zeros((CONV_HALO, D), F32), prev_ref[...])
        xs = _shifted(jnp.concatenate([prev, cur_ref[...]], axis=0), tm + CONV_HALO)
        for c0 in range(0, tm, rc):
            acc = jnp.zeros((rc, D), F32)
            for k in range(CONV_WIDTH):
                off = first_tap + k
                a8, b = off // 8 * 8, off % 8
                acc = acc + w_ref[k:k + 1, :] * xs[b][c0 + a8:c0 + a8 + rc, :]
            z_ref[c0:c0 + rc, :] = acc + b_ref[...]
        z = z_ref[...]
        mu = jnp.mean(z, axis=-1, keepdims=True)
        zc = z - mu
        r = lax.rsqrt(jnp.mean(zc * zc, axis=-1, keepdims=True) + EPS)
        un = zc * r * g_ref[...] + beta_ref[...]
        s_ref[...] = (un * _sigmoid(un)).astype(BF16)

    nb = tm // CONV_HALO
    vec = pl.BlockSpec((1, D), lambda i: (0, 0))
    return pl.pallas_call(
        body, name="conv_fwd",
        grid=(S // tm,),
        in_specs=[pl.BlockSpec((tm, D), lambda i: (i, 0)),
                  pl.BlockSpec((CONV_HALO, D), lambda i: (jnp.maximum(i * nb - 1, 0), 0)),
                  pl.BlockSpec((CONV_HALO, D), lambda i: (0, 0)), vec, vec, vec],
        out_specs=[pl.BlockSpec((tm, D), lambda i: (i, 0)), pl.BlockSpec((tm, D), lambda i: (i, 0))],
        out_shape=[jax.ShapeDtypeStruct((S, D), F32), jax.ShapeDtypeStruct((S, D), BF16)],
        compiler_params=_params(("parallel",)),
    )(u, u, w, b_dw, ln_g, ln_b)


def conv_bwd(dz, u, w_dw):
    S, D = u.shape
    tm = _tile(S, (256, 128))
    rc = 32
    first_tap = CONV_HALO - (CONV_WIDTH - 1)
    w = jnp.concatenate([w_dw, jnp.zeros((CONV_HALO - CONV_WIDTH, D), F32)], axis=0)
    n_tiles = S // tm
    nb = tm // CONV_HALO

    def body(dz_ref, dzn_ref, u_ref, up_ref, w_ref, du_ref, dw_ref):
        i = pl.program_id(0)
        nxt = jnp.where(i == n_tiles - 1, jnp.zeros((CONV_HALO, D), F32), dzn_ref[...])
        dzs = _shifted(jnp.concatenate([dz_ref[...], nxt], axis=0), tm + CONV_HALO)
        for c0 in range(0, tm, rc):
            acc = jnp.zeros((rc, D), F32)
            for m in range(CONV_WIDTH):
                a8, b = m // 8 * 8, m % 8
                acc = acc + w_ref[CONV_WIDTH - 1 - m:CONV_WIDTH - m, :] * dzs[b][c0 + a8:c0 + a8 + rc, :]
            du_ref[c0:c0 + rc, :] = acc
        prev = jnp.where(i == 0, jnp.zeros((CONV_HALO, D), F32), up_ref[...])
        us = _shifted(jnp.concatenate([prev, u_ref[...]], axis=0), tm + CONV_HALO)
        dz = dz_ref[...]

        @pl.when(i == 0)
        def _():
            dw_ref[...] = jnp.zeros_like(dw_ref)

        for k in range(CONV_WIDTH):
            off = first_tap + k
            a8, b = off // 8 * 8, off % 8
            dw_ref[k:k + 1, :] += jnp.sum(dz * us[b][a8:a8 + tm, :], axis=0, keepdims=True)

    last_blk = S // CONV_HALO - 1
    du, dw = pl.pallas_call(
        body, name="conv_bwd",
        grid=(n_tiles,),
        in_specs=[pl.BlockSpec((tm, D), lambda i: (i, 0)),
                  pl.BlockSpec((CONV_HALO, D), lambda i: (jnp.minimum((i + 1) * nb, last_blk), 0)),
                  pl.BlockSpec((tm, D), lambda i: (i, 0)),
                  pl.BlockSpec((CONV_HALO, D), lambda i: (jnp.maximum(i * nb - 1, 0), 0)),
                  pl.BlockSpec((CONV_HALO, D), lambda i: (0, 0))],
        out_specs=[pl.BlockSpec((tm, D), lambda i: (i, 0)), pl.BlockSpec((CONV_HALO, D), lambda i: (0, 0))],
        out_shape=[jax.ShapeDtypeStruct((S, D), F32), jax.ShapeDtypeStruct((CONV_HALO, D), F32)],
        compiler_params=_params(("arbitrary",)),
    )(dz, dz, u, u, w)
    return du, dw[:CONV_WIDTH]


def conv_module_fwd(h, g, sh, sc, gate, p):
    D = h.shape[1]
    hn = norm_mod(h, g, sh, sc, "conv_norm_mod")
    pre = mm(hn, p["w_pw1"], "nn", "conv_pw1")
    ba, bg = p["b_pw1"][:, :D], p["b_pw1"][:, D:]

    def glu(a, gt, ba, bg):
        return (a + ba) * _sigmoid(gt + bg)
    u = rowwise(glu, [(pre, D, 0), (pre, D, 1)], [ba, bg], [(D, F32)], [], "conv_glu")[0]
    z, s = conv_fwd(u, p["w_dw"], p["b_dw"], p["ln_g"], p["ln_b"])
    yraw = mm(s, p["w_pw2"], "nn", "conv_pw2")
    h_out, y = residual(h, yraw, gate, 1.0, "conv_residual", bias=p["b_pw2"])
    return h_out, (h, hn, pre, u, z, s, y)


def conv_module_bwd(dh_out, saved, g, sc, gate, p):
    h, hn, pre, u, z, s, y = saved
    D = h.shape[1]
    dy, d_gate, d_b_pw2 = residual_bwd(dh_out, y, gate, 1.0, "conv_residual_bwd", with_bias_sum=True)
    d_w_pw2 = mm(s, dy, "tn", "conv_pw2_dw")
    ds = mm(dy, p["w_pw2"], "nt", "conv_pw2_dx")

    def ln_bwd(z, ds, g, beta):
        mu = jnp.mean(z, axis=-1, keepdims=True)
        zc = z - mu
        r = lax.rsqrt(jnp.mean(zc * zc, axis=-1, keepdims=True) + EPS)
        xhat = zc * r
        un = xhat * g + beta
        sig = _sigmoid(un)
        d_un = ds * (sig * (1 + un * (1 - sig)))
        dxhat = d_un * g
        dz = r * (dxhat - jnp.mean(dxhat, axis=-1, keepdims=True)
                  - xhat * jnp.mean(dxhat * xhat, axis=-1, keepdims=True))
        return dz, d_un * xhat, d_un, dz
    dz, d_ln_g, d_ln_b, d_b_dw = rowwise(ln_bwd, [z, ds], [p["ln_g"], p["ln_b"]], [(D, F32)], [D, D, D],
                                         "conv_ln_bwd")
    du, d_w_dw = conv_bwd(dz, u, p["w_dw"])
    ba, bg = p["b_pw1"][:, :D], p["b_pw1"][:, D:]

    def glu_bwd(a, gt, du, ba, bg):
        sg = _sigmoid(gt + bg)
        da = du * sg
        dg = du * (a + ba) * (sg * (1 - sg))
        dpre = jnp.concatenate([da, dg], axis=1)
        return dpre.astype(BF16), dpre
    dpre, d_b_pw1 = rowwise(glu_bwd, [(pre, D, 0), (pre, D, 1), du], [ba, bg], [(2 * D, BF16)], [2 * D],
                            "conv_glu_bwd")
    d_w_pw1 = mm(hn, dpre, "tn", "conv_pw1_dw")
    dhn = mm(dpre, p["w_pw1"], "nt", "conv_pw1_dx")
    dh_in, d_sh, d_sc, d_g = norm_mod_bwd(h, dhn, dh_out, g, sc, "norm_mod_bwd")
    grads = dict(w_pw1=d_w_pw1, b_pw1=d_b_pw1, w_dw=d_w_dw, b_dw=d_b_dw, ln_g=d_ln_g, ln_b=d_ln_b,
                 w_pw2=d_w_pw2, b_pw2=d_b_pw2)
    return dh_in, (d_sh, d_sc, d_gate, d_g), grads


def _rope(x, c, s1, s2):
    n = x.shape[1]
    return x * c + pltpu.roll(x, n - QK_ROPE // 2, 1) * s1 + pltpu.roll(x, QK_ROPE // 2, 1) * s2


def _rope_t(dy, c, s1, s2):
    n = dy.shape[1]
    return dy * c + pltpu.roll(dy * s1, QK_ROPE // 2, 1) + pltpu.roll(dy * s2, n - QK_ROPE // 2, 1)


def rope_tables(positions):
    inv_freq = ROPE_THETA ** (-jnp.arange(0, QK_ROPE, 2, dtype=F32) / QK_ROPE)
    ang = positions.astype(F32)[:, None] * inv_freq
    cos, sin = jnp.cos(ang), jnp.sin(ang)
    S = positions.shape[0]
    one = jnp.ones((S, QK_NOPE), F32)
    z16 = jnp.zeros((S, QK_ROPE // 2), F32)
    zn = jnp.zeros((S, QK_NOPE), F32)
    zt = jnp.zeros((S, HEAD_PAD - QK_NOPE - QK_ROPE), F32)
    c = jnp.concatenate([one, cos, cos, zt], axis=1)
    s1 = jnp.concatenate([zn, -sin, z16, zt], axis=1)
    s2 = jnp.concatenate([zn, z16, sin, zt], axis=1)
    return c, s1, s2


def attn_fwd(qr, kv, kpe, n_heads):
    S = qr.shape[0]
    H = n_heads
    tk = _tile(S, (ATTN_TILE,))
    nk = S // tk
    w = 2 if nk % 2 == 0 else 1
    tq = w * tk
    c2 = (QK_NOPE + QK_ROPE) ** -0.5 * LOG2_E
    nt = (((1,), (1,)), ((), ()))

    assert V_HEAD < HEAD_PAD
    ones_row = HEAD_PAD - 1

    def body(q_ref, k_ref, v_ref, kpe_ref, o_ref, lse_ref, kf_ref, vt_ref, m_ref, acc_ref):
        qi = pl.program_id(1)
        feature = lax.broadcasted_iota(jnp.int32, (HEAD_PAD, tk), 0)

        @pl.when(qi == 0)
        def _():
            kf_ref[...] = k_ref[...] + kpe_ref[...]
            for c in range(nk):
                vt = jnp.transpose(v_ref[c * tk:(c + 1) * tk, :].astype(F32))
                vt_ref[c] = jnp.where(feature == ones_row, 1.0, vt).astype(BF16)

        q = q_ref[...]
        m_ref[...] = jnp.full((1, tq), -jnp.inf, F32)
        acc_ref[...] = jnp.zeros((HEAD_PAD, tq), F32)

        def tile(j, first_visible):
            k = kf_ref[pl.ds(pl.multiple_of(j * tk, tk), tk), :]
            t = lax.dot_general(k, q, nt, preferred_element_type=F32) * c2
            if first_visible is not None:
                krow = lax.broadcasted_iota(jnp.int32, (tk, tq), 0)
                qcol = lax.broadcasted_iota(jnp.int32, (tk, tq), 1)
                t = jnp.where(krow + first_visible <= qcol, t, NEG)
            m_old = m_ref[...]
            m_new = jnp.maximum(m_old, jnp.max(t, axis=0, keepdims=True))
            alpha = jnp.exp2(m_old - m_new)
            p = jnp.exp2(t - m_new)
            acc_ref[...] = alpha * acc_ref[...] + jnp.dot(vt_ref[j], p.astype(BF16), preferred_element_type=F32)
            m_ref[...] = m_new

        def unmasked(j, carry):
            tile(j, None)
            return carry

        lax.fori_loop(0, w * qi, unmasked, 0)
        for u in range(w):
            tile(w * qi + u, u * tk)
        acc = acc_ref[...]
        l = acc_ref[ones_row:ones_row + 1, :]
        out_feature = lax.broadcasted_iota(jnp.int32, (HEAD_PAD, tq), 0)
        o_ref[...] = jnp.transpose(jnp.where(out_feature == ones_row, 0.0, acc / l))
        lse = m_ref[...] + jnp.log(l) * LOG2_E
        for u in range(w):
            lse_ref[u] = lse[:, u * tk:(u + 1) * tk]

    return pl.pallas_call(
        body, name="attn_fwd",
        grid=(H, S // tq),
        in_specs=[pl.BlockSpec((tq, HEAD_PAD), lambda h, i: (i, h)),
                  pl.BlockSpec((S, HEAD_PAD), lambda h, i: (0, h)),
                  pl.BlockSpec((S, HEAD_PAD), lambda h, i: (0, H + h)),
                  pl.BlockSpec((S, HEAD_PAD), lambda h, i: (0, 0))],
        out_specs=[pl.BlockSpec((tq, HEAD_PAD), lambda h, i: (i, h)),
                   pl.BlockSpec((None, w, 1, tk), lambda h, i: (h, i, 0, 0))],
        out_shape=[jax.ShapeDtypeStruct((S, H * HEAD_PAD), F32), jax.ShapeDtypeStruct((H, nk, 1, tk), F32)],
        scratch_shapes=[pltpu.VMEM((S, HEAD_PAD), BF16), pltpu.VMEM((nk, HEAD_PAD, tk), BF16),
                        pltpu.VMEM((1, tq), F32), pltpu.VMEM((HEAD_PAD, tq), F32)],
        compiler_params=_params(("parallel", "arbitrary")),
    )(qr, kv, kv, kpe)


def attn_delta(o, do, n_heads):
    S = o.shape[0]
    H = n_heads
    tq = _tile(S, (ATTN_TILE,))
    nq = S // tq

    def body(o_ref, do_ref, d_ref):
        for c in range(nq):
            rows = slice(c * tq, (c + 1) * tq)
            prod = o_ref[rows, :] * do_ref[rows, :].astype(F32)
            d_ref[c] = jnp.sum(jnp.transpose(prod), axis=0, keepdims=True)

    return pl.pallas_call(
        body, name="attn_delta",
        grid=(H,),
        in_specs=[pl.BlockSpec((S, HEAD_PAD), lambda h: (0, h)), pl.BlockSpec((S, HEAD_PAD), lambda h: (0, h))],
        out_specs=pl.BlockSpec((None, nq, 1, tq), lambda h: (h, 0, 0, 0)),
        out_shape=jax.ShapeDtypeStruct((H, nq, 1, tq), F32),
        compiler_params=_params(("parallel",)),
    )(o, do)


def attn_bwd(qr, kv, kpe, do, lse2, delta, n_heads):
    S = qr.shape[0]
    H = n_heads
    tk = _tile(S, (ATTN_TILE,))
    nk = S // tk
    w = 2 if nk % 2 == 0 else 1
    tq = w * tk
    nq = S // tq
    scale = (QK_NOPE + QK_ROPE) ** -0.5
    c2 = scale * LOG2_E
    nt = (((1,), (1,)), ((), ()))
    lse2 = lse2.reshape(H, nq, 1, tq)
    delta4 = delta.reshape(H, nq, 1, tq)

    def body(k_ref, v_ref, kpe_ref, q_ref, do_ref, lse_ref, dl_ref, dq_ref, dk_ref, dv_ref, dka_ref, dva_ref,
             dqt_ref):
        kj = pl.program_id(1)
        k = k_ref[...] + kpe_ref[...]
        kt = jnp.transpose(k.astype(F32)).astype(BF16)
        v = v_ref[...]

        @pl.when(kj == 0)
        def _():
            dqt_ref[...] = jnp.zeros_like(dqt_ref)

        dka_ref[...] = jnp.zeros_like(dka_ref)
        dva_ref[...] = jnp.zeros_like(dva_ref)

        def tile(i, masked):
            start = pl.multiple_of(i * tq, tq)
            q = q_ref[pl.ds(start, tq), :]
            do = do_ref[pl.ds(start, tq), :]
            t = lax.dot_general(k, q, nt, preferred_element_type=F32) * c2
            if masked:
                krow = lax.broadcasted_iota(jnp.int32, (tk, tq), 0)
                qcol = lax.broadcasted_iota(jnp.int32, (tk, tq), 1)
                t = jnp.where(krow + (kj % w) * tk <= qcol, t, NEG)
            pt = jnp.exp2(t - lse_ref[i])
            dva_ref[...] += jnp.dot(pt.astype(BF16), do, preferred_element_type=F32)
            dpt = lax.dot_general(v, do, nt, preferred_element_type=F32)
            dst = (pt * (dpt - dl_ref[i]) * scale).astype(BF16)
            dka_ref[...] += jnp.dot(dst, q, preferred_element_type=F32)
            dqt_ref[i] += jnp.dot(kt, dst, preferred_element_type=F32)

        tile(kj // w, True)

        def unmasked(i, carry):
            tile(i, False)
            return carry

        lax.fori_loop(kj // w + 1, nq, unmasked, 0)
        dk_ref[...] = dka_ref[...]
        dv_ref[...] = dva_ref[...]

        @pl.when(kj == nk - 1)
        def _():
            for c in range(nq):
                dq_ref[c * tq:(c + 1) * tq, :] = jnp.transpose(dqt_ref[c])

    blk = pl.BlockSpec((tk, HEAD_PAD), lambda h, j: (j, h))
    whole = pl.BlockSpec((S, HEAD_PAD), lambda h, j: (0, h))
    stat = pl.BlockSpec((None, nq, 1, tq), lambda h, j: (h, 0, 0, 0))
    shp = jax.ShapeDtypeStruct((S, H * HEAD_PAD), F32)
    return pl.pallas_call(
        body, name="attn_bwd",
        grid=(H, nk),
        in_specs=[blk, pl.BlockSpec((tk, HEAD_PAD), lambda h, j: (j, H + h)),
                  pl.BlockSpec((tk, HEAD_PAD), lambda h, j: (j, 0)), whole, whole, stat, stat],
        out_specs=[whole, blk, blk],
        out_shape=[shp, shp, shp],
        scratch_shapes=[pltpu.VMEM((tk, HEAD_PAD), F32), pltpu.VMEM((tk, HEAD_PAD), F32),
                        pltpu.VMEM((nq, HEAD_PAD, tq), F32)],
        compiler_params=_params(("parallel", "arbitrary")),
    )(kv, kv, kpe, qr, do, lse2, delta4)


def _pad_heads(w, width):
    R = w.shape[0]
    w3 = w.reshape(R, -1, width)
    return jnp.pad(w3, ((0, 0), (0, 0), (0, HEAD_PAD - width))).reshape(R, -1)


def _unpad_heads(w, width):
    R = w.shape[0]
    return w.reshape(R, -1, HEAD_PAD)[:, :, :width].reshape(R, -1)


def mla_pad_weights(p):
    H = N_HEADS
    w_q_b = _pad_heads(p["w_q_b"], QK_NOPE + QK_ROPE)
    kvb = p["w_kv_b"].reshape(KV_LORA, H, QK_NOPE + V_HEAD)
    wk = _pad_heads(kvb[:, :, :QK_NOPE].reshape(KV_LORA, -1), QK_NOPE)
    wv = _pad_heads(kvb[:, :, QK_NOPE:].reshape(KV_LORA, -1), V_HEAD)
    D = p["w_kv_a"].shape[0]
    a = p["w_kv_a"]
    w_kv_a = jnp.concatenate([a[:, :KV_LORA], jnp.zeros((D, QK_NOPE), a.dtype), a[:, KV_LORA:],
                              jnp.zeros((D, HEAD_PAD - QK_NOPE - QK_ROPE), a.dtype)], axis=1)
    wo = p["w_o"].reshape(H, V_HEAD, -1)
    w_o = jnp.pad(wo, ((0, 0), (0, HEAD_PAD - V_HEAD), (0, 0))).reshape(H * HEAD_PAD, -1)
    return dict(w_q_a=p["w_q_a"], w_q_b=w_q_b, w_kv_b=jnp.concatenate([wk, wv], axis=1), w_kv_a=w_kv_a, w_o=w_o)


def mla_kv_fwd(h, g, sh, sc, kv_a_norm_g, pw, tabs):
    hkv = norm_mod(h, g, sh, sc, "kv_norm_mod")
    ckvp = mm(hkv, pw["w_kv_a"], "nn", "kv_a")

    def f(ckv, kpe, c, s1, s2, g):
        xhat, _ = _rms(ckv)
        return (xhat * g).astype(BF16), _rope(kpe, c, s1, s2).astype(BF16)
    ckv_n, kpe_r = rowwise(f, [(ckvp, KV_LORA, 0), (ckvp, HEAD_PAD, KV_LORA // HEAD_PAD), *tabs], [kv_a_norm_g],
                           [(KV_LORA, BF16), (HEAD_PAD, BF16)], [], "kv_a_norm_rope")
    kv = mm(ckv_n, pw["w_kv_b"], "nn", "kv_b", out_dtype=BF16)
    return kv, kpe_r, (h, hkv, ckvp, ckv_n)


def mla_kv_bwd(dh_stream, dk, dv, saved, g, sc, kv_a_norm_g, pw, tabs):
    h, hkv, ckvp, ckv_n = saved
    H = N_HEADS
    lane = jnp.arange(HEAD_PAD)
    pe_mask = ((lane >= QK_NOPE) & (lane < QK_NOPE + QK_ROPE)).astype(F32)[None, :]

    def f(dk, dv, c, s1, s2, mask):
        tot = dk[:, :HEAD_PAD]
        for hh in range(1, H):
            tot = tot + dk[:, hh * HEAD_PAD:(hh + 1) * HEAD_PAD]
        dkpe = _rope_t(tot * mask, c, s1, s2) * mask
        return jnp.concatenate([dk, dv], axis=1).astype(BF16), dkpe
    dkv, dkpe = rowwise(f, [dk, dv, *tabs], [pe_mask], [(2 * H * HEAD_PAD, BF16), (HEAD_PAD, F32)], [],
                        "kv_split_bwd")
    d_w_kv_b = mm(ckv_n, dkv, "tn", "kv_b_dw")
    dckv_n = mm(dkv, pw["w_kv_b"], "nt", "kv_b_dx")

    def f2(ckv, dn, dkpe, g):
        xhat, r = _rms(ckv)
        dx = _rms_bwd(xhat, r, dn * g)
        return jnp.concatenate([dx, dkpe], axis=1).astype(BF16), dn * xhat
    dckvp, d_kv_a_g = rowwise(f2, [(ckvp, KV_LORA, 0), dckv_n, dkpe], [kv_a_norm_g],
                              [(KV_LORA + HEAD_PAD, BF16)], [KV_LORA], "kv_a_norm_bwd")
    d_w_kv_a = mm(hkv, dckvp, "tn", "kv_a_dw")
    dhkv = mm(dckvp, pw["w_kv_a"], "nt", "kv_a_dx")
    dh, d_sh, d_sc, d_g = norm_mod_bwd(h, dhkv, dh_stream, g, sc, "norm_mod_bwd")
    return dh, (d_sh, d_sc, d_g), d_kv_a_g, d_w_kv_a, d_w_kv_b


def mla_fwd(h, g, sh, sc, gate, q_a_norm_g, pw, kv, kpe_r, tabs):
    H = N_HEADS
    hn = norm_mod(h, g, sh, sc, "mla_norm_mod")
    qa = mm(hn, pw["w_q_a"], "nn", "q_a")

    def f(qa, g):
        xhat, _ = _rms(qa)
        return (xhat * g).astype(BF16)
    qa_n = rowwise(f, [qa], [q_a_norm_g], [(qa.shape[1], BF16)], [], "q_a_norm")[0]
    qp = mm(qa_n, pw["w_q_b"], "nn", "q_b")

    def frope(q, c, s1, s2):
        return jnp.concatenate([_rope(q[:, hh * HEAD_PAD:(hh + 1) * HEAD_PAD], c, s1, s2) for hh in range(H)],
                               axis=1).astype(BF16)
    qr = rowwise(frope, [qp, *tabs], [], [(H * HEAD_PAD, BF16)], [], "q_rope")[0]
    o, lse = attn_fwd(qr, kv, kpe_r, H)
    y = mm(o, pw["w_o"], "nn", "w_o")
    h_out, _ = residual(h, y, gate, 1.0, "mla_residual")
    return h_out, (h, hn, qa, qa_n, qr, o, lse, y)


def mla_bwd(dh_out, saved, g, sc, gate, q_a_norm_g, pw, kv, kpe_r, tabs):
    h, hn, qa, qa_n, qr, o, lse, y = saved
    H = N_HEADS
    dy, d_gate = residual_bwd(dh_out, y, gate, 1.0, "mla_residual_bwd")
    d_w_o = mm(o, dy, "tn", "w_o_dw")
    do = mm(dy, pw["w_o"], "nt", "w_o_dx", out_dtype=BF16)
    delta = attn_delta(o, do, H)
    dqr, dk, dv = attn_bwd(qr, kv, kpe_r, do, lse, delta, H)

    def frope_t(dq, c, s1, s2):
        return jnp.concatenate([_rope_t(dq[:, hh * HEAD_PAD:(hh + 1) * HEAD_PAD], c, s1, s2) for hh in range(H)],
                               axis=1).astype(BF16)
    dqp = rowwise(frope_t, [dqr, *tabs], [], [(H * HEAD_PAD, BF16)], [], "q_rope_bwd")[0]
    d_w_q_b = mm(qa_n, dqp, "tn", "q_b_dw")
    dqa_n = mm(dqp, pw["w_q_b"], "nt", "q_b_dx")

    def f(qa, dn, g):
        xhat, r = _rms(qa)
        return _rms_bwd(xhat, r, dn * g).astype(BF16), dn * xhat
    dqa, d_q_a_g = rowwise(f, [qa, dqa_n], [q_a_norm_g], [(qa.shape[1], BF16)], [qa.shape[1]], "q_a_norm_bwd")
    d_w_q_a = mm(hn, dqa, "tn", "q_a_dw")
    dhn = mm(dqa, pw["w_q_a"], "nt", "q_a_dx")
    dh_in, d_sh, d_sc, d_g = norm_mod_bwd(h, dhn, dh_out, g, sc, "norm_mod_bwd")
    grads = dict(w_q_a=d_w_q_a, q_a_norm_g=d_q_a_g, w_q_b=d_w_q_b, w_o=d_w_o)
    return dh_in, (d_sh, d_sc, d_gate, d_g), grads, dk, dv


def loss_head(h, target, g):
    D = h.shape[1]

    def f(h, t, g):
        xhat, r = _rms(h)
        err = xhat * g - t
        dy = err * (1.0 / D)
        dh = _rms_bwd(xhat, r, dy * g)
        return dh, (0.5 / D) * err * err, dy * xhat
    return rowwise(f, [h, target], [g], [(D, F32)], [D, D], "loss_head")


def _place():
    x, y, c = lax.axis_index("x"), lax.axis_index("y"), lax.axis_index("c")
    chips = [(1 - x, y), (x, 1 - y), (1 - x, 1 - y)]
    return x, y, c, chips


HBM_SPEC = pl.BlockSpec(memory_space=pltpu.HBM)


def all_gather8(v):
    m, n = v.shape

    def body(x_ref, out_ref, send_sems, recv_sems, local_sem):
        x, y, c, chips = _place()
        me, sibling = (x, y, c), (x, y, 1 - c)

        def rows(px, py, pc):
            return out_ref.at[4 * px + 2 * py + pc]

        def copy(k, block, to, src=None):
            return pltpu.make_async_remote_copy(
                src_ref=rows(*block) if src is None else src, dst_ref=rows(*block),
                send_sem=send_sems.at[k], recv_sem=recv_sems.at[k], device_id=to, device_id_type=MESH)

        mine = pltpu.make_async_copy(x_ref, rows(*me), local_sem)
        mine.start()
        first = [copy(0, me, sibling, src=x_ref)]
        first += [copy(1 + j, me, (*chip, c), src=x_ref) for j, chip in enumerate(chips)]
        for cp in first:
            cp.start()
        passed = [copy(4 + j, (*chip, c), sibling) for j, chip in enumerate(chips)]
        for j, chip in enumerate(chips):
            copy(1 + j, (*chip, c), me).wait_recv()
            passed[j].start()
        copy(0, sibling, me).wait_recv()
        for j, chip in enumerate(chips):
            copy(4 + j, (*chip, 1 - c), me).wait_recv()
        for cp in first + passed:
            cp.wait_send()
        mine.wait()

    return pl.pallas_call(
        body, name="all_gather8",
        out_shape=jax.ShapeDtypeStruct((8, m, n), v.dtype),
        in_specs=[pl.BlockSpec(memory_space=pltpu.VMEM)],
        out_specs=pl.BlockSpec(memory_space=pltpu.VMEM),
        scratch_shapes=[pltpu.SemaphoreType.DMA((7,)), pltpu.SemaphoreType.DMA((7,)), pltpu.SemaphoreType.DMA],
        compiler_params=pltpu.CompilerParams(vmem_limit_bytes=VMEM_LIMIT_BYTES),
    )(v)


def gather_weights(bufs):
    n = len(bufs)

    def body(*refs):
        ins, outs = refs[:n], refs[n:2 * n]
        send_sems, recv_sems = refs[2 * n:]
        x, y, c, chips = _place()
        across_x, across_y, across_both = chips
        sibling = (x, y, 1 - c)
        me = 2 * x + y
        via_in = (x + (1 - c) * (1 - 2 * x), y + c * (1 - 2 * y))
        via_out = (x + c * (1 - 2 * x), y + (1 - c) * (1 - 2 * y))

        def idx(chip):
            return 2 * chip[0] + chip[1]

        def copy(w, k, src, dst, to):
            return pltpu.make_async_remote_copy(src_ref=src, dst_ref=dst, send_sem=send_sems.at[6 * w + k],
                                                recv_sem=recv_sems.at[6 * w + k], device_id=to, device_id_type=MESH)

        def landed(w, k, chip):
            blk = outs[w].at[idx(chip), c]
            copy(w, k, blk, blk, (*chip, c)).wait_recv()
            return blk

        sends = [copy(w, j, ins[w].at[me, c], outs[w].at[me, c], (*chip, c))
                 for w in range(n) for j, chip in enumerate((across_x, across_y))]
        for cp in sends:
            cp.start()
        for w in range(n):
            blk = landed(w, c, via_in)
            sends += [copy(w, 2, blk, blk, (*via_out, c)), copy(w, 3 + c, blk, blk, sibling)]
            sends[-2].start()
            sends[-1].start()
        for w in range(n):
            blk = landed(w, 1 - c, via_out)
            sends.append(copy(w, 4 - c, blk, blk, sibling))
            sends[-1].start()
        for w in range(n):
            blk = landed(w, 2, across_both)
            sends.append(copy(w, 5, blk, blk, sibling))
            sends[-1].start()
        for w in range(n):
            for j, chip in enumerate(chips):
                other = outs[w].at[idx(chip), 1 - c]
                copy(w, 3 + j, other, other, sibling).wait_recv()
        for cp in sends:
            cp.wait_send()

    return pl.pallas_call(
        body, name="gather_weights",
        out_shape=[jax.ShapeDtypeStruct(b.shape, b.dtype) for b in bufs],
        in_specs=[HBM_SPEC] * n, out_specs=[HBM_SPEC] * n,
        input_output_aliases={w: w for w in range(n)},
        scratch_shapes=[pltpu.SemaphoreType.DMA((6 * n,)), pltpu.SemaphoreType.DMA((6 * n,))],
    )(*bufs)


def exchange_halves(gs):
    n = len(gs)

    def body(*refs):
        ins, theirs = refs[:n], refs[n:2 * n]
        send_sems, recv_sems = refs[2 * n:]
        x, y, c, _ = _place()
        sends = [pltpu.make_async_remote_copy(src_ref=ins[w].at[:, 1 - c], dst_ref=theirs[w],
                                              send_sem=send_sems.at[w], recv_sem=recv_sems.at[w],
                                              device_id=(x, y, 1 - c), device_id_type=MESH) for w in range(n)]
        for cp in sends:
            cp.start()
        for cp in sends:
            cp.wait()

    return pl.pallas_call(
        body, name="exchange_halves",
        out_shape=[jax.ShapeDtypeStruct((4,) + g.shape[2:], g.dtype) for g in gs],
        in_specs=[HBM_SPEC] * n, out_specs=[HBM_SPEC] * n,
        scratch_shapes=[pltpu.SemaphoreType.DMA((n,)), pltpu.SemaphoreType.DMA((n,))],
    )(*gs)


def scatter_blocks(ps):
    n = len(ps)

    def body(*refs):
        ins, outs = refs[:n], refs[n:2 * n]
        send_sems, recv_sems = refs[2 * n:]
        x, y, c, chips = _place()
        sends = [pltpu.make_async_remote_copy(src_ref=ins[w].at[2 * chip[0] + chip[1]], dst_ref=outs[w].at[j],
                                              send_sem=send_sems.at[3 * w + j], recv_sem=recv_sems.at[3 * w + j],
                                              device_id=(*chip, c), device_id_type=MESH)
                 for w in range(n) for j, chip in enumerate(chips)]
        for cp in sends:
            cp.start()
        for cp in sends:
            cp.wait()

    return pl.pallas_call(
        body, name="scatter_blocks",
        out_shape=[jax.ShapeDtypeStruct((3,) + p.shape[1:], p.dtype) for p in ps],
        in_specs=[HBM_SPEC] * n, out_specs=[HBM_SPEC] * n,
        scratch_shapes=[pltpu.SemaphoreType.DMA((3 * n,)), pltpu.SemaphoreType.DMA((3 * n,))],
    )(*ps)


def join_halves(qs):
    n = len(qs)

    def body(*refs):
        ins, outs = refs[:n], refs[n:2 * n]
        send_sems, recv_sems = refs[2 * n:]
        x, y, c, _ = _place()
        sends = [pltpu.make_async_remote_copy(src_ref=ins[w].at[c], dst_ref=outs[w].at[c], send_sem=send_sems.at[w],
                                              recv_sem=recv_sems.at[w], device_id=(x, y, 1 - c), device_id_type=MESH)
                 for w in range(n)]
        for cp in sends:
            cp.start()
        for w in range(n):
            other = outs[w].at[1 - c]
            pltpu.make_async_remote_copy(src_ref=other, dst_ref=other, send_sem=send_sems.at[w],
                                         recv_sem=recv_sems.at[w], device_id=(x, y, 1 - c),
                                         device_id_type=MESH).wait_recv()
        for cp in sends:
            cp.wait_send()

    return pl.pallas_call(
        body, name="join_halves",
        out_shape=[jax.ShapeDtypeStruct(q.shape, q.dtype) for q in qs],
        in_specs=[HBM_SPEC] * n, out_specs=[HBM_SPEC] * n,
        input_output_aliases={w: w for w in range(n)},
        scratch_shapes=[pltpu.SemaphoreType.DMA((n,)), pltpu.SemaphoreType.DMA((n,))],
    )(*qs)


def _row_tile(R, row_bytes):
    tm = R
    for t in (512, 256, 128, 64, 32, 16, 8):
        if R % t == 0:
            tm = t
            if t * row_bytes <= ROW_TILE_BUDGET:
                break
    return tm


def sum_siblings(g, theirs, place):
    _, _, R, C = g.shape
    tm = _row_tile(R, 3 * C * 4)

    def body(place_ref, a_ref, b_ref, o_ref):
        o_ref[...] = (a_ref[...] + b_ref[...]).astype(BF16)

    return pl.pallas_call(
        body, name="sum_siblings",
        grid_spec=pltpu.PrefetchScalarGridSpec(
            num_scalar_prefetch=1, grid=(4, R // tm),
            in_specs=[pl.BlockSpec((None, None, tm, C), lambda j, i, s: (j, s[1], i, 0)),
                      pl.BlockSpec((None, tm, C), lambda j, i, s: (j, i, 0))],
            out_specs=pl.BlockSpec((None, tm, C), lambda j, i, s: (j, i, 0))),
        out_shape=jax.ShapeDtypeStruct((4, R, C), BF16),
        compiler_params=_params(("parallel", "parallel")),
    )(place, g, theirs)


def sum_chips(p, landed, place):
    _, R, C = p.shape
    tm = _row_tile(R, 5 * C * 4)

    def body(place_ref, p_ref, l0_ref, l1_ref, l2_ref, o_ref):
        o_ref[...] = ((p_ref[...].astype(F32) + l0_ref[...].astype(F32)) + l1_ref[...].astype(F32)
                      ) + l2_ref[...].astype(F32)

    return pl.pallas_call(
        body, name="sum_chips",
        grid_spec=pltpu.PrefetchScalarGridSpec(
            num_scalar_prefetch=1, grid=(R // tm,),
            in_specs=[pl.BlockSpec((None, tm, C), lambda i, s: (s[0], i, 0))]
            + [pl.BlockSpec((None, tm, C), lambda i, s, j=j: (j, i, 0)) for j in range(3)],
            out_specs=pl.BlockSpec((None, tm, C), lambda i, s: (s[1], i, 0))),
        out_shape=jax.ShapeDtypeStruct((2, R, C), F32),
        compiler_params=_params(("parallel",)),
    )(place, p, landed, landed, landed)


def sum_blocks(items, name):
    R, C = items[0][0].shape[1:]
    tm = _row_tile(R, C * 4 * (len(items) + 1))
    n = len(items)

    def body(*refs):
        acc = refs[0][...].astype(F32)
        for r in refs[1:n]:
            acc = acc + r[...].astype(F32)
        refs[n][...] = acc

    return pl.pallas_call(
        body, name=name,
        grid=(R // tm,),
        in_specs=[pl.BlockSpec((None, tm, C), lambda i, j=j: (j, i, 0)) for _, j in items],
        out_specs=pl.BlockSpec((tm, C), lambda i: (i, 0)),
        out_shape=jax.ShapeDtypeStruct((R, C), F32),
        compiler_params=_params(("parallel",)),
    )(*[a for a, _ in items])


def reduce_scatter_grads(gs, place):
    theirs = exchange_halves(gs)
    ps = [sum_siblings(g, t, place) for g, t in zip(gs, theirs)]
    landed = scatter_blocks(ps)
    qs = [sum_chips(p, l, place) for p, l in zip(ps, landed)]
    joined = join_halves(qs)
    return [j.reshape(2 * j.shape[1], j.shape[2]) for j in joined]


def adamw(w, g, m, v):
    shape = w.shape
    C = shape[-1]
    R = w.size // C

    def f(w, g, m, v):
        m = ADAM_B1 * m + (1.0 - ADAM_B1) * g
        v = ADAM_B2 * v + (1.0 - ADAM_B2) * (g * g)
        m_hat = m / (1.0 - ADAM_B1 ** ADAM_STEP)
        v_hat = v / (1.0 - ADAM_B2 ** ADAM_STEP)
        delta = -ADAM_LR * (m_hat / (jnp.sqrt(v_hat) + ADAM_EPS) + ADAM_WD * w)
        return delta, m, v

    d, nm, nv = rowwise(f, [a.reshape(R, C) for a in (w, g, m, v)], [], [(C, F32)] * 3, [], "adamw")
    return d.reshape(shape), nm.reshape(shape), nv.reshape(shape)


def _cast_into_slot(w, place):
    C = w.shape[-1]
    w2 = w.reshape(-1, C)
    R = w2.shape[0]
    tm = _row_tile(R, 6 * C)

    def body(place_ref, w_ref, o_ref):
        o_ref[...] = w_ref[...].astype(BF16)

    out = pl.pallas_call(
        body, name="cast_bf16",
        grid_spec=pltpu.PrefetchScalarGridSpec(
            num_scalar_prefetch=1, grid=(R // tm,),
            in_specs=[pl.BlockSpec((tm, C), lambda i, s: (i, 0))],
            out_specs=pl.BlockSpec((None, tm, C), lambda i, s: (s[0], i, 0))),
        out_shape=jax.ShapeDtypeStruct((4, R, C), BF16),
        compiler_params=_params(("parallel",)),
    )(place, w2)
    return out.reshape(4, 2, R // 2, C)


def _pack(vs):
    flat = jnp.concatenate([v.reshape(-1) for v in vs])
    n = flat.shape[0]
    total = -(-n // F32_TILE) * F32_TILE
    return jnp.pad(flat, (0, total - n)).reshape(total // LANES, LANES)


def _unpack(flat, like):
    out, o = [], 0
    for shp in like:
        sz = 1
        for d in shp:
            sz *= d
        out.append(flat[o:o + sz].reshape(shp))
        o += sz
    return out


def _cols_to_blocks(g, n_chips=4):
    R, N = g.shape
    C = N // n_chips
    return g.reshape(R, n_chips, C).transpose(1, 0, 2).reshape(n_chips, 2, R // 2, C)


def _rows_to_blocks(g, n_chips=4):
    R, C = g.shape
    return g.reshape(n_chips, 2, R // n_chips // 2, C)


def kernel(x, c, positions, ada_w, ada_b, norm_g, ffn_w13, ffn_w2, conv_w_pw1, conv_b_pw1, conv_w_dw, conv_b_dw, conv_ln_g, conv_ln_b, conv_w_pw2, conv_b_pw2, kv_ada_w, kv_ada_b, kv_norm_g, w_kv_a, kv_a_norm_g, w_kv_b, w_q_a, q_a_norm_g, w_q_b, w_o, final_norm_g, loss_target, m_ada_w, m_ada_b, m_norm_g, m_ffn_w13, m_ffn_w2, m_conv_w_pw1, m_conv_b_pw1, m_conv_w_dw, m_conv_b_dw, m_conv_ln_g, m_conv_ln_b, m_conv_w_pw2, m_conv_b_pw2, m_kv_ada_w, m_kv_ada_b, m_kv_norm_g, m_w_kv_a, m_kv_a_norm_g, m_w_kv_b, m_w_q_a, m_q_a_norm_g, m_w_q_b, m_w_o, m_final_norm_g, v_ada_w, v_ada_b, v_norm_g, v_ffn_w13, v_ffn_w2, v_conv_w_pw1, v_conv_b_pw1, v_conv_w_dw, v_conv_b_dw, v_conv_ln_g, v_conv_ln_b, v_conv_w_pw2, v_conv_b_pw2, v_kv_ada_w, v_kv_ada_b, v_kv_norm_g, v_w_kv_a, v_kv_a_norm_g, v_w_kv_b, v_w_q_a, v_q_a_norm_g, v_w_q_b, v_w_o, v_final_norm_g):
    S, D = x.shape[1], x.shape[2]
    H = N_HEADS
    F = ffn_w2.shape[2] * 4
    xi, yi, ci = lax.axis_index("x"), lax.axis_index("y"), lax.axis_index("c")
    chip = 2 * xi + yi
    dev = 2 * chip + ci
    place = jnp.stack([chip, ci]).astype(jnp.int32)
    h0 = x[0]
    target = loss_target[0]

    silu_c = rowwise(lambda a: a * _sigmoid(a), [c], [], [(D, F32)], [], "silu_c")[0]
    silu_all = all_gather8(silu_c.reshape(8, D // 8)).reshape(8, D)
    n_ada = ada_w.shape[2]
    n_kv = kv_ada_w.shape[1]
    ada_b_mine = lax.dynamic_slice_in_dim(ada_b, chip * n_ada, n_ada, axis=1)
    kv_b_mine = lax.dynamic_slice_in_dim(kv_ada_b, chip * n_kv, n_kv, axis=0)[None, :]
    mods = [mm(silu_all, ada_w[l], "nn", "ada_rows", bias=ada_b_mine[l:l + 1]) for l in range(2)]
    mods.append(mm(silu_all, kv_ada_w, "nn", "kv_ada_rows", bias=kv_b_mine))
    n_mod_cols = 2 * n_ada + n_kv
    mod_pack = jnp.concatenate(mods, axis=1).reshape(-1, LANES)
    mod_all = all_gather8(mod_pack).reshape(8, 8, n_mod_cols)[0::2]
    mod_mine = lax.dynamic_index_in_dim(mod_all, dev, axis=1, keepdims=False)
    mod = [mod_mine[:, l * n_ada:(l + 1) * n_ada].reshape(N_MOD, D) for l in range(2)]
    kv_mod = mod_mine[:, 2 * n_ada:].reshape(2, D)
    kv_shift, kv_scale = kv_mod[0:1], kv_mod[1:2]

    def mrow(l, k):
        return mod[l][k:k + 1]

    big = dict(ffn_w13=ffn_w13, ffn_w2=ffn_w2, conv_w_pw1=conv_w_pw1, conv_w_pw2=conv_w_pw2, w_kv_a=w_kv_a,
               w_kv_b=w_kv_b, w_q_a=w_q_a, w_q_b=w_q_b, w_o=w_o)
    names = list(big)
    gathered = gather_weights([_cast_into_slot(big[k], place) for k in names])
    gw = dict(zip(names, gathered))
    small_like = [norm_g.shape, conv_b_pw1.shape, conv_w_dw.shape, conv_b_dw.shape, conv_ln_g.shape,
                  conv_ln_b.shape, conv_b_pw2.shape]
    small_pack = _pack([norm_g, conv_b_pw1, conv_w_dw, conv_b_dw, conv_ln_g, conv_ln_b, conv_b_pw2])
    small_all = all_gather8(small_pack)[0::2].reshape(4, -1)
    per_chip = [_unpack(small_all[j], small_like) for j in range(4)]
    smalls = [jnp.concatenate([per_chip[j][k] for j in range(4)], axis=-1) for k in range(len(small_like))]
    norm_g_f, b_pw1_f, w_dw_f, b_dw_f, ln_g_f, ln_b_f, b_pw2_f = smalls

    gw13 = gw["ffn_w13"].reshape(4, 2, 2, D, F // 2)
    w2 = gw["ffn_w2"].reshape(4, 2, 2, F // 4, D).transpose(1, 2, 0, 3, 4).reshape(2, 2, F, D)
    conv_p = dict(
        w_pw1=gw["conv_w_pw1"].reshape(4, D, 2 * D // 4).transpose(1, 0, 2).reshape(D, 2 * D),
        b_pw1=b_pw1_f, w_dw=w_dw_f[0], b_dw=b_dw_f, ln_g=ln_g_f, ln_b=ln_b_f,
        w_pw2=gw["conv_w_pw2"].reshape(D, D), b_pw2=b_pw2_f)
    q_lora = w_q_a.shape[2]
    mla_p = dict(
        w_kv_a=gw["w_kv_a"].reshape(D, KV_LORA + QK_ROPE),
        w_kv_b=gw["w_kv_b"].reshape(4, KV_LORA, -1).transpose(1, 0, 2).reshape(KV_LORA, -1),
        w_q_a=gw["w_q_a"].reshape(D, q_lora),
        w_q_b=gw["w_q_b"].reshape(4, q_lora, -1).transpose(1, 0, 2).reshape(q_lora, -1),
        w_o=gw["w_o"].reshape(H * V_HEAD, D))
    pw = mla_pad_weights(mla_p)
    tabs = rope_tables(positions[0])

    def ng(l, k):
        return norm_g_f[l, k][None, :]

    h = h0
    h, s_f1_0 = ffn_fwd(h, ng(0, 0), mrow(0, 0), mrow(0, 1), mrow(0, 2), gw13, 0, 0, w2[0, 0])
    h, s_conv = conv_module_fwd(h, ng(0, 1), mrow(0, 3), mrow(0, 4), mrow(0, 5), conv_p)
    h, s_f2_0 = ffn_fwd(h, ng(0, 2), mrow(0, 6), mrow(0, 7), mrow(0, 8), gw13, 0, 1, w2[0, 1])
    kv_norm = kv_norm_g[None, :]
    kv_a_g = kv_a_norm_g[None, :]
    kv, kpe_r, s_kv = mla_kv_fwd(h, kv_norm, kv_shift, kv_scale, kv_a_g, pw, tabs)
    h, s_f1_1 = ffn_fwd(h, ng(1, 0), mrow(1, 0), mrow(1, 1), mrow(1, 2), gw13, 1, 0, w2[1, 0])
    h, s_mla = mla_fwd(h, ng(1, 1), mrow(1, 3), mrow(1, 4), mrow(1, 5), q_a_norm_g, pw, kv, kpe_r, tabs)
    h, s_f2_1 = ffn_fwd(h, ng(1, 2), mrow(1, 6), mrow(1, 7), mrow(1, 8), gw13, 1, 1, w2[1, 1])
    dh, loss_cols, d_final_g = loss_head(h, target, final_norm_g[None, :])

    dh, v_f2_1, dw13_11, dw2_11 = ffn_bwd(dh, s_f2_1, ng(1, 2), mrow(1, 7), mrow(1, 8), gw13, 1, 1, w2[1, 1])
    dh, v_mla, g_mla, dk, dv = mla_bwd(dh, s_mla, ng(1, 1), mrow(1, 4), mrow(1, 5), q_a_norm_g, pw, kv, kpe_r, tabs)
    dh, v_f1_1, dw13_10, dw2_10 = ffn_bwd(dh, s_f1_1, ng(1, 0), mrow(1, 1), mrow(1, 2), gw13, 1, 0, w2[1, 0])
    dh, v_kv, d_kv_a_g, d_w_kv_a, d_w_kv_b = mla_kv_bwd(dh, dk, dv, s_kv, kv_norm, kv_scale, kv_a_g, pw, tabs)
    dh, v_f2_0, dw13_01, dw2_01 = ffn_bwd(dh, s_f2_0, ng(0, 2), mrow(0, 7), mrow(0, 8), gw13, 0, 1, w2[0, 1])
    dh, v_conv, g_conv = conv_module_bwd(dh, s_conv, ng(0, 1), mrow(0, 4), mrow(0, 5), conv_p)
    dh, v_f1_0, dw13_00, dw2_00 = ffn_bwd(dh, s_f1_0, ng(0, 0), mrow(0, 1), mrow(0, 2), gw13, 0, 0, w2[0, 0])
    grad_x = dh[None]

    d_w_kv_a_u = jnp.concatenate([d_w_kv_a[:, :KV_LORA], d_w_kv_a[:, KV_LORA + QK_NOPE:KV_LORA + QK_NOPE + QK_ROPE]],
                                 axis=1)
    hk = H * HEAD_PAD
    dkb = jnp.concatenate([d_w_kv_b[:, :hk].reshape(KV_LORA, H, HEAD_PAD)[:, :, :QK_NOPE],
                           d_w_kv_b[:, hk:].reshape(KV_LORA, H, HEAD_PAD)[:, :, :V_HEAD]], axis=2).reshape(KV_LORA, -1)
    d_w_q_b_u = _unpad_heads(g_mla["w_q_b"], QK_NOPE + QK_ROPE)
    d_w_o_u = g_mla["w_o"].reshape(H, HEAD_PAD, D)[:, :V_HEAD].reshape(H * V_HEAD, D)
    full = [dw.reshape(4, 2, D // 2, F // 2) for dw in (dw13_00, dw13_01, dw13_10, dw13_11)] + [
            _rows_to_blocks(dw2_00), _rows_to_blocks(dw2_01), _rows_to_blocks(dw2_10), _rows_to_blocks(dw2_11),
            _cols_to_blocks(g_conv["w_pw1"]), _rows_to_blocks(g_conv["w_pw2"]), _rows_to_blocks(d_w_kv_a_u),
            _cols_to_blocks(dkb), _rows_to_blocks(g_mla["w_q_a"]), _cols_to_blocks(d_w_q_b_u),
            _rows_to_blocks(d_w_o_u)]
    red = reduce_scatter_grads(full, place)
    g_ffn_w13 = jnp.stack(red[0:4]).reshape(ffn_w13.shape)
    g_ffn_w2 = jnp.stack(red[4:8]).reshape(ffn_w2.shape)
    g_conv_w_pw1 = red[8].reshape(conv_w_pw1.shape)
    g_conv_w_pw2 = red[9].reshape(conv_w_pw2.shape)
    g_w_kv_a = red[10].reshape(w_kv_a.shape)
    g_w_kv_b = red[11].reshape(w_kv_b.shape)
    g_w_q_a = red[12].reshape(w_q_a.shape)
    g_w_q_b = red[13].reshape(w_q_b.shape)
    g_w_o = red[14].reshape(w_o.shape)

    def dmod(v1, vm, v2):
        return jnp.concatenate([v1[0], v1[1], v1[2], vm[0], vm[1], vm[2], v2[0], v2[1], v2[2]], axis=1)
    d_mod0 = dmod(v_f1_0, v_conv, v_f2_0)
    d_mod1 = dmod(v_f1_1, v_mla, v_f2_1)
    d_kv_mod = jnp.concatenate([v_kv[0], v_kv[1]], axis=1)
    d_norm_g = jnp.concatenate([v_f1_0[3], v_conv[3], v_f2_0[3], v_f1_1[3], v_mla[3], v_f2_1[3]], axis=0)
    vec_list = [d_mod0, d_mod1, d_kv_mod, d_norm_g, g_conv["b_pw1"], g_conv["w_dw"], g_conv["b_dw"], g_conv["ln_g"],
                g_conv["ln_b"], g_conv["b_pw2"], v_kv[2], d_kv_a_g, g_mla["q_a_norm_g"], d_final_g, loss_cols]
    vec_like = [v.shape for v in vec_list]
    vec_pack = _pack(vec_list)
    n_mod_rows = (2 * N_MOD * D + 2 * D) // LANES
    vec_all = all_gather8(vec_pack)
    vec_sum = sum_blocks([(vec_all, d) for d in range(8)], "sum_devices").reshape(-1)
    (_, _, _, s_norm_g, s_b_pw1, s_w_dw, s_b_dw, s_ln_g, s_ln_b, s_b_pw2, s_kv_norm_g, s_kv_a_g, s_q_a_g,
     s_final_g, s_loss) = _unpack(vec_sum, vec_like)
    loss = jnp.sum(s_loss)
    dmod_all = vec_all[:, :n_mod_rows].reshape(8, 2 * N_MOD * D + 2 * D)
    dmod_sum = vec_sum[:2 * N_MOD * D + 2 * D]
    g_ada_b = dmod_sum[:2 * N_MOD * D].reshape(2, N_MOD * D)
    g_kv_ada_b = dmod_sum[2 * N_MOD * D:]
    g_ada_w = []
    for l in range(2):
        cols = lax.dynamic_slice_in_dim(dmod_all[:, l * N_MOD * D:(l + 1) * N_MOD * D], chip * n_ada, n_ada, axis=1)
        g_ada_w.append(mm(silu_all, cols, "tn", "ada_w_grad"))
    g_ada_w = jnp.stack(g_ada_w)
    kv_cols = lax.dynamic_slice_in_dim(dmod_all[:, 2 * N_MOD * D:], chip * n_kv, n_kv, axis=1)
    g_kv_ada_w = mm(silu_all, kv_cols, "tn", "kv_ada_w_grad")

    def shard(v, width):
        return lax.dynamic_slice_in_dim(v, chip * width, width, axis=v.ndim - 1)

    Dq = D // 4
    g_norm_g = shard(s_norm_g.reshape(2, 3, D), Dq)
    g_conv_b_pw1 = shard(s_b_pw1, 2 * D // 4)
    g_conv_w_dw = shard(s_w_dw, Dq)[None]
    g_conv_b_dw = shard(s_b_dw, Dq)
    g_conv_ln_g = shard(s_ln_g, Dq)
    g_conv_ln_b = shard(s_ln_b, Dq)
    g_conv_b_pw2 = shard(s_b_pw2, Dq)

    grads = [g_ada_w, g_ada_b, g_norm_g, g_ffn_w13, g_ffn_w2, g_conv_w_pw1, g_conv_b_pw1, g_conv_w_dw, g_conv_b_dw,
             g_conv_ln_g, g_conv_ln_b, g_conv_w_pw2, g_conv_b_pw2, g_kv_ada_w, g_kv_ada_b, s_kv_norm_g[0], g_w_kv_a,
             s_kv_a_g[0], g_w_kv_b, g_w_q_a, s_q_a_g, g_w_q_b, g_w_o, s_final_g[0]]
    weights = [ada_w, ada_b, norm_g, ffn_w13, ffn_w2, conv_w_pw1, conv_b_pw1, conv_w_dw, conv_b_dw, conv_ln_g,
               conv_ln_b, conv_w_pw2, conv_b_pw2, kv_ada_w, kv_ada_b, kv_norm_g, w_kv_a, kv_a_norm_g, w_kv_b, w_q_a,
               q_a_norm_g, w_q_b, w_o, final_norm_g]
    ms = [m_ada_w, m_ada_b, m_norm_g, m_ffn_w13, m_ffn_w2, m_conv_w_pw1, m_conv_b_pw1, m_conv_w_dw, m_conv_b_dw,
          m_conv_ln_g, m_conv_ln_b, m_conv_w_pw2, m_conv_b_pw2, m_kv_ada_w, m_kv_ada_b, m_kv_norm_g, m_w_kv_a,
          m_kv_a_norm_g, m_w_kv_b, m_w_q_a, m_q_a_norm_g, m_w_q_b, m_w_o, m_final_norm_g]
    vs = [v_ada_w, v_ada_b, v_norm_g, v_ffn_w13, v_ffn_w2, v_conv_w_pw1, v_conv_b_pw1, v_conv_w_dw, v_conv_b_dw,
          v_conv_ln_g, v_conv_ln_b, v_conv_w_pw2, v_conv_b_pw2, v_kv_ada_w, v_kv_ada_b, v_kv_norm_g, v_w_kv_a,
          v_kv_a_norm_g, v_w_kv_b, v_w_q_a, v_q_a_norm_g, v_w_q_b, v_w_o, v_final_norm_g]
    grads = [g.reshape(w.shape) for g, w in zip(grads, weights)]
    deltas, new_m, new_v = [], [], []
    for w, g, m, v in zip(weights, grads, ms, vs):
        d, nm, nv = adamw(w, g, m, v)
        deltas.append(d)
        new_m.append(nm)
        new_v.append(nv)
    return (loss, grad_x, *grads, *deltas, *new_m, *new_v)
```

```python
import jax
import jax.numpy as jnp
from jax import lax
from jax.experimental import pallas as pl
from jax.experimental.pallas import tpu as pltpu

F32 = jnp.float32
BF16 = jnp.bfloat16
MESH = pl.DeviceIdType.MESH

N_HEADS = 16
QK_NOPE = 64
QK_ROPE = 32
V_HEAD = 64
KV_LORA = 256
CONV_WIDTH = 31
ROPE_THETA = 10000.0
EPS = 1e-6
N_MOD = 9
HEAD_PAD = 128
ATTN_TILE = 512
CONV_HALO = 32

ADAM_LR = 0.001
ADAM_B1 = 0.9
ADAM_B2 = 0.999
ADAM_EPS = 1e-08
ADAM_WD = 0.01
ADAM_STEP = 10

VMEM_LIMIT_BYTES = 56 * 2 ** 20
ROW_TILE_BUDGET = 10 * 2 ** 20
MM_VMEM_BUDGET = 40 * 2 ** 20
LANES = 128
F32_TILE = 8 * LANES
NEG = float(jnp.finfo(jnp.float32).min)
LOG2_E = 1.4426950408889634


def _tile(n, prefs):
    for t in prefs:
        if n % t == 0:
            return t
    return n


def _params(sem):
    return pltpu.CompilerParams(dimension_semantics=sem, vmem_limit_bytes=VMEM_LIMIT_BYTES)


def _mm_tiles(M, N, K, mode, a_bytes, b_bytes, o_bytes):
    if mode == "tn":
        tk_opts = [t for t in (2048, 1024, 512, 256, 128) if K % t == 0] or [K]
        tm_opts = ([M] if M <= 2816 else []) + [t for t in (1024, 512, 256, 128) if M % t == 0 and t < M]
    else:
        tk_opts = [K]
        tm_opts = [t for t in (1024, 512, 256, 128) if M % t == 0] or [M]
    tn_opts = [t for t in (1408, 1024, 512, 384, 256, 128) if N % t == 0] or [N]

    def need(tm, tn, tk):
        blocks = 2 * (tm * tk * a_bytes + tk * tn * b_bytes + tm * tn * o_bytes)
        return blocks + (tm * tn * 4 if mode == "tn" else 0)

    tk_floor = next((t for t in tk_opts if t <= 512), tk_opts[-1])
    for tm in tm_opts:
        for tn in tn_opts:
            if need(tm, tn, tk_floor) <= MM_VMEM_BUDGET:
                return tm, tn, next(tk for tk in tk_opts if need(tm, tn, tk) <= MM_VMEM_BUDGET)
    return tm_opts[-1], tn_opts[-1], tk_opts[-1]


def mm(a, b, mode, name, out_dtype=F32, bias=None):
    if mode == "nn":
        (M, K), (K2, N) = a.shape, b.shape
        dims = (((1,), (0,)), ((), ()))
    elif mode == "nt":
        (M, K), (N, K2) = a.shape, b.shape
        dims = (((1,), (1,)), ((), ()))
    else:
        (K, M), (K2, N) = a.shape, b.shape
        dims = (((0,), (0,)), ((), ()))
    assert K == K2, (a.shape, b.shape, mode)
    tm, tn, tk = _mm_tiles(M, N, K, mode, a.dtype.itemsize, b.dtype.itemsize, jnp.dtype(out_dtype).itemsize)
    nk = K // tk
    if mode == "tn":
        a_spec = pl.BlockSpec((tk, tm), lambda i, j, k: (k, i))
        b_spec = pl.BlockSpec((tk, tn), lambda i, j, k: (k, j))
    elif mode == "nn":
        a_spec = pl.BlockSpec((tm, tk), lambda i, j, k: (i, k))
        b_spec = pl.BlockSpec((tk, tn), lambda i, j, k: (k, j))
    else:
        a_spec = pl.BlockSpec((tm, tk), lambda i, j, k: (i, k))
        b_spec = pl.BlockSpec((tn, tk), lambda i, j, k: (j, k))
    in_specs = [a_spec, b_spec]
    operands = [a, b]
    if bias is not None:
        in_specs.append(pl.BlockSpec((1, tn), lambda i, j, k: (0, j)))
        operands.append(bias)
    has_bias = bias is not None

    def body(*refs):
        a_ref, b_ref = refs[0], refs[1]
        bias_ref = refs[2] if has_bias else None
        o_ref = refs[3] if has_bias else refs[2]
        prod = lax.dot_general(a_ref[...].astype(BF16), b_ref[...].astype(BF16), dims,
                               preferred_element_type=F32)
        if nk == 1:
            if has_bias:
                prod = prod + bias_ref[...]
            o_ref[...] = prod.astype(o_ref.dtype)
        else:
            acc_ref = refs[-1]
            k = pl.program_id(2)

            @pl.when(k == 0)
            def _():
                acc_ref[...] = jnp.zeros_like(acc_ref)

            acc_ref[...] += prod

            @pl.when(k == nk - 1)
            def _():
                out = acc_ref[...]
                if has_bias:
                    out = out + bias_ref[...]
                o_ref[...] = out.astype(o_ref.dtype)

    return pl.pallas_call(
        body, name=name,
        grid=(M // tm, N // tn, nk),
        in_specs=in_specs,
        out_specs=pl.BlockSpec((tm, tn), lambda i, j, k: (i, j)),
        out_shape=jax.ShapeDtypeStruct((M, N), out_dtype),
        scratch_shapes=[pltpu.VMEM((tm, tn), F32)] if nk > 1 else [],
        compiler_params=_params(("parallel", "parallel", "arbitrary")),
    )(*operands)


def mm_fused(a, b, mode, name, tn, epi, epi_outs, pro=None, pro_rows=(), pro_vecs=(), pro_out=False, n_pro_sums=0,
             epi_rows=(), epi_vecs=(), b_blocks=None, n_cols=None):
    M, K = a.shape
    if b_blocks is not None:
        n_b, N = len(b_blocks), n_cols
    else:
        n_b = b.shape[0] if b.ndim == 3 else 1
        N = b.shape[-1] if mode == "nn" else b.shape[0]
    dims = (((1,), (0,)), ((), ())) if mode == "nn" else (((1,), (1,)), ((), ()))
    nj = N // tn
    epi_outs = [o if len(o) == 3 else (*o, None) for o in epi_outs]
    row_bytes = 2 * (K * a.dtype.itemsize + sum(K * r.dtype.itemsize for r in pro_rows) + (2 * K if pro_out else 0)
                     + sum(w * r.dtype.itemsize * (r.shape[0] if r.ndim == 3 else 1) for r, w in epi_rows)
                     + sum(w * jnp.dtype(dt).itemsize * (L or 1) for w, dt, L in epi_outs)
                     ) + (2 * K if pro is not None else 0)
    fixed = 2 * n_b * K * tn * b.dtype.itemsize
    tm = next((t for t in (1024, 512, 256, 128) if M % t == 0 and t * row_bytes + fixed <= MM_VMEM_BUDGET), M)
    row = lambda i, j: (i, 0)
    tile = lambda i, j: (i, j)
    stack = lambda i, j: (0, i, j)
    in_specs = [pl.BlockSpec((tm, K), row)] + [pl.BlockSpec((tm, K), row) for _ in pro_rows]
    in_specs += [pl.BlockSpec(v.shape, lambda i, j: (0, 0)) for v in pro_vecs]
    if b_blocks is not None:
        in_specs += [pl.BlockSpec(shape, imap) for shape, imap in b_blocks]
    elif b.ndim == 3:
        in_specs += [pl.BlockSpec((None, K, tn), lambda i, j, h=h: (h, 0, j)) for h in range(n_b)]
    elif mode == "nn":
        in_specs += [pl.BlockSpec((K, tn), lambda i, j: (0, j))]
    else:
        in_specs += [pl.BlockSpec((tn, K), lambda i, j: (j, 0))]
    in_specs += [pl.BlockSpec((r.shape[0], tm, w), stack) if r.ndim == 3 else pl.BlockSpec((tm, w), tile)
                 for r, w in epi_rows]
    in_specs += [pl.BlockSpec((1, tn), lambda i, j: (0, j)) for _ in epi_vecs]
    out_specs, out_shape = [], []
    if pro_out:
        out_specs.append(pl.BlockSpec((tm, K), row))
        out_shape.append(jax.ShapeDtypeStruct((M, K), BF16))
    for _ in range(n_pro_sums):
        out_specs.append(pl.BlockSpec((1, K), lambda i, j: (0, 0)))
        out_shape.append(jax.ShapeDtypeStruct((1, K), F32))
    for w, dt, L in epi_outs:
        out_specs.append(pl.BlockSpec((tm, w), tile) if L is None else pl.BlockSpec((L, tm, w), stack))
        out_shape.append(jax.ShapeDtypeStruct((M, nj * w) if L is None else (L, M, nj * w), dt))
    n_pr, n_pv, n_er, n_ev = len(pro_rows), len(pro_vecs), len(epi_rows), len(epi_vecs)
    n_a = 1 + n_pr + n_pv
    n_in = n_a + n_b + n_er + n_ev
    n_po = 1 if pro_out else 0

    def body(*refs):
        i, j = pl.program_id(0), pl.program_id(1)
        a_ref = refs[0]
        outs = refs[n_in:]
        if pro is not None:
            lhs_ref = refs[-1]

            @pl.when(j == 0)
            def _():
                res = pro(*[r[...] for r in refs[:1 + n_pr + n_pv]])
                if not isinstance(res, (tuple, list)):
                    res = (res,)
                lhs_ref[...] = res[0]
                if pro_out:
                    outs[0][...] = res[0]
                for s_ref, val in zip(outs[n_po:n_po + n_pro_sums], res[1:]):
                    part = jnp.sum(val.astype(F32), axis=0, keepdims=True)

                    @pl.when(i == 0)
                    def _(s_ref=s_ref, part=part):
                        s_ref[...] = part

                    @pl.when(i != 0)
                    def _(s_ref=s_ref, part=part):
                        s_ref[...] += part

            lhs = lhs_ref[...]
        else:
            lhs = a_ref[...].astype(BF16)
        accs = [lax.dot_general(lhs, b_ref[...].astype(BF16), dims, preferred_element_type=F32)
                for b_ref in refs[n_a:n_a + n_b]]
        res = epi(*accs, *[r[...] for r in refs[n_a + n_b:n_in]])
        if not isinstance(res, (tuple, list)):
            res = (res,)
        for o_ref, val in zip(outs[n_po + n_pro_sums:], res):
            if isinstance(val, (tuple, list)):
                for h, part in enumerate(val):
                    o_ref[h] = part.astype(o_ref.dtype)
            else:
                o_ref[...] = val.astype(o_ref.dtype)

    return pl.pallas_call(
        body, name=name,
        grid=(M // tm, nj),
        in_specs=in_specs, out_specs=out_specs, out_shape=out_shape,
        scratch_shapes=[pltpu.VMEM((tm, K), BF16)] if pro is not None else [],
        compiler_params=_params(("arbitrary", "arbitrary")),
    )(a, *pro_rows, *pro_vecs, *([b] * n_b), *[r for r, _ in epi_rows], *epi_vecs)


def rowwise(fn, rows, vecs, outs, sums, name, tm=None):
    norm = [(r, r.shape[1], 0) if not isinstance(r, tuple) else r for r in rows]
    S = norm[0][0].shape[0]
    if tm is None:
        tm = _row_tile(S, sum(w * r.dtype.itemsize for r, w, _ in norm)
                       + sum(n * jnp.dtype(dt).itemsize for n, dt in outs))
    n_rows, n_vecs, n_outs, n_sums = len(norm), len(vecs), len(outs), len(sums)
    in_specs = [pl.BlockSpec((tm, w), lambda i, cb=cb: (i, cb)) for _, w, cb in norm]
    in_specs += [pl.BlockSpec(v.shape, lambda i: (0, 0)) for v in vecs]
    out_specs = [pl.BlockSpec((tm, n), lambda i: (i, 0)) for n, _ in outs]
    out_specs += [pl.BlockSpec((1, n), lambda i: (0, 0)) for n in sums]
    out_shape = [jax.ShapeDtypeStruct((S, n), dt) for n, dt in outs]
    out_shape += [jax.ShapeDtypeStruct((1, n), F32) for n in sums]

    def body(*refs):
        ins = [r[...] for r in refs[:n_rows + n_vecs]]
        res = fn(*ins)
        if not isinstance(res, (tuple, list)):
            res = (res,)
        out_refs = refs[n_rows + n_vecs:]
        for o_ref, val in zip(out_refs[:n_outs], res[:n_outs]):
            o_ref[...] = val.astype(o_ref.dtype)
        if n_sums:
            i = pl.program_id(0)
            for s_ref, val in zip(out_refs[n_outs:], res[n_outs:]):
                part = jnp.sum(val.astype(F32), axis=0, keepdims=True)

                @pl.when(i == 0)
                def _(s_ref=s_ref, part=part):
                    s_ref[...] = part

                @pl.when(i != 0)
                def _(s_ref=s_ref, part=part):
                    s_ref[...] += part

    res = pl.pallas_call(
        body, name=name,
        grid=(S // tm,),
        in_specs=in_specs, out_specs=out_specs, out_shape=out_shape,
        compiler_params=_params(("arbitrary",) if n_sums else ("parallel",)),
    )(*[r for r, _, _ in norm], *vecs)
    return res


def _sigmoid(x):
    return jax.nn.sigmoid(x)


def _rms(x):
    r = lax.rsqrt(jnp.mean(x * x, axis=-1, keepdims=True) + EPS)
    return x * r, r


def _rms_bwd(xhat, r, dxhat):
    return r * (dxhat - xhat * jnp.mean(dxhat * xhat, axis=-1, keepdims=True))


def norm_mod(h, g, sh, sc, name):
    def f(h, g, sh, sc):
        xhat, _ = _rms(h)
        return ((xhat * g) * (1 + sc) + sh).astype(BF16)
    return rowwise(f, [h], [g, sh, sc], [(h.shape[1], BF16)], [], name)[0]


def norm_mod_bwd(h, dhn, dh_out, g, sc, name):
    D = h.shape[1]

    def f(h, dhn, dres, g, sc):
        xhat, r = _rms(h)
        dxn = dhn * (1 + sc)
        return _rms_bwd(xhat, r, dxn * g) + dres, dhn, dhn * (xhat * g), dxn * xhat

    return rowwise(f, [h, dhn, dh_out], [g, sc], [(D, F32)], [D, D, D], name)


def residual(h, y, gate, coef, name, bias=None):
    D = h.shape[1]
    if bias is None:
        def f(h, y, gate):
            return h + (coef * gate) * y
        return rowwise(f, [h, y], [gate], [(D, F32)], [], name)[0], y

    def fb(h, y, gate, bias):
        yb = y + bias
        return h + (coef * gate) * yb, yb
    return rowwise(fb, [h, y], [gate, bias], [(D, F32), (D, F32)], [], name)


def residual_bwd(dh_out, y, gate, coef, name, with_bias_sum=False):
    D = y.shape[1]

    def f(dh, y, gate):
        dy = (coef * gate) * dh
        res = (dy.astype(BF16), coef * dh * y)
        return res + ((dy,) if with_bias_sum else ())
    return rowwise(f, [dh_out, y], [gate], [(D, BF16)], [D, D] if with_bias_sum else [D], name)


def ffn_w13_dx(dab, gw13, l, i):
    _, S, F = dab.shape
    D, C = gw13.shape[3:]
    tm = _tile(S, (1024, 512, 256, 128))
    nt = (((1,), (1,)), ((), ()))

    def body(a_ref, b_ref, o_ref, acc_ref):
        k = pl.program_id(1)
        prod = lax.dot_general(a_ref[...], b_ref[...], nt, preferred_element_type=F32)

        @pl.when(k == 0)
        def _():
            acc_ref[...] = prod

        @pl.when((k > 0) & (k < 3))
        def _():
            acc_ref[...] += prod

        @pl.when(k == 3)
        def _():
            o_ref[...] = acc_ref[...] + prod

    return pl.pallas_call(
        body, name="ffn_w13_dx",
        grid=(S // tm, 4),
        in_specs=[pl.BlockSpec((None, tm, C), lambda r, k: (k // 2, r, k % 2)),
                  pl.BlockSpec((None, None, None, D, C), lambda r, k: (k, l, i, 0, 0))],
        out_specs=pl.BlockSpec((tm, D), lambda r, k: (r, 0)),
        out_shape=jax.ShapeDtypeStruct((S, D), F32),
        scratch_shapes=[pltpu.VMEM((tm, D), F32)],
        compiler_params=_params(("parallel", "arbitrary")),
    )(dab, gw13)


def ffn_w13_grad(hn, dab):
    S, D = hn.shape
    F = dab.shape[2]
    C = F // 2
    tk = next(t for t in (2048, 1024, 512, 256, 128) if S % t == 0)
    tn_dims = (((0,), (0,)), ((), ()))
    nk = S // tk

    def body(a_ref, b_ref, o_ref, acc_ref):
        k = pl.program_id(1)

        @pl.when(k == 0)
        def _():
            acc_ref[...] = jnp.zeros_like(acc_ref)

        acc_ref[...] += lax.dot_general(a_ref[...], b_ref[...], tn_dims, preferred_element_type=F32)

        @pl.when(k == nk - 1)
        def _():
            o_ref[...] = acc_ref[...]

    return pl.pallas_call(
        body, name="ffn_w13_dw",
        grid=(4, nk),
        in_specs=[pl.BlockSpec((tk, D), lambda j, k: (k, 0)),
                  pl.BlockSpec((None, tk, C), lambda j, k: (j // 2, k, j % 2))],
        out_specs=pl.BlockSpec((None, D, C), lambda j, k: (j, 0, 0)),
        out_shape=jax.ShapeDtypeStruct((4, D, C), F32),
        scratch_shapes=[pltpu.VMEM((D, C), F32)],
        compiler_params=_params(("parallel", "arbitrary")),
    )(hn, dab)


def ffn_fwd(h, g, sh, sc, gate, gw13, l, i, w2):
    F, D = w2.shape
    C = F // 2

    def norm(h, g, sh, sc):
        xhat, _ = _rms(h)
        return ((xhat * g) * (1 + sc) + sh).astype(BF16)

    def act(a, b):
        sig = _sigmoid(a)
        sa = a * sig
        return (b * (sig + sa * (1 - sig)), sa), sa * b
    blocks = [((None, None, None, D, C), lambda r, j, half=half: (2 * half + j, l, i, 0, 0)) for half in range(2)]
    hn, dt_dab, t = mm_fused(h, gw13, "nn", "ffn_w13", C, act, [(C, BF16, 2), (C, BF16)],
                             pro=norm, pro_vecs=[g, sh, sc], pro_out=True, b_blocks=blocks, n_cols=F)

    def res(acc, h, gate):
        return h + (0.5 * gate) * acc, acc
    h_out, y = mm_fused(t, w2, "nn", "ffn_w2", D, res, [(D, F32), (D, F32)], epi_rows=[(h, D)], epi_vecs=[gate])
    return h_out, (h, hn, dt_dab, t, y)


def ffn_bwd(dh_out, saved, g, sc, gate, gw13, l, i, w2):
    h, hn, dt_dab, t, y = saved
    F, D = w2.shape
    C = F // 2

    def scale(dh, y, gate):
        return ((0.5 * gate) * dh).astype(BF16), 0.5 * dh * y

    def act_bwd(dt, f):
        return ((dt * f[0].astype(F32), dt * f[1].astype(F32)),)
    dy, d_gate, dab = mm_fused(dh_out, w2, "nt", "ffn_w2_dx", C, act_bwd, [(C, BF16, 2)],
                               pro=scale, pro_rows=[y], pro_vecs=[gate], pro_out=True, n_pro_sums=1,
                               epi_rows=[(dt_dab, C)])
    dw2 = mm(t, dy, "tn", "ffn_w2_dw")
    dw13 = ffn_w13_grad(hn, dab)
    dhn = ffn_w13_dx(dab, gw13, l, i)
    dh_in, d_sh, d_sc, d_g = norm_mod_bwd(h, dhn, dh_out, g, sc, "norm_mod_bwd")
    return dh_in, (d_sh, d_sc, d_gate, d_g), dw13, dw2


def _shifted(xbuf, n):
    return [xbuf] + [pltpu.roll(xbuf, n - b, 0) for b in range(1, 8)]


def conv_fwd(u, w_dw, b_dw, ln_g, ln_b):
    S, D = u.shape
    tm = _tile(S, (256, 128))
    rc = 32
    first_tap = CONV_HALO - (CONV_WIDTH - 1)
    w = jnp.concatenate([w_dw, jnp.zeros((CONV_HALO - CONV_WIDTH, D), F32)], axis=0)

    def body(cur_ref, prev_ref, w_ref, b_ref, g_ref, beta_ref, z_ref, s_ref):
        i = pl.program_id(0)
        prev = jnp.where(i == 0, jnp.zeros((CONV_HALO, D), F32), prev_ref[...])
        xs = _shifted(jnp.concatenate([prev, cur_ref[...]], axis=0), tm + CONV_HALO)
        for c0 in range(0, tm, rc):
            acc = jnp.zeros((rc, D), F32)
            for k in range(CONV_WIDTH):
                off = first_tap + k
                a8, b = off // 8 * 8, off % 8
                acc = acc + w_ref[k:k + 1, :] * xs[b][c0 + a8:c0 + a8 + rc, :]
            z_ref[c0:c0 + rc, :] = acc + b_ref[...]
        z = z_ref[...]
        mu = jnp.mean(z, axis=-1, keepdims=True)
        zc = z - mu
        r = lax.rsqrt(jnp.mean(zc * zc, axis=-1, keepdims=True) + EPS)
        un = zc * r * g_ref[...] + beta_ref[...]
        s_ref[...] = (un * _sigmoid(un)).astype(BF16)

    nb = tm // CONV_HALO
    vec = pl.BlockSpec((1, D), lambda i: (0, 0))
    return pl.pallas_call(
        body, name="conv_fwd",
        grid=(S // tm,),
        in_specs=[pl.BlockSpec((tm, D), lambda i: (i, 0)),
                  pl.BlockSpec((CONV_HALO, D), lambda i: (jnp.maximum(i * nb - 1, 0), 0)),
                  pl.BlockSpec((CONV_HALO, D), lambda i: (0, 0)), vec, vec, vec],
        out_specs=[pl.BlockSpec((tm, D), lambda i: (i, 0)), pl.BlockSpec((tm, D), lambda i: (i, 0))],
        out_shape=[jax.ShapeDtypeStruct((S, D), F32), jax.ShapeDtypeStruct((S, D), BF16)],
        compiler_params=_params(("parallel",)),
    )(u, u, w, b_dw, ln_g, ln_b)


def conv_bwd(dz, u, w_dw):
    S, D = u.shape
    tm = _tile(S, (256, 128))
    rc = 32
    first_tap = CONV_HALO - (CONV_WIDTH - 1)
    w = jnp.concatenate([w_dw, jnp.zeros((CONV_HALO - CONV_WIDTH, D), F32)], axis=0)
    n_tiles = S // tm
    nb = tm // CONV_HALO

    def body(dz_ref, dzn_ref, u_ref, up_ref, w_ref, du_ref, dw_ref):
        i = pl.program_id(0)
        nxt = jnp.where(i == n_tiles - 1, jnp.zeros((CONV_HALO, D), F32), dzn_ref[...])
        dzs = _shifted(jnp.concatenate([dz_ref[...], nxt], axis=0), tm + CONV_HALO)
        for c0 in range(0, tm, rc):
            acc = jnp.zeros((rc, D), F32)
            for m in range(CONV_WIDTH):
                a8, b = m // 8 * 8, m % 8
                acc = acc + w_ref[CONV_WIDTH - 1 - m:CONV_WIDTH - m, :] * dzs[b][c0 + a8:c0 + a8 + rc, :]
            du_ref[c0:c0 + rc, :] = acc
        prev = jnp.where(i == 0, jnp.zeros((CONV_HALO, D), F32), up_ref[...])
        us = _shifted(jnp.concatenate([prev, u_ref[...]], axis=0), tm + CONV_HALO)
        dz = dz_ref[...]

        @pl.when(i == 0)
        def _():
            dw_ref[...] = jnp.zeros_like(dw_ref)

        for k in range(CONV_WIDTH):
            off = first_tap + k
            a8, b = off // 8 * 8, off % 8
            dw_ref[k:k + 1, :] += jnp.sum(dz * us[b][a8:a8 + tm, :], axis=0, keepdims=True)

    last_blk = S // CONV_HALO - 1
    du, dw = pl.pallas_call(
        body, name="conv_bwd",
        grid=(n_tiles,),
        in_specs=[pl.BlockSpec((tm, D), lambda i: (i, 0)),
                  pl.BlockSpec((CONV_HALO, D), lambda i: (jnp.minimum((i + 1) * nb, last_blk), 0)),
                  pl.BlockSpec((tm, D), lambda i: (i, 0)),
                  pl.BlockSpec((CONV_HALO, D), lambda i: (jnp.maximum(i * nb - 1, 0), 0)),
                  pl.BlockSpec((CONV_HALO, D), lambda i: (0, 0))],
        out_specs=[pl.BlockSpec((tm, D), lambda i: (i, 0)), pl.BlockSpec((CONV_HALO, D), lambda i: (0, 0))],
        out_shape=[jax.ShapeDtypeStruct((S, D), F32), jax.ShapeDtypeStruct((CONV_HALO, D), F32)],
        compiler_params=_params(("arbitrary",)),
    )(dz, dz, u, u, w)
    return du, dw[:CONV_WIDTH]


def conv_module_fwd(h, g, sh, sc, gate, p):
    D = h.shape[1]
    hn = norm_mod(h, g, sh, sc, "conv_norm_mod")
    pre = mm(hn, p["w_pw1"], "nn", "conv_pw1")
    ba, bg = p["b_pw1"][:, :D], p["b_pw1"][:, D:]

    def glu(a, gt, ba, bg):
        return (a + ba) * _sigmoid(gt + bg)
    u = rowwise(glu, [(pre, D, 0), (pre, D, 1)], [ba, bg], [(D, F32)], [], "conv_glu")[0]
    z, s = conv_fwd(u, p["w_dw"], p["b_dw"], p["ln_g"], p["ln_b"])
    yraw = mm(s, p["w_pw2"], "nn", "conv_pw2")
    h_out, y = residual(h, yraw, gate, 1.0, "conv_residual", bias=p["b_pw2"])
    return h_out, (h, hn, pre, u, z, s, y)


def conv_module_bwd(dh_out, saved, g, sc, gate, p):
    h, hn, pre, u, z, s, y = saved
    D = h.shape[1]
    dy, d_gate, d_b_pw2 = residual_bwd(dh_out, y, gate, 1.0, "conv_residual_bwd", with_bias_sum=True)
    d_w_pw2 = mm(s, dy, "tn", "conv_pw2_dw")
    ds = mm(dy, p["w_pw2"], "nt", "conv_pw2_dx")

    def ln_bwd(z, ds, g, beta):
        mu = jnp.mean(z, axis=-1, keepdims=True)
        zc = z - mu
        r = lax.rsqrt(jnp.mean(zc * zc, axis=-1, keepdims=True) + EPS)
        xhat = zc * r
        un = xhat * g + beta
        sig = _sigmoid(un)
        d_un = ds * (sig * (1 + un * (1 - sig)))
        dxhat = d_un * g
        dz = r * (dxhat - jnp.mean(dxhat, axis=-1, keepdims=True)
                  - xhat * jnp.mean(dxhat * xhat, axis=-1, keepdims=True))
        return dz, d_un * xhat, d_un, dz
    dz, d_ln_g, d_ln_b, d_b_dw = rowwise(ln_bwd, [z, ds], [p["ln_g"], p["ln_b"]], [(D, F32)], [D, D, D],
                                         "conv_ln_bwd")
    du, d_w_dw = conv_bwd(dz, u, p["w_dw"])
    ba, bg = p["b_pw1"][:, :D], p["b_pw1"][:, D:]

    def glu_bwd(a, gt, du, ba, bg):
        sg = _sigmoid(gt + bg)
        da = du * sg
        dg = du * (a + ba) * (sg * (1 - sg))
        dpre = jnp.concatenate([da, dg], axis=1)
        return dpre.astype(BF16), dpre
    dpre, d_b_pw1 = rowwise(glu_bwd, [(pre, D, 0), (pre, D, 1), du], [ba, bg], [(2 * D, BF16)], [2 * D],
                            "conv_glu_bwd")
    d_w_pw1 = mm(hn, dpre, "tn", "conv_pw1_dw")
    dhn = mm(dpre, p["w_pw1"], "nt", "conv_pw1_dx")
    dh_in, d_sh, d_sc, d_g = norm_mod_bwd(h, dhn, dh_out, g, sc, "norm_mod_bwd")
    grads = dict(w_pw1=d_w_pw1, b_pw1=d_b_pw1, w_dw=d_w_dw, b_dw=d_b_dw, ln_g=d_ln_g, ln_b=d_ln_b,
                 w_pw2=d_w_pw2, b_pw2=d_b_pw2)
    return dh_in, (d_sh, d_sc, d_gate, d_g), grads


def _rope(x, c, s1, s2):
    n = x.shape[1]
    return x * c + pltpu.roll(x, n - QK_ROPE // 2, 1) * s1 + pltpu.roll(x, QK_ROPE // 2, 1) * s2


def _rope_t(dy, c, s1, s2):
    n = dy.shape[1]
    return dy * c + pltpu.roll(dy * s1, QK_ROPE // 2, 1) + pltpu.roll(dy * s2, n - QK_ROPE // 2, 1)


def rope_tables(positions):
    inv_freq = ROPE_THETA ** (-jnp.arange(0, QK_ROPE, 2, dtype=F32) / QK_ROPE)
    ang = positions.astype(F32)[:, None] * inv_freq
    cos, sin = jnp.cos(ang), jnp.sin(ang)
    S = positions.shape[0]
    one = jnp.ones((S, QK_NOPE), F32)
    z16 = jnp.zeros((S, QK_ROPE // 2), F32)
    zn = jnp.zeros((S, QK_NOPE), F32)
    zt = jnp.zeros((S, HEAD_PAD - QK_NOPE - QK_ROPE), F32)
    c = jnp.concatenate([one, cos, cos, zt], axis=1)
    s1 = jnp.concatenate([zn, -sin, z16, zt], axis=1)
    s2 = jnp.concatenate([zn, z16, sin, zt], axis=1)
    return c, s1, s2


def attn_fwd(qr, kv, kpe, n_heads):
    S = qr.shape[0]
    H = n_heads
    tk = _tile(S, (ATTN_TILE,))
    nk = S // tk
    w = 2 if nk % 2 == 0 else 1
    tq = w * tk
    c2 = (QK_NOPE + QK_ROPE) ** -0.5 * LOG2_E
    nt = (((1,), (1,)), ((), ()))

    assert V_HEAD < HEAD_PAD
    ones_row = HEAD_PAD - 1

    def body(q_ref, k_ref, v_ref, kpe_ref, o_ref, lse_ref, kf_ref, vt_ref, m_ref, acc_ref):
        qi = pl.program_id(1)
        feature = lax.broadcasted_iota(jnp.int32, (HEAD_PAD, tk), 0)

        @pl.when(qi == 0)
        def _():
            kf_ref[...] = k_ref[...] + kpe_ref[...]
            for c in range(nk):
                vt = jnp.transpose(v_ref[c * tk:(c + 1) * tk, :].astype(F32))
                vt_ref[c] = jnp.where(feature == ones_row, 1.0, vt).astype(BF16)

        q = q_ref[...]
        m_ref[...] = jnp.full((1, tq), -jnp.inf, F32)
        acc_ref[...] = jnp.zeros((HEAD_PAD, tq), F32)

        def tile(j, first_visible):
            k = kf_ref[pl.ds(pl.multiple_of(j * tk, tk), tk), :]
            t = lax.dot_general(k, q, nt, preferred_element_type=F32) * c2
            if first_visible is not None:
                krow = lax.broadcasted_iota(jnp.int32, (tk, tq), 0)
                qcol = lax.broadcasted_iota(jnp.int32, (tk, tq), 1)
                t = jnp.where(krow + first_visible <= qcol, t, NEG)
            m_old = m_ref[...]
            m_new = jnp.maximum(m_old, jnp.max(t, axis=0, keepdims=True))
            alpha = jnp.exp2(m_old - m_new)
            p = jnp.exp2(t - m_new)
            acc_ref[...] = alpha * acc_ref[...] + jnp.dot(vt_ref[j], p.astype(BF16), preferred_element_type=F32)
            m_ref[...] = m_new

        def unmasked(j, carry):
            tile(j, None)
            return carry

        lax.fori_loop(0, w * qi, unmasked, 0)
        for u in range(w):
            tile(w * qi + u, u * tk)
        acc = acc_ref[...]
        l = acc_ref[ones_row:ones_row + 1, :]
        out_feature = lax.broadcasted_iota(jnp.int32, (HEAD_PAD, tq), 0)
        o_ref[...] = jnp.transpose(jnp.where(out_feature == ones_row, 0.0, acc / l))
        lse = m_ref[...] + jnp.log(l) * LOG2_E
        for u in range(w):
            lse_ref[u] = lse[:, u * tk:(u + 1) * tk]

    return pl.pallas_call(
        body, name="attn_fwd",
        grid=(H, S // tq),
        in_specs=[pl.BlockSpec((tq, HEAD_PAD), lambda h, i: (i, h)),
                  pl.BlockSpec((S, HEAD_PAD), lambda h, i: (0, h)),
                  pl.BlockSpec((S, HEAD_PAD), lambda h, i: (0, H + h)),
                  pl.BlockSpec((S, HEAD_PAD), lambda h, i: (0, 0))],
        out_specs=[pl.BlockSpec((tq, HEAD_PAD), lambda h, i: (i, h)),
                   pl.BlockSpec((None, w, 1, tk), lambda h, i: (h, i, 0, 0))],
        out_shape=[jax.ShapeDtypeStruct((S, H * HEAD_PAD), F32), jax.ShapeDtypeStruct((H, nk, 1, tk), F32)],
        scratch_shapes=[pltpu.VMEM((S, HEAD_PAD), BF16), pltpu.VMEM((nk, HEAD_PAD, tk), BF16),
                        pltpu.VMEM((1, tq), F32), pltpu.VMEM((HEAD_PAD, tq), F32)],
        compiler_params=_params(("parallel", "arbitrary")),
    )(qr, kv, kv, kpe)


def attn_delta(o, do, n_heads):
    S = o.shape[0]
    H = n_heads
    tq = _tile(S, (ATTN_TILE,))
    nq = S // tq

    def body(o_ref, do_ref, d_ref):
        for c in range(nq):
            rows = slice(c * tq, (c + 1) * tq)
            prod = o_ref[rows, :] * do_ref[rows, :].astype(F32)
            d_ref[c] = jnp.sum(jnp.transpose(prod), axis=0, keepdims=True)

    return pl.pallas_call(
        body, name="attn_delta",
        grid=(H,),
        in_specs=[pl.BlockSpec((S, HEAD_PAD), lambda h: (0, h)), pl.BlockSpec((S, HEAD_PAD), lambda h: (0, h))],
        out_specs=pl.BlockSpec((None, nq, 1, tq), lambda h: (h, 0, 0, 0)),
        out_shape=jax.ShapeDtypeStruct((H, nq, 1, tq), F32),
        compiler_params=_params(("parallel",)),
    )(o, do)


def attn_bwd(qr, kv, kpe, do, lse2, delta, n_heads):
    S = qr.shape[0]
    H = n_heads
    tk = _tile(S, (ATTN_TILE,))
    nk = S // tk
    w = 2 if nk % 2 == 0 else 1
    tq = w * tk
    nq = S // tq
    scale = (QK_NOPE + QK_ROPE) ** -0.5
    c2 = scale * LOG2_E
    nt = (((1,), (1,)), ((), ()))
    lse2 = lse2.reshape(H, nq, 1, tq)
    delta4 = delta.reshape(H, nq, 1, tq)

    def body(k_ref, v_ref, kpe_ref, q_ref, do_ref, lse_ref, dl_ref, dq_ref, dk_ref, dv_ref, dka_ref, dva_ref,
             dqt_ref):
        kj = pl.program_id(1)
        k = k_ref[...] + kpe_ref[...]
        kt = jnp.transpose(k.astype(F32)).astype(BF16)
        v = v_ref[...]

        @pl.when(kj == 0)
        def _():
            dqt_ref[...] = jnp.zeros_like(dqt_ref)

        dka_ref[...] = jnp.zeros_like(dka_ref)
        dva_ref[...] = jnp.zeros_like(dva_ref)

        def tile(i, masked):
            start = pl.multiple_of(i * tq, tq)
            q = q_ref[pl.ds(start, tq), :]
            do = do_ref[pl.ds(start, tq), :]
            t = lax.dot_general(k, q, nt, preferred_element_type=F32) * c2
            if masked:
                krow = lax.broadcasted_iota(jnp.int32, (tk, tq), 0)
                qcol = lax.broadcasted_iota(jnp.int32, (tk, tq), 1)
                t = jnp.where(krow + (kj % w) * tk <= qcol, t, NEG)
            pt = jnp.exp2(t - lse_ref[i])
            dva_ref[...] += jnp.dot(pt.astype(BF16), do, preferred_element_type=F32)
            dpt = lax.dot_general(v, do, nt, preferred_element_type=F32)
            dst = (pt * (dpt - dl_ref[i]) * scale).astype(BF16)
            dka_ref[...] += jnp.dot(dst, q, preferred_element_type=F32)
            dqt_ref[i] += jnp.dot(kt, dst, preferred_element_type=F32)

        tile(kj // w, True)

        def unmasked(i, carry):
            tile(i, False)
            return carry

        lax.fori_loop(kj // w + 1, nq, unmasked, 0)
        dk_ref[...] = dka_ref[...]
        dv_ref[...] = dva_ref[...]

        @pl.when(kj == nk - 1)
        def _():
            for c in range(nq):
                dq_ref[c * tq:(c + 1) * tq, :] = jnp.transpose(dqt_ref[c])

    blk = pl.BlockSpec((tk, HEAD_PAD), lambda h, j: (j, h))
    whole = pl.BlockSpec((S, HEAD_PAD), lambda h, j: (0, h))
    stat = pl.BlockSpec((None, nq, 1, tq), lambda h, j: (h, 0, 0, 0))
    shp = jax.ShapeDtypeStruct((S, H * HEAD_PAD), F32)
    return pl.pallas_call(
        body, name="attn_bwd",
        grid=(H, nk),
        in_specs=[blk, pl.BlockSpec((tk, HEAD_PAD), lambda h, j: (j, H + h)),
                  pl.BlockSpec((tk, HEAD_PAD), lambda h, j: (j, 0)), whole, whole, stat, stat],
        out_specs=[whole, blk, blk],
        out_shape=[shp, shp, shp],
        scratch_shapes=[pltpu.VMEM((tk, HEAD_PAD), F32), pltpu.VMEM((tk, HEAD_PAD), F32),
                        pltpu.VMEM((nq, HEAD_PAD, tq), F32)],
        compiler_params=_params(("parallel", "arbitrary")),
    )(kv, kv, kpe, qr, do, lse2, delta4)


def _pad_heads(w, width):
    R = w.shape[0]
    w3 = w.reshape(R, -1, width)
    return jnp.pad(w3, ((0, 0), (0, 0), (0, HEAD_PAD - width))).reshape(R, -1)


def _unpad_heads(w, width):
    R = w.shape[0]
    return w.reshape(R, -1, HEAD_PAD)[:, :, :width].reshape(R, -1)


def mla_pad_weights(p):
    H = N_HEADS
    w_q_b = _pad_heads(p["w_q_b"], QK_NOPE + QK_ROPE)
    kvb = p["w_kv_b"].reshape(KV_LORA, H, QK_NOPE + V_HEAD)
    wk = _pad_heads(kvb[:, :, :QK_NOPE].reshape(KV_LORA, -1), QK_NOPE)
    wv = _pad_heads(kvb[:, :, QK_NOPE:].reshape(KV_LORA, -1), V_HEAD)
    D = p["w_kv_a"].shape[0]
    a = p["w_kv_a"]
    w_kv_a = jnp.concatenate([a[:, :KV_LORA], jnp.zeros((D, QK_NOPE), a.dtype), a[:, KV_LORA:],
                              jnp.zeros((D, HEAD_PAD - QK_NOPE - QK_ROPE), a.dtype)], axis=1)
    wo = p["w_o"].reshape(H, V_HEAD, -1)
    w_o = jnp.pad(wo, ((0, 0), (0, HEAD_PAD - V_HEAD), (0, 0))).reshape(H * HEAD_PAD, -1)
    return dict(w_q_a=p["w_q_a"], w_q_b=w_q_b, w_kv_b=jnp.concatenate([wk, wv], axis=1), w_kv_a=w_kv_a, w_o=w_o)


def mla_kv_fwd(h, g, sh, sc, kv_a_norm_g, pw, tabs):
    hkv = norm_mod(h, g, sh, sc, "kv_norm_mod")
    ckvp = mm(hkv, pw["w_kv_a"], "nn", "kv_a")

    def f(ckv, kpe, c, s1, s2, g):
        xhat, _ = _rms(ckv)
        return (xhat * g).astype(BF16), _rope(kpe, c, s1, s2).astype(BF16)
    ckv_n, kpe_r = rowwise(f, [(ckvp, KV_LORA, 0), (ckvp, HEAD_PAD, KV_LORA // HEAD_PAD), *tabs], [kv_a_norm_g],
                           [(KV_LORA, BF16), (HEAD_PAD, BF16)], [], "kv_a_norm_rope")
    kv = mm(ckv_n, pw["w_kv_b"], "nn", "kv_b", out_dtype=BF16)
    return kv, kpe_r, (h, hkv, ckvp, ckv_n)


def mla_kv_bwd(dh_stream, dk, dv, saved, g, sc, kv_a_norm_g, pw, tabs):
    h, hkv, ckvp, ckv_n = saved
    H = N_HEADS
    lane = jnp.arange(HEAD_PAD)
    pe_mask = ((lane >= QK_NOPE) & (lane < QK_NOPE + QK_ROPE)).astype(F32)[None, :]

    def f(dk, dv, c, s1, s2, mask):
        tot = dk[:, :HEAD_PAD]
        for hh in range(1, H):
            tot = tot + dk[:, hh * HEAD_PAD:(hh + 1) * HEAD_PAD]
        dkpe = _rope_t(tot * mask, c, s1, s2) * mask
        return jnp.concatenate([dk, dv], axis=1).astype(BF16), dkpe
    dkv, dkpe = rowwise(f, [dk, dv, *tabs], [pe_mask], [(2 * H * HEAD_PAD, BF16), (HEAD_PAD, F32)], [],
                        "kv_split_bwd")
    d_w_kv_b = mm(ckv_n, dkv, "tn", "kv_b_dw")
    dckv_n = mm(dkv, pw["w_kv_b"], "nt", "kv_b_dx")

    def f2(ckv, dn, dkpe, g):
        xhat, r = _rms(ckv)
        dx = _rms_bwd(xhat, r, dn * g)
        return jnp.concatenate([dx, dkpe], axis=1).astype(BF16), dn * xhat
    dckvp, d_kv_a_g = rowwise(f2, [(ckvp, KV_LORA, 0), dckv_n, dkpe], [kv_a_norm_g],
                              [(KV_LORA + HEAD_PAD, BF16)], [KV_LORA], "kv_a_norm_bwd")
    d_w_kv_a = mm(hkv, dckvp, "tn", "kv_a_dw")
    dhkv = mm(dckvp, pw["w_kv_a"], "nt", "kv_a_dx")
    dh, d_sh, d_sc, d_g = norm_mod_bwd(h, dhkv, dh_stream, g, sc, "norm_mod_bwd")
    return dh, (d_sh, d_sc, d_g), d_kv_a_g, d_w_kv_a, d_w_kv_b


def mla_fwd(h, g, sh, sc, gate, q_a_norm_g, pw, kv, kpe_r, tabs):
    H = N_HEADS
    hn = norm_mod(h, g, sh, sc, "mla_norm_mod")
    qa = mm(hn, pw["w_q_a"], "nn", "q_a")

    def f(qa, g):
        xhat, _ = _rms(qa)
        return (xhat * g).astype(BF16)
    qa_n = rowwise(f, [qa], [q_a_norm_g], [(qa.shape[1], BF16)], [], "q_a_norm")[0]
    qp = mm(qa_n, pw["w_q_b"], "nn", "q_b")

    def frope(q, c, s1, s2):
        return jnp.concatenate([_rope(q[:, hh * HEAD_PAD:(hh + 1) * HEAD_PAD], c, s1, s2) for hh in range(H)],
                               axis=1).astype(BF16)
    qr = rowwise(frope, [qp, *tabs], [], [(H * HEAD_PAD, BF16)], [], "q_rope")[0]
    o, lse = attn_fwd(qr, kv, kpe_r, H)
    y = mm(o, pw["w_o"], "nn", "w_o")
    h_out, _ = residual(h, y, gate, 1.0, "mla_residual")
    return h_out, (h, hn, qa, qa_n, qr, o, lse, y)


def mla_bwd(dh_out, saved, g, sc, gate, q_a_norm_g, pw, kv, kpe_r, tabs):
    h, hn, qa, qa_n, qr, o, lse, y = saved
    H = N_HEADS
    dy, d_gate = residual_bwd(dh_out, y, gate, 1.0, "mla_residual_bwd")
    d_w_o = mm(o, dy, "tn", "w_o_dw")
    do = mm(dy, pw["w_o"], "nt", "w_o_dx", out_dtype=BF16)
    delta = attn_delta(o, do, H)
    dqr, dk, dv = attn_bwd(qr, kv, kpe_r, do, lse, delta, H)

    def frope_t(dq, c, s1, s2):
        return jnp.concatenate([_rope_t(dq[:, hh * HEAD_PAD:(hh + 1) * HEAD_PAD], c, s1, s2) for hh in range(H)],
                               axis=1).astype(BF16)
    dqp = rowwise(frope_t, [dqr, *tabs], [], [(H * HEAD_PAD, BF16)], [], "q_rope_bwd")[0]
    d_w_q_b = mm(qa_n, dqp, "tn", "q_b_dw")
    dqa_n = mm(dqp, pw["w_q_b"], "nt", "q_b_dx")

    def f(qa, dn, g):
        xhat, r = _rms(qa)
        return _rms_bwd(xhat, r, dn * g).astype(BF16), dn * xhat
    dqa, d_q_a_g = rowwise(f, [qa, dqa_n], [q_a_norm_g], [(qa.shape[1], BF16)], [qa.shape[1]], "q_a_norm_bwd")
    d_w_q_a = mm(hn, dqa, "tn", "q_a_dw")
    dhn = mm(dqa, pw["w_q_a"], "nt", "q_a_dx")
    dh_in, d_sh, d_sc, d_g = norm_mod_bwd(h, dhn, dh_out, g, sc, "norm_mod_bwd")
    grads = dict(w_q_a=d_w_q_a, q_a_norm_g=d_q_a_g, w_q_b=d_w_q_b, w_o=d_w_o)
    return dh_in, (d_sh, d_sc, d_gate, d_g), grads, dk, dv


def loss_head(h, target, g):
    D = h.shape[1]

    def f(h, t, g):
        xhat, r = _rms(h)
        err = xhat * g - t
        dy = err * (1.0 / D)
        dh = _rms_bwd(xhat, r, dy * g)
        return dh, (0.5 / D) * err * err, dy * xhat
    return rowwise(f, [h, target], [g], [(D, F32)], [D, D], "loss_head")


def _place():
    x, y, c = lax.axis_index("x"), lax.axis_index("y"), lax.axis_index("c")
    chips = [(1 - x, y), (x, 1 - y), (1 - x, 1 - y)]
    return x, y, c, chips


HBM_SPEC = pl.BlockSpec(memory_space=pltpu.HBM)


def all_gather8(v):
    m, n = v.shape

    def body(x_ref, out_ref, send_sems, recv_sems, local_sem):
        x, y, c, chips = _place()
        me, sibling = (x, y, c), (x, y, 1 - c)

        def rows(px, py, pc):
            return out_ref.at[4 * px + 2 * py + pc]

        def copy(k, block, to, src=None):
            return pltpu.make_async_remote_copy(
                src_ref=rows(*block) if src is None else src, dst_ref=rows(*block),
                send_sem=send_sems.at[k], recv_sem=recv_sems.at[k], device_id=to, device_id_type=MESH)

        mine = pltpu.make_async_copy(x_ref, rows(*me), local_sem)
        mine.start()
        first = [copy(0, me, sibling, src=x_ref)]
        first += [copy(1 + j, me, (*chip, c), src=x_ref) for j, chip in enumerate(chips)]
        for cp in first:
            cp.start()
        passed = [copy(4 + j, (*chip, c), sibling) for j, chip in enumerate(chips)]
        for j, chip in enumerate(chips):
            copy(1 + j, (*chip, c), me).wait_recv()
            passed[j].start()
        copy(0, sibling, me).wait_recv()
        for j, chip in enumerate(chips):
            copy(4 + j, (*chip, 1 - c), me).wait_recv()
        for cp in first + passed:
            cp.wait_send()
        mine.wait()

    return pl.pallas_call(
        body, name="all_gather8",
        out_shape=jax.ShapeDtypeStruct((8, m, n), v.dtype),
        in_specs=[pl.BlockSpec(memory_space=pltpu.VMEM)],
        out_specs=pl.BlockSpec(memory_space=pltpu.VMEM),
        scratch_shapes=[pltpu.SemaphoreType.DMA((7,)), pltpu.SemaphoreType.DMA((7,)), pltpu.SemaphoreType.DMA],
        compiler_params=pltpu.CompilerParams(vmem_limit_bytes=VMEM_LIMIT_BYTES),
    )(v)


def gather_weights(bufs):
    n = len(bufs)

    def body(*refs):
        ins, outs = refs[:n], refs[n:2 * n]
        send_sems, recv_sems = refs[2 * n:]
        x, y, c, chips = _place()
        across_x, across_y, across_both = chips
        sibling = (x, y, 1 - c)
        me = 2 * x + y
        via_in = (x + (1 - c) * (1 - 2 * x), y + c * (1 - 2 * y))
        via_out = (x + c * (1 - 2 * x), y + (1 - c) * (1 - 2 * y))

        def idx(chip):
            return 2 * chip[0] + chip[1]

        def copy(w, k, src, dst, to):
            return pltpu.make_async_remote_copy(src_ref=src, dst_ref=dst, send_sem=send_sems.at[6 * w + k],
                                                recv_sem=recv_sems.at[6 * w + k], device_id=to, device_id_type=MESH)

        def landed(w, k, chip):
            blk = outs[w].at[idx(chip), c]
            copy(w, k, blk, blk, (*chip, c)).wait_recv()
            return blk

        sends = [copy(w, j, ins[w].at[me, c], outs[w].at[me, c], (*chip, c))
                 for w in range(n) for j, chip in enumerate((across_x, across_y))]
        for cp in sends:
            cp.start()
        for w in range(n):
            blk = landed(w, c, via_in)
            sends += [copy(w, 2, blk, blk, (*via_out, c)), copy(w, 3 + c, blk, blk, sibling)]
            sends[-2].start()
            sends[-1].start()
        for w in range(n):
            blk = landed(w, 1 - c, via_out)
            sends.append(copy(w, 4 - c, blk, blk, sibling))
            sends[-1].start()
        for w in range(n):
            blk = landed(w, 2, across_both)
            sends.append(copy(w, 5, blk, blk, sibling))
            sends[-1].start()
        for w in range(n):
            for j, chip in enumerate(chips):
                other = outs[w].at[idx(chip), 1 - c]
                copy(w, 3 + j, other, other, sibling).wait_recv()
        for cp in sends:
            cp.wait_send()

    return pl.pallas_call(
        body, name="gather_weights",
        out_shape=[jax.ShapeDtypeStruct(b.shape, b.dtype) for b in bufs],
        in_specs=[HBM_SPEC] * n, out_specs=[HBM_SPEC] * n,
        input_output_aliases={w: w for w in range(n)},
        scratch_shapes=[pltpu.SemaphoreType.DMA((6 * n,)), pltpu.SemaphoreType.DMA((6 * n,))],
    )(*bufs)


SEM_SPEC = pl.BlockSpec(memory_space=pltpu.SEMAPHORE)
SPLIT_COPY = pltpu.CompilerParams(has_side_effects=pltpu.SideEffectType.DATAFLOW_SIDE_EFFECTING)
PEERS_PER_BLOCK = 6


def gather_start(groups, carried):
    flat = [a for grp in groups for a in grp]
    group_of = [g for g, grp in enumerate(groups) for _ in grp]
    n, n_g, n_all = len(flat), len(groups), len(flat) + len(carried)

    def body(*refs):
        ins, sems = refs[:n], refs[n_all:n_all + 2 * n_g]
        x, y, c, chips = _place()
        me = 2 * x + y
        for w in range(n):
            mine = ins[w].at[me, c]
            for chip in chips:
                for core in range(2):
                    pltpu.make_async_remote_copy(src_ref=mine, dst_ref=mine, send_sem=sems[2 * group_of[w]],
                                                 recv_sem=sems[2 * group_of[w] + 1], device_id=(*chip, core),
                                                 device_id_type=MESH).start()

    operands = flat + list(carried)
    res = pl.pallas_call(
        body, name="gather_start",
        out_shape=[pltpu.SemaphoreType.DMA(())] * (2 * n_g) + [pltpu.HBM(a.shape, a.dtype) for a in operands],
        in_specs=[HBM_SPEC] * n_all,
        out_specs=[SEM_SPEC] * (2 * n_g) + [HBM_SPEC] * n_all,
        input_output_aliases={w: 2 * n_g + w for w in range(n_all)},
        compiler_params=SPLIT_COPY,
    )(*[pltpu.with_memory_space_constraint(a, pltpu.HBM) for a in operands])
    sems = [(res[2 * g], res[2 * g + 1]) for g in range(n_g)]
    arrays, k = [], 2 * n_g
    for grp in groups:
        arrays.append(list(res[k:k + len(grp)]))
        k += len(grp)
    return sems, arrays, list(res[k:])


def gather_wait(arrays, sems, after, name):
    n = len(arrays)

    def body(*refs):
        ins, send_sem, recv_sem = refs[:n], refs[n], refs[n + 1]
        x, y, c, _ = _place()
        for w in range(n):
            half = ins[w].at[0, 0]
            cp = pltpu.make_async_remote_copy(src_ref=half, dst_ref=half, send_sem=send_sem, recv_sem=recv_sem,
                                              device_id=(x, y, c), device_id_type=MESH)
            for _ in range(PEERS_PER_BLOCK):
                cp.wait_send()
            for _ in range(PEERS_PER_BLOCK):
                cp.wait_recv()

    return pl.pallas_call(
        body, name=name,
        out_shape=[pltpu.HBM(a.shape, a.dtype) for a in arrays],
        in_specs=[HBM_SPEC] * n + [SEM_SPEC, SEM_SPEC, pl.BlockSpec(memory_space=pl.ANY)],
        out_specs=[HBM_SPEC] * n,
        input_output_aliases={w: w for w in range(n)},
        compiler_params=SPLIT_COPY,
    )(*arrays, *sems, after)


def exchange_halves(gs):
    n = len(gs)

    def body(*refs):
        ins, theirs = refs[:n], refs[n:2 * n]
        send_sems, recv_sems = refs[2 * n:]
        x, y, c, _ = _place()
        sends = [pltpu.make_async_remote_copy(src_ref=ins[w].at[:, 1 - c], dst_ref=theirs[w],
                                              send_sem=send_sems.at[w], recv_sem=recv_sems.at[w],
                                              device_id=(x, y, 1 - c), device_id_type=MESH) for w in range(n)]
        for cp in sends:
            cp.start()
        for cp in sends:
            cp.wait()

    return pl.pallas_call(
        body, name="exchange_halves",
        out_shape=[jax.ShapeDtypeStruct((4,) + g.shape[2:], g.dtype) for g in gs],
        in_specs=[HBM_SPEC] * n, out_specs=[HBM_SPEC] * n,
        scratch_shapes=[pltpu.SemaphoreType.DMA((n,)), pltpu.SemaphoreType.DMA((n,))],
    )(*gs)


def scatter_blocks(ps):
    n = len(ps)

    def body(*refs):
        ins, outs = refs[:n], refs[n:2 * n]
        send_sems, recv_sems = refs[2 * n:]
        x, y, c, chips = _place()
        sends = [pltpu.make_async_remote_copy(src_ref=ins[w].at[2 * chip[0] + chip[1]], dst_ref=outs[w].at[j],
                                              send_sem=send_sems.at[3 * w + j], recv_sem=recv_sems.at[3 * w + j],
                                              device_id=(*chip, c), device_id_type=MESH)
                 for w in range(n) for j, chip in enumerate(chips)]
        for cp in sends:
            cp.start()
        for cp in sends:
            cp.wait()

    return pl.pallas_call(
        body, name="scatter_blocks",
        out_shape=[jax.ShapeDtypeStruct((3,) + p.shape[1:], p.dtype) for p in ps],
        in_specs=[HBM_SPEC] * n, out_specs=[HBM_SPEC] * n,
        scratch_shapes=[pltpu.SemaphoreType.DMA((3 * n,)), pltpu.SemaphoreType.DMA((3 * n,))],
    )(*ps)


def join_halves(qs):
    n = len(qs)

    def body(*refs):
        ins, outs = refs[:n], refs[n:2 * n]
        send_sems, recv_sems = refs[2 * n:]
        x, y, c, _ = _place()
        sends = [pltpu.make_async_remote_copy(src_ref=ins[w].at[c], dst_ref=outs[w].at[c], send_sem=send_sems.at[w],
                                              recv_sem=recv_sems.at[w], device_id=(x, y, 1 - c), device_id_type=MESH)
                 for w in range(n)]
        for cp in sends:
            cp.start()
        for w in range(n):
            other = outs[w].at[1 - c]
            pltpu.make_async_remote_copy(src_ref=other, dst_ref=other, send_sem=send_sems.at[w],
                                         recv_sem=recv_sems.at[w], device_id=(x, y, 1 - c),
                                         device_id_type=MESH).wait_recv()
        for cp in sends:
            cp.wait_send()

    return pl.pallas_call(
        body, name="join_halves",
        out_shape=[jax.ShapeDtypeStruct(q.shape, q.dtype) for q in qs],
        in_specs=[HBM_SPEC] * n, out_specs=[HBM_SPEC] * n,
        input_output_aliases={w: w for w in range(n)},
        scratch_shapes=[pltpu.SemaphoreType.DMA((n,)), pltpu.SemaphoreType.DMA((n,))],
    )(*qs)


def _row_tile(R, row_bytes):
    tm = R
    for t in (512, 256, 128, 64, 32, 16, 8):
        if R % t == 0:
            tm = t
            if t * row_bytes <= ROW_TILE_BUDGET:
                break
    return tm


def sum_siblings(g, theirs, place):
    _, _, R, C = g.shape
    tm = _row_tile(R, 3 * C * 4)

    def body(place_ref, a_ref, b_ref, o_ref):
        o_ref[...] = (a_ref[...] + b_ref[...]).astype(BF16)

    return pl.pallas_call(
        body, name="sum_siblings",
        grid_spec=pltpu.PrefetchScalarGridSpec(
            num_scalar_prefetch=1, grid=(4, R // tm),
            in_specs=[pl.BlockSpec((None, None, tm, C), lambda j, i, s: (j, s[1], i, 0)),
                      pl.BlockSpec((None, tm, C), lambda j, i, s: (j, i, 0))],
            out_specs=pl.BlockSpec((None, tm, C), lambda j, i, s: (j, i, 0))),
        out_shape=jax.ShapeDtypeStruct((4, R, C), BF16),
        compiler_params=_params(("parallel", "parallel")),
    )(place, g, theirs)


def sum_chips(p, landed, place):
    _, R, C = p.shape
    tm = _row_tile(R, 5 * C * 4)

    def body(place_ref, p_ref, l0_ref, l1_ref, l2_ref, o_ref):
        o_ref[...] = ((p_ref[...].astype(F32) + l0_ref[...].astype(F32)) + l1_ref[...].astype(F32)
                      ) + l2_ref[...].astype(F32)

    return pl.pallas_call(
        body, name="sum_chips",
        grid_spec=pltpu.PrefetchScalarGridSpec(
            num_scalar_prefetch=1, grid=(R // tm,),
            in_specs=[pl.BlockSpec((None, tm, C), lambda i, s: (s[0], i, 0))]
            + [pl.BlockSpec((None, tm, C), lambda i, s, j=j: (j, i, 0)) for j in range(3)],
            out_specs=pl.BlockSpec((None, tm, C), lambda i, s: (s[1], i, 0))),
        out_shape=jax.ShapeDtypeStruct((2, R, C), F32),
        compiler_params=_params(("parallel",)),
    )(place, p, landed, landed, landed)


def sum_blocks(items, name):
    R, C = items[0][0].shape[1:]
    tm = _row_tile(R, C * 4 * (len(items) + 1))
    n = len(items)

    def body(*refs):
        acc = refs[0][...].astype(F32)
        for r in refs[1:n]:
            acc = acc + r[...].astype(F32)
        refs[n][...] = acc

    return pl.pallas_call(
        body, name=name,
        grid=(R // tm,),
        in_specs=[pl.BlockSpec((None, tm, C), lambda i, j=j: (j, i, 0)) for _, j in items],
        out_specs=pl.BlockSpec((tm, C), lambda i: (i, 0)),
        out_shape=jax.ShapeDtypeStruct((R, C), F32),
        compiler_params=_params(("parallel",)),
    )(*[a for a, _ in items])


def reduce_scatter_grads(gs, place):
    theirs = exchange_halves(gs)
    ps = [sum_siblings(g, t, place) for g, t in zip(gs, theirs)]
    landed = scatter_blocks(ps)
    qs = [sum_chips(p, l, place) for p, l in zip(ps, landed)]
    joined = join_halves(qs)
    return [j.reshape(2 * j.shape[1], j.shape[2]) for j in joined]


def adamw(w, g, m, v):
    shape = w.shape
    C = shape[-1]
    R = w.size // C

    def f(w, g, m, v):
        m = ADAM_B1 * m + (1.0 - ADAM_B1) * g
        v = ADAM_B2 * v + (1.0 - ADAM_B2) * (g * g)
        m_hat = m / (1.0 - ADAM_B1 ** ADAM_STEP)
        v_hat = v / (1.0 - ADAM_B2 ** ADAM_STEP)
        delta = -ADAM_LR * (m_hat / (jnp.sqrt(v_hat) + ADAM_EPS) + ADAM_WD * w)
        return delta, m, v

    d, nm, nv = rowwise(f, [a.reshape(R, C) for a in (w, g, m, v)], [], [(C, F32)] * 3, [], "adamw")
    return d.reshape(shape), nm.reshape(shape), nv.reshape(shape)


def _cast_into_slot(w, place):
    C = w.shape[-1]
    w2 = w.reshape(-1, C)
    R = w2.shape[0]
    tm = _row_tile(R, 6 * C)

    def body(place_ref, w_ref, o_ref):
        o_ref[...] = w_ref[...].astype(BF16)

    out = pl.pallas_call(
        body, name="cast_bf16",
        grid_spec=pltpu.PrefetchScalarGridSpec(
            num_scalar_prefetch=1, grid=(R // tm,),
            in_specs=[pl.BlockSpec((tm, C), lambda i, s: (i, 0))],
            out_specs=pl.BlockSpec((None, tm, C), lambda i, s: (s[0], i, 0))),
        out_shape=jax.ShapeDtypeStruct((4, R, C), BF16),
        compiler_params=_params(("parallel",)),
    )(place, w2)
    return out.reshape(4, 2, R // 2, C)


def _pack(vs):
    flat = jnp.concatenate([v.reshape(-1) for v in vs])
    n = flat.shape[0]
    total = -(-n // F32_TILE) * F32_TILE
    return jnp.pad(flat, (0, total - n)).reshape(total // LANES, LANES)


def _unpack(flat, like):
    out, o = [], 0
    for shp in like:
        sz = 1
        for d in shp:
            sz *= d
        out.append(flat[o:o + sz].reshape(shp))
        o += sz
    return out


def _cols_to_blocks(g, n_chips=4):
    R, N = g.shape
    C = N // n_chips
    return g.reshape(R, n_chips, C).transpose(1, 0, 2).reshape(n_chips, 2, R // 2, C)


def _rows_to_blocks(g, n_chips=4):
    R, C = g.shape
    return g.reshape(n_chips, 2, R // n_chips // 2, C)


def kernel(x, c, positions, ada_w, ada_b, norm_g, ffn_w13, ffn_w2, conv_w_pw1, conv_b_pw1, conv_w_dw, conv_b_dw, conv_ln_g, conv_ln_b, conv_w_pw2, conv_b_pw2, kv_ada_w, kv_ada_b, kv_norm_g, w_kv_a, kv_a_norm_g, w_kv_b, w_q_a, q_a_norm_g, w_q_b, w_o, final_norm_g, loss_target, m_ada_w, m_ada_b, m_norm_g, m_ffn_w13, m_ffn_w2, m_conv_w_pw1, m_conv_b_pw1, m_conv_w_dw, m_conv_b_dw, m_conv_ln_g, m_conv_ln_b, m_conv_w_pw2, m_conv_b_pw2, m_kv_ada_w, m_kv_ada_b, m_kv_norm_g, m_w_kv_a, m_kv_a_norm_g, m_w_kv_b, m_w_q_a, m_q_a_norm_g, m_w_q_b, m_w_o, m_final_norm_g, v_ada_w, v_ada_b, v_norm_g, v_ffn_w13, v_ffn_w2, v_conv_w_pw1, v_conv_b_pw1, v_conv_w_dw, v_conv_b_dw, v_conv_ln_g, v_conv_ln_b, v_conv_w_pw2, v_conv_b_pw2, v_kv_ada_w, v_kv_ada_b, v_kv_norm_g, v_w_kv_a, v_kv_a_norm_g, v_w_kv_b, v_w_q_a, v_q_a_norm_g, v_w_q_b, v_w_o, v_final_norm_g):
    S, D = x.shape[1], x.shape[2]
    H = N_HEADS
    F = ffn_w2.shape[2] * 4
    xi, yi, ci = lax.axis_index("x"), lax.axis_index("y"), lax.axis_index("c")
    chip = 2 * xi + yi
    dev = 2 * chip + ci
    place = jnp.stack([chip, ci]).astype(jnp.int32)
    h0 = x[0]
    target = loss_target[0]

    silu_c = rowwise(lambda a: a * _sigmoid(a), [c], [], [(D, F32)], [], "silu_c")[0]
    silu_all = all_gather8(silu_c.reshape(8, D // 8)).reshape(8, D)
    n_ada = ada_w.shape[2]
    n_kv = kv_ada_w.shape[1]
    ada_b_mine = lax.dynamic_slice_in_dim(ada_b, chip * n_ada, n_ada, axis=1)
    kv_b_mine = lax.dynamic_slice_in_dim(kv_ada_b, chip * n_kv, n_kv, axis=0)[None, :]
    mods = [mm(silu_all, ada_w[l], "nn", "ada_rows", bias=ada_b_mine[l:l + 1]) for l in range(2)]
    mods.append(mm(silu_all, kv_ada_w, "nn", "kv_ada_rows", bias=kv_b_mine))
    n_mod_cols = 2 * n_ada + n_kv
    mod_pack = jnp.concatenate(mods, axis=1).reshape(-1, LANES)
    mod_all = all_gather8(mod_pack).reshape(8, 8, n_mod_cols)[0::2]
    mod_mine = lax.dynamic_index_in_dim(mod_all, dev, axis=1, keepdims=False)
    mod = [mod_mine[:, l * n_ada:(l + 1) * n_ada].reshape(N_MOD, D) for l in range(2)]
    kv_mod = mod_mine[:, 2 * n_ada:].reshape(2, D)
    kv_shift, kv_scale = kv_mod[0:1], kv_mod[1:2]

    def mrow(l, k):
        return mod[l][k:k + 1]

    def slot(w):
        return _cast_into_slot(w, place)
    first = gather_weights([slot(ffn_w13[0, 0]), slot(ffn_w2[0, 0])])
    groups = [[slot(conv_w_pw1), slot(conv_w_pw2)],
              [slot(ffn_w13[0, 1]), slot(ffn_w2[0, 1])],
              [slot(w_kv_a), slot(w_kv_b), slot(ffn_w13[1, 0]), slot(ffn_w2[1, 0]), slot(w_q_a), slot(w_q_b), slot(w_o),
               slot(ffn_w13[1, 1]), slot(ffn_w2[1, 1])]]
    sems, started, first = gather_start(groups, first)

    def ffn_weights(w13_blocks, w2_blocks):
        return w13_blocks.reshape(4, 1, 1, D, F // 2), w2_blocks.reshape(F, D)
    small_like = [norm_g.shape, conv_b_pw1.shape, conv_w_dw.shape, conv_b_dw.shape, conv_ln_g.shape,
                  conv_ln_b.shape, conv_b_pw2.shape]
    small_pack = _pack([norm_g, conv_b_pw1, conv_w_dw, conv_b_dw, conv_ln_g, conv_ln_b, conv_b_pw2])
    small_all = all_gather8(small_pack)[0::2].reshape(4, -1)
    per_chip = [_unpack(small_all[j], small_like) for j in range(4)]
    smalls = [jnp.concatenate([per_chip[j][k] for j in range(4)], axis=-1) for k in range(len(small_like))]
    norm_g_f, b_pw1_f, w_dw_f, b_dw_f, ln_g_f, ln_b_f, b_pw2_f = smalls

    tabs = rope_tables(positions[0])

    def ng(l, k):
        return norm_g_f[l, k][None, :]

    h = h0
    ffn00 = ffn_weights(*first)
    h, s_f1_0 = ffn_fwd(h, ng(0, 0), mrow(0, 0), mrow(0, 1), mrow(0, 2), ffn00[0], 0, 0, ffn00[1])
    g_pw1, g_pw2 = gather_wait(started[0], sems[0], h, "gather_wait_conv")
    conv_p = dict(
        w_pw1=g_pw1.reshape(4, D, 2 * D // 4).transpose(1, 0, 2).reshape(D, 2 * D),
        b_pw1=b_pw1_f, w_dw=w_dw_f[0], b_dw=b_dw_f, ln_g=ln_g_f, ln_b=ln_b_f,
        w_pw2=g_pw2.reshape(D, D), b_pw2=b_pw2_f)
    h, s_conv = conv_module_fwd(h, ng(0, 1), mrow(0, 3), mrow(0, 4), mrow(0, 5), conv_p)
    ffn01 = ffn_weights(*gather_wait(started[1], sems[1], h, "gather_wait_ffn"))
    h, s_f2_0 = ffn_fwd(h, ng(0, 2), mrow(0, 6), mrow(0, 7), mrow(0, 8), ffn01[0], 0, 0, ffn01[1])
    (g_kv_a, g_kv_b, g_w13_10, g_w2_10, g_q_a, g_q_b, g_w_o, g_w13_11, g_w2_11) = gather_wait(
        started[2], sems[2], h, "gather_wait_layer1")
    ffn10, ffn11 = ffn_weights(g_w13_10, g_w2_10), ffn_weights(g_w13_11, g_w2_11)
    q_lora = w_q_a.shape[2]
    pw = mla_pad_weights(dict(
        w_kv_a=g_kv_a.reshape(D, KV_LORA + QK_ROPE),
        w_kv_b=g_kv_b.reshape(4, KV_LORA, -1).transpose(1, 0, 2).reshape(KV_LORA, -1),
        w_q_a=g_q_a.reshape(D, q_lora),
        w_q_b=g_q_b.reshape(4, q_lora, -1).transpose(1, 0, 2).reshape(q_lora, -1),
        w_o=g_w_o.reshape(H * V_HEAD, D)))
    kv_norm = kv_norm_g[None, :]
    kv_a_g = kv_a_norm_g[None, :]
    kv, kpe_r, s_kv = mla_kv_fwd(h, kv_norm, kv_shift, kv_scale, kv_a_g, pw, tabs)
    h, s_f1_1 = ffn_fwd(h, ng(1, 0), mrow(1, 0), mrow(1, 1), mrow(1, 2), ffn10[0], 0, 0, ffn10[1])
    h, s_mla = mla_fwd(h, ng(1, 1), mrow(1, 3), mrow(1, 4), mrow(1, 5), q_a_norm_g, pw, kv, kpe_r, tabs)
    h, s_f2_1 = ffn_fwd(h, ng(1, 2), mrow(1, 6), mrow(1, 7), mrow(1, 8), ffn11[0], 0, 0, ffn11[1])
    dh, loss_cols, d_final_g = loss_head(h, target, final_norm_g[None, :])

    dh, v_f2_1, dw13_11, dw2_11 = ffn_bwd(dh, s_f2_1, ng(1, 2), mrow(1, 7), mrow(1, 8), ffn11[0], 0, 0, ffn11[1])
    dh, v_mla, g_mla, dk, dv = mla_bwd(dh, s_mla, ng(1, 1), mrow(1, 4), mrow(1, 5), q_a_norm_g, pw, kv, kpe_r, tabs)
    dh, v_f1_1, dw13_10, dw2_10 = ffn_bwd(dh, s_f1_1, ng(1, 0), mrow(1, 1), mrow(1, 2), ffn10[0], 0, 0, ffn10[1])
    dh, v_kv, d_kv_a_g, d_w_kv_a, d_w_kv_b = mla_kv_bwd(dh, dk, dv, s_kv, kv_norm, kv_scale, kv_a_g, pw, tabs)
    dh, v_f2_0, dw13_01, dw2_01 = ffn_bwd(dh, s_f2_0, ng(0, 2), mrow(0, 7), mrow(0, 8), ffn01[0], 0, 0, ffn01[1])
    dh, v_conv, g_conv = conv_module_bwd(dh, s_conv, ng(0, 1), mrow(0, 4), mrow(0, 5), conv_p)
    dh, v_f1_0, dw13_00, dw2_00 = ffn_bwd(dh, s_f1_0, ng(0, 0), mrow(0, 1), mrow(0, 2), ffn00[0], 0, 0, ffn00[1])
    grad_x = dh[None]

    d_w_kv_a_u = jnp.concatenate([d_w_kv_a[:, :KV_LORA], d_w_kv_a[:, KV_LORA + QK_NOPE:KV_LORA + QK_NOPE + QK_ROPE]],
                                 axis=1)
    hk = H * HEAD_PAD
    dkb = jnp.concatenate([d_w_kv_b[:, :hk].reshape(KV_LORA, H, HEAD_PAD)[:, :, :QK_NOPE],
                           d_w_kv_b[:, hk:].reshape(KV_LORA, H, HEAD_PAD)[:, :, :V_HEAD]], axis=2).reshape(KV_LORA, -1)
    d_w_q_b_u = _unpad_heads(g_mla["w_q_b"], QK_NOPE + QK_ROPE)
    d_w_o_u = g_mla["w_o"].reshape(H, HEAD_PAD, D)[:, :V_HEAD].reshape(H * V_HEAD, D)
    full = [dw.reshape(4, 2, D // 2, F // 2) for dw in (dw13_00, dw13_01, dw13_10, dw13_11)] + [
            _rows_to_blocks(dw2_00), _rows_to_blocks(dw2_01), _rows_to_blocks(dw2_10), _rows_to_blocks(dw2_11),
            _cols_to_blocks(g_conv["w_pw1"]), _rows_to_blocks(g_conv["w_pw2"]), _rows_to_blocks(d_w_kv_a_u),
            _cols_to_blocks(dkb), _rows_to_blocks(g_mla["w_q_a"]), _cols_to_blocks(d_w_q_b_u),
            _rows_to_blocks(d_w_o_u)]
    red = reduce_scatter_grads(full, place)
    g_ffn_w13 = jnp.stack(red[0:4]).reshape(ffn_w13.shape)
    g_ffn_w2 = jnp.stack(red[4:8]).reshape(ffn_w2.shape)
    g_conv_w_pw1 = red[8].reshape(conv_w_pw1.shape)
    g_conv_w_pw2 = red[9].reshape(conv_w_pw2.shape)
    g_w_kv_a = red[10].reshape(w_kv_a.shape)
    g_w_kv_b = red[11].reshape(w_kv_b.shape)
    g_w_q_a = red[12].reshape(w_q_a.shape)
    g_w_q_b = red[13].reshape(w_q_b.shape)
    g_w_o = red[14].reshape(w_o.shape)

    def dmod(v1, vm, v2):
        return jnp.concatenate([v1[0], v1[1], v1[2], vm[0], vm[1], vm[2], v2[0], v2[1], v2[2]], axis=1)
    d_mod0 = dmod(v_f1_0, v_conv, v_f2_0)
    d_mod1 = dmod(v_f1_1, v_mla, v_f2_1)
    d_kv_mod = jnp.concatenate([v_kv[0], v_kv[1]], axis=1)
    d_norm_g = jnp.concatenate([v_f1_0[3], v_conv[3], v_f2_0[3], v_f1_1[3], v_mla[3], v_f2_1[3]], axis=0)
    vec_list = [d_mod0, d_mod1, d_kv_mod, d_norm_g, g_conv["b_pw1"], g_conv["w_dw"], g_conv["b_dw"], g_conv["ln_g"],
                g_conv["ln_b"], g_conv["b_pw2"], v_kv[2], d_kv_a_g, g_mla["q_a_norm_g"], d_final_g, loss_cols]
    vec_like = [v.shape for v in vec_list]
    vec_pack = _pack(vec_list)
    n_mod_rows = (2 * N_MOD * D + 2 * D) // LANES
    vec_all = all_gather8(vec_pack)
    vec_sum = sum_blocks([(vec_all, d) for d in range(8)], "sum_devices").reshape(-1)
    (_, _, _, s_norm_g, s_b_pw1, s_w_dw, s_b_dw, s_ln_g, s_ln_b, s_b_pw2, s_kv_norm_g, s_kv_a_g, s_q_a_g,
     s_final_g, s_loss) = _unpack(vec_sum, vec_like)
    loss = jnp.sum(s_loss)
    dmod_all = vec_all[:, :n_mod_rows].reshape(8, 2 * N_MOD * D + 2 * D)
    dmod_sum = vec_sum[:2 * N_MOD * D + 2 * D]
    g_ada_b = dmod_sum[:2 * N_MOD * D].reshape(2, N_MOD * D)
    g_kv_ada_b = dmod_sum[2 * N_MOD * D:]
    g_ada_w = []
    for l in range(2):
        cols = lax.dynamic_slice_in_dim(dmod_all[:, l * N_MOD * D:(l + 1) * N_MOD * D], chip * n_ada, n_ada, axis=1)
        g_ada_w.append(mm(silu_all, cols, "tn", "ada_w_grad"))
    g_ada_w = jnp.stack(g_ada_w)
    kv_cols = lax.dynamic_slice_in_dim(dmod_all[:, 2 * N_MOD * D:], chip * n_kv, n_kv, axis=1)
    g_kv_ada_w = mm(silu_all, kv_cols, "tn", "kv_ada_w_grad")

    def shard(v, width):
        return lax.dynamic_slice_in_dim(v, chip * width, width, axis=v.ndim - 1)

    Dq = D // 4
    g_norm_g = shard(s_norm_g.reshape(2, 3, D), Dq)
    g_conv_b_pw1 = shard(s_b_pw1, 2 * D // 4)
    g_conv_w_dw = shard(s_w_dw, Dq)[None]
    g_conv_b_dw = shard(s_b_dw, Dq)
    g_conv_ln_g = shard(s_ln_g, Dq)
    g_conv_ln_b = shard(s_ln_b, Dq)
    g_conv_b_pw2 = shard(s_b_pw2, Dq)

    grads = [g_ada_w, g_ada_b, g_norm_g, g_ffn_w13, g_ffn_w2, g_conv_w_pw1, g_conv_b_pw1, g_conv_w_dw, g_conv_b_dw,
             g_conv_ln_g, g_conv_ln_b, g_conv_w_pw2, g_conv_b_pw2, g_kv_ada_w, g_kv_ada_b, s_kv_norm_g[0], g_w_kv_a,
             s_kv_a_g[0], g_w_kv_b, g_w_q_a, s_q_a_g, g_w_q_b, g_w_o, s_final_g[0]]
    weights = [ada_w, ada_b, norm_g, ffn_w13, ffn_w2, conv_w_pw1, conv_b_pw1, conv_w_dw, conv_b_dw, conv_ln_g,
               conv_ln_b, conv_w_pw2, conv_b_pw2, kv_ada_w, kv_ada_b, kv_norm_g, w_kv_a, kv_a_norm_g, w_kv_b, w_q_a,
               q_a_norm_g, w_q_b, w_o, final_norm_g]
    ms = [m_ada_w, m_ada_b, m_norm_g, m_ffn_w13, m_ffn_w2, m_conv_w_pw1, m_conv_b_pw1, m_conv_w_dw, m_conv_b_dw,
          m_conv_ln_g, m_conv_ln_b, m_conv_w_pw2, m_conv_b_pw2, m_kv_ada_w, m_kv_ada_b, m_kv_norm_g, m_w_kv_a,
          m_kv_a_norm_g, m_w_kv_b, m_w_q_a, m_q_a_norm_g, m_w_q_b, m_w_o, m_final_norm_g]
    vs = [v_ada_w, v_ada_b, v_norm_g, v_ffn_w13, v_ffn_w2, v_conv_w_pw1, v_conv_b_pw1, v_conv_w_dw, v_conv_b_dw,
          v_conv_ln_g, v_conv_ln_b, v_conv_w_pw2, v_conv_b_pw2, v_kv_ada_w, v_kv_ada_b, v_kv_norm_g, v_w_kv_a,
          v_kv_a_norm_g, v_w_kv_b, v_w_q_a, v_q_a_norm_g, v_w_q_b, v_w_o, v_final_norm_g]
    grads = [g.reshape(w.shape) for g, w in zip(grads, weights)]
    deltas, new_m, new_v = [], [], []
    for w, g, m, v in zip(weights, grads, ms, vs):
        d, nm, nv = adamw(w, g, m, v)
        deltas.append(d)
        new_m.append(nm)
        new_v.append(nv)
    return (loss, grad_x, *grads, *deltas, *new_m, *new_v)
```

```python
import jax
import jax.numpy as jnp
from jax import lax
from jax.experimental import pallas as pl
from jax.experimental.pallas import tpu as pltpu

F32 = jnp.float32
BF16 = jnp.bfloat16
MESH = pl.DeviceIdType.MESH

N_HEADS = 16
QK_NOPE = 64
QK_ROPE = 32
V_HEAD = 64
KV_LORA = 256
CONV_WIDTH = 31
ROPE_THETA = 10000.0
EPS = 1e-6
N_MOD = 9
HEAD_PAD = 128
ATTN_TILE = 512
CONV_HALO = 32

ADAM_LR = 0.001
ADAM_B1 = 0.9
ADAM_B2 = 0.999
ADAM_EPS = 1e-08
ADAM_WD = 0.01
ADAM_STEP = 10

VMEM_LIMIT_BYTES = 56 * 2 ** 20
ROW_TILE_BUDGET = 10 * 2 ** 20
MM_VMEM_BUDGET = 40 * 2 ** 20
LANES = 128
F32_TILE = 8 * LANES
NEG = float(jnp.finfo(jnp.float32).min)
LOG2_E = 1.4426950408889634


def _tile(n, prefs):
    for t in prefs:
        if n % t == 0:
            return t
    return n


def _params(sem):
    return pltpu.CompilerParams(dimension_semantics=sem, vmem_limit_bytes=VMEM_LIMIT_BYTES)


def _mm_tiles(M, N, K, mode, a_bytes, b_bytes, o_bytes):
    if mode == "tn":
        tk_opts = [t for t in (2048, 1024, 512, 256, 128) if K % t == 0] or [K]
        tm_opts = ([M] if M <= 2816 else []) + [t for t in (1024, 512, 256, 128) if M % t == 0 and t < M]
    else:
        tk_opts = [K]
        tm_opts = [t for t in (1024, 512, 256, 128) if M % t == 0] or [M]
    tn_opts = [t for t in (1408, 1024, 512, 384, 256, 128) if N % t == 0] or [N]

    def need(tm, tn, tk):
        blocks = 2 * (tm * tk * a_bytes + tk * tn * b_bytes + tm * tn * o_bytes)
        return blocks + (tm * tn * 4 if mode == "tn" else 0)

    tk_floor = next((t for t in tk_opts if t <= 512), tk_opts[-1])
    for tm in tm_opts:
        for tn in tn_opts:
            if need(tm, tn, tk_floor) <= MM_VMEM_BUDGET:
                return tm, tn, next(tk for tk in tk_opts if need(tm, tn, tk) <= MM_VMEM_BUDGET)
    return tm_opts[-1], tn_opts[-1], tk_opts[-1]


def mm(a, b, mode, name, out_dtype=F32, bias=None):
    if mode == "nn":
        (M, K), (K2, N) = a.shape, b.shape
        dims = (((1,), (0,)), ((), ()))
    elif mode == "nt":
        (M, K), (N, K2) = a.shape, b.shape
        dims = (((1,), (1,)), ((), ()))
    else:
        (K, M), (K2, N) = a.shape, b.shape
        dims = (((0,), (0,)), ((), ()))
    assert K == K2, (a.shape, b.shape, mode)
    tm, tn, tk = _mm_tiles(M, N, K, mode, a.dtype.itemsize, b.dtype.itemsize, jnp.dtype(out_dtype).itemsize)
    nk = K // tk
    if mode == "tn":
        a_spec = pl.BlockSpec((tk, tm), lambda i, j, k: (k, i))
        b_spec = pl.BlockSpec((tk, tn), lambda i, j, k: (k, j))
    elif mode == "nn":
        a_spec = pl.BlockSpec((tm, tk), lambda i, j, k: (i, k))
        b_spec = pl.BlockSpec((tk, tn), lambda i, j, k: (k, j))
    else:
        a_spec = pl.BlockSpec((tm, tk), lambda i, j, k: (i, k))
        b_spec = pl.BlockSpec((tn, tk), lambda i, j, k: (j, k))
    in_specs = [a_spec, b_spec]
    operands = [a, b]
    if bias is not None:
        in_specs.append(pl.BlockSpec((1, tn), lambda i, j, k: (0, j)))
        operands.append(bias)
    has_bias = bias is not None

    def body(*refs):
        a_ref, b_ref = refs[0], refs[1]
        bias_ref = refs[2] if has_bias else None
        o_ref = refs[3] if has_bias else refs[2]
        prod = lax.dot_general(a_ref[...].astype(BF16), b_ref[...].astype(BF16), dims,
                               preferred_element_type=F32)
        if nk == 1:
            if has_bias:
                prod = prod + bias_ref[...]
            o_ref[...] = prod.astype(o_ref.dtype)
        else:
            acc_ref = refs[-1]
            k = pl.program_id(2)

            @pl.when(k == 0)
            def _():
                acc_ref[...] = jnp.zeros_like(acc_ref)

            acc_ref[...] += prod

            @pl.when(k == nk - 1)
            def _():
                out = acc_ref[...]
                if has_bias:
                    out = out + bias_ref[...]
                o_ref[...] = out.astype(o_ref.dtype)

    return pl.pallas_call(
        body, name=name,
        grid=(M // tm, N // tn, nk),
        in_specs=in_specs,
        out_specs=pl.BlockSpec((tm, tn), lambda i, j, k: (i, j)),
        out_shape=jax.ShapeDtypeStruct((M, N), out_dtype),
        scratch_shapes=[pltpu.VMEM((tm, tn), F32)] if nk > 1 else [],
        compiler_params=_params(("parallel", "parallel", "arbitrary")),
    )(*operands)


def mm_fused(a, b, mode, name, tn, epi, epi_outs, pro=None, pro_rows=(), pro_vecs=(), pro_out=False, n_pro_sums=0,
             epi_rows=(), epi_vecs=(), b_blocks=None, n_cols=None):
    M, K = a.shape
    if b_blocks is not None:
        n_b, N = len(b_blocks), n_cols
    else:
        n_b = b.shape[0] if b.ndim == 3 else 1
        N = b.shape[-1] if mode == "nn" else b.shape[0]
    dims = (((1,), (0,)), ((), ())) if mode == "nn" else (((1,), (1,)), ((), ()))
    nj = N // tn
    epi_outs = [o if len(o) == 3 else (*o, None) for o in epi_outs]
    row_bytes = 2 * (K * a.dtype.itemsize + sum(K * r.dtype.itemsize for r in pro_rows) + (2 * K if pro_out else 0)
                     + sum(w * r.dtype.itemsize * (r.shape[0] if r.ndim == 3 else 1) for r, w in epi_rows)
                     + sum(w * jnp.dtype(dt).itemsize * (L or 1) for w, dt, L in epi_outs)
                     ) + (2 * K if pro is not None else 0)
    fixed = 2 * n_b * K * tn * b.dtype.itemsize
    tm = next((t for t in (1024, 512, 256, 128) if M % t == 0 and t * row_bytes + fixed <= MM_VMEM_BUDGET), M)
    row = lambda i, j: (i, 0)
    tile = lambda i, j: (i, j)
    stack = lambda i, j: (0, i, j)
    in_specs = [pl.BlockSpec((tm, K), row)] + [pl.BlockSpec((tm, K), row) for _ in pro_rows]
    in_specs += [pl.BlockSpec(v.shape, lambda i, j: (0, 0)) for v in pro_vecs]
    if b_blocks is not None:
        in_specs += [pl.BlockSpec(shape, imap) for shape, imap in b_blocks]
    elif b.ndim == 3:
        in_specs += [pl.BlockSpec((None, K, tn), lambda i, j, h=h: (h, 0, j)) for h in range(n_b)]
    elif mode == "nn":
        in_specs += [pl.BlockSpec((K, tn), lambda i, j: (0, j))]
    else:
        in_specs += [pl.BlockSpec((tn, K), lambda i, j: (j, 0))]
    in_specs += [pl.BlockSpec((r.shape[0], tm, w), stack) if r.ndim == 3 else pl.BlockSpec((tm, w), tile)
                 for r, w in epi_rows]
    in_specs += [pl.BlockSpec((1, tn), lambda i, j: (0, j)) for _ in epi_vecs]
    out_specs, out_shape = [], []
    if pro_out:
        out_specs.append(pl.BlockSpec((tm, K), row))
        out_shape.append(jax.ShapeDtypeStruct((M, K), BF16))
    for _ in range(n_pro_sums):
        out_specs.append(pl.BlockSpec((1, K), lambda i, j: (0, 0)))
        out_shape.append(jax.ShapeDtypeStruct((1, K), F32))
    for w, dt, L in epi_outs:
        out_specs.append(pl.BlockSpec((tm, w), tile) if L is None else pl.BlockSpec((L, tm, w), stack))
        out_shape.append(jax.ShapeDtypeStruct((M, nj * w) if L is None else (L, M, nj * w), dt))
    n_pr, n_pv, n_er, n_ev = len(pro_rows), len(pro_vecs), len(epi_rows), len(epi_vecs)
    n_a = 1 + n_pr + n_pv
    n_in = n_a + n_b + n_er + n_ev
    n_po = 1 if pro_out else 0

    def body(*refs):
        i, j = pl.program_id(0), pl.program_id(1)
        a_ref = refs[0]
        outs = refs[n_in:]
        if pro is not None:
            lhs_ref = refs[-1]

            @pl.when(j == 0)
            def _():
                res = pro(*[r[...] for r in refs[:1 + n_pr + n_pv]])
                if not isinstance(res, (tuple, list)):
                    res = (res,)
                lhs_ref[...] = res[0]
                if pro_out:
                    outs[0][...] = res[0]
                for s_ref, val in zip(outs[n_po:n_po + n_pro_sums], res[1:]):
                    part = jnp.sum(val.astype(F32), axis=0, keepdims=True)

                    @pl.when(i == 0)
                    def _(s_ref=s_ref, part=part):
                        s_ref[...] = part

                    @pl.when(i != 0)
                    def _(s_ref=s_ref, part=part):
                        s_ref[...] += part

            lhs = lhs_ref[...]
        else:
            lhs = a_ref[...].astype(BF16)
        accs = [lax.dot_general(lhs, b_ref[...].astype(BF16), dims, preferred_element_type=F32)
                for b_ref in refs[n_a:n_a + n_b]]
        res = epi(*accs, *[r[...] for r in refs[n_a + n_b:n_in]])
        if not isinstance(res, (tuple, list)):
            res = (res,)
        for o_ref, val in zip(outs[n_po + n_pro_sums:], res):
            if isinstance(val, (tuple, list)):
                for h, part in enumerate(val):
                    o_ref[h] = part.astype(o_ref.dtype)
            else:
                o_ref[...] = val.astype(o_ref.dtype)

    return pl.pallas_call(
        body, name=name,
        grid=(M // tm, nj),
        in_specs=in_specs, out_specs=out_specs, out_shape=out_shape,
        scratch_shapes=[pltpu.VMEM((tm, K), BF16)] if pro is not None else [],
        compiler_params=_params(("arbitrary", "arbitrary")),
    )(a, *pro_rows, *pro_vecs, *([b] * n_b), *[r for r, _ in epi_rows], *epi_vecs)


def rowwise(fn, rows, vecs, outs, sums, name, tm=None):
    norm = [(r, r.shape[1], 0) if not isinstance(r, tuple) else r for r in rows]
    S = norm[0][0].shape[0]
    if tm is None:
        tm = _row_tile(S, sum(w * r.dtype.itemsize for r, w, _ in norm)
                       + sum(n * jnp.dtype(dt).itemsize for n, dt in outs))
    n_rows, n_vecs, n_outs, n_sums = len(norm), len(vecs), len(outs), len(sums)
    in_specs = [pl.BlockSpec((tm, w), lambda i, cb=cb: (i, cb)) for _, w, cb in norm]
    in_specs += [pl.BlockSpec(v.shape, lambda i: (0, 0)) for v in vecs]
    out_specs = [pl.BlockSpec((tm, n), lambda i: (i, 0)) for n, _ in outs]
    out_specs += [pl.BlockSpec((1, n), lambda i: (0, 0)) for n in sums]
    out_shape = [jax.ShapeDtypeStruct((S, n), dt) for n, dt in outs]
    out_shape += [jax.ShapeDtypeStruct((1, n), F32) for n in sums]

    def body(*refs):
        ins = [r[...] for r in refs[:n_rows + n_vecs]]
        res = fn(*ins)
        if not isinstance(res, (tuple, list)):
            res = (res,)
        out_refs = refs[n_rows + n_vecs:]
        for o_ref, val in zip(out_refs[:n_outs], res[:n_outs]):
            o_ref[...] = val.astype(o_ref.dtype)
        if n_sums:
            i = pl.program_id(0)
            for s_ref, val in zip(out_refs[n_outs:], res[n_outs:]):
                part = jnp.sum(val.astype(F32), axis=0, keepdims=True)

                @pl.when(i == 0)
                def _(s_ref=s_ref, part=part):
                    s_ref[...] = part

                @pl.when(i != 0)
                def _(s_ref=s_ref, part=part):
                    s_ref[...] += part

    res = pl.pallas_call(
        body, name=name,
        grid=(S // tm,),
        in_specs=in_specs, out_specs=out_specs, out_shape=out_shape,
        compiler_params=_params(("arbitrary",) if n_sums else ("parallel",)),
    )(*[r for r, _, _ in norm], *vecs)
    return res


def _sigmoid(x):
    return jax.nn.sigmoid(x)


def _rms(x):
    r = lax.rsqrt(jnp.mean(x * x, axis=-1, keepdims=True) + EPS)
    return x * r, r


def _rms_bwd(xhat, r, dxhat):
    return r * (dxhat - xhat * jnp.mean(dxhat * xhat, axis=-1, keepdims=True))


def norm_mod(h, g, sh, sc, name):
    def f(h, g, sh, sc):
        xhat, _ = _rms(h)
        return ((xhat * g) * (1 + sc) + sh).astype(BF16)
    return rowwise(f, [h], [g, sh, sc], [(h.shape[1], BF16)], [], name)[0]


def norm_mod_bwd(h, dhn, dh_out, g, sc, name):
    D = h.shape[1]

    def f(h, dhn, dres, g, sc):
        xhat, r = _rms(h)
        dxn = dhn * (1 + sc)
        return _rms_bwd(xhat, r, dxn * g) + dres, dhn, dhn * (xhat * g), dxn * xhat

    return rowwise(f, [h, dhn, dh_out], [g, sc], [(D, F32)], [D, D, D], name)


def residual(h, y, gate, coef, name, bias=None):
    D = h.shape[1]
    if bias is None:
        def f(h, y, gate):
            return h + (coef * gate) * y
        return rowwise(f, [h, y], [gate], [(D, F32)], [], name)[0], y

    def fb(h, y, gate, bias):
        yb = y + bias
        return h + (coef * gate) * yb, yb
    return rowwise(fb, [h, y], [gate, bias], [(D, F32), (D, F32)], [], name)


def residual_bwd(dh_out, y, gate, coef, name, with_bias_sum=False):
    D = y.shape[1]

    def f(dh, y, gate):
        dy = (coef * gate) * dh
        res = (dy.astype(BF16), coef * dh * y)
        return res + ((dy,) if with_bias_sum else ())
    return rowwise(f, [dh_out, y], [gate], [(D, BF16)], [D, D] if with_bias_sum else [D], name)


def ffn_w13_dx(dab, gw13, l, i):
    _, S, F = dab.shape
    D, C = gw13.shape[3:]
    tm = _tile(S, (1024, 512, 256, 128))
    nt = (((1,), (1,)), ((), ()))

    def body(a_ref, b_ref, o_ref, acc_ref):
        k = pl.program_id(1)
        prod = lax.dot_general(a_ref[...], b_ref[...], nt, preferred_element_type=F32)

        @pl.when(k == 0)
        def _():
            acc_ref[...] = prod

        @pl.when((k > 0) & (k < 3))
        def _():
            acc_ref[...] += prod

        @pl.when(k == 3)
        def _():
            o_ref[...] = acc_ref[...] + prod

    return pl.pallas_call(
        body, name="ffn_w13_dx",
        grid=(S // tm, 4),
        in_specs=[pl.BlockSpec((None, tm, C), lambda r, k: (k // 2, r, k % 2)),
                  pl.BlockSpec((None, None, None, D, C), lambda r, k: (k, l, i, 0, 0))],
        out_specs=pl.BlockSpec((tm, D), lambda r, k: (r, 0)),
        out_shape=jax.ShapeDtypeStruct((S, D), F32),
        scratch_shapes=[pltpu.VMEM((tm, D), F32)],
        compiler_params=_params(("parallel", "arbitrary")),
    )(dab, gw13)


def ffn_w13_grad(hn, dab):
    S, D = hn.shape
    F = dab.shape[2]
    C = F // 2
    tk = next(t for t in (2048, 1024, 512, 256, 128) if S % t == 0)
    tn_dims = (((0,), (0,)), ((), ()))
    nk = S // tk

    def body(a_ref, b_ref, o_ref, acc_ref):
        k = pl.program_id(1)

        @pl.when(k == 0)
        def _():
            acc_ref[...] = jnp.zeros_like(acc_ref)

        acc_ref[...] += lax.dot_general(a_ref[...], b_ref[...], tn_dims, preferred_element_type=F32)

        @pl.when(k == nk - 1)
        def _():
            o_ref[...] = acc_ref[...]

    return pl.pallas_call(
        body, name="ffn_w13_dw",
        grid=(4, nk),
        in_specs=[pl.BlockSpec((tk, D), lambda j, k: (k, 0)),
                  pl.BlockSpec((None, tk, C), lambda j, k: (j // 2, k, j % 2))],
        out_specs=pl.BlockSpec((None, D, C), lambda j, k: (j, 0, 0)),
        out_shape=jax.ShapeDtypeStruct((4, D, C), F32),
        scratch_shapes=[pltpu.VMEM((D, C), F32)],
        compiler_params=_params(("parallel", "arbitrary")),
    )(hn, dab)


def ffn_fwd(h, g, sh, sc, gate, gw13, l, i, w2):
    F, D = w2.shape
    C = F // 2

    def norm(h, g, sh, sc):
        xhat, _ = _rms(h)
        return ((xhat * g) * (1 + sc) + sh).astype(BF16)

    def act(a, b):
        sig = _sigmoid(a)
        sa = a * sig
        return (b * (sig + sa * (1 - sig)), sa), sa * b
    blocks = [((None, None, None, D, C), lambda r, j, half=half: (2 * half + j, l, i, 0, 0)) for half in range(2)]
    hn, dt_dab, t = mm_fused(h, gw13, "nn", "ffn_w13", C, act, [(C, BF16, 2), (C, BF16)],
                             pro=norm, pro_vecs=[g, sh, sc], pro_out=True, b_blocks=blocks, n_cols=F)

    def res(acc, h, gate):
        return h + (0.5 * gate) * acc, acc
    h_out, y = mm_fused(t, w2, "nn", "ffn_w2", D, res, [(D, F32), (D, F32)], epi_rows=[(h, D)], epi_vecs=[gate])
    return h_out, (h, hn, dt_dab, t, y)


def ffn_bwd(dh_out, saved, g, sc, gate, gw13, l, i, w2):
    h, hn, dt_dab, t, y = saved
    F, D = w2.shape
    C = F // 2

    def scale(dh, y, gate):
        return ((0.5 * gate) * dh).astype(BF16), 0.5 * dh * y

    def act_bwd(dt, f):
        return ((dt * f[0].astype(F32), dt * f[1].astype(F32)),)
    dy, d_gate, dab = mm_fused(dh_out, w2, "nt", "ffn_w2_dx", C, act_bwd, [(C, BF16, 2)],
                               pro=scale, pro_rows=[y], pro_vecs=[gate], pro_out=True, n_pro_sums=1,
                               epi_rows=[(dt_dab, C)])
    dw2 = mm(t, dy, "tn", "ffn_w2_dw")
    dw13 = ffn_w13_grad(hn, dab)
    dhn = ffn_w13_dx(dab, gw13, l, i)
    dh_in, d_sh, d_sc, d_g = norm_mod_bwd(h, dhn, dh_out, g, sc, "norm_mod_bwd")
    return dh_in, (d_sh, d_sc, d_gate, d_g), dw13, dw2


def _shifted(xbuf, n):
    return [xbuf] + [pltpu.roll(xbuf, n - b, 0) for b in range(1, 8)]


def conv_fwd(u, w_dw, b_dw, ln_g, ln_b):
    S, D = u.shape
    tm = _tile(S, (256, 128))
    rc = 32
    first_tap = CONV_HALO - (CONV_WIDTH - 1)
    w = jnp.concatenate([w_dw, jnp.zeros((CONV_HALO - CONV_WIDTH, D), F32)], axis=0)

    def body(cur_ref, prev_ref, w_ref, b_ref, g_ref, beta_ref, z_ref, s_ref):
        i = pl.program_id(0)
        prev = jnp.where(i == 0, jnp.zeros((CONV_HALO, D), F32), prev_ref[...])
        xs = _shifted(jnp.concatenate([prev, cur_ref[...]], axis=0), tm + CONV_HALO)
        for c0 in range(0, tm, rc):
            acc = jnp.zeros((rc, D), F32)
            for k in range(CONV_WIDTH):
                off = first_tap + k
                a8, b = off // 8 * 8, off % 8
                acc = acc + w_ref[k:k + 1, :] * xs[b][c0 + a8:c0 + a8 + rc, :]
            z_ref[c0:c0 + rc, :] = acc + b_ref[...]
        z = z_ref[...]
        mu = jnp.mean(z, axis=-1, keepdims=True)
        zc = z - mu
        r = lax.rsqrt(jnp.mean(zc * zc, axis=-1, keepdims=True) + EPS)
        un = zc * r * g_ref[...] + beta_ref[...]
        s_ref[...] = (un * _sigmoid(un)).astype(BF16)

    nb = tm // CONV_HALO
    vec = pl.BlockSpec((1, D), lambda i: (0, 0))
    return pl.pallas_call(
        body, name="conv_fwd",
        grid=(S // tm,),
        in_specs=[pl.BlockSpec((tm, D), lambda i: (i, 0)),
                  pl.BlockSpec((CONV_HALO, D), lambda i: (jnp.maximum(i * nb - 1, 0), 0)),
                  pl.BlockSpec((CONV_HALO, D), lambda i: (0, 0)), vec, vec, vec],
        out_specs=[pl.BlockSpec((tm, D), lambda i: (i, 0)), pl.BlockSpec((tm, D), lambda i: (i, 0))],
        out_shape=[jax.ShapeDtypeStruct((S, D), F32), jax.ShapeDtypeStruct((S, D), BF16)],
        compiler_params=_params(("parallel",)),
    )(u, u, w, b_dw, ln_g, ln_b)


def conv_bwd(dz, u, w_dw):
    S, D = u.shape
    tm = _tile(S, (256, 128))
    rc = 32
    first_tap = CONV_HALO - (CONV_WIDTH - 1)
    w = jnp.concatenate([w_dw, jnp.zeros((CONV_HALO - CONV_WIDTH, D), F32)], axis=0)
    n_tiles = S // tm
    nb = tm // CONV_HALO

    def body(dz_ref, dzn_ref, u_ref, up_ref, w_ref, du_ref, dw_ref):
        i = pl.program_id(0)
        nxt = jnp.where(i == n_tiles - 1, jnp.zeros((CONV_HALO, D), F32), dzn_ref[...])
        dzs = _shifted(jnp.concatenate([dz_ref[...], nxt], axis=0), tm + CONV_HALO)
        for c0 in range(0, tm, rc):
            acc = jnp.zeros((rc, D), F32)
            for m in range(CONV_WIDTH):
                a8, b = m // 8 * 8, m % 8
                acc = acc + w_ref[CONV_WIDTH - 1 - m:CONV_WIDTH - m, :] * dzs[b][c0 + a8:c0 + a8 + rc, :]
            du_ref[c0:c0 + rc, :] = acc
        prev = jnp.where(i == 0, jnp.zeros((CONV_HALO, D), F32), up_ref[...])
        us = _shifted(jnp.concatenate([prev, u_ref[...]], axis=0), tm + CONV_HALO)
        dz = dz_ref[...]

        @pl.when(i == 0)
        def _():
            dw_ref[...] = jnp.zeros_like(dw_ref)

        for k in range(CONV_WIDTH):
            off = first_tap + k
            a8, b = off // 8 * 8, off % 8
            dw_ref[k:k + 1, :] += jnp.sum(dz * us[b][a8:a8 + tm, :], axis=0, keepdims=True)

    last_blk = S // CONV_HALO - 1
    du, dw = pl.pallas_call(
        body, name="conv_bwd",
        grid=(n_tiles,),
        in_specs=[pl.BlockSpec((tm, D), lambda i: (i, 0)),
                  pl.BlockSpec((CONV_HALO, D), lambda i: (jnp.minimum((i + 1) * nb, last_blk), 0)),
                  pl.BlockSpec((tm, D), lambda i: (i, 0)),
                  pl.BlockSpec((CONV_HALO, D), lambda i: (jnp.maximum(i * nb - 1, 0), 0)),
                  pl.BlockSpec((CONV_HALO, D), lambda i: (0, 0))],
        out_specs=[pl.BlockSpec((tm, D), lambda i: (i, 0)), pl.BlockSpec((CONV_HALO, D), lambda i: (0, 0))],
        out_shape=[jax.ShapeDtypeStruct((S, D), F32), jax.ShapeDtypeStruct((CONV_HALO, D), F32)],
        compiler_params=_params(("arbitrary",)),
    )(dz, dz, u, u, w)
    return du, dw[:CONV_WIDTH]


def conv_module_fwd(h, g, sh, sc, gate, p):
    D = h.shape[1]
    hn = norm_mod(h, g, sh, sc, "conv_norm_mod")
    pre = mm(hn, p["w_pw1"], "nn", "conv_pw1")
    ba, bg = p["b_pw1"][:, :D], p["b_pw1"][:, D:]

    def glu(a, gt, ba, bg):
        return (a + ba) * _sigmoid(gt + bg)
    u = rowwise(glu, [(pre, D, 0), (pre, D, 1)], [ba, bg], [(D, F32)], [], "conv_glu")[0]
    z, s = conv_fwd(u, p["w_dw"], p["b_dw"], p["ln_g"], p["ln_b"])
    yraw = mm(s, p["w_pw2"], "nn", "conv_pw2")
    h_out, y = residual(h, yraw, gate, 1.0, "conv_residual", bias=p["b_pw2"])
    return h_out, (h, hn, pre, u, z, s, y)


def conv_module_bwd(dh_out, saved, g, sc, gate, p):
    h, hn, pre, u, z, s, y = saved
    D = h.shape[1]
    dy, d_gate, d_b_pw2 = residual_bwd(dh_out, y, gate, 1.0, "conv_residual_bwd", with_bias_sum=True)
    d_w_pw2 = mm(s, dy, "tn", "conv_pw2_dw")
    ds = mm(dy, p["w_pw2"], "nt", "conv_pw2_dx")

    def ln_bwd(z, ds, g, beta):
        mu = jnp.mean(z, axis=-1, keepdims=True)
        zc = z - mu
        r = lax.rsqrt(jnp.mean(zc * zc, axis=-1, keepdims=True) + EPS)
        xhat = zc * r
        un = xhat * g + beta
        sig = _sigmoid(un)
        d_un = ds * (sig * (1 + un * (1 - sig)))
        dxhat = d_un * g
        dz = r * (dxhat - jnp.mean(dxhat, axis=-1, keepdims=True)
                  - xhat * jnp.mean(dxhat * xhat, axis=-1, keepdims=True))
        return dz, d_un * xhat, d_un, dz
    dz, d_ln_g, d_ln_b, d_b_dw = rowwise(ln_bwd, [z, ds], [p["ln_g"], p["ln_b"]], [(D, F32)], [D, D, D],
                                         "conv_ln_bwd")
    du, d_w_dw = conv_bwd(dz, u, p["w_dw"])
    ba, bg = p["b_pw1"][:, :D], p["b_pw1"][:, D:]

    def glu_bwd(a, gt, du, ba, bg):
        sg = _sigmoid(gt + bg)
        da = du * sg
        dg = du * (a + ba) * (sg * (1 - sg))
        dpre = jnp.concatenate([da, dg], axis=1)
        return dpre.astype(BF16), dpre
    dpre, d_b_pw1 = rowwise(glu_bwd, [(pre, D, 0), (pre, D, 1), du], [ba, bg], [(2 * D, BF16)], [2 * D],
                            "conv_glu_bwd")
    d_w_pw1 = mm(hn, dpre, "tn", "conv_pw1_dw")
    dhn = mm(dpre, p["w_pw1"], "nt", "conv_pw1_dx")
    dh_in, d_sh, d_sc, d_g = norm_mod_bwd(h, dhn, dh_out, g, sc, "norm_mod_bwd")
    grads = dict(w_pw1=d_w_pw1, b_pw1=d_b_pw1, w_dw=d_w_dw, b_dw=d_b_dw, ln_g=d_ln_g, ln_b=d_ln_b,
                 w_pw2=d_w_pw2, b_pw2=d_b_pw2)
    return dh_in, (d_sh, d_sc, d_gate, d_g), grads


def _rope(x, c, s1, s2):
    n = x.shape[1]
    return x * c + pltpu.roll(x, n - QK_ROPE // 2, 1) * s1 + pltpu.roll(x, QK_ROPE // 2, 1) * s2


def _rope_t(dy, c, s1, s2):
    n = dy.shape[1]
    return dy * c + pltpu.roll(dy * s1, QK_ROPE // 2, 1) + pltpu.roll(dy * s2, n - QK_ROPE // 2, 1)


def rope_tables(positions):
    inv_freq = ROPE_THETA ** (-jnp.arange(0, QK_ROPE, 2, dtype=F32) / QK_ROPE)
    ang = positions.astype(F32)[:, None] * inv_freq
    cos, sin = jnp.cos(ang), jnp.sin(ang)
    S = positions.shape[0]
    one = jnp.ones((S, QK_NOPE), F32)
    z16 = jnp.zeros((S, QK_ROPE // 2), F32)
    zn = jnp.zeros((S, QK_NOPE), F32)
    zt = jnp.zeros((S, HEAD_PAD - QK_NOPE - QK_ROPE), F32)
    c = jnp.concatenate([one, cos, cos, zt], axis=1)
    s1 = jnp.concatenate([zn, -sin, z16, zt], axis=1)
    s2 = jnp.concatenate([zn, z16, sin, zt], axis=1)
    return c, s1, s2


def attn_fwd(qr, kv, kpe, n_heads):
    S = qr.shape[0]
    H = n_heads
    tk = _tile(S, (ATTN_TILE,))
    nk = S // tk
    w = 2 if nk % 2 == 0 else 1
    tq = w * tk
    c2 = (QK_NOPE + QK_ROPE) ** -0.5 * LOG2_E
    nt = (((1,), (1,)), ((), ()))

    assert V_HEAD < HEAD_PAD
    ones_row = HEAD_PAD - 1

    def body(q_ref, k_ref, v_ref, kpe_ref, o_ref, lse_ref, kf_ref, vt_ref, m_ref, acc_ref):
        qi = pl.program_id(1)
        feature = lax.broadcasted_iota(jnp.int32, (HEAD_PAD, tk), 0)

        @pl.when(qi == 0)
        def _():
            kf_ref[...] = k_ref[...] + kpe_ref[...]
            for c in range(nk):
                vt = jnp.transpose(v_ref[c * tk:(c + 1) * tk, :].astype(F32))
                vt_ref[c] = jnp.where(feature == ones_row, 1.0, vt).astype(BF16)

        q = q_ref[...]
        m_ref[...] = jnp.full((1, tq), -jnp.inf, F32)
        acc_ref[...] = jnp.zeros((HEAD_PAD, tq), F32)

        def tile(j, first_visible):
            k = kf_ref[pl.ds(pl.multiple_of(j * tk, tk), tk), :]
            t = lax.dot_general(k, q, nt, preferred_element_type=F32) * c2
            if first_visible is not None:
                krow = lax.broadcasted_iota(jnp.int32, (tk, tq), 0)
                qcol = lax.broadcasted_iota(jnp.int32, (tk, tq), 1)
                t = jnp.where(krow + first_visible <= qcol, t, NEG)
            m_old = m_ref[...]
            m_new = jnp.maximum(m_old, jnp.max(t, axis=0, keepdims=True))
            alpha = jnp.exp2(m_old - m_new)
            p = jnp.exp2(t - m_new)
            acc_ref[...] = alpha * acc_ref[...] + jnp.dot(vt_ref[j], p.astype(BF16), preferred_element_type=F32)
            m_ref[...] = m_new

        def unmasked(j, carry):
            tile(j, None)
            return carry

        lax.fori_loop(0, w * qi, unmasked, 0)
        for u in range(w):
            tile(w * qi + u, u * tk)
        acc = acc_ref[...]
        l = acc_ref[ones_row:ones_row + 1, :]
        out_feature = lax.broadcasted_iota(jnp.int32, (HEAD_PAD, tq), 0)
        o_ref[...] = jnp.transpose(jnp.where(out_feature == ones_row, 0.0, acc / l))
        lse = m_ref[...] + jnp.log(l) * LOG2_E
        for u in range(w):
            lse_ref[u] = lse[:, u * tk:(u + 1) * tk]

    return pl.pallas_call(
        body, name="attn_fwd",
        grid=(H, S // tq),
        in_specs=[pl.BlockSpec((tq, HEAD_PAD), lambda h, i: (i, h)),
                  pl.BlockSpec((S, HEAD_PAD), lambda h, i: (0, h)),
                  pl.BlockSpec((S, HEAD_PAD), lambda h, i: (0, H + h)),
                  pl.BlockSpec((S, HEAD_PAD), lambda h, i: (0, 0))],
        out_specs=[pl.BlockSpec((tq, HEAD_PAD), lambda h, i: (i, h)),
                   pl.BlockSpec((None, w, 1, tk), lambda h, i: (h, i, 0, 0))],
        out_shape=[jax.ShapeDtypeStruct((S, H * HEAD_PAD), F32), jax.ShapeDtypeStruct((H, nk, 1, tk), F32)],
        scratch_shapes=[pltpu.VMEM((S, HEAD_PAD), BF16), pltpu.VMEM((nk, HEAD_PAD, tk), BF16),
                        pltpu.VMEM((1, tq), F32), pltpu.VMEM((HEAD_PAD, tq), F32)],
        compiler_params=_params(("parallel", "arbitrary")),
    )(qr, kv, kv, kpe)


def attn_delta(o, do, n_heads):
    S = o.shape[0]
    H = n_heads
    tq = _tile(S, (ATTN_TILE,))
    nq = S // tq

    def body(o_ref, do_ref, d_ref):
        for c in range(nq):
            rows = slice(c * tq, (c + 1) * tq)
            prod = o_ref[rows, :] * do_ref[rows, :].astype(F32)
            d_ref[c] = jnp.sum(jnp.transpose(prod), axis=0, keepdims=True)

    return pl.pallas_call(
        body, name="attn_delta",
        grid=(H,),
        in_specs=[pl.BlockSpec((S, HEAD_PAD), lambda h: (0, h)), pl.BlockSpec((S, HEAD_PAD), lambda h: (0, h))],
        out_specs=pl.BlockSpec((None, nq, 1, tq), lambda h: (h, 0, 0, 0)),
        out_shape=jax.ShapeDtypeStruct((H, nq, 1, tq), F32),
        compiler_params=_params(("parallel",)),
    )(o, do)


def attn_bwd(qr, kv, kpe, do, lse2, delta, n_heads):
    S = qr.shape[0]
    H = n_heads
    tk = _tile(S, (ATTN_TILE,))
    nk = S // tk
    w = 2 if nk % 2 == 0 else 1
    tq = w * tk
    nq = S // tq
    scale = (QK_NOPE + QK_ROPE) ** -0.5
    c2 = scale * LOG2_E
    nt = (((1,), (1,)), ((), ()))
    lse2 = lse2.reshape(H, nq, 1, tq)
    delta4 = delta.reshape(H, nq, 1, tq)

    def body(k_ref, v_ref, kpe_ref, q_ref, do_ref, lse_ref, dl_ref, dq_ref, dk_ref, dv_ref, dka_ref, dva_ref,
             dqt_ref):
        kj = pl.program_id(1)
        k = k_ref[...] + kpe_ref[...]
        kt = jnp.transpose(k.astype(F32)).astype(BF16)
        v = v_ref[...]

        @pl.when(kj == 0)
        def _():
            dqt_ref[...] = jnp.zeros_like(dqt_ref)

        dka_ref[...] = jnp.zeros_like(dka_ref)
        dva_ref[...] = jnp.zeros_like(dva_ref)

        def tile(i, masked):
            start = pl.multiple_of(i * tq, tq)
            q = q_ref[pl.ds(start, tq), :]
            do = do_ref[pl.ds(start, tq), :]
            t = lax.dot_general(k, q, nt, preferred_element_type=F32) * c2
            if masked:
                krow = lax.broadcasted_iota(jnp.int32, (tk, tq), 0)
                qcol = lax.broadcasted_iota(jnp.int32, (tk, tq), 1)
                t = jnp.where(krow + (kj % w) * tk <= qcol, t, NEG)
            pt = jnp.exp2(t - lse_ref[i])
            dva_ref[...] += jnp.dot(pt.astype(BF16), do, preferred_element_type=F32)
            dpt = lax.dot_general(v, do, nt, preferred_element_type=F32)
            dst = (pt * (dpt - dl_ref[i]) * scale).astype(BF16)
            dka_ref[...] += jnp.dot(dst, q, preferred_element_type=F32)
            dqt_ref[i] += jnp.dot(kt, dst, preferred_element_type=F32)

        tile(kj // w, True)

        def unmasked(i, carry):
            tile(i, False)
            return carry

        lax.fori_loop(kj // w + 1, nq, unmasked, 0)
        dk_ref[...] = dka_ref[...]
        dv_ref[...] = dva_ref[...]

        @pl.when(kj == nk - 1)
        def _():
            for c in range(nq):
                dq_ref[c * tq:(c + 1) * tq, :] = jnp.transpose(dqt_ref[c])

    blk = pl.BlockSpec((tk, HEAD_PAD), lambda h, j: (j, h))
    whole = pl.BlockSpec((S, HEAD_PAD), lambda h, j: (0, h))
    stat = pl.BlockSpec((None, nq, 1, tq), lambda h, j: (h, 0, 0, 0))
    shp = jax.ShapeDtypeStruct((S, H * HEAD_PAD), F32)
    return pl.pallas_call(
        body, name="attn_bwd",
        grid=(H, nk),
        in_specs=[blk, pl.BlockSpec((tk, HEAD_PAD), lambda h, j: (j, H + h)),
                  pl.BlockSpec((tk, HEAD_PAD), lambda h, j: (j, 0)), whole, whole, stat, stat],
        out_specs=[whole, blk, blk],
        out_shape=[shp, shp, shp],
        scratch_shapes=[pltpu.VMEM((tk, HEAD_PAD), F32), pltpu.VMEM((tk, HEAD_PAD), F32),
                        pltpu.VMEM((nq, HEAD_PAD, tq), F32)],
        compiler_params=_params(("parallel", "arbitrary")),
    )(kv, kv, kpe, qr, do, lse2, delta4)


def _pad_heads(w, width):
    R = w.shape[0]
    w3 = w.reshape(R, -1, width)
    return jnp.pad(w3, ((0, 0), (0, 0), (0, HEAD_PAD - width))).reshape(R, -1)


def _unpad_heads(w, width):
    R = w.shape[0]
    return w.reshape(R, -1, HEAD_PAD)[:, :, :width].reshape(R, -1)


def mla_pad_weights(p):
    H = N_HEADS
    w_q_b = _pad_heads(p["w_q_b"], QK_NOPE + QK_ROPE)
    kvb = p["w_kv_b"].reshape(KV_LORA, H, QK_NOPE + V_HEAD)
    wk = _pad_heads(kvb[:, :, :QK_NOPE].reshape(KV_LORA, -1), QK_NOPE)
    wv = _pad_heads(kvb[:, :, QK_NOPE:].reshape(KV_LORA, -1), V_HEAD)
    D = p["w_kv_a"].shape[0]
    a = p["w_kv_a"]
    w_kv_a = jnp.concatenate([a[:, :KV_LORA], jnp.zeros((D, QK_NOPE), a.dtype), a[:, KV_LORA:],
                              jnp.zeros((D, HEAD_PAD - QK_NOPE - QK_ROPE), a.dtype)], axis=1)
    wo = p["w_o"].reshape(H, V_HEAD, -1)
    w_o = jnp.pad(wo, ((0, 0), (0, HEAD_PAD - V_HEAD), (0, 0))).reshape(H * HEAD_PAD, -1)
    return dict(w_q_a=p["w_q_a"], w_q_b=w_q_b, w_kv_b=jnp.concatenate([wk, wv], axis=1), w_kv_a=w_kv_a, w_o=w_o)


def mla_kv_fwd(h, g, sh, sc, kv_a_norm_g, pw, tabs):
    hkv = norm_mod(h, g, sh, sc, "kv_norm_mod")
    ckvp = mm(hkv, pw["w_kv_a"], "nn", "kv_a")

    def f(ckv, kpe, c, s1, s2, g):
        xhat, _ = _rms(ckv)
        return (xhat * g).astype(BF16), _rope(kpe, c, s1, s2).astype(BF16)
    ckv_n, kpe_r = rowwise(f, [(ckvp, KV_LORA, 0), (ckvp, HEAD_PAD, KV_LORA // HEAD_PAD), *tabs], [kv_a_norm_g],
                           [(KV_LORA, BF16), (HEAD_PAD, BF16)], [], "kv_a_norm_rope")
    kv = mm(ckv_n, pw["w_kv_b"], "nn", "kv_b", out_dtype=BF16)
    return kv, kpe_r, (h, hkv, ckvp, ckv_n)


def mla_kv_bwd(dh_stream, dk, dv, saved, g, sc, kv_a_norm_g, pw, tabs):
    h, hkv, ckvp, ckv_n = saved
    H = N_HEADS
    lane = jnp.arange(HEAD_PAD)
    pe_mask = ((lane >= QK_NOPE) & (lane < QK_NOPE + QK_ROPE)).astype(F32)[None, :]

    def f(dk, dv, c, s1, s2, mask):
        tot = dk[:, :HEAD_PAD]
        for hh in range(1, H):
            tot = tot + dk[:, hh * HEAD_PAD:(hh + 1) * HEAD_PAD]
        dkpe = _rope_t(tot * mask, c, s1, s2) * mask
        return jnp.concatenate([dk, dv], axis=1).astype(BF16), dkpe
    dkv, dkpe = rowwise(f, [dk, dv, *tabs], [pe_mask], [(2 * H * HEAD_PAD, BF16), (HEAD_PAD, F32)], [],
                        "kv_split_bwd")
    d_w_kv_b = mm(ckv_n, dkv, "tn", "kv_b_dw")
    dckv_n = mm(dkv, pw["w_kv_b"], "nt", "kv_b_dx")

    def f2(ckv, dn, dkpe, g):
        xhat, r = _rms(ckv)
        dx = _rms_bwd(xhat, r, dn * g)
        return jnp.concatenate([dx, dkpe], axis=1).astype(BF16), dn * xhat
    dckvp, d_kv_a_g = rowwise(f2, [(ckvp, KV_LORA, 0), dckv_n, dkpe], [kv_a_norm_g],
                              [(KV_LORA + HEAD_PAD, BF16)], [KV_LORA], "kv_a_norm_bwd")
    d_w_kv_a = mm(hkv, dckvp, "tn", "kv_a_dw")
    dhkv = mm(dckvp, pw["w_kv_a"], "nt", "kv_a_dx")
    dh, d_sh, d_sc, d_g = norm_mod_bwd(h, dhkv, dh_stream, g, sc, "norm_mod_bwd")
    return dh, (d_sh, d_sc, d_g), d_kv_a_g, d_w_kv_a, d_w_kv_b


def mla_fwd(h, g, sh, sc, gate, q_a_norm_g, pw, kv, kpe_r, tabs):
    H = N_HEADS
    hn = norm_mod(h, g, sh, sc, "mla_norm_mod")
    qa = mm(hn, pw["w_q_a"], "nn", "q_a")

    def f(qa, g):
        xhat, _ = _rms(qa)
        return (xhat * g).astype(BF16)
    qa_n = rowwise(f, [qa], [q_a_norm_g], [(qa.shape[1], BF16)], [], "q_a_norm")[0]
    qp = mm(qa_n, pw["w_q_b"], "nn", "q_b")

    def frope(q, c, s1, s2):
        return jnp.concatenate([_rope(q[:, hh * HEAD_PAD:(hh + 1) * HEAD_PAD], c, s1, s2) for hh in range(H)],
                               axis=1).astype(BF16)
    qr = rowwise(frope, [qp, *tabs], [], [(H * HEAD_PAD, BF16)], [], "q_rope")[0]
    o, lse = attn_fwd(qr, kv, kpe_r, H)
    y = mm(o, pw["w_o"], "nn", "w_o")
    h_out, _ = residual(h, y, gate, 1.0, "mla_residual")
    return h_out, (h, hn, qa, qa_n, qr, o, lse, y)


def mla_bwd(dh_out, saved, g, sc, gate, q_a_norm_g, pw, kv, kpe_r, tabs):
    h, hn, qa, qa_n, qr, o, lse, y = saved
    H = N_HEADS
    dy, d_gate = residual_bwd(dh_out, y, gate, 1.0, "mla_residual_bwd")
    d_w_o = mm(o, dy, "tn", "w_o_dw")
    do = mm(dy, pw["w_o"], "nt", "w_o_dx", out_dtype=BF16)
    delta = attn_delta(o, do, H)
    dqr, dk, dv = attn_bwd(qr, kv, kpe_r, do, lse, delta, H)

    def frope_t(dq, c, s1, s2):
        return jnp.concatenate([_rope_t(dq[:, hh * HEAD_PAD:(hh + 1) * HEAD_PAD], c, s1, s2) for hh in range(H)],
                               axis=1).astype(BF16)
    dqp = rowwise(frope_t, [dqr, *tabs], [], [(H * HEAD_PAD, BF16)], [], "q_rope_bwd")[0]
    d_w_q_b = mm(qa_n, dqp, "tn", "q_b_dw")
    dqa_n = mm(dqp, pw["w_q_b"], "nt", "q_b_dx")

    def f(qa, dn, g):
        xhat, r = _rms(qa)
        return _rms_bwd(xhat, r, dn * g).astype(BF16), dn * xhat
    dqa, d_q_a_g = rowwise(f, [qa, dqa_n], [q_a_norm_g], [(qa.shape[1], BF16)], [qa.shape[1]], "q_a_norm_bwd")
    d_w_q_a = mm(hn, dqa, "tn", "q_a_dw")
    dhn = mm(dqa, pw["w_q_a"], "nt", "q_a_dx")
    dh_in, d_sh, d_sc, d_g = norm_mod_bwd(h, dhn, dh_out, g, sc, "norm_mod_bwd")
    grads = dict(w_q_a=d_w_q_a, q_a_norm_g=d_q_a_g, w_q_b=d_w_q_b, w_o=d_w_o)
    return dh_in, (d_sh, d_sc, d_gate, d_g), grads, dk, dv


def loss_head(h, target, g):
    D = h.shape[1]

    def f(h, t, g):
        xhat, r = _rms(h)
        err = xhat * g - t
        dy = err * (1.0 / D)
        dh = _rms_bwd(xhat, r, dy * g)
        return dh, (0.5 / D) * err * err, dy * xhat
    return rowwise(f, [h, target], [g], [(D, F32)], [D, D], "loss_head")


def _place():
    x, y, c = lax.axis_index("x"), lax.axis_index("y"), lax.axis_index("c")
    chips = [(1 - x, y), (x, 1 - y), (1 - x, 1 - y)]
    return x, y, c, chips


HBM_SPEC = pl.BlockSpec(memory_space=pltpu.HBM)


def all_gather8(v):
    m, n = v.shape

    def body(x_ref, out_ref, send_sems, recv_sems, local_sem):
        x, y, c, chips = _place()
        me, sibling = (x, y, c), (x, y, 1 - c)

        def rows(px, py, pc):
            return out_ref.at[4 * px + 2 * py + pc]

        def copy(k, block, to, src=None):
            return pltpu.make_async_remote_copy(
                src_ref=rows(*block) if src is None else src, dst_ref=rows(*block),
                send_sem=send_sems.at[k], recv_sem=recv_sems.at[k], device_id=to, device_id_type=MESH)

        mine = pltpu.make_async_copy(x_ref, rows(*me), local_sem)
        mine.start()
        first = [copy(0, me, sibling, src=x_ref)]
        first += [copy(1 + j, me, (*chip, c), src=x_ref) for j, chip in enumerate(chips)]
        for cp in first:
            cp.start()
        passed = [copy(4 + j, (*chip, c), sibling) for j, chip in enumerate(chips)]
        for j, chip in enumerate(chips):
            copy(1 + j, (*chip, c), me).wait_recv()
            passed[j].start()
        copy(0, sibling, me).wait_recv()
        for j, chip in enumerate(chips):
            copy(4 + j, (*chip, 1 - c), me).wait_recv()
        for cp in first + passed:
            cp.wait_send()
        mine.wait()

    return pl.pallas_call(
        body, name="all_gather8",
        out_shape=jax.ShapeDtypeStruct((8, m, n), v.dtype),
        in_specs=[pl.BlockSpec(memory_space=pltpu.VMEM)],
        out_specs=pl.BlockSpec(memory_space=pltpu.VMEM),
        scratch_shapes=[pltpu.SemaphoreType.DMA((7,)), pltpu.SemaphoreType.DMA((7,)), pltpu.SemaphoreType.DMA],
        compiler_params=pltpu.CompilerParams(vmem_limit_bytes=VMEM_LIMIT_BYTES),
    )(v)


def gather_weights(bufs):
    n = len(bufs)

    def body(*refs):
        ins, outs = refs[:n], refs[n:2 * n]
        send_sems, recv_sems = refs[2 * n:]
        x, y, c, chips = _place()
        across_x, across_y, across_both = chips
        sibling = (x, y, 1 - c)
        me = 2 * x + y
        via_in = (x + (1 - c) * (1 - 2 * x), y + c * (1 - 2 * y))
        via_out = (x + c * (1 - 2 * x), y + (1 - c) * (1 - 2 * y))

        def idx(chip):
            return 2 * chip[0] + chip[1]

        def copy(w, k, src, dst, to):
            return pltpu.make_async_remote_copy(src_ref=src, dst_ref=dst, send_sem=send_sems.at[6 * w + k],
                                                recv_sem=recv_sems.at[6 * w + k], device_id=to, device_id_type=MESH)

        def landed(w, k, chip):
            blk = outs[w].at[idx(chip), c]
            copy(w, k, blk, blk, (*chip, c)).wait_recv()
            return blk

        sends = [copy(w, j, ins[w].at[me, c], outs[w].at[me, c], (*chip, c))
                 for w in range(n) for j, chip in enumerate((across_x, across_y))]
        for cp in sends:
            cp.start()
        for w in range(n):
            blk = landed(w, c, via_in)
            sends += [copy(w, 2, blk, blk, (*via_out, c)), copy(w, 3 + c, blk, blk, sibling)]
            sends[-2].start()
            sends[-1].start()
        for w in range(n):
            blk = landed(w, 1 - c, via_out)
            sends.append(copy(w, 4 - c, blk, blk, sibling))
            sends[-1].start()
        for w in range(n):
            blk = landed(w, 2, across_both)
            sends.append(copy(w, 5, blk, blk, sibling))
            sends[-1].start()
        for w in range(n):
            for j, chip in enumerate(chips):
                other = outs[w].at[idx(chip), 1 - c]
                copy(w, 3 + j, other, other, sibling).wait_recv()
        for cp in sends:
            cp.wait_send()

    return pl.pallas_call(
        body, name="gather_weights",
        out_shape=[jax.ShapeDtypeStruct(b.shape, b.dtype) for b in bufs],
        in_specs=[HBM_SPEC] * n, out_specs=[HBM_SPEC] * n,
        input_output_aliases={w: w for w in range(n)},
        scratch_shapes=[pltpu.SemaphoreType.DMA((6 * n,)), pltpu.SemaphoreType.DMA((6 * n,))],
    )(*bufs)


SEM_SPEC = pl.BlockSpec(memory_space=pltpu.SEMAPHORE)
SPLIT_COPY = pltpu.CompilerParams(has_side_effects=pltpu.SideEffectType.DATAFLOW_SIDE_EFFECTING)
PEERS_PER_BLOCK = 6


def gather_start(groups, carried):
    flat = [a for grp in groups for a in grp]
    group_of = [g for g, grp in enumerate(groups) for _ in grp]
    n, n_g, n_all = len(flat), len(groups), len(flat) + len(carried)

    def body(*refs):
        ins, sems = refs[:n], refs[n_all:n_all + 2 * n_g]
        x, y, c, chips = _place()
        me = 2 * x + y
        for w in range(n):
            mine = ins[w].at[me, c]
            for chip in chips:
                for core in range(2):
                    pltpu.make_async_remote_copy(src_ref=mine, dst_ref=mine, send_sem=sems[2 * group_of[w]],
                                                 recv_sem=sems[2 * group_of[w] + 1], device_id=(*chip, core),
                                                 device_id_type=MESH).start()

    operands = flat + list(carried)
    res = pl.pallas_call(
        body, name="gather_start",
        out_shape=[pltpu.SemaphoreType.DMA(())] * (2 * n_g) + [pltpu.HBM(a.shape, a.dtype) for a in operands],
        in_specs=[HBM_SPEC] * n_all,
        out_specs=[SEM_SPEC] * (2 * n_g) + [HBM_SPEC] * n_all,
        input_output_aliases={w: 2 * n_g + w for w in range(n_all)},
        compiler_params=SPLIT_COPY,
    )(*[pltpu.with_memory_space_constraint(a, pltpu.HBM) for a in operands])
    sems = [(res[2 * g], res[2 * g + 1]) for g in range(n_g)]
    arrays, k = [], 2 * n_g
    for grp in groups:
        arrays.append(list(res[k:k + len(grp)]))
        k += len(grp)
    return sems, arrays, list(res[k:])


def gather_wait(arrays, sems, after, name):
    n = len(arrays)

    def body(*refs):
        ins, send_sem, recv_sem = refs[:n], refs[n], refs[n + 1]
        x, y, c, _ = _place()
        for w in range(n):
            half = ins[w].at[0, 0]
            cp = pltpu.make_async_remote_copy(src_ref=half, dst_ref=half, send_sem=send_sem, recv_sem=recv_sem,
                                              device_id=(x, y, c), device_id_type=MESH)
            for _ in range(PEERS_PER_BLOCK):
                cp.wait_send()
            for _ in range(PEERS_PER_BLOCK):
                cp.wait_recv()

    return pl.pallas_call(
        body, name=name,
        out_shape=[pltpu.HBM(a.shape, a.dtype) for a in arrays],
        in_specs=[HBM_SPEC] * n + [SEM_SPEC, SEM_SPEC, pl.BlockSpec(memory_space=pl.ANY)],
        out_specs=[HBM_SPEC] * n,
        input_output_aliases={w: w for w in range(n)},
        compiler_params=SPLIT_COPY,
    )(*arrays, *sems, after)


def exchange_halves(gs):
    n = len(gs)

    def body(*refs):
        ins, theirs = refs[:n], refs[n:2 * n]
        send_sems, recv_sems = refs[2 * n:]
        x, y, c, _ = _place()
        sends = [pltpu.make_async_remote_copy(src_ref=ins[w].at[:, 1 - c], dst_ref=theirs[w],
                                              send_sem=send_sems.at[w], recv_sem=recv_sems.at[w],
                                              device_id=(x, y, 1 - c), device_id_type=MESH) for w in range(n)]
        for cp in sends:
            cp.start()
        for cp in sends:
            cp.wait()

    return pl.pallas_call(
        body, name="exchange_halves",
        out_shape=[jax.ShapeDtypeStruct((4,) + g.shape[2:], g.dtype) for g in gs],
        in_specs=[HBM_SPEC] * n, out_specs=[HBM_SPEC] * n,
        scratch_shapes=[pltpu.SemaphoreType.DMA((n,)), pltpu.SemaphoreType.DMA((n,))],
    )(*gs)


def join_halves(qs):
    n = len(qs)

    def body(*refs):
        ins, outs = refs[:n], refs[n:2 * n]
        send_sems, recv_sems = refs[2 * n:]
        x, y, c, _ = _place()
        sends = [pltpu.make_async_remote_copy(src_ref=ins[w].at[c], dst_ref=outs[w].at[c], send_sem=send_sems.at[w],
                                              recv_sem=recv_sems.at[w], device_id=(x, y, 1 - c), device_id_type=MESH)
                 for w in range(n)]
        for cp in sends:
            cp.start()
        for w in range(n):
            other = outs[w].at[1 - c]
            pltpu.make_async_remote_copy(src_ref=other, dst_ref=other, send_sem=send_sems.at[w],
                                         recv_sem=recv_sems.at[w], device_id=(x, y, 1 - c),
                                         device_id_type=MESH).wait_recv()
        for cp in sends:
            cp.wait_send()

    return pl.pallas_call(
        body, name="join_halves",
        out_shape=[jax.ShapeDtypeStruct(q.shape, q.dtype) for q in qs],
        in_specs=[HBM_SPEC] * n, out_specs=[HBM_SPEC] * n,
        input_output_aliases={w: w for w in range(n)},
        scratch_shapes=[pltpu.SemaphoreType.DMA((n,)), pltpu.SemaphoreType.DMA((n,))],
    )(*qs)


def _row_tile(R, row_bytes):
    tm = R
    for t in (512, 256, 128, 64, 32, 16, 8):
        if R % t == 0:
            tm = t
            if t * row_bytes <= ROW_TILE_BUDGET:
                break
    return tm


def sum_siblings(g, theirs, place):
    _, _, R, C = g.shape
    tm = _row_tile(R, 3 * C * 4)

    def body(place_ref, a_ref, b_ref, o_ref):
        o_ref[...] = (a_ref[...] + b_ref[...]).astype(BF16)

    return pl.pallas_call(
        body, name="sum_siblings",
        grid_spec=pltpu.PrefetchScalarGridSpec(
            num_scalar_prefetch=1, grid=(4, R // tm),
            in_specs=[pl.BlockSpec((None, None, tm, C), lambda j, i, s: (j, s[1], i, 0)),
                      pl.BlockSpec((None, tm, C), lambda j, i, s: (j, i, 0))],
            out_specs=pl.BlockSpec((None, tm, C), lambda j, i, s: (j, i, 0))),
        out_shape=jax.ShapeDtypeStruct((4, R, C), BF16),
        compiler_params=_params(("parallel", "parallel")),
    )(place, g, theirs)


def sum_chips(p, landed, place):
    _, R, C = p.shape
    tm = _row_tile(R, 5 * C * 4)

    def body(place_ref, p_ref, l0_ref, l1_ref, l2_ref, o_ref):
        o_ref[...] = ((p_ref[...].astype(F32) + l0_ref[...].astype(F32)) + l1_ref[...].astype(F32)
                      ) + l2_ref[...].astype(F32)

    return pl.pallas_call(
        body, name="sum_chips",
        grid_spec=pltpu.PrefetchScalarGridSpec(
            num_scalar_prefetch=1, grid=(R // tm,),
            in_specs=[pl.BlockSpec((None, tm, C), lambda i, s: (s[0], i, 0))]
            + [pl.BlockSpec((None, tm, C), lambda i, s, j=j: (j, i, 0)) for j in range(3)],
            out_specs=pl.BlockSpec((None, tm, C), lambda i, s: (s[1], i, 0))),
        out_shape=jax.ShapeDtypeStruct((2, R, C), F32),
        compiler_params=_params(("parallel",)),
    )(place, p, landed, landed, landed)


def sum_blocks(items, name):
    R, C = items[0][0].shape[1:]
    tm = _row_tile(R, C * 4 * (len(items) + 1))
    n = len(items)

    def body(*refs):
        acc = refs[0][...].astype(F32)
        for r in refs[1:n]:
            acc = acc + r[...].astype(F32)
        refs[n][...] = acc

    return pl.pallas_call(
        body, name=name,
        grid=(R // tm,),
        in_specs=[pl.BlockSpec((None, tm, C), lambda i, j=j: (j, i, 0)) for _, j in items],
        out_specs=pl.BlockSpec((tm, C), lambda i: (i, 0)),
        out_shape=jax.ShapeDtypeStruct((R, C), F32),
        compiler_params=_params(("parallel",)),
    )(*[a for a, _ in items])


def scatter_start(ps, name):
    n = len(ps)

    def body(*refs):
        ins, lands, sems = refs[:n], refs[n:2 * n], refs[2 * n:2 * n + 2]
        x, y, c, chips = _place()
        for w in range(n):
            for j, chip in enumerate(chips):
                pltpu.make_async_remote_copy(src_ref=ins[w].at[2 * chip[0] + chip[1]], dst_ref=lands[w].at[j],
                                             send_sem=sems[0], recv_sem=sems[1], device_id=(*chip, c),
                                             device_id_type=MESH).start()

    operands = list(ps) + [lax.empty((3,) + p.shape[1:], p.dtype) for p in ps]
    res = pl.pallas_call(
        body, name=name,
        out_shape=[pltpu.SemaphoreType.DMA(())] * 2 + [pltpu.HBM(a.shape, a.dtype) for a in operands],
        in_specs=[HBM_SPEC] * (2 * n),
        out_specs=[SEM_SPEC] * 2 + [HBM_SPEC] * (2 * n),
        input_output_aliases={w: 2 + w for w in range(2 * n)},
        compiler_params=SPLIT_COPY,
    )(*[pltpu.with_memory_space_constraint(a, pltpu.HBM) for a in operands])
    return (res[0], res[1]), list(res[2:2 + n]), list(res[2 + n:])


def scatter_wait(ps, lands, sems, after, name):
    n = len(ps)

    def body(*refs):
        lands_in, send_sem, recv_sem = refs[n:2 * n], refs[2 * n], refs[2 * n + 1]
        x, y, c, _ = _place()
        for w in range(n):
            blk = lands_in[w].at[0]
            cp = pltpu.make_async_remote_copy(src_ref=blk, dst_ref=blk, send_sem=send_sem, recv_sem=recv_sem,
                                              device_id=(x, y, c), device_id_type=MESH)
            for _ in range(3):
                cp.wait_send()
            for _ in range(3):
                cp.wait_recv()

    operands = list(ps) + list(lands)
    res = pl.pallas_call(
        body, name=name,
        out_shape=[pltpu.HBM(a.shape, a.dtype) for a in operands],
        in_specs=[HBM_SPEC] * (2 * n) + [SEM_SPEC, SEM_SPEC, pl.BlockSpec(memory_space=pl.ANY)],
        out_specs=[HBM_SPEC] * (2 * n),
        input_output_aliases={w: w for w in range(2 * n)},
        compiler_params=SPLIT_COPY,
    )(*operands, *sems, after)
    return list(res[:n]), list(res[n:])


def reduce_start(gs, place, name):
    theirs = exchange_halves(gs)
    return scatter_start([sum_siblings(g, t, place) for g, t in zip(gs, theirs)], "scatter_start_" + name)


def reduce_finish(started, place, after, name):
    sems, ps, lands = started
    ps, lands = scatter_wait(ps, lands, sems, after, "scatter_wait_" + name)
    return [sum_chips(p, l, place) for p, l in zip(ps, lands)]


def adamw(w, g, m, v):
    shape = w.shape
    C = shape[-1]
    R = w.size // C

    def f(w, g, m, v):
        m = ADAM_B1 * m + (1.0 - ADAM_B1) * g
        v = ADAM_B2 * v + (1.0 - ADAM_B2) * (g * g)
        m_hat = m / (1.0 - ADAM_B1 ** ADAM_STEP)
        v_hat = v / (1.0 - ADAM_B2 ** ADAM_STEP)
        delta = -ADAM_LR * (m_hat / (jnp.sqrt(v_hat) + ADAM_EPS) + ADAM_WD * w)
        return delta, m, v

    d, nm, nv = rowwise(f, [a.reshape(R, C) for a in (w, g, m, v)], [], [(C, F32)] * 3, [], "adamw")
    return d.reshape(shape), nm.reshape(shape), nv.reshape(shape)


def _cast_into_slot(w, place):
    C = w.shape[-1]
    w2 = w.reshape(-1, C)
    R = w2.shape[0]
    tm = _row_tile(R, 6 * C)

    def body(place_ref, w_ref, o_ref):
        o_ref[...] = w_ref[...].astype(BF16)

    out = pl.pallas_call(
        body, name="cast_bf16",
        grid_spec=pltpu.PrefetchScalarGridSpec(
            num_scalar_prefetch=1, grid=(R // tm,),
            in_specs=[pl.BlockSpec((tm, C), lambda i, s: (i, 0))],
            out_specs=pl.BlockSpec((None, tm, C), lambda i, s: (s[0], i, 0))),
        out_shape=jax.ShapeDtypeStruct((4, R, C), BF16),
        compiler_params=_params(("parallel",)),
    )(place, w2)
    return out.reshape(4, 2, R // 2, C)


def _pack(vs):
    flat = jnp.concatenate([v.reshape(-1) for v in vs])
    n = flat.shape[0]
    total = -(-n // F32_TILE) * F32_TILE
    return jnp.pad(flat, (0, total - n)).reshape(total // LANES, LANES)


def _unpack(flat, like):
    out, o = [], 0
    for shp in like:
        sz = 1
        for d in shp:
            sz *= d
        out.append(flat[o:o + sz].reshape(shp))
        o += sz
    return out


def _cols_to_blocks(g, n_chips=4):
    R, N = g.shape
    C = N // n_chips
    return g.reshape(R, n_chips, C).transpose(1, 0, 2).reshape(n_chips, 2, R // 2, C)


def _rows_to_blocks(g, n_chips=4):
    R, C = g.shape
    return g.reshape(n_chips, 2, R // n_chips // 2, C)


def kernel(x, c, positions, ada_w, ada_b, norm_g, ffn_w13, ffn_w2, conv_w_pw1, conv_b_pw1, conv_w_dw, conv_b_dw, conv_ln_g, conv_ln_b, conv_w_pw2, conv_b_pw2, kv_ada_w, kv_ada_b, kv_norm_g, w_kv_a, kv_a_norm_g, w_kv_b, w_q_a, q_a_norm_g, w_q_b, w_o, final_norm_g, loss_target, m_ada_w, m_ada_b, m_norm_g, m_ffn_w13, m_ffn_w2, m_conv_w_pw1, m_conv_b_pw1, m_conv_w_dw, m_conv_b_dw, m_conv_ln_g, m_conv_ln_b, m_conv_w_pw2, m_conv_b_pw2, m_kv_ada_w, m_kv_ada_b, m_kv_norm_g, m_w_kv_a, m_kv_a_norm_g, m_w_kv_b, m_w_q_a, m_q_a_norm_g, m_w_q_b, m_w_o, m_final_norm_g, v_ada_w, v_ada_b, v_norm_g, v_ffn_w13, v_ffn_w2, v_conv_w_pw1, v_conv_b_pw1, v_conv_w_dw, v_conv_b_dw, v_conv_ln_g, v_conv_ln_b, v_conv_w_pw2, v_conv_b_pw2, v_kv_ada_w, v_kv_ada_b, v_kv_norm_g, v_w_kv_a, v_kv_a_norm_g, v_w_kv_b, v_w_q_a, v_q_a_norm_g, v_w_q_b, v_w_o, v_final_norm_g):
    S, D = x.shape[1], x.shape[2]
    H = N_HEADS
    F = ffn_w2.shape[2] * 4
    xi, yi, ci = lax.axis_index("x"), lax.axis_index("y"), lax.axis_index("c")
    chip = 2 * xi + yi
    dev = 2 * chip + ci
    place = jnp.stack([chip, ci]).astype(jnp.int32)
    h0 = x[0]
    target = loss_target[0]

    silu_c = rowwise(lambda a: a * _sigmoid(a), [c], [], [(D, F32)], [], "silu_c")[0]
    silu_all = all_gather8(silu_c.reshape(8, D // 8)).reshape(8, D)
    n_ada = ada_w.shape[2]
    n_kv = kv_ada_w.shape[1]
    ada_b_mine = lax.dynamic_slice_in_dim(ada_b, chip * n_ada, n_ada, axis=1)
    kv_b_mine = lax.dynamic_slice_in_dim(kv_ada_b, chip * n_kv, n_kv, axis=0)[None, :]
    mods = [mm(silu_all, ada_w[l], "nn", "ada_rows", bias=ada_b_mine[l:l + 1]) for l in range(2)]
    mods.append(mm(silu_all, kv_ada_w, "nn", "kv_ada_rows", bias=kv_b_mine))
    n_mod_cols = 2 * n_ada + n_kv
    mod_pack = jnp.concatenate(mods, axis=1).reshape(-1, LANES)
    mod_all = all_gather8(mod_pack).reshape(8, 8, n_mod_cols)[0::2]
    mod_mine = lax.dynamic_index_in_dim(mod_all, dev, axis=1, keepdims=False)
    mod = [mod_mine[:, l * n_ada:(l + 1) * n_ada].reshape(N_MOD, D) for l in range(2)]
    kv_mod = mod_mine[:, 2 * n_ada:].reshape(2, D)
    kv_shift, kv_scale = kv_mod[0:1], kv_mod[1:2]

    def mrow(l, k):
        return mod[l][k:k + 1]

    def slot(w):
        return _cast_into_slot(w, place)
    first = gather_weights([slot(ffn_w13[0, 0]), slot(ffn_w2[0, 0])])
    groups = [[slot(conv_w_pw1), slot(conv_w_pw2)],
              [slot(ffn_w13[0, 1]), slot(ffn_w2[0, 1])],
              [slot(w_kv_a), slot(w_kv_b), slot(ffn_w13[1, 0]), slot(ffn_w2[1, 0]), slot(w_q_a), slot(w_q_b), slot(w_o),
               slot(ffn_w13[1, 1]), slot(ffn_w2[1, 1])]]
    sems, started, first = gather_start(groups, first)

    def ffn_weights(w13_blocks, w2_blocks):
        return w13_blocks.reshape(4, 1, 1, D, F // 2), w2_blocks.reshape(F, D)
    small_like = [norm_g.shape, conv_b_pw1.shape, conv_w_dw.shape, conv_b_dw.shape, conv_ln_g.shape,
                  conv_ln_b.shape, conv_b_pw2.shape]
    small_pack = _pack([norm_g, conv_b_pw1, conv_w_dw, conv_b_dw, conv_ln_g, conv_ln_b, conv_b_pw2])
    small_all = all_gather8(small_pack)[0::2].reshape(4, -1)
    per_chip = [_unpack(small_all[j], small_like) for j in range(4)]
    smalls = [jnp.concatenate([per_chip[j][k] for j in range(4)], axis=-1) for k in range(len(small_like))]
    norm_g_f, b_pw1_f, w_dw_f, b_dw_f, ln_g_f, ln_b_f, b_pw2_f = smalls

    tabs = rope_tables(positions[0])

    def ng(l, k):
        return norm_g_f[l, k][None, :]

    h = h0
    ffn00 = ffn_weights(*first)
    h, s_f1_0 = ffn_fwd(h, ng(0, 0), mrow(0, 0), mrow(0, 1), mrow(0, 2), ffn00[0], 0, 0, ffn00[1])
    g_pw1, g_pw2 = gather_wait(started[0], sems[0], h, "gather_wait_conv")
    conv_p = dict(
        w_pw1=g_pw1.reshape(4, D, 2 * D // 4).transpose(1, 0, 2).reshape(D, 2 * D),
        b_pw1=b_pw1_f, w_dw=w_dw_f[0], b_dw=b_dw_f, ln_g=ln_g_f, ln_b=ln_b_f,
        w_pw2=g_pw2.reshape(D, D), b_pw2=b_pw2_f)
    h, s_conv = conv_module_fwd(h, ng(0, 1), mrow(0, 3), mrow(0, 4), mrow(0, 5), conv_p)
    ffn01 = ffn_weights(*gather_wait(started[1], sems[1], h, "gather_wait_ffn"))
    h, s_f2_0 = ffn_fwd(h, ng(0, 2), mrow(0, 6), mrow(0, 7), mrow(0, 8), ffn01[0], 0, 0, ffn01[1])
    (g_kv_a, g_kv_b, g_w13_10, g_w2_10, g_q_a, g_q_b, g_w_o, g_w13_11, g_w2_11) = gather_wait(
        started[2], sems[2], h, "gather_wait_layer1")
    ffn10, ffn11 = ffn_weights(g_w13_10, g_w2_10), ffn_weights(g_w13_11, g_w2_11)
    q_lora = w_q_a.shape[2]
    pw = mla_pad_weights(dict(
        w_kv_a=g_kv_a.reshape(D, KV_LORA + QK_ROPE),
        w_kv_b=g_kv_b.reshape(4, KV_LORA, -1).transpose(1, 0, 2).reshape(KV_LORA, -1),
        w_q_a=g_q_a.reshape(D, q_lora),
        w_q_b=g_q_b.reshape(4, q_lora, -1).transpose(1, 0, 2).reshape(q_lora, -1),
        w_o=g_w_o.reshape(H * V_HEAD, D)))
    kv_norm = kv_norm_g[None, :]
    kv_a_g = kv_a_norm_g[None, :]
    kv, kpe_r, s_kv = mla_kv_fwd(h, kv_norm, kv_shift, kv_scale, kv_a_g, pw, tabs)
    h, s_f1_1 = ffn_fwd(h, ng(1, 0), mrow(1, 0), mrow(1, 1), mrow(1, 2), ffn10[0], 0, 0, ffn10[1])
    h, s_mla = mla_fwd(h, ng(1, 1), mrow(1, 3), mrow(1, 4), mrow(1, 5), q_a_norm_g, pw, kv, kpe_r, tabs)
    h, s_f2_1 = ffn_fwd(h, ng(1, 2), mrow(1, 6), mrow(1, 7), mrow(1, 8), ffn11[0], 0, 0, ffn11[1])
    dh, loss_cols, d_final_g = loss_head(h, target, final_norm_g[None, :])

    def w13_blocks(dw):
        return dw.reshape(4, 2, D // 2, F // 2)

    dh, v_f2_1, dw13_11, dw2_11 = ffn_bwd(dh, s_f2_1, ng(1, 2), mrow(1, 7), mrow(1, 8), ffn11[0], 0, 0, ffn11[1])
    red_a = reduce_start([w13_blocks(dw13_11), _rows_to_blocks(dw2_11)], place, "a")
    dh, v_mla, g_mla, dk, dv = mla_bwd(dh, s_mla, ng(1, 1), mrow(1, 4), mrow(1, 5), q_a_norm_g, pw, kv, kpe_r, tabs)
    dh, v_f1_1, dw13_10, dw2_10 = ffn_bwd(dh, s_f1_1, ng(1, 0), mrow(1, 1), mrow(1, 2), ffn10[0], 0, 0, ffn10[1])
    dh, v_kv, d_kv_a_g, d_w_kv_a, d_w_kv_b = mla_kv_bwd(dh, dk, dv, s_kv, kv_norm, kv_scale, kv_a_g, pw, tabs)
    d_w_kv_a_u = jnp.concatenate([d_w_kv_a[:, :KV_LORA], d_w_kv_a[:, KV_LORA + QK_NOPE:KV_LORA + QK_NOPE + QK_ROPE]],
                                 axis=1)
    hk = H * HEAD_PAD
    dkb = jnp.concatenate([d_w_kv_b[:, :hk].reshape(KV_LORA, H, HEAD_PAD)[:, :, :QK_NOPE],
                           d_w_kv_b[:, hk:].reshape(KV_LORA, H, HEAD_PAD)[:, :, :V_HEAD]], axis=2).reshape(KV_LORA, -1)
    d_w_q_b_u = _unpad_heads(g_mla["w_q_b"], QK_NOPE + QK_ROPE)
    d_w_o_u = g_mla["w_o"].reshape(H, HEAD_PAD, D)[:, :V_HEAD].reshape(H * V_HEAD, D)
    q_w13_11, q_w2_11 = reduce_finish(red_a, place, dh, "a")
    red_b = reduce_start([w13_blocks(dw13_10), _rows_to_blocks(dw2_10), _rows_to_blocks(d_w_kv_a_u), _cols_to_blocks(dkb),
                          _rows_to_blocks(g_mla["w_q_a"]), _cols_to_blocks(d_w_q_b_u), _rows_to_blocks(d_w_o_u)],
                         place, "b")
    dh, v_f2_0, dw13_01, dw2_01 = ffn_bwd(dh, s_f2_0, ng(0, 2), mrow(0, 7), mrow(0, 8), ffn01[0], 0, 0, ffn01[1])
    dh, v_conv, g_conv = conv_module_bwd(dh, s_conv, ng(0, 1), mrow(0, 4), mrow(0, 5), conv_p)
    q_w13_10, q_w2_10, q_kv_a, q_kv_b, q_q_a, q_q_b, q_w_o = reduce_finish(red_b, place, dh, "b")
    red_c = reduce_start([w13_blocks(dw13_01), _rows_to_blocks(dw2_01), _cols_to_blocks(g_conv["w_pw1"]),
                          _rows_to_blocks(g_conv["w_pw2"])], place, "c")
    dh, v_f1_0, dw13_00, dw2_00 = ffn_bwd(dh, s_f1_0, ng(0, 0), mrow(0, 1), mrow(0, 2), ffn00[0], 0, 0, ffn00[1])
    grad_x = dh[None]
    q_w13_01, q_w2_01, q_pw1, q_pw2 = reduce_finish(red_c, place, dh, "c")
    red_d = reduce_start([w13_blocks(dw13_00), _rows_to_blocks(dw2_00)], place, "d")
    q_w13_00, q_w2_00 = reduce_finish(red_d, place, dh, "d")
    red = [j.reshape(2 * j.shape[1], j.shape[2]) for j in join_halves(
        [q_w13_00, q_w13_01, q_w13_10, q_w13_11, q_w2_00, q_w2_01, q_w2_10, q_w2_11, q_pw1, q_pw2, q_kv_a, q_kv_b,
         q_q_a, q_q_b, q_w_o])]
    g_ffn_w13 = jnp.stack(red[0:4]).reshape(ffn_w13.shape)
    g_ffn_w2 = jnp.stack(red[4:8]).reshape(ffn_w2.shape)
    g_conv_w_pw1 = red[8].reshape(conv_w_pw1.shape)
    g_conv_w_pw2 = red[9].reshape(conv_w_pw2.shape)
    g_w_kv_a = red[10].reshape(w_kv_a.shape)
    g_w_kv_b = red[11].reshape(w_kv_b.shape)
    g_w_q_a = red[12].reshape(w_q_a.shape)
    g_w_q_b = red[13].reshape(w_q_b.shape)
    g_w_o = red[14].reshape(w_o.shape)

    def dmod(v1, vm, v2):
        return jnp.concatenate([v1[0], v1[1], v1[2], vm[0], vm[1], vm[2], v2[0], v2[1], v2[2]], axis=1)
    d_mod0 = dmod(v_f1_0, v_conv, v_f2_0)
    d_mod1 = dmod(v_f1_1, v_mla, v_f2_1)
    d_kv_mod = jnp.concatenate([v_kv[0], v_kv[1]], axis=1)
    d_norm_g = jnp.concatenate([v_f1_0[3], v_conv[3], v_f2_0[3], v_f1_1[3], v_mla[3], v_f2_1[3]], axis=0)
    vec_list = [d_mod0, d_mod1, d_kv_mod, d_norm_g, g_conv["b_pw1"], g_conv["w_dw"], g_conv["b_dw"], g_conv["ln_g"],
                g_conv["ln_b"], g_conv["b_pw2"], v_kv[2], d_kv_a_g, g_mla["q_a_norm_g"], d_final_g, loss_cols]
    vec_like = [v.shape for v in vec_list]
    vec_pack = _pack(vec_list)
    n_mod_rows = (2 * N_MOD * D + 2 * D) // LANES
    vec_all = all_gather8(vec_pack)
    vec_sum = sum_blocks([(vec_all, d) for d in range(8)], "sum_devices").reshape(-1)
    (_, _, _, s_norm_g, s_b_pw1, s_w_dw, s_b_dw, s_ln_g, s_ln_b, s_b_pw2, s_kv_norm_g, s_kv_a_g, s_q_a_g,
     s_final_g, s_loss) = _unpack(vec_sum, vec_like)
    loss = jnp.sum(s_loss)
    dmod_all = vec_all[:, :n_mod_rows].reshape(8, 2 * N_MOD * D + 2 * D)
    dmod_sum = vec_sum[:2 * N_MOD * D + 2 * D]
    g_ada_b = dmod_sum[:2 * N_MOD * D].reshape(2, N_MOD * D)
    g_kv_ada_b = dmod_sum[2 * N_MOD * D:]
    g_ada_w = []
    for l in range(2):
        cols = lax.dynamic_slice_in_dim(dmod_all[:, l * N_MOD * D:(l + 1) * N_MOD * D], chip * n_ada, n_ada, axis=1)
        g_ada_w.append(mm(silu_all, cols, "tn", "ada_w_grad"))
    g_ada_w = jnp.stack(g_ada_w)
    kv_cols = lax.dynamic_slice_in_dim(dmod_all[:, 2 * N_MOD * D:], chip * n_kv, n_kv, axis=1)
    g_kv_ada_w = mm(silu_all, kv_cols, "tn", "kv_ada_w_grad")

    def shard(v, width):
        return lax.dynamic_slice_in_dim(v, chip * width, width, axis=v.ndim - 1)

    Dq = D // 4
    g_norm_g = shard(s_norm_g.reshape(2, 3, D), Dq)
    g_conv_b_pw1 = shard(s_b_pw1, 2 * D // 4)
    g_conv_w_dw = shard(s_w_dw, Dq)[None]
    g_conv_b_dw = shard(s_b_dw, Dq)
    g_conv_ln_g = shard(s_ln_g, Dq)
    g_conv_ln_b = shard(s_ln_b, Dq)
    g_conv_b_pw2 = shard(s_b_pw2, Dq)

    grads = [g_ada_w, g_ada_b, g_norm_g, g_ffn_w13, g_ffn_w2, g_conv_w_pw1, g_conv_b_pw1, g_conv_w_dw, g_conv_b_dw,
             g_conv_ln_g, g_conv_ln_b, g_conv_w_pw2, g_conv_b_pw2, g_kv_ada_w, g_kv_ada_b, s_kv_norm_g[0], g_w_kv_a,
             s_kv_a_g[0], g_w_kv_b, g_w_q_a, s_q_a_g, g_w_q_b, g_w_o, s_final_g[0]]
    weights = [ada_w, ada_b, norm_g, ffn_w13, ffn_w2, conv_w_pw1, conv_b_pw1, conv_w_dw, conv_b_dw, conv_ln_g,
               conv_ln_b, conv_w_pw2, conv_b_pw2, kv_ada_w, kv_ada_b, kv_norm_g, w_kv_a, kv_a_norm_g, w_kv_b, w_q_a,
               q_a_norm_g, w_q_b, w_o, final_norm_g]
    ms = [m_ada_w, m_ada_b, m_norm_g, m_ffn_w13, m_ffn_w2, m_conv_w_pw1, m_conv_b_pw1, m_conv_w_dw, m_conv_b_dw,
          m_conv_ln_g, m_conv_ln_b, m_conv_w_pw2, m_conv_b_pw2, m_kv_ada_w, m_kv_ada_b, m_kv_norm_g, m_w_kv_a,
          m_kv_a_norm_g, m_w_kv_b, m_w_q_a, m_q_a_norm_g, m_w_q_b, m_w_o, m_final_norm_g]
    vs = [v_ada_w, v_ada_b, v_norm_g, v_ffn_w13, v_ffn_w2, v_conv_w_pw1, v_conv_b_pw1, v_conv_w_dw, v_conv_b_dw,
          v_conv_ln_g, v_conv_ln_b, v_conv_w_pw2, v_conv_b_pw2, v_kv_ada_w, v_kv_ada_b, v_kv_norm_g, v_w_kv_a,
          v_kv_a_norm_g, v_w_kv_b, v_w_q_a, v_q_a_norm_g, v_w_q_b, v_w_o, v_final_norm_g]
    grads = [g.reshape(w.shape) for g, w in zip(grads, weights)]
    deltas, new_m, new_v = [], [], []
    for w, g, m, v in zip(weights, grads, ms, vs):
        d, nm, nv = adamw(w, g, m, v)
        deltas.append(d)
        new_m.append(nm)
        new_v.append(nv)
    return (loss, grad_x, *grads, *deltas, *new_m, *new_v)
```

```python
import jax
import jax.numpy as jnp
from jax import lax
from jax.experimental import pallas as pl
from jax.experimental.pallas import tpu as pltpu

F32 = jnp.float32
BF16 = jnp.bfloat16
MESH = pl.DeviceIdType.MESH

N_HEADS = 16
QK_NOPE = 64
QK_ROPE = 32
V_HEAD = 64
KV_LORA = 256
CONV_WIDTH = 31
ROPE_THETA = 10000.0
EPS = 1e-6
N_MOD = 9
HEAD_PAD = 128
ATTN_TILE = 512
CONV_HALO = 32

ADAM_LR = 0.001
ADAM_B1 = 0.9
ADAM_B2 = 0.999
ADAM_EPS = 1e-08
ADAM_WD = 0.01
ADAM_STEP = 10

VMEM_LIMIT_BYTES = 56 * 2 ** 20
ROW_TILE_BUDGET = 10 * 2 ** 20
MM_VMEM_BUDGET = 40 * 2 ** 20
LANES = 128
F32_TILE = 8 * LANES
NEG = float(jnp.finfo(jnp.float32).min)
LOG2_E = 1.4426950408889634


def _tile(n, prefs):
    for t in prefs:
        if n % t == 0:
            return t
    return n


def _params(sem):
    return pltpu.CompilerParams(dimension_semantics=sem, vmem_limit_bytes=VMEM_LIMIT_BYTES)


def _mm_tiles(M, N, K, mode, a_bytes, b_bytes, o_bytes):
    if mode == "tn":
        tk_opts = [t for t in (2048, 1024, 512, 256, 128) if K % t == 0] or [K]
        tm_opts = ([M] if M <= 2816 else []) + [t for t in (1024, 512, 256, 128) if M % t == 0 and t < M]
    else:
        tk_opts = [K]
        tm_opts = [t for t in (1024, 512, 256, 128) if M % t == 0] or [M]
    tn_opts = [t for t in (1408, 1024, 512, 384, 256, 128) if N % t == 0] or [N]

    def need(tm, tn, tk):
        blocks = 2 * (tm * tk * a_bytes + tk * tn * b_bytes + tm * tn * o_bytes)
        return blocks + (tm * tn * 4 if mode == "tn" else 0)

    tk_floor = next((t for t in tk_opts if t <= 512), tk_opts[-1])
    for tm in tm_opts:
        for tn in tn_opts:
            if need(tm, tn, tk_floor) <= MM_VMEM_BUDGET:
                return tm, tn, next(tk for tk in tk_opts if need(tm, tn, tk) <= MM_VMEM_BUDGET)
    return tm_opts[-1], tn_opts[-1], tk_opts[-1]


def mm(a, b, mode, name, out_dtype=F32, bias=None):
    if mode == "nn":
        (M, K), (K2, N) = a.shape, b.shape
        dims = (((1,), (0,)), ((), ()))
    elif mode == "nt":
        (M, K), (N, K2) = a.shape, b.shape
        dims = (((1,), (1,)), ((), ()))
    else:
        (K, M), (K2, N) = a.shape, b.shape
        dims = (((0,), (0,)), ((), ()))
    assert K == K2, (a.shape, b.shape, mode)
    tm, tn, tk = _mm_tiles(M, N, K, mode, a.dtype.itemsize, b.dtype.itemsize, jnp.dtype(out_dtype).itemsize)
    nk = K // tk
    if mode == "tn":
        a_spec = pl.BlockSpec((tk, tm), lambda i, j, k: (k, i))
        b_spec = pl.BlockSpec((tk, tn), lambda i, j, k: (k, j))
    elif mode == "nn":
        a_spec = pl.BlockSpec((tm, tk), lambda i, j, k: (i, k))
        b_spec = pl.BlockSpec((tk, tn), lambda i, j, k: (k, j))
    else:
        a_spec = pl.BlockSpec((tm, tk), lambda i, j, k: (i, k))
        b_spec = pl.BlockSpec((tn, tk), lambda i, j, k: (j, k))
    in_specs = [a_spec, b_spec]
    operands = [a, b]
    if bias is not None:
        in_specs.append(pl.BlockSpec((1, tn), lambda i, j, k: (0, j)))
        operands.append(bias)
    has_bias = bias is not None

    def body(*refs):
        a_ref, b_ref = refs[0], refs[1]
        bias_ref = refs[2] if has_bias else None
        o_ref = refs[3] if has_bias else refs[2]
        prod = lax.dot_general(a_ref[...].astype(BF16), b_ref[...].astype(BF16), dims,
                               preferred_element_type=F32)
        if nk == 1:
            if has_bias:
                prod = prod + bias_ref[...]
            o_ref[...] = prod.astype(o_ref.dtype)
        else:
            acc_ref = refs[-1]
            k = pl.program_id(2)

            @pl.when(k == 0)
            def _():
                acc_ref[...] = jnp.zeros_like(acc_ref)

            acc_ref[...] += prod

            @pl.when(k == nk - 1)
            def _():
                out = acc_ref[...]
                if has_bias:
                    out = out + bias_ref[...]
                o_ref[...] = out.astype(o_ref.dtype)

    return pl.pallas_call(
        body, name=name,
        grid=(M // tm, N // tn, nk),
        in_specs=in_specs,
        out_specs=pl.BlockSpec((tm, tn), lambda i, j, k: (i, j)),
        out_shape=jax.ShapeDtypeStruct((M, N), out_dtype),
        scratch_shapes=[pltpu.VMEM((tm, tn), F32)] if nk > 1 else [],
        compiler_params=_params(("parallel", "parallel", "arbitrary")),
    )(*operands)


def mm_fused(a, b, mode, name, tn, epi, epi_outs, pro=None, pro_rows=(), pro_vecs=(), pro_out=False, n_pro_sums=0,
             epi_rows=(), epi_vecs=(), b_blocks=None, n_cols=None):
    M, K = a.shape
    if b_blocks is not None:
        n_b, N = len(b_blocks), n_cols
    else:
        n_b = b.shape[0] if b.ndim == 3 else 1
        N = b.shape[-1] if mode == "nn" else b.shape[0]
    dims = (((1,), (0,)), ((), ())) if mode == "nn" else (((1,), (1,)), ((), ()))
    nj = N // tn
    epi_outs = [o if len(o) == 3 else (*o, None) for o in epi_outs]
    row_bytes = 2 * (K * a.dtype.itemsize + sum(K * r.dtype.itemsize for r in pro_rows) + (2 * K if pro_out else 0)
                     + sum(w * r.dtype.itemsize * (r.shape[0] if r.ndim == 3 else 1) for r, w in epi_rows)
                     + sum(w * jnp.dtype(dt).itemsize * (L or 1) for w, dt, L in epi_outs)
                     ) + (2 * K if pro is not None else 0)
    fixed = 2 * n_b * K * tn * b.dtype.itemsize
    tm = next((t for t in (1024, 512, 256, 128) if M % t == 0 and t * row_bytes + fixed <= MM_VMEM_BUDGET), M)
    row = lambda i, j: (i, 0)
    tile = lambda i, j: (i, j)
    stack = lambda i, j: (0, i, j)
    in_specs = [pl.BlockSpec((tm, K), row)] + [pl.BlockSpec((tm, K), row) for _ in pro_rows]
    in_specs += [pl.BlockSpec(v.shape, lambda i, j: (0, 0)) for v in pro_vecs]
    if b_blocks is not None:
        in_specs += [pl.BlockSpec(shape, imap) for shape, imap in b_blocks]
    elif b.ndim == 3:
        in_specs += [pl.BlockSpec((None, K, tn), lambda i, j, h=h: (h, 0, j)) for h in range(n_b)]
    elif mode == "nn":
        in_specs += [pl.BlockSpec((K, tn), lambda i, j: (0, j))]
    else:
        in_specs += [pl.BlockSpec((tn, K), lambda i, j: (j, 0))]
    in_specs += [pl.BlockSpec((r.shape[0], tm, w), stack) if r.ndim == 3 else pl.BlockSpec((tm, w), tile)
                 for r, w in epi_rows]
    in_specs += [pl.BlockSpec((1, tn), lambda i, j: (0, j)) for _ in epi_vecs]
    out_specs, out_shape = [], []
    if pro_out:
        out_specs.append(pl.BlockSpec((tm, K), row))
        out_shape.append(jax.ShapeDtypeStruct((M, K), BF16))
    for _ in range(n_pro_sums):
        out_specs.append(pl.BlockSpec((1, K), lambda i, j: (0, 0)))
        out_shape.append(jax.ShapeDtypeStruct((1, K), F32))
    for w, dt, L in epi_outs:
        out_specs.append(pl.BlockSpec((tm, w), tile) if L is None else pl.BlockSpec((L, tm, w), stack))
        out_shape.append(jax.ShapeDtypeStruct((M, nj * w) if L is None else (L, M, nj * w), dt))
    n_pr, n_pv, n_er, n_ev = len(pro_rows), len(pro_vecs), len(epi_rows), len(epi_vecs)
    n_a = 1 + n_pr + n_pv
    n_in = n_a + n_b + n_er + n_ev
    n_po = 1 if pro_out else 0

    def body(*refs):
        i, j = pl.program_id(0), pl.program_id(1)
        a_ref = refs[0]
        outs = refs[n_in:]
        if pro is not None:
            lhs_ref = refs[-1]

            @pl.when(j == 0)
            def _():
                res = pro(*[r[...] for r in refs[:1 + n_pr + n_pv]])
                if not isinstance(res, (tuple, list)):
                    res = (res,)
                lhs_ref[...] = res[0]
                if pro_out:
                    outs[0][...] = res[0]
                for s_ref, val in zip(outs[n_po:n_po + n_pro_sums], res[1:]):
                    part = jnp.sum(val.astype(F32), axis=0, keepdims=True)

                    @pl.when(i == 0)
                    def _(s_ref=s_ref, part=part):
                        s_ref[...] = part

                    @pl.when(i != 0)
                    def _(s_ref=s_ref, part=part):
                        s_ref[...] += part

            lhs = lhs_ref[...]
        else:
            lhs = a_ref[...].astype(BF16)
        accs = [lax.dot_general(lhs, b_ref[...].astype(BF16), dims, preferred_element_type=F32)
                for b_ref in refs[n_a:n_a + n_b]]
        res = epi(*accs, *[r[...] for r in refs[n_a + n_b:n_in]])
        if not isinstance(res, (tuple, list)):
            res = (res,)
        for o_ref, val in zip(outs[n_po + n_pro_sums:], res):
            if isinstance(val, (tuple, list)):
                for h, part in enumerate(val):
                    o_ref[h] = part.astype(o_ref.dtype)
            else:
                o_ref[...] = val.astype(o_ref.dtype)

    return pl.pallas_call(
        body, name=name,
        grid=(M // tm, nj),
        in_specs=in_specs, out_specs=out_specs, out_shape=out_shape,
        scratch_shapes=[pltpu.VMEM((tm, K), BF16)] if pro is not None else [],
        compiler_params=_params(("arbitrary", "arbitrary")),
    )(a, *pro_rows, *pro_vecs, *([b] * n_b), *[r for r, _ in epi_rows], *epi_vecs)


def rowwise(fn, rows, vecs, outs, sums, name, tm=None):
    norm = [(r, r.shape[1], 0) if not isinstance(r, tuple) else r for r in rows]
    S = norm[0][0].shape[0]
    if tm is None:
        tm = _row_tile(S, sum(w * r.dtype.itemsize for r, w, _ in norm)
                       + sum(n * jnp.dtype(dt).itemsize for n, dt in outs))
    n_rows, n_vecs, n_outs, n_sums = len(norm), len(vecs), len(outs), len(sums)
    in_specs = [pl.BlockSpec((tm, w), lambda i, cb=cb: (i, cb)) for _, w, cb in norm]
    in_specs += [pl.BlockSpec(v.shape, lambda i: (0, 0)) for v in vecs]
    out_specs = [pl.BlockSpec((tm, n), lambda i: (i, 0)) for n, _ in outs]
    out_specs += [pl.BlockSpec((1, n), lambda i: (0, 0)) for n in sums]
    out_shape = [jax.ShapeDtypeStruct((S, n), dt) for n, dt in outs]
    out_shape += [jax.ShapeDtypeStruct((1, n), F32) for n in sums]

    def body(*refs):
        ins = [r[...] for r in refs[:n_rows + n_vecs]]
        res = fn(*ins)
        if not isinstance(res, (tuple, list)):
            res = (res,)
        out_refs = refs[n_rows + n_vecs:]
        for o_ref, val in zip(out_refs[:n_outs], res[:n_outs]):
            o_ref[...] = val.astype(o_ref.dtype)
        if n_sums:
            i = pl.program_id(0)
            for s_ref, val in zip(out_refs[n_outs:], res[n_outs:]):
                part = jnp.sum(val.astype(F32), axis=0, keepdims=True)

                @pl.when(i == 0)
                def _(s_ref=s_ref, part=part):
                    s_ref[...] = part

                @pl.when(i != 0)
                def _(s_ref=s_ref, part=part):
                    s_ref[...] += part

    res = pl.pallas_call(
        body, name=name,
        grid=(S // tm,),
        in_specs=in_specs, out_specs=out_specs, out_shape=out_shape,
        compiler_params=_params(("arbitrary",) if n_sums else ("parallel",)),
    )(*[r for r, _, _ in norm], *vecs)
    return res


def _sigmoid(x):
    return jax.nn.sigmoid(x)


def _rms(x):
    r = lax.rsqrt(jnp.mean(x * x, axis=-1, keepdims=True) + EPS)
    return x * r, r


def _rms_bwd(xhat, r, dxhat):
    return r * (dxhat - xhat * jnp.mean(dxhat * xhat, axis=-1, keepdims=True))


def norm_mod(h, g, sh, sc, name):
    def f(h, g, sh, sc):
        xhat, _ = _rms(h)
        return ((xhat * g) * (1 + sc) + sh).astype(BF16)
    return rowwise(f, [h], [g, sh, sc], [(h.shape[1], BF16)], [], name)[0]


def norm_mod_bwd(h, dhn, dh_out, g, sc, name):
    D = h.shape[1]

    def f(h, dhn, dres, g, sc):
        xhat, r = _rms(h)
        dxn = dhn * (1 + sc)
        return _rms_bwd(xhat, r, dxn * g) + dres, dhn, dhn * (xhat * g), dxn * xhat

    return rowwise(f, [h, dhn, dh_out], [g, sc], [(D, F32)], [D, D, D], name)


def residual(h, y, gate, coef, name, bias=None):
    D = h.shape[1]
    if bias is None:
        def f(h, y, gate):
            return h + (coef * gate) * y
        return rowwise(f, [h, y], [gate], [(D, F32)], [], name)[0], y

    def fb(h, y, gate, bias):
        yb = y + bias
        return h + (coef * gate) * yb, yb
    return rowwise(fb, [h, y], [gate, bias], [(D, F32), (D, F32)], [], name)


def residual_bwd(dh_out, y, gate, coef, name, with_bias_sum=False):
    D = y.shape[1]

    def f(dh, y, gate):
        dy = (coef * gate) * dh
        res = (dy.astype(BF16), coef * dh * y)
        return res + ((dy,) if with_bias_sum else ())
    return rowwise(f, [dh_out, y], [gate], [(D, BF16)], [D, D] if with_bias_sum else [D], name)


def ffn_w13_dx(dab, gw13, l, i):
    _, S, F = dab.shape
    D, C = gw13.shape[3:]
    tm = _tile(S, (1024, 512, 256, 128))
    nt = (((1,), (1,)), ((), ()))

    def body(a_ref, b_ref, o_ref, acc_ref):
        k = pl.program_id(1)
        prod = lax.dot_general(a_ref[...], b_ref[...], nt, preferred_element_type=F32)

        @pl.when(k == 0)
        def _():
            acc_ref[...] = prod

        @pl.when((k > 0) & (k < 3))
        def _():
            acc_ref[...] += prod

        @pl.when(k == 3)
        def _():
            o_ref[...] = acc_ref[...] + prod

    return pl.pallas_call(
        body, name="ffn_w13_dx",
        grid=(S // tm, 4),
        in_specs=[pl.BlockSpec((None, tm, C), lambda r, k: (k // 2, r, k % 2)),
                  pl.BlockSpec((None, None, None, D, C), lambda r, k: (k, l, i, 0, 0))],
        out_specs=pl.BlockSpec((tm, D), lambda r, k: (r, 0)),
        out_shape=jax.ShapeDtypeStruct((S, D), F32),
        scratch_shapes=[pltpu.VMEM((tm, D), F32)],
        compiler_params=_params(("parallel", "arbitrary")),
    )(dab, gw13)


def ffn_w13_grad(hn, dab):
    S, D = hn.shape
    F = dab.shape[2]
    C = F // 2
    tk = next(t for t in (2048, 1024, 512, 256, 128) if S % t == 0)
    tn_dims = (((0,), (0,)), ((), ()))
    nk = S // tk

    def body(a_ref, b_ref, o_ref, acc_ref):
        k = pl.program_id(1)

        @pl.when(k == 0)
        def _():
            acc_ref[...] = jnp.zeros_like(acc_ref)

        acc_ref[...] += lax.dot_general(a_ref[...], b_ref[...], tn_dims, preferred_element_type=F32)

        @pl.when(k == nk - 1)
        def _():
            o_ref[...] = acc_ref[...]

    return pl.pallas_call(
        body, name="ffn_w13_dw",
        grid=(4, nk),
        in_specs=[pl.BlockSpec((tk, D), lambda j, k: (k, 0)),
                  pl.BlockSpec((None, tk, C), lambda j, k: (j // 2, k, j % 2))],
        out_specs=pl.BlockSpec((None, D, C), lambda j, k: (j, 0, 0)),
        out_shape=jax.ShapeDtypeStruct((4, D, C), F32),
        scratch_shapes=[pltpu.VMEM((D, C), F32)],
        compiler_params=_params(("parallel", "arbitrary")),
    )(hn, dab)


def ffn_fwd(h, g, sh, sc, gate, gw13, l, i, w2):
    F, D = w2.shape
    C = F // 2

    def norm(h, g, sh, sc):
        xhat, _ = _rms(h)
        return ((xhat * g) * (1 + sc) + sh).astype(BF16)

    def act(a, b):
        sig = _sigmoid(a)
        sa = a * sig
        return (b * (sig + sa * (1 - sig)), sa), sa * b
    blocks = [((None, None, None, D, C), lambda r, j, half=half: (2 * half + j, l, i, 0, 0)) for half in range(2)]
    hn, dt_dab, t = mm_fused(h, gw13, "nn", "ffn_w13", C, act, [(C, BF16, 2), (C, BF16)],
                             pro=norm, pro_vecs=[g, sh, sc], pro_out=True, b_blocks=blocks, n_cols=F)

    def res(acc, h, gate):
        return h + (0.5 * gate) * acc, acc
    h_out, y = mm_fused(t, w2, "nn", "ffn_w2", D, res, [(D, F32), (D, F32)], epi_rows=[(h, D)], epi_vecs=[gate])
    return h_out, (h, hn, dt_dab, t, y)


def ffn_bwd(dh_out, saved, g, sc, gate, gw13, l, i, w2):
    h, hn, dt_dab, t, y = saved
    F, D = w2.shape
    C = F // 2

    def scale(dh, y, gate):
        return ((0.5 * gate) * dh).astype(BF16), 0.5 * dh * y

    def act_bwd(dt, f):
        return ((dt * f[0].astype(F32), dt * f[1].astype(F32)),)
    dy, d_gate, dab = mm_fused(dh_out, w2, "nt", "ffn_w2_dx", C, act_bwd, [(C, BF16, 2)],
                               pro=scale, pro_rows=[y], pro_vecs=[gate], pro_out=True, n_pro_sums=1,
                               epi_rows=[(dt_dab, C)])
    dw2 = mm(t, dy, "tn", "ffn_w2_dw")
    dw13 = ffn_w13_grad(hn, dab)
    dhn = ffn_w13_dx(dab, gw13, l, i)
    dh_in, d_sh, d_sc, d_g = norm_mod_bwd(h, dhn, dh_out, g, sc, "norm_mod_bwd")
    return dh_in, (d_sh, d_sc, d_gate, d_g), dw13, dw2


def _shifted(xbuf, n):
    return [xbuf] + [pltpu.roll(xbuf, n - b, 0) for b in range(1, 8)]


def conv_fwd(u, w_dw, b_dw, ln_g, ln_b):
    S, D = u.shape
    tm = _tile(S, (256, 128))
    rc = 32
    first_tap = CONV_HALO - (CONV_WIDTH - 1)
    w = jnp.concatenate([w_dw, jnp.zeros((CONV_HALO - CONV_WIDTH, D), F32)], axis=0)

    def body(cur_ref, prev_ref, w_ref, b_ref, g_ref, beta_ref, z_ref, s_ref):
        i = pl.program_id(0)
        prev = jnp.where(i == 0, jnp.zeros((CONV_HALO, D), F32), prev_ref[...])
        xs = _shifted(jnp.concatenate([prev, cur_ref[...]], axis=0), tm + CONV_HALO)
        for c0 in range(0, tm, rc):
            acc = jnp.zeros((rc, D), F32)
            for k in range(CONV_WIDTH):
                off = first_tap + k
                a8, b = off // 8 * 8, off % 8
                acc = acc + w_ref[k:k + 1, :] * xs[b][c0 + a8:c0 + a8 + rc, :]
            z_ref[c0:c0 + rc, :] = acc + b_ref[...]
        z = z_ref[...]
        mu = jnp.mean(z, axis=-1, keepdims=True)
        zc = z - mu
        r = lax.rsqrt(jnp.mean(zc * zc, axis=-1, keepdims=True) + EPS)
        un = zc * r * g_ref[...] + beta_ref[...]
        s_ref[...] = (un * _sigmoid(un)).astype(BF16)

    nb = tm // CONV_HALO
    vec = pl.BlockSpec((1, D), lambda i: (0, 0))
    return pl.pallas_call(
        body, name="conv_fwd",
        grid=(S // tm,),
        in_specs=[pl.BlockSpec((tm, D), lambda i: (i, 0)),
                  pl.BlockSpec((CONV_HALO, D), lambda i: (jnp.maximum(i * nb - 1, 0), 0)),
                  pl.BlockSpec((CONV_HALO, D), lambda i: (0, 0)), vec, vec, vec],
        out_specs=[pl.BlockSpec((tm, D), lambda i: (i, 0)), pl.BlockSpec((tm, D), lambda i: (i, 0))],
        out_shape=[jax.ShapeDtypeStruct((S, D), F32), jax.ShapeDtypeStruct((S, D), BF16)],
        compiler_params=_params(("parallel",)),
    )(u, u, w, b_dw, ln_g, ln_b)


def conv_bwd(dz, u, w_dw):
    S, D = u.shape
    tm = _tile(S, (256, 128))
    rc = 32
    first_tap = CONV_HALO - (CONV_WIDTH - 1)
    w = jnp.concatenate([w_dw, jnp.zeros((CONV_HALO - CONV_WIDTH, D), F32)], axis=0)
    n_tiles = S // tm
    nb = tm // CONV_HALO

    def body(dz_ref, dzn_ref, u_ref, up_ref, w_ref, du_ref, dw_ref):
        i = pl.program_id(0)
        nxt = jnp.where(i == n_tiles - 1, jnp.zeros((CONV_HALO, D), F32), dzn_ref[...])
        dzs = _shifted(jnp.concatenate([dz_ref[...], nxt], axis=0), tm + CONV_HALO)
        for c0 in range(0, tm, rc):
            acc = jnp.zeros((rc, D), F32)
            for m in range(CONV_WIDTH):
                a8, b = m // 8 * 8, m % 8
                acc = acc + w_ref[CONV_WIDTH - 1 - m:CONV_WIDTH - m, :] * dzs[b][c0 + a8:c0 + a8 + rc, :]
            du_ref[c0:c0 + rc, :] = acc
        prev = jnp.where(i == 0, jnp.zeros((CONV_HALO, D), F32), up_ref[...])
        us = _shifted(jnp.concatenate([prev, u_ref[...]], axis=0), tm + CONV_HALO)
        dz = dz_ref[...]

        @pl.when(i == 0)
        def _():
            dw_ref[...] = jnp.zeros_like(dw_ref)

        for k in range(CONV_WIDTH):
            off = first_tap + k
            a8, b = off // 8 * 8, off % 8
            dw_ref[k:k + 1, :] += jnp.sum(dz * us[b][a8:a8 + tm, :], axis=0, keepdims=True)

    last_blk = S // CONV_HALO - 1
    du, dw = pl.pallas_call(
        body, name="conv_bwd",
        grid=(n_tiles,),
        in_specs=[pl.BlockSpec((tm, D), lambda i: (i, 0)),
                  pl.BlockSpec((CONV_HALO, D), lambda i: (jnp.minimum((i + 1) * nb, last_blk), 0)),
                  pl.BlockSpec((tm, D), lambda i: (i, 0)),
                  pl.BlockSpec((CONV_HALO, D), lambda i: (jnp.maximum(i * nb - 1, 0), 0)),
                  pl.BlockSpec((CONV_HALO, D), lambda i: (0, 0))],
        out_specs=[pl.BlockSpec((tm, D), lambda i: (i, 0)), pl.BlockSpec((CONV_HALO, D), lambda i: (0, 0))],
        out_shape=[jax.ShapeDtypeStruct((S, D), F32), jax.ShapeDtypeStruct((CONV_HALO, D), F32)],
        compiler_params=_params(("arbitrary",)),
    )(dz, dz, u, u, w)
    return du, dw[:CONV_WIDTH]


def conv_module_fwd(h, g, sh, sc, gate, p):
    D = h.shape[1]
    hn = norm_mod(h, g, sh, sc, "conv_norm_mod")
    pre = mm(hn, p["w_pw1"], "nn", "conv_pw1")
    ba, bg = p["b_pw1"][:, :D], p["b_pw1"][:, D:]

    def glu(a, gt, ba, bg):
        return (a + ba) * _sigmoid(gt + bg)
    u = rowwise(glu, [(pre, D, 0), (pre, D, 1)], [ba, bg], [(D, F32)], [], "conv_glu")[0]
    z, s = conv_fwd(u, p["w_dw"], p["b_dw"], p["ln_g"], p["ln_b"])
    yraw = mm(s, p["w_pw2"], "nn", "conv_pw2")
    h_out, y = residual(h, yraw, gate, 1.0, "conv_residual", bias=p["b_pw2"])
    return h_out, (h, hn, pre, u, z, s, y)


def conv_module_bwd(dh_out, saved, g, sc, gate, p):
    h, hn, pre, u, z, s, y = saved
    D = h.shape[1]
    dy, d_gate, d_b_pw2 = residual_bwd(dh_out, y, gate, 1.0, "conv_residual_bwd", with_bias_sum=True)
    d_w_pw2 = mm(s, dy, "tn", "conv_pw2_dw")
    ds = mm(dy, p["w_pw2"], "nt", "conv_pw2_dx")

    def ln_bwd(z, ds, g, beta):
        mu = jnp.mean(z, axis=-1, keepdims=True)
        zc = z - mu
        r = lax.rsqrt(jnp.mean(zc * zc, axis=-1, keepdims=True) + EPS)
        xhat = zc * r
        un = xhat * g + beta
        sig = _sigmoid(un)
        d_un = ds * (sig * (1 + un * (1 - sig)))
        dxhat = d_un * g
        dz = r * (dxhat - jnp.mean(dxhat, axis=-1, keepdims=True)
                  - xhat * jnp.mean(dxhat * xhat, axis=-1, keepdims=True))
        return dz, d_un * xhat, d_un, dz
    dz, d_ln_g, d_ln_b, d_b_dw = rowwise(ln_bwd, [z, ds], [p["ln_g"], p["ln_b"]], [(D, F32)], [D, D, D],
                                         "conv_ln_bwd")
    du, d_w_dw = conv_bwd(dz, u, p["w_dw"])
    ba, bg = p["b_pw1"][:, :D], p["b_pw1"][:, D:]

    def glu_bwd(a, gt, du, ba, bg):
        sg = _sigmoid(gt + bg)
        da = du * sg
        dg = du * (a + ba) * (sg * (1 - sg))
        dpre = jnp.concatenate([da, dg], axis=1)
        return dpre.astype(BF16), dpre
    dpre, d_b_pw1 = rowwise(glu_bwd, [(pre, D, 0), (pre, D, 1), du], [ba, bg], [(2 * D, BF16)], [2 * D],
                            "conv_glu_bwd")
    d_w_pw1 = mm(hn, dpre, "tn", "conv_pw1_dw")
    dhn = mm(dpre, p["w_pw1"], "nt", "conv_pw1_dx")
    dh_in, d_sh, d_sc, d_g = norm_mod_bwd(h, dhn, dh_out, g, sc, "norm_mod_bwd")
    grads = dict(w_pw1=d_w_pw1, b_pw1=d_b_pw1, w_dw=d_w_dw, b_dw=d_b_dw, ln_g=d_ln_g, ln_b=d_ln_b,
                 w_pw2=d_w_pw2, b_pw2=d_b_pw2)
    return dh_in, (d_sh, d_sc, d_gate, d_g), grads


def _rope(x, c, s1, s2):
    n = x.shape[1]
    return x * c + pltpu.roll(x, n - QK_ROPE // 2, 1) * s1 + pltpu.roll(x, QK_ROPE // 2, 1) * s2


def _rope_t(dy, c, s1, s2):
    n = dy.shape[1]
    return dy * c + pltpu.roll(dy * s1, QK_ROPE // 2, 1) + pltpu.roll(dy * s2, n - QK_ROPE // 2, 1)


def rope_tables(positions):
    inv_freq = ROPE_THETA ** (-jnp.arange(0, QK_ROPE, 2, dtype=F32) / QK_ROPE)
    ang = positions.astype(F32)[:, None] * inv_freq
    cos, sin = jnp.cos(ang), jnp.sin(ang)
    S = positions.shape[0]
    one = jnp.ones((S, QK_NOPE), F32)
    z16 = jnp.zeros((S, QK_ROPE // 2), F32)
    zn = jnp.zeros((S, QK_NOPE), F32)
    zt = jnp.zeros((S, HEAD_PAD - QK_NOPE - QK_ROPE), F32)
    c = jnp.concatenate([one, cos, cos, zt], axis=1)
    s1 = jnp.concatenate([zn, -sin, z16, zt], axis=1)
    s2 = jnp.concatenate([zn, z16, sin, zt], axis=1)
    return c, s1, s2


def attn_fwd(qr, kv, kpe, n_heads):
    S = qr.shape[0]
    H = n_heads
    tk = _tile(S, (ATTN_TILE,))
    nk = S // tk
    w = 2 if nk % 2 == 0 else 1
    tq = w * tk
    c2 = (QK_NOPE + QK_ROPE) ** -0.5 * LOG2_E
    nt = (((1,), (1,)), ((), ()))

    assert V_HEAD < HEAD_PAD
    ones_row = HEAD_PAD - 1

    def body(q_ref, k_ref, v_ref, kpe_ref, o_ref, lse_ref, kf_ref, vt_ref, m_ref, acc_ref):
        qi = pl.program_id(1)
        feature = lax.broadcasted_iota(jnp.int32, (HEAD_PAD, tk), 0)

        @pl.when(qi == 0)
        def _():
            kf_ref[...] = k_ref[...] + kpe_ref[...]
            for c in range(nk):
                vt = jnp.transpose(v_ref[c * tk:(c + 1) * tk, :].astype(F32))
                vt_ref[c] = jnp.where(feature == ones_row, 1.0, vt).astype(BF16)

        q = q_ref[...]
        m_ref[...] = jnp.full((1, tq), -jnp.inf, F32)
        acc_ref[...] = jnp.zeros((HEAD_PAD, tq), F32)

        def tile(j, first_visible):
            k = kf_ref[pl.ds(pl.multiple_of(j * tk, tk), tk), :]
            t = lax.dot_general(k, q, nt, preferred_element_type=F32) * c2
            if first_visible is not None:
                krow = lax.broadcasted_iota(jnp.int32, (tk, tq), 0)
                qcol = lax.broadcasted_iota(jnp.int32, (tk, tq), 1)
                t = jnp.where(krow + first_visible <= qcol, t, NEG)
            m_old = m_ref[...]
            m_new = jnp.maximum(m_old, jnp.max(t, axis=0, keepdims=True))
            alpha = jnp.exp2(m_old - m_new)
            p = jnp.exp2(t - m_new)
            acc_ref[...] = alpha * acc_ref[...] + jnp.dot(vt_ref[j], p.astype(BF16), preferred_element_type=F32)
            m_ref[...] = m_new

        def unmasked(j, carry):
            tile(j, None)
            return carry

        lax.fori_loop(0, w * qi, unmasked, 0)
        for u in range(w):
            tile(w * qi + u, u * tk)
        acc = acc_ref[...]
        l = acc_ref[ones_row:ones_row + 1, :]
        out_feature = lax.broadcasted_iota(jnp.int32, (HEAD_PAD, tq), 0)
        o_ref[...] = jnp.transpose(jnp.where(out_feature == ones_row, 0.0, acc / l))
        lse = m_ref[...] + jnp.log(l) * LOG2_E
        for u in range(w):
            lse_ref[u] = lse[:, u * tk:(u + 1) * tk]

    return pl.pallas_call(
        body, name="attn_fwd",
        grid=(H, S // tq),
        in_specs=[pl.BlockSpec((tq, HEAD_PAD), lambda h, i: (i, h)),
                  pl.BlockSpec((S, HEAD_PAD), lambda h, i: (0, h)),
                  pl.BlockSpec((S, HEAD_PAD), lambda h, i: (0, H + h)),
                  pl.BlockSpec((S, HEAD_PAD), lambda h, i: (0, 0))],
        out_specs=[pl.BlockSpec((tq, HEAD_PAD), lambda h, i: (i, h)),
                   pl.BlockSpec((None, w, 1, tk), lambda h, i: (h, i, 0, 0))],
        out_shape=[jax.ShapeDtypeStruct((S, H * HEAD_PAD), F32), jax.ShapeDtypeStruct((H, nk, 1, tk), F32)],
        scratch_shapes=[pltpu.VMEM((S, HEAD_PAD), BF16), pltpu.VMEM((nk, HEAD_PAD, tk), BF16),
                        pltpu.VMEM((1, tq), F32), pltpu.VMEM((HEAD_PAD, tq), F32)],
        compiler_params=_params(("parallel", "arbitrary")),
    )(qr, kv, kv, kpe)


def attn_delta(o, do, n_heads):
    S = o.shape[0]
    H = n_heads
    tq = _tile(S, (ATTN_TILE,))
    nq = S // tq

    def body(o_ref, do_ref, d_ref):
        for c in range(nq):
            rows = slice(c * tq, (c + 1) * tq)
            prod = o_ref[rows, :] * do_ref[rows, :].astype(F32)
            d_ref[c] = jnp.sum(jnp.transpose(prod), axis=0, keepdims=True)

    return pl.pallas_call(
        body, name="attn_delta",
        grid=(H,),
        in_specs=[pl.BlockSpec((S, HEAD_PAD), lambda h: (0, h)), pl.BlockSpec((S, HEAD_PAD), lambda h: (0, h))],
        out_specs=pl.BlockSpec((None, nq, 1, tq), lambda h: (h, 0, 0, 0)),
        out_shape=jax.ShapeDtypeStruct((H, nq, 1, tq), F32),
        compiler_params=_params(("parallel",)),
    )(o, do)


def attn_bwd(qr, kv, kpe, do, lse2, delta, n_heads):
    S = qr.shape[0]
    H = n_heads
    tk = _tile(S, (ATTN_TILE,))
    nk = S // tk
    w = 2 if nk % 2 == 0 else 1
    tq = w * tk
    nq = S // tq
    scale = (QK_NOPE + QK_ROPE) ** -0.5
    c2 = scale * LOG2_E
    nt = (((1,), (1,)), ((), ()))
    lse2 = lse2.reshape(H, nq, 1, tq)
    delta4 = delta.reshape(H, nq, 1, tq)

    def body(k_ref, v_ref, kpe_ref, q_ref, do_ref, lse_ref, dl_ref, dq_ref, dk_ref, dv_ref, dka_ref, dva_ref,
             dqt_ref):
        kj = pl.program_id(1)
        k = k_ref[...] + kpe_ref[...]
        kt = jnp.transpose(k.astype(F32)).astype(BF16)
        v = v_ref[...]

        @pl.when(kj == 0)
        def _():
            dqt_ref[...] = jnp.zeros_like(dqt_ref)

        dka_ref[...] = jnp.zeros_like(dka_ref)
        dva_ref[...] = jnp.zeros_like(dva_ref)

        def tile(i, masked):
            start = pl.multiple_of(i * tq, tq)
            q = q_ref[pl.ds(start, tq), :]
            do = do_ref[pl.ds(start, tq), :]
            t = lax.dot_general(k, q, nt, preferred_element_type=F32) * c2
            if masked:
                krow = lax.broadcasted_iota(jnp.int32, (tk, tq), 0)
                qcol = lax.broadcasted_iota(jnp.int32, (tk, tq), 1)
                t = jnp.where(krow + (kj % w) * tk <= qcol, t, NEG)
            pt = jnp.exp2(t - lse_ref[i])
            dva_ref[...] += jnp.dot(pt.astype(BF16), do, preferred_element_type=F32)
            dpt = lax.dot_general(v, do, nt, preferred_element_type=F32)
            dst = (pt * (dpt - dl_ref[i]) * scale).astype(BF16)
            dka_ref[...] += jnp.dot(dst, q, preferred_element_type=F32)
            dqt_ref[i] += jnp.dot(kt, dst, preferred_element_type=F32)

        tile(kj // w, True)

        def unmasked(i, carry):
            tile(i, False)
            return carry

        lax.fori_loop(kj // w + 1, nq, unmasked, 0)
        dk_ref[...] = dka_ref[...]
        dv_ref[...] = dva_ref[...]

        @pl.when(kj == nk - 1)
        def _():
            for c in range(nq):
                dq_ref[c * tq:(c + 1) * tq, :] = jnp.transpose(dqt_ref[c])

    blk = pl.BlockSpec((tk, HEAD_PAD), lambda h, j: (j, h))
    whole = pl.BlockSpec((S, HEAD_PAD), lambda h, j: (0, h))
    stat = pl.BlockSpec((None, nq, 1, tq), lambda h, j: (h, 0, 0, 0))
    shp = jax.ShapeDtypeStruct((S, H * HEAD_PAD), F32)
    return pl.pallas_call(
        body, name="attn_bwd",
        grid=(H, nk),
        in_specs=[blk, pl.BlockSpec((tk, HEAD_PAD), lambda h, j: (j, H + h)),
                  pl.BlockSpec((tk, HEAD_PAD), lambda h, j: (j, 0)), whole, whole, stat, stat],
        out_specs=[whole, blk, blk],
        out_shape=[shp, shp, shp],
        scratch_shapes=[pltpu.VMEM((tk, HEAD_PAD), F32), pltpu.VMEM((tk, HEAD_PAD), F32),
                        pltpu.VMEM((nq, HEAD_PAD, tq), F32)],
        compiler_params=_params(("parallel", "arbitrary")),
    )(kv, kv, kpe, qr, do, lse2, delta4)


def _pad_heads(w, width):
    R = w.shape[0]
    w3 = w.reshape(R, -1, width)
    return jnp.pad(w3, ((0, 0), (0, 0), (0, HEAD_PAD - width))).reshape(R, -1)


def _unpad_heads(w, width):
    R = w.shape[0]
    return w.reshape(R, -1, HEAD_PAD)[:, :, :width].reshape(R, -1)


def mla_pad_weights(p):
    H = N_HEADS
    w_q_b = _pad_heads(p["w_q_b"], QK_NOPE + QK_ROPE)
    kvb = p["w_kv_b"].reshape(KV_LORA, H, QK_NOPE + V_HEAD)
    wk = _pad_heads(kvb[:, :, :QK_NOPE].reshape(KV_LORA, -1), QK_NOPE)
    wv = _pad_heads(kvb[:, :, QK_NOPE:].reshape(KV_LORA, -1), V_HEAD)
    D = p["w_kv_a"].shape[0]
    a = p["w_kv_a"]
    w_kv_a = jnp.concatenate([a[:, :KV_LORA], jnp.zeros((D, QK_NOPE), a.dtype), a[:, KV_LORA:],
                              jnp.zeros((D, HEAD_PAD - QK_NOPE - QK_ROPE), a.dtype)], axis=1)
    wo = p["w_o"].reshape(H, V_HEAD, -1)
    w_o = jnp.pad(wo, ((0, 0), (0, HEAD_PAD - V_HEAD), (0, 0))).reshape(H * HEAD_PAD, -1)
    return dict(w_q_a=p["w_q_a"], w_q_b=w_q_b, w_kv_b=jnp.concatenate([wk, wv], axis=1), w_kv_a=w_kv_a, w_o=w_o)


def mla_kv_fwd(h, g, sh, sc, kv_a_norm_g, pw, tabs):
    hkv = norm_mod(h, g, sh, sc, "kv_norm_mod")
    ckvp = mm(hkv, pw["w_kv_a"], "nn", "kv_a")

    def f(ckv, kpe, c, s1, s2, g):
        xhat, _ = _rms(ckv)
        return (xhat * g).astype(BF16), _rope(kpe, c, s1, s2).astype(BF16)
    ckv_n, kpe_r = rowwise(f, [(ckvp, KV_LORA, 0), (ckvp, HEAD_PAD, KV_LORA // HEAD_PAD), *tabs], [kv_a_norm_g],
                           [(KV_LORA, BF16), (HEAD_PAD, BF16)], [], "kv_a_norm_rope")
    kv = mm(ckv_n, pw["w_kv_b"], "nn", "kv_b", out_dtype=BF16)
    return kv, kpe_r, (h, hkv, ckvp, ckv_n)


def mla_kv_bwd(dh_stream, dk, dv, saved, g, sc, kv_a_norm_g, pw, tabs):
    h, hkv, ckvp, ckv_n = saved
    H = N_HEADS
    lane = jnp.arange(HEAD_PAD)
    pe_mask = ((lane >= QK_NOPE) & (lane < QK_NOPE + QK_ROPE)).astype(F32)[None, :]

    def f(dk, dv, c, s1, s2, mask):
        tot = dk[:, :HEAD_PAD]
        for hh in range(1, H):
            tot = tot + dk[:, hh * HEAD_PAD:(hh + 1) * HEAD_PAD]
        dkpe = _rope_t(tot * mask, c, s1, s2) * mask
        return jnp.concatenate([dk, dv], axis=1).astype(BF16), dkpe
    dkv, dkpe = rowwise(f, [dk, dv, *tabs], [pe_mask], [(2 * H * HEAD_PAD, BF16), (HEAD_PAD, F32)], [],
                        "kv_split_bwd")
    d_w_kv_b = mm(ckv_n, dkv, "tn", "kv_b_dw")
    dckv_n = mm(dkv, pw["w_kv_b"], "nt", "kv_b_dx")

    def f2(ckv, dn, dkpe, g):
        xhat, r = _rms(ckv)
        dx = _rms_bwd(xhat, r, dn * g)
        return jnp.concatenate([dx, dkpe], axis=1).astype(BF16), dn * xhat
    dckvp, d_kv_a_g = rowwise(f2, [(ckvp, KV_LORA, 0), dckv_n, dkpe], [kv_a_norm_g],
                              [(KV_LORA + HEAD_PAD, BF16)], [KV_LORA], "kv_a_norm_bwd")
    d_w_kv_a = mm(hkv, dckvp, "tn", "kv_a_dw")
    dhkv = mm(dckvp, pw["w_kv_a"], "nt", "kv_a_dx")
    dh, d_sh, d_sc, d_g = norm_mod_bwd(h, dhkv, dh_stream, g, sc, "norm_mod_bwd")
    return dh, (d_sh, d_sc, d_g), d_kv_a_g, d_w_kv_a, d_w_kv_b


def mla_fwd(h, g, sh, sc, gate, q_a_norm_g, pw, kv, kpe_r, tabs):
    H = N_HEADS
    hn = norm_mod(h, g, sh, sc, "mla_norm_mod")
    qa = mm(hn, pw["w_q_a"], "nn", "q_a")

    def f(qa, g):
        xhat, _ = _rms(qa)
        return (xhat * g).astype(BF16)
    qa_n = rowwise(f, [qa], [q_a_norm_g], [(qa.shape[1], BF16)], [], "q_a_norm")[0]
    qp = mm(qa_n, pw["w_q_b"], "nn", "q_b")

    def frope(q, c, s1, s2):
        return jnp.concatenate([_rope(q[:, hh * HEAD_PAD:(hh + 1) * HEAD_PAD], c, s1, s2) for hh in range(H)],
                               axis=1).astype(BF16)
    qr = rowwise(frope, [qp, *tabs], [], [(H * HEAD_PAD, BF16)], [], "q_rope")[0]
    o, lse = attn_fwd(qr, kv, kpe_r, H)
    y = mm(o, pw["w_o"], "nn", "w_o")
    h_out, _ = residual(h, y, gate, 1.0, "mla_residual")
    return h_out, (h, hn, qa, qa_n, qr, o, lse, y)


def mla_bwd(dh_out, saved, g, sc, gate, q_a_norm_g, pw, kv, kpe_r, tabs):
    h, hn, qa, qa_n, qr, o, lse, y = saved
    H = N_HEADS
    dy, d_gate = residual_bwd(dh_out, y, gate, 1.0, "mla_residual_bwd")
    d_w_o = mm(o, dy, "tn", "w_o_dw")
    do = mm(dy, pw["w_o"], "nt", "w_o_dx", out_dtype=BF16)
    delta = attn_delta(o, do, H)
    dqr, dk, dv = attn_bwd(qr, kv, kpe_r, do, lse, delta, H)

    def frope_t(dq, c, s1, s2):
        return jnp.concatenate([_rope_t(dq[:, hh * HEAD_PAD:(hh + 1) * HEAD_PAD], c, s1, s2) for hh in range(H)],
                               axis=1).astype(BF16)
    dqp = rowwise(frope_t, [dqr, *tabs], [], [(H * HEAD_PAD, BF16)], [], "q_rope_bwd")[0]
    d_w_q_b = mm(qa_n, dqp, "tn", "q_b_dw")
    dqa_n = mm(dqp, pw["w_q_b"], "nt", "q_b_dx")

    def f(qa, dn, g):
        xhat, r = _rms(qa)
        return _rms_bwd(xhat, r, dn * g).astype(BF16), dn * xhat
    dqa, d_q_a_g = rowwise(f, [qa, dqa_n], [q_a_norm_g], [(qa.shape[1], BF16)], [qa.shape[1]], "q_a_norm_bwd")
    d_w_q_a = mm(hn, dqa, "tn", "q_a_dw")
    dhn = mm(dqa, pw["w_q_a"], "nt", "q_a_dx")
    dh_in, d_sh, d_sc, d_g = norm_mod_bwd(h, dhn, dh_out, g, sc, "norm_mod_bwd")
    grads = dict(w_q_a=d_w_q_a, q_a_norm_g=d_q_a_g, w_q_b=d_w_q_b, w_o=d_w_o)
    return dh_in, (d_sh, d_sc, d_gate, d_g), grads, dk, dv


def loss_head(h, target, g):
    D = h.shape[1]

    def f(h, t, g):
        xhat, r = _rms(h)
        err = xhat * g - t
        dy = err * (1.0 / D)
        dh = _rms_bwd(xhat, r, dy * g)
        return dh, (0.5 / D) * err * err, dy * xhat
    return rowwise(f, [h, target], [g], [(D, F32)], [D, D], "loss_head")


def _place():
    x, y, c = lax.axis_index("x"), lax.axis_index("y"), lax.axis_index("c")
    chips = [(1 - x, y), (x, 1 - y), (1 - x, 1 - y)]
    return x, y, c, chips


HBM_SPEC = pl.BlockSpec(memory_space=pltpu.HBM)


def all_gather8(v):
    m, n = v.shape

    def body(x_ref, out_ref, send_sems, recv_sems, local_sem):
        x, y, c, chips = _place()
        me, sibling = (x, y, c), (x, y, 1 - c)

        def rows(px, py, pc):
            return out_ref.at[4 * px + 2 * py + pc]

        def copy(k, block, to, src=None):
            return pltpu.make_async_remote_copy(
                src_ref=rows(*block) if src is None else src, dst_ref=rows(*block),
                send_sem=send_sems.at[k], recv_sem=recv_sems.at[k], device_id=to, device_id_type=MESH)

        mine = pltpu.make_async_copy(x_ref, rows(*me), local_sem)
        mine.start()
        first = [copy(0, me, sibling, src=x_ref)]
        first += [copy(1 + j, me, (*chip, c), src=x_ref) for j, chip in enumerate(chips)]
        for cp in first:
            cp.start()
        passed = [copy(4 + j, (*chip, c), sibling) for j, chip in enumerate(chips)]
        for j, chip in enumerate(chips):
            copy(1 + j, (*chip, c), me).wait_recv()
            passed[j].start()
        copy(0, sibling, me).wait_recv()
        for j, chip in enumerate(chips):
            copy(4 + j, (*chip, 1 - c), me).wait_recv()
        for cp in first + passed:
            cp.wait_send()
        mine.wait()

    return pl.pallas_call(
        body, name="all_gather8",
        out_shape=jax.ShapeDtypeStruct((8, m, n), v.dtype),
        in_specs=[pl.BlockSpec(memory_space=pltpu.VMEM)],
        out_specs=pl.BlockSpec(memory_space=pltpu.VMEM),
        scratch_shapes=[pltpu.SemaphoreType.DMA((7,)), pltpu.SemaphoreType.DMA((7,)), pltpu.SemaphoreType.DMA],
        compiler_params=pltpu.CompilerParams(vmem_limit_bytes=VMEM_LIMIT_BYTES),
    )(v)


def gather_weights(bufs):
    n = len(bufs)

    def body(*refs):
        ins, outs = refs[:n], refs[n:2 * n]
        send_sems, recv_sems = refs[2 * n:]
        x, y, c, chips = _place()
        across_x, across_y, across_both = chips
        sibling = (x, y, 1 - c)
        me = 2 * x + y
        via_in = (x + (1 - c) * (1 - 2 * x), y + c * (1 - 2 * y))
        via_out = (x + c * (1 - 2 * x), y + (1 - c) * (1 - 2 * y))

        def idx(chip):
            return 2 * chip[0] + chip[1]

        def copy(w, k, src, dst, to):
            return pltpu.make_async_remote_copy(src_ref=src, dst_ref=dst, send_sem=send_sems.at[6 * w + k],
                                                recv_sem=recv_sems.at[6 * w + k], device_id=to, device_id_type=MESH)

        def landed(w, k, chip):
            blk = outs[w].at[idx(chip), c]
            copy(w, k, blk, blk, (*chip, c)).wait_recv()
            return blk

        sends = [copy(w, j, ins[w].at[me, c], outs[w].at[me, c], (*chip, c))
                 for w in range(n) for j, chip in enumerate((across_x, across_y))]
        for cp in sends:
            cp.start()
        for w in range(n):
            blk = landed(w, c, via_in)
            sends += [copy(w, 2, blk, blk, (*via_out, c)), copy(w, 3 + c, blk, blk, sibling)]
            sends[-2].start()
            sends[-1].start()
        for w in range(n):
            blk = landed(w, 1 - c, via_out)
            sends.append(copy(w, 4 - c, blk, blk, sibling))
            sends[-1].start()
        for w in range(n):
            blk = landed(w, 2, across_both)
            sends.append(copy(w, 5, blk, blk, sibling))
            sends[-1].start()
        for w in range(n):
            for j, chip in enumerate(chips):
                other = outs[w].at[idx(chip), 1 - c]
                copy(w, 3 + j, other, other, sibling).wait_recv()
        for cp in sends:
            cp.wait_send()

    return pl.pallas_call(
        body, name="gather_weights",
        out_shape=[jax.ShapeDtypeStruct(b.shape, b.dtype) for b in bufs],
        in_specs=[HBM_SPEC] * n, out_specs=[HBM_SPEC] * n,
        input_output_aliases={w: w for w in range(n)},
        scratch_shapes=[pltpu.SemaphoreType.DMA((6 * n,)), pltpu.SemaphoreType.DMA((6 * n,))],
    )(*bufs)


SEM_SPEC = pl.BlockSpec(memory_space=pltpu.SEMAPHORE)
SPLIT_COPY = pltpu.CompilerParams(has_side_effects=pltpu.SideEffectType.DATAFLOW_SIDE_EFFECTING)
PEERS_PER_BLOCK = 6


def gather_start(groups, carried):
    flat = [a for grp in groups for a in grp]
    group_of = [g for g, grp in enumerate(groups) for _ in grp]
    n, n_g, n_all = len(flat), len(groups), len(flat) + len(carried)

    def body(*refs):
        ins, sems = refs[:n], refs[n_all:n_all + 2 * n_g]
        x, y, c, chips = _place()
        me = 2 * x + y
        for w in range(n):
            mine = ins[w].at[me, c]
            for chip in chips:
                for core in range(2):
                    pltpu.make_async_remote_copy(src_ref=mine, dst_ref=mine, send_sem=sems[2 * group_of[w]],
                                                 recv_sem=sems[2 * group_of[w] + 1], device_id=(*chip, core),
                                                 device_id_type=MESH).start()

    operands = flat + list(carried)
    res = pl.pallas_call(
        body, name="gather_start",
        out_shape=[pltpu.SemaphoreType.DMA(())] * (2 * n_g) + [pltpu.HBM(a.shape, a.dtype) for a in operands],
        in_specs=[HBM_SPEC] * n_all,
        out_specs=[SEM_SPEC] * (2 * n_g) + [HBM_SPEC] * n_all,
        input_output_aliases={w: 2 * n_g + w for w in range(n_all)},
        compiler_params=SPLIT_COPY,
    )(*[pltpu.with_memory_space_constraint(a, pltpu.HBM) for a in operands])
    sems = [(res[2 * g], res[2 * g + 1]) for g in range(n_g)]
    arrays, k = [], 2 * n_g
    for grp in groups:
        arrays.append(list(res[k:k + len(grp)]))
        k += len(grp)
    return sems, arrays, list(res[k:])


def gather_wait(arrays, sems, after, name):
    n = len(arrays)

    def body(*refs):
        ins, send_sem, recv_sem = refs[:n], refs[n], refs[n + 1]
        x, y, c, _ = _place()
        for w in range(n):
            half = ins[w].at[0, 0]
            cp = pltpu.make_async_remote_copy(src_ref=half, dst_ref=half, send_sem=send_sem, recv_sem=recv_sem,
                                              device_id=(x, y, c), device_id_type=MESH)
            for _ in range(PEERS_PER_BLOCK):
                cp.wait_send()
            for _ in range(PEERS_PER_BLOCK):
                cp.wait_recv()

    return pl.pallas_call(
        body, name=name,
        out_shape=[pltpu.HBM(a.shape, a.dtype) for a in arrays],
        in_specs=[HBM_SPEC] * n + [SEM_SPEC, SEM_SPEC, pl.BlockSpec(memory_space=pl.ANY)],
        out_specs=[HBM_SPEC] * n,
        input_output_aliases={w: w for w in range(n)},
        compiler_params=SPLIT_COPY,
    )(*arrays, *sems, after)


def exchange_halves(gs):
    n = len(gs)

    def body(*refs):
        ins, theirs = refs[:n], refs[n:2 * n]
        send_sems, recv_sems = refs[2 * n:]
        x, y, c, _ = _place()
        sends = [pltpu.make_async_remote_copy(src_ref=ins[w].at[:, 1 - c], dst_ref=theirs[w],
                                              send_sem=send_sems.at[w], recv_sem=recv_sems.at[w],
                                              device_id=(x, y, 1 - c), device_id_type=MESH) for w in range(n)]
        for cp in sends:
            cp.start()
        for cp in sends:
            cp.wait()

    return pl.pallas_call(
        body, name="exchange_halves",
        out_shape=[jax.ShapeDtypeStruct((4,) + g.shape[2:], g.dtype) for g in gs],
        in_specs=[HBM_SPEC] * n, out_specs=[HBM_SPEC] * n,
        scratch_shapes=[pltpu.SemaphoreType.DMA((n,)), pltpu.SemaphoreType.DMA((n,))],
    )(*gs)


def join_halves(qs):
    n = len(qs)

    def body(*refs):
        ins, outs = refs[:n], refs[n:2 * n]
        send_sems, recv_sems = refs[2 * n:]
        x, y, c, _ = _place()
        sends = [pltpu.make_async_remote_copy(src_ref=ins[w].at[c], dst_ref=outs[w].at[c], send_sem=send_sems.at[w],
                                              recv_sem=recv_sems.at[w], device_id=(x, y, 1 - c), device_id_type=MESH)
                 for w in range(n)]
        for cp in sends:
            cp.start()
        for w in range(n):
            other = outs[w].at[1 - c]
            pltpu.make_async_remote_copy(src_ref=other, dst_ref=other, send_sem=send_sems.at[w],
                                         recv_sem=recv_sems.at[w], device_id=(x, y, 1 - c),
                                         device_id_type=MESH).wait_recv()
        for cp in sends:
            cp.wait_send()

    return pl.pallas_call(
        body, name="join_halves",
        out_shape=[jax.ShapeDtypeStruct(q.shape, q.dtype) for q in qs],
        in_specs=[HBM_SPEC] * n, out_specs=[HBM_SPEC] * n,
        input_output_aliases={w: w for w in range(n)},
        scratch_shapes=[pltpu.SemaphoreType.DMA((n,)), pltpu.SemaphoreType.DMA((n,))],
    )(*qs)


def _row_tile(R, row_bytes):
    tm = R
    for t in (512, 256, 128, 64, 32, 16, 8):
        if R % t == 0:
            tm = t
            if t * row_bytes <= ROW_TILE_BUDGET:
                break
    return tm


def sum_siblings(g, theirs, place):
    _, _, R, C = g.shape
    tm = _row_tile(R, 3 * C * 4)

    def body(place_ref, a_ref, b_ref, o_ref):
        o_ref[...] = (a_ref[...] + b_ref[...]).astype(BF16)

    return pl.pallas_call(
        body, name="sum_siblings",
        grid_spec=pltpu.PrefetchScalarGridSpec(
            num_scalar_prefetch=1, grid=(4, R // tm),
            in_specs=[pl.BlockSpec((None, None, tm, C), lambda j, i, s: (j, s[1], i, 0)),
                      pl.BlockSpec((None, tm, C), lambda j, i, s: (j, i, 0))],
            out_specs=pl.BlockSpec((None, tm, C), lambda j, i, s: (j, i, 0))),
        out_shape=jax.ShapeDtypeStruct((4, R, C), BF16),
        compiler_params=_params(("parallel", "parallel")),
    )(place, g, theirs)


def sum_chips(p, landed, place):
    _, R, C = p.shape
    tm = _row_tile(R, 5 * C * 4)

    def body(place_ref, p_ref, l0_ref, l1_ref, l2_ref, o_ref):
        o_ref[...] = ((p_ref[...].astype(F32) + l0_ref[...].astype(F32)) + l1_ref[...].astype(F32)
                      ) + l2_ref[...].astype(F32)

    return pl.pallas_call(
        body, name="sum_chips",
        grid_spec=pltpu.PrefetchScalarGridSpec(
            num_scalar_prefetch=1, grid=(R // tm,),
            in_specs=[pl.BlockSpec((None, tm, C), lambda i, s: (s[0], i, 0))]
            + [pl.BlockSpec((None, tm, C), lambda i, s, j=j: (j, i, 0)) for j in range(3)],
            out_specs=pl.BlockSpec((None, tm, C), lambda i, s: (s[1], i, 0))),
        out_shape=jax.ShapeDtypeStruct((2, R, C), F32),
        compiler_params=_params(("parallel",)),
    )(place, p, landed, landed, landed)


def sum_blocks(items, name):
    R, C = items[0][0].shape[1:]
    tm = _row_tile(R, C * 4 * (len(items) + 1))
    n = len(items)

    def body(*refs):
        acc = refs[0][...].astype(F32)
        for r in refs[1:n]:
            acc = acc + r[...].astype(F32)
        refs[n][...] = acc

    return pl.pallas_call(
        body, name=name,
        grid=(R // tm,),
        in_specs=[pl.BlockSpec((None, tm, C), lambda i, j=j: (j, i, 0)) for _, j in items],
        out_specs=pl.BlockSpec((tm, C), lambda i: (i, 0)),
        out_shape=jax.ShapeDtypeStruct((R, C), F32),
        compiler_params=_params(("parallel",)),
    )(*[a for a, _ in items])


def scatter_start(ps, carried, name):
    n = len(ps)

    def body(*refs):
        ins, lands, sems = refs[:n], refs[n:2 * n], refs[2 * n + 1:2 * n + 3]
        x, y, c, chips = _place()
        for w in range(n):
            for j, chip in enumerate(chips):
                pltpu.make_async_remote_copy(src_ref=ins[w].at[2 * chip[0] + chip[1]], dst_ref=lands[w].at[j],
                                             send_sem=sems[0], recv_sem=sems[1], device_id=(*chip, c),
                                             device_id_type=MESH).start()

    operands = list(ps) + [lax.empty((3,) + p.shape[1:], p.dtype) for p in ps] + [carried]
    res = pl.pallas_call(
        body, name=name,
        out_shape=[pltpu.SemaphoreType.DMA(())] * 2 + [pltpu.HBM(a.shape, a.dtype) for a in operands],
        in_specs=[HBM_SPEC] * (2 * n + 1),
        out_specs=[SEM_SPEC] * 2 + [HBM_SPEC] * (2 * n + 1),
        input_output_aliases={w: 2 + w for w in range(2 * n + 1)},
        compiler_params=SPLIT_COPY,
    )(*[pltpu.with_memory_space_constraint(a, pltpu.HBM) for a in operands])
    return (res[0], res[1]), list(res[2:2 + n]), list(res[2 + n:2 + 2 * n]), res[-1]


def scatter_wait(ps, lands, sems, after, name):
    n = len(ps)

    def body(*refs):
        lands_in, send_sem, recv_sem = refs[n:2 * n], refs[2 * n], refs[2 * n + 1]
        x, y, c, _ = _place()
        for w in range(n):
            blk = lands_in[w].at[0]
            cp = pltpu.make_async_remote_copy(src_ref=blk, dst_ref=blk, send_sem=send_sem, recv_sem=recv_sem,
                                              device_id=(x, y, c), device_id_type=MESH)
            for _ in range(3):
                cp.wait_send()
            for _ in range(3):
                cp.wait_recv()

    operands = list(ps) + list(lands)
    res = pl.pallas_call(
        body, name=name,
        out_shape=[pltpu.HBM(a.shape, a.dtype) for a in operands],
        in_specs=[HBM_SPEC] * (2 * n) + [SEM_SPEC, SEM_SPEC, pl.BlockSpec(memory_space=pl.ANY)],
        out_specs=[HBM_SPEC] * (2 * n),
        input_output_aliases={w: w for w in range(2 * n)},
        compiler_params=SPLIT_COPY,
    )(*operands, *sems, after)
    return list(res[:n]), list(res[n:])


def reduce_start(gs, place, stream, name):
    theirs = exchange_halves(gs)
    sems, ps, lands, stream = scatter_start([sum_siblings(g, t, place) for g, t in zip(gs, theirs)], stream,
                                            "scatter_start_" + name)
    return (sems, ps, lands), stream


def reduce_finish(started, place, after, name):
    sems, ps, lands = started
    ps, lands = scatter_wait(ps, lands, sems, after, "scatter_wait_" + name)
    return [sum_chips(p, l, place) for p, l in zip(ps, lands)]


def adamw(w, g, m, v):
    shape = w.shape
    C = shape[-1]
    R = w.size // C

    def f(w, g, m, v):
        m = ADAM_B1 * m + (1.0 - ADAM_B1) * g
        v = ADAM_B2 * v + (1.0 - ADAM_B2) * (g * g)
        m_hat = m / (1.0 - ADAM_B1 ** ADAM_STEP)
        v_hat = v / (1.0 - ADAM_B2 ** ADAM_STEP)
        delta = -ADAM_LR * (m_hat / (jnp.sqrt(v_hat) + ADAM_EPS) + ADAM_WD * w)
        return delta, m, v

    d, nm, nv = rowwise(f, [a.reshape(R, C) for a in (w, g, m, v)], [], [(C, F32)] * 3, [], "adamw")
    return d.reshape(shape), nm.reshape(shape), nv.reshape(shape)


def _cast_into_slot(w, place):
    C = w.shape[-1]
    w2 = w.reshape(-1, C)
    R = w2.shape[0]
    tm = _row_tile(R, 6 * C)

    def body(place_ref, w_ref, o_ref):
        o_ref[...] = w_ref[...].astype(BF16)

    out = pl.pallas_call(
        body, name="cast_bf16",
        grid_spec=pltpu.PrefetchScalarGridSpec(
            num_scalar_prefetch=1, grid=(R // tm,),
            in_specs=[pl.BlockSpec((tm, C), lambda i, s: (i, 0))],
            out_specs=pl.BlockSpec((None, tm, C), lambda i, s: (s[0], i, 0))),
        out_shape=jax.ShapeDtypeStruct((4, R, C), BF16),
        compiler_params=_params(("parallel",)),
    )(place, w2)
    return out.reshape(4, 2, R // 2, C)


def _pack(vs):
    flat = jnp.concatenate([v.reshape(-1) for v in vs])
    n = flat.shape[0]
    total = -(-n // F32_TILE) * F32_TILE
    return jnp.pad(flat, (0, total - n)).reshape(total // LANES, LANES)


def _unpack(flat, like):
    out, o = [], 0
    for shp in like:
        sz = 1
        for d in shp:
            sz *= d
        out.append(flat[o:o + sz].reshape(shp))
        o += sz
    return out


def _cols_to_blocks(g, n_chips=4):
    R, N = g.shape
    C = N // n_chips
    return g.reshape(R, n_chips, C).transpose(1, 0, 2).reshape(n_chips, 2, R // 2, C)


def _rows_to_blocks(g, n_chips=4):
    R, C = g.shape
    return g.reshape(n_chips, 2, R // n_chips // 2, C)


def kernel(x, c, positions, ada_w, ada_b, norm_g, ffn_w13, ffn_w2, conv_w_pw1, conv_b_pw1, conv_w_dw, conv_b_dw, conv_ln_g, conv_ln_b, conv_w_pw2, conv_b_pw2, kv_ada_w, kv_ada_b, kv_norm_g, w_kv_a, kv_a_norm_g, w_kv_b, w_q_a, q_a_norm_g, w_q_b, w_o, final_norm_g, loss_target, m_ada_w, m_ada_b, m_norm_g, m_ffn_w13, m_ffn_w2, m_conv_w_pw1, m_conv_b_pw1, m_conv_w_dw, m_conv_b_dw, m_conv_ln_g, m_conv_ln_b, m_conv_w_pw2, m_conv_b_pw2, m_kv_ada_w, m_kv_ada_b, m_kv_norm_g, m_w_kv_a, m_kv_a_norm_g, m_w_kv_b, m_w_q_a, m_q_a_norm_g, m_w_q_b, m_w_o, m_final_norm_g, v_ada_w, v_ada_b, v_norm_g, v_ffn_w13, v_ffn_w2, v_conv_w_pw1, v_conv_b_pw1, v_conv_w_dw, v_conv_b_dw, v_conv_ln_g, v_conv_ln_b, v_conv_w_pw2, v_conv_b_pw2, v_kv_ada_w, v_kv_ada_b, v_kv_norm_g, v_w_kv_a, v_kv_a_norm_g, v_w_kv_b, v_w_q_a, v_q_a_norm_g, v_w_q_b, v_w_o, v_final_norm_g):
    S, D = x.shape[1], x.shape[2]
    H = N_HEADS
    F = ffn_w2.shape[2] * 4
    xi, yi, ci = lax.axis_index("x"), lax.axis_index("y"), lax.axis_index("c")
    chip = 2 * xi + yi
    dev = 2 * chip + ci
    place = jnp.stack([chip, ci]).astype(jnp.int32)
    h0 = x[0]
    target = loss_target[0]

    silu_c = rowwise(lambda a: a * _sigmoid(a), [c], [], [(D, F32)], [], "silu_c")[0]
    silu_all = all_gather8(silu_c.reshape(8, D // 8)).reshape(8, D)
    n_ada = ada_w.shape[2]
    n_kv = kv_ada_w.shape[1]
    ada_b_mine = lax.dynamic_slice_in_dim(ada_b, chip * n_ada, n_ada, axis=1)
    kv_b_mine = lax.dynamic_slice_in_dim(kv_ada_b, chip * n_kv, n_kv, axis=0)[None, :]
    mods = [mm(silu_all, ada_w[l], "nn", "ada_rows", bias=ada_b_mine[l:l + 1]) for l in range(2)]
    mods.append(mm(silu_all, kv_ada_w, "nn", "kv_ada_rows", bias=kv_b_mine))
    n_mod_cols = 2 * n_ada + n_kv
    mod_pack = jnp.concatenate(mods, axis=1).reshape(-1, LANES)
    mod_all = all_gather8(mod_pack).reshape(8, 8, n_mod_cols)[0::2]
    mod_mine = lax.dynamic_index_in_dim(mod_all, dev, axis=1, keepdims=False)
    mod = [mod_mine[:, l * n_ada:(l + 1) * n_ada].reshape(N_MOD, D) for l in range(2)]
    kv_mod = mod_mine[:, 2 * n_ada:].reshape(2, D)
    kv_shift, kv_scale = kv_mod[0:1], kv_mod[1:2]

    def mrow(l, k):
        return mod[l][k:k + 1]

    def slot(w):
        return _cast_into_slot(w, place)
    first = gather_weights([slot(ffn_w13[0, 0]), slot(ffn_w2[0, 0])])
    groups = [[slot(conv_w_pw1), slot(conv_w_pw2)],
              [slot(ffn_w13[0, 1]), slot(ffn_w2[0, 1])],
              [slot(w_kv_a), slot(w_kv_b), slot(ffn_w13[1, 0]), slot(ffn_w2[1, 0]), slot(w_q_a), slot(w_q_b), slot(w_o),
               slot(ffn_w13[1, 1]), slot(ffn_w2[1, 1])]]
    sems, started, first = gather_start(groups, first)

    def ffn_weights(w13_blocks, w2_blocks):
        return w13_blocks.reshape(4, 1, 1, D, F // 2), w2_blocks.reshape(F, D)
    small_like = [norm_g.shape, conv_b_pw1.shape, conv_w_dw.shape, conv_b_dw.shape, conv_ln_g.shape,
                  conv_ln_b.shape, conv_b_pw2.shape]
    small_pack = _pack([norm_g, conv_b_pw1, conv_w_dw, conv_b_dw, conv_ln_g, conv_ln_b, conv_b_pw2])
    small_all = all_gather8(small_pack)[0::2].reshape(4, -1)
    per_chip = [_unpack(small_all[j], small_like) for j in range(4)]
    smalls = [jnp.concatenate([per_chip[j][k] for j in range(4)], axis=-1) for k in range(len(small_like))]
    norm_g_f, b_pw1_f, w_dw_f, b_dw_f, ln_g_f, ln_b_f, b_pw2_f = smalls

    tabs = rope_tables(positions[0])

    def ng(l, k):
        return norm_g_f[l, k][None, :]

    h = h0
    ffn00 = ffn_weights(*first)
    h, s_f1_0 = ffn_fwd(h, ng(0, 0), mrow(0, 0), mrow(0, 1), mrow(0, 2), ffn00[0], 0, 0, ffn00[1])
    g_pw1, g_pw2 = gather_wait(started[0], sems[0], h, "gather_wait_conv")
    conv_p = dict(
        w_pw1=g_pw1.reshape(4, D, 2 * D // 4).transpose(1, 0, 2).reshape(D, 2 * D),
        b_pw1=b_pw1_f, w_dw=w_dw_f[0], b_dw=b_dw_f, ln_g=ln_g_f, ln_b=ln_b_f,
        w_pw2=g_pw2.reshape(D, D), b_pw2=b_pw2_f)
    h, s_conv = conv_module_fwd(h, ng(0, 1), mrow(0, 3), mrow(0, 4), mrow(0, 5), conv_p)
    ffn01 = ffn_weights(*gather_wait(started[1], sems[1], h, "gather_wait_ffn"))
    h, s_f2_0 = ffn_fwd(h, ng(0, 2), mrow(0, 6), mrow(0, 7), mrow(0, 8), ffn01[0], 0, 0, ffn01[1])
    (g_kv_a, g_kv_b, g_w13_10, g_w2_10, g_q_a, g_q_b, g_w_o, g_w13_11, g_w2_11) = gather_wait(
        started[2], sems[2], h, "gather_wait_layer1")
    ffn10, ffn11 = ffn_weights(g_w13_10, g_w2_10), ffn_weights(g_w13_11, g_w2_11)
    q_lora = w_q_a.shape[2]
    pw = mla_pad_weights(dict(
        w_kv_a=g_kv_a.reshape(D, KV_LORA + QK_ROPE),
        w_kv_b=g_kv_b.reshape(4, KV_LORA, -1).transpose(1, 0, 2).reshape(KV_LORA, -1),
        w_q_a=g_q_a.reshape(D, q_lora),
        w_q_b=g_q_b.reshape(4, q_lora, -1).transpose(1, 0, 2).reshape(q_lora, -1),
        w_o=g_w_o.reshape(H * V_HEAD, D)))
    kv_norm = kv_norm_g[None, :]
    kv_a_g = kv_a_norm_g[None, :]
    kv, kpe_r, s_kv = mla_kv_fwd(h, kv_norm, kv_shift, kv_scale, kv_a_g, pw, tabs)
    h, s_f1_1 = ffn_fwd(h, ng(1, 0), mrow(1, 0), mrow(1, 1), mrow(1, 2), ffn10[0], 0, 0, ffn10[1])
    h, s_mla = mla_fwd(h, ng(1, 1), mrow(1, 3), mrow(1, 4), mrow(1, 5), q_a_norm_g, pw, kv, kpe_r, tabs)
    h, s_f2_1 = ffn_fwd(h, ng(1, 2), mrow(1, 6), mrow(1, 7), mrow(1, 8), ffn11[0], 0, 0, ffn11[1])
    dh, loss_cols, d_final_g = loss_head(h, target, final_norm_g[None, :])

    def w13_blocks(dw):
        return dw.reshape(4, 2, D // 2, F // 2)

    dh, v_f2_1, dw13_11, dw2_11 = ffn_bwd(dh, s_f2_1, ng(1, 2), mrow(1, 7), mrow(1, 8), ffn11[0], 0, 0, ffn11[1])
    red_a, dh = reduce_start([w13_blocks(dw13_11), _rows_to_blocks(dw2_11)], place, dh, "a")
    dh, v_mla, g_mla, dk, dv = mla_bwd(dh, s_mla, ng(1, 1), mrow(1, 4), mrow(1, 5), q_a_norm_g, pw, kv, kpe_r, tabs)
    dh, v_f1_1, dw13_10, dw2_10 = ffn_bwd(dh, s_f1_1, ng(1, 0), mrow(1, 1), mrow(1, 2), ffn10[0], 0, 0, ffn10[1])
    dh, v_kv, d_kv_a_g, d_w_kv_a, d_w_kv_b = mla_kv_bwd(dh, dk, dv, s_kv, kv_norm, kv_scale, kv_a_g, pw, tabs)
    d_w_kv_a_u = jnp.concatenate([d_w_kv_a[:, :KV_LORA], d_w_kv_a[:, KV_LORA + QK_NOPE:KV_LORA + QK_NOPE + QK_ROPE]],
                                 axis=1)
    hk = H * HEAD_PAD
    dkb = jnp.concatenate([d_w_kv_b[:, :hk].reshape(KV_LORA, H, HEAD_PAD)[:, :, :QK_NOPE],
                           d_w_kv_b[:, hk:].reshape(KV_LORA, H, HEAD_PAD)[:, :, :V_HEAD]], axis=2).reshape(KV_LORA, -1)
    d_w_q_b_u = _unpad_heads(g_mla["w_q_b"], QK_NOPE + QK_ROPE)
    d_w_o_u = g_mla["w_o"].reshape(H, HEAD_PAD, D)[:, :V_HEAD].reshape(H * V_HEAD, D)
    q_w13_11, q_w2_11 = reduce_finish(red_a, place, dh, "a")
    red_b, dh = reduce_start([w13_blocks(dw13_10), _rows_to_blocks(dw2_10), _rows_to_blocks(d_w_kv_a_u),
                              _cols_to_blocks(dkb), _rows_to_blocks(g_mla["w_q_a"]), _cols_to_blocks(d_w_q_b_u),
                              _rows_to_blocks(d_w_o_u)], place, dh, "b")
    dh, v_f2_0, dw13_01, dw2_01 = ffn_bwd(dh, s_f2_0, ng(0, 2), mrow(0, 7), mrow(0, 8), ffn01[0], 0, 0, ffn01[1])
    dh, v_conv, g_conv = conv_module_bwd(dh, s_conv, ng(0, 1), mrow(0, 4), mrow(0, 5), conv_p)
    q_w13_10, q_w2_10, q_kv_a, q_kv_b, q_q_a, q_q_b, q_w_o = reduce_finish(red_b, place, dh, "b")
    red_c, dh = reduce_start([w13_blocks(dw13_01), _rows_to_blocks(dw2_01), _cols_to_blocks(g_conv["w_pw1"]),
                              _rows_to_blocks(g_conv["w_pw2"])], place, dh, "c")
    dh, v_f1_0, dw13_00, dw2_00 = ffn_bwd(dh, s_f1_0, ng(0, 0), mrow(0, 1), mrow(0, 2), ffn00[0], 0, 0, ffn00[1])
    grad_x = dh[None]
    q_w13_01, q_w2_01, q_pw1, q_pw2 = reduce_finish(red_c, place, dh, "c")
    red_d, loss_cols = reduce_start([w13_blocks(dw13_00), _rows_to_blocks(dw2_00)], place, loss_cols, "d")
    q_w13_00, q_w2_00 = reduce_finish(red_d, place, dh, "d")
    red = [j.reshape(2 * j.shape[1], j.shape[2]) for j in join_halves(
        [q_w13_00, q_w13_01, q_w13_10, q_w13_11, q_w2_00, q_w2_01, q_w2_10, q_w2_11, q_pw1, q_pw2, q_kv_a, q_kv_b,
         q_q_a, q_q_b, q_w_o])]
    g_ffn_w13 = jnp.stack(red[0:4]).reshape(ffn_w13.shape)
    g_ffn_w2 = jnp.stack(red[4:8]).reshape(ffn_w2.shape)
    g_conv_w_pw1 = red[8].reshape(conv_w_pw1.shape)
    g_conv_w_pw2 = red[9].reshape(conv_w_pw2.shape)
    g_w_kv_a = red[10].reshape(w_kv_a.shape)
    g_w_kv_b = red[11].reshape(w_kv_b.shape)
    g_w_q_a = red[12].reshape(w_q_a.shape)
    g_w_q_b = red[13].reshape(w_q_b.shape)
    g_w_o = red[14].reshape(w_o.shape)

    def dmod(v1, vm, v2):
        return jnp.concatenate([v1[0], v1[1], v1[2], vm[0], vm[1], vm[2], v2[0], v2[1], v2[2]], axis=1)
    d_mod0 = dmod(v_f1_0, v_conv, v_f2_0)
    d_mod1 = dmod(v_f1_1, v_mla, v_f2_1)
    d_kv_mod = jnp.concatenate([v_kv[0], v_kv[1]], axis=1)
    d_norm_g = jnp.concatenate([v_f1_0[3], v_conv[3], v_f2_0[3], v_f1_1[3], v_mla[3], v_f2_1[3]], axis=0)
    vec_list = [d_mod0, d_mod1, d_kv_mod, d_norm_g, g_conv["b_pw1"], g_conv["w_dw"], g_conv["b_dw"], g_conv["ln_g"],
                g_conv["ln_b"], g_conv["b_pw2"], v_kv[2], d_kv_a_g, g_mla["q_a_norm_g"], d_final_g, loss_cols]
    vec_like = [v.shape for v in vec_list]
    vec_pack = _pack(vec_list)
    n_mod_rows = (2 * N_MOD * D + 2 * D) // LANES
    vec_all = all_gather8(vec_pack)
    vec_sum = sum_blocks([(vec_all, d) for d in range(8)], "sum_devices").reshape(-1)
    (_, _, _, s_norm_g, s_b_pw1, s_w_dw, s_b_dw, s_ln_g, s_ln_b, s_b_pw2, s_kv_norm_g, s_kv_a_g, s_q_a_g,
     s_final_g, s_loss) = _unpack(vec_sum, vec_like)
    loss = jnp.sum(s_loss)
    dmod_all = vec_all[:, :n_mod_rows].reshape(8, 2 * N_MOD * D + 2 * D)
    dmod_sum = vec_sum[:2 * N_MOD * D + 2 * D]
    g_ada_b = dmod_sum[:2 * N_MOD * D].reshape(2, N_MOD * D)
    g_kv_ada_b = dmod_sum[2 * N_MOD * D:]
    g_ada_w = []
    for l in range(2):
        cols = lax.dynamic_slice_in_dim(dmod_all[:, l * N_MOD * D:(l + 1) * N_MOD * D], chip * n_ada, n_ada, axis=1)
        g_ada_w.append(mm(silu_all, cols, "tn", "ada_w_grad"))
    g_ada_w = jnp.stack(g_ada_w)
    kv_cols = lax.dynamic_slice_in_dim(dmod_all[:, 2 * N_MOD * D:], chip * n_kv, n_kv, axis=1)
    g_kv_ada_w = mm(silu_all, kv_cols, "tn", "kv_ada_w_grad")

    def shard(v, width):
        return lax.dynamic_slice_in_dim(v, chip * width, width, axis=v.ndim - 1)

    Dq = D // 4
    g_norm_g = shard(s_norm_g.reshape(2, 3, D), Dq)
    g_conv_b_pw1 = shard(s_b_pw1, 2 * D // 4)
    g_conv_w_dw = shard(s_w_dw, Dq)[None]
    g_conv_b_dw = shard(s_b_dw, Dq)
    g_conv_ln_g = shard(s_ln_g, Dq)
    g_conv_ln_b = shard(s_ln_b, Dq)
    g_conv_b_pw2 = shard(s_b_pw2, Dq)

    grads = [g_ada_w, g_ada_b, g_norm_g, g_ffn_w13, g_ffn_w2, g_conv_w_pw1, g_conv_b_pw1, g_conv_w_dw, g_conv_b_dw,
             g_conv_ln_g, g_conv_ln_b, g_conv_w_pw2, g_conv_b_pw2, g_kv_ada_w, g_kv_ada_b, s_kv_norm_g[0], g_w_kv_a,
             s_kv_a_g[0], g_w_kv_b, g_w_q_a, s_q_a_g, g_w_q_b, g_w_o, s_final_g[0]]
    weights = [ada_w, ada_b, norm_g, ffn_w13, ffn_w2, conv_w_pw1, conv_b_pw1, conv_w_dw, conv_b_dw, conv_ln_g,
               conv_ln_b, conv_w_pw2, conv_b_pw2, kv_ada_w, kv_ada_b, kv_norm_g, w_kv_a, kv_a_norm_g, w_kv_b, w_q_a,
               q_a_norm_g, w_q_b, w_o, final_norm_g]
    ms = [m_ada_w, m_ada_b, m_norm_g, m_ffn_w13, m_ffn_w2, m_conv_w_pw1, m_conv_b_pw1, m_conv_w_dw, m_conv_b_dw,
          m_conv_ln_g, m_conv_ln_b, m_conv_w_pw2, m_conv_b_pw2, m_kv_ada_w, m_kv_ada_b, m_kv_norm_g, m_w_kv_a,
          m_kv_a_norm_g, m_w_kv_b, m_w_q_a, m_q_a_norm_g, m_w_q_b, m_w_o, m_final_norm_g]
    vs = [v_ada_w, v_ada_b, v_norm_g, v_ffn_w13, v_ffn_w2, v_conv_w_pw1, v_conv_b_pw1, v_conv_w_dw, v_conv_b_dw,
          v_conv_ln_g, v_conv_ln_b, v_conv_w_pw2, v_conv_b_pw2, v_kv_ada_w, v_kv_ada_b, v_kv_norm_g, v_w_kv_a,
          v_kv_a_norm_g, v_w_kv_b, v_w_q_a, v_q_a_norm_g, v_w_q_b, v_w_o, v_final_norm_g]
    grads = [g.reshape(w.shape) for g, w in zip(grads, weights)]
    deltas, new_m, new_v = [], [], []
    for w, g, m, v in zip(weights, grads, ms, vs):
        d, nm, nv = adamw(w, g, m, v)
        deltas.append(d)
        new_m.append(nm)
        new_v.append(nv)
    return (loss, grad_x, *grads, *deltas, *new_m, *new_v)
```

```python
import jax
import jax.numpy as jnp
from jax import lax
from jax.experimental import pallas as pl
from jax.experimental.pallas import tpu as pltpu

F32 = jnp.float32
BF16 = jnp.bfloat16
MESH = pl.DeviceIdType.MESH

N_HEADS = 16
QK_NOPE = 64
QK_ROPE = 32
V_HEAD = 64
KV_LORA = 256
CONV_WIDTH = 31
ROPE_THETA = 10000.0
EPS = 1e-6
N_MOD = 9
HEAD_PAD = 128
ATTN_TILE = 512
CONV_HALO = 32

ADAM_LR = 0.001
ADAM_B1 = 0.9
ADAM_B2 = 0.999
ADAM_EPS = 1e-08
ADAM_WD = 0.01
ADAM_STEP = 10

VMEM_LIMIT_BYTES = 56 * 2 ** 20
ROW_TILE_BUDGET = 10 * 2 ** 20
MM_VMEM_BUDGET = 40 * 2 ** 20
LANES = 128
F32_TILE = 8 * LANES
NEG = float(jnp.finfo(jnp.float32).min)
LOG2_E = 1.4426950408889634


def _tile(n, prefs):
    for t in prefs:
        if n % t == 0:
            return t
    return n


def _params(sem):
    return pltpu.CompilerParams(dimension_semantics=sem, vmem_limit_bytes=VMEM_LIMIT_BYTES)


def _mm_tiles(M, N, K, mode, a_bytes, b_bytes, o_bytes):
    if mode == "tn":
        tk_opts = [t for t in (2048, 1024, 512, 256, 128) if K % t == 0] or [K]
        tm_opts = ([M] if M <= 2816 else []) + [t for t in (1024, 512, 256, 128) if M % t == 0 and t < M]
    else:
        tk_opts = [K]
        tm_opts = [t for t in (1024, 512, 256, 128) if M % t == 0] or [M]
    tn_opts = [t for t in (1408, 1024, 512, 384, 256, 128) if N % t == 0] or [N]

    def need(tm, tn, tk):
        blocks = 2 * (tm * tk * a_bytes + tk * tn * b_bytes + tm * tn * o_bytes)
        return blocks + (tm * tn * 4 if mode == "tn" else 0)

    tk_floor = next((t for t in tk_opts if t <= 512), tk_opts[-1])
    for tm in tm_opts:
        for tn in tn_opts:
            if need(tm, tn, tk_floor) <= MM_VMEM_BUDGET:
                return tm, tn, next(tk for tk in tk_opts if need(tm, tn, tk) <= MM_VMEM_BUDGET)
    return tm_opts[-1], tn_opts[-1], tk_opts[-1]


def mm(a, b, mode, name, out_dtype=F32, bias=None):
    if mode == "nn":
        (M, K), (K2, N) = a.shape, b.shape
        dims = (((1,), (0,)), ((), ()))
    elif mode == "nt":
        (M, K), (N, K2) = a.shape, b.shape
        dims = (((1,), (1,)), ((), ()))
    else:
        (K, M), (K2, N) = a.shape, b.shape
        dims = (((0,), (0,)), ((), ()))
    assert K == K2, (a.shape, b.shape, mode)
    tm, tn, tk = _mm_tiles(M, N, K, mode, a.dtype.itemsize, b.dtype.itemsize, jnp.dtype(out_dtype).itemsize)
    nk = K // tk
    if mode == "tn":
        a_spec = pl.BlockSpec((tk, tm), lambda i, j, k: (k, i))
        b_spec = pl.BlockSpec((tk, tn), lambda i, j, k: (k, j))
    elif mode == "nn":
        a_spec = pl.BlockSpec((tm, tk), lambda i, j, k: (i, k))
        b_spec = pl.BlockSpec((tk, tn), lambda i, j, k: (k, j))
    else:
        a_spec = pl.BlockSpec((tm, tk), lambda i, j, k: (i, k))
        b_spec = pl.BlockSpec((tn, tk), lambda i, j, k: (j, k))
    in_specs = [a_spec, b_spec]
    operands = [a, b]
    if bias is not None:
        in_specs.append(pl.BlockSpec((1, tn), lambda i, j, k: (0, j)))
        operands.append(bias)
    has_bias = bias is not None

    def body(*refs):
        a_ref, b_ref = refs[0], refs[1]
        bias_ref = refs[2] if has_bias else None
        o_ref = refs[3] if has_bias else refs[2]
        prod = lax.dot_general(a_ref[...].astype(BF16), b_ref[...].astype(BF16), dims,
                               preferred_element_type=F32)
        if nk == 1:
            if has_bias:
                prod = prod + bias_ref[...]
            o_ref[...] = prod.astype(o_ref.dtype)
        else:
            acc_ref = refs[-1]
            k = pl.program_id(2)

            @pl.when(k == 0)
            def _():
                acc_ref[...] = jnp.zeros_like(acc_ref)

            acc_ref[...] += prod

            @pl.when(k == nk - 1)
            def _():
                out = acc_ref[...]
                if has_bias:
                    out = out + bias_ref[...]
                o_ref[...] = out.astype(o_ref.dtype)

    return pl.pallas_call(
        body, name=name,
        grid=(M // tm, N // tn, nk),
        in_specs=in_specs,
        out_specs=pl.BlockSpec((tm, tn), lambda i, j, k: (i, j)),
        out_shape=jax.ShapeDtypeStruct((M, N), out_dtype),
        scratch_shapes=[pltpu.VMEM((tm, tn), F32)] if nk > 1 else [],
        compiler_params=_params(("parallel", "parallel", "arbitrary")),
    )(*operands)


def mm_fused(a, b, mode, name, tn, epi, epi_outs, pro=None, pro_rows=(), pro_vecs=(), pro_out=False, n_pro_sums=0,
             epi_rows=(), epi_vecs=(), b_blocks=None, n_cols=None):
    M, K = a.shape
    if b_blocks is not None:
        n_b, N = len(b_blocks), n_cols
    else:
        n_b = b.shape[0] if b.ndim == 3 else 1
        N = b.shape[-1] if mode == "nn" else b.shape[0]
    dims = (((1,), (0,)), ((), ())) if mode == "nn" else (((1,), (1,)), ((), ()))
    nj = N // tn
    epi_outs = [o if len(o) == 3 else (*o, None) for o in epi_outs]
    row_bytes = 2 * (K * a.dtype.itemsize + sum(K * r.dtype.itemsize for r in pro_rows) + (2 * K if pro_out else 0)
                     + sum(w * r.dtype.itemsize * (r.shape[0] if r.ndim == 3 else 1) for r, w in epi_rows)
                     + sum(w * jnp.dtype(dt).itemsize * (L or 1) for w, dt, L in epi_outs)
                     ) + (2 * K if pro is not None else 0)
    fixed = 2 * n_b * K * tn * b.dtype.itemsize
    tm = next((t for t in (1024, 512, 256, 128) if M % t == 0 and t * row_bytes + fixed <= MM_VMEM_BUDGET), M)
    row = lambda i, j: (i, 0)
    tile = lambda i, j: (i, j)
    stack = lambda i, j: (0, i, j)
    in_specs = [pl.BlockSpec((tm, K), row)] + [pl.BlockSpec((tm, K), row) for _ in pro_rows]
    in_specs += [pl.BlockSpec(v.shape, lambda i, j: (0, 0)) for v in pro_vecs]
    if b_blocks is not None:
        in_specs += [pl.BlockSpec(shape, imap) for shape, imap in b_blocks]
    elif b.ndim == 3:
        in_specs += [pl.BlockSpec((None, K, tn), lambda i, j, h=h: (h, 0, j)) for h in range(n_b)]
    elif mode == "nn":
        in_specs += [pl.BlockSpec((K, tn), lambda i, j: (0, j))]
    else:
        in_specs += [pl.BlockSpec((tn, K), lambda i, j: (j, 0))]
    in_specs += [pl.BlockSpec((r.shape[0], tm, w), stack) if r.ndim == 3 else pl.BlockSpec((tm, w), tile)
                 for r, w in epi_rows]
    in_specs += [pl.BlockSpec((1, tn), lambda i, j: (0, j)) for _ in epi_vecs]
    out_specs, out_shape = [], []
    if pro_out:
        out_specs.append(pl.BlockSpec((tm, K), row))
        out_shape.append(jax.ShapeDtypeStruct((M, K), BF16))
    for _ in range(n_pro_sums):
        out_specs.append(pl.BlockSpec((1, K), lambda i, j: (0, 0)))
        out_shape.append(jax.ShapeDtypeStruct((1, K), F32))
    for w, dt, L in epi_outs:
        out_specs.append(pl.BlockSpec((tm, w), tile) if L is None else pl.BlockSpec((L, tm, w), stack))
        out_shape.append(jax.ShapeDtypeStruct((M, nj * w) if L is None else (L, M, nj * w), dt))
    n_pr, n_pv, n_er, n_ev = len(pro_rows), len(pro_vecs), len(epi_rows), len(epi_vecs)
    n_a = 1 + n_pr + n_pv
    n_in = n_a + n_b + n_er + n_ev
    n_po = 1 if pro_out else 0

    def body(*refs):
        i, j = pl.program_id(0), pl.program_id(1)
        a_ref = refs[0]
        outs = refs[n_in:]
        if pro is not None:
            lhs_ref = refs[-1]

            @pl.when(j == 0)
            def _():
                res = pro(*[r[...] for r in refs[:1 + n_pr + n_pv]])
                if not isinstance(res, (tuple, list)):
                    res = (res,)
                lhs_ref[...] = res[0]
                if pro_out:
                    outs[0][...] = res[0]
                for s_ref, val in zip(outs[n_po:n_po + n_pro_sums], res[1:]):
                    part = jnp.sum(val.astype(F32), axis=0, keepdims=True)

                    @pl.when(i == 0)
                    def _(s_ref=s_ref, part=part):
                        s_ref[...] = part

                    @pl.when(i != 0)
                    def _(s_ref=s_ref, part=part):
                        s_ref[...] += part

            lhs = lhs_ref[...]
        else:
            lhs = a_ref[...].astype(BF16)
        accs = [lax.dot_general(lhs, b_ref[...].astype(BF16), dims, preferred_element_type=F32)
                for b_ref in refs[n_a:n_a + n_b]]
        res = epi(*accs, *[r[...] for r in refs[n_a + n_b:n_in]])
        if not isinstance(res, (tuple, list)):
            res = (res,)
        for o_ref, val in zip(outs[n_po + n_pro_sums:], res):
            if isinstance(val, (tuple, list)):
                for h, part in enumerate(val):
                    o_ref[h] = part.astype(o_ref.dtype)
            else:
                o_ref[...] = val.astype(o_ref.dtype)

    return pl.pallas_call(
        body, name=name,
        grid=(M // tm, nj),
        in_specs=in_specs, out_specs=out_specs, out_shape=out_shape,
        scratch_shapes=[pltpu.VMEM((tm, K), BF16)] if pro is not None else [],
        compiler_params=_params(("arbitrary", "arbitrary")),
    )(a, *pro_rows, *pro_vecs, *([b] * n_b), *[r for r, _ in epi_rows], *epi_vecs)


def rowwise(fn, rows, vecs, outs, sums, name, tm=None):
    norm = [(r, r.shape[1], 0) if not isinstance(r, tuple) else r for r in rows]
    S = norm[0][0].shape[0]
    if tm is None:
        tm = _row_tile(S, sum(w * r.dtype.itemsize for r, w, _ in norm)
                       + sum(n * jnp.dtype(dt).itemsize for n, dt in outs))
    n_rows, n_vecs, n_outs, n_sums = len(norm), len(vecs), len(outs), len(sums)
    in_specs = [pl.BlockSpec((tm, w), lambda i, cb=cb: (i, cb)) for _, w, cb in norm]
    in_specs += [pl.BlockSpec(v.shape, lambda i: (0, 0)) for v in vecs]
    out_specs = [pl.BlockSpec((tm, n), lambda i: (i, 0)) for n, _ in outs]
    out_specs += [pl.BlockSpec((1, n), lambda i: (0, 0)) for n in sums]
    out_shape = [jax.ShapeDtypeStruct((S, n), dt) for n, dt in outs]
    out_shape += [jax.ShapeDtypeStruct((1, n), F32) for n in sums]

    def body(*refs):
        ins = [r[...] for r in refs[:n_rows + n_vecs]]
        res = fn(*ins)
        if not isinstance(res, (tuple, list)):
            res = (res,)
        out_refs = refs[n_rows + n_vecs:]
        for o_ref, val in zip(out_refs[:n_outs], res[:n_outs]):
            o_ref[...] = val.astype(o_ref.dtype)
        if n_sums:
            i = pl.program_id(0)
            for s_ref, val in zip(out_refs[n_outs:], res[n_outs:]):
                part = jnp.sum(val.astype(F32), axis=0, keepdims=True)

                @pl.when(i == 0)
                def _(s_ref=s_ref, part=part):
                    s_ref[...] = part

                @pl.when(i != 0)
                def _(s_ref=s_ref, part=part):
                    s_ref[...] += part

    res = pl.pallas_call(
        body, name=name,
        grid=(S // tm,),
        in_specs=in_specs, out_specs=out_specs, out_shape=out_shape,
        compiler_params=_params(("arbitrary",) if n_sums else ("parallel",)),
    )(*[r for r, _, _ in norm], *vecs)
    return res


def _sigmoid(x):
    return jax.nn.sigmoid(x)


def _rms(x):
    r = lax.rsqrt(jnp.mean(x * x, axis=-1, keepdims=True) + EPS)
    return x * r, r


def _rms_bwd(xhat, r, dxhat):
    return r * (dxhat - xhat * jnp.mean(dxhat * xhat, axis=-1, keepdims=True))


def norm_mod(h, g, sh, sc, name):
    def f(h, g, sh, sc):
        xhat, _ = _rms(h)
        return ((xhat * g) * (1 + sc) + sh).astype(BF16)
    return rowwise(f, [h], [g, sh, sc], [(h.shape[1], BF16)], [], name)[0]


def norm_mod_bwd(h, dhn, dh_out, g, sc, name):
    D = h.shape[1]

    def f(h, dhn, dres, g, sc):
        xhat, r = _rms(h)
        dxn = dhn * (1 + sc)
        return _rms_bwd(xhat, r, dxn * g) + dres, dhn, dhn * (xhat * g), dxn * xhat

    return rowwise(f, [h, dhn, dh_out], [g, sc], [(D, F32)], [D, D, D], name)


def residual(h, y, gate, coef, name, bias=None):
    D = h.shape[1]
    if bias is None:
        def f(h, y, gate):
            return h + (coef * gate) * y
        return rowwise(f, [h, y], [gate], [(D, F32)], [], name)[0], y

    def fb(h, y, gate, bias):
        yb = y + bias
        return h + (coef * gate) * yb, yb
    return rowwise(fb, [h, y], [gate, bias], [(D, F32), (D, F32)], [], name)


def residual_bwd(dh_out, y, gate, coef, name, with_bias_sum=False):
    D = y.shape[1]

    def f(dh, y, gate):
        dy = (coef * gate) * dh
        res = (dy.astype(BF16), coef * dh * y)
        return res + ((dy,) if with_bias_sum else ())
    return rowwise(f, [dh_out, y], [gate], [(D, BF16)], [D, D] if with_bias_sum else [D], name)


def ffn_w13_dx(dab, gw13, l, i):
    _, S, F = dab.shape
    D, C = gw13.shape[3:]
    tm = _tile(S, (1024, 512, 256, 128))
    nt = (((1,), (1,)), ((), ()))

    def body(a_ref, b_ref, o_ref, acc_ref):
        k = pl.program_id(1)
        prod = lax.dot_general(a_ref[...], b_ref[...], nt, preferred_element_type=F32)

        @pl.when(k == 0)
        def _():
            acc_ref[...] = prod

        @pl.when((k > 0) & (k < 3))
        def _():
            acc_ref[...] += prod

        @pl.when(k == 3)
        def _():
            o_ref[...] = acc_ref[...] + prod

    return pl.pallas_call(
        body, name="ffn_w13_dx",
        grid=(S // tm, 4),
        in_specs=[pl.BlockSpec((None, tm, C), lambda r, k: (k // 2, r, k % 2)),
                  pl.BlockSpec((None, None, None, D, C), lambda r, k: (k, l, i, 0, 0))],
        out_specs=pl.BlockSpec((tm, D), lambda r, k: (r, 0)),
        out_shape=jax.ShapeDtypeStruct((S, D), F32),
        scratch_shapes=[pltpu.VMEM((tm, D), F32)],
        compiler_params=_params(("parallel", "arbitrary")),
    )(dab, gw13)


def ffn_w13_grad(hn, dab):
    S, D = hn.shape
    F = dab.shape[2]
    C = F // 2
    tk = next(t for t in (2048, 1024, 512, 256, 128) if S % t == 0)
    tn_dims = (((0,), (0,)), ((), ()))
    nk = S // tk

    def body(a_ref, b_ref, o_ref, acc_ref):
        k = pl.program_id(1)

        @pl.when(k == 0)
        def _():
            acc_ref[...] = jnp.zeros_like(acc_ref)

        acc_ref[...] += lax.dot_general(a_ref[...], b_ref[...], tn_dims, preferred_element_type=F32)

        @pl.when(k == nk - 1)
        def _():
            o_ref[...] = acc_ref[...]

    return pl.pallas_call(
        body, name="ffn_w13_dw",
        grid=(4, nk),
        in_specs=[pl.BlockSpec((tk, D), lambda j, k: (k, 0)),
                  pl.BlockSpec((None, tk, C), lambda j, k: (j // 2, k, j % 2))],
        out_specs=pl.BlockSpec((None, D, C), lambda j, k: (j, 0, 0)),
        out_shape=jax.ShapeDtypeStruct((4, D, C), F32),
        scratch_shapes=[pltpu.VMEM((D, C), F32)],
        compiler_params=_params(("parallel", "arbitrary")),
    )(hn, dab)


def ffn_fwd(h, g, sh, sc, gate, gw13, l, i, w2):
    F, D = w2.shape
    C = F // 2

    def norm(h, g, sh, sc):
        xhat, _ = _rms(h)
        return ((xhat * g) * (1 + sc) + sh).astype(BF16)

    def act(a, b):
        sig = _sigmoid(a)
        sa = a * sig
        return (b * (sig + sa * (1 - sig)), sa), sa * b
    blocks = [((None, None, None, D, C), lambda r, j, half=half: (2 * half + j, l, i, 0, 0)) for half in range(2)]
    hn, dt_dab, t = mm_fused(h, gw13, "nn", "ffn_w13", C, act, [(C, BF16, 2), (C, BF16)],
                             pro=norm, pro_vecs=[g, sh, sc], pro_out=True, b_blocks=blocks, n_cols=F)

    def res(acc, h, gate):
        return h + (0.5 * gate) * acc, acc
    h_out, y = mm_fused(t, w2, "nn", "ffn_w2", D, res, [(D, F32), (D, F32)], epi_rows=[(h, D)], epi_vecs=[gate])
    return h_out, (h, hn, dt_dab, t, y)


def ffn_bwd(dh_out, saved, g, sc, gate, gw13, l, i, w2):
    h, hn, dt_dab, t, y = saved
    F, D = w2.shape
    C = F // 2

    def scale(dh, y, gate):
        return ((0.5 * gate) * dh).astype(BF16), 0.5 * dh * y

    def act_bwd(dt, f):
        return ((dt * f[0].astype(F32), dt * f[1].astype(F32)),)
    dy, d_gate, dab = mm_fused(dh_out, w2, "nt", "ffn_w2_dx", C, act_bwd, [(C, BF16, 2)],
                               pro=scale, pro_rows=[y], pro_vecs=[gate], pro_out=True, n_pro_sums=1,
                               epi_rows=[(dt_dab, C)])
    dw2 = mm(t, dy, "tn", "ffn_w2_dw")
    dw13 = ffn_w13_grad(hn, dab)
    dhn = ffn_w13_dx(dab, gw13, l, i)
    dh_in, d_sh, d_sc, d_g = norm_mod_bwd(h, dhn, dh_out, g, sc, "norm_mod_bwd")
    return dh_in, (d_sh, d_sc, d_gate, d_g), dw13, dw2


def _shifted(xbuf, n):
    return [xbuf] + [pltpu.roll(xbuf, n - b, 0) for b in range(1, 8)]


def conv_fwd(u, w_dw, b_dw, ln_g, ln_b):
    S, D = u.shape
    tm = _tile(S, (256, 128))
    rc = 32
    first_tap = CONV_HALO - (CONV_WIDTH - 1)
    w = jnp.concatenate([w_dw, jnp.zeros((CONV_HALO - CONV_WIDTH, D), F32)], axis=0)

    def body(cur_ref, prev_ref, w_ref, b_ref, g_ref, beta_ref, z_ref, s_ref):
        i = pl.program_id(0)
        prev = jnp.where(i == 0, jnp.zeros((CONV_HALO, D), F32), prev_ref[...])
        xs = _shifted(jnp.concatenate([prev, cur_ref[...]], axis=0), tm + CONV_HALO)
        for c0 in range(0, tm, rc):
            acc = jnp.zeros((rc, D), F32)
            for k in range(CONV_WIDTH):
                off = first_tap + k
                a8, b = off // 8 * 8, off % 8
                acc = acc + w_ref[k:k + 1, :] * xs[b][c0 + a8:c0 + a8 + rc, :]
            z_ref[c0:c0 + rc, :] = acc + b_ref[...]
        z = z_ref[...]
        mu = jnp.mean(z, axis=-1, keepdims=True)
        zc = z - mu
        r = lax.rsqrt(jnp.mean(zc * zc, axis=-1, keepdims=True) + EPS)
        un = zc * r * g_ref[...] + beta_ref[...]
        s_ref[...] = (un * _sigmoid(un)).astype(BF16)

    nb = tm // CONV_HALO
    vec = pl.BlockSpec((1, D), lambda i: (0, 0))
    return pl.pallas_call(
        body, name="conv_fwd",
        grid=(S // tm,),
        in_specs=[pl.BlockSpec((tm, D), lambda i: (i, 0)),
                  pl.BlockSpec((CONV_HALO, D), lambda i: (jnp.maximum(i * nb - 1, 0), 0)),
                  pl.BlockSpec((CONV_HALO, D), lambda i: (0, 0)), vec, vec, vec],
        out_specs=[pl.BlockSpec((tm, D), lambda i: (i, 0)), pl.BlockSpec((tm, D), lambda i: (i, 0))],
        out_shape=[jax.ShapeDtypeStruct((S, D), F32), jax.ShapeDtypeStruct((S, D), BF16)],
        compiler_params=_params(("parallel",)),
    )(u, u, w, b_dw, ln_g, ln_b)


def conv_bwd(dz, u, w_dw):
    S, D = u.shape
    tm = _tile(S, (256, 128))
    rc = 32
    first_tap = CONV_HALO - (CONV_WIDTH - 1)
    w = jnp.concatenate([w_dw, jnp.zeros((CONV_HALO - CONV_WIDTH, D), F32)], axis=0)
    n_tiles = S // tm
    nb = tm // CONV_HALO

    def body(dz_ref, dzn_ref, u_ref, up_ref, w_ref, du_ref, dw_ref):
        i = pl.program_id(0)
        nxt = jnp.where(i == n_tiles - 1, jnp.zeros((CONV_HALO, D), F32), dzn_ref[...])
        dzs = _shifted(jnp.concatenate([dz_ref[...], nxt], axis=0), tm + CONV_HALO)
        for c0 in range(0, tm, rc):
            acc = jnp.zeros((rc, D), F32)
            for m in range(CONV_WIDTH):
                a8, b = m // 8 * 8, m % 8
                acc = acc + w_ref[CONV_WIDTH - 1 - m:CONV_WIDTH - m, :] * dzs[b][c0 + a8:c0 + a8 + rc, :]
            du_ref[c0:c0 + rc, :] = acc
        prev = jnp.where(i == 0, jnp.zeros((CONV_HALO, D), F32), up_ref[...])
        us = _shifted(jnp.concatenate([prev, u_ref[...]], axis=0), tm + CONV_HALO)
        dz = dz_ref[...]

        @pl.when(i == 0)
        def _():
            dw_ref[...] = jnp.zeros_like(dw_ref)

        for k in range(CONV_WIDTH):
            off = first_tap + k
            a8, b = off // 8 * 8, off % 8
            dw_ref[k:k + 1, :] += jnp.sum(dz * us[b][a8:a8 + tm, :], axis=0, keepdims=True)

    last_blk = S // CONV_HALO - 1
    du, dw = pl.pallas_call(
        body, name="conv_bwd",
        grid=(n_tiles,),
        in_specs=[pl.BlockSpec((tm, D), lambda i: (i, 0)),
                  pl.BlockSpec((CONV_HALO, D), lambda i: (jnp.minimum((i + 1) * nb, last_blk), 0)),
                  pl.BlockSpec((tm, D), lambda i: (i, 0)),
                  pl.BlockSpec((CONV_HALO, D), lambda i: (jnp.maximum(i * nb - 1, 0), 0)),
                  pl.BlockSpec((CONV_HALO, D), lambda i: (0, 0))],
        out_specs=[pl.BlockSpec((tm, D), lambda i: (i, 0)), pl.BlockSpec((CONV_HALO, D), lambda i: (0, 0))],
        out_shape=[jax.ShapeDtypeStruct((S, D), F32), jax.ShapeDtypeStruct((CONV_HALO, D), F32)],
        compiler_params=_params(("arbitrary",)),
    )(dz, dz, u, u, w)
    return du, dw[:CONV_WIDTH]


def conv_module_fwd(h, g, sh, sc, gate, p):
    D = h.shape[1]
    hn = norm_mod(h, g, sh, sc, "conv_norm_mod")
    pre = mm(hn, p["w_pw1"], "nn", "conv_pw1")
    ba, bg = p["b_pw1"][:, :D], p["b_pw1"][:, D:]

    def glu(a, gt, ba, bg):
        return (a + ba) * _sigmoid(gt + bg)
    u = rowwise(glu, [(pre, D, 0), (pre, D, 1)], [ba, bg], [(D, F32)], [], "conv_glu")[0]
    z, s = conv_fwd(u, p["w_dw"], p["b_dw"], p["ln_g"], p["ln_b"])
    yraw = mm(s, p["w_pw2"], "nn", "conv_pw2")
    h_out, y = residual(h, yraw, gate, 1.0, "conv_residual", bias=p["b_pw2"])
    return h_out, (h, hn, pre, u, z, s, y)


def conv_module_bwd(dh_out, saved, g, sc, gate, p):
    h, hn, pre, u, z, s, y = saved
    D = h.shape[1]
    dy, d_gate, d_b_pw2 = residual_bwd(dh_out, y, gate, 1.0, "conv_residual_bwd", with_bias_sum=True)
    d_w_pw2 = mm(s, dy, "tn", "conv_pw2_dw")
    ds = mm(dy, p["w_pw2"], "nt", "conv_pw2_dx")

    def ln_bwd(z, ds, g, beta):
        mu = jnp.mean(z, axis=-1, keepdims=True)
        zc = z - mu
        r = lax.rsqrt(jnp.mean(zc * zc, axis=-1, keepdims=True) + EPS)
        xhat = zc * r
        un = xhat * g + beta
        sig = _sigmoid(un)
        d_un = ds * (sig * (1 + un * (1 - sig)))
        dxhat = d_un * g
        dz = r * (dxhat - jnp.mean(dxhat, axis=-1, keepdims=True)
                  - xhat * jnp.mean(dxhat * xhat, axis=-1, keepdims=True))
        return dz, d_un * xhat, d_un, dz
    dz, d_ln_g, d_ln_b, d_b_dw = rowwise(ln_bwd, [z, ds], [p["ln_g"], p["ln_b"]], [(D, F32)], [D, D, D],
                                         "conv_ln_bwd")
    du, d_w_dw = conv_bwd(dz, u, p["w_dw"])
    ba, bg = p["b_pw1"][:, :D], p["b_pw1"][:, D:]

    def glu_bwd(a, gt, du, ba, bg):
        sg = _sigmoid(gt + bg)
        da = du * sg
        dg = du * (a + ba) * (sg * (1 - sg))
        dpre = jnp.concatenate([da, dg], axis=1)
        return dpre.astype(BF16), dpre
    dpre, d_b_pw1 = rowwise(glu_bwd, [(pre, D, 0), (pre, D, 1), du], [ba, bg], [(2 * D, BF16)], [2 * D],
                            "conv_glu_bwd")
    d_w_pw1 = mm(hn, dpre, "tn", "conv_pw1_dw")
    dhn = mm(dpre, p["w_pw1"], "nt", "conv_pw1_dx")
    dh_in, d_sh, d_sc, d_g = norm_mod_bwd(h, dhn, dh_out, g, sc, "norm_mod_bwd")
    grads = dict(w_pw1=d_w_pw1, b_pw1=d_b_pw1, w_dw=d_w_dw, b_dw=d_b_dw, ln_g=d_ln_g, ln_b=d_ln_b,
                 w_pw2=d_w_pw2, b_pw2=d_b_pw2)
    return dh_in, (d_sh, d_sc, d_gate, d_g), grads


def _rope(x, c, s1, s2):
    n = x.shape[1]
    return x * c + pltpu.roll(x, n - QK_ROPE // 2, 1) * s1 + pltpu.roll(x, QK_ROPE // 2, 1) * s2


def _rope_t(dy, c, s1, s2):
    n = dy.shape[1]
    return dy * c + pltpu.roll(dy * s1, QK_ROPE // 2, 1) + pltpu.roll(dy * s2, n - QK_ROPE // 2, 1)


def rope_tables(positions):
    inv_freq = ROPE_THETA ** (-jnp.arange(0, QK_ROPE, 2, dtype=F32) / QK_ROPE)
    ang = positions.astype(F32)[:, None] * inv_freq
    cos, sin = jnp.cos(ang), jnp.sin(ang)
    S = positions.shape[0]
    one = jnp.ones((S, QK_NOPE), F32)
    z16 = jnp.zeros((S, QK_ROPE // 2), F32)
    zn = jnp.zeros((S, QK_NOPE), F32)
    zt = jnp.zeros((S, HEAD_PAD - QK_NOPE - QK_ROPE), F32)
    c = jnp.concatenate([one, cos, cos, zt], axis=1)
    s1 = jnp.concatenate([zn, -sin, z16, zt], axis=1)
    s2 = jnp.concatenate([zn, z16, sin, zt], axis=1)
    return c, s1, s2


def attn_fwd(qr, kv, kpe, n_heads):
    S = qr.shape[0]
    H = n_heads
    tk = _tile(S, (ATTN_TILE,))
    nk = S // tk
    w = 2 if nk % 2 == 0 else 1
    tq = w * tk
    c2 = (QK_NOPE + QK_ROPE) ** -0.5 * LOG2_E
    nt = (((1,), (1,)), ((), ()))

    assert V_HEAD < HEAD_PAD
    ones_row = HEAD_PAD - 1

    def body(q_ref, k_ref, v_ref, kpe_ref, o_ref, lse_ref, kf_ref, vt_ref, m_ref, acc_ref):
        qi = pl.program_id(1)
        feature = lax.broadcasted_iota(jnp.int32, (HEAD_PAD, tk), 0)

        @pl.when(qi == 0)
        def _():
            kf_ref[...] = k_ref[...] + kpe_ref[...]
            for c in range(nk):
                vt = jnp.transpose(v_ref[c * tk:(c + 1) * tk, :].astype(F32))
                vt_ref[c] = jnp.where(feature == ones_row, 1.0, vt).astype(BF16)

        q = q_ref[...]
        m_ref[...] = jnp.full((1, tq), -jnp.inf, F32)
        acc_ref[...] = jnp.zeros((HEAD_PAD, tq), F32)

        def tile(j, first_visible):
            k = kf_ref[pl.ds(pl.multiple_of(j * tk, tk), tk), :]
            t = lax.dot_general(k, q, nt, preferred_element_type=F32) * c2
            if first_visible is not None:
                krow = lax.broadcasted_iota(jnp.int32, (tk, tq), 0)
                qcol = lax.broadcasted_iota(jnp.int32, (tk, tq), 1)
                t = jnp.where(krow + first_visible <= qcol, t, NEG)
            m_old = m_ref[...]
            m_new = jnp.maximum(m_old, jnp.max(t, axis=0, keepdims=True))
            alpha = jnp.exp2(m_old - m_new)
            p = jnp.exp2(t - m_new)
            acc_ref[...] = alpha * acc_ref[...] + jnp.dot(vt_ref[j], p.astype(BF16), preferred_element_type=F32)
            m_ref[...] = m_new

        def unmasked(j, carry):
            tile(j, None)
            return carry

        lax.fori_loop(0, w * qi, unmasked, 0)
        for u in range(w):
            tile(w * qi + u, u * tk)
        acc = acc_ref[...]
        l = acc_ref[ones_row:ones_row + 1, :]
        out_feature = lax.broadcasted_iota(jnp.int32, (HEAD_PAD, tq), 0)
        o_ref[...] = jnp.transpose(jnp.where(out_feature == ones_row, 0.0, acc / l))
        lse = m_ref[...] + jnp.log(l) * LOG2_E
        for u in range(w):
            lse_ref[u] = lse[:, u * tk:(u + 1) * tk]

    return pl.pallas_call(
        body, name="attn_fwd",
        grid=(H, S // tq),
        in_specs=[pl.BlockSpec((tq, HEAD_PAD), lambda h, i: (i, h)),
                  pl.BlockSpec((S, HEAD_PAD), lambda h, i: (0, h)),
                  pl.BlockSpec((S, HEAD_PAD), lambda h, i: (0, H + h)),
                  pl.BlockSpec((S, HEAD_PAD), lambda h, i: (0, 0))],
        out_specs=[pl.BlockSpec((tq, HEAD_PAD), lambda h, i: (i, h)),
                   pl.BlockSpec((None, w, 1, tk), lambda h, i: (h, i, 0, 0))],
        out_shape=[jax.ShapeDtypeStruct((S, H * HEAD_PAD), F32), jax.ShapeDtypeStruct((H, nk, 1, tk), F32)],
        scratch_shapes=[pltpu.VMEM((S, HEAD_PAD), BF16), pltpu.VMEM((nk, HEAD_PAD, tk), BF16),
                        pltpu.VMEM((1, tq), F32), pltpu.VMEM((HEAD_PAD, tq), F32)],
        compiler_params=_params(("parallel", "arbitrary")),
    )(qr, kv, kv, kpe)


def attn_delta(o, do, n_heads):
    S = o.shape[0]
    H = n_heads
    tq = _tile(S, (ATTN_TILE,))
    nq = S // tq

    def body(o_ref, do_ref, d_ref):
        for c in range(nq):
            rows = slice(c * tq, (c + 1) * tq)
            prod = o_ref[rows, :] * do_ref[rows, :].astype(F32)
            d_ref[c] = jnp.sum(jnp.transpose(prod), axis=0, keepdims=True)

    return pl.pallas_call(
        body, name="attn_delta",
        grid=(H,),
        in_specs=[pl.BlockSpec((S, HEAD_PAD), lambda h: (0, h)), pl.BlockSpec((S, HEAD_PAD), lambda h: (0, h))],
        out_specs=pl.BlockSpec((None, nq, 1, tq), lambda h: (h, 0, 0, 0)),
        out_shape=jax.ShapeDtypeStruct((H, nq, 1, tq), F32),
        compiler_params=_params(("parallel",)),
    )(o, do)


def attn_bwd(qr, kv, kpe, do, lse2, delta, n_heads):
    S = qr.shape[0]
    H = n_heads
    tk = _tile(S, (ATTN_TILE,))
    nk = S // tk
    w = 2 if nk % 2 == 0 else 1
    tq = w * tk
    nq = S // tq
    scale = (QK_NOPE + QK_ROPE) ** -0.5
    c2 = scale * LOG2_E
    nt = (((1,), (1,)), ((), ()))
    lse2 = lse2.reshape(H, nq, 1, tq)
    delta4 = delta.reshape(H, nq, 1, tq)

    def body(k_ref, v_ref, kpe_ref, q_ref, do_ref, lse_ref, dl_ref, dq_ref, dk_ref, dv_ref, dka_ref, dva_ref,
             dqt_ref):
        kj = pl.program_id(1)
        k = k_ref[...] + kpe_ref[...]
        kt = jnp.transpose(k.astype(F32)).astype(BF16)
        v = v_ref[...]

        @pl.when(kj == 0)
        def _():
            dqt_ref[...] = jnp.zeros_like(dqt_ref)

        dka_ref[...] = jnp.zeros_like(dka_ref)
        dva_ref[...] = jnp.zeros_like(dva_ref)

        def tile(i, masked):
            start = pl.multiple_of(i * tq, tq)
            q = q_ref[pl.ds(start, tq), :]
            do = do_ref[pl.ds(start, tq), :]
            t = lax.dot_general(k, q, nt, preferred_element_type=F32) * c2
            if masked:
                krow = lax.broadcasted_iota(jnp.int32, (tk, tq), 0)
                qcol = lax.broadcasted_iota(jnp.int32, (tk, tq), 1)
                t = jnp.where(krow + (kj % w) * tk <= qcol, t, NEG)
            pt = jnp.exp2(t - lse_ref[i])
            dva_ref[...] += jnp.dot(pt.astype(BF16), do, preferred_element_type=F32)
            dpt = lax.dot_general(v, do, nt, preferred_element_type=F32)
            dst = (pt * (dpt - dl_ref[i]) * scale).astype(BF16)
            dka_ref[...] += jnp.dot(dst, q, preferred_element_type=F32)
            dqt_ref[i] += jnp.dot(kt, dst, preferred_element_type=F32)

        tile(kj // w, True)

        def unmasked(i, carry):
            tile(i, False)
            return carry

        lax.fori_loop(kj // w + 1, nq, unmasked, 0)
        dk_ref[...] = dka_ref[...]
        dv_ref[...] = dva_ref[...]

        @pl.when(kj == nk - 1)
        def _():
            for c in range(nq):
                dq_ref[c * tq:(c + 1) * tq, :] = jnp.transpose(dqt_ref[c])

    blk = pl.BlockSpec((tk, HEAD_PAD), lambda h, j: (j, h))
    whole = pl.BlockSpec((S, HEAD_PAD), lambda h, j: (0, h))
    stat = pl.BlockSpec((None, nq, 1, tq), lambda h, j: (h, 0, 0, 0))
    shp = jax.ShapeDtypeStruct((S, H * HEAD_PAD), F32)
    return pl.pallas_call(
        body, name="attn_bwd",
        grid=(H, nk),
        in_specs=[blk, pl.BlockSpec((tk, HEAD_PAD), lambda h, j: (j, H + h)),
                  pl.BlockSpec((tk, HEAD_PAD), lambda h, j: (j, 0)), whole, whole, stat, stat],
        out_specs=[whole, blk, blk],
        out_shape=[shp, shp, shp],
        scratch_shapes=[pltpu.VMEM((tk, HEAD_PAD), F32), pltpu.VMEM((tk, HEAD_PAD), F32),
                        pltpu.VMEM((nq, HEAD_PAD, tq), F32)],
        compiler_params=_params(("parallel", "arbitrary")),
    )(kv, kv, kpe, qr, do, lse2, delta4)


def _pad_heads(w, width):
    R = w.shape[0]
    w3 = w.reshape(R, -1, width)
    return jnp.pad(w3, ((0, 0), (0, 0), (0, HEAD_PAD - width))).reshape(R, -1)


def _unpad_heads(w, width):
    R = w.shape[0]
    return w.reshape(R, -1, HEAD_PAD)[:, :, :width].reshape(R, -1)


def mla_pad_weights(p):
    H = N_HEADS
    w_q_b = _pad_heads(p["w_q_b"], QK_NOPE + QK_ROPE)
    kvb = p["w_kv_b"].reshape(KV_LORA, H, QK_NOPE + V_HEAD)
    wk = _pad_heads(kvb[:, :, :QK_NOPE].reshape(KV_LORA, -1), QK_NOPE)
    wv = _pad_heads(kvb[:, :, QK_NOPE:].reshape(KV_LORA, -1), V_HEAD)
    D = p["w_kv_a"].shape[0]
    a = p["w_kv_a"]
    w_kv_a = jnp.concatenate([a[:, :KV_LORA], jnp.zeros((D, QK_NOPE), a.dtype), a[:, KV_LORA:],
                              jnp.zeros((D, HEAD_PAD - QK_NOPE - QK_ROPE), a.dtype)], axis=1)
    wo = p["w_o"].reshape(H, V_HEAD, -1)
    w_o = jnp.pad(wo, ((0, 0), (0, HEAD_PAD - V_HEAD), (0, 0))).reshape(H * HEAD_PAD, -1)
    return dict(w_q_a=p["w_q_a"], w_q_b=w_q_b, w_kv_b=jnp.concatenate([wk, wv], axis=1), w_kv_a=w_kv_a, w_o=w_o)


def mla_kv_fwd(h, g, sh, sc, kv_a_norm_g, pw, tabs):
    hkv = norm_mod(h, g, sh, sc, "kv_norm_mod")
    ckvp = mm(hkv, pw["w_kv_a"], "nn", "kv_a")

    def f(ckv, kpe, c, s1, s2, g):
        xhat, _ = _rms(ckv)
        return (xhat * g).astype(BF16), _rope(kpe, c, s1, s2).astype(BF16)
    ckv_n, kpe_r = rowwise(f, [(ckvp, KV_LORA, 0), (ckvp, HEAD_PAD, KV_LORA // HEAD_PAD), *tabs], [kv_a_norm_g],
                           [(KV_LORA, BF16), (HEAD_PAD, BF16)], [], "kv_a_norm_rope")
    kv = mm(ckv_n, pw["w_kv_b"], "nn", "kv_b", out_dtype=BF16)
    return kv, kpe_r, (h, hkv, ckvp, ckv_n)


def mla_kv_bwd(dh_stream, dk, dv, saved, g, sc, kv_a_norm_g, pw, tabs):
    h, hkv, ckvp, ckv_n = saved
    H = N_HEADS
    lane = jnp.arange(HEAD_PAD)
    pe_mask = ((lane >= QK_NOPE) & (lane < QK_NOPE + QK_ROPE)).astype(F32)[None, :]

    def f(dk, dv, c, s1, s2, mask):
        tot = dk[:, :HEAD_PAD]
        for hh in range(1, H):
            tot = tot + dk[:, hh * HEAD_PAD:(hh + 1) * HEAD_PAD]
        dkpe = _rope_t(tot * mask, c, s1, s2) * mask
        return jnp.concatenate([dk, dv], axis=1).astype(BF16), dkpe
    dkv, dkpe = rowwise(f, [dk, dv, *tabs], [pe_mask], [(2 * H * HEAD_PAD, BF16), (HEAD_PAD, F32)], [],
                        "kv_split_bwd")
    d_w_kv_b = mm(ckv_n, dkv, "tn", "kv_b_dw")
    dckv_n = mm(dkv, pw["w_kv_b"], "nt", "kv_b_dx")

    def f2(ckv, dn, dkpe, g):
        xhat, r = _rms(ckv)
        dx = _rms_bwd(xhat, r, dn * g)
        return jnp.concatenate([dx, dkpe], axis=1).astype(BF16), dn * xhat
    dckvp, d_kv_a_g = rowwise(f2, [(ckvp, KV_LORA, 0), dckv_n, dkpe], [kv_a_norm_g],
                              [(KV_LORA + HEAD_PAD, BF16)], [KV_LORA], "kv_a_norm_bwd")
    d_w_kv_a = mm(hkv, dckvp, "tn", "kv_a_dw")
    dhkv = mm(dckvp, pw["w_kv_a"], "nt", "kv_a_dx")
    dh, d_sh, d_sc, d_g = norm_mod_bwd(h, dhkv, dh_stream, g, sc, "norm_mod_bwd")
    return dh, (d_sh, d_sc, d_g), d_kv_a_g, d_w_kv_a, d_w_kv_b


def mla_fwd(h, g, sh, sc, gate, q_a_norm_g, pw, kv, kpe_r, tabs):
    H = N_HEADS
    hn = norm_mod(h, g, sh, sc, "mla_norm_mod")
    qa = mm(hn, pw["w_q_a"], "nn", "q_a")

    def f(qa, g):
        xhat, _ = _rms(qa)
        return (xhat * g).astype(BF16)
    qa_n = rowwise(f, [qa], [q_a_norm_g], [(qa.shape[1], BF16)], [], "q_a_norm")[0]
    qp = mm(qa_n, pw["w_q_b"], "nn", "q_b")

    def frope(q, c, s1, s2):
        return jnp.concatenate([_rope(q[:, hh * HEAD_PAD:(hh + 1) * HEAD_PAD], c, s1, s2) for hh in range(H)],
                               axis=1).astype(BF16)
    qr = rowwise(frope, [qp, *tabs], [], [(H * HEAD_PAD, BF16)], [], "q_rope")[0]
    o, lse = attn_fwd(qr, kv, kpe_r, H)
    y = mm(o, pw["w_o"], "nn", "w_o")
    h_out, _ = residual(h, y, gate, 1.0, "mla_residual")
    return h_out, (h, hn, qa, qa_n, qr, o, lse, y)


def mla_bwd(dh_out, saved, g, sc, gate, q_a_norm_g, pw, kv, kpe_r, tabs):
    h, hn, qa, qa_n, qr, o, lse, y = saved
    H = N_HEADS
    dy, d_gate = residual_bwd(dh_out, y, gate, 1.0, "mla_residual_bwd")
    d_w_o = mm(o, dy, "tn", "w_o_dw")
    do = mm(dy, pw["w_o"], "nt", "w_o_dx", out_dtype=BF16)
    delta = attn_delta(o, do, H)
    dqr, dk, dv = attn_bwd(qr, kv, kpe_r, do, lse, delta, H)

    def frope_t(dq, c, s1, s2):
        return jnp.concatenate([_rope_t(dq[:, hh * HEAD_PAD:(hh + 1) * HEAD_PAD], c, s1, s2) for hh in range(H)],
                               axis=1).astype(BF16)
    dqp = rowwise(frope_t, [dqr, *tabs], [], [(H * HEAD_PAD, BF16)], [], "q_rope_bwd")[0]
    d_w_q_b = mm(qa_n, dqp, "tn", "q_b_dw")
    dqa_n = mm(dqp, pw["w_q_b"], "nt", "q_b_dx")

    def f(qa, dn, g):
        xhat, r = _rms(qa)
        return _rms_bwd(xhat, r, dn * g).astype(BF16), dn * xhat
    dqa, d_q_a_g = rowwise(f, [qa, dqa_n], [q_a_norm_g], [(qa.shape[1], BF16)], [qa.shape[1]], "q_a_norm_bwd")
    d_w_q_a = mm(hn, dqa, "tn", "q_a_dw")
    dhn = mm(dqa, pw["w_q_a"], "nt", "q_a_dx")
    dh_in, d_sh, d_sc, d_g = norm_mod_bwd(h, dhn, dh_out, g, sc, "norm_mod_bwd")
    grads = dict(w_q_a=d_w_q_a, q_a_norm_g=d_q_a_g, w_q_b=d_w_q_b, w_o=d_w_o)
    return dh_in, (d_sh, d_sc, d_gate, d_g), grads, dk, dv


def loss_head(h, target, g):
    D = h.shape[1]

    def f(h, t, g):
        xhat, r = _rms(h)
        err = xhat * g - t
        dy = err * (1.0 / D)
        dh = _rms_bwd(xhat, r, dy * g)
        return dh, (0.5 / D) * err * err, dy * xhat
    return rowwise(f, [h, target], [g], [(D, F32)], [D, D], "loss_head")


def _place():
    x, y, c = lax.axis_index("x"), lax.axis_index("y"), lax.axis_index("c")
    chips = [(1 - x, y), (x, 1 - y), (1 - x, 1 - y)]
    return x, y, c, chips


HBM_SPEC = pl.BlockSpec(memory_space=pltpu.HBM)


def all_gather8(v):
    m, n = v.shape

    def body(x_ref, out_ref, send_sems, recv_sems, local_sem):
        x, y, c, chips = _place()
        me, sibling = (x, y, c), (x, y, 1 - c)

        def rows(px, py, pc):
            return out_ref.at[4 * px + 2 * py + pc]

        def copy(k, block, to, src=None):
            return pltpu.make_async_remote_copy(
                src_ref=rows(*block) if src is None else src, dst_ref=rows(*block),
                send_sem=send_sems.at[k], recv_sem=recv_sems.at[k], device_id=to, device_id_type=MESH)

        mine = pltpu.make_async_copy(x_ref, rows(*me), local_sem)
        mine.start()
        first = [copy(0, me, sibling, src=x_ref)]
        first += [copy(1 + j, me, (*chip, c), src=x_ref) for j, chip in enumerate(chips)]
        for cp in first:
            cp.start()
        passed = [copy(4 + j, (*chip, c), sibling) for j, chip in enumerate(chips)]
        for j, chip in enumerate(chips):
            copy(1 + j, (*chip, c), me).wait_recv()
            passed[j].start()
        copy(0, sibling, me).wait_recv()
        for j, chip in enumerate(chips):
            copy(4 + j, (*chip, 1 - c), me).wait_recv()
        for cp in first + passed:
            cp.wait_send()
        mine.wait()

    return pl.pallas_call(
        body, name="all_gather8",
        out_shape=jax.ShapeDtypeStruct((8, m, n), v.dtype),
        in_specs=[pl.BlockSpec(memory_space=pltpu.VMEM)],
        out_specs=pl.BlockSpec(memory_space=pltpu.VMEM),
        scratch_shapes=[pltpu.SemaphoreType.DMA((7,)), pltpu.SemaphoreType.DMA((7,)), pltpu.SemaphoreType.DMA],
        compiler_params=pltpu.CompilerParams(vmem_limit_bytes=VMEM_LIMIT_BYTES),
    )(v)


def gather_weights(bufs):
    n = len(bufs)

    def body(*refs):
        ins, outs = refs[:n], refs[n:2 * n]
        send_sems, recv_sems = refs[2 * n:]
        x, y, c, chips = _place()
        across_x, across_y, across_both = chips
        sibling = (x, y, 1 - c)
        me = 2 * x + y
        via_in = (x + (1 - c) * (1 - 2 * x), y + c * (1 - 2 * y))
        via_out = (x + c * (1 - 2 * x), y + (1 - c) * (1 - 2 * y))

        def idx(chip):
            return 2 * chip[0] + chip[1]

        def copy(w, k, src, dst, to):
            return pltpu.make_async_remote_copy(src_ref=src, dst_ref=dst, send_sem=send_sems.at[6 * w + k],
                                                recv_sem=recv_sems.at[6 * w + k], device_id=to, device_id_type=MESH)

        def landed(w, k, chip):
            blk = outs[w].at[idx(chip), c]
            copy(w, k, blk, blk, (*chip, c)).wait_recv()
            return blk

        sends = [copy(w, j, ins[w].at[me, c], outs[w].at[me, c], (*chip, c))
                 for w in range(n) for j, chip in enumerate((across_x, across_y))]
        for cp in sends:
            cp.start()
        for w in range(n):
            blk = landed(w, c, via_in)
            sends += [copy(w, 2, blk, blk, (*via_out, c)), copy(w, 3 + c, blk, blk, sibling)]
            sends[-2].start()
            sends[-1].start()
        for w in range(n):
            blk = landed(w, 1 - c, via_out)
            sends.append(copy(w, 4 - c, blk, blk, sibling))
            sends[-1].start()
        for w in range(n):
            blk = landed(w, 2, across_both)
            sends.append(copy(w, 5, blk, blk, sibling))
            sends[-1].start()
        for w in range(n):
            for j, chip in enumerate(chips):
                other = outs[w].at[idx(chip), 1 - c]
                copy(w, 3 + j, other, other, sibling).wait_recv()
        for cp in sends:
            cp.wait_send()

    return pl.pallas_call(
        body, name="gather_weights",
        out_shape=[jax.ShapeDtypeStruct(b.shape, b.dtype) for b in bufs],
        in_specs=[HBM_SPEC] * n, out_specs=[HBM_SPEC] * n,
        input_output_aliases={w: w for w in range(n)},
        scratch_shapes=[pltpu.SemaphoreType.DMA((6 * n,)), pltpu.SemaphoreType.DMA((6 * n,))],
    )(*bufs)


SEM_SPEC = pl.BlockSpec(memory_space=pltpu.SEMAPHORE)
SPLIT_COPY = pltpu.CompilerParams(has_side_effects=pltpu.SideEffectType.DATAFLOW_SIDE_EFFECTING)
PEERS_PER_BLOCK = 6


def gather_start(groups, carried):
    flat = [a for grp in groups for a in grp]
    group_of = [g for g, grp in enumerate(groups) for _ in grp]
    n, n_g, n_all = len(flat), len(groups), len(flat) + len(carried)

    def body(*refs):
        ins, sems = refs[:n], refs[n_all:n_all + 2 * n_g]
        x, y, c, chips = _place()
        me = 2 * x + y
        for w in range(n):
            mine = ins[w].at[me, c]
            for chip in chips:
                for core in range(2):
                    pltpu.make_async_remote_copy(src_ref=mine, dst_ref=mine, send_sem=sems[2 * group_of[w]],
                                                 recv_sem=sems[2 * group_of[w] + 1], device_id=(*chip, core),
                                                 device_id_type=MESH).start()

    operands = flat + list(carried)
    res = pl.pallas_call(
        body, name="gather_start",
        out_shape=[pltpu.SemaphoreType.DMA(())] * (2 * n_g) + [pltpu.HBM(a.shape, a.dtype) for a in operands],
        in_specs=[HBM_SPEC] * n_all,
        out_specs=[SEM_SPEC] * (2 * n_g) + [HBM_SPEC] * n_all,
        input_output_aliases={w: 2 * n_g + w for w in range(n_all)},
        compiler_params=SPLIT_COPY,
    )(*[pltpu.with_memory_space_constraint(a, pltpu.HBM) for a in operands])
    sems = [(res[2 * g], res[2 * g + 1]) for g in range(n_g)]
    arrays, k = [], 2 * n_g
    for grp in groups:
        arrays.append(list(res[k:k + len(grp)]))
        k += len(grp)
    return sems, arrays, list(res[k:])


def gather_wait(arrays, sems, after, name):
    n = len(arrays)

    def body(*refs):
        ins, send_sem, recv_sem = refs[:n], refs[n], refs[n + 1]
        x, y, c, _ = _place()
        for w in range(n):
            half = ins[w].at[0, 0]
            cp = pltpu.make_async_remote_copy(src_ref=half, dst_ref=half, send_sem=send_sem, recv_sem=recv_sem,
                                              device_id=(x, y, c), device_id_type=MESH)
            for _ in range(PEERS_PER_BLOCK):
                cp.wait_send()
            for _ in range(PEERS_PER_BLOCK):
                cp.wait_recv()

    return pl.pallas_call(
        body, name=name,
        out_shape=[pltpu.HBM(a.shape, a.dtype) for a in arrays],
        in_specs=[HBM_SPEC] * n + [SEM_SPEC, SEM_SPEC, pl.BlockSpec(memory_space=pl.ANY)],
        out_specs=[HBM_SPEC] * n,
        input_output_aliases={w: w for w in range(n)},
        compiler_params=SPLIT_COPY,
    )(*arrays, *sems, after)


def exchange_halves(gs):
    n = len(gs)

    def body(*refs):
        ins, theirs = refs[:n], refs[n:2 * n]
        send_sems, recv_sems = refs[2 * n:]
        x, y, c, _ = _place()
        sends = [pltpu.make_async_remote_copy(src_ref=ins[w].at[:, 1 - c], dst_ref=theirs[w],
                                              send_sem=send_sems.at[w], recv_sem=recv_sems.at[w],
                                              device_id=(x, y, 1 - c), device_id_type=MESH) for w in range(n)]
        for cp in sends:
            cp.start()
        for cp in sends:
            cp.wait()

    return pl.pallas_call(
        body, name="exchange_halves",
        out_shape=[jax.ShapeDtypeStruct((4,) + g.shape[2:], g.dtype) for g in gs],
        in_specs=[HBM_SPEC] * n, out_specs=[HBM_SPEC] * n,
        scratch_shapes=[pltpu.SemaphoreType.DMA((n,)), pltpu.SemaphoreType.DMA((n,))],
    )(*gs)


def join_halves(qs):
    n = len(qs)

    def body(*refs):
        ins, outs = refs[:n], refs[n:2 * n]
        send_sems, recv_sems = refs[2 * n:]
        x, y, c, _ = _place()
        sends = [pltpu.make_async_remote_copy(src_ref=ins[w].at[c], dst_ref=outs[w].at[c], send_sem=send_sems.at[w],
                                              recv_sem=recv_sems.at[w], device_id=(x, y, 1 - c), device_id_type=MESH)
                 for w in range(n)]
        for cp in sends:
            cp.start()
        for w in range(n):
            other = outs[w].at[1 - c]
            pltpu.make_async_remote_copy(src_ref=other, dst_ref=other, send_sem=send_sems.at[w],
                                         recv_sem=recv_sems.at[w], device_id=(x, y, 1 - c),
                                         device_id_type=MESH).wait_recv()
        for cp in sends:
            cp.wait_send()

    return pl.pallas_call(
        body, name="join_halves",
        out_shape=[jax.ShapeDtypeStruct(q.shape, q.dtype) for q in qs],
        in_specs=[HBM_SPEC] * n, out_specs=[HBM_SPEC] * n,
        input_output_aliases={w: w for w in range(n)},
        scratch_shapes=[pltpu.SemaphoreType.DMA((n,)), pltpu.SemaphoreType.DMA((n,))],
    )(*qs)


def _row_tile(R, row_bytes):
    tm = R
    for t in (512, 256, 128, 64, 32, 16, 8):
        if R % t == 0:
            tm = t
            if t * row_bytes <= ROW_TILE_BUDGET:
                break
    return tm


def sum_siblings(g, theirs, place):
    _, _, R, C = g.shape
    tm = _row_tile(R, 3 * C * 4)

    def body(place_ref, a_ref, b_ref, o_ref):
        o_ref[...] = (a_ref[...] + b_ref[...]).astype(BF16)

    return pl.pallas_call(
        body, name="sum_siblings",
        grid_spec=pltpu.PrefetchScalarGridSpec(
            num_scalar_prefetch=1, grid=(4, R // tm),
            in_specs=[pl.BlockSpec((None, None, tm, C), lambda j, i, s: (j, s[1], i, 0)),
                      pl.BlockSpec((None, tm, C), lambda j, i, s: (j, i, 0))],
            out_specs=pl.BlockSpec((None, tm, C), lambda j, i, s: (j, i, 0))),
        out_shape=jax.ShapeDtypeStruct((4, R, C), BF16),
        compiler_params=_params(("parallel", "parallel")),
    )(place, g, theirs)


def sum_chips(p, landed, place):
    _, R, C = p.shape
    tm = _row_tile(R, 5 * C * 4)

    def body(place_ref, p_ref, l0_ref, l1_ref, l2_ref, o_ref):
        o_ref[...] = ((p_ref[...].astype(F32) + l0_ref[...].astype(F32)) + l1_ref[...].astype(F32)
                      ) + l2_ref[...].astype(F32)

    return pl.pallas_call(
        body, name="sum_chips",
        grid_spec=pltpu.PrefetchScalarGridSpec(
            num_scalar_prefetch=1, grid=(R // tm,),
            in_specs=[pl.BlockSpec((None, tm, C), lambda i, s: (s[0], i, 0))]
            + [pl.BlockSpec((None, tm, C), lambda i, s, j=j: (j, i, 0)) for j in range(3)],
            out_specs=pl.BlockSpec((None, tm, C), lambda i, s: (s[1], i, 0))),
        out_shape=jax.ShapeDtypeStruct((2, R, C), F32),
        compiler_params=_params(("parallel",)),
    )(place, p, landed, landed, landed)


def sum_blocks(items, name):
    R, C = items[0][0].shape[1:]
    tm = _row_tile(R, C * 4 * (len(items) + 1))
    n = len(items)

    def body(*refs):
        acc = refs[0][...].astype(F32)
        for r in refs[1:n]:
            acc = acc + r[...].astype(F32)
        refs[n][...] = acc

    return pl.pallas_call(
        body, name=name,
        grid=(R // tm,),
        in_specs=[pl.BlockSpec((None, tm, C), lambda i, j=j: (j, i, 0)) for _, j in items],
        out_specs=pl.BlockSpec((tm, C), lambda i: (i, 0)),
        out_shape=jax.ShapeDtypeStruct((R, C), F32),
        compiler_params=_params(("parallel",)),
    )(*[a for a, _ in items])


def scatter_start(ps, carried, name):
    n = len(ps)

    def body(*refs):
        ins, lands, sems = refs[:n], refs[n:2 * n], refs[2 * n + 1:2 * n + 3]
        x, y, c, chips = _place()
        for w in range(n):
            for j, chip in enumerate(chips):
                pltpu.make_async_remote_copy(src_ref=ins[w].at[2 * chip[0] + chip[1]], dst_ref=lands[w].at[j],
                                             send_sem=sems[0], recv_sem=sems[1], device_id=(*chip, c),
                                             device_id_type=MESH).start()

    operands = list(ps) + [lax.empty((3,) + p.shape[1:], p.dtype) for p in ps] + [carried]
    res = pl.pallas_call(
        body, name=name,
        out_shape=[pltpu.SemaphoreType.DMA(())] * 2 + [pltpu.HBM(a.shape, a.dtype) for a in operands],
        in_specs=[HBM_SPEC] * (2 * n + 1),
        out_specs=[SEM_SPEC] * 2 + [HBM_SPEC] * (2 * n + 1),
        input_output_aliases={w: 2 + w for w in range(2 * n + 1)},
        compiler_params=SPLIT_COPY,
    )(*[pltpu.with_memory_space_constraint(a, pltpu.HBM) for a in operands])
    return (res[0], res[1]), list(res[2:2 + n]), list(res[2 + n:2 + 2 * n]), res[-1]


def scatter_wait(ps, lands, sems, after, name):
    n = len(ps)

    def body(*refs):
        lands_in, send_sem, recv_sem = refs[n:2 * n], refs[2 * n], refs[2 * n + 1]
        x, y, c, _ = _place()
        for w in range(n):
            blk = lands_in[w].at[0]
            cp = pltpu.make_async_remote_copy(src_ref=blk, dst_ref=blk, send_sem=send_sem, recv_sem=recv_sem,
                                              device_id=(x, y, c), device_id_type=MESH)
            for _ in range(3):
                cp.wait_send()
            for _ in range(3):
                cp.wait_recv()

    operands = list(ps) + list(lands)
    res = pl.pallas_call(
        body, name=name,
        out_shape=[pltpu.HBM(a.shape, a.dtype) for a in operands],
        in_specs=[HBM_SPEC] * (2 * n) + [SEM_SPEC, SEM_SPEC, pl.BlockSpec(memory_space=pl.ANY)],
        out_specs=[HBM_SPEC] * (2 * n),
        input_output_aliases={w: w for w in range(2 * n)},
        compiler_params=SPLIT_COPY,
    )(*operands, *sems, after)
    return list(res[:n]), list(res[n:])


def reduce_start(gs, place, stream, name):
    theirs = exchange_halves(gs)
    sems, ps, lands, stream = scatter_start([sum_siblings(g, t, place) for g, t in zip(gs, theirs)], stream,
                                            "scatter_start_" + name)
    return (sems, ps, lands), stream


def reduce_finish(started, place, after, name):
    sems, ps, lands = started
    ps, lands = scatter_wait(ps, lands, sems, after, "scatter_wait_" + name)
    return [sum_chips(p, l, place) for p, l in zip(ps, lands)]


def adamw(w, g, m, v):
    shape = w.shape
    C = shape[-1]
    R = w.size // C

    def f(w, g, m, v):
        m = ADAM_B1 * m + (1.0 - ADAM_B1) * g
        v = ADAM_B2 * v + (1.0 - ADAM_B2) * (g * g)
        m_hat = m / (1.0 - ADAM_B1 ** ADAM_STEP)
        v_hat = v / (1.0 - ADAM_B2 ** ADAM_STEP)
        delta = -ADAM_LR * (m_hat / (jnp.sqrt(v_hat) + ADAM_EPS) + ADAM_WD * w)
        return delta, m, v

    d, nm, nv = rowwise(f, [a.reshape(R, C) for a in (w, g, m, v)], [], [(C, F32)] * 3, [], "adamw")
    return d.reshape(shape), nm.reshape(shape), nv.reshape(shape)


def _cast_into_slot(w, place):
    C = w.shape[-1]
    w2 = w.reshape(-1, C)
    R = w2.shape[0]
    tm = _row_tile(R, 6 * C)

    def body(place_ref, w_ref, o_ref):
        o_ref[...] = w_ref[...].astype(BF16)

    out = pl.pallas_call(
        body, name="cast_bf16",
        grid_spec=pltpu.PrefetchScalarGridSpec(
            num_scalar_prefetch=1, grid=(R // tm,),
            in_specs=[pl.BlockSpec((tm, C), lambda i, s: (i, 0))],
            out_specs=pl.BlockSpec((None, tm, C), lambda i, s: (s[0], i, 0))),
        out_shape=jax.ShapeDtypeStruct((4, R, C), BF16),
        compiler_params=_params(("parallel",)),
    )(place, w2)
    return out.reshape(4, 2, R // 2, C)


def _pack(vs):
    flat = jnp.concatenate([v.reshape(-1) for v in vs])
    n = flat.shape[0]
    total = -(-n // F32_TILE) * F32_TILE
    return jnp.pad(flat, (0, total - n)).reshape(total // LANES, LANES)


def _unpack(flat, like):
    out, o = [], 0
    for shp in like:
        sz = 1
        for d in shp:
            sz *= d
        out.append(flat[o:o + sz].reshape(shp))
        o += sz
    return out


def _cols_to_blocks(g, n_chips=4):
    R, N = g.shape
    C = N // n_chips
    return g.reshape(R, n_chips, C).transpose(1, 0, 2).reshape(n_chips, 2, R // 2, C)


def _rows_to_blocks(g, n_chips=4):
    R, C = g.shape
    return g.reshape(n_chips, 2, R // n_chips // 2, C)


def kernel(x, c, positions, ada_w, ada_b, norm_g, ffn_w13, ffn_w2, conv_w_pw1, conv_b_pw1, conv_w_dw, conv_b_dw, conv_ln_g, conv_ln_b, conv_w_pw2, conv_b_pw2, kv_ada_w, kv_ada_b, kv_norm_g, w_kv_a, kv_a_norm_g, w_kv_b, w_q_a, q_a_norm_g, w_q_b, w_o, final_norm_g, loss_target, m_ada_w, m_ada_b, m_norm_g, m_ffn_w13, m_ffn_w2, m_conv_w_pw1, m_conv_b_pw1, m_conv_w_dw, m_conv_b_dw, m_conv_ln_g, m_conv_ln_b, m_conv_w_pw2, m_conv_b_pw2, m_kv_ada_w, m_kv_ada_b, m_kv_norm_g, m_w_kv_a, m_kv_a_norm_g, m_w_kv_b, m_w_q_a, m_q_a_norm_g, m_w_q_b, m_w_o, m_final_norm_g, v_ada_w, v_ada_b, v_norm_g, v_ffn_w13, v_ffn_w2, v_conv_w_pw1, v_conv_b_pw1, v_conv_w_dw, v_conv_b_dw, v_conv_ln_g, v_conv_ln_b, v_conv_w_pw2, v_conv_b_pw2, v_kv_ada_w, v_kv_ada_b, v_kv_norm_g, v_w_kv_a, v_kv_a_norm_g, v_w_kv_b, v_w_q_a, v_q_a_norm_g, v_w_q_b, v_w_o, v_final_norm_g):
    S, D = x.shape[1], x.shape[2]
    H = N_HEADS
    F = ffn_w2.shape[2] * 4
    xi, yi, ci = lax.axis_index("x"), lax.axis_index("y"), lax.axis_index("c")
    chip = 2 * xi + yi
    dev = 2 * chip + ci
    place = jnp.stack([chip, ci]).astype(jnp.int32)
    h0 = x[0]
    target = loss_target[0]

    silu_c = rowwise(lambda a: a * _sigmoid(a), [c], [], [(D, F32)], [], "silu_c")[0]
    silu_all = all_gather8(silu_c.reshape(8, D // 8)).reshape(8, D)
    n_ada = ada_w.shape[2]
    n_kv = kv_ada_w.shape[1]
    ada_b_mine = lax.dynamic_slice_in_dim(ada_b, chip * n_ada, n_ada, axis=1)
    kv_b_mine = lax.dynamic_slice_in_dim(kv_ada_b, chip * n_kv, n_kv, axis=0)[None, :]
    mods = [mm(silu_all, ada_w[l], "nn", "ada_rows", bias=ada_b_mine[l:l + 1]) for l in range(2)]
    mods.append(mm(silu_all, kv_ada_w, "nn", "kv_ada_rows", bias=kv_b_mine))
    n_mod_cols = 2 * n_ada + n_kv
    mod_pack = jnp.concatenate(mods, axis=1).reshape(-1, LANES)
    mod_all = all_gather8(mod_pack).reshape(8, 8, n_mod_cols)[0::2]
    mod_mine = lax.dynamic_index_in_dim(mod_all, dev, axis=1, keepdims=False)
    mod = [mod_mine[:, l * n_ada:(l + 1) * n_ada].reshape(N_MOD, D) for l in range(2)]
    kv_mod = mod_mine[:, 2 * n_ada:].reshape(2, D)
    kv_shift, kv_scale = kv_mod[0:1], kv_mod[1:2]

    def mrow(l, k):
        return mod[l][k:k + 1]

    def slot(w):
        return _cast_into_slot(w, place)
    first = gather_weights([slot(ffn_w13[0, 0]), slot(ffn_w2[0, 0])])
    groups = [[slot(conv_w_pw1), slot(conv_w_pw2)],
              [slot(ffn_w13[0, 1]), slot(ffn_w2[0, 1])],
              [slot(w_kv_a), slot(w_kv_b), slot(ffn_w13[1, 0]), slot(ffn_w2[1, 0]), slot(w_q_a), slot(w_q_b), slot(w_o),
               slot(ffn_w13[1, 1]), slot(ffn_w2[1, 1])]]
    sems, started, first = gather_start(groups, first)

    def ffn_weights(w13_blocks, w2_blocks):
        return w13_blocks.reshape(4, 1, 1, D, F // 2), w2_blocks.reshape(F, D)
    small_like = [norm_g.shape, conv_b_pw1.shape, conv_w_dw.shape, conv_b_dw.shape, conv_ln_g.shape,
                  conv_ln_b.shape, conv_b_pw2.shape]
    small_pack = _pack([norm_g, conv_b_pw1, conv_w_dw, conv_b_dw, conv_ln_g, conv_ln_b, conv_b_pw2])
    small_all = all_gather8(small_pack)[0::2].reshape(4, -1)
    per_chip = [_unpack(small_all[j], small_like) for j in range(4)]
    smalls = [jnp.concatenate([per_chip[j][k] for j in range(4)], axis=-1) for k in range(len(small_like))]
    norm_g_f, b_pw1_f, w_dw_f, b_dw_f, ln_g_f, ln_b_f, b_pw2_f = smalls

    tabs = rope_tables(positions[0])

    def ng(l, k):
        return norm_g_f[l, k][None, :]

    h = h0
    ffn00 = ffn_weights(*first)
    h, s_f1_0 = ffn_fwd(h, ng(0, 0), mrow(0, 0), mrow(0, 1), mrow(0, 2), ffn00[0], 0, 0, ffn00[1])
    g_pw1, g_pw2 = gather_wait(started[0], sems[0], h, "gather_wait_conv")
    conv_p = dict(
        w_pw1=g_pw1.reshape(4, D, 2 * D // 4).transpose(1, 0, 2).reshape(D, 2 * D),
        b_pw1=b_pw1_f, w_dw=w_dw_f[0], b_dw=b_dw_f, ln_g=ln_g_f, ln_b=ln_b_f,
        w_pw2=g_pw2.reshape(D, D), b_pw2=b_pw2_f)
    h, s_conv = conv_module_fwd(h, ng(0, 1), mrow(0, 3), mrow(0, 4), mrow(0, 5), conv_p)
    ffn01 = ffn_weights(*gather_wait(started[1], sems[1], h, "gather_wait_ffn"))
    h, s_f2_0 = ffn_fwd(h, ng(0, 2), mrow(0, 6), mrow(0, 7), mrow(0, 8), ffn01[0], 0, 0, ffn01[1])
    (g_kv_a, g_kv_b, g_w13_10, g_w2_10, g_q_a, g_q_b, g_w_o, g_w13_11, g_w2_11) = gather_wait(
        started[2], sems[2], h, "gather_wait_layer1")
    ffn10, ffn11 = ffn_weights(g_w13_10, g_w2_10), ffn_weights(g_w13_11, g_w2_11)
    q_lora = w_q_a.shape[2]
    pw = mla_pad_weights(dict(
        w_kv_a=g_kv_a.reshape(D, KV_LORA + QK_ROPE),
        w_kv_b=g_kv_b.reshape(4, KV_LORA, -1).transpose(1, 0, 2).reshape(KV_LORA, -1),
        w_q_a=g_q_a.reshape(D, q_lora),
        w_q_b=g_q_b.reshape(4, q_lora, -1).transpose(1, 0, 2).reshape(q_lora, -1),
        w_o=g_w_o.reshape(H * V_HEAD, D)))
    kv_norm = kv_norm_g[None, :]
    kv_a_g = kv_a_norm_g[None, :]
    kv, kpe_r, s_kv = mla_kv_fwd(h, kv_norm, kv_shift, kv_scale, kv_a_g, pw, tabs)
    h, s_f1_1 = ffn_fwd(h, ng(1, 0), mrow(1, 0), mrow(1, 1), mrow(1, 2), ffn10[0], 0, 0, ffn10[1])
    h, s_mla = mla_fwd(h, ng(1, 1), mrow(1, 3), mrow(1, 4), mrow(1, 5), q_a_norm_g, pw, kv, kpe_r, tabs)
    h, s_f2_1 = ffn_fwd(h, ng(1, 2), mrow(1, 6), mrow(1, 7), mrow(1, 8), ffn11[0], 0, 0, ffn11[1])
    dh, loss_cols, d_final_g = loss_head(h, target, final_norm_g[None, :])

    def w13_blocks(dw):
        return dw.reshape(4, 2, D // 2, F // 2)

    dh, v_f2_1, dw13_11, dw2_11 = ffn_bwd(dh, s_f2_1, ng(1, 2), mrow(1, 7), mrow(1, 8), ffn11[0], 0, 0, ffn11[1])
    red_a, dh = reduce_start([w13_blocks(dw13_11), _rows_to_blocks(dw2_11)], place, dh, "a")
    dh, v_mla, g_mla, dk, dv = mla_bwd(dh, s_mla, ng(1, 1), mrow(1, 4), mrow(1, 5), q_a_norm_g, pw, kv, kpe_r, tabs)
    dh, v_f1_1, dw13_10, dw2_10 = ffn_bwd(dh, s_f1_1, ng(1, 0), mrow(1, 1), mrow(1, 2), ffn10[0], 0, 0, ffn10[1])
    dh, v_kv, d_kv_a_g, d_w_kv_a, d_w_kv_b = mla_kv_bwd(dh, dk, dv, s_kv, kv_norm, kv_scale, kv_a_g, pw, tabs)
    d_w_kv_a_u = jnp.concatenate([d_w_kv_a[:, :KV_LORA], d_w_kv_a[:, KV_LORA + QK_NOPE:KV_LORA + QK_NOPE + QK_ROPE]],
                                 axis=1)
    hk = H * HEAD_PAD
    dkb = jnp.concatenate([d_w_kv_b[:, :hk].reshape(KV_LORA, H, HEAD_PAD)[:, :, :QK_NOPE],
                           d_w_kv_b[:, hk:].reshape(KV_LORA, H, HEAD_PAD)[:, :, :V_HEAD]], axis=2).reshape(KV_LORA, -1)
    d_w_q_b_u = _unpad_heads(g_mla["w_q_b"], QK_NOPE + QK_ROPE)
    d_w_o_u = g_mla["w_o"].reshape(H, HEAD_PAD, D)[:, :V_HEAD].reshape(H * V_HEAD, D)
    q_w13_11, q_w2_11 = reduce_finish(red_a, place, dh, "a")
    red_b, dh = reduce_start([w13_blocks(dw13_10), _rows_to_blocks(dw2_10), _rows_to_blocks(d_w_kv_a_u),
                              _cols_to_blocks(dkb), _rows_to_blocks(g_mla["w_q_a"]), _cols_to_blocks(d_w_q_b_u),
                              _rows_to_blocks(d_w_o_u)], place, dh, "b")
    dh, v_f2_0, dw13_01, dw2_01 = ffn_bwd(dh, s_f2_0, ng(0, 2), mrow(0, 7), mrow(0, 8), ffn01[0], 0, 0, ffn01[1])
    dh, v_conv, g_conv = conv_module_bwd(dh, s_conv, ng(0, 1), mrow(0, 4), mrow(0, 5), conv_p)
    q_w13_10, q_w2_10, q_kv_a, q_kv_b, q_q_a, q_q_b, q_w_o = reduce_finish(red_b, place, dh, "b")
    red_c, dh = reduce_start([w13_blocks(dw13_01), _rows_to_blocks(dw2_01), _cols_to_blocks(g_conv["w_pw1"]),
                              _rows_to_blocks(g_conv["w_pw2"])], place, dh, "c")
    dh, v_f1_0, dw13_00, dw2_00 = ffn_bwd(dh, s_f1_0, ng(0, 0), mrow(0, 1), mrow(0, 2), ffn00[0], 0, 0, ffn00[1])
    grad_x = dh[None]
    q_w13_01, q_w2_01, q_pw1, q_pw2 = reduce_finish(red_c, place, dh, "c")
    def dmod(v1, vm, v2):
        return jnp.concatenate([v1[0], v1[1], v1[2], vm[0], vm[1], vm[2], v2[0], v2[1], v2[2]], axis=1)
    d_mod0 = dmod(v_f1_0, v_conv, v_f2_0)
    d_mod1 = dmod(v_f1_1, v_mla, v_f2_1)
    d_kv_mod = jnp.concatenate([v_kv[0], v_kv[1]], axis=1)
    d_norm_g = jnp.concatenate([v_f1_0[3], v_conv[3], v_f2_0[3], v_f1_1[3], v_mla[3], v_f2_1[3]], axis=0)
    vec_list = [d_mod0, d_mod1, d_kv_mod, d_norm_g, g_conv["b_pw1"], g_conv["w_dw"], g_conv["b_dw"], g_conv["ln_g"],
                g_conv["ln_b"], g_conv["b_pw2"], v_kv[2], d_kv_a_g, g_mla["q_a_norm_g"], d_final_g, loss_cols]
    vec_like = [v.shape for v in vec_list]
    vec_pack = _pack(vec_list)
    n_mod_rows = (2 * N_MOD * D + 2 * D) // LANES
    vec_all = all_gather8(vec_pack)
    red_d, silu_all = reduce_start([w13_blocks(dw13_00), _rows_to_blocks(dw2_00)], place, silu_all, "d")
    vec_sum = sum_blocks([(vec_all, d) for d in range(8)], "sum_devices").reshape(-1)
    (_, _, _, s_norm_g, s_b_pw1, s_w_dw, s_b_dw, s_ln_g, s_ln_b, s_b_pw2, s_kv_norm_g, s_kv_a_g, s_q_a_g,
     s_final_g, s_loss) = _unpack(vec_sum, vec_like)
    loss = jnp.sum(s_loss)
    dmod_all = vec_all[:, :n_mod_rows].reshape(8, 2 * N_MOD * D + 2 * D)
    dmod_sum = vec_sum[:2 * N_MOD * D + 2 * D]
    g_ada_b = dmod_sum[:2 * N_MOD * D].reshape(2, N_MOD * D)
    g_kv_ada_b = dmod_sum[2 * N_MOD * D:]
    g_ada_w = []
    for l in range(2):
        cols = lax.dynamic_slice_in_dim(dmod_all[:, l * N_MOD * D:(l + 1) * N_MOD * D], chip * n_ada, n_ada, axis=1)
        g_ada_w.append(mm(silu_all, cols, "tn", "ada_w_grad"))
    g_ada_w = jnp.stack(g_ada_w)
    kv_cols = lax.dynamic_slice_in_dim(dmod_all[:, 2 * N_MOD * D:], chip * n_kv, n_kv, axis=1)
    g_kv_ada_w = mm(silu_all, kv_cols, "tn", "kv_ada_w_grad")

    def shard(v, width):
        return lax.dynamic_slice_in_dim(v, chip * width, width, axis=v.ndim - 1)

    Dq = D // 4
    g_norm_g = shard(s_norm_g.reshape(2, 3, D), Dq)
    g_conv_b_pw1 = shard(s_b_pw1, 2 * D // 4)
    g_conv_w_dw = shard(s_w_dw, Dq)[None]
    g_conv_b_dw = shard(s_b_dw, Dq)
    g_conv_ln_g = shard(s_ln_g, Dq)
    g_conv_ln_b = shard(s_ln_b, Dq)
    g_conv_b_pw2 = shard(s_b_pw2, Dq)

    small = dict(ada_w=g_ada_w, ada_b=g_ada_b, norm_g=g_norm_g, conv_b_pw1=g_conv_b_pw1, conv_w_dw=g_conv_w_dw,
                 conv_b_dw=g_conv_b_dw, conv_ln_g=g_conv_ln_g, conv_ln_b=g_conv_ln_b, conv_b_pw2=g_conv_b_pw2,
                 kv_ada_w=g_kv_ada_w, kv_ada_b=g_kv_ada_b, kv_norm_g=s_kv_norm_g, kv_a_norm_g=s_kv_a_g,
                 q_a_norm_g=s_q_a_g, final_norm_g=s_final_g)
    order = ["ada_w", "ada_b", "norm_g", "ffn_w13", "ffn_w2", "conv_w_pw1", "conv_b_pw1", "conv_w_dw", "conv_b_dw",
             "conv_ln_g", "conv_ln_b", "conv_w_pw2", "conv_b_pw2", "kv_ada_w", "kv_ada_b", "kv_norm_g", "w_kv_a",
             "kv_a_norm_g", "w_kv_b", "w_q_a", "q_a_norm_g", "w_q_b", "w_o", "final_norm_g"]
    weights = [ada_w, ada_b, norm_g, ffn_w13, ffn_w2, conv_w_pw1, conv_b_pw1, conv_w_dw, conv_b_dw, conv_ln_g,
               conv_ln_b, conv_w_pw2, conv_b_pw2, kv_ada_w, kv_ada_b, kv_norm_g, w_kv_a, kv_a_norm_g, w_kv_b, w_q_a,
               q_a_norm_g, w_q_b, w_o, final_norm_g]
    ms = [m_ada_w, m_ada_b, m_norm_g, m_ffn_w13, m_ffn_w2, m_conv_w_pw1, m_conv_b_pw1, m_conv_w_dw, m_conv_b_dw,
          m_conv_ln_g, m_conv_ln_b, m_conv_w_pw2, m_conv_b_pw2, m_kv_ada_w, m_kv_ada_b, m_kv_norm_g, m_w_kv_a,
          m_kv_a_norm_g, m_w_kv_b, m_w_q_a, m_q_a_norm_g, m_w_q_b, m_w_o, m_final_norm_g]
    vs = [v_ada_w, v_ada_b, v_norm_g, v_ffn_w13, v_ffn_w2, v_conv_w_pw1, v_conv_b_pw1, v_conv_w_dw, v_conv_b_dw,
          v_conv_ln_g, v_conv_ln_b, v_conv_w_pw2, v_conv_b_pw2, v_kv_ada_w, v_kv_ada_b, v_kv_norm_g, v_w_kv_a,
          v_kv_a_norm_g, v_w_kv_b, v_w_q_a, v_q_a_norm_g, v_w_q_b, v_w_o, v_final_norm_g]
    state = {k: (w, m, v) for k, w, m, v in zip(order, weights, ms, vs)}
    results = {}

    def update(names, grads_by_name):
        for k in names:
            w, m, v = state[k]
            g = grads_by_name[k].reshape(w.shape)
            results[k] = (g, *adamw(w, g, m, v))

    update([k for k in order if k in small], small)
    q_w13_00, q_w2_00 = reduce_finish(red_d, place, dh, "d")
    red = [j.reshape(2 * j.shape[1], j.shape[2]) for j in join_halves(
        [q_w13_00, q_w13_01, q_w13_10, q_w13_11, q_w2_00, q_w2_01, q_w2_10, q_w2_11, q_pw1, q_pw2, q_kv_a, q_kv_b,
         q_q_a, q_q_b, q_w_o])]
    big = dict(ffn_w13=jnp.stack(red[0:4]), ffn_w2=jnp.stack(red[4:8]), conv_w_pw1=red[8], conv_w_pw2=red[9],
               w_kv_a=red[10], w_kv_b=red[11], w_q_a=red[12], w_q_b=red[13], w_o=red[14])
    update([k for k in order if k in big], big)
    outs = [results[k] for k in order]
    return (loss, grad_x, *[o[0] for o in outs], *[o[1] for o in outs], *[o[2] for o in outs], *[o[3] for o in outs])
```

```python
import jax
import jax.numpy as jnp
from jax import lax
from jax.experimental import pallas as pl
from jax.experimental.pallas import tpu as pltpu

F32 = jnp.float32
BF16 = jnp.bfloat16
MESH = pl.DeviceIdType.MESH

N_HEADS = 16
QK_NOPE = 64
QK_ROPE = 32
V_HEAD = 64
KV_LORA = 256
CONV_WIDTH = 31
ROPE_THETA = 10000.0
EPS = 1e-6
N_MOD = 9
HEAD_PAD = 128
ATTN_TILE = 512
CONV_HALO = 32

ADAM_LR = 0.001
ADAM_B1 = 0.9
ADAM_B2 = 0.999
ADAM_EPS = 1e-08
ADAM_WD = 0.01
ADAM_STEP = 10

VMEM_LIMIT_BYTES = 56 * 2 ** 20
ROW_TILE_BUDGET = 10 * 2 ** 20
MM_VMEM_BUDGET = 40 * 2 ** 20
LANES = 128
F32_TILE = 8 * LANES
NEG = float(jnp.finfo(jnp.float32).min)
LOG2_E = 1.4426950408889634


def _tile(n, prefs):
    for t in prefs:
        if n % t == 0:
            return t
    return n


def _params(sem):
    return pltpu.CompilerParams(dimension_semantics=sem, vmem_limit_bytes=VMEM_LIMIT_BYTES)


def _mm_tiles(M, N, K, mode, a_bytes, b_bytes, o_bytes):
    if mode == "tn":
        tk_opts = [t for t in (2048, 1024, 512, 256, 128) if K % t == 0] or [K]
        tm_opts = ([M] if M <= 2816 else []) + [t for t in (1024, 512, 256, 128) if M % t == 0 and t < M]
    else:
        tk_opts = [K]
        tm_opts = [t for t in (1024, 512, 256, 128) if M % t == 0] or [M]
    tn_opts = [t for t in (1408, 1024, 512, 384, 256, 128) if N % t == 0] or [N]

    def need(tm, tn, tk):
        blocks = 2 * (tm * tk * a_bytes + tk * tn * b_bytes + tm * tn * o_bytes)
        return blocks + (tm * tn * 4 if mode == "tn" else 0)

    tk_floor = next((t for t in tk_opts if t <= 512), tk_opts[-1])
    for tm in tm_opts:
        for tn in tn_opts:
            if need(tm, tn, tk_floor) <= MM_VMEM_BUDGET:
                return tm, tn, next(tk for tk in tk_opts if need(tm, tn, tk) <= MM_VMEM_BUDGET)
    return tm_opts[-1], tn_opts[-1], tk_opts[-1]


def mm(a, b, mode, name, out_dtype=F32, bias=None):
    if mode == "nn":
        (M, K), (K2, N) = a.shape, b.shape
        dims = (((1,), (0,)), ((), ()))
    elif mode == "nt":
        (M, K), (N, K2) = a.shape, b.shape
        dims = (((1,), (1,)), ((), ()))
    else:
        (K, M), (K2, N) = a.shape, b.shape
        dims = (((0,), (0,)), ((), ()))
    assert K == K2, (a.shape, b.shape, mode)
    tm, tn, tk = _mm_tiles(M, N, K, mode, a.dtype.itemsize, b.dtype.itemsize, jnp.dtype(out_dtype).itemsize)
    nk = K // tk
    if mode == "tn":
        a_spec = pl.BlockSpec((tk, tm), lambda i, j, k: (k, i))
        b_spec = pl.BlockSpec((tk, tn), lambda i, j, k: (k, j))
    elif mode == "nn":
        a_spec = pl.BlockSpec((tm, tk), lambda i, j, k: (i, k))
        b_spec = pl.BlockSpec((tk, tn), lambda i, j, k: (k, j))
    else:
        a_spec = pl.BlockSpec((tm, tk), lambda i, j, k: (i, k))
        b_spec = pl.BlockSpec((tn, tk), lambda i, j, k: (j, k))
    in_specs = [a_spec, b_spec]
    operands = [a, b]
    if bias is not None:
        in_specs.append(pl.BlockSpec((1, tn), lambda i, j, k: (0, j)))
        operands.append(bias)
    has_bias = bias is not None

    def body(*refs):
        a_ref, b_ref = refs[0], refs[1]
        bias_ref = refs[2] if has_bias else None
        o_ref = refs[3] if has_bias else refs[2]
        prod = lax.dot_general(a_ref[...].astype(BF16), b_ref[...].astype(BF16), dims,
                               preferred_element_type=F32)
        if nk == 1:
            if has_bias:
                prod = prod + bias_ref[...]
            o_ref[...] = prod.astype(o_ref.dtype)
        else:
            acc_ref = refs[-1]
            k = pl.program_id(2)

            @pl.when(k == 0)
            def _():
                acc_ref[...] = jnp.zeros_like(acc_ref)

            acc_ref[...] += prod

            @pl.when(k == nk - 1)
            def _():
                out = acc_ref[...]
                if has_bias:
                    out = out + bias_ref[...]
                o_ref[...] = out.astype(o_ref.dtype)

    return pl.pallas_call(
        body, name=name,
        grid=(M // tm, N // tn, nk),
        in_specs=in_specs,
        out_specs=pl.BlockSpec((tm, tn), lambda i, j, k: (i, j)),
        out_shape=jax.ShapeDtypeStruct((M, N), out_dtype),
        scratch_shapes=[pltpu.VMEM((tm, tn), F32)] if nk > 1 else [],
        compiler_params=_params(("parallel", "parallel", "arbitrary")),
    )(*operands)


def mm_fused(a, b, mode, name, tn, epi, epi_outs, pro=None, pro_rows=(), pro_vecs=(), pro_out=False, n_pro_sums=0,
             epi_rows=(), epi_vecs=(), b_blocks=None, n_cols=None):
    M, K = a.shape
    if b_blocks is not None:
        n_b, N = len(b_blocks), n_cols
    else:
        n_b = b.shape[0] if b.ndim == 3 else 1
        N = b.shape[-1] if mode == "nn" else b.shape[0]
    dims = (((1,), (0,)), ((), ())) if mode == "nn" else (((1,), (1,)), ((), ()))
    nj = N // tn
    epi_outs = [o if len(o) == 3 else (*o, None) for o in epi_outs]
    row_bytes = 2 * (K * a.dtype.itemsize + sum(K * r.dtype.itemsize for r in pro_rows) + (2 * K if pro_out else 0)
                     + sum(w * r.dtype.itemsize * (r.shape[0] if r.ndim == 3 else 1) for r, w in epi_rows)
                     + sum(w * jnp.dtype(dt).itemsize * (L or 1) for w, dt, L in epi_outs)
                     ) + (2 * K if pro is not None else 0)
    fixed = 2 * n_b * K * tn * b.dtype.itemsize
    tm = next((t for t in (1024, 512, 256, 128) if M % t == 0 and t * row_bytes + fixed <= MM_VMEM_BUDGET), M)
    row = lambda i, j: (i, 0)
    tile = lambda i, j: (i, j)
    stack = lambda i, j: (0, i, j)
    in_specs = [pl.BlockSpec((tm, K), row)] + [pl.BlockSpec((tm, K), row) for _ in pro_rows]
    in_specs += [pl.BlockSpec(v.shape, lambda i, j: (0, 0)) for v in pro_vecs]
    if b_blocks is not None:
        in_specs += [pl.BlockSpec(shape, imap) for shape, imap in b_blocks]
    elif b.ndim == 3:
        in_specs += [pl.BlockSpec((None, K, tn), lambda i, j, h=h: (h, 0, j)) for h in range(n_b)]
    elif mode == "nn":
        in_specs += [pl.BlockSpec((K, tn), lambda i, j: (0, j))]
    else:
        in_specs += [pl.BlockSpec((tn, K), lambda i, j: (j, 0))]
    in_specs += [pl.BlockSpec((r.shape[0], tm, w), stack) if r.ndim == 3 else pl.BlockSpec((tm, w), tile)
                 for r, w in epi_rows]
    in_specs += [pl.BlockSpec((1, tn), lambda i, j: (0, j)) for _ in epi_vecs]
    out_specs, out_shape = [], []
    if pro_out:
        out_specs.append(pl.BlockSpec((tm, K), row))
        out_shape.append(jax.ShapeDtypeStruct((M, K), BF16))
    for _ in range(n_pro_sums):
        out_specs.append(pl.BlockSpec((1, K), lambda i, j: (0, 0)))
        out_shape.append(jax.ShapeDtypeStruct((1, K), F32))
    for w, dt, L in epi_outs:
        out_specs.append(pl.BlockSpec((tm, w), tile) if L is None else pl.BlockSpec((L, tm, w), stack))
        out_shape.append(jax.ShapeDtypeStruct((M, nj * w) if L is None else (L, M, nj * w), dt))
    n_pr, n_pv, n_er, n_ev = len(pro_rows), len(pro_vecs), len(epi_rows), len(epi_vecs)
    n_a = 1 + n_pr + n_pv
    n_in = n_a + n_b + n_er + n_ev
    n_po = 1 if pro_out else 0

    def body(*refs):
        i, j = pl.program_id(0), pl.program_id(1)
        a_ref = refs[0]
        outs = refs[n_in:]
        if pro is not None:
            lhs_ref = refs[-1]

            @pl.when(j == 0)
            def _():
                res = pro(*[r[...] for r in refs[:1 + n_pr + n_pv]])
                if not isinstance(res, (tuple, list)):
                    res = (res,)
                lhs_ref[...] = res[0]
                if pro_out:
                    outs[0][...] = res[0]
                for s_ref, val in zip(outs[n_po:n_po + n_pro_sums], res[1:]):
                    part = jnp.sum(val.astype(F32), axis=0, keepdims=True)

                    @pl.when(i == 0)
                    def _(s_ref=s_ref, part=part):
                        s_ref[...] = part

                    @pl.when(i != 0)
                    def _(s_ref=s_ref, part=part):
                        s_ref[...] += part

            lhs = lhs_ref[...]
        else:
            lhs = a_ref[...].astype(BF16)
        accs = [lax.dot_general(lhs, b_ref[...].astype(BF16), dims, preferred_element_type=F32)
                for b_ref in refs[n_a:n_a + n_b]]
        res = epi(*accs, *[r[...] for r in refs[n_a + n_b:n_in]])
        if not isinstance(res, (tuple, list)):
            res = (res,)
        for o_ref, val in zip(outs[n_po + n_pro_sums:], res):
            if isinstance(val, (tuple, list)):
                for h, part in enumerate(val):
                    o_ref[h] = part.astype(o_ref.dtype)
            else:
                o_ref[...] = val.astype(o_ref.dtype)

    return pl.pallas_call(
        body, name=name,
        grid=(M // tm, nj),
        in_specs=in_specs, out_specs=out_specs, out_shape=out_shape,
        scratch_shapes=[pltpu.VMEM((tm, K), BF16)] if pro is not None else [],
        compiler_params=_params(("arbitrary", "arbitrary")),
    )(a, *pro_rows, *pro_vecs, *([b] * n_b), *[r for r, _ in epi_rows], *epi_vecs)


def rowwise(fn, rows, vecs, outs, sums, name, tm=None):
    norm = [(r, r.shape[1], 0) if not isinstance(r, tuple) else r for r in rows]
    S = norm[0][0].shape[0]
    if tm is None:
        tm = _row_tile(S, sum(w * r.dtype.itemsize for r, w, _ in norm)
                       + sum(n * jnp.dtype(dt).itemsize for n, dt in outs))
    n_rows, n_vecs, n_outs, n_sums = len(norm), len(vecs), len(outs), len(sums)
    in_specs = [pl.BlockSpec((tm, w), lambda i, cb=cb: (i, cb)) for _, w, cb in norm]
    in_specs += [pl.BlockSpec(v.shape, lambda i: (0, 0)) for v in vecs]
    out_specs = [pl.BlockSpec((tm, n), lambda i: (i, 0)) for n, _ in outs]
    out_specs += [pl.BlockSpec((1, n), lambda i: (0, 0)) for n in sums]
    out_shape = [jax.ShapeDtypeStruct((S, n), dt) for n, dt in outs]
    out_shape += [jax.ShapeDtypeStruct((1, n), F32) for n in sums]

    def body(*refs):
        ins = [r[...] for r in refs[:n_rows + n_vecs]]
        res = fn(*ins)
        if not isinstance(res, (tuple, list)):
            res = (res,)
        out_refs = refs[n_rows + n_vecs:]
        for o_ref, val in zip(out_refs[:n_outs], res[:n_outs]):
            o_ref[...] = val.astype(o_ref.dtype)
        if n_sums:
            i = pl.program_id(0)
            for s_ref, val in zip(out_refs[n_outs:], res[n_outs:]):
                part = jnp.sum(val.astype(F32), axis=0, keepdims=True)

                @pl.when(i == 0)
                def _(s_ref=s_ref, part=part):
                    s_ref[...] = part

                @pl.when(i != 0)
                def _(s_ref=s_ref, part=part):
                    s_ref[...] += part

    res = pl.pallas_call(
        body, name=name,
        grid=(S // tm,),
        in_specs=in_specs, out_specs=out_specs, out_shape=out_shape,
        compiler_params=_params(("arbitrary",) if n_sums else ("parallel",)),
    )(*[r for r, _, _ in norm], *vecs)
    return res


def _sigmoid(x):
    return jax.nn.sigmoid(x)


def _rms(x):
    r = lax.rsqrt(jnp.mean(x * x, axis=-1, keepdims=True) + EPS)
    return x * r, r


def _rms_bwd(xhat, r, dxhat):
    return r * (dxhat - xhat * jnp.mean(dxhat * xhat, axis=-1, keepdims=True))


def norm_mod(h, g, sh, sc, name):
    def f(h, g, sh, sc):
        xhat, _ = _rms(h)
        return ((xhat * g) * (1 + sc) + sh).astype(BF16)
    return rowwise(f, [h], [g, sh, sc], [(h.shape[1], BF16)], [], name)[0]


def norm_mod_bwd(h, dhn, dh_out, g, sc, name):
    D = h.shape[1]

    def f(h, dhn, dres, g, sc):
        xhat, r = _rms(h)
        dxn = dhn * (1 + sc)
        return _rms_bwd(xhat, r, dxn * g) + dres, dhn, dhn * (xhat * g), dxn * xhat

    return rowwise(f, [h, dhn, dh_out], [g, sc], [(D, F32)], [D, D, D], name)


def residual(h, y, gate, coef, name, bias=None):
    D = h.shape[1]
    if bias is None:
        def f(h, y, gate):
            return h + (coef * gate) * y
        return rowwise(f, [h, y], [gate], [(D, F32)], [], name)[0], y

    def fb(h, y, gate, bias):
        yb = y + bias
        return h + (coef * gate) * yb, yb
    return rowwise(fb, [h, y], [gate, bias], [(D, F32), (D, F32)], [], name)


def residual_bwd(dh_out, y, gate, coef, name, with_bias_sum=False):
    D = y.shape[1]

    def f(dh, y, gate):
        dy = (coef * gate) * dh
        res = (dy.astype(BF16), coef * dh * y)
        return res + ((dy,) if with_bias_sum else ())
    return rowwise(f, [dh_out, y], [gate], [(D, BF16)], [D, D] if with_bias_sum else [D], name)


def ffn_w13_dx(dab, gw13, l, i):
    _, S, F = dab.shape
    D, C = gw13.shape[3:]
    tm = _tile(S, (1024, 512, 256, 128))
    nt = (((1,), (1,)), ((), ()))

    def body(a_ref, b_ref, o_ref, acc_ref):
        k = pl.program_id(1)
        prod = lax.dot_general(a_ref[...], b_ref[...], nt, preferred_element_type=F32)

        @pl.when(k == 0)
        def _():
            acc_ref[...] = prod

        @pl.when((k > 0) & (k < 3))
        def _():
            acc_ref[...] += prod

        @pl.when(k == 3)
        def _():
            o_ref[...] = acc_ref[...] + prod

    return pl.pallas_call(
        body, name="ffn_w13_dx",
        grid=(S // tm, 4),
        in_specs=[pl.BlockSpec((None, tm, C), lambda r, k: (k // 2, r, k % 2)),
                  pl.BlockSpec((None, None, None, D, C), lambda r, k: (k, l, i, 0, 0))],
        out_specs=pl.BlockSpec((tm, D), lambda r, k: (r, 0)),
        out_shape=jax.ShapeDtypeStruct((S, D), F32),
        scratch_shapes=[pltpu.VMEM((tm, D), F32)],
        compiler_params=_params(("parallel", "arbitrary")),
    )(dab, gw13)


def ffn_w13_grad(hn, dab):
    S, D = hn.shape
    F = dab.shape[2]
    C = F // 2
    tk = next(t for t in (2048, 1024, 512, 256, 128) if S % t == 0)
    tn_dims = (((0,), (0,)), ((), ()))
    nk = S // tk

    def body(a_ref, b_ref, o_ref, acc_ref):
        k = pl.program_id(1)

        @pl.when(k == 0)
        def _():
            acc_ref[...] = jnp.zeros_like(acc_ref)

        acc_ref[...] += lax.dot_general(a_ref[...], b_ref[...], tn_dims, preferred_element_type=F32)

        @pl.when(k == nk - 1)
        def _():
            o_ref[...] = acc_ref[...]

    return pl.pallas_call(
        body, name="ffn_w13_dw",
        grid=(4, nk),
        in_specs=[pl.BlockSpec((tk, D), lambda j, k: (k, 0)),
                  pl.BlockSpec((None, tk, C), lambda j, k: (j // 2, k, j % 2))],
        out_specs=pl.BlockSpec((None, D, C), lambda j, k: (j, 0, 0)),
        out_shape=jax.ShapeDtypeStruct((4, D, C), F32),
        scratch_shapes=[pltpu.VMEM((D, C), F32)],
        compiler_params=_params(("parallel", "arbitrary")),
    )(hn, dab)


def ffn_fwd(h, g, sh, sc, gate, gw13, l, i, w2):
    F, D = w2.shape
    C = F // 2

    def norm(h, g, sh, sc):
        xhat, _ = _rms(h)
        return ((xhat * g) * (1 + sc) + sh).astype(BF16)

    def act(a, b):
        sig = _sigmoid(a)
        sa = a * sig
        return (b * (sig + sa * (1 - sig)), sa), sa * b
    blocks = [((None, None, None, D, C), lambda r, j, half=half: (2 * half + j, l, i, 0, 0)) for half in range(2)]
    hn, dt_dab, t = mm_fused(h, gw13, "nn", "ffn_w13", C, act, [(C, BF16, 2), (C, BF16)],
                             pro=norm, pro_vecs=[g, sh, sc], pro_out=True, b_blocks=blocks, n_cols=F)

    def res(acc, h, gate):
        return h + (0.5 * gate) * acc, acc
    h_out, y = mm_fused(t, w2, "nn", "ffn_w2", D, res, [(D, F32), (D, F32)], epi_rows=[(h, D)], epi_vecs=[gate])
    return h_out, (h, hn, dt_dab, t, y)


def ffn_bwd(dh_out, saved, g, sc, gate, gw13, l, i, w2, after_weight_grads=None):
    h, hn, dt_dab, t, y = saved
    F, D = w2.shape
    C = F // 2

    def scale(dh, y, gate):
        return ((0.5 * gate) * dh).astype(BF16), 0.5 * dh * y

    def act_bwd(dt, f):
        return ((dt * f[0].astype(F32), dt * f[1].astype(F32)),)
    dy, d_gate, dab = mm_fused(dh_out, w2, "nt", "ffn_w2_dx", C, act_bwd, [(C, BF16, 2)],
                               pro=scale, pro_rows=[y], pro_vecs=[gate], pro_out=True, n_pro_sums=1,
                               epi_rows=[(dt_dab, C)])
    dw2 = mm(t, dy, "tn", "ffn_w2_dw")
    dw13 = ffn_w13_grad(hn, dab)
    if after_weight_grads is not None:
        dab = after_weight_grads(dw13, dw2, dab)
    dhn = ffn_w13_dx(dab, gw13, l, i)
    dh_in, d_sh, d_sc, d_g = norm_mod_bwd(h, dhn, dh_out, g, sc, "norm_mod_bwd")
    return dh_in, (d_sh, d_sc, d_gate, d_g), dw13, dw2


def _shifted(xbuf, n):
    return [xbuf] + [pltpu.roll(xbuf, n - b, 0) for b in range(1, 8)]


def conv_fwd(u, w_dw, b_dw, ln_g, ln_b):
    S, D = u.shape
    tm = _tile(S, (256, 128))
    rc = 32
    first_tap = CONV_HALO - (CONV_WIDTH - 1)
    w = jnp.concatenate([w_dw, jnp.zeros((CONV_HALO - CONV_WIDTH, D), F32)], axis=0)

    def body(cur_ref, prev_ref, w_ref, b_ref, g_ref, beta_ref, z_ref, s_ref):
        i = pl.program_id(0)
        prev = jnp.where(i == 0, jnp.zeros((CONV_HALO, D), F32), prev_ref[...])
        xs = _shifted(jnp.concatenate([prev, cur_ref[...]], axis=0), tm + CONV_HALO)
        for c0 in range(0, tm, rc):
            acc = jnp.zeros((rc, D), F32)
            for k in range(CONV_WIDTH):
                off = first_tap + k
                a8, b = off // 8 * 8, off % 8
                acc = acc + w_ref[k:k + 1, :] * xs[b][c0 + a8:c0 + a8 + rc, :]
            z_ref[c0:c0 + rc, :] = acc + b_ref[...]
        z = z_ref[...]
        mu = jnp.mean(z, axis=-1, keepdims=True)
        zc = z - mu
        r = lax.rsqrt(jnp.mean(zc * zc, axis=-1, keepdims=True) + EPS)
        un = zc * r * g_ref[...] + beta_ref[...]
        s_ref[...] = (un * _sigmoid(un)).astype(BF16)

    nb = tm // CONV_HALO
    vec = pl.BlockSpec((1, D), lambda i: (0, 0))
    return pl.pallas_call(
        body, name="conv_fwd",
        grid=(S // tm,),
        in_specs=[pl.BlockSpec((tm, D), lambda i: (i, 0)),
                  pl.BlockSpec((CONV_HALO, D), lambda i: (jnp.maximum(i * nb - 1, 0), 0)),
                  pl.BlockSpec((CONV_HALO, D), lambda i: (0, 0)), vec, vec, vec],
        out_specs=[pl.BlockSpec((tm, D), lambda i: (i, 0)), pl.BlockSpec((tm, D), lambda i: (i, 0))],
        out_shape=[jax.ShapeDtypeStruct((S, D), F32), jax.ShapeDtypeStruct((S, D), BF16)],
        compiler_params=_params(("parallel",)),
    )(u, u, w, b_dw, ln_g, ln_b)


def conv_bwd(dz, u, w_dw):
    S, D = u.shape
    tm = _tile(S, (256, 128))
    rc = 32
    first_tap = CONV_HALO - (CONV_WIDTH - 1)
    w = jnp.concatenate([w_dw, jnp.zeros((CONV_HALO - CONV_WIDTH, D), F32)], axis=0)
    n_tiles = S // tm
    nb = tm // CONV_HALO

    def body(dz_ref, dzn_ref, u_ref, up_ref, w_ref, du_ref, dw_ref):
        i = pl.program_id(0)
        nxt = jnp.where(i == n_tiles - 1, jnp.zeros((CONV_HALO, D), F32), dzn_ref[...])
        dzs = _shifted(jnp.concatenate([dz_ref[...], nxt], axis=0), tm + CONV_HALO)
        for c0 in range(0, tm, rc):
            acc = jnp.zeros((rc, D), F32)
            for m in range(CONV_WIDTH):
                a8, b = m // 8 * 8, m % 8
                acc = acc + w_ref[CONV_WIDTH - 1 - m:CONV_WIDTH - m, :] * dzs[b][c0 + a8:c0 + a8 + rc, :]
            du_ref[c0:c0 + rc, :] = acc
        prev = jnp.where(i == 0, jnp.zeros((CONV_HALO, D), F32), up_ref[...])
        us = _shifted(jnp.concatenate([prev, u_ref[...]], axis=0), tm + CONV_HALO)
        dz = dz_ref[...]

        @pl.when(i == 0)
        def _():
            dw_ref[...] = jnp.zeros_like(dw_ref)

        for k in range(CONV_WIDTH):
            off = first_tap + k
            a8, b = off // 8 * 8, off % 8
            dw_ref[k:k + 1, :] += jnp.sum(dz * us[b][a8:a8 + tm, :], axis=0, keepdims=True)

    last_blk = S // CONV_HALO - 1
    du, dw = pl.pallas_call(
        body, name="conv_bwd",
        grid=(n_tiles,),
        in_specs=[pl.BlockSpec((tm, D), lambda i: (i, 0)),
                  pl.BlockSpec((CONV_HALO, D), lambda i: (jnp.minimum((i + 1) * nb, last_blk), 0)),
                  pl.BlockSpec((tm, D), lambda i: (i, 0)),
                  pl.BlockSpec((CONV_HALO, D), lambda i: (jnp.maximum(i * nb - 1, 0), 0)),
                  pl.BlockSpec((CONV_HALO, D), lambda i: (0, 0))],
        out_specs=[pl.BlockSpec((tm, D), lambda i: (i, 0)), pl.BlockSpec((CONV_HALO, D), lambda i: (0, 0))],
        out_shape=[jax.ShapeDtypeStruct((S, D), F32), jax.ShapeDtypeStruct((CONV_HALO, D), F32)],
        compiler_params=_params(("arbitrary",)),
    )(dz, dz, u, u, w)
    return du, dw[:CONV_WIDTH]


def conv_module_fwd(h, g, sh, sc, gate, p):
    D = h.shape[1]
    hn = norm_mod(h, g, sh, sc, "conv_norm_mod")
    pre = mm(hn, p["w_pw1"], "nn", "conv_pw1")
    ba, bg = p["b_pw1"][:, :D], p["b_pw1"][:, D:]

    def glu(a, gt, ba, bg):
        return (a + ba) * _sigmoid(gt + bg)
    u = rowwise(glu, [(pre, D, 0), (pre, D, 1)], [ba, bg], [(D, F32)], [], "conv_glu")[0]
    z, s = conv_fwd(u, p["w_dw"], p["b_dw"], p["ln_g"], p["ln_b"])
    yraw = mm(s, p["w_pw2"], "nn", "conv_pw2")
    h_out, y = residual(h, yraw, gate, 1.0, "conv_residual", bias=p["b_pw2"])
    return h_out, (h, hn, pre, u, z, s, y)


def conv_module_bwd(dh_out, saved, g, sc, gate, p):
    h, hn, pre, u, z, s, y = saved
    D = h.shape[1]
    dy, d_gate, d_b_pw2 = residual_bwd(dh_out, y, gate, 1.0, "conv_residual_bwd", with_bias_sum=True)
    d_w_pw2 = mm(s, dy, "tn", "conv_pw2_dw")
    ds = mm(dy, p["w_pw2"], "nt", "conv_pw2_dx")

    def ln_bwd(z, ds, g, beta):
        mu = jnp.mean(z, axis=-1, keepdims=True)
        zc = z - mu
        r = lax.rsqrt(jnp.mean(zc * zc, axis=-1, keepdims=True) + EPS)
        xhat = zc * r
        un = xhat * g + beta
        sig = _sigmoid(un)
        d_un = ds * (sig * (1 + un * (1 - sig)))
        dxhat = d_un * g
        dz = r * (dxhat - jnp.mean(dxhat, axis=-1, keepdims=True)
                  - xhat * jnp.mean(dxhat * xhat, axis=-1, keepdims=True))
        return dz, d_un * xhat, d_un, dz
    dz, d_ln_g, d_ln_b, d_b_dw = rowwise(ln_bwd, [z, ds], [p["ln_g"], p["ln_b"]], [(D, F32)], [D, D, D],
                                         "conv_ln_bwd")
    du, d_w_dw = conv_bwd(dz, u, p["w_dw"])
    ba, bg = p["b_pw1"][:, :D], p["b_pw1"][:, D:]

    def glu_bwd(a, gt, du, ba, bg):
        sg = _sigmoid(gt + bg)
        da = du * sg
        dg = du * (a + ba) * (sg * (1 - sg))
        dpre = jnp.concatenate([da, dg], axis=1)
        return dpre.astype(BF16), dpre
    dpre, d_b_pw1 = rowwise(glu_bwd, [(pre, D, 0), (pre, D, 1), du], [ba, bg], [(2 * D, BF16)], [2 * D],
                            "conv_glu_bwd")
    d_w_pw1 = mm(hn, dpre, "tn", "conv_pw1_dw")
    dhn = mm(dpre, p["w_pw1"], "nt", "conv_pw1_dx")
    dh_in, d_sh, d_sc, d_g = norm_mod_bwd(h, dhn, dh_out, g, sc, "norm_mod_bwd")
    grads = dict(w_pw1=d_w_pw1, b_pw1=d_b_pw1, w_dw=d_w_dw, b_dw=d_b_dw, ln_g=d_ln_g, ln_b=d_ln_b,
                 w_pw2=d_w_pw2, b_pw2=d_b_pw2)
    return dh_in, (d_sh, d_sc, d_gate, d_g), grads


def _rope(x, c, s1, s2):
    n = x.shape[1]
    return x * c + pltpu.roll(x, n - QK_ROPE // 2, 1) * s1 + pltpu.roll(x, QK_ROPE // 2, 1) * s2


def _rope_t(dy, c, s1, s2):
    n = dy.shape[1]
    return dy * c + pltpu.roll(dy * s1, QK_ROPE // 2, 1) + pltpu.roll(dy * s2, n - QK_ROPE // 2, 1)


def rope_tables(positions):
    inv_freq = ROPE_THETA ** (-jnp.arange(0, QK_ROPE, 2, dtype=F32) / QK_ROPE)
    ang = positions.astype(F32)[:, None] * inv_freq
    cos, sin = jnp.cos(ang), jnp.sin(ang)
    S = positions.shape[0]
    one = jnp.ones((S, QK_NOPE), F32)
    z16 = jnp.zeros((S, QK_ROPE // 2), F32)
    zn = jnp.zeros((S, QK_NOPE), F32)
    zt = jnp.zeros((S, HEAD_PAD - QK_NOPE - QK_ROPE), F32)
    c = jnp.concatenate([one, cos, cos, zt], axis=1)
    s1 = jnp.concatenate([zn, -sin, z16, zt], axis=1)
    s2 = jnp.concatenate([zn, z16, sin, zt], axis=1)
    return c, s1, s2


def attn_fwd(qr, kv, kpe, n_heads):
    S = qr.shape[0]
    H = n_heads
    tk = _tile(S, (ATTN_TILE,))
    nk = S // tk
    w = 2 if nk % 2 == 0 else 1
    tq = w * tk
    c2 = (QK_NOPE + QK_ROPE) ** -0.5 * LOG2_E
    nt = (((1,), (1,)), ((), ()))

    assert V_HEAD < HEAD_PAD
    ones_row = HEAD_PAD - 1

    def body(q_ref, k_ref, v_ref, kpe_ref, o_ref, lse_ref, kf_ref, vt_ref, m_ref, acc_ref):
        qi = pl.program_id(1)
        feature = lax.broadcasted_iota(jnp.int32, (HEAD_PAD, tk), 0)

        @pl.when(qi == 0)
        def _():
            kf_ref[...] = k_ref[...] + kpe_ref[...]
            for c in range(nk):
                vt = jnp.transpose(v_ref[c * tk:(c + 1) * tk, :].astype(F32))
                vt_ref[c] = jnp.where(feature == ones_row, 1.0, vt).astype(BF16)

        q = q_ref[...]
        m_ref[...] = jnp.full((1, tq), -jnp.inf, F32)
        acc_ref[...] = jnp.zeros((HEAD_PAD, tq), F32)

        def tile(j, first_visible):
            k = kf_ref[pl.ds(pl.multiple_of(j * tk, tk), tk), :]
            t = lax.dot_general(k, q, nt, preferred_element_type=F32) * c2
            if first_visible is not None:
                krow = lax.broadcasted_iota(jnp.int32, (tk, tq), 0)
                qcol = lax.broadcasted_iota(jnp.int32, (tk, tq), 1)
                t = jnp.where(krow + first_visible <= qcol, t, NEG)
            m_old = m_ref[...]
            m_new = jnp.maximum(m_old, jnp.max(t, axis=0, keepdims=True))
            alpha = jnp.exp2(m_old - m_new)
            p = jnp.exp2(t - m_new)
            acc_ref[...] = alpha * acc_ref[...] + jnp.dot(vt_ref[j], p.astype(BF16), preferred_element_type=F32)
            m_ref[...] = m_new

        def unmasked(j, carry):
            tile(j, None)
            return carry

        lax.fori_loop(0, w * qi, unmasked, 0)
        for u in range(w):
            tile(w * qi + u, u * tk)
        acc = acc_ref[...]
        l = acc_ref[ones_row:ones_row + 1, :]
        out_feature = lax.broadcasted_iota(jnp.int32, (HEAD_PAD, tq), 0)
        o_ref[...] = jnp.transpose(jnp.where(out_feature == ones_row, 0.0, acc / l))
        lse = m_ref[...] + jnp.log(l) * LOG2_E
        for u in range(w):
            lse_ref[u] = lse[:, u * tk:(u + 1) * tk]

    return pl.pallas_call(
        body, name="attn_fwd",
        grid=(H, S // tq),
        in_specs=[pl.BlockSpec((tq, HEAD_PAD), lambda h, i: (i, h)),
                  pl.BlockSpec((S, HEAD_PAD), lambda h, i: (0, h)),
                  pl.BlockSpec((S, HEAD_PAD), lambda h, i: (0, H + h)),
                  pl.BlockSpec((S, HEAD_PAD), lambda h, i: (0, 0))],
        out_specs=[pl.BlockSpec((tq, HEAD_PAD), lambda h, i: (i, h)),
                   pl.BlockSpec((None, w, 1, tk), lambda h, i: (h, i, 0, 0))],
        out_shape=[jax.ShapeDtypeStruct((S, H * HEAD_PAD), F32), jax.ShapeDtypeStruct((H, nk, 1, tk), F32)],
        scratch_shapes=[pltpu.VMEM((S, HEAD_PAD), BF16), pltpu.VMEM((nk, HEAD_PAD, tk), BF16),
                        pltpu.VMEM((1, tq), F32), pltpu.VMEM((HEAD_PAD, tq), F32)],
        compiler_params=_params(("parallel", "arbitrary")),
    )(qr, kv, kv, kpe)


def attn_delta(o, do, n_heads):
    S = o.shape[0]
    H = n_heads
    tq = _tile(S, (ATTN_TILE,))
    nq = S // tq

    def body(o_ref, do_ref, d_ref):
        for c in range(nq):
            rows = slice(c * tq, (c + 1) * tq)
            prod = o_ref[rows, :] * do_ref[rows, :].astype(F32)
            d_ref[c] = jnp.sum(jnp.transpose(prod), axis=0, keepdims=True)

    return pl.pallas_call(
        body, name="attn_delta",
        grid=(H,),
        in_specs=[pl.BlockSpec((S, HEAD_PAD), lambda h: (0, h)), pl.BlockSpec((S, HEAD_PAD), lambda h: (0, h))],
        out_specs=pl.BlockSpec((None, nq, 1, tq), lambda h: (h, 0, 0, 0)),
        out_shape=jax.ShapeDtypeStruct((H, nq, 1, tq), F32),
        compiler_params=_params(("parallel",)),
    )(o, do)


def attn_bwd(qr, kv, kpe, do, lse2, delta, n_heads):
    S = qr.shape[0]
    H = n_heads
    tk = _tile(S, (ATTN_TILE,))
    nk = S // tk
    w = 2 if nk % 2 == 0 else 1
    tq = w * tk
    nq = S // tq
    scale = (QK_NOPE + QK_ROPE) ** -0.5
    c2 = scale * LOG2_E
    nt = (((1,), (1,)), ((), ()))
    lse2 = lse2.reshape(H, nq, 1, tq)
    delta4 = delta.reshape(H, nq, 1, tq)

    def body(k_ref, v_ref, kpe_ref, q_ref, do_ref, lse_ref, dl_ref, dq_ref, dk_ref, dv_ref, dka_ref, dva_ref,
             dqt_ref):
        kj = pl.program_id(1)
        k = k_ref[...] + kpe_ref[...]
        kt = jnp.transpose(k.astype(F32)).astype(BF16)
        v = v_ref[...]

        @pl.when(kj == 0)
        def _():
            dqt_ref[...] = jnp.zeros_like(dqt_ref)

        dka_ref[...] = jnp.zeros_like(dka_ref)
        dva_ref[...] = jnp.zeros_like(dva_ref)

        def tile(i, masked):
            start = pl.multiple_of(i * tq, tq)
            q = q_ref[pl.ds(start, tq), :]
            do = do_ref[pl.ds(start, tq), :]
            t = lax.dot_general(k, q, nt, preferred_element_type=F32) * c2
            if masked:
                krow = lax.broadcasted_iota(jnp.int32, (tk, tq), 0)
                qcol = lax.broadcasted_iota(jnp.int32, (tk, tq), 1)
                t = jnp.where(krow + (kj % w) * tk <= qcol, t, NEG)
            pt = jnp.exp2(t - lse_ref[i])
            dva_ref[...] += jnp.dot(pt.astype(BF16), do, preferred_element_type=F32)
            dpt = lax.dot_general(v, do, nt, preferred_element_type=F32)
            dst = (pt * (dpt - dl_ref[i]) * scale).astype(BF16)
            dka_ref[...] += jnp.dot(dst, q, preferred_element_type=F32)
            dqt_ref[i] += jnp.dot(kt, dst, preferred_element_type=F32)

        tile(kj // w, True)

        def unmasked(i, carry):
            tile(i, False)
            return carry

        lax.fori_loop(kj // w + 1, nq, unmasked, 0)
        dk_ref[...] = dka_ref[...]
        dv_ref[...] = dva_ref[...]

        @pl.when(kj == nk - 1)
        def _():
            for c in range(nq):
                dq_ref[c * tq:(c + 1) * tq, :] = jnp.transpose(dqt_ref[c])

    blk = pl.BlockSpec((tk, HEAD_PAD), lambda h, j: (j, h))
    whole = pl.BlockSpec((S, HEAD_PAD), lambda h, j: (0, h))
    stat = pl.BlockSpec((None, nq, 1, tq), lambda h, j: (h, 0, 0, 0))
    shp = jax.ShapeDtypeStruct((S, H * HEAD_PAD), F32)
    return pl.pallas_call(
        body, name="attn_bwd",
        grid=(H, nk),
        in_specs=[blk, pl.BlockSpec((tk, HEAD_PAD), lambda h, j: (j, H + h)),
                  pl.BlockSpec((tk, HEAD_PAD), lambda h, j: (j, 0)), whole, whole, stat, stat],
        out_specs=[whole, blk, blk],
        out_shape=[shp, shp, shp],
        scratch_shapes=[pltpu.VMEM((tk, HEAD_PAD), F32), pltpu.VMEM((tk, HEAD_PAD), F32),
                        pltpu.VMEM((nq, HEAD_PAD, tq), F32)],
        compiler_params=_params(("parallel", "arbitrary")),
    )(kv, kv, kpe, qr, do, lse2, delta4)


def _pad_heads(w, width):
    R = w.shape[0]
    w3 = w.reshape(R, -1, width)
    return jnp.pad(w3, ((0, 0), (0, 0), (0, HEAD_PAD - width))).reshape(R, -1)


def _unpad_heads(w, width):
    R = w.shape[0]
    return w.reshape(R, -1, HEAD_PAD)[:, :, :width].reshape(R, -1)


def mla_pad_weights(p):
    H = N_HEADS
    w_q_b = _pad_heads(p["w_q_b"], QK_NOPE + QK_ROPE)
    kvb = p["w_kv_b"].reshape(KV_LORA, H, QK_NOPE + V_HEAD)
    wk = _pad_heads(kvb[:, :, :QK_NOPE].reshape(KV_LORA, -1), QK_NOPE)
    wv = _pad_heads(kvb[:, :, QK_NOPE:].reshape(KV_LORA, -1), V_HEAD)
    D = p["w_kv_a"].shape[0]
    a = p["w_kv_a"]
    w_kv_a = jnp.concatenate([a[:, :KV_LORA], jnp.zeros((D, QK_NOPE), a.dtype), a[:, KV_LORA:],
                              jnp.zeros((D, HEAD_PAD - QK_NOPE - QK_ROPE), a.dtype)], axis=1)
    wo = p["w_o"].reshape(H, V_HEAD, -1)
    w_o = jnp.pad(wo, ((0, 0), (0, HEAD_PAD - V_HEAD), (0, 0))).reshape(H * HEAD_PAD, -1)
    return dict(w_q_a=p["w_q_a"], w_q_b=w_q_b, w_kv_b=jnp.concatenate([wk, wv], axis=1), w_kv_a=w_kv_a, w_o=w_o)


def mla_kv_fwd(h, g, sh, sc, kv_a_norm_g, pw, tabs):
    hkv = norm_mod(h, g, sh, sc, "kv_norm_mod")
    ckvp = mm(hkv, pw["w_kv_a"], "nn", "kv_a")

    def f(ckv, kpe, c, s1, s2, g):
        xhat, _ = _rms(ckv)
        return (xhat * g).astype(BF16), _rope(kpe, c, s1, s2).astype(BF16)
    ckv_n, kpe_r = rowwise(f, [(ckvp, KV_LORA, 0), (ckvp, HEAD_PAD, KV_LORA // HEAD_PAD), *tabs], [kv_a_norm_g],
                           [(KV_LORA, BF16), (HEAD_PAD, BF16)], [], "kv_a_norm_rope")
    kv = mm(ckv_n, pw["w_kv_b"], "nn", "kv_b", out_dtype=BF16)
    return kv, kpe_r, (h, hkv, ckvp, ckv_n)


def mla_kv_bwd(dh_stream, dk, dv, saved, g, sc, kv_a_norm_g, pw, tabs):
    h, hkv, ckvp, ckv_n = saved
    H = N_HEADS
    lane = jnp.arange(HEAD_PAD)
    pe_mask = ((lane >= QK_NOPE) & (lane < QK_NOPE + QK_ROPE)).astype(F32)[None, :]

    def f(dk, dv, c, s1, s2, mask):
        tot = dk[:, :HEAD_PAD]
        for hh in range(1, H):
            tot = tot + dk[:, hh * HEAD_PAD:(hh + 1) * HEAD_PAD]
        dkpe = _rope_t(tot * mask, c, s1, s2) * mask
        return jnp.concatenate([dk, dv], axis=1).astype(BF16), dkpe
    dkv, dkpe = rowwise(f, [dk, dv, *tabs], [pe_mask], [(2 * H * HEAD_PAD, BF16), (HEAD_PAD, F32)], [],
                        "kv_split_bwd")
    d_w_kv_b = mm(ckv_n, dkv, "tn", "kv_b_dw")
    dckv_n = mm(dkv, pw["w_kv_b"], "nt", "kv_b_dx")

    def f2(ckv, dn, dkpe, g):
        xhat, r = _rms(ckv)
        dx = _rms_bwd(xhat, r, dn * g)
        return jnp.concatenate([dx, dkpe], axis=1).astype(BF16), dn * xhat
    dckvp, d_kv_a_g = rowwise(f2, [(ckvp, KV_LORA, 0), dckv_n, dkpe], [kv_a_norm_g],
                              [(KV_LORA + HEAD_PAD, BF16)], [KV_LORA], "kv_a_norm_bwd")
    d_w_kv_a = mm(hkv, dckvp, "tn", "kv_a_dw")
    dhkv = mm(dckvp, pw["w_kv_a"], "nt", "kv_a_dx")
    dh, d_sh, d_sc, d_g = norm_mod_bwd(h, dhkv, dh_stream, g, sc, "norm_mod_bwd")
    return dh, (d_sh, d_sc, d_g), d_kv_a_g, d_w_kv_a, d_w_kv_b


def mla_fwd(h, g, sh, sc, gate, q_a_norm_g, pw, kv, kpe_r, tabs):
    H = N_HEADS
    hn = norm_mod(h, g, sh, sc, "mla_norm_mod")
    qa = mm(hn, pw["w_q_a"], "nn", "q_a")

    def f(qa, g):
        xhat, _ = _rms(qa)
        return (xhat * g).astype(BF16)
    qa_n = rowwise(f, [qa], [q_a_norm_g], [(qa.shape[1], BF16)], [], "q_a_norm")[0]
    qp = mm(qa_n, pw["w_q_b"], "nn", "q_b")

    def frope(q, c, s1, s2):
        return jnp.concatenate([_rope(q[:, hh * HEAD_PAD:(hh + 1) * HEAD_PAD], c, s1, s2) for hh in range(H)],
                               axis=1).astype(BF16)
    qr = rowwise(frope, [qp, *tabs], [], [(H * HEAD_PAD, BF16)], [], "q_rope")[0]
    o, lse = attn_fwd(qr, kv, kpe_r, H)
    y = mm(o, pw["w_o"], "nn", "w_o")
    h_out, _ = residual(h, y, gate, 1.0, "mla_residual")
    return h_out, (h, hn, qa, qa_n, qr, o, lse, y)


def mla_bwd(dh_out, saved, g, sc, gate, q_a_norm_g, pw, kv, kpe_r, tabs):
    h, hn, qa, qa_n, qr, o, lse, y = saved
    H = N_HEADS
    dy, d_gate = residual_bwd(dh_out, y, gate, 1.0, "mla_residual_bwd")
    d_w_o = mm(o, dy, "tn", "w_o_dw")
    do = mm(dy, pw["w_o"], "nt", "w_o_dx", out_dtype=BF16)
    delta = attn_delta(o, do, H)
    dqr, dk, dv = attn_bwd(qr, kv, kpe_r, do, lse, delta, H)

    def frope_t(dq, c, s1, s2):
        return jnp.concatenate([_rope_t(dq[:, hh * HEAD_PAD:(hh + 1) * HEAD_PAD], c, s1, s2) for hh in range(H)],
                               axis=1).astype(BF16)
    dqp = rowwise(frope_t, [dqr, *tabs], [], [(H * HEAD_PAD, BF16)], [], "q_rope_bwd")[0]
    d_w_q_b = mm(qa_n, dqp, "tn", "q_b_dw")
    dqa_n = mm(dqp, pw["w_q_b"], "nt", "q_b_dx")

    def f(qa, dn, g):
        xhat, r = _rms(qa)
        return _rms_bwd(xhat, r, dn * g).astype(BF16), dn * xhat
    dqa, d_q_a_g = rowwise(f, [qa, dqa_n], [q_a_norm_g], [(qa.shape[1], BF16)], [qa.shape[1]], "q_a_norm_bwd")
    d_w_q_a = mm(hn, dqa, "tn", "q_a_dw")
    dhn = mm(dqa, pw["w_q_a"], "nt", "q_a_dx")
    dh_in, d_sh, d_sc, d_g = norm_mod_bwd(h, dhn, dh_out, g, sc, "norm_mod_bwd")
    grads = dict(w_q_a=d_w_q_a, q_a_norm_g=d_q_a_g, w_q_b=d_w_q_b, w_o=d_w_o)
    return dh_in, (d_sh, d_sc, d_gate, d_g), grads, dk, dv


def loss_head(h, target, g):
    D = h.shape[1]

    def f(h, t, g):
        xhat, r = _rms(h)
        err = xhat * g - t
        dy = err * (1.0 / D)
        dh = _rms_bwd(xhat, r, dy * g)
        return dh, (0.5 / D) * err * err, dy * xhat
    return rowwise(f, [h, target], [g], [(D, F32)], [D, D], "loss_head")


def _place():
    x, y, c = lax.axis_index("x"), lax.axis_index("y"), lax.axis_index("c")
    chips = [(1 - x, y), (x, 1 - y), (1 - x, 1 - y)]
    return x, y, c, chips


HBM_SPEC = pl.BlockSpec(memory_space=pltpu.HBM)


def all_gather8(v):
    m, n = v.shape

    def body(x_ref, out_ref, send_sems, recv_sems, local_sem):
        x, y, c, chips = _place()
        me, sibling = (x, y, c), (x, y, 1 - c)

        def rows(px, py, pc):
            return out_ref.at[4 * px + 2 * py + pc]

        def copy(k, block, to, src=None):
            return pltpu.make_async_remote_copy(
                src_ref=rows(*block) if src is None else src, dst_ref=rows(*block),
                send_sem=send_sems.at[k], recv_sem=recv_sems.at[k], device_id=to, device_id_type=MESH)

        mine = pltpu.make_async_copy(x_ref, rows(*me), local_sem)
        mine.start()
        first = [copy(0, me, sibling, src=x_ref)]
        first += [copy(1 + j, me, (*chip, c), src=x_ref) for j, chip in enumerate(chips)]
        for cp in first:
            cp.start()
        passed = [copy(4 + j, (*chip, c), sibling) for j, chip in enumerate(chips)]
        for j, chip in enumerate(chips):
            copy(1 + j, (*chip, c), me).wait_recv()
            passed[j].start()
        copy(0, sibling, me).wait_recv()
        for j, chip in enumerate(chips):
            copy(4 + j, (*chip, 1 - c), me).wait_recv()
        for cp in first + passed:
            cp.wait_send()
        mine.wait()

    return pl.pallas_call(
        body, name="all_gather8",
        out_shape=jax.ShapeDtypeStruct((8, m, n), v.dtype),
        in_specs=[pl.BlockSpec(memory_space=pltpu.VMEM)],
        out_specs=pl.BlockSpec(memory_space=pltpu.VMEM),
        scratch_shapes=[pltpu.SemaphoreType.DMA((7,)), pltpu.SemaphoreType.DMA((7,)), pltpu.SemaphoreType.DMA],
        compiler_params=pltpu.CompilerParams(vmem_limit_bytes=VMEM_LIMIT_BYTES),
    )(v)


def gather_weights(bufs):
    n = len(bufs)

    def body(*refs):
        ins, outs = refs[:n], refs[n:2 * n]
        send_sems, recv_sems = refs[2 * n:]
        x, y, c, chips = _place()
        across_x, across_y, across_both = chips
        sibling = (x, y, 1 - c)
        me = 2 * x + y
        via_in = (x + (1 - c) * (1 - 2 * x), y + c * (1 - 2 * y))
        via_out = (x + c * (1 - 2 * x), y + (1 - c) * (1 - 2 * y))

        def idx(chip):
            return 2 * chip[0] + chip[1]

        def copy(w, k, src, dst, to):
            return pltpu.make_async_remote_copy(src_ref=src, dst_ref=dst, send_sem=send_sems.at[6 * w + k],
                                                recv_sem=recv_sems.at[6 * w + k], device_id=to, device_id_type=MESH)

        def landed(w, k, chip):
            blk = outs[w].at[idx(chip), c]
            copy(w, k, blk, blk, (*chip, c)).wait_recv()
            return blk

        sends = [copy(w, j, ins[w].at[me, c], outs[w].at[me, c], (*chip, c))
                 for w in range(n) for j, chip in enumerate((across_x, across_y))]
        for cp in sends:
            cp.start()
        for w in range(n):
            blk = landed(w, c, via_in)
            sends += [copy(w, 2, blk, blk, (*via_out, c)), copy(w, 3 + c, blk, blk, sibling)]
            sends[-2].start()
            sends[-1].start()
        for w in range(n):
            blk = landed(w, 1 - c, via_out)
            sends.append(copy(w, 4 - c, blk, blk, sibling))
            sends[-1].start()
        for w in range(n):
            blk = landed(w, 2, across_both)
            sends.append(copy(w, 5, blk, blk, sibling))
            sends[-1].start()
        for w in range(n):
            for j, chip in enumerate(chips):
                other = outs[w].at[idx(chip), 1 - c]
                copy(w, 3 + j, other, other, sibling).wait_recv()
        for cp in sends:
            cp.wait_send()

    return pl.pallas_call(
        body, name="gather_weights",
        out_shape=[jax.ShapeDtypeStruct(b.shape, b.dtype) for b in bufs],
        in_specs=[HBM_SPEC] * n, out_specs=[HBM_SPEC] * n,
        input_output_aliases={w: w for w in range(n)},
        scratch_shapes=[pltpu.SemaphoreType.DMA((6 * n,)), pltpu.SemaphoreType.DMA((6 * n,))],
    )(*bufs)


SEM_SPEC = pl.BlockSpec(memory_space=pltpu.SEMAPHORE)
SPLIT_COPY = pltpu.CompilerParams(has_side_effects=pltpu.SideEffectType.DATAFLOW_SIDE_EFFECTING)
PEERS_PER_BLOCK = 6


def gather_start(groups, carried):
    flat = [a for grp in groups for a in grp]
    group_of = [g for g, grp in enumerate(groups) for _ in grp]
    n, n_g, n_all = len(flat), len(groups), len(flat) + len(carried)

    def body(*refs):
        ins, sems = refs[:n], refs[n_all:n_all + 2 * n_g]
        x, y, c, chips = _place()
        me = 2 * x + y
        for w in range(n):
            mine = ins[w].at[me, c]
            for chip in chips:
                for core in range(2):
                    pltpu.make_async_remote_copy(src_ref=mine, dst_ref=mine, send_sem=sems[2 * group_of[w]],
                                                 recv_sem=sems[2 * group_of[w] + 1], device_id=(*chip, core),
                                                 device_id_type=MESH).start()

    operands = flat + list(carried)
    res = pl.pallas_call(
        body, name="gather_start",
        out_shape=[pltpu.SemaphoreType.DMA(())] * (2 * n_g) + [pltpu.HBM(a.shape, a.dtype) for a in operands],
        in_specs=[HBM_SPEC] * n_all,
        out_specs=[SEM_SPEC] * (2 * n_g) + [HBM_SPEC] * n_all,
        input_output_aliases={w: 2 * n_g + w for w in range(n_all)},
        compiler_params=SPLIT_COPY,
    )(*[pltpu.with_memory_space_constraint(a, pltpu.HBM) for a in operands])
    sems = [(res[2 * g], res[2 * g + 1]) for g in range(n_g)]
    arrays, k = [], 2 * n_g
    for grp in groups:
        arrays.append(list(res[k:k + len(grp)]))
        k += len(grp)
    return sems, arrays, list(res[k:])


def gather_wait(arrays, sems, after, name):
    n = len(arrays)

    def body(*refs):
        ins, send_sem, recv_sem = refs[:n], refs[n], refs[n + 1]
        x, y, c, _ = _place()
        for w in range(n):
            half = ins[w].at[0, 0]
            cp = pltpu.make_async_remote_copy(src_ref=half, dst_ref=half, send_sem=send_sem, recv_sem=recv_sem,
                                              device_id=(x, y, c), device_id_type=MESH)
            for _ in range(PEERS_PER_BLOCK):
                cp.wait_send()
            for _ in range(PEERS_PER_BLOCK):
                cp.wait_recv()

    return pl.pallas_call(
        body, name=name,
        out_shape=[pltpu.HBM(a.shape, a.dtype) for a in arrays],
        in_specs=[HBM_SPEC] * n + [SEM_SPEC, SEM_SPEC, pl.BlockSpec(memory_space=pl.ANY)],
        out_specs=[HBM_SPEC] * n,
        input_output_aliases={w: w for w in range(n)},
        compiler_params=SPLIT_COPY,
    )(*arrays, *sems, after)


def exchange_halves(gs):
    n = len(gs)

    def body(*refs):
        ins, theirs = refs[:n], refs[n:2 * n]
        send_sems, recv_sems = refs[2 * n:]
        x, y, c, _ = _place()
        sends = [pltpu.make_async_remote_copy(src_ref=ins[w].at[:, 1 - c], dst_ref=theirs[w],
                                              send_sem=send_sems.at[w], recv_sem=recv_sems.at[w],
                                              device_id=(x, y, 1 - c), device_id_type=MESH) for w in range(n)]
        for cp in sends:
            cp.start()
        for cp in sends:
            cp.wait()

    return pl.pallas_call(
        body, name="exchange_halves",
        out_shape=[jax.ShapeDtypeStruct((4,) + g.shape[2:], g.dtype) for g in gs],
        in_specs=[HBM_SPEC] * n, out_specs=[HBM_SPEC] * n,
        scratch_shapes=[pltpu.SemaphoreType.DMA((n,)), pltpu.SemaphoreType.DMA((n,))],
    )(*gs)


def join_halves(qs):
    n = len(qs)

    def body(*refs):
        ins, outs = refs[:n], refs[n:2 * n]
        send_sems, recv_sems = refs[2 * n:]
        x, y, c, _ = _place()
        sends = [pltpu.make_async_remote_copy(src_ref=ins[w].at[c], dst_ref=outs[w].at[c], send_sem=send_sems.at[w],
                                              recv_sem=recv_sems.at[w], device_id=(x, y, 1 - c), device_id_type=MESH)
                 for w in range(n)]
        for cp in sends:
            cp.start()
        for w in range(n):
            other = outs[w].at[1 - c]
            pltpu.make_async_remote_copy(src_ref=other, dst_ref=other, send_sem=send_sems.at[w],
                                         recv_sem=recv_sems.at[w], device_id=(x, y, 1 - c),
                                         device_id_type=MESH).wait_recv()
        for cp in sends:
            cp.wait_send()

    return pl.pallas_call(
        body, name="join_halves",
        out_shape=[jax.ShapeDtypeStruct(q.shape, q.dtype) for q in qs],
        in_specs=[HBM_SPEC] * n, out_specs=[HBM_SPEC] * n,
        input_output_aliases={w: w for w in range(n)},
        scratch_shapes=[pltpu.SemaphoreType.DMA((n,)), pltpu.SemaphoreType.DMA((n,))],
    )(*qs)


def _row_tile(R, row_bytes):
    tm = R
    for t in (512, 256, 128, 64, 32, 16, 8):
        if R % t == 0:
            tm = t
            if t * row_bytes <= ROW_TILE_BUDGET:
                break
    return tm


def sum_siblings(g, theirs, place):
    _, _, R, C = g.shape
    tm = _row_tile(R, 3 * C * 4)

    def body(place_ref, a_ref, b_ref, o_ref):
        o_ref[...] = (a_ref[...] + b_ref[...]).astype(BF16)

    return pl.pallas_call(
        body, name="sum_siblings",
        grid_spec=pltpu.PrefetchScalarGridSpec(
            num_scalar_prefetch=1, grid=(4, R // tm),
            in_specs=[pl.BlockSpec((None, None, tm, C), lambda j, i, s: (j, s[1], i, 0)),
                      pl.BlockSpec((None, tm, C), lambda j, i, s: (j, i, 0))],
            out_specs=pl.BlockSpec((None, tm, C), lambda j, i, s: (j, i, 0))),
        out_shape=jax.ShapeDtypeStruct((4, R, C), BF16),
        compiler_params=_params(("parallel", "parallel")),
    )(place, g, theirs)


def sum_chips(p, landed, place):
    _, R, C = p.shape
    tm = _row_tile(R, 5 * C * 4)

    def body(place_ref, p_ref, l0_ref, l1_ref, l2_ref, o_ref):
        o_ref[...] = ((p_ref[...].astype(F32) + l0_ref[...].astype(F32)) + l1_ref[...].astype(F32)
                      ) + l2_ref[...].astype(F32)

    return pl.pallas_call(
        body, name="sum_chips",
        grid_spec=pltpu.PrefetchScalarGridSpec(
            num_scalar_prefetch=1, grid=(R // tm,),
            in_specs=[pl.BlockSpec((None, tm, C), lambda i, s: (s[0], i, 0))]
            + [pl.BlockSpec((None, tm, C), lambda i, s, j=j: (j, i, 0)) for j in range(3)],
            out_specs=pl.BlockSpec((None, tm, C), lambda i, s: (s[1], i, 0))),
        out_shape=jax.ShapeDtypeStruct((2, R, C), F32),
        compiler_params=_params(("parallel",)),
    )(place, p, landed, landed, landed)


def sum_blocks(items, name):
    R, C = items[0][0].shape[1:]
    tm = _row_tile(R, C * 4 * (len(items) + 1))
    n = len(items)

    def body(*refs):
        acc = refs[0][...].astype(F32)
        for r in refs[1:n]:
            acc = acc + r[...].astype(F32)
        refs[n][...] = acc

    return pl.pallas_call(
        body, name=name,
        grid=(R // tm,),
        in_specs=[pl.BlockSpec((None, tm, C), lambda i, j=j: (j, i, 0)) for _, j in items],
        out_specs=pl.BlockSpec((tm, C), lambda i: (i, 0)),
        out_shape=jax.ShapeDtypeStruct((R, C), F32),
        compiler_params=_params(("parallel",)),
    )(*[a for a, _ in items])


def scatter_start(ps, carried, name):
    n = len(ps)

    def body(*refs):
        ins, lands, sems = refs[:n], refs[n:2 * n], refs[2 * n + 1:2 * n + 3]
        x, y, c, chips = _place()
        for w in range(n):
            for j, chip in enumerate(chips):
                pltpu.make_async_remote_copy(src_ref=ins[w].at[2 * chip[0] + chip[1]], dst_ref=lands[w].at[j],
                                             send_sem=sems[0], recv_sem=sems[1], device_id=(*chip, c),
                                             device_id_type=MESH).start()

    operands = list(ps) + [lax.empty((3,) + p.shape[1:], p.dtype) for p in ps] + [carried]
    res = pl.pallas_call(
        body, name=name,
        out_shape=[pltpu.SemaphoreType.DMA(())] * 2 + [pltpu.HBM(a.shape, a.dtype) for a in operands],
        in_specs=[HBM_SPEC] * (2 * n + 1),
        out_specs=[SEM_SPEC] * 2 + [HBM_SPEC] * (2 * n + 1),
        input_output_aliases={w: 2 + w for w in range(2 * n + 1)},
        compiler_params=SPLIT_COPY,
    )(*[pltpu.with_memory_space_constraint(a, pltpu.HBM) for a in operands])
    return (res[0], res[1]), list(res[2:2 + n]), list(res[2 + n:2 + 2 * n]), res[-1]


def scatter_wait(ps, lands, sems, after, name):
    n = len(ps)

    def body(*refs):
        lands_in, send_sem, recv_sem = refs[n:2 * n], refs[2 * n], refs[2 * n + 1]
        x, y, c, _ = _place()
        for w in range(n):
            blk = lands_in[w].at[0]
            cp = pltpu.make_async_remote_copy(src_ref=blk, dst_ref=blk, send_sem=send_sem, recv_sem=recv_sem,
                                              device_id=(x, y, c), device_id_type=MESH)
            for _ in range(3):
                cp.wait_send()
            for _ in range(3):
                cp.wait_recv()

    operands = list(ps) + list(lands)
    res = pl.pallas_call(
        body, name=name,
        out_shape=[pltpu.HBM(a.shape, a.dtype) for a in operands],
        in_specs=[HBM_SPEC] * (2 * n) + [SEM_SPEC, SEM_SPEC, pl.BlockSpec(memory_space=pl.ANY)],
        out_specs=[HBM_SPEC] * (2 * n),
        input_output_aliases={w: w for w in range(2 * n)},
        compiler_params=SPLIT_COPY,
    )(*operands, *sems, after)
    return list(res[:n]), list(res[n:])


def reduce_start(gs, place, stream, name):
    theirs = exchange_halves(gs)
    sems, ps, lands, stream = scatter_start([sum_siblings(g, t, place) for g, t in zip(gs, theirs)], stream,
                                            "scatter_start_" + name)
    return (sems, ps, lands), stream


def reduce_finish(started, place, after, name):
    sems, ps, lands = started
    ps, lands = scatter_wait(ps, lands, sems, after, "scatter_wait_" + name)
    return [sum_chips(p, l, place) for p, l in zip(ps, lands)]


def adamw(w, g, m, v):
    shape = w.shape
    C = shape[-1]
    R = w.size // C

    def f(w, g, m, v):
        m = ADAM_B1 * m + (1.0 - ADAM_B1) * g
        v = ADAM_B2 * v + (1.0 - ADAM_B2) * (g * g)
        m_hat = m / (1.0 - ADAM_B1 ** ADAM_STEP)
        v_hat = v / (1.0 - ADAM_B2 ** ADAM_STEP)
        delta = -ADAM_LR * (m_hat / (jnp.sqrt(v_hat) + ADAM_EPS) + ADAM_WD * w)
        return delta, m, v

    d, nm, nv = rowwise(f, [a.reshape(R, C) for a in (w, g, m, v)], [], [(C, F32)] * 3, [], "adamw")
    return d.reshape(shape), nm.reshape(shape), nv.reshape(shape)


def _cast_into_slot(w, place):
    C = w.shape[-1]
    w2 = w.reshape(-1, C)
    R = w2.shape[0]
    tm = _row_tile(R, 6 * C)

    def body(place_ref, w_ref, o_ref):
        o_ref[...] = w_ref[...].astype(BF16)

    out = pl.pallas_call(
        body, name="cast_bf16",
        grid_spec=pltpu.PrefetchScalarGridSpec(
            num_scalar_prefetch=1, grid=(R // tm,),
            in_specs=[pl.BlockSpec((tm, C), lambda i, s: (i, 0))],
            out_specs=pl.BlockSpec((None, tm, C), lambda i, s: (s[0], i, 0))),
        out_shape=jax.ShapeDtypeStruct((4, R, C), BF16),
        compiler_params=_params(("parallel",)),
    )(place, w2)
    return out.reshape(4, 2, R // 2, C)


def _pack(vs):
    flat = jnp.concatenate([v.reshape(-1) for v in vs])
    n = flat.shape[0]
    total = -(-n // F32_TILE) * F32_TILE
    return jnp.pad(flat, (0, total - n)).reshape(total // LANES, LANES)


def _unpack(flat, like):
    out, o = [], 0
    for shp in like:
        sz = 1
        for d in shp:
            sz *= d
        out.append(flat[o:o + sz].reshape(shp))
        o += sz
    return out


def _cols_to_blocks(g, n_chips=4):
    R, N = g.shape
    C = N // n_chips
    return g.reshape(R, n_chips, C).transpose(1, 0, 2).reshape(n_chips, 2, R // 2, C)


def _rows_to_blocks(g, n_chips=4):
    R, C = g.shape
    return g.reshape(n_chips, 2, R // n_chips // 2, C)


def kernel(x, c, positions, ada_w, ada_b, norm_g, ffn_w13, ffn_w2, conv_w_pw1, conv_b_pw1, conv_w_dw, conv_b_dw, conv_ln_g, conv_ln_b, conv_w_pw2, conv_b_pw2, kv_ada_w, kv_ada_b, kv_norm_g, w_kv_a, kv_a_norm_g, w_kv_b, w_q_a, q_a_norm_g, w_q_b, w_o, final_norm_g, loss_target, m_ada_w, m_ada_b, m_norm_g, m_ffn_w13, m_ffn_w2, m_conv_w_pw1, m_conv_b_pw1, m_conv_w_dw, m_conv_b_dw, m_conv_ln_g, m_conv_ln_b, m_conv_w_pw2, m_conv_b_pw2, m_kv_ada_w, m_kv_ada_b, m_kv_norm_g, m_w_kv_a, m_kv_a_norm_g, m_w_kv_b, m_w_q_a, m_q_a_norm_g, m_w_q_b, m_w_o, m_final_norm_g, v_ada_w, v_ada_b, v_norm_g, v_ffn_w13, v_ffn_w2, v_conv_w_pw1, v_conv_b_pw1, v_conv_w_dw, v_conv_b_dw, v_conv_ln_g, v_conv_ln_b, v_conv_w_pw2, v_conv_b_pw2, v_kv_ada_w, v_kv_ada_b, v_kv_norm_g, v_w_kv_a, v_kv_a_norm_g, v_w_kv_b, v_w_q_a, v_q_a_norm_g, v_w_q_b, v_w_o, v_final_norm_g):
    S, D = x.shape[1], x.shape[2]
    H = N_HEADS
    F = ffn_w2.shape[2] * 4
    xi, yi, ci = lax.axis_index("x"), lax.axis_index("y"), lax.axis_index("c")
    chip = 2 * xi + yi
    dev = 2 * chip + ci
    place = jnp.stack([chip, ci]).astype(jnp.int32)
    h0 = x[0]
    target = loss_target[0]

    silu_c = rowwise(lambda a: a * _sigmoid(a), [c], [], [(D, F32)], [], "silu_c")[0]
    silu_all = all_gather8(silu_c.reshape(8, D // 8)).reshape(8, D)
    n_ada = ada_w.shape[2]
    n_kv = kv_ada_w.shape[1]
    ada_b_mine = lax.dynamic_slice_in_dim(ada_b, chip * n_ada, n_ada, axis=1)
    kv_b_mine = lax.dynamic_slice_in_dim(kv_ada_b, chip * n_kv, n_kv, axis=0)[None, :]
    mods = [mm(silu_all, ada_w[l], "nn", "ada_rows", bias=ada_b_mine[l:l + 1]) for l in range(2)]
    mods.append(mm(silu_all, kv_ada_w, "nn", "kv_ada_rows", bias=kv_b_mine))
    n_mod_cols = 2 * n_ada + n_kv
    mod_pack = jnp.concatenate(mods, axis=1).reshape(-1, LANES)
    mod_all = all_gather8(mod_pack).reshape(8, 8, n_mod_cols)[0::2]
    mod_mine = lax.dynamic_index_in_dim(mod_all, dev, axis=1, keepdims=False)
    mod = [mod_mine[:, l * n_ada:(l + 1) * n_ada].reshape(N_MOD, D) for l in range(2)]
    kv_mod = mod_mine[:, 2 * n_ada:].reshape(2, D)
    kv_shift, kv_scale = kv_mod[0:1], kv_mod[1:2]

    def mrow(l, k):
        return mod[l][k:k + 1]

    def slot(w):
        return _cast_into_slot(w, place)
    first = gather_weights([slot(ffn_w13[0, 0]), slot(ffn_w2[0, 0])])
    groups = [[slot(conv_w_pw1), slot(conv_w_pw2)],
              [slot(ffn_w13[0, 1]), slot(ffn_w2[0, 1])],
              [slot(w_kv_a), slot(w_kv_b), slot(ffn_w13[1, 0]), slot(ffn_w2[1, 0]), slot(w_q_a), slot(w_q_b), slot(w_o),
               slot(ffn_w13[1, 1]), slot(ffn_w2[1, 1])]]
    sems, started, first = gather_start(groups, first)

    def ffn_weights(w13_blocks, w2_blocks):
        return w13_blocks.reshape(4, 1, 1, D, F // 2), w2_blocks.reshape(F, D)
    small_like = [norm_g.shape, conv_b_pw1.shape, conv_w_dw.shape, conv_b_dw.shape, conv_ln_g.shape,
                  conv_ln_b.shape, conv_b_pw2.shape]
    small_pack = _pack([norm_g, conv_b_pw1, conv_w_dw, conv_b_dw, conv_ln_g, conv_ln_b, conv_b_pw2])
    small_all = all_gather8(small_pack)[0::2].reshape(4, -1)
    per_chip = [_unpack(small_all[j], small_like) for j in range(4)]
    smalls = [jnp.concatenate([per_chip[j][k] for j in range(4)], axis=-1) for k in range(len(small_like))]
    norm_g_f, b_pw1_f, w_dw_f, b_dw_f, ln_g_f, ln_b_f, b_pw2_f = smalls

    tabs = rope_tables(positions[0])

    def ng(l, k):
        return norm_g_f[l, k][None, :]

    h = h0
    ffn00 = ffn_weights(*first)
    h, s_f1_0 = ffn_fwd(h, ng(0, 0), mrow(0, 0), mrow(0, 1), mrow(0, 2), ffn00[0], 0, 0, ffn00[1])
    g_pw1, g_pw2 = gather_wait(started[0], sems[0], h, "gather_wait_conv")
    conv_p = dict(
        w_pw1=g_pw1.reshape(4, D, 2 * D // 4).transpose(1, 0, 2).reshape(D, 2 * D),
        b_pw1=b_pw1_f, w_dw=w_dw_f[0], b_dw=b_dw_f, ln_g=ln_g_f, ln_b=ln_b_f,
        w_pw2=g_pw2.reshape(D, D), b_pw2=b_pw2_f)
    h, s_conv = conv_module_fwd(h, ng(0, 1), mrow(0, 3), mrow(0, 4), mrow(0, 5), conv_p)
    ffn01 = ffn_weights(*gather_wait(started[1], sems[1], h, "gather_wait_ffn"))
    h, s_f2_0 = ffn_fwd(h, ng(0, 2), mrow(0, 6), mrow(0, 7), mrow(0, 8), ffn01[0], 0, 0, ffn01[1])
    (g_kv_a, g_kv_b, g_w13_10, g_w2_10, g_q_a, g_q_b, g_w_o, g_w13_11, g_w2_11) = gather_wait(
        started[2], sems[2], h, "gather_wait_layer1")
    ffn10, ffn11 = ffn_weights(g_w13_10, g_w2_10), ffn_weights(g_w13_11, g_w2_11)
    q_lora = w_q_a.shape[2]
    pw = mla_pad_weights(dict(
        w_kv_a=g_kv_a.reshape(D, KV_LORA + QK_ROPE),
        w_kv_b=g_kv_b.reshape(4, KV_LORA, -1).transpose(1, 0, 2).reshape(KV_LORA, -1),
        w_q_a=g_q_a.reshape(D, q_lora),
        w_q_b=g_q_b.reshape(4, q_lora, -1).transpose(1, 0, 2).reshape(q_lora, -1),
        w_o=g_w_o.reshape(H * V_HEAD, D)))
    kv_norm = kv_norm_g[None, :]
    kv_a_g = kv_a_norm_g[None, :]
    kv, kpe_r, s_kv = mla_kv_fwd(h, kv_norm, kv_shift, kv_scale, kv_a_g, pw, tabs)
    h, s_f1_1 = ffn_fwd(h, ng(1, 0), mrow(1, 0), mrow(1, 1), mrow(1, 2), ffn10[0], 0, 0, ffn10[1])
    h, s_mla = mla_fwd(h, ng(1, 1), mrow(1, 3), mrow(1, 4), mrow(1, 5), q_a_norm_g, pw, kv, kpe_r, tabs)
    h, s_f2_1 = ffn_fwd(h, ng(1, 2), mrow(1, 6), mrow(1, 7), mrow(1, 8), ffn11[0], 0, 0, ffn11[1])
    dh, loss_cols, d_final_g = loss_head(h, target, final_norm_g[None, :])

    def w13_blocks(dw):
        return dw.reshape(4, 2, D // 2, F // 2)

    dh, v_f2_1, dw13_11, dw2_11 = ffn_bwd(dh, s_f2_1, ng(1, 2), mrow(1, 7), mrow(1, 8), ffn11[0], 0, 0, ffn11[1])
    red_a, dh = reduce_start([w13_blocks(dw13_11), _rows_to_blocks(dw2_11)], place, dh, "a")
    dh, v_mla, g_mla, dk, dv = mla_bwd(dh, s_mla, ng(1, 1), mrow(1, 4), mrow(1, 5), q_a_norm_g, pw, kv, kpe_r, tabs)
    dh, v_f1_1, dw13_10, dw2_10 = ffn_bwd(dh, s_f1_1, ng(1, 0), mrow(1, 1), mrow(1, 2), ffn10[0], 0, 0, ffn10[1])
    dh, v_kv, d_kv_a_g, d_w_kv_a, d_w_kv_b = mla_kv_bwd(dh, dk, dv, s_kv, kv_norm, kv_scale, kv_a_g, pw, tabs)
    d_w_kv_a_u = jnp.concatenate([d_w_kv_a[:, :KV_LORA], d_w_kv_a[:, KV_LORA + QK_NOPE:KV_LORA + QK_NOPE + QK_ROPE]],
                                 axis=1)
    hk = H * HEAD_PAD
    dkb = jnp.concatenate([d_w_kv_b[:, :hk].reshape(KV_LORA, H, HEAD_PAD)[:, :, :QK_NOPE],
                           d_w_kv_b[:, hk:].reshape(KV_LORA, H, HEAD_PAD)[:, :, :V_HEAD]], axis=2).reshape(KV_LORA, -1)
    d_w_q_b_u = _unpad_heads(g_mla["w_q_b"], QK_NOPE + QK_ROPE)
    d_w_o_u = g_mla["w_o"].reshape(H, HEAD_PAD, D)[:, :V_HEAD].reshape(H * V_HEAD, D)
    q_w13_11, q_w2_11 = reduce_finish(red_a, place, dh, "a")
    red_b, dh = reduce_start([w13_blocks(dw13_10), _rows_to_blocks(dw2_10), _rows_to_blocks(d_w_kv_a_u),
                              _cols_to_blocks(dkb), _rows_to_blocks(g_mla["w_q_a"]), _cols_to_blocks(d_w_q_b_u),
                              _rows_to_blocks(d_w_o_u)], place, dh, "b")
    dh, v_f2_0, dw13_01, dw2_01 = ffn_bwd(dh, s_f2_0, ng(0, 2), mrow(0, 7), mrow(0, 8), ffn01[0], 0, 0, ffn01[1])
    dh, v_conv, g_conv = conv_module_bwd(dh, s_conv, ng(0, 1), mrow(0, 4), mrow(0, 5), conv_p)
    q_w13_10, q_w2_10, q_kv_a, q_kv_b, q_q_a, q_q_b, q_w_o = reduce_finish(red_b, place, dh, "b")
    red_c, dh = reduce_start([w13_blocks(dw13_01), _rows_to_blocks(dw2_01), _cols_to_blocks(g_conv["w_pw1"]),
                              _rows_to_blocks(g_conv["w_pw2"])], place, dh, "c")
    last_group = {}

    def start_last_group(dw13, dw2, dab):
        last_group["started"], dab = reduce_start([w13_blocks(dw13), _rows_to_blocks(dw2)], place, dab, "d")
        return dab

    dh, v_f1_0, _, _ = ffn_bwd(dh, s_f1_0, ng(0, 0), mrow(0, 1), mrow(0, 2), ffn00[0], 0, 0, ffn00[1],
                               after_weight_grads=start_last_group)
    grad_x = dh[None]
    q_w13_01, q_w2_01, q_pw1, q_pw2 = reduce_finish(red_c, place, dh, "c")
    def dmod(v1, vm, v2):
        return jnp.concatenate([v1[0], v1[1], v1[2], vm[0], vm[1], vm[2], v2[0], v2[1], v2[2]], axis=1)
    d_mod0 = dmod(v_f1_0, v_conv, v_f2_0)
    d_mod1 = dmod(v_f1_1, v_mla, v_f2_1)
    d_kv_mod = jnp.concatenate([v_kv[0], v_kv[1]], axis=1)
    d_norm_g = jnp.concatenate([v_f1_0[3], v_conv[3], v_f2_0[3], v_f1_1[3], v_mla[3], v_f2_1[3]], axis=0)
    vec_list = [d_mod0, d_mod1, d_kv_mod, d_norm_g, g_conv["b_pw1"], g_conv["w_dw"], g_conv["b_dw"], g_conv["ln_g"],
                g_conv["ln_b"], g_conv["b_pw2"], v_kv[2], d_kv_a_g, g_mla["q_a_norm_g"], d_final_g, loss_cols]
    vec_like = [v.shape for v in vec_list]
    vec_pack = _pack(vec_list)
    n_mod_rows = (2 * N_MOD * D + 2 * D) // LANES
    vec_all = all_gather8(vec_pack)
    vec_sum = sum_blocks([(vec_all, d) for d in range(8)], "sum_devices").reshape(-1)
    (_, _, _, s_norm_g, s_b_pw1, s_w_dw, s_b_dw, s_ln_g, s_ln_b, s_b_pw2, s_kv_norm_g, s_kv_a_g, s_q_a_g,
     s_final_g, s_loss) = _unpack(vec_sum, vec_like)
    loss = jnp.sum(s_loss)
    dmod_all = vec_all[:, :n_mod_rows].reshape(8, 2 * N_MOD * D + 2 * D)
    dmod_sum = vec_sum[:2 * N_MOD * D + 2 * D]
    g_ada_b = dmod_sum[:2 * N_MOD * D].reshape(2, N_MOD * D)
    g_kv_ada_b = dmod_sum[2 * N_MOD * D:]
    g_ada_w = []
    for l in range(2):
        cols = lax.dynamic_slice_in_dim(dmod_all[:, l * N_MOD * D:(l + 1) * N_MOD * D], chip * n_ada, n_ada, axis=1)
        g_ada_w.append(mm(silu_all, cols, "tn", "ada_w_grad"))
    g_ada_w = jnp.stack(g_ada_w)
    kv_cols = lax.dynamic_slice_in_dim(dmod_all[:, 2 * N_MOD * D:], chip * n_kv, n_kv, axis=1)
    g_kv_ada_w = mm(silu_all, kv_cols, "tn", "kv_ada_w_grad")

    def shard(v, width):
        return lax.dynamic_slice_in_dim(v, chip * width, width, axis=v.ndim - 1)

    Dq = D // 4
    g_norm_g = shard(s_norm_g.reshape(2, 3, D), Dq)
    g_conv_b_pw1 = shard(s_b_pw1, 2 * D // 4)
    g_conv_w_dw = shard(s_w_dw, Dq)[None]
    g_conv_b_dw = shard(s_b_dw, Dq)
    g_conv_ln_g = shard(s_ln_g, Dq)
    g_conv_ln_b = shard(s_ln_b, Dq)
    g_conv_b_pw2 = shard(s_b_pw2, Dq)

    small = dict(ada_w=g_ada_w, ada_b=g_ada_b, norm_g=g_norm_g, conv_b_pw1=g_conv_b_pw1, conv_w_dw=g_conv_w_dw,
                 conv_b_dw=g_conv_b_dw, conv_ln_g=g_conv_ln_g, conv_ln_b=g_conv_ln_b, conv_b_pw2=g_conv_b_pw2,
                 kv_ada_w=g_kv_ada_w, kv_ada_b=g_kv_ada_b, kv_norm_g=s_kv_norm_g, kv_a_norm_g=s_kv_a_g,
                 q_a_norm_g=s_q_a_g, final_norm_g=s_final_g)
    order = ["ada_w", "ada_b", "norm_g", "ffn_w13", "ffn_w2", "conv_w_pw1", "conv_b_pw1", "conv_w_dw", "conv_b_dw",
             "conv_ln_g", "conv_ln_b", "conv_w_pw2", "conv_b_pw2", "kv_ada_w", "kv_ada_b", "kv_norm_g", "w_kv_a",
             "kv_a_norm_g", "w_kv_b", "w_q_a", "q_a_norm_g", "w_q_b", "w_o", "final_norm_g"]
    weights = [ada_w, ada_b, norm_g, ffn_w13, ffn_w2, conv_w_pw1, conv_b_pw1, conv_w_dw, conv_b_dw, conv_ln_g,
               conv_ln_b, conv_w_pw2, conv_b_pw2, kv_ada_w, kv_ada_b, kv_norm_g, w_kv_a, kv_a_norm_g, w_kv_b, w_q_a,
               q_a_norm_g, w_q_b, w_o, final_norm_g]
    ms = [m_ada_w, m_ada_b, m_norm_g, m_ffn_w13, m_ffn_w2, m_conv_w_pw1, m_conv_b_pw1, m_conv_w_dw, m_conv_b_dw,
          m_conv_ln_g, m_conv_ln_b, m_conv_w_pw2, m_conv_b_pw2, m_kv_ada_w, m_kv_ada_b, m_kv_norm_g, m_w_kv_a,
          m_kv_a_norm_g, m_w_kv_b, m_w_q_a, m_q_a_norm_g, m_w_q_b, m_w_o, m_final_norm_g]
    vs = [v_ada_w, v_ada_b, v_norm_g, v_ffn_w13, v_ffn_w2, v_conv_w_pw1, v_conv_b_pw1, v_conv_w_dw, v_conv_b_dw,
          v_conv_ln_g, v_conv_ln_b, v_conv_w_pw2, v_conv_b_pw2, v_kv_ada_w, v_kv_ada_b, v_kv_norm_g, v_w_kv_a,
          v_kv_a_norm_g, v_w_kv_b, v_w_q_a, v_q_a_norm_g, v_w_q_b, v_w_o, v_final_norm_g]
    state = {k: (w, m, v) for k, w, m, v in zip(order, weights, ms, vs)}
    results = {}

    def update(names, grads_by_name):
        for k in names:
            w, m, v = state[k]
            g = grads_by_name[k].reshape(w.shape)
            results[k] = (g, *adamw(w, g, m, v))

    update([k for k in order if k in small], small)
    q_w13_00, q_w2_00 = reduce_finish(last_group["started"], place, dh, "d")
    red = [j.reshape(2 * j.shape[1], j.shape[2]) for j in join_halves(
        [q_w13_00, q_w13_01, q_w13_10, q_w13_11, q_w2_00, q_w2_01, q_w2_10, q_w2_11, q_pw1, q_pw2, q_kv_a, q_kv_b,
         q_q_a, q_q_b, q_w_o])]
    big = dict(ffn_w13=jnp.stack(red[0:4]), ffn_w2=jnp.stack(red[4:8]), conv_w_pw1=red[8], conv_w_pw2=red[9],
               w_kv_a=red[10], w_kv_b=red[11], w_q_a=red[12], w_q_b=red[13], w_o=red[14])
    update([k for k in order if k in big], big)
    outs = [results[k] for k in order]
    return (loss, grad_x, *[o[0] for o in outs], *[o[1] for o in outs], *[o[2] for o in outs], *[o[3] for o in outs])
```

```python
import jax
import jax.numpy as jnp
from jax import lax
from jax.experimental import pallas as pl
from jax.experimental.pallas import tpu as pltpu

F32 = jnp.float32
BF16 = jnp.bfloat16
MESH = pl.DeviceIdType.MESH

N_HEADS = 16
QK_NOPE = 64
QK_ROPE = 32
V_HEAD = 64
KV_LORA = 256
CONV_WIDTH = 31
ROPE_THETA = 10000.0
EPS = 1e-6
N_MOD = 9
HEAD_PAD = 128
ATTN_TILE = 512
CONV_HALO = 32

ADAM_LR = 0.001
ADAM_B1 = 0.9
ADAM_B2 = 0.999
ADAM_EPS = 1e-08
ADAM_WD = 0.01
ADAM_STEP = 10

VMEM_LIMIT_BYTES = 56 * 2 ** 20
ROW_TILE_BUDGET = 10 * 2 ** 20
MM_VMEM_BUDGET = 40 * 2 ** 20
LANES = 128
F32_TILE = 8 * LANES
NEG = float(jnp.finfo(jnp.float32).min)
LOG2_E = 1.4426950408889634


def _tile(n, prefs):
    for t in prefs:
        if n % t == 0:
            return t
    return n


def _params(sem):
    return pltpu.CompilerParams(dimension_semantics=sem, vmem_limit_bytes=VMEM_LIMIT_BYTES)


def _mm_tiles(M, N, K, mode, a_bytes, b_bytes, o_bytes):
    if mode == "tn":
        tk_opts = [t for t in (2048, 1024, 512, 256, 128) if K % t == 0] or [K]
        tm_opts = ([M] if M <= 2816 else []) + [t for t in (1024, 512, 256, 128) if M % t == 0 and t < M]
    else:
        tk_opts = [K]
        tm_opts = [t for t in (1024, 512, 256, 128) if M % t == 0] or [M]
    tn_opts = [t for t in (1408, 1024, 512, 384, 256, 128) if N % t == 0] or [N]

    def need(tm, tn, tk):
        blocks = 2 * (tm * tk * a_bytes + tk * tn * b_bytes + tm * tn * o_bytes)
        return blocks + (tm * tn * 4 if mode == "tn" else 0)

    tk_floor = next((t for t in tk_opts if t <= 512), tk_opts[-1])
    for tm in tm_opts:
        for tn in tn_opts:
            if need(tm, tn, tk_floor) <= MM_VMEM_BUDGET:
                return tm, tn, next(tk for tk in tk_opts if need(tm, tn, tk) <= MM_VMEM_BUDGET)
    return tm_opts[-1], tn_opts[-1], tk_opts[-1]


def mm(a, b, mode, name, out_dtype=F32, bias=None):
    if mode == "nn":
        (M, K), (K2, N) = a.shape, b.shape
        dims = (((1,), (0,)), ((), ()))
    elif mode == "nt":
        (M, K), (N, K2) = a.shape, b.shape
        dims = (((1,), (1,)), ((), ()))
    else:
        (K, M), (K2, N) = a.shape, b.shape
        dims = (((0,), (0,)), ((), ()))
    assert K == K2, (a.shape, b.shape, mode)
    tm, tn, tk = _mm_tiles(M, N, K, mode, a.dtype.itemsize, b.dtype.itemsize, jnp.dtype(out_dtype).itemsize)
    nk = K // tk
    if mode == "tn":
        a_spec = pl.BlockSpec((tk, tm), lambda i, j, k: (k, i))
        b_spec = pl.BlockSpec((tk, tn), lambda i, j, k: (k, j))
    elif mode == "nn":
        a_spec = pl.BlockSpec((tm, tk), lambda i, j, k: (i, k))
        b_spec = pl.BlockSpec((tk, tn), lambda i, j, k: (k, j))
    else:
        a_spec = pl.BlockSpec((tm, tk), lambda i, j, k: (i, k))
        b_spec = pl.BlockSpec((tn, tk), lambda i, j, k: (j, k))
    in_specs = [a_spec, b_spec]
    operands = [a, b]
    if bias is not None:
        in_specs.append(pl.BlockSpec((1, tn), lambda i, j, k: (0, j)))
        operands.append(bias)
    has_bias = bias is not None

    def body(*refs):
        a_ref, b_ref = refs[0], refs[1]
        bias_ref = refs[2] if has_bias else None
        o_ref = refs[3] if has_bias else refs[2]
        prod = lax.dot_general(a_ref[...].astype(BF16), b_ref[...].astype(BF16), dims,
                               preferred_element_type=F32)
        if nk == 1:
            if has_bias:
                prod = prod + bias_ref[...]
            o_ref[...] = prod.astype(o_ref.dtype)
        else:
            acc_ref = refs[-1]
            k = pl.program_id(2)

            @pl.when(k == 0)
            def _():
                acc_ref[...] = jnp.zeros_like(acc_ref)

            acc_ref[...] += prod

            @pl.when(k == nk - 1)
            def _():
                out = acc_ref[...]
                if has_bias:
                    out = out + bias_ref[...]
                o_ref[...] = out.astype(o_ref.dtype)

    return pl.pallas_call(
        body, name=name,
        grid=(M // tm, N // tn, nk),
        in_specs=in_specs,
        out_specs=pl.BlockSpec((tm, tn), lambda i, j, k: (i, j)),
        out_shape=jax.ShapeDtypeStruct((M, N), out_dtype),
        scratch_shapes=[pltpu.VMEM((tm, tn), F32)] if nk > 1 else [],
        compiler_params=_params(("parallel", "parallel", "arbitrary")),
    )(*operands)


def mm_fused(a, b, mode, name, tn, epi, epi_outs, pro=None, pro_rows=(), pro_vecs=(), pro_out=False, n_pro_sums=0,
             epi_rows=(), epi_vecs=(), b_blocks=None, n_cols=None):
    M, K = a.shape
    if b_blocks is not None:
        n_b, N = len(b_blocks), n_cols
    else:
        n_b = b.shape[0] if b.ndim == 3 else 1
        N = b.shape[-1] if mode == "nn" else b.shape[0]
    dims = (((1,), (0,)), ((), ())) if mode == "nn" else (((1,), (1,)), ((), ()))
    nj = N // tn
    epi_outs = [o if len(o) == 3 else (*o, None) for o in epi_outs]
    row_bytes = 2 * (K * a.dtype.itemsize + sum(K * r.dtype.itemsize for r in pro_rows) + (2 * K if pro_out else 0)
                     + sum(w * r.dtype.itemsize * (r.shape[0] if r.ndim == 3 else 1) for r, w in epi_rows)
                     + sum(w * jnp.dtype(dt).itemsize * (L or 1) for w, dt, L in epi_outs)
                     ) + (2 * K if pro is not None else 0)
    fixed = 2 * n_b * K * tn * b.dtype.itemsize
    tm = next((t for t in (1024, 512, 256, 128) if M % t == 0 and t * row_bytes + fixed <= MM_VMEM_BUDGET), M)
    row = lambda i, j: (i, 0)
    tile = lambda i, j: (i, j)
    stack = lambda i, j: (0, i, j)
    in_specs = [pl.BlockSpec((tm, K), row)] + [pl.BlockSpec((tm, K), row) for _ in pro_rows]
    in_specs += [pl.BlockSpec(v.shape, lambda i, j: (0, 0)) for v in pro_vecs]
    if b_blocks is not None:
        in_specs += [pl.BlockSpec(shape, imap) for shape, imap in b_blocks]
    elif b.ndim == 3:
        in_specs += [pl.BlockSpec((None, K, tn), lambda i, j, h=h: (h, 0, j)) for h in range(n_b)]
    elif mode == "nn":
        in_specs += [pl.BlockSpec((K, tn), lambda i, j: (0, j))]
    else:
        in_specs += [pl.BlockSpec((tn, K), lambda i, j: (j, 0))]
    in_specs += [pl.BlockSpec((r.shape[0], tm, w), stack) if r.ndim == 3 else pl.BlockSpec((tm, w), tile)
                 for r, w in epi_rows]
    in_specs += [pl.BlockSpec((1, tn), lambda i, j: (0, j)) for _ in epi_vecs]
    out_specs, out_shape = [], []
    if pro_out:
        out_specs.append(pl.BlockSpec((tm, K), row))
        out_shape.append(jax.ShapeDtypeStruct((M, K), BF16))
    for _ in range(n_pro_sums):
        out_specs.append(pl.BlockSpec((1, K), lambda i, j: (0, 0)))
        out_shape.append(jax.ShapeDtypeStruct((1, K), F32))
    for w, dt, L in epi_outs:
        out_specs.append(pl.BlockSpec((tm, w), tile) if L is None else pl.BlockSpec((L, tm, w), stack))
        out_shape.append(jax.ShapeDtypeStruct((M, nj * w) if L is None else (L, M, nj * w), dt))
    n_pr, n_pv, n_er, n_ev = len(pro_rows), len(pro_vecs), len(epi_rows), len(epi_vecs)
    n_a = 1 + n_pr + n_pv
    n_in = n_a + n_b + n_er + n_ev
    n_po = 1 if pro_out else 0

    def body(*refs):
        i, j = pl.program_id(0), pl.program_id(1)
        a_ref = refs[0]
        outs = refs[n_in:]
        if pro is not None:
            lhs_ref = refs[-1]

            @pl.when(j == 0)
            def _():
                res = pro(*[r[...] for r in refs[:1 + n_pr + n_pv]])
                if not isinstance(res, (tuple, list)):
                    res = (res,)
                lhs_ref[...] = res[0]
                if pro_out:
                    outs[0][...] = res[0]
                for s_ref, val in zip(outs[n_po:n_po + n_pro_sums], res[1:]):
                    part = jnp.sum(val.astype(F32), axis=0, keepdims=True)

                    @pl.when(i == 0)
                    def _(s_ref=s_ref, part=part):
                        s_ref[...] = part

                    @pl.when(i != 0)
                    def _(s_ref=s_ref, part=part):
                        s_ref[...] += part

            lhs = lhs_ref[...]
        else:
            lhs = a_ref[...].astype(BF16)
        accs = [lax.dot_general(lhs, b_ref[...].astype(BF16), dims, preferred_element_type=F32)
                for b_ref in refs[n_a:n_a + n_b]]
        res = epi(*accs, *[r[...] for r in refs[n_a + n_b:n_in]])
        if not isinstance(res, (tuple, list)):
            res = (res,)
        for o_ref, val in zip(outs[n_po + n_pro_sums:], res):
            if isinstance(val, (tuple, list)):
                for h, part in enumerate(val):
                    o_ref[h] = part.astype(o_ref.dtype)
            else:
                o_ref[...] = val.astype(o_ref.dtype)

    return pl.pallas_call(
        body, name=name,
        grid=(M // tm, nj),
        in_specs=in_specs, out_specs=out_specs, out_shape=out_shape,
        scratch_shapes=[pltpu.VMEM((tm, K), BF16)] if pro is not None else [],
        compiler_params=_params(("arbitrary", "arbitrary")),
    )(a, *pro_rows, *pro_vecs, *([b] * n_b), *[r for r, _ in epi_rows], *epi_vecs)


def rowwise(fn, rows, vecs, outs, sums, name, tm=None):
    norm = [(r, r.shape[1], 0) if not isinstance(r, tuple) else r for r in rows]
    S = norm[0][0].shape[0]
    if tm is None:
        tm = _row_tile(S, sum(w * r.dtype.itemsize for r, w, _ in norm)
                       + sum(n * jnp.dtype(dt).itemsize for n, dt in outs))
    n_rows, n_vecs, n_outs, n_sums = len(norm), len(vecs), len(outs), len(sums)
    in_specs = [pl.BlockSpec((tm, w), lambda i, cb=cb: (i, cb)) for _, w, cb in norm]
    in_specs += [pl.BlockSpec(v.shape, lambda i: (0, 0)) for v in vecs]
    out_specs = [pl.BlockSpec((tm, n), lambda i: (i, 0)) for n, _ in outs]
    out_specs += [pl.BlockSpec((1, n), lambda i: (0, 0)) for n in sums]
    out_shape = [jax.ShapeDtypeStruct((S, n), dt) for n, dt in outs]
    out_shape += [jax.ShapeDtypeStruct((1, n), F32) for n in sums]

    def body(*refs):
        ins = [r[...] for r in refs[:n_rows + n_vecs]]
        res = fn(*ins)
        if not isinstance(res, (tuple, list)):
            res = (res,)
        out_refs = refs[n_rows + n_vecs:]
        for o_ref, val in zip(out_refs[:n_outs], res[:n_outs]):
            o_ref[...] = val.astype(o_ref.dtype)
        if n_sums:
            i = pl.program_id(0)
            for s_ref, val in zip(out_refs[n_outs:], res[n_outs:]):
                part = jnp.sum(val.astype(F32), axis=0, keepdims=True)

                @pl.when(i == 0)
                def _(s_ref=s_ref, part=part):
                    s_ref[...] = part

                @pl.when(i != 0)
                def _(s_ref=s_ref, part=part):
                    s_ref[...] += part

    res = pl.pallas_call(
        body, name=name,
        grid=(S // tm,),
        in_specs=in_specs, out_specs=out_specs, out_shape=out_shape,
        compiler_params=_params(("arbitrary",) if n_sums else ("parallel",)),
    )(*[r for r, _, _ in norm], *vecs)
    return res


def _sigmoid(x):
    return jax.nn.sigmoid(x)


def _rms(x):
    r = lax.rsqrt(jnp.mean(x * x, axis=-1, keepdims=True) + EPS)
    return x * r, r


def _rms_bwd(xhat, r, dxhat):
    return r * (dxhat - xhat * jnp.mean(dxhat * xhat, axis=-1, keepdims=True))


def norm_mod(h, g, sh, sc, name):
    def f(h, g, sh, sc):
        xhat, _ = _rms(h)
        return ((xhat * g) * (1 + sc) + sh).astype(BF16)
    return rowwise(f, [h], [g, sh, sc], [(h.shape[1], BF16)], [], name)[0]


def norm_mod_bwd(h, dhn, dh_out, g, sc, name):
    D = h.shape[1]

    def f(h, dhn, dres, g, sc):
        xhat, r = _rms(h)
        dxn = dhn * (1 + sc)
        return _rms_bwd(xhat, r, dxn * g) + dres, dhn, dhn * (xhat * g), dxn * xhat

    return rowwise(f, [h, dhn, dh_out], [g, sc], [(D, F32)], [D, D, D], name)


def residual(h, y, gate, coef, name, bias=None):
    D = h.shape[1]
    if bias is None:
        def f(h, y, gate):
            return h + (coef * gate) * y
        return rowwise(f, [h, y], [gate], [(D, F32)], [], name)[0], y

    def fb(h, y, gate, bias):
        yb = y + bias
        return h + (coef * gate) * yb, yb
    return rowwise(fb, [h, y], [gate, bias], [(D, F32), (D, F32)], [], name)


def residual_bwd(dh_out, y, gate, coef, name, with_bias_sum=False):
    D = y.shape[1]

    def f(dh, y, gate):
        dy = (coef * gate) * dh
        res = (dy.astype(BF16), coef * dh * y)
        return res + ((dy,) if with_bias_sum else ())
    return rowwise(f, [dh_out, y], [gate], [(D, BF16)], [D, D] if with_bias_sum else [D], name)


def ffn_w13_dx(dab, gw13, l, i):
    _, S, F = dab.shape
    D, C = gw13.shape[3:]
    tm = _tile(S, (1024, 512, 256, 128))
    nt = (((1,), (1,)), ((), ()))

    def body(a_ref, b_ref, o_ref, acc_ref):
        k = pl.program_id(1)
        prod = lax.dot_general(a_ref[...], b_ref[...], nt, preferred_element_type=F32)

        @pl.when(k == 0)
        def _():
            acc_ref[...] = prod

        @pl.when((k > 0) & (k < 3))
        def _():
            acc_ref[...] += prod

        @pl.when(k == 3)
        def _():
            o_ref[...] = acc_ref[...] + prod

    return pl.pallas_call(
        body, name="ffn_w13_dx",
        grid=(S // tm, 4),
        in_specs=[pl.BlockSpec((None, tm, C), lambda r, k: (k // 2, r, k % 2)),
                  pl.BlockSpec((None, None, None, D, C), lambda r, k: (k, l, i, 0, 0))],
        out_specs=pl.BlockSpec((tm, D), lambda r, k: (r, 0)),
        out_shape=jax.ShapeDtypeStruct((S, D), F32),
        scratch_shapes=[pltpu.VMEM((tm, D), F32)],
        compiler_params=_params(("parallel", "arbitrary")),
    )(dab, gw13)


def ffn_w13_grad(hn, dab):
    S, D = hn.shape
    F = dab.shape[2]
    C = F // 2
    tk = next(t for t in (2048, 1024, 512, 256, 128) if S % t == 0)
    tn_dims = (((0,), (0,)), ((), ()))
    nk = S // tk

    def body(a_ref, b_ref, o_ref, acc_ref):
        k = pl.program_id(1)

        @pl.when(k == 0)
        def _():
            acc_ref[...] = jnp.zeros_like(acc_ref)

        acc_ref[...] += lax.dot_general(a_ref[...], b_ref[...], tn_dims, preferred_element_type=F32)

        @pl.when(k == nk - 1)
        def _():
            o_ref[...] = acc_ref[...]

    return pl.pallas_call(
        body, name="ffn_w13_dw",
        grid=(4, nk),
        in_specs=[pl.BlockSpec((tk, D), lambda j, k: (k, 0)),
                  pl.BlockSpec((None, tk, C), lambda j, k: (j // 2, k, j % 2))],
        out_specs=pl.BlockSpec((None, D, C), lambda j, k: (j, 0, 0)),
        out_shape=jax.ShapeDtypeStruct((4, D, C), F32),
        scratch_shapes=[pltpu.VMEM((D, C), F32)],
        compiler_params=_params(("parallel", "arbitrary")),
    )(hn, dab)


def ffn_fwd(h, g, sh, sc, gate, gw13, l, i, w2):
    F, D = w2.shape
    C = F // 2

    def norm(h, g, sh, sc):
        xhat, _ = _rms(h)
        return ((xhat * g) * (1 + sc) + sh).astype(BF16)

    def act(a, b):
        sig = _sigmoid(a)
        sa = a * sig
        return (b * (sig + sa * (1 - sig)), sa), sa * b
    blocks = [((None, None, None, D, C), lambda r, j, half=half: (2 * half + j, l, i, 0, 0)) for half in range(2)]
    hn, dt_dab, t = mm_fused(h, gw13, "nn", "ffn_w13", C, act, [(C, BF16, 2), (C, BF16)],
                             pro=norm, pro_vecs=[g, sh, sc], pro_out=True, b_blocks=blocks, n_cols=F)

    def res(acc, h, gate):
        return h + (0.5 * gate) * acc, acc
    h_out, y = mm_fused(t, w2, "nn", "ffn_w2", D, res, [(D, F32), (D, F32)], epi_rows=[(h, D)], epi_vecs=[gate])
    return h_out, (h, hn, dt_dab, t, y)


def ffn_bwd(dh_out, saved, g, sc, gate, gw13, l, i, w2, after_weight_grads=None):
    h, hn, dt_dab, t, y = saved
    F, D = w2.shape
    C = F // 2

    def scale(dh, y, gate):
        return ((0.5 * gate) * dh).astype(BF16), 0.5 * dh * y

    def act_bwd(dt, f):
        return ((dt * f[0].astype(F32), dt * f[1].astype(F32)),)
    dy, d_gate, dab = mm_fused(dh_out, w2, "nt", "ffn_w2_dx", C, act_bwd, [(C, BF16, 2)],
                               pro=scale, pro_rows=[y], pro_vecs=[gate], pro_out=True, n_pro_sums=1,
                               epi_rows=[(dt_dab, C)])
    dw2 = mm(t, dy, "tn", "ffn_w2_dw")
    dw13 = ffn_w13_grad(hn, dab)
    if after_weight_grads is not None:
        dab = after_weight_grads(dw13, dw2, dab)
    dhn = ffn_w13_dx(dab, gw13, l, i)
    dh_in, d_sh, d_sc, d_g = norm_mod_bwd(h, dhn, dh_out, g, sc, "norm_mod_bwd")
    return dh_in, (d_sh, d_sc, d_gate, d_g), dw13, dw2


def _shifted(xbuf, n):
    return [xbuf] + [pltpu.roll(xbuf, n - b, 0) for b in range(1, 8)]


def conv_fwd(u, w_dw, b_dw, ln_g, ln_b):
    S, D = u.shape
    tm = _tile(S, (256, 128))
    rc = 32
    first_tap = CONV_HALO - (CONV_WIDTH - 1)
    w = jnp.concatenate([w_dw, jnp.zeros((CONV_HALO - CONV_WIDTH, D), F32)], axis=0)

    def body(cur_ref, prev_ref, w_ref, b_ref, g_ref, beta_ref, z_ref, s_ref):
        i = pl.program_id(0)
        prev = jnp.where(i == 0, jnp.zeros((CONV_HALO, D), F32), prev_ref[...])
        xs = _shifted(jnp.concatenate([prev, cur_ref[...]], axis=0), tm + CONV_HALO)
        for c0 in range(0, tm, rc):
            acc = jnp.zeros((rc, D), F32)
            for k in range(CONV_WIDTH):
                off = first_tap + k
                a8, b = off // 8 * 8, off % 8
                acc = acc + w_ref[k:k + 1, :] * xs[b][c0 + a8:c0 + a8 + rc, :]
            z_ref[c0:c0 + rc, :] = acc + b_ref[...]
        z = z_ref[...]
        mu = jnp.mean(z, axis=-1, keepdims=True)
        zc = z - mu
        r = lax.rsqrt(jnp.mean(zc * zc, axis=-1, keepdims=True) + EPS)
        un = zc * r * g_ref[...] + beta_ref[...]
        s_ref[...] = (un * _sigmoid(un)).astype(BF16)

    nb = tm // CONV_HALO
    vec = pl.BlockSpec((1, D), lambda i: (0, 0))
    return pl.pallas_call(
        body, name="conv_fwd",
        grid=(S // tm,),
        in_specs=[pl.BlockSpec((tm, D), lambda i: (i, 0)),
                  pl.BlockSpec((CONV_HALO, D), lambda i: (jnp.maximum(i * nb - 1, 0), 0)),
                  pl.BlockSpec((CONV_HALO, D), lambda i: (0, 0)), vec, vec, vec],
        out_specs=[pl.BlockSpec((tm, D), lambda i: (i, 0)), pl.BlockSpec((tm, D), lambda i: (i, 0))],
        out_shape=[jax.ShapeDtypeStruct((S, D), F32), jax.ShapeDtypeStruct((S, D), BF16)],
        compiler_params=_params(("parallel",)),
    )(u, u, w, b_dw, ln_g, ln_b)


def conv_bwd(dz, u, w_dw):
    S, D_all = u.shape
    D = D_all // 2 if D_all % (2 * LANES) == 0 else D_all
    tm = _tile(S, (256, 128))
    rc = 32
    first_tap = CONV_HALO - (CONV_WIDTH - 1)
    w = jnp.concatenate([w_dw, jnp.zeros((CONV_HALO - CONV_WIDTH, D_all), F32)], axis=0)
    n_tiles = S // tm
    nb = tm // CONV_HALO

    def body(dz_ref, dzn_ref, u_ref, up_ref, w_ref, du_ref, dw_ref):
        i = pl.program_id(1)
        nxt = jnp.where(i == n_tiles - 1, jnp.zeros((CONV_HALO, D), F32), dzn_ref[...])
        dzs = _shifted(jnp.concatenate([dz_ref[...], nxt], axis=0), tm + CONV_HALO)
        for c0 in range(0, tm, rc):
            acc = jnp.zeros((rc, D), F32)
            for m in range(CONV_WIDTH):
                a8, b = m // 8 * 8, m % 8
                acc = acc + w_ref[CONV_WIDTH - 1 - m:CONV_WIDTH - m, :] * dzs[b][c0 + a8:c0 + a8 + rc, :]
            du_ref[c0:c0 + rc, :] = acc
        prev = jnp.where(i == 0, jnp.zeros((CONV_HALO, D), F32), up_ref[...])
        us = _shifted(jnp.concatenate([prev, u_ref[...]], axis=0), tm + CONV_HALO)
        dz = dz_ref[...]

        @pl.when(i == 0)
        def _():
            dw_ref[...] = jnp.zeros_like(dw_ref)

        for k in range(CONV_WIDTH):
            off = first_tap + k
            a8, b = off // 8 * 8, off % 8
            dw_ref[k:k + 1, :] += jnp.sum(dz * us[b][a8:a8 + tm, :], axis=0, keepdims=True)

    last_blk = S // CONV_HALO - 1
    du, dw = pl.pallas_call(
        body, name="conv_bwd",
        grid=(D_all // D, n_tiles),
        in_specs=[pl.BlockSpec((tm, D), lambda j, i: (i, j)),
                  pl.BlockSpec((CONV_HALO, D), lambda j, i: (jnp.minimum((i + 1) * nb, last_blk), j)),
                  pl.BlockSpec((tm, D), lambda j, i: (i, j)),
                  pl.BlockSpec((CONV_HALO, D), lambda j, i: (jnp.maximum(i * nb - 1, 0), j)),
                  pl.BlockSpec((CONV_HALO, D), lambda j, i: (0, j))],
        out_specs=[pl.BlockSpec((tm, D), lambda j, i: (i, j)), pl.BlockSpec((CONV_HALO, D), lambda j, i: (0, j))],
        out_shape=[jax.ShapeDtypeStruct((S, D_all), F32), jax.ShapeDtypeStruct((CONV_HALO, D_all), F32)],
        compiler_params=_params(("parallel", "arbitrary")),
    )(dz, dz, u, u, w)
    return du, dw[:CONV_WIDTH]


def conv_module_fwd(h, g, sh, sc, gate, p):
    D = h.shape[1]
    hn = norm_mod(h, g, sh, sc, "conv_norm_mod")
    pre = mm(hn, p["w_pw1"], "nn", "conv_pw1")
    ba, bg = p["b_pw1"][:, :D], p["b_pw1"][:, D:]

    def glu(a, gt, ba, bg):
        return (a + ba) * _sigmoid(gt + bg)
    u = rowwise(glu, [(pre, D, 0), (pre, D, 1)], [ba, bg], [(D, F32)], [], "conv_glu")[0]
    z, s = conv_fwd(u, p["w_dw"], p["b_dw"], p["ln_g"], p["ln_b"])
    yraw = mm(s, p["w_pw2"], "nn", "conv_pw2")
    h_out, y = residual(h, yraw, gate, 1.0, "conv_residual", bias=p["b_pw2"])
    return h_out, (h, hn, pre, u, z, s, y)


def conv_module_bwd(dh_out, saved, g, sc, gate, p):
    h, hn, pre, u, z, s, y = saved
    D = h.shape[1]
    dy, d_gate, d_b_pw2 = residual_bwd(dh_out, y, gate, 1.0, "conv_residual_bwd", with_bias_sum=True)
    d_w_pw2 = mm(s, dy, "tn", "conv_pw2_dw")
    ds = mm(dy, p["w_pw2"], "nt", "conv_pw2_dx")

    def ln_bwd(z, ds, g, beta):
        mu = jnp.mean(z, axis=-1, keepdims=True)
        zc = z - mu
        r = lax.rsqrt(jnp.mean(zc * zc, axis=-1, keepdims=True) + EPS)
        xhat = zc * r
        un = xhat * g + beta
        sig = _sigmoid(un)
        d_un = ds * (sig * (1 + un * (1 - sig)))
        dxhat = d_un * g
        dz = r * (dxhat - jnp.mean(dxhat, axis=-1, keepdims=True)
                  - xhat * jnp.mean(dxhat * xhat, axis=-1, keepdims=True))
        return dz, d_un * xhat, d_un, dz
    dz, d_ln_g, d_ln_b, d_b_dw = rowwise(ln_bwd, [z, ds], [p["ln_g"], p["ln_b"]], [(D, F32)], [D, D, D],
                                         "conv_ln_bwd")
    du, d_w_dw = conv_bwd(dz, u, p["w_dw"])
    ba, bg = p["b_pw1"][:, :D], p["b_pw1"][:, D:]

    def glu_bwd(a, gt, du, ba, bg):
        sg = _sigmoid(gt + bg)
        da = du * sg
        dg = du * (a + ba) * (sg * (1 - sg))
        dpre = jnp.concatenate([da, dg], axis=1)
        return dpre.astype(BF16), dpre
    dpre, d_b_pw1 = rowwise(glu_bwd, [(pre, D, 0), (pre, D, 1), du], [ba, bg], [(2 * D, BF16)], [2 * D],
                            "conv_glu_bwd")
    d_w_pw1 = mm(hn, dpre, "tn", "conv_pw1_dw")
    dhn = mm(dpre, p["w_pw1"], "nt", "conv_pw1_dx")
    dh_in, d_sh, d_sc, d_g = norm_mod_bwd(h, dhn, dh_out, g, sc, "norm_mod_bwd")
    grads = dict(w_pw1=d_w_pw1, b_pw1=d_b_pw1, w_dw=d_w_dw, b_dw=d_b_dw, ln_g=d_ln_g, ln_b=d_ln_b,
                 w_pw2=d_w_pw2, b_pw2=d_b_pw2)
    return dh_in, (d_sh, d_sc, d_gate, d_g), grads


def _rope(x, c, s1, s2):
    n = x.shape[1]
    return x * c + pltpu.roll(x, n - QK_ROPE // 2, 1) * s1 + pltpu.roll(x, QK_ROPE // 2, 1) * s2


def _rope_t(dy, c, s1, s2):
    n = dy.shape[1]
    return dy * c + pltpu.roll(dy * s1, QK_ROPE // 2, 1) + pltpu.roll(dy * s2, n - QK_ROPE // 2, 1)


def rope_tables(positions):
    inv_freq = ROPE_THETA ** (-jnp.arange(0, QK_ROPE, 2, dtype=F32) / QK_ROPE)
    ang = positions.astype(F32)[:, None] * inv_freq
    cos, sin = jnp.cos(ang), jnp.sin(ang)
    S = positions.shape[0]
    one = jnp.ones((S, QK_NOPE), F32)
    z16 = jnp.zeros((S, QK_ROPE // 2), F32)
    zn = jnp.zeros((S, QK_NOPE), F32)
    zt = jnp.zeros((S, HEAD_PAD - QK_NOPE - QK_ROPE), F32)
    c = jnp.concatenate([one, cos, cos, zt], axis=1)
    s1 = jnp.concatenate([zn, -sin, z16, zt], axis=1)
    s2 = jnp.concatenate([zn, z16, sin, zt], axis=1)
    return c, s1, s2


def attn_fwd(qr, kv, kpe, n_heads):
    S = qr.shape[0]
    H = n_heads
    tk = _tile(S, (ATTN_TILE,))
    nk = S // tk
    w = 2 if nk % 2 == 0 else 1
    tq = w * tk
    c2 = (QK_NOPE + QK_ROPE) ** -0.5 * LOG2_E
    nt = (((1,), (1,)), ((), ()))

    assert V_HEAD < HEAD_PAD
    ones_row = HEAD_PAD - 1

    def body(q_ref, k_ref, v_ref, kpe_ref, o_ref, lse_ref, kf_ref, vt_ref, m_ref, acc_ref):
        qi = pl.program_id(1)
        feature = lax.broadcasted_iota(jnp.int32, (HEAD_PAD, tk), 0)

        @pl.when(qi == 0)
        def _():
            kf_ref[...] = k_ref[...] + kpe_ref[...]
            for c in range(nk):
                vt = jnp.transpose(v_ref[c * tk:(c + 1) * tk, :].astype(F32))
                vt_ref[c] = jnp.where(feature == ones_row, 1.0, vt).astype(BF16)

        q = q_ref[...]
        m_ref[...] = jnp.full((1, tq), -jnp.inf, F32)
        acc_ref[...] = jnp.zeros((HEAD_PAD, tq), F32)

        def tile(j, first_visible):
            k = kf_ref[pl.ds(pl.multiple_of(j * tk, tk), tk), :]
            t = lax.dot_general(k, q, nt, preferred_element_type=F32) * c2
            if first_visible is not None:
                krow = lax.broadcasted_iota(jnp.int32, (tk, tq), 0)
                qcol = lax.broadcasted_iota(jnp.int32, (tk, tq), 1)
                t = jnp.where(krow + first_visible <= qcol, t, NEG)
            m_old = m_ref[...]
            m_new = jnp.maximum(m_old, jnp.max(t, axis=0, keepdims=True))
            alpha = jnp.exp2(m_old - m_new)
            p = jnp.exp2(t - m_new)
            acc_ref[...] = alpha * acc_ref[...] + jnp.dot(vt_ref[j], p.astype(BF16), preferred_element_type=F32)
            m_ref[...] = m_new

        def unmasked(j, carry):
            tile(j, None)
            return carry

        lax.fori_loop(0, w * qi, unmasked, 0)
        for u in range(w):
            tile(w * qi + u, u * tk)
        acc = acc_ref[...]
        l = acc_ref[ones_row:ones_row + 1, :]
        out_feature = lax.broadcasted_iota(jnp.int32, (HEAD_PAD, tq), 0)
        o_ref[...] = jnp.transpose(jnp.where(out_feature == ones_row, 0.0, acc / l))
        lse = m_ref[...] + jnp.log(l) * LOG2_E
        for u in range(w):
            lse_ref[u] = lse[:, u * tk:(u + 1) * tk]

    return pl.pallas_call(
        body, name="attn_fwd",
        grid=(H, S // tq),
        in_specs=[pl.BlockSpec((tq, HEAD_PAD), lambda h, i: (i, h)),
                  pl.BlockSpec((S, HEAD_PAD), lambda h, i: (0, h)),
                  pl.BlockSpec((S, HEAD_PAD), lambda h, i: (0, H + h)),
                  pl.BlockSpec((S, HEAD_PAD), lambda h, i: (0, 0))],
        out_specs=[pl.BlockSpec((tq, HEAD_PAD), lambda h, i: (i, h)),
                   pl.BlockSpec((None, w, 1, tk), lambda h, i: (h, i, 0, 0))],
        out_shape=[jax.ShapeDtypeStruct((S, H * HEAD_PAD), F32), jax.ShapeDtypeStruct((H, nk, 1, tk), F32)],
        scratch_shapes=[pltpu.VMEM((S, HEAD_PAD), BF16), pltpu.VMEM((nk, HEAD_PAD, tk), BF16),
                        pltpu.VMEM((1, tq), F32), pltpu.VMEM((HEAD_PAD, tq), F32)],
        compiler_params=_params(("parallel", "arbitrary")),
    )(qr, kv, kv, kpe)


def attn_delta(o, do, n_heads):
    S = o.shape[0]
    H = n_heads
    tq = _tile(S, (ATTN_TILE,))
    nq = S // tq

    def body(o_ref, do_ref, d_ref):
        for c in range(nq):
            rows = slice(c * tq, (c + 1) * tq)
            prod = o_ref[rows, :] * do_ref[rows, :].astype(F32)
            d_ref[c] = jnp.sum(jnp.transpose(prod), axis=0, keepdims=True)

    return pl.pallas_call(
        body, name="attn_delta",
        grid=(H,),
        in_specs=[pl.BlockSpec((S, HEAD_PAD), lambda h: (0, h)), pl.BlockSpec((S, HEAD_PAD), lambda h: (0, h))],
        out_specs=pl.BlockSpec((None, nq, 1, tq), lambda h: (h, 0, 0, 0)),
        out_shape=jax.ShapeDtypeStruct((H, nq, 1, tq), F32),
        compiler_params=_params(("parallel",)),
    )(o, do)


def attn_bwd(qr, kv, kpe, do, lse2, delta, n_heads):
    S = qr.shape[0]
    H = n_heads
    tk = _tile(S, (ATTN_TILE,))
    nk = S // tk
    w = 2 if nk % 2 == 0 else 1
    tq = w * tk
    nq = S // tq
    scale = (QK_NOPE + QK_ROPE) ** -0.5
    c2 = scale * LOG2_E
    nt = (((1,), (1,)), ((), ()))
    lse2 = lse2.reshape(H, nq, 1, tq)
    delta4 = delta.reshape(H, nq, 1, tq)

    def body(k_ref, v_ref, kpe_ref, q_ref, do_ref, lse_ref, dl_ref, dq_ref, dk_ref, dv_ref, dka_ref, dva_ref,
             dqt_ref):
        kj = pl.program_id(1)
        k = k_ref[...] + kpe_ref[...]
        kt = jnp.transpose(k.astype(F32)).astype(BF16)
        v = v_ref[...]

        @pl.when(kj == 0)
        def _():
            dqt_ref[...] = jnp.zeros_like(dqt_ref)

        dka_ref[...] = jnp.zeros_like(dka_ref)
        dva_ref[...] = jnp.zeros_like(dva_ref)

        def tile(i, masked):
            start = pl.multiple_of(i * tq, tq)
            q = q_ref[pl.ds(start, tq), :]
            do = do_ref[pl.ds(start, tq), :]
            t = lax.dot_general(k, q, nt, preferred_element_type=F32) * c2
            if masked:
                krow = lax.broadcasted_iota(jnp.int32, (tk, tq), 0)
                qcol = lax.broadcasted_iota(jnp.int32, (tk, tq), 1)
                t = jnp.where(krow + (kj % w) * tk <= qcol, t, NEG)
            pt = jnp.exp2(t - lse_ref[i])
            dva_ref[...] += jnp.dot(pt.astype(BF16), do, preferred_element_type=F32)
            dpt = lax.dot_general(v, do, nt, preferred_element_type=F32)
            dst = (pt * (dpt - dl_ref[i]) * scale).astype(BF16)
            dka_ref[...] += jnp.dot(dst, q, preferred_element_type=F32)
            dqt_ref[i] += jnp.dot(kt, dst, preferred_element_type=F32)

        tile(kj // w, True)

        def unmasked(i, carry):
            tile(i, False)
            return carry

        lax.fori_loop(kj // w + 1, nq, unmasked, 0)
        dk_ref[...] = dka_ref[...]
        dv_ref[...] = dva_ref[...]

        @pl.when(kj == nk - 1)
        def _():
            for c in range(nq):
                dq_ref[c * tq:(c + 1) * tq, :] = jnp.transpose(dqt_ref[c])

    blk = pl.BlockSpec((tk, HEAD_PAD), lambda h, j: (j, h))
    whole = pl.BlockSpec((S, HEAD_PAD), lambda h, j: (0, h))
    stat = pl.BlockSpec((None, nq, 1, tq), lambda h, j: (h, 0, 0, 0))
    shp = jax.ShapeDtypeStruct((S, H * HEAD_PAD), F32)
    return pl.pallas_call(
        body, name="attn_bwd",
        grid=(H, nk),
        in_specs=[blk, pl.BlockSpec((tk, HEAD_PAD), lambda h, j: (j, H + h)),
                  pl.BlockSpec((tk, HEAD_PAD), lambda h, j: (j, 0)), whole, whole, stat, stat],
        out_specs=[whole, blk, blk],
        out_shape=[shp, shp, shp],
        scratch_shapes=[pltpu.VMEM((tk, HEAD_PAD), F32), pltpu.VMEM((tk, HEAD_PAD), F32),
                        pltpu.VMEM((nq, HEAD_PAD, tq), F32)],
        compiler_params=_params(("parallel", "arbitrary")),
    )(kv, kv, kpe, qr, do, lse2, delta4)


def _pad_heads(w, width):
    R = w.shape[0]
    w3 = w.reshape(R, -1, width)
    return jnp.pad(w3, ((0, 0), (0, 0), (0, HEAD_PAD - width))).reshape(R, -1)


def _unpad_heads(w, width):
    R = w.shape[0]
    return w.reshape(R, -1, HEAD_PAD)[:, :, :width].reshape(R, -1)


def mla_pad_weights(p):
    H = N_HEADS
    w_q_b = _pad_heads(p["w_q_b"], QK_NOPE + QK_ROPE)
    kvb = p["w_kv_b"].reshape(KV_LORA, H, QK_NOPE + V_HEAD)
    wk = _pad_heads(kvb[:, :, :QK_NOPE].reshape(KV_LORA, -1), QK_NOPE)
    wv = _pad_heads(kvb[:, :, QK_NOPE:].reshape(KV_LORA, -1), V_HEAD)
    D = p["w_kv_a"].shape[0]
    a = p["w_kv_a"]
    w_kv_a = jnp.concatenate([a[:, :KV_LORA], jnp.zeros((D, QK_NOPE), a.dtype), a[:, KV_LORA:],
                              jnp.zeros((D, HEAD_PAD - QK_NOPE - QK_ROPE), a.dtype)], axis=1)
    wo = p["w_o"].reshape(H, V_HEAD, -1)
    w_o = jnp.pad(wo, ((0, 0), (0, HEAD_PAD - V_HEAD), (0, 0))).reshape(H * HEAD_PAD, -1)
    return dict(w_q_a=p["w_q_a"], w_q_b=w_q_b, w_kv_b=jnp.concatenate([wk, wv], axis=1), w_kv_a=w_kv_a, w_o=w_o)


def mla_kv_fwd(h, g, sh, sc, kv_a_norm_g, pw, tabs):
    hkv = norm_mod(h, g, sh, sc, "kv_norm_mod")
    ckvp = mm(hkv, pw["w_kv_a"], "nn", "kv_a")

    def f(ckv, kpe, c, s1, s2, g):
        xhat, _ = _rms(ckv)
        return (xhat * g).astype(BF16), _rope(kpe, c, s1, s2).astype(BF16)
    ckv_n, kpe_r = rowwise(f, [(ckvp, KV_LORA, 0), (ckvp, HEAD_PAD, KV_LORA // HEAD_PAD), *tabs], [kv_a_norm_g],
                           [(KV_LORA, BF16), (HEAD_PAD, BF16)], [], "kv_a_norm_rope")
    kv = mm(ckv_n, pw["w_kv_b"], "nn", "kv_b", out_dtype=BF16)
    return kv, kpe_r, (h, hkv, ckvp, ckv_n)


def mla_kv_bwd(dh_stream, dk, dv, saved, g, sc, kv_a_norm_g, pw, tabs):
    h, hkv, ckvp, ckv_n = saved
    H = N_HEADS
    lane = jnp.arange(HEAD_PAD)
    pe_mask = ((lane >= QK_NOPE) & (lane < QK_NOPE + QK_ROPE)).astype(F32)[None, :]

    def f(dk, dv, c, s1, s2, mask):
        tot = dk[:, :HEAD_PAD]
        for hh in range(1, H):
            tot = tot + dk[:, hh * HEAD_PAD:(hh + 1) * HEAD_PAD]
        dkpe = _rope_t(tot * mask, c, s1, s2) * mask
        return jnp.concatenate([dk, dv], axis=1).astype(BF16), dkpe
    dkv, dkpe = rowwise(f, [dk, dv, *tabs], [pe_mask], [(2 * H * HEAD_PAD, BF16), (HEAD_PAD, F32)], [],
                        "kv_split_bwd")
    d_w_kv_b = mm(ckv_n, dkv, "tn", "kv_b_dw")
    dckv_n = mm(dkv, pw["w_kv_b"], "nt", "kv_b_dx")

    def f2(ckv, dn, dkpe, g):
        xhat, r = _rms(ckv)
        dx = _rms_bwd(xhat, r, dn * g)
        return jnp.concatenate([dx, dkpe], axis=1).astype(BF16), dn * xhat
    dckvp, d_kv_a_g = rowwise(f2, [(ckvp, KV_LORA, 0), dckv_n, dkpe], [kv_a_norm_g],
                              [(KV_LORA + HEAD_PAD, BF16)], [KV_LORA], "kv_a_norm_bwd")
    d_w_kv_a = mm(hkv, dckvp, "tn", "kv_a_dw")
    dhkv = mm(dckvp, pw["w_kv_a"], "nt", "kv_a_dx")
    dh, d_sh, d_sc, d_g = norm_mod_bwd(h, dhkv, dh_stream, g, sc, "norm_mod_bwd")
    return dh, (d_sh, d_sc, d_g), d_kv_a_g, d_w_kv_a, d_w_kv_b


def mla_fwd(h, g, sh, sc, gate, q_a_norm_g, pw, kv, kpe_r, tabs):
    H = N_HEADS
    hn = norm_mod(h, g, sh, sc, "mla_norm_mod")
    qa = mm(hn, pw["w_q_a"], "nn", "q_a")

    def f(qa, g):
        xhat, _ = _rms(qa)
        return (xhat * g).astype(BF16)
    qa_n = rowwise(f, [qa], [q_a_norm_g], [(qa.shape[1], BF16)], [], "q_a_norm")[0]
    qp = mm(qa_n, pw["w_q_b"], "nn", "q_b")

    def frope(q, c, s1, s2):
        return jnp.concatenate([_rope(q[:, hh * HEAD_PAD:(hh + 1) * HEAD_PAD], c, s1, s2) for hh in range(H)],
                               axis=1).astype(BF16)
    qr = rowwise(frope, [qp, *tabs], [], [(H * HEAD_PAD, BF16)], [], "q_rope")[0]
    o, lse = attn_fwd(qr, kv, kpe_r, H)
    y = mm(o, pw["w_o"], "nn", "w_o")
    h_out, _ = residual(h, y, gate, 1.0, "mla_residual")
    return h_out, (h, hn, qa, qa_n, qr, o, lse, y)


def mla_bwd(dh_out, saved, g, sc, gate, q_a_norm_g, pw, kv, kpe_r, tabs):
    h, hn, qa, qa_n, qr, o, lse, y = saved
    H = N_HEADS
    dy, d_gate = residual_bwd(dh_out, y, gate, 1.0, "mla_residual_bwd")
    d_w_o = mm(o, dy, "tn", "w_o_dw")
    do = mm(dy, pw["w_o"], "nt", "w_o_dx", out_dtype=BF16)
    delta = attn_delta(o, do, H)
    dqr, dk, dv = attn_bwd(qr, kv, kpe_r, do, lse, delta, H)

    def frope_t(dq, c, s1, s2):
        return jnp.concatenate([_rope_t(dq[:, hh * HEAD_PAD:(hh + 1) * HEAD_PAD], c, s1, s2) for hh in range(H)],
                               axis=1).astype(BF16)
    dqp = rowwise(frope_t, [dqr, *tabs], [], [(H * HEAD_PAD, BF16)], [], "q_rope_bwd")[0]
    d_w_q_b = mm(qa_n, dqp, "tn", "q_b_dw")
    dqa_n = mm(dqp, pw["w_q_b"], "nt", "q_b_dx")

    def f(qa, dn, g):
        xhat, r = _rms(qa)
        return _rms_bwd(xhat, r, dn * g).astype(BF16), dn * xhat
    dqa, d_q_a_g = rowwise(f, [qa, dqa_n], [q_a_norm_g], [(qa.shape[1], BF16)], [qa.shape[1]], "q_a_norm_bwd")
    d_w_q_a = mm(hn, dqa, "tn", "q_a_dw")
    dhn = mm(dqa, pw["w_q_a"], "nt", "q_a_dx")
    dh_in, d_sh, d_sc, d_g = norm_mod_bwd(h, dhn, dh_out, g, sc, "norm_mod_bwd")
    grads = dict(w_q_a=d_w_q_a, q_a_norm_g=d_q_a_g, w_q_b=d_w_q_b, w_o=d_w_o)
    return dh_in, (d_sh, d_sc, d_gate, d_g), grads, dk, dv


def loss_head(h, target, g):
    D = h.shape[1]

    def f(h, t, g):
        xhat, r = _rms(h)
        err = xhat * g - t
        dy = err * (1.0 / D)
        dh = _rms_bwd(xhat, r, dy * g)
        return dh, (0.5 / D) * err * err, dy * xhat
    return rowwise(f, [h, target], [g], [(D, F32)], [D, D], "loss_head")


def _place():
    x, y, c = lax.axis_index("x"), lax.axis_index("y"), lax.axis_index("c")
    chips = [(1 - x, y), (x, 1 - y), (1 - x, 1 - y)]
    return x, y, c, chips


HBM_SPEC = pl.BlockSpec(memory_space=pltpu.HBM)


def all_gather8(v):
    m, n = v.shape

    def body(x_ref, out_ref, send_sems, recv_sems, local_sem):
        x, y, c, chips = _place()
        me, sibling = (x, y, c), (x, y, 1 - c)

        def rows(px, py, pc):
            return out_ref.at[4 * px + 2 * py + pc]

        def copy(k, block, to, src=None):
            return pltpu.make_async_remote_copy(
                src_ref=rows(*block) if src is None else src, dst_ref=rows(*block),
                send_sem=send_sems.at[k], recv_sem=recv_sems.at[k], device_id=to, device_id_type=MESH)

        mine = pltpu.make_async_copy(x_ref, rows(*me), local_sem)
        mine.start()
        first = [copy(0, me, sibling, src=x_ref)]
        first += [copy(1 + j, me, (*chip, c), src=x_ref) for j, chip in enumerate(chips)]
        for cp in first:
            cp.start()
        passed = [copy(4 + j, (*chip, c), sibling) for j, chip in enumerate(chips)]
        for j, chip in enumerate(chips):
            copy(1 + j, (*chip, c), me).wait_recv()
            passed[j].start()
        copy(0, sibling, me).wait_recv()
        for j, chip in enumerate(chips):
            copy(4 + j, (*chip, 1 - c), me).wait_recv()
        for cp in first + passed:
            cp.wait_send()
        mine.wait()

    return pl.pallas_call(
        body, name="all_gather8",
        out_shape=jax.ShapeDtypeStruct((8, m, n), v.dtype),
        in_specs=[pl.BlockSpec(memory_space=pltpu.VMEM)],
        out_specs=pl.BlockSpec(memory_space=pltpu.VMEM),
        scratch_shapes=[pltpu.SemaphoreType.DMA((7,)), pltpu.SemaphoreType.DMA((7,)), pltpu.SemaphoreType.DMA],
        compiler_params=pltpu.CompilerParams(vmem_limit_bytes=VMEM_LIMIT_BYTES),
    )(v)


def gather_weights(bufs):
    n = len(bufs)

    def body(*refs):
        ins, outs = refs[:n], refs[n:2 * n]
        send_sems, recv_sems = refs[2 * n:]
        x, y, c, chips = _place()
        across_x, across_y, across_both = chips
        sibling = (x, y, 1 - c)
        me = 2 * x + y
        via_in = (x + (1 - c) * (1 - 2 * x), y + c * (1 - 2 * y))
        via_out = (x + c * (1 - 2 * x), y + (1 - c) * (1 - 2 * y))

        def idx(chip):
            return 2 * chip[0] + chip[1]

        def copy(w, k, src, dst, to):
            return pltpu.make_async_remote_copy(src_ref=src, dst_ref=dst, send_sem=send_sems.at[6 * w + k],
                                                recv_sem=recv_sems.at[6 * w + k], device_id=to, device_id_type=MESH)

        def landed(w, k, chip):
            blk = outs[w].at[idx(chip), c]
            copy(w, k, blk, blk, (*chip, c)).wait_recv()
            return blk

        sends = [copy(w, j, ins[w].at[me, c], outs[w].at[me, c], (*chip, c))
                 for w in range(n) for j, chip in enumerate((across_x, across_y))]
        for cp in sends:
            cp.start()
        for w in range(n):
            blk = landed(w, c, via_in)
            sends += [copy(w, 2, blk, blk, (*via_out, c)), copy(w, 3 + c, blk, blk, sibling)]
            sends[-2].start()
            sends[-1].start()
        for w in range(n):
            blk = landed(w, 1 - c, via_out)
            sends.append(copy(w, 4 - c, blk, blk, sibling))
            sends[-1].start()
        for w in range(n):
            blk = landed(w, 2, across_both)
            sends.append(copy(w, 5, blk, blk, sibling))
            sends[-1].start()
        for w in range(n):
            for j, chip in enumerate(chips):
                other = outs[w].at[idx(chip), 1 - c]
                copy(w, 3 + j, other, other, sibling).wait_recv()
        for cp in sends:
            cp.wait_send()

    return pl.pallas_call(
        body, name="gather_weights",
        out_shape=[jax.ShapeDtypeStruct(b.shape, b.dtype) for b in bufs],
        in_specs=[HBM_SPEC] * n, out_specs=[HBM_SPEC] * n,
        input_output_aliases={w: w for w in range(n)},
        scratch_shapes=[pltpu.SemaphoreType.DMA((6 * n,)), pltpu.SemaphoreType.DMA((6 * n,))],
    )(*bufs)


SEM_SPEC = pl.BlockSpec(memory_space=pltpu.SEMAPHORE)
SPLIT_COPY = pltpu.CompilerParams(has_side_effects=pltpu.SideEffectType.DATAFLOW_SIDE_EFFECTING)
PEERS_PER_BLOCK = 6


def gather_start(groups, carried):
    flat = [a for grp in groups for a in grp]
    group_of = [g for g, grp in enumerate(groups) for _ in grp]
    n, n_g, n_all = len(flat), len(groups), len(flat) + len(carried)

    def body(*refs):
        ins, sems = refs[:n], refs[n_all:n_all + 2 * n_g]
        x, y, c, chips = _place()
        me = 2 * x + y
        for w in range(n):
            mine = ins[w].at[me, c]
            for chip in chips:
                for core in range(2):
                    pltpu.make_async_remote_copy(src_ref=mine, dst_ref=mine, send_sem=sems[2 * group_of[w]],
                                                 recv_sem=sems[2 * group_of[w] + 1], device_id=(*chip, core),
                                                 device_id_type=MESH).start()

    operands = flat + list(carried)
    res = pl.pallas_call(
        body, name="gather_start",
        out_shape=[pltpu.SemaphoreType.DMA(())] * (2 * n_g) + [pltpu.HBM(a.shape, a.dtype) for a in operands],
        in_specs=[HBM_SPEC] * n_all,
        out_specs=[SEM_SPEC] * (2 * n_g) + [HBM_SPEC] * n_all,
        input_output_aliases={w: 2 * n_g + w for w in range(n_all)},
        compiler_params=SPLIT_COPY,
    )(*[pltpu.with_memory_space_constraint(a, pltpu.HBM) for a in operands])
    sems = [(res[2 * g], res[2 * g + 1]) for g in range(n_g)]
    arrays, k = [], 2 * n_g
    for grp in groups:
        arrays.append(list(res[k:k + len(grp)]))
        k += len(grp)
    return sems, arrays, list(res[k:])


def gather_wait(arrays, sems, after, name):
    n = len(arrays)

    def body(*refs):
        ins, send_sem, recv_sem = refs[:n], refs[n], refs[n + 1]
        x, y, c, _ = _place()
        for w in range(n):
            half = ins[w].at[0, 0]
            cp = pltpu.make_async_remote_copy(src_ref=half, dst_ref=half, send_sem=send_sem, recv_sem=recv_sem,
                                              device_id=(x, y, c), device_id_type=MESH)
            for _ in range(PEERS_PER_BLOCK):
                cp.wait_send()
            for _ in range(PEERS_PER_BLOCK):
                cp.wait_recv()

    return pl.pallas_call(
        body, name=name,
        out_shape=[pltpu.HBM(a.shape, a.dtype) for a in arrays],
        in_specs=[HBM_SPEC] * n + [SEM_SPEC, SEM_SPEC, pl.BlockSpec(memory_space=pl.ANY)],
        out_specs=[HBM_SPEC] * n,
        input_output_aliases={w: w for w in range(n)},
        compiler_params=SPLIT_COPY,
    )(*arrays, *sems, after)


def exchange_halves(gs):
    n = len(gs)

    def body(*refs):
        ins, theirs = refs[:n], refs[n:2 * n]
        send_sems, recv_sems = refs[2 * n:]
        x, y, c, _ = _place()
        sends = [pltpu.make_async_remote_copy(src_ref=ins[w].at[:, 1 - c], dst_ref=theirs[w],
                                              send_sem=send_sems.at[w], recv_sem=recv_sems.at[w],
                                              device_id=(x, y, 1 - c), device_id_type=MESH) for w in range(n)]
        for cp in sends:
            cp.start()
        for cp in sends:
            cp.wait()

    return pl.pallas_call(
        body, name="exchange_halves",
        out_shape=[jax.ShapeDtypeStruct((4,) + g.shape[2:], g.dtype) for g in gs],
        in_specs=[HBM_SPEC] * n, out_specs=[HBM_SPEC] * n,
        scratch_shapes=[pltpu.SemaphoreType.DMA((n,)), pltpu.SemaphoreType.DMA((n,))],
    )(*gs)


def join_halves(qs):
    n = len(qs)

    def body(*refs):
        ins, outs = refs[:n], refs[n:2 * n]
        send_sems, recv_sems = refs[2 * n:]
        x, y, c, _ = _place()
        sends = [pltpu.make_async_remote_copy(src_ref=ins[w].at[c], dst_ref=outs[w].at[c], send_sem=send_sems.at[w],
                                              recv_sem=recv_sems.at[w], device_id=(x, y, 1 - c), device_id_type=MESH)
                 for w in range(n)]
        for cp in sends:
            cp.start()
        for w in range(n):
            other = outs[w].at[1 - c]
            pltpu.make_async_remote_copy(src_ref=other, dst_ref=other, send_sem=send_sems.at[w],
                                         recv_sem=recv_sems.at[w], device_id=(x, y, 1 - c),
                                         device_id_type=MESH).wait_recv()
        for cp in sends:
            cp.wait_send()

    return pl.pallas_call(
        body, name="join_halves",
        out_shape=[jax.ShapeDtypeStruct(q.shape, q.dtype) for q in qs],
        in_specs=[HBM_SPEC] * n, out_specs=[HBM_SPEC] * n,
        input_output_aliases={w: w for w in range(n)},
        scratch_shapes=[pltpu.SemaphoreType.DMA((n,)), pltpu.SemaphoreType.DMA((n,))],
    )(*qs)


def _row_tile(R, row_bytes):
    tm = R
    for t in (512, 256, 128, 64, 32, 16, 8):
        if R % t == 0:
            tm = t
            if t * row_bytes <= ROW_TILE_BUDGET:
                break
    return tm


def sum_siblings(g, theirs, place):
    _, _, R, C = g.shape
    tm = _row_tile(R, 3 * C * 4)

    def body(place_ref, a_ref, b_ref, o_ref):
        o_ref[...] = (a_ref[...] + b_ref[...]).astype(BF16)

    return pl.pallas_call(
        body, name="sum_siblings",
        grid_spec=pltpu.PrefetchScalarGridSpec(
            num_scalar_prefetch=1, grid=(4, R // tm),
            in_specs=[pl.BlockSpec((None, None, tm, C), lambda j, i, s: (j, s[1], i, 0)),
                      pl.BlockSpec((None, tm, C), lambda j, i, s: (j, i, 0))],
            out_specs=pl.BlockSpec((None, tm, C), lambda j, i, s: (j, i, 0))),
        out_shape=jax.ShapeDtypeStruct((4, R, C), BF16),
        compiler_params=_params(("parallel", "parallel")),
    )(place, g, theirs)


def sum_chips(p, landed, place):
    _, R, C = p.shape
    tm = _row_tile(R, 5 * C * 4)

    def body(place_ref, p_ref, l0_ref, l1_ref, l2_ref, o_ref):
        o_ref[...] = ((p_ref[...].astype(F32) + l0_ref[...].astype(F32)) + l1_ref[...].astype(F32)
                      ) + l2_ref[...].astype(F32)

    return pl.pallas_call(
        body, name="sum_chips",
        grid_spec=pltpu.PrefetchScalarGridSpec(
            num_scalar_prefetch=1, grid=(R // tm,),
            in_specs=[pl.BlockSpec((None, tm, C), lambda i, s: (s[0], i, 0))]
            + [pl.BlockSpec((None, tm, C), lambda i, s, j=j: (j, i, 0)) for j in range(3)],
            out_specs=pl.BlockSpec((None, tm, C), lambda i, s: (s[1], i, 0))),
        out_shape=jax.ShapeDtypeStruct((2, R, C), F32),
        compiler_params=_params(("parallel",)),
    )(place, p, landed, landed, landed)


def sum_blocks(items, name):
    R, C = items[0][0].shape[1:]
    tm = _row_tile(R, C * 4 * (len(items) + 1))
    n = len(items)

    def body(*refs):
        acc = refs[0][...].astype(F32)
        for r in refs[1:n]:
            acc = acc + r[...].astype(F32)
        refs[n][...] = acc

    return pl.pallas_call(
        body, name=name,
        grid=(R // tm,),
        in_specs=[pl.BlockSpec((None, tm, C), lambda i, j=j: (j, i, 0)) for _, j in items],
        out_specs=pl.BlockSpec((tm, C), lambda i: (i, 0)),
        out_shape=jax.ShapeDtypeStruct((R, C), F32),
        compiler_params=_params(("parallel",)),
    )(*[a for a, _ in items])


def scatter_start(ps, carried, name):
    n = len(ps)

    def body(*refs):
        ins, lands, sems = refs[:n], refs[n:2 * n], refs[2 * n + 1:2 * n + 3]
        x, y, c, chips = _place()
        for w in range(n):
            for j, chip in enumerate(chips):
                pltpu.make_async_remote_copy(src_ref=ins[w].at[2 * chip[0] + chip[1]], dst_ref=lands[w].at[j],
                                             send_sem=sems[0], recv_sem=sems[1], device_id=(*chip, c),
                                             device_id_type=MESH).start()

    operands = list(ps) + [lax.empty((3,) + p.shape[1:], p.dtype) for p in ps] + [carried]
    res = pl.pallas_call(
        body, name=name,
        out_shape=[pltpu.SemaphoreType.DMA(())] * 2 + [pltpu.HBM(a.shape, a.dtype) for a in operands],
        in_specs=[HBM_SPEC] * (2 * n + 1),
        out_specs=[SEM_SPEC] * 2 + [HBM_SPEC] * (2 * n + 1),
        input_output_aliases={w: 2 + w for w in range(2 * n + 1)},
        compiler_params=SPLIT_COPY,
    )(*[pltpu.with_memory_space_constraint(a, pltpu.HBM) for a in operands])
    return (res[0], res[1]), list(res[2:2 + n]), list(res[2 + n:2 + 2 * n]), res[-1]


def scatter_wait(ps, lands, sems, after, name):
    n = len(ps)

    def body(*refs):
        lands_in, send_sem, recv_sem = refs[n:2 * n], refs[2 * n], refs[2 * n + 1]
        x, y, c, _ = _place()
        for w in range(n):
            blk = lands_in[w].at[0]
            cp = pltpu.make_async_remote_copy(src_ref=blk, dst_ref=blk, send_sem=send_sem, recv_sem=recv_sem,
                                              device_id=(x, y, c), device_id_type=MESH)
            for _ in range(3):
                cp.wait_send()
            for _ in range(3):
                cp.wait_recv()

    operands = list(ps) + list(lands)
    res = pl.pallas_call(
        body, name=name,
        out_shape=[pltpu.HBM(a.shape, a.dtype) for a in operands],
        in_specs=[HBM_SPEC] * (2 * n) + [SEM_SPEC, SEM_SPEC, pl.BlockSpec(memory_space=pl.ANY)],
        out_specs=[HBM_SPEC] * (2 * n),
        input_output_aliases={w: w for w in range(2 * n)},
        compiler_params=SPLIT_COPY,
    )(*operands, *sems, after)
    return list(res[:n]), list(res[n:])


def reduce_start(gs, place, stream, name):
    theirs = exchange_halves(gs)
    sems, ps, lands, stream = scatter_start([sum_siblings(g, t, place) for g, t in zip(gs, theirs)], stream,
                                            "scatter_start_" + name)
    return (sems, ps, lands), stream


def reduce_finish(started, place, after, name):
    sems, ps, lands = started
    ps, lands = scatter_wait(ps, lands, sems, after, "scatter_wait_" + name)
    return [sum_chips(p, l, place) for p, l in zip(ps, lands)]


def adamw(w, g, m, v):
    shape = w.shape
    C = shape[-1]
    R = w.size // C

    def f(w, g, m, v):
        m = ADAM_B1 * m + (1.0 - ADAM_B1) * g
        v = ADAM_B2 * v + (1.0 - ADAM_B2) * (g * g)
        m_hat = m / (1.0 - ADAM_B1 ** ADAM_STEP)
        v_hat = v / (1.0 - ADAM_B2 ** ADAM_STEP)
        delta = -ADAM_LR * (m_hat / (jnp.sqrt(v_hat) + ADAM_EPS) + ADAM_WD * w)
        return delta, m, v

    d, nm, nv = rowwise(f, [a.reshape(R, C) for a in (w, g, m, v)], [], [(C, F32)] * 3, [], "adamw")
    return d.reshape(shape), nm.reshape(shape), nv.reshape(shape)


def _cast_into_slot(w, place):
    C = w.shape[-1]
    w2 = w.reshape(-1, C)
    R = w2.shape[0]
    tm = _row_tile(R, 6 * C)

    def body(place_ref, w_ref, o_ref):
        o_ref[...] = w_ref[...].astype(BF16)

    out = pl.pallas_call(
        body, name="cast_bf16",
        grid_spec=pltpu.PrefetchScalarGridSpec(
            num_scalar_prefetch=1, grid=(R // tm,),
            in_specs=[pl.BlockSpec((tm, C), lambda i, s: (i, 0))],
            out_specs=pl.BlockSpec((None, tm, C), lambda i, s: (s[0], i, 0))),
        out_shape=jax.ShapeDtypeStruct((4, R, C), BF16),
        compiler_params=_params(("parallel",)),
    )(place, w2)
    return out.reshape(4, 2, R // 2, C)


def _pack(vs):
    flat = jnp.concatenate([v.reshape(-1) for v in vs])
    n = flat.shape[0]
    total = -(-n // F32_TILE) * F32_TILE
    return jnp.pad(flat, (0, total - n)).reshape(total // LANES, LANES)


def _unpack(flat, like):
    out, o = [], 0
    for shp in like:
        sz = 1
        for d in shp:
            sz *= d
        out.append(flat[o:o + sz].reshape(shp))
        o += sz
    return out


def _cols_to_blocks(g, n_chips=4):
    R, N = g.shape
    C = N // n_chips
    return g.reshape(R, n_chips, C).transpose(1, 0, 2).reshape(n_chips, 2, R // 2, C)


def _rows_to_blocks(g, n_chips=4):
    R, C = g.shape
    return g.reshape(n_chips, 2, R // n_chips // 2, C)


def kernel(x, c, positions, ada_w, ada_b, norm_g, ffn_w13, ffn_w2, conv_w_pw1, conv_b_pw1, conv_w_dw, conv_b_dw, conv_ln_g, conv_ln_b, conv_w_pw2, conv_b_pw2, kv_ada_w, kv_ada_b, kv_norm_g, w_kv_a, kv_a_norm_g, w_kv_b, w_q_a, q_a_norm_g, w_q_b, w_o, final_norm_g, loss_target, m_ada_w, m_ada_b, m_norm_g, m_ffn_w13, m_ffn_w2, m_conv_w_pw1, m_conv_b_pw1, m_conv_w_dw, m_conv_b_dw, m_conv_ln_g, m_conv_ln_b, m_conv_w_pw2, m_conv_b_pw2, m_kv_ada_w, m_kv_ada_b, m_kv_norm_g, m_w_kv_a, m_kv_a_norm_g, m_w_kv_b, m_w_q_a, m_q_a_norm_g, m_w_q_b, m_w_o, m_final_norm_g, v_ada_w, v_ada_b, v_norm_g, v_ffn_w13, v_ffn_w2, v_conv_w_pw1, v_conv_b_pw1, v_conv_w_dw, v_conv_b_dw, v_conv_ln_g, v_conv_ln_b, v_conv_w_pw2, v_conv_b_pw2, v_kv_ada_w, v_kv_ada_b, v_kv_norm_g, v_w_kv_a, v_kv_a_norm_g, v_w_kv_b, v_w_q_a, v_q_a_norm_g, v_w_q_b, v_w_o, v_final_norm_g):
    S, D = x.shape[1], x.shape[2]
    H = N_HEADS
    F = ffn_w2.shape[2] * 4
    xi, yi, ci = lax.axis_index("x"), lax.axis_index("y"), lax.axis_index("c")
    chip = 2 * xi + yi
    dev = 2 * chip + ci
    place = jnp.stack([chip, ci]).astype(jnp.int32)
    h0 = x[0]
    target = loss_target[0]

    silu_c = rowwise(lambda a: a * _sigmoid(a), [c], [], [(D, F32)], [], "silu_c")[0]
    silu_all = all_gather8(silu_c.reshape(8, D // 8)).reshape(8, D)
    n_ada = ada_w.shape[2]
    n_kv = kv_ada_w.shape[1]
    ada_b_mine = lax.dynamic_slice_in_dim(ada_b, chip * n_ada, n_ada, axis=1)
    kv_b_mine = lax.dynamic_slice_in_dim(kv_ada_b, chip * n_kv, n_kv, axis=0)[None, :]
    mods = [mm(silu_all, ada_w[l], "nn", "ada_rows", bias=ada_b_mine[l:l + 1]) for l in range(2)]
    mods.append(mm(silu_all, kv_ada_w, "nn", "kv_ada_rows", bias=kv_b_mine))
    n_mod_cols = 2 * n_ada + n_kv
    mod_pack = jnp.concatenate(mods, axis=1).reshape(-1, LANES)
    mod_all = all_gather8(mod_pack).reshape(8, 8, n_mod_cols)[0::2]
    mod_mine = lax.dynamic_index_in_dim(mod_all, dev, axis=1, keepdims=False)
    mod = [mod_mine[:, l * n_ada:(l + 1) * n_ada].reshape(N_MOD, D) for l in range(2)]
    kv_mod = mod_mine[:, 2 * n_ada:].reshape(2, D)
    kv_shift, kv_scale = kv_mod[0:1], kv_mod[1:2]

    def mrow(l, k):
        return mod[l][k:k + 1]

    def slot(w):
        return _cast_into_slot(w, place)
    first = gather_weights([slot(ffn_w13[0, 0]), slot(ffn_w2[0, 0])])
    groups = [[slot(conv_w_pw1), slot(conv_w_pw2)],
              [slot(ffn_w13[0, 1]), slot(ffn_w2[0, 1])],
              [slot(w_kv_a), slot(w_kv_b), slot(ffn_w13[1, 0]), slot(ffn_w2[1, 0]), slot(w_q_a), slot(w_q_b), slot(w_o),
               slot(ffn_w13[1, 1]), slot(ffn_w2[1, 1])]]
    sems, started, first = gather_start(groups, first)

    def ffn_weights(w13_blocks, w2_blocks):
        return w13_blocks.reshape(4, 1, 1, D, F // 2), w2_blocks.reshape(F, D)
    small_like = [norm_g.shape, conv_b_pw1.shape, conv_w_dw.shape, conv_b_dw.shape, conv_ln_g.shape,
                  conv_ln_b.shape, conv_b_pw2.shape]
    small_pack = _pack([norm_g, conv_b_pw1, conv_w_dw, conv_b_dw, conv_ln_g, conv_ln_b, conv_b_pw2])
    small_all = all_gather8(small_pack)[0::2].reshape(4, -1)
    per_chip = [_unpack(small_all[j], small_like) for j in range(4)]
    smalls = [jnp.concatenate([per_chip[j][k] for j in range(4)], axis=-1) for k in range(len(small_like))]
    norm_g_f, b_pw1_f, w_dw_f, b_dw_f, ln_g_f, ln_b_f, b_pw2_f = smalls

    tabs = rope_tables(positions[0])

    def ng(l, k):
        return norm_g_f[l, k][None, :]

    h = h0
    ffn00 = ffn_weights(*first)
    h, s_f1_0 = ffn_fwd(h, ng(0, 0), mrow(0, 0), mrow(0, 1), mrow(0, 2), ffn00[0], 0, 0, ffn00[1])
    g_pw1, g_pw2 = gather_wait(started[0], sems[0], h, "gather_wait_conv")
    conv_p = dict(
        w_pw1=g_pw1.reshape(4, D, 2 * D // 4).transpose(1, 0, 2).reshape(D, 2 * D),
        b_pw1=b_pw1_f, w_dw=w_dw_f[0], b_dw=b_dw_f, ln_g=ln_g_f, ln_b=ln_b_f,
        w_pw2=g_pw2.reshape(D, D), b_pw2=b_pw2_f)
    h, s_conv = conv_module_fwd(h, ng(0, 1), mrow(0, 3), mrow(0, 4), mrow(0, 5), conv_p)
    ffn01 = ffn_weights(*gather_wait(started[1], sems[1], h, "gather_wait_ffn"))
    h, s_f2_0 = ffn_fwd(h, ng(0, 2), mrow(0, 6), mrow(0, 7), mrow(0, 8), ffn01[0], 0, 0, ffn01[1])
    (g_kv_a, g_kv_b, g_w13_10, g_w2_10, g_q_a, g_q_b, g_w_o, g_w13_11, g_w2_11) = gather_wait(
        started[2], sems[2], h, "gather_wait_layer1")
    ffn10, ffn11 = ffn_weights(g_w13_10, g_w2_10), ffn_weights(g_w13_11, g_w2_11)
    q_lora = w_q_a.shape[2]
    pw = mla_pad_weights(dict(
        w_kv_a=g_kv_a.reshape(D, KV_LORA + QK_ROPE),
        w_kv_b=g_kv_b.reshape(4, KV_LORA, -1).transpose(1, 0, 2).reshape(KV_LORA, -1),
        w_q_a=g_q_a.reshape(D, q_lora),
        w_q_b=g_q_b.reshape(4, q_lora, -1).transpose(1, 0, 2).reshape(q_lora, -1),
        w_o=g_w_o.reshape(H * V_HEAD, D)))
    kv_norm = kv_norm_g[None, :]
    kv_a_g = kv_a_norm_g[None, :]
    kv, kpe_r, s_kv = mla_kv_fwd(h, kv_norm, kv_shift, kv_scale, kv_a_g, pw, tabs)
    h, s_f1_1 = ffn_fwd(h, ng(1, 0), mrow(1, 0), mrow(1, 1), mrow(1, 2), ffn10[0], 0, 0, ffn10[1])
    h, s_mla = mla_fwd(h, ng(1, 1), mrow(1, 3), mrow(1, 4), mrow(1, 5), q_a_norm_g, pw, kv, kpe_r, tabs)
    h, s_f2_1 = ffn_fwd(h, ng(1, 2), mrow(1, 6), mrow(1, 7), mrow(1, 8), ffn11[0], 0, 0, ffn11[1])
    dh, loss_cols, d_final_g = loss_head(h, target, final_norm_g[None, :])

    def w13_blocks(dw):
        return dw.reshape(4, 2, D // 2, F // 2)

    dh, v_f2_1, dw13_11, dw2_11 = ffn_bwd(dh, s_f2_1, ng(1, 2), mrow(1, 7), mrow(1, 8), ffn11[0], 0, 0, ffn11[1])
    red_a, dh = reduce_start([w13_blocks(dw13_11), _rows_to_blocks(dw2_11)], place, dh, "a")
    dh, v_mla, g_mla, dk, dv = mla_bwd(dh, s_mla, ng(1, 1), mrow(1, 4), mrow(1, 5), q_a_norm_g, pw, kv, kpe_r, tabs)
    dh, v_f1_1, dw13_10, dw2_10 = ffn_bwd(dh, s_f1_1, ng(1, 0), mrow(1, 1), mrow(1, 2), ffn10[0], 0, 0, ffn10[1])
    dh, v_kv, d_kv_a_g, d_w_kv_a, d_w_kv_b = mla_kv_bwd(dh, dk, dv, s_kv, kv_norm, kv_scale, kv_a_g, pw, tabs)
    d_w_kv_a_u = jnp.concatenate([d_w_kv_a[:, :KV_LORA], d_w_kv_a[:, KV_LORA + QK_NOPE:KV_LORA + QK_NOPE + QK_ROPE]],
                                 axis=1)
    hk = H * HEAD_PAD
    dkb = jnp.concatenate([d_w_kv_b[:, :hk].reshape(KV_LORA, H, HEAD_PAD)[:, :, :QK_NOPE],
                           d_w_kv_b[:, hk:].reshape(KV_LORA, H, HEAD_PAD)[:, :, :V_HEAD]], axis=2).reshape(KV_LORA, -1)
    d_w_q_b_u = _unpad_heads(g_mla["w_q_b"], QK_NOPE + QK_ROPE)
    d_w_o_u = g_mla["w_o"].reshape(H, HEAD_PAD, D)[:, :V_HEAD].reshape(H * V_HEAD, D)
    q_w13_11, q_w2_11 = reduce_finish(red_a, place, dh, "a")
    red_b, dh = reduce_start([w13_blocks(dw13_10), _rows_to_blocks(dw2_10), _rows_to_blocks(d_w_kv_a_u),
                              _cols_to_blocks(dkb), _rows_to_blocks(g_mla["w_q_a"]), _cols_to_blocks(d_w_q_b_u),
                              _rows_to_blocks(d_w_o_u)], place, dh, "b")
    dh, v_f2_0, dw13_01, dw2_01 = ffn_bwd(dh, s_f2_0, ng(0, 2), mrow(0, 7), mrow(0, 8), ffn01[0], 0, 0, ffn01[1])
    dh, v_conv, g_conv = conv_module_bwd(dh, s_conv, ng(0, 1), mrow(0, 4), mrow(0, 5), conv_p)
    q_w13_10, q_w2_10, q_kv_a, q_kv_b, q_q_a, q_q_b, q_w_o = reduce_finish(red_b, place, dh, "b")
    red_c, dh = reduce_start([w13_blocks(dw13_01), _rows_to_blocks(dw2_01), _cols_to_blocks(g_conv["w_pw1"]),
                              _rows_to_blocks(g_conv["w_pw2"])], place, dh, "c")
    last_group = {}

    def start_last_group(dw13, dw2, dab):
        last_group["started"], dab = reduce_start([w13_blocks(dw13), _rows_to_blocks(dw2)], place, dab, "d")
        return dab

    dh, v_f1_0, _, _ = ffn_bwd(dh, s_f1_0, ng(0, 0), mrow(0, 1), mrow(0, 2), ffn00[0], 0, 0, ffn00[1],
                               after_weight_grads=start_last_group)
    grad_x = dh[None]
    q_w13_01, q_w2_01, q_pw1, q_pw2 = reduce_finish(red_c, place, dh, "c")
    def dmod(v1, vm, v2):
        return jnp.concatenate([v1[0], v1[1], v1[2], vm[0], vm[1], vm[2], v2[0], v2[1], v2[2]], axis=1)
    d_mod0 = dmod(v_f1_0, v_conv, v_f2_0)
    d_mod1 = dmod(v_f1_1, v_mla, v_f2_1)
    d_kv_mod = jnp.concatenate([v_kv[0], v_kv[1]], axis=1)
    d_norm_g = jnp.concatenate([v_f1_0[3], v_conv[3], v_f2_0[3], v_f1_1[3], v_mla[3], v_f2_1[3]], axis=0)
    vec_list = [d_mod0, d_mod1, d_kv_mod, d_norm_g, g_conv["b_pw1"], g_conv["w_dw"], g_conv["b_dw"], g_conv["ln_g"],
                g_conv["ln_b"], g_conv["b_pw2"], v_kv[2], d_kv_a_g, g_mla["q_a_norm_g"], d_final_g, loss_cols]
    vec_like = [v.shape for v in vec_list]
    vec_pack = _pack(vec_list)
    n_mod_rows = (2 * N_MOD * D + 2 * D) // LANES
    vec_all = all_gather8(vec_pack)
    vec_sum = sum_blocks([(vec_all, d) for d in range(8)], "sum_devices").reshape(-1)
    (_, _, _, s_norm_g, s_b_pw1, s_w_dw, s_b_dw, s_ln_g, s_ln_b, s_b_pw2, s_kv_norm_g, s_kv_a_g, s_q_a_g,
     s_final_g, s_loss) = _unpack(vec_sum, vec_like)
    loss = jnp.sum(s_loss)
    dmod_all = vec_all[:, :n_mod_rows].reshape(8, 2 * N_MOD * D + 2 * D)
    dmod_sum = vec_sum[:2 * N_MOD * D + 2 * D]
    g_ada_b = dmod_sum[:2 * N_MOD * D].reshape(2, N_MOD * D)
    g_kv_ada_b = dmod_sum[2 * N_MOD * D:]
    g_ada_w = []
    for l in range(2):
        cols = lax.dynamic_slice_in_dim(dmod_all[:, l * N_MOD * D:(l + 1) * N_MOD * D], chip * n_ada, n_ada, axis=1)
        g_ada_w.append(mm(silu_all, cols, "tn", "ada_w_grad"))
    g_ada_w = jnp.stack(g_ada_w)
    kv_cols = lax.dynamic_slice_in_dim(dmod_all[:, 2 * N_MOD * D:], chip * n_kv, n_kv, axis=1)
    g_kv_ada_w = mm(silu_all, kv_cols, "tn", "kv_ada_w_grad")

    def shard(v, width):
        return lax.dynamic_slice_in_dim(v, chip * width, width, axis=v.ndim - 1)

    Dq = D // 4
    g_norm_g = shard(s_norm_g.reshape(2, 3, D), Dq)
    g_conv_b_pw1 = shard(s_b_pw1, 2 * D // 4)
    g_conv_w_dw = shard(s_w_dw, Dq)[None]
    g_conv_b_dw = shard(s_b_dw, Dq)
    g_conv_ln_g = shard(s_ln_g, Dq)
    g_conv_ln_b = shard(s_ln_b, Dq)
    g_conv_b_pw2 = shard(s_b_pw2, Dq)

    small = dict(ada_w=g_ada_w, ada_b=g_ada_b, norm_g=g_norm_g, conv_b_pw1=g_conv_b_pw1, conv_w_dw=g_conv_w_dw,
                 conv_b_dw=g_conv_b_dw, conv_ln_g=g_conv_ln_g, conv_ln_b=g_conv_ln_b, conv_b_pw2=g_conv_b_pw2,
                 kv_ada_w=g_kv_ada_w, kv_ada_b=g_kv_ada_b, kv_norm_g=s_kv_norm_g, kv_a_norm_g=s_kv_a_g,
                 q_a_norm_g=s_q_a_g, final_norm_g=s_final_g)
    order = ["ada_w", "ada_b", "norm_g", "ffn_w13", "ffn_w2", "conv_w_pw1", "conv_b_pw1", "conv_w_dw", "conv_b_dw",
             "conv_ln_g", "conv_ln_b", "conv_w_pw2", "conv_b_pw2", "kv_ada_w", "kv_ada_b", "kv_norm_g", "w_kv_a",
             "kv_a_norm_g", "w_kv_b", "w_q_a", "q_a_norm_g", "w_q_b", "w_o", "final_norm_g"]
    weights = [ada_w, ada_b, norm_g, ffn_w13, ffn_w2, conv_w_pw1, conv_b_pw1, conv_w_dw, conv_b_dw, conv_ln_g,
               conv_ln_b, conv_w_pw2, conv_b_pw2, kv_ada_w, kv_ada_b, kv_norm_g, w_kv_a, kv_a_norm_g, w_kv_b, w_q_a,
               q_a_norm_g, w_q_b, w_o, final_norm_g]
    ms = [m_ada_w, m_ada_b, m_norm_g, m_ffn_w13, m_ffn_w2, m_conv_w_pw1, m_conv_b_pw1, m_conv_w_dw, m_conv_b_dw,
          m_conv_ln_g, m_conv_ln_b, m_conv_w_pw2, m_conv_b_pw2, m_kv_ada_w, m_kv_ada_b, m_kv_norm_g, m_w_kv_a,
          m_kv_a_norm_g, m_w_kv_b, m_w_q_a, m_q_a_norm_g, m_w_q_b, m_w_o, m_final_norm_g]
    vs = [v_ada_w, v_ada_b, v_norm_g, v_ffn_w13, v_ffn_w2, v_conv_w_pw1, v_conv_b_pw1, v_conv_w_dw, v_conv_b_dw,
          v_conv_ln_g, v_conv_ln_b, v_conv_w_pw2, v_conv_b_pw2, v_kv_ada_w, v_kv_ada_b, v_kv_norm_g, v_w_kv_a,
          v_kv_a_norm_g, v_w_kv_b, v_w_q_a, v_q_a_norm_g, v_w_q_b, v_w_o, v_final_norm_g]
    state = {k: (w, m, v) for k, w, m, v in zip(order, weights, ms, vs)}
    results = {}

    def update(names, grads_by_name):
        for k in names:
            w, m, v = state[k]
            g = grads_by_name[k].reshape(w.shape)
            results[k] = (g, *adamw(w, g, m, v))

    update([k for k in order if k in small], small)
    q_w13_00, q_w2_00 = reduce_finish(last_group["started"], place, dh, "d")
    red = [j.reshape(2 * j.shape[1], j.shape[2]) for j in join_halves(
        [q_w13_00, q_w13_01, q_w13_10, q_w13_11, q_w2_00, q_w2_01, q_w2_10, q_w2_11, q_pw1, q_pw2, q_kv_a, q_kv_b,
         q_q_a, q_q_b, q_w_o])]
    big = dict(ffn_w13=jnp.stack(red[0:4]), ffn_w2=jnp.stack(red[4:8]), conv_w_pw1=red[8], conv_w_pw2=red[9],
               w_kv_a=red[10], w_kv_b=red[11], w_q_a=red[12], w_q_b=red[13], w_o=red[14])
    update([k for k in order if k in big], big)
    outs = [results[k] for k in order]
    return (loss, grad_x, *[o[0] for o in outs], *[o[1] for o in outs], *[o[2] for o in outs], *[o[3] for o in outs])
```
